```python
import jax, jax.numpy as jnp
from jax import lax
import numpy as np

D_MODEL = 2048
BATCH = 8
SEQ = 4096
DEPTH = 1

GRID_W = 64
CTX_LEN = 256
HEAD_DIM = 128
N_Q_HEADS = 16
N_KV_HEADS = 4
GQA_GROUP = N_Q_HEADS // N_KV_HEADS
ATTN_W = N_Q_HEADS * HEAD_DIM
KV_W = N_KV_HEADS * HEAD_DIM
LRU_W = D_MODEL
LRU_BLOCKS = 16
LRU_BLOCK_DIM = LRU_W // LRU_BLOCKS
LRU_C = 8.0
CONV_W = 4
CONV_LEFT = 2
D_FF = 5632
Q_BLOCK = 128
ROPE_THETA = 10000.0
EPS = 1e-6
N_MOD = 9
FFN_RES = 0.5
OFF_Q = 0
OFF_K = OFF_Q + ATTN_W
OFF_V = OFF_K + KV_W
OFF_LX = OFF_V + KV_W
OFF_LG = OFF_LX + LRU_W
OFF_GA = OFF_LG + LRU_W
OFF_GL = OFF_GA + D_MODEL
IN_W = OFF_GL + D_MODEL

kernel_name = 'hybrid_gqa_rglru_macaron_dit_layer'


def rms_norm(t, g):
    tf = t.astype(jnp.float32)
    y = tf * lax.rsqrt(jnp.mean(tf * tf, axis=-1, keepdims=True) + EPS)
    return (y * g.astype(jnp.float32)).astype(t.dtype)


def modulate(h, shift, scale):
    return h * (1.0 + scale) + shift


def swiglu(h, wg, wu, wd):
    return (jax.nn.silu(h @ wg) * (h @ wu)) @ wd


def axial_rope_tables(n_tok):
    rows = n_tok // GRID_W
    row = jnp.repeat(jnp.arange(rows, dtype=jnp.float32), GRID_W)
    col = jnp.tile(jnp.arange(GRID_W, dtype=jnp.float32), rows)
    axis_dims = HEAD_DIM // 2
    freqs = ROPE_THETA ** (-jnp.arange(0, axis_dims, 2, dtype=jnp.float32) / axis_dims)
    ang = jnp.concatenate([row[:, None] * freqs, col[:, None] * freqs], axis=-1)
    return jnp.cos(ang), jnp.sin(ang)


def apply_rope(t, cos, sin):
    tf = t.astype(jnp.float32).reshape(t.shape[:-1] + (HEAD_DIM // 2, 2))
    t1, t2 = tf[..., 0], tf[..., 1]
    out = jnp.stack([t1 * cos - t2 * sin, t1 * sin + t2 * cos], axis=-1)
    return out.reshape(t.shape).astype(t.dtype)


def to_heads(t, n_heads):
    b, n, _ = t.shape
    return t.reshape(b, n, n_heads, HEAD_DIM).transpose(0, 2, 1, 3)


def group_queries(q):
    b, _, n, _ = q.shape
    return q.reshape(b, N_KV_HEADS, GQA_GROUP, n, HEAD_DIM)


def latent_attention(q, k_lat, v_lat, k_ctx, v_ctx):
    b, _, _, n, _ = q.shape
    k_all = jnp.concatenate([k_ctx, k_lat], axis=2)
    v_all = jnp.concatenate([v_ctx, v_lat], axis=2)
    n_blk = n // Q_BLOCK
    qb = jnp.moveaxis(q.reshape(b, N_KV_HEADS, GQA_GROUP, n_blk, Q_BLOCK, HEAD_DIM), 3, 0)
    scale = HEAD_DIM ** -0.5

    def one_block(q_blk):
        s = jnp.einsum('bkgqd,bksd->bkgqs', q_blk, k_all, preferred_element_type=jnp.float32) * scale
        p = jax.nn.softmax(s, axis=-1)
        return jnp.einsum('bkgqs,bksd->bkgqd', p.astype(v_all.dtype), v_all)

    ob = lax.map(one_block, qb)
    return ob.transpose(1, 0, 4, 2, 3, 5).reshape(b, n, ATTN_W)


def context_attention(q, k, v):
    b, _, _, n, _ = q.shape
    s = jnp.einsum('bkgqd,bksd->bkgqs', q, k, preferred_element_type=jnp.float32) * (HEAD_DIM ** -0.5)
    p = jax.nn.softmax(s, axis=-1)
    o = jnp.einsum('bkgqs,bksd->bkgqd', p.astype(v.dtype), v)
    return o.transpose(0, 3, 1, 2, 4).reshape(b, n, ATTN_W)


def centred_dwconv(t, w, b):
    n = t.shape[1]
    tp = jnp.pad(t, ((0, 0), (CONV_LEFT, CONV_W - 1 - CONV_LEFT), (0, 0)))
    out = b
    for k in range(CONV_W):
        out = out + tp[:, k:k + n] * w[k]
    return out


def block_diag(t, w, b):
    tb = t.reshape(t.shape[:-1] + (LRU_BLOCKS, LRU_BLOCK_DIM))
    return jnp.einsum('btnd,nde->btne', tb, w).reshape(t.shape) + b


def rglru_coeffs(xc, w_a, b_a, w_x, b_x, lam):
    xf = xc.astype(jnp.float32)
    r = jax.nn.sigmoid(block_diag(xf, w_a, b_a).astype(jnp.float32))
    i = jax.nn.sigmoid(block_diag(xf, w_x, b_x).astype(jnp.float32))
    log_a = -LRU_C * r * jax.nn.softplus(-lam.astype(jnp.float32))
    a = jnp.exp(log_a)
    u = jnp.sqrt(-jnp.expm1(2.0 * log_a)) * (i * xf)
    return a, u


def linear_scan(a, u, h0, reverse):
    def combine(e1, e2):
        a1, b1 = e1
        a2, b2 = e2
        return a1 * a2, a2 * b1 + b2
    a_cum, b_cum = lax.associative_scan(combine, (a, u), axis=1, reverse=reverse)
    return a_cum * h0[:, None, :] + b_cum


def rglru_bidir(xc, h0_f, h0_b, wa, ba, wx, bx, lam):
    a_f, u_f = rglru_coeffs(xc, wa[0], ba[0], wx[0], bx[0], lam[0])
    a_b, u_b = rglru_coeffs(xc, wa[1], ba[1], wx[1], bx[1], lam[1])
    return linear_scan(a_f, u_f, h0_f, False), linear_scan(a_b, u_b, h0_b, True)


def gated_lru_out(h_f, h_b, gate, dtype):
    return ((h_f + h_b) * jax.nn.gelu(gate.astype(jnp.float32))).astype(dtype)


def merge_branches(attn, lru, ga, gl, w_out):
    return (jax.nn.sigmoid(ga) * attn + jax.nn.sigmoid(gl) * lru) @ w_out


def hybrid_layer(x, ctx, c, c_ctx, w_mod, b_mod, norm_g, ffn_wg, ffn_wu, ffn_wd, w_in, w_out,
                 q_norm_g, k_norm_g, conv_w, conv_b, lru_wa, lru_ba, lru_wx, lru_bx, lru_lambda,
                 cos, sin, update_ctx):
    b = x.shape[0]
    mod_x = (jax.nn.silu(c) @ w_mod + b_mod).reshape(b, N_MOD, 1, D_MODEL)
    mod_c = (jax.nn.silu(c_ctx) @ w_mod + b_mod).reshape(N_MOD, D_MODEL)
    sh1, sc1, g1, sh2, sc2, g2, sh3, sc3, g3 = [mod_x[:, i] for i in range(N_MOD)]
    csh1, csc1, cg1, csh2, csc2, cg2, csh3, csc3, cg3 = [mod_c[i] for i in range(N_MOD)]

    x = x + FFN_RES * g1 * swiglu(modulate(rms_norm(x, norm_g[0]), sh1, sc1), ffn_wg[0], ffn_wu[0], ffn_wd[0])
    ctx = ctx + FFN_RES * cg1 * swiglu(modulate(rms_norm(ctx, norm_g[0]), csh1, csc1), ffn_wg[0], ffn_wu[0], ffn_wd[0])

    hx = modulate(rms_norm(x, norm_g[1]), sh2, sc2)
    hc = modulate(rms_norm(ctx, norm_g[1]), csh2, csc2)

    pc = hc @ w_in[:, OFF_K:OFF_LG]
    k_c = rms_norm(to_heads(pc[..., :KV_W], N_KV_HEADS), k_norm_g)
    v_c = to_heads(pc[..., KV_W:2 * KV_W], N_KV_HEADS)
    xc_c = centred_dwconv(pc[..., 2 * KV_W:], conv_w, conv_b)
    zeros = jnp.zeros((b, LRU_W), jnp.float32)
    hf_c, hb_c = rglru_bidir(xc_c, zeros, zeros, lru_wa, lru_ba, lru_wx, lru_bx, lru_lambda)

    p = hx @ w_in
    q = apply_rope(rms_norm(to_heads(p[..., OFF_Q:OFF_K], N_Q_HEADS), q_norm_g), cos, sin)
    k = apply_rope(rms_norm(to_heads(p[..., OFF_K:OFF_V], N_KV_HEADS), k_norm_g), cos, sin)
    v = to_heads(p[..., OFF_V:OFF_LX], N_KV_HEADS)
    attn = latent_attention(group_queries(q), k, v, k_c, v_c)
    xc = centred_dwconv(p[..., OFF_LX:OFF_LG], conv_w, conv_b)
    hf, hb = rglru_bidir(xc, hf_c[:, -1], hb_c[:, 0], lru_wa, lru_ba, lru_wx, lru_bx, lru_lambda)
    lru = gated_lru_out(hf, hb, p[..., OFF_LG:OFF_GA], x.dtype)
    x = x + g2 * merge_branches(attn, lru, p[..., OFF_GA:OFF_GL], p[..., OFF_GL:], w_out)

    if update_ctx:
        pq_c = hc @ w_in[:, OFF_Q:OFF_K]
        pg_c = hc @ w_in[:, OFF_LG:]
        q_c = rms_norm(to_heads(pq_c, N_Q_HEADS), q_norm_g)
        attn_c = context_attention(group_queries(q_c), k_c, v_c)
        lru_c = gated_lru_out(hf_c, hb_c, pg_c[..., :LRU_W], ctx.dtype)
        ctx = ctx + cg2 * merge_branches(attn_c, lru_c, pg_c[..., LRU_W:LRU_W + D_MODEL],
                                         pg_c[..., LRU_W + D_MODEL:], w_out)

    x = x + FFN_RES * g3 * swiglu(modulate(rms_norm(x, norm_g[2]), sh3, sc3), ffn_wg[1], ffn_wu[1], ffn_wd[1])
    if update_ctx:
        ctx = ctx + FFN_RES * cg3 * swiglu(modulate(rms_norm(ctx, norm_g[2]), csh3, csc3), ffn_wg[1], ffn_wu[1], ffn_wd[1])
    return x, ctx


def _fwd_setup_inputs(seed: int = 0) -> dict:
    key = jax.random.key(seed)
    ks = jax.random.split(key, 24)
    f32 = jnp.float32

    def nrm(k, shape, scale):
        return jax.random.normal(k, shape, f32) * scale

    u = jax.random.uniform(ks[20], (DEPTH, 2, LRU_W), f32, 0.9, 0.999)
    base = u ** (1.0 / LRU_C)
    return {
        'x': nrm(ks[0], (BATCH, SEQ, D_MODEL), 1.0),
        'c': nrm(ks[1], (BATCH, D_MODEL), 1.0),
        'ctx': nrm(ks[2], (BATCH, CTX_LEN, D_MODEL), 1.0),
        'c_ctx': nrm(ks[3], (D_MODEL,), 1.0),
        'w_mod': nrm(ks[4], (DEPTH, D_MODEL, N_MOD * D_MODEL), 0.5 * D_MODEL ** -0.5),
        'b_mod': nrm(ks[5], (DEPTH, N_MOD * D_MODEL), 0.01),
        'norm_g': 1.0 + nrm(ks[6], (DEPTH, 3, D_MODEL), 0.02),
        'ffn_wg': nrm(ks[7], (DEPTH, 2, D_MODEL, D_FF), D_MODEL ** -0.5),
        'ffn_wu': nrm(ks[8], (DEPTH, 2, D_MODEL, D_FF), D_MODEL ** -0.5),
        'ffn_wd': nrm(ks[9], (DEPTH, 2, D_FF, D_MODEL), D_FF ** -0.5),
        'w_in': nrm(ks[10], (DEPTH, D_MODEL, IN_W), D_MODEL ** -0.5),
        'w_out': nrm(ks[11], (DEPTH, D_MODEL, D_MODEL), D_MODEL ** -0.5),
        'q_norm_g': 1.0 + nrm(ks[12], (DEPTH, HEAD_DIM), 0.02),
        'k_norm_g': 1.0 + nrm(ks[13], (DEPTH, HEAD_DIM), 0.02),
        'conv_w': nrm(ks[14], (DEPTH, CONV_W, LRU_W), CONV_W ** -0.5),
        'conv_b': nrm(ks[15], (DEPTH, LRU_W), 0.01),
        'lru_wa': nrm(ks[16], (DEPTH, 2, LRU_BLOCKS, LRU_BLOCK_DIM, LRU_BLOCK_DIM), LRU_BLOCK_DIM ** -0.5),
        'lru_ba': nrm(ks[17], (DEPTH, 2, LRU_W), 0.01),
        'lru_wx': nrm(ks[18], (DEPTH, 2, LRU_BLOCKS, LRU_BLOCK_DIM, LRU_BLOCK_DIM), LRU_BLOCK_DIM ** -0.5),
        'lru_bx': nrm(ks[19], (DEPTH, 2, LRU_W), 0.01),
        'lru_lambda': jnp.log(base) - jnp.log1p(-base),
        'final_norm_g': 1.0 + nrm(ks[21], (D_MODEL,), 0.02),
    }


def _fwd_reference(x, c, ctx, c_ctx, w_mod, b_mod, norm_g, ffn_wg, ffn_wu, ffn_wd, w_in, w_out,
              q_norm_g, k_norm_g, conv_w, conv_b, lru_wa, lru_ba, lru_wx, lru_bx, lru_lambda,
              final_norm_g):
    cos, sin = axial_rope_tables(x.shape[1])
    for l in range(DEPTH):
        x, ctx = hybrid_layer(x, ctx, c, c_ctx, w_mod[l], b_mod[l], norm_g[l], ffn_wg[l], ffn_wu[l], ffn_wd[l],
                              w_in[l], w_out[l], q_norm_g[l], k_norm_g[l], conv_w[l], conv_b[l],
                              lru_wa[l], lru_ba[l], lru_wx[l], lru_bx[l], lru_lambda[l],
                              cos, sin, l < DEPTH - 1)
    return rms_norm(x, final_norm_g)


import jax as _jax
import jax.numpy as _jnp

TWIN_FORMAT = 'train_step'
FWD_PARAMS = ['x', 'c', 'ctx', 'c_ctx', 'w_mod', 'b_mod', 'norm_g', 'ffn_wg', 'ffn_wu', 'ffn_wd', 'w_in', 'w_out', 'q_norm_g', 'k_norm_g', 'conv_w', 'conv_b', 'lru_wa', 'lru_ba', 'lru_wx', 'lru_bx', 'lru_lambda', 'final_norm_g']
TWIN_WEIGHTS = ['c_ctx', 'w_mod', 'b_mod', 'norm_g', 'ffn_wg', 'ffn_wu', 'ffn_wd', 'w_in', 'w_out', 'q_norm_g', 'k_norm_g', 'conv_w', 'conv_b', 'lru_wa', 'lru_ba', 'lru_wx', 'lru_bx', 'lru_lambda', 'final_norm_g']
TWIN_DIFF_INPUT = 'x'
TWIN_INPUTS = ['x', 'c', 'ctx', 'c_ctx', 'w_mod', 'b_mod', 'norm_g', 'ffn_wg', 'ffn_wu', 'ffn_wd', 'w_in', 'w_out', 'q_norm_g', 'k_norm_g', 'conv_w', 'conv_b', 'lru_wa', 'lru_ba', 'lru_wx', 'lru_bx', 'lru_lambda', 'final_norm_g', 'loss_target', 'm_c_ctx', 'm_w_mod', 'm_b_mod', 'm_norm_g', 'm_ffn_wg', 'm_ffn_wu', 'm_ffn_wd', 'm_w_in', 'm_w_out', 'm_q_norm_g', 'm_k_norm_g', 'm_conv_w', 'm_conv_b', 'm_lru_wa', 'm_lru_ba', 'm_lru_wx', 'm_lru_bx', 'm_lru_lambda', 'm_final_norm_g', 'v_c_ctx', 'v_w_mod', 'v_b_mod', 'v_norm_g', 'v_ffn_wg', 'v_ffn_wu', 'v_ffn_wd', 'v_w_in', 'v_w_out', 'v_q_norm_g', 'v_k_norm_g', 'v_conv_w', 'v_conv_b', 'v_lru_wa', 'v_lru_ba', 'v_lru_wx', 'v_lru_bx', 'v_lru_lambda', 'v_final_norm_g']
TWIN_OUTPUTS = ['loss', 'grad_x', 'grad_c_ctx', 'grad_w_mod', 'grad_b_mod', 'grad_norm_g', 'grad_ffn_wg', 'grad_ffn_wu', 'grad_ffn_wd', 'grad_w_in', 'grad_w_out', 'grad_q_norm_g', 'grad_k_norm_g', 'grad_conv_w', 'grad_conv_b', 'grad_lru_wa', 'grad_lru_ba', 'grad_lru_wx', 'grad_lru_bx', 'grad_lru_lambda', 'grad_final_norm_g', 'delta_c_ctx', 'delta_w_mod', 'delta_b_mod', 'delta_norm_g', 'delta_ffn_wg', 'delta_ffn_wu', 'delta_ffn_wd', 'delta_w_in', 'delta_w_out', 'delta_q_norm_g', 'delta_k_norm_g', 'delta_conv_w', 'delta_conv_b', 'delta_lru_wa', 'delta_lru_ba', 'delta_lru_wx', 'delta_lru_bx', 'delta_lru_lambda', 'delta_final_norm_g', 'new_m_c_ctx', 'new_m_w_mod', 'new_m_b_mod', 'new_m_norm_g', 'new_m_ffn_wg', 'new_m_ffn_wu', 'new_m_ffn_wd', 'new_m_w_in', 'new_m_w_out', 'new_m_q_norm_g', 'new_m_k_norm_g', 'new_m_conv_w', 'new_m_conv_b', 'new_m_lru_wa', 'new_m_lru_ba', 'new_m_lru_wx', 'new_m_lru_bx', 'new_m_lru_lambda', 'new_m_final_norm_g', 'new_v_c_ctx', 'new_v_w_mod', 'new_v_b_mod', 'new_v_norm_g', 'new_v_ffn_wg', 'new_v_ffn_wu', 'new_v_ffn_wd', 'new_v_w_in', 'new_v_w_out', 'new_v_q_norm_g', 'new_v_k_norm_g', 'new_v_conv_w', 'new_v_conv_b', 'new_v_lru_wa', 'new_v_lru_ba', 'new_v_lru_wx', 'new_v_lru_bx', 'new_v_lru_lambda', 'new_v_final_norm_g']
TWIN_LEAF_KINDS = {'loss': 'loss', 'grad_x': 'grad_x', 'grad_c_ctx': 'grad_w', 'grad_w_mod': 'grad_w', 'grad_b_mod': 'grad_w', 'grad_norm_g': 'grad_w', 'grad_ffn_wg': 'grad_w', 'grad_ffn_wu': 'grad_w', 'grad_ffn_wd': 'grad_w', 'grad_w_in': 'grad_w', 'grad_w_out': 'grad_w', 'grad_q_norm_g': 'grad_w', 'grad_k_norm_g': 'grad_w', 'grad_conv_w': 'grad_w', 'grad_conv_b': 'grad_w', 'grad_lru_wa': 'grad_w', 'grad_lru_ba': 'grad_w', 'grad_lru_wx': 'grad_w', 'grad_lru_bx': 'grad_w', 'grad_lru_lambda': 'grad_w', 'grad_final_norm_g': 'grad_w', 'delta_c_ctx': 'delta_w', 'delta_w_mod': 'delta_w', 'delta_b_mod': 'delta_w', 'delta_norm_g': 'delta_w', 'delta_ffn_wg': 'delta_w', 'delta_ffn_wu': 'delta_w', 'delta_ffn_wd': 'delta_w', 'delta_w_in': 'delta_w', 'delta_w_out': 'delta_w', 'delta_q_norm_g': 'delta_w', 'delta_k_norm_g': 'delta_w', 'delta_conv_w': 'delta_w', 'delta_conv_b': 'delta_w', 'delta_lru_wa': 'delta_w', 'delta_lru_ba': 'delta_w', 'delta_lru_wx': 'delta_w', 'delta_lru_bx': 'delta_w', 'delta_lru_lambda': 'delta_w', 'delta_final_norm_g': 'delta_w', 'new_m_c_ctx': 'new_m', 'new_m_w_mod': 'new_m', 'new_m_b_mod': 'new_m', 'new_m_norm_g': 'new_m', 'new_m_ffn_wg': 'new_m', 'new_m_ffn_wu': 'new_m', 'new_m_ffn_wd': 'new_m', 'new_m_w_in': 'new_m', 'new_m_w_out': 'new_m', 'new_m_q_norm_g': 'new_m', 'new_m_k_norm_g': 'new_m', 'new_m_conv_w': 'new_m', 'new_m_conv_b': 'new_m', 'new_m_lru_wa': 'new_m', 'new_m_lru_ba': 'new_m', 'new_m_lru_wx': 'new_m', 'new_m_lru_bx': 'new_m', 'new_m_lru_lambda': 'new_m', 'new_m_final_norm_g': 'new_m', 'new_v_c_ctx': 'new_v', 'new_v_w_mod': 'new_v', 'new_v_b_mod': 'new_v', 'new_v_norm_g': 'new_v', 'new_v_ffn_wg': 'new_v', 'new_v_ffn_wu': 'new_v', 'new_v_ffn_wd': 'new_v', 'new_v_w_in': 'new_v', 'new_v_w_out': 'new_v', 'new_v_q_norm_g': 'new_v', 'new_v_k_norm_g': 'new_v', 'new_v_conv_w': 'new_v', 'new_v_conv_b': 'new_v', 'new_v_lru_wa': 'new_v', 'new_v_lru_ba': 'new_v', 'new_v_lru_wx': 'new_v', 'new_v_lru_bx': 'new_v', 'new_v_lru_lambda': 'new_v', 'new_v_final_norm_g': 'new_v'}


def _forward(args):
    return _fwd_reference(*[args[k] for k in FWD_PARAMS])


def _output_shape():
    def fwd():
        inp = _fwd_setup_inputs(0)
        return _fwd_reference(*[inp[k] for k in FWD_PARAMS])
    out = _jax.eval_shape(fwd)
    return out.shape, out.dtype

N_MICROBATCH = 1
ADAM_LR = 0.001
ADAM_B1 = 0.9
ADAM_B2 = 0.999
ADAM_EPS = 1e-08
ADAM_WD = 0.01
ADAM_STEP = 10
PER_EXAMPLE_BATCH_AXIS = {'x': 0, 'c': 0, 'ctx': 0, 'loss_target': 0}
SHARED_INPUTS = []
_WEIGHT_DTYPES = {'c_ctx': _jnp.float32, 'w_mod': _jnp.float32, 'b_mod': _jnp.float32, 'norm_g': _jnp.float32, 'ffn_wg': _jnp.float32, 'ffn_wu': _jnp.float32, 'ffn_wd': _jnp.float32, 'w_in': _jnp.float32, 'w_out': _jnp.float32, 'q_norm_g': _jnp.float32, 'k_norm_g': _jnp.float32, 'conv_w': _jnp.float32, 'conv_b': _jnp.float32, 'lru_wa': _jnp.float32, 'lru_ba': _jnp.float32, 'lru_wx': _jnp.float32, 'lru_bx': _jnp.float32, 'lru_lambda': _jnp.float32, 'final_norm_g': _jnp.float32}
MOMENT_SCALE = {'c_ctx': 1.059136e-02, 'w_mod': 4.169110e-02, 'b_mod': 7.173636e-02, 'norm_g': 2.430433e-02, 'ffn_wg': 5.768822e-03, 'ffn_wu': 5.596310e-03, 'ffn_wd': 9.285078e-03, 'w_in': 2.387036e-02, 'w_out': 3.962539e-02, 'q_norm_g': 4.161553e-03, 'k_norm_g': 4.020885e-03, 'conv_w': 3.733524e-02, 'conv_b': 1.164171e-01, 'lru_wa': 2.336386e-03, 'lru_ba': 3.259509e-03, 'lru_wx': 4.603851e-03, 'lru_bx': 7.518506e-03, 'lru_lambda': 8.067700e-03, 'final_norm_g': 1.604736e+01}


def _to_microbatches(a, axis):
    t = _jnp.moveaxis(a, axis, 0)
    t = t.reshape((N_MICROBATCH, t.shape[0] // N_MICROBATCH) + t.shape[1:])
    return _jnp.moveaxis(t, 1, axis + 1)


def setup_inputs(seed: int = 0) -> dict:
    inp = _fwd_setup_inputs(seed)
    key = _jax.random.fold_in(_jax.random.key(seed), 7919)
    shape, _ = _output_shape()
    out = dict(inp)
    out["loss_target"] = _jax.random.normal(_jax.random.fold_in(key, 0), shape, _jnp.float32)
    for i, name in enumerate(TWIN_WEIGHTS):
        w = inp[name].astype(_jnp.float32)
        if MOMENT_SCALE is None:
            s = _jnp.sqrt(_jnp.mean(_jnp.square(w)) + 1e-30)
        else:
            s = MOMENT_SCALE[name]
        km, kv = _jax.random.split(_jax.random.fold_in(key, i + 1))
        out[name] = w
        out["m_" + name] = s * _jax.random.normal(km, w.shape, _jnp.float32)
        out["v_" + name] = (s * s) * _jax.random.uniform(kv, w.shape, _jnp.float32, 0.5, 1.5)
    if N_MICROBATCH > 1:
        for name, axis in PER_EXAMPLE_BATCH_AXIS.items():
            out[name] = _to_microbatches(out[name], axis)
    return {'x': out['x'], 'c': out['c'], 'ctx': out['ctx'], 'c_ctx': out['c_ctx'], 'w_mod': out['w_mod'], 'b_mod': out['b_mod'], 'norm_g': out['norm_g'], 'ffn_wg': out['ffn_wg'], 'ffn_wu': out['ffn_wu'], 'ffn_wd': out['ffn_wd'], 'w_in': out['w_in'], 'w_out': out['w_out'], 'q_norm_g': out['q_norm_g'], 'k_norm_g': out['k_norm_g'], 'conv_w': out['conv_w'], 'conv_b': out['conv_b'], 'lru_wa': out['lru_wa'], 'lru_ba': out['lru_ba'], 'lru_wx': out['lru_wx'], 'lru_bx': out['lru_bx'], 'lru_lambda': out['lru_lambda'], 'final_norm_g': out['final_norm_g'], 'loss_target': out['loss_target'], 'm_c_ctx': out['m_c_ctx'], 'm_w_mod': out['m_w_mod'], 'm_b_mod': out['m_b_mod'], 'm_norm_g': out['m_norm_g'], 'm_ffn_wg': out['m_ffn_wg'], 'm_ffn_wu': out['m_ffn_wu'], 'm_ffn_wd': out['m_ffn_wd'], 'm_w_in': out['m_w_in'], 'm_w_out': out['m_w_out'], 'm_q_norm_g': out['m_q_norm_g'], 'm_k_norm_g': out['m_k_norm_g'], 'm_conv_w': out['m_conv_w'], 'm_conv_b': out['m_conv_b'], 'm_lru_wa': out['m_lru_wa'], 'm_lru_ba': out['m_lru_ba'], 'm_lru_wx': out['m_lru_wx'], 'm_lru_bx': out['m_lru_bx'], 'm_lru_lambda': out['m_lru_lambda'], 'm_final_norm_g': out['m_final_norm_g'], 'v_c_ctx': out['v_c_ctx'], 'v_w_mod': out['v_w_mod'], 'v_b_mod': out['v_b_mod'], 'v_norm_g': out['v_norm_g'], 'v_ffn_wg': out['v_ffn_wg'], 'v_ffn_wu': out['v_ffn_wu'], 'v_ffn_wd': out['v_ffn_wd'], 'v_w_in': out['v_w_in'], 'v_w_out': out['v_w_out'], 'v_q_norm_g': out['v_q_norm_g'], 'v_k_norm_g': out['v_k_norm_g'], 'v_conv_w': out['v_conv_w'], 'v_conv_b': out['v_conv_b'], 'v_lru_wa': out['v_lru_wa'], 'v_lru_ba': out['v_lru_ba'], 'v_lru_wx': out['v_lru_wx'], 'v_lru_bx': out['v_lru_bx'], 'v_lru_lambda': out['v_lru_lambda'], 'v_final_norm_g': out['v_final_norm_g']}


def _loss(weights, diff, rest, loss_target):
    with _jax.named_scope("forward"):
        args = {**rest, TWIN_DIFF_INPUT: diff, **{k: w.astype(_WEIGHT_DTYPES[k]) for k, w in weights.items()}}
        y = _forward(args)
    with _jax.named_scope("loss_head"):
        err = _jnp.square(y.astype(_jnp.float32) - loss_target)
        return 0.5 * _jnp.sum(_jnp.mean(err, axis=-1)) if err.ndim else 0.5 * err


def _adamw(w, g, m, v):
    m = ADAM_B1 * m + (1.0 - ADAM_B1) * g
    v = ADAM_B2 * v + (1.0 - ADAM_B2) * _jnp.square(g)
    m_hat = m / (1.0 - ADAM_B1 ** ADAM_STEP)
    v_hat = v / (1.0 - ADAM_B2 ** ADAM_STEP)
    delta = -ADAM_LR * (m_hat / (_jnp.sqrt(v_hat) + ADAM_EPS) + ADAM_WD * w)
    return delta, m, v


def reference(x, c, ctx, c_ctx, w_mod, b_mod, norm_g, ffn_wg, ffn_wu, ffn_wd, w_in, w_out, q_norm_g, k_norm_g, conv_w, conv_b, lru_wa, lru_ba, lru_wx, lru_bx, lru_lambda, final_norm_g, loss_target, m_c_ctx, m_w_mod, m_b_mod, m_norm_g, m_ffn_wg, m_ffn_wu, m_ffn_wd, m_w_in, m_w_out, m_q_norm_g, m_k_norm_g, m_conv_w, m_conv_b, m_lru_wa, m_lru_ba, m_lru_wx, m_lru_bx, m_lru_lambda, m_final_norm_g, v_c_ctx, v_w_mod, v_b_mod, v_norm_g, v_ffn_wg, v_ffn_wu, v_ffn_wd, v_w_in, v_w_out, v_q_norm_g, v_k_norm_g, v_conv_w, v_conv_b, v_lru_wa, v_lru_ba, v_lru_wx, v_lru_bx, v_lru_lambda, v_final_norm_g):
    given = dict(x=x, c=c, ctx=ctx, c_ctx=c_ctx, w_mod=w_mod, b_mod=b_mod, norm_g=norm_g, ffn_wg=ffn_wg, ffn_wu=ffn_wu, ffn_wd=ffn_wd, w_in=w_in, w_out=w_out, q_norm_g=q_norm_g, k_norm_g=k_norm_g, conv_w=conv_w, conv_b=conv_b, lru_wa=lru_wa, lru_ba=lru_ba, lru_wx=lru_wx, lru_bx=lru_bx, lru_lambda=lru_lambda, final_norm_g=final_norm_g, loss_target=loss_target, m_c_ctx=m_c_ctx, m_w_mod=m_w_mod, m_b_mod=m_b_mod, m_norm_g=m_norm_g, m_ffn_wg=m_ffn_wg, m_ffn_wu=m_ffn_wu, m_ffn_wd=m_ffn_wd, m_w_in=m_w_in, m_w_out=m_w_out, m_q_norm_g=m_q_norm_g, m_k_norm_g=m_k_norm_g, m_conv_w=m_conv_w, m_conv_b=m_conv_b, m_lru_wa=m_lru_wa, m_lru_ba=m_lru_ba, m_lru_wx=m_lru_wx, m_lru_bx=m_lru_bx, m_lru_lambda=m_lru_lambda, m_final_norm_g=m_final_norm_g, v_c_ctx=v_c_ctx, v_w_mod=v_w_mod, v_b_mod=v_b_mod, v_norm_g=v_norm_g, v_ffn_wg=v_ffn_wg, v_ffn_wu=v_ffn_wu, v_ffn_wd=v_ffn_wd, v_w_in=v_w_in, v_w_out=v_w_out, v_q_norm_g=v_q_norm_g, v_k_norm_g=v_k_norm_g, v_conv_w=v_conv_w, v_conv_b=v_conv_b, v_lru_wa=v_lru_wa, v_lru_ba=v_lru_ba, v_lru_wx=v_lru_wx, v_lru_bx=v_lru_bx, v_lru_lambda=v_lru_lambda, v_final_norm_g=v_final_norm_g)
    weights = {n: given[n] for n in TWIN_WEIGHTS}
    shared = {n: given[n] for n in SHARED_INPUTS}
    per_example = {n: given[n] for n in ['x', 'c', 'ctx']}
    grad_fn = _jax.value_and_grad(_loss, argnums=(0, 1))

    def one_microbatch(ex, loss_target):
        ex = dict(ex)
        diff = ex.pop(TWIN_DIFF_INPUT)
        return grad_fn(weights, diff, {**shared, **ex}, loss_target)

    if N_MICROBATCH == 1:
        loss, (grad_w, grad_x) = one_microbatch(per_example, given["loss_target"])
    else:
        def body(carry, xs):
            loss_sum, grad_sum = carry
            l_k, (gw_k, gx_k) = one_microbatch(xs[0], xs[1])
            with _jax.named_scope("update"):
                return (loss_sum + l_k, _jax.tree.map(_jnp.add, grad_sum, gw_k)), gx_k

        init = (_jnp.zeros((), _jnp.float32), _jax.tree.map(_jnp.zeros_like, weights))
        (loss, grad_w), grad_x = _jax.lax.scan(body, init, (per_example, given["loss_target"]))
    with _jax.named_scope("update"):
        delta_w, new_m, new_v = {}, {}, {}
        for n in TWIN_WEIGHTS:
            delta_w[n], new_m[n], new_v[n] = _adamw(weights[n], grad_w[n], given["m_" + n], given["v_" + n])
    return (loss, grad_x, *[grad_w[n] for n in TWIN_WEIGHTS], *[delta_w[n] for n in TWIN_WEIGHTS],
            *[new_m[n] for n in TWIN_WEIGHTS], *[new_v[n] for n in TWIN_WEIGHTS])
```

```python
import functools

import jax
import jax.numpy as jnp
from jax import lax
from jax.experimental import pallas as pl
from jax.experimental.pallas import tpu as pltpu

F32 = jnp.float32
BF = jnp.bfloat16
EPS = 1e-6
HEAD_DIM = 128
GRID_W = 64
ROPE_THETA = 10000.0
LRU_C = 8.0
FFN_RES = 0.5
N_MOD = 9
ADAM_LR, ADAM_B1, ADAM_B2, ADAM_EPS, ADAM_WD, ADAM_STEP = 0.001, 0.9, 0.999, 1e-08, 0.01, 10
VMEM_LIMIT = 52 * 1024 * 1024
MESH = pl.DeviceIdType.MESH
ANY = pl.BlockSpec(memory_space=pl.ANY)


def _sds(shape, dt):
    return jax.ShapeDtypeStruct(tuple(shape), dt)


def _pick(n, cands):
    for c in cands:
        if n % c == 0:
            return c
    return n


def _cparams(**kw):
    return pltpu.CompilerParams(vmem_limit_bytes=VMEM_LIMIT, **kw)


def _sig(x):
    return 1.0 / (1.0 + jnp.exp(-x))


def _gelu(x):
    t = jnp.tanh(0.7978845608028654 * (x + 0.044715 * x * x * x))
    return 0.5 * x * (1.0 + t), t


def _gelu_grad(x, t):
    return 0.5 * (1.0 + t) + 0.5 * x * (1.0 - t * t) * 0.7978845608028654 * (1.0 + 3.0 * 0.044715 * x * x)


def ew_call(name, grid, fn, ins, outs, first=None, aliases=None):
    n_in = len(ins)

    def body(*refs):
        ids = tuple(pl.program_id(a) for a in range(len(grid)))
        vals = fn(ids, *refs[:n_in])
        for (_, _, acc), o_ref, v in zip(outs, refs[n_in:], vals):
            if v is None:
                continue
            if not acc:
                o_ref[...] = v.astype(o_ref.dtype)
            else:
                is_first = first(ids)

                @pl.when(is_first)
                def _(o_ref=o_ref, v=v):
                    o_ref[...] = v.astype(o_ref.dtype)

                @pl.when(jnp.logical_not(is_first))
                def _(o_ref=o_ref, v=v):
                    o_ref[...] += v.astype(o_ref.dtype)

    res = pl.pallas_call(
        body, grid=grid, name=name,
        in_specs=[s for _, s in ins], out_specs=[s for _, s, _ in outs], out_shape=[o for o, _, _ in outs],
        input_output_aliases=aliases or {}, compiler_params=_cparams(),
    )(*[a for a, _ in ins])
    return res


def fused_mm(name, grid, ins, prods, acc_shapes, epi, outs, extras=(), pre=None):
    n_in, n_ex, n_out = len(ins), len(extras), len(outs)
    nk = grid[-1]
    pre = pre or {}

    def body(*refs):
        in_refs = refs[:n_in]
        ex_refs = refs[n_in:n_in + n_ex]
        out_refs = refs[n_in + n_ex:n_in + n_ex + n_out]
        accs = refs[n_in + n_ex + n_out:]
        ids = tuple(pl.program_id(a) for a in range(len(grid)))
        k = ids[-1]

        @pl.when(k == 0)
        def _():
            for a in accs:
                a[...] = jnp.zeros(a.shape, F32)

        loaded = {}

        def operand(i):
            if i not in loaded:
                v = in_refs[i][...]
                if i in pre:
                    v = pre[i](v)
                loaded[i] = v.astype(BF)
            return loaded[i]

        for ia, ib, dims, ai in prods:
            accs[ai][...] += lax.dot_general(operand(ia), operand(ib), (dims, ((), ())), preferred_element_type=F32)

        @pl.when(k == nk - 1)
        def _():
            vals = epi(ids, [a[...] for a in accs], ex_refs)
            for o_ref, v in zip(out_refs, vals):
                o_ref[...] = v.astype(o_ref.dtype)

    return pl.pallas_call(
        body, grid=grid, name=name,
        in_specs=[s for _, s in ins] + [s for _, s in extras],
        out_specs=[s for _, s in outs], out_shape=[o for o, _ in outs],
        scratch_shapes=[pltpu.VMEM(s, F32) for s in acc_shapes], compiler_params=_cparams(),
    )(*[a for a, _ in ins], *[a for a, _ in extras])


NN = ((1,), (0,))
NT = ((1,), (1,))
TN = ((0,), (0,))


class Dims:
    def __init__(self, S, C, D, F4, W4, NS, LB):
        self.S, self.C, self.D, self.F4, self.W4, self.NS, self.LB = S, C, D, F4, W4, NS, LB
        self.T = S + C
        self.DFF = F4 * NS
        self.INW = W4 * NS
        self.NQ = D // HEAD_DIM
        self.KVW = (self.INW - 5 * D) // 2
        self.NKV = self.KVW // HEAD_DIM
        self.G = self.NQ // self.NKV
        self.OFF_K = D
        self.OFF_V = D + self.KVW
        self.OFF_LX = D + 2 * self.KVW
        self.OFF_LG = self.OFF_LX + D
        self.OFF_GA = self.OFF_LG + D
        self.OFF_GL = self.OFF_GA + D
        self.bm = _pick(C, [256, 128, 64, 32, 16, 8])
        self.nCb = C // self.bm
        self.nTb = self.T // self.bm
        self.nSb = S // self.bm
        self.mT = _pick(self.T, [544, 512, 384, 256, 128])
        self.mS = _pick(S, [512, 256, 128])
        self.cw = _pick(D, [1024, 512, 256, 128]) if (self.OFF_LX % 1024 == 0 and D % 1024 == 0) else _pick(
            self.OFF_LX, [512, 256, 128])
        self.nsub = 2 if (W4 % 256 == 0 and W4 >= 512) else 1
        self.wb = W4 // self.nsub
        self.LBD = D // LB
        self.bq = _pick(C, [256, 128]) if S % _pick(C, [256, 128]) == 0 else 128


def rope_tables(dm):
    rows = dm.S // GRID_W
    row = jnp.repeat(jnp.arange(rows, dtype=F32), GRID_W)
    col = jnp.tile(jnp.arange(GRID_W, dtype=F32), rows)
    axis_dims = HEAD_DIM // 2
    freqs = ROPE_THETA ** (-jnp.arange(0, axis_dims, 2, dtype=F32) / axis_dims)
    ang = jnp.concatenate([row[:, None] * freqs, col[:, None] * freqs], axis=-1)
    cos = jnp.repeat(jnp.cos(ang), 2, axis=-1)
    sin = jnp.repeat(jnp.sin(ang), 2, axis=-1)
    sign = jnp.tile(jnp.array([-1.0, 1.0], F32), HEAD_DIM // 2)
    sin = sin * sign
    cos = jnp.concatenate([jnp.ones((dm.C, HEAD_DIM), F32), cos], axis=0)
    sin = jnp.concatenate([jnp.zeros((dm.C, HEAD_DIM), F32), sin], axis=0)
    return cos, sin


def _pair_swap(y):
    lane = lax.broadcasted_iota(jnp.int32, y.shape, 1)
    nxt = pltpu.roll(y, y.shape[1] - 1, 1)
    prv = pltpu.roll(y, 1, 1)
    return jnp.where((lane & 1) == 0, nxt, prv)


def normmod_fwd(name, dm, x, norm_g3, stage, modv, rows_T):
    D, bm = dm.D, dm.bm
    nb = dm.nTb if rows_T else dm.nSb
    typ = (lambda i: jnp.where(i < dm.nCb, 0, 1)) if rows_T else (lambda i: 1)

    def fn(ids, x_ref, g_ref, sh_ref, sc_ref):
        xv = x_ref[...]
        r = lax.rsqrt(jnp.mean(xv * xv, axis=-1, keepdims=True) + EPS)
        n = xv * r * g_ref[...]
        return [n * (1.0 + sc_ref[...]) + sh_ref[...]]

    return ew_call(
        name, (nb,), fn,
        [(x, pl.BlockSpec((bm, D), lambda i: (i, 0))),
         (norm_g3, pl.BlockSpec((None, 1, D), lambda i: (stage, 0, 0))),
         (modv, pl.BlockSpec((None, None, 1, D), lambda i: (typ(i), 3 * stage, 0, 0))),
         (modv, pl.BlockSpec((None, None, 1, D), lambda i: (typ(i), 3 * stage + 1, 0, 0)))],
        [(_sds(x.shape, BF), pl.BlockSpec((bm, D), lambda i: (i, 0)), False)])[0]


def normmod_bwd(name, dm, dh, x, dres, norm_g3, stage, modv, rows_T, dres_lat_only):
    D, bm = dm.D, dm.bm
    nb = dm.nTb if rows_T else dm.nSb
    nCb = dm.nCb
    typ = (lambda i: jnp.where(i < nCb, 0, 1)) if rows_T else (lambda i: 1)
    if dres_lat_only:
        dres_map = lambda i: (jnp.maximum(i - nCb, 0), 0)
    else:
        dres_map = lambda i: (i, 0)

    def fn(ids, dh_ref, x_ref, dres_ref, g_ref, sc_ref):
        i = ids[0]
        xv = x_ref[...]
        dhv = dh_ref[...].astype(F32)
        r = lax.rsqrt(jnp.mean(xv * xv, axis=-1, keepdims=True) + EPS)
        xn = xv * r
        g = g_ref[...]
        n = xn * g
        dn = dhv * (1.0 + sc_ref[...])
        dxn = dn * g
        dx = r * (dxn - xn * jnp.mean(dxn * xn, axis=-1, keepdims=True))
        dresv = dres_ref[...]
        if dres_lat_only:
            dresv = jnp.where(i >= nCb, dresv, 0.0)
        dsh = jnp.sum(dhv, axis=0, keepdims=True)
        dsc = jnp.sum(dhv * n, axis=0, keepdims=True)
        dg = jnp.sum(dn * xn, axis=0, keepdims=True)
        return [dx + dresv, dsh, dsc, dg]

    if rows_T:
        first = lambda ids: (ids[0] == 0) | (ids[0] == nCb)
    else:
        first = lambda ids: ids[0] == 0
    acc = (_sds((2, 1, D), F32), pl.BlockSpec((None, 1, D), lambda i: (typ(i), 0, 0)), True)
    return ew_call(
        name, (nb,), fn,
        [(dh, pl.BlockSpec((bm, D), lambda i: (i, 0))),
         (x, pl.BlockSpec((bm, D), lambda i: (i, 0))),
         (dres, pl.BlockSpec((bm, D), dres_map)),
         (norm_g3, pl.BlockSpec((None, 1, D), lambda i: (stage, 0, 0))),
         (modv, pl.BlockSpec((None, None, 1, D), lambda i: (typ(i), 3 * stage + 1, 0, 0)))],
        [(_sds(x.shape, F32), pl.BlockSpec((bm, D), lambda i: (i, 0)), False), acc, acc, acc], first=first)


def gate_bwd(name, dm, dx, f, modv, gidx, scale, rows_T):
    D, bm = dm.D, dm.bm
    nb = dm.nTb if rows_T else dm.nSb
    nCb = dm.nCb
    typ = (lambda i: jnp.where(i < nCb, 0, 1)) if rows_T else (lambda i: 1)

    def fn(ids, dx_ref, f_ref, g_ref):
        dxv = dx_ref[...]
        return [scale * g_ref[...] * dxv, jnp.sum(scale * f_ref[...].astype(F32) * dxv, axis=0, keepdims=True)]

    if rows_T:
        first = lambda ids: (ids[0] == 0) | (ids[0] == nCb)
    else:
        first = lambda ids: ids[0] == 0
    return ew_call(
        name, (nb,), fn,
        [(dx, pl.BlockSpec((bm, D), lambda i: (i, 0))),
         (f, pl.BlockSpec((bm, D), lambda i: (i, 0))),
         (modv, pl.BlockSpec((None, None, 1, D), lambda i: (typ(i), gidx, 0, 0)))],
        [(_sds(dx.shape, BF), pl.BlockSpec((bm, D), lambda i: (i, 0)), False),
         (_sds((2, 1, D), F32), pl.BlockSpec((None, 1, D), lambda i: (typ(i), 0, 0)), True)], first=first)


def ffn_fwd(name, dm, h, xres, wg, wu, wd, layer, modv, gidx, rows_T):
    D, F4, NS = dm.D, dm.F4, dm.NS
    M = h.shape[0]
    bm = dm.mT if rows_T else dm.mS
    bk = _pick(D, [512, 256, 128])
    C = dm.C

    def epi_up(ids, accs, ex):
        a, u = accs
        return [a, u, a * _sig(a) * u]

    hspec = pl.BlockSpec((bm, bk), lambda i, j, k: (i, k))
    wspec = pl.BlockSpec((None, None, bk, F4), lambda i, j, k: (j, layer, k, 0))
    ospec = pl.BlockSpec((bm, F4), lambda i, j, k: (i, j))
    a, u, s = fused_mm(
        name + "_up", (M // bm, NS, D // bk), [(h, hspec), (wg, wspec), (wu, wspec)],
        [(0, 1, NN, 0), (0, 2, NN, 1)], [(bm, F4), (bm, F4)], epi_up,
        [(_sds((M, dm.DFF), BF), ospec)] * 3)

    bn = _pick(D, [1024, 512, 256, 128])

    def epi_dn(ids, accs, ex):
        f = accs[0]
        if rows_T:
            row = ids[0] * bm + lax.broadcasted_iota(jnp.int32, (bm, 1), 0)
            gate = jnp.where(row < C, ex[1][...], ex[2][...])
        else:
            gate = ex[2][...]
        return [ex[0][...] + FFN_RES * gate * f, f]

    gspec = lambda t: pl.BlockSpec((None, None, 1, bn), lambda i, j, k: (t, gidx, 0, j))
    xo, f = fused_mm(
        name + "_down", (M // bm, D // bn, NS),
        [(s, pl.BlockSpec((bm, F4), lambda i, j, k: (i, k))),
         (wd, pl.BlockSpec((None, None, F4, bn), lambda i, j, k: (k, layer, 0, j)))],
        [(0, 1, NN, 0)], [(bm, bn)], epi_dn,
        [(_sds((M, D), F32), pl.BlockSpec((bm, bn), lambda i, j, k: (i, j))),
         (_sds((M, D), BF), pl.BlockSpec((bm, bn), lambda i, j, k: (i, j)))],
        extras=[(xres, pl.BlockSpec((bm, bn), lambda i, j, k: (i, j))), (modv, gspec(0)), (modv, gspec(1))])
    return xo, a, u, s, f


def ffn_bwd(name, dm, df, h, a, u, s, wg, wu, wd, layer, rows_T):
    D, F4, NS = dm.D, dm.F4, dm.NS
    M = h.shape[0]
    bm = dm.mT if rows_T else dm.mS
    bk = _pick(D, [512, 256, 128])

    def epi_ds(ids, accs, ex):
        ds = accs[0]
        av = ex[0][...].astype(F32)
        uv = ex[1][...].astype(F32)
        sg = _sig(av)
        return [ds * uv * (sg * (1.0 + av * (1.0 - sg))), ds * av * sg]

    ospec = pl.BlockSpec((bm, F4), lambda i, j, k: (i, j))
    da, du = fused_mm(
        name + "_ds", (M // bm, NS, D // bk),
        [(df, pl.BlockSpec((bm, bk), lambda i, j, k: (i, k))),
         (wd, pl.BlockSpec((None, None, F4, bk), lambda i, j, k: (j, layer, 0, k)))],
        [(0, 1, NT, 0)], [(bm, F4)], epi_ds, [(_sds((M, dm.DFF), BF), ospec)] * 2,
        extras=[(a, ospec), (u, ospec)])

    ident = lambda ids, accs, ex: list(accs)
    bn = _pick(D, [1024, 512, 256, 128])
    dwd = fused_mm(
        name + "_dwd", (NS, D // bn, M // bm),
        [(s, pl.BlockSpec((bm, F4), lambda i, j, k: (k, i))),
         (df, pl.BlockSpec((bm, bn), lambda i, j, k: (k, j)))],
        [(0, 1, TN, 0)], [(F4, bn)], ident,
        [(_sds((NS, F4, D), BF), pl.BlockSpec((None, F4, bn), lambda i, j, k: (i, 0, j)))])[0]

    dwg, dwu = fused_mm(
        name + "_dwgu", (D // bn, NS, M // bm),
        [(h, pl.BlockSpec((bm, bn), lambda i, j, k: (k, i))),
         (da, pl.BlockSpec((bm, F4), lambda i, j, k: (k, j))),
         (du, pl.BlockSpec((bm, F4), lambda i, j, k: (k, j)))],
        [(0, 1, TN, 0), (0, 2, TN, 1)], [(bn, F4), (bn, F4)], ident,
        [(_sds((NS, D, F4), BF), pl.BlockSpec((None, bn, F4), lambda i, j, k: (j, i, 0)))] * 2)

    dh = fused_mm(
        name + "_dh", (M // bm, D // bn, NS),
        [(da, pl.BlockSpec((bm, F4), lambda i, j, k: (i, k))),
         (wg, pl.BlockSpec((None, None, bn, F4), lambda i, j, k: (k, layer, j, 0))),
         (du, pl.BlockSpec((bm, F4), lambda i, j, k: (i, k))),
         (wu, pl.BlockSpec((None, None, bn, F4), lambda i, j, k: (k, layer, j, 0)))],
        [(0, 1, NT, 0), (2, 3, NT, 0)], [(bm, bn)], ident,
        [(_sds((M, D), F32), pl.BlockSpec((bm, bn), lambda i, j, k: (i, j)))])[0]
    return dh, dwg, dwu, dwd


def qk_prep(dm, P, gq, gk, cosf, sinf):
    D, KVW, bm = dm.D, dm.KVW, dm.bm

    def head_norm_rope(xh, g, c, s):
        r = lax.rsqrt(jnp.mean(xh * xh, axis=-1, keepdims=True) + EPS)
        y = xh * r * g
        return y * c + _pair_swap(y) * s

    def fn(ids, q_ref, k_ref, v_ref, gq_ref, gk_ref, c_ref, s_ref):
        c, s = c_ref[...], s_ref[...]
        qs = [head_norm_rope(q_ref[:, h * HEAD_DIM:(h + 1) * HEAD_DIM], gq_ref[...], c, s) for h in range(dm.NQ)]
        ks = [head_norm_rope(k_ref[:, h * HEAD_DIM:(h + 1) * HEAD_DIM], gk_ref[...], c, s) for h in range(dm.NKV)]
        return [jnp.concatenate(qs, axis=1), jnp.concatenate(ks, axis=1), v_ref[...]]

    hspec = pl.BlockSpec((bm, HEAD_DIM), lambda i: (i, 0))
    vec = pl.BlockSpec((1, HEAD_DIM), lambda i: (0, 0))
    return ew_call(
        "qk_prep", (dm.nTb,), fn,
        [(P, pl.BlockSpec((bm, D), lambda i: (i, 0))),
         (P, pl.BlockSpec((bm, KVW), lambda i: (i, dm.OFF_K // KVW))),
         (P, pl.BlockSpec((bm, KVW), lambda i: (i, dm.OFF_V // KVW))),
         (gq, vec), (gk, vec), (cosf, hspec), (sinf, hspec)],
        [(_sds((dm.T, D), BF), pl.BlockSpec((bm, D), lambda i: (i, 0)), False),
         (_sds((dm.T, KVW), BF), pl.BlockSpec((bm, KVW), lambda i: (i, 0)), False),
         (_sds((dm.T, KVW), BF), pl.BlockSpec((bm, KVW), lambda i: (i, 0)), False)])


def qk_prep_bwd(dm, dq, dk, dv, P, gq, gk, cosf, sinf, dP):
    D, KVW, bm, nCb = dm.D, dm.KVW, dm.bm, dm.nCb
    W = D + 2 * KVW

    def head_bwd(d, xh, g, c, s):
        dy = d * c - _pair_swap(d) * s
        r = lax.rsqrt(jnp.mean(xh * xh, axis=-1, keepdims=True) + EPS)
        xn = xh * r
        dg = jnp.sum(dy * xn, axis=0, keepdims=True)
        dxn = dy * g
        return r * (dxn - xn * jnp.mean(dxn * xn, axis=-1, keepdims=True)), dg

    def fn(ids, dq_ref, dk_ref, dv_ref, q_ref, k_ref, gq_ref, gk_ref, c_ref, s_ref, dp_any):
        i = ids[0]
        c, s = c_ref[...], s_ref[...]
        lat = i >= nCb
        outs, dgq = [], jnp.zeros((1, HEAD_DIM), F32)
        for h in range(dm.NQ):
            sl = slice(h * HEAD_DIM, (h + 1) * HEAD_DIM)
            d = jnp.where(lat, dq_ref[:, sl], 0.0)
            dx, dg = head_bwd(d, q_ref[:, sl], gq_ref[...], c, s)
            outs.append(dx)
            dgq = dgq + dg
        dgk = jnp.zeros((1, HEAD_DIM), F32)
        for h in range(dm.NKV):
            sl = slice(h * HEAD_DIM, (h + 1) * HEAD_DIM)
            dx, dg = head_bwd(dk_ref[:, sl], k_ref[:, sl], gk_ref[...], c, s)
            outs.append(dx)
            dgk = dgk + dg
        outs.append(dv_ref[...])
        return [jnp.concatenate(outs, axis=1), dgq, dgk]

    hspec = pl.BlockSpec((bm, HEAD_DIM), lambda i: (i, 0))
    vec = pl.BlockSpec((1, HEAD_DIM), lambda i: (0, 0))
    return ew_call(
        "qk_prep_bwd", (dm.nTb,), fn,
        [(dq, pl.BlockSpec((bm, D), lambda i: (jnp.maximum(i - nCb, 0), 0))),
         (dk, pl.BlockSpec((bm, KVW), lambda i: (i, 0))),
         (dv, pl.BlockSpec((bm, KVW), lambda i: (i, 0))),
         (P, pl.BlockSpec((bm, D), lambda i: (i, 0))),
         (P, pl.BlockSpec((bm, KVW), lambda i: (i, dm.OFF_K // KVW))),
         (gq, vec), (gk, vec), (cosf, hspec), (sinf, hspec), (dP, ANY)],
        [(_sds(dP.shape, BF), pl.BlockSpec((bm, W), lambda i: (i, 0)), False),
         (_sds((1, HEAD_DIM), F32), vec, True), (_sds((1, HEAD_DIM), F32), vec, True)],
        first=lambda ids: ids[0] == 0, aliases={9: 0})


def attention_fwd(dm, qr, kr, vb):
    S, T, D, G, nCb = dm.S, dm.T, dm.D, dm.G, dm.nCb
    bq = dm.bq
    off = dm.C // bq
    scale = HEAD_DIM ** -0.5
    GW = G * HEAD_DIM

    def body(q_ref, k_ref, v_ref, o_ref):
        k = k_ref[...]
        v = v_ref[...]
        for h in range(G):
            sl = slice(h * HEAD_DIM, (h + 1) * HEAD_DIM)
            s = lax.dot_general(q_ref[:, sl], k, (NT, ((), ())), preferred_element_type=F32) * scale
            m = jnp.max(s, axis=-1, keepdims=True)
            p = jnp.exp(s - m)
            l = jnp.sum(p, axis=-1, keepdims=True)
            o = lax.dot_general(p.astype(BF), v, (NN, ((), ())), preferred_element_type=F32)
            o_ref[:, sl] = o / l

    return pl.pallas_call(
        body, grid=(dm.NKV, S // bq), name="attn_fwd",
        in_specs=[pl.BlockSpec((bq, GW), lambda g, i: (i + off, g)),
                  pl.BlockSpec((T, HEAD_DIM), lambda g, i: (0, g)),
                  pl.BlockSpec((T, HEAD_DIM), lambda g, i: (0, g))],
        out_specs=pl.BlockSpec((bq, GW), lambda g, i: (i, g)),
        out_shape=_sds((S, D), F32), compiler_params=_cparams(),
    )(qr, kr, vb)


def attention_bwd(dm, qr, kr, vb, dattn):
    S, T, D, G = dm.S, dm.T, dm.D, dm.G
    bq = dm.bq
    off = dm.C // bq
    scale = HEAD_DIM ** -0.5
    GW = G * HEAD_DIM

    def body(q_ref, k_ref, v_ref, do_ref, dq_ref, dk_ref, dv_ref):
        i = pl.program_id(1)

        @pl.when(i == 0)
        def _():
            dk_ref[...] = jnp.zeros(dk_ref.shape, F32)
            dv_ref[...] = jnp.zeros(dv_ref.shape, F32)

        k = k_ref[...]
        v = v_ref[...]
        for h in range(G):
            sl = slice(h * HEAD_DIM, (h + 1) * HEAD_DIM)
            q = q_ref[:, sl]
            do = do_ref[:, sl]
            s = lax.dot_general(q, k, (NT, ((), ())), preferred_element_type=F32) * scale
            m = jnp.max(s, axis=-1, keepdims=True)
            e = jnp.exp(s - m)
            p = e / jnp.sum(e, axis=-1, keepdims=True)
            pb = p.astype(BF)
            dv_ref[...] += lax.dot_general(pb, do, (TN, ((), ())), preferred_element_type=F32)
            dp = lax.dot_general(do, v, (NT, ((), ())), preferred_element_type=F32)
            ds = p * (dp - jnp.sum(p * dp, axis=-1, keepdims=True)) * scale
            dsb = ds.astype(BF)
            dq_ref[:, sl] = lax.dot_general(dsb, k, (NN, ((), ())), preferred_element_type=F32)
            dk_ref[...] += lax.dot_general(dsb, q, (TN, ((), ())), preferred_element_type=F32)

    return pl.pallas_call(
        body, grid=(dm.NKV, S // bq), name="attn_bwd",
        in_specs=[pl.BlockSpec((bq, GW), lambda g, i: (i + off, g)),
                  pl.BlockSpec((T, HEAD_DIM), lambda g, i: (0, g)),
                  pl.BlockSpec((T, HEAD_DIM), lambda g, i: (0, g)),
                  pl.BlockSpec((bq, GW), lambda g, i: (i + off, g))],
        out_specs=[pl.BlockSpec((bq, GW), lambda g, i: (i, g)),
                   pl.BlockSpec((T, HEAD_DIM), lambda g, i: (0, g)),
                   pl.BlockSpec((T, HEAD_DIM), lambda g, i: (0, g))],
        out_shape=[_sds((S, D), F32), _sds((T, dm.KVW), F32), _sds((T, dm.KVW), F32)],
        compiler_params=_cparams(),
    )(qr, kr, vb, dattn)


def _conv_taps(dm, lx, masks_only=False):
    T, C = dm.T, dm.C
    t = lax.broadcasted_iota(jnp.int32, (T, 1), 0)
    valid = [(t >= 2) & ((t < C) | (t >= C + 2)), (t >= 1) & ((t < C) | (t >= C + 1)), None,
             (t != C - 1) & (t != T - 1)]
    shifts = [2, 1, 0, T - 1]
    taps = []
    for k in range(4):
        if k == 2:
            taps.append(lx)
        else:
            taps.append(jnp.where(valid[k], pltpu.roll(lx, shifts[k], 0), 0.0))
    return taps


def _scan_tiles(dm, asc, split, a_ref, u_ref, out_ref, mode):
    T, C = dm.T, dm.C
    nT, nC = T // 8, C // 8
    row = lax.broadcasted_iota(jnp.int32, (8, HEAD_DIM), 0)

    def tile_of(i):
        if not split:
            return i if asc else nT - 1 - i
        if asc:
            return jnp.where(i < nT - nC, nC + i, i - (nT - nC))
        return jnp.where(i < nC, nC - 1 - i, nT - 1 - (i - nC))

    def body(i, carry):
        off = pl.multiple_of(tile_of(i) * 8, 8)
        a = a_ref[pl.ds(off, 8), :]
        b = u_ref[pl.ds(off, 8), :]
        if mode == 'lam':
            if asc:
                coef = jnp.where(row == 0, 1.0, pltpu.roll(a, 1, 0))
            else:
                coef = jnp.where(row == 7, 1.0, pltpu.roll(a, 7, 0))
        else:
            coef = a
        A, B = coef, b
        for d in (1, 2, 4):
            if asc:
                ok = row >= d
                A_sh = jnp.where(ok, pltpu.roll(A, d, 0), 1.0)
                B_sh = jnp.where(ok, pltpu.roll(B, d, 0), 0.0)
            else:
                ok = row < 8 - d
                A_sh = jnp.where(ok, pltpu.roll(A, 8 - d, 0), 1.0)
                B_sh = jnp.where(ok, pltpu.roll(B, 8 - d, 0), 0.0)
            B = B + A * B_sh
            A = A * A_sh
        h = A * carry + B
        out_ref[pl.ds(off, 8), :] = h
        last = h[7:8, :] if asc else h[0:1, :]
        if mode == 'lam':
            last = last * (a[7:8, :] if asc else a[0:1, :])
        return jnp.broadcast_to(last, (8, HEAD_DIM))

    lax.fori_loop(0, nT, body, jnp.zeros((8, HEAD_DIM), F32))


def _lru_gates(xc, wa, ba, wx, bx, sp):
    xb = xc.astype(BF)
    r = _sig(jnp.dot(xb, wa, preferred_element_type=F32) + ba)
    i = _sig(jnp.dot(xb, wx, preferred_element_type=F32) + bx)
    a = jnp.exp(-LRU_C * r * sp)
    m = jnp.sqrt(1.0 - a * a)
    return r, i, a, m


def lru_fwd(dm, P, conv_w, conv_b, wa, ba, wx, bx, sp, direction):
    T, D, LB = dm.T, dm.D, dm.LB
    W = dm.LBD
    R = _pick(T, [272, 256, 128, 64, 8])
    lxb = dm.OFF_LX // W

    def body(lx_ref, cw_ref, cb_ref, wa_ref, ba_ref, wx_ref, bx_ref, sp_ref, h_ref, xc_ref, a_ref):
        taps = _conv_taps(dm, lx_ref[...])
        xc = cb_ref[...]
        for k in range(4):
            xc = xc + taps[k] * cw_ref[k:k + 1, :]
        xc_ref[...] = xc
        wa_, wx_ = wa_ref[...].astype(BF), wx_ref[...].astype(BF)

        def chunk(ci, _):
            off = pl.multiple_of(ci * R, 8)
            x = xc_ref[pl.ds(off, R), :]
            r, i, a, m = _lru_gates(x, wa_, ba_ref[...], wx_, bx_ref[...], sp_ref[...])
            a_ref[pl.ds(off, R), :] = a
            h_ref[pl.ds(off, R), :] = m * i * x
            return 0

        lax.fori_loop(0, T // R, chunk, 0)
        _scan_tiles(dm, direction == 0, direction == 1, a_ref, h_ref, h_ref, 'h')

    strip = lambda j: (0, j)
    vec = pl.BlockSpec((None, 1, W), lambda j: (direction, 0, j))
    mat = pl.BlockSpec((None, None, W, W), lambda j: (direction, j, 0, 0))
    return pl.pallas_call(
        body, grid=(LB,), name="lru_fwd%d" % direction,
        in_specs=[pl.BlockSpec((T, W), lambda j: (0, lxb + j)),
                  pl.BlockSpec((4, W), strip), pl.BlockSpec((1, W), strip), mat, vec, mat, vec, vec],
        out_specs=pl.BlockSpec((T, W), strip), out_shape=_sds((T, D), F32),
        scratch_shapes=[pltpu.VMEM((T, W), F32), pltpu.VMEM((T, W), F32)], compiler_params=_cparams(),
    )(P, conv_w, conv_b, wa, ba, wx, bx, sp)


def lru_bwd(dm, P, dh, h, conv_w, conv_b, wa, ba, wx, bx, sp, sg, direction, dxc_in, dP):
    T, C, D, LB = dm.T, dm.C, dm.D, dm.LB
    W = dm.LBD
    R = _pick(T, [272, 256, 128, 64, 8])
    lxb = dm.OFF_LX // W
    last = direction == 1

    def body(*refs):
        (lx_ref, dh_ref, h_ref, cw_ref, cb_ref, wa_ref, ba_ref, wx_ref, bx_ref, sp_ref, sg_ref) = refs[:11]
        pos = 11
        if last:
            dxin_ref, _dp_any = refs[pos:pos + 2]
            pos += 2
        out0 = refs[pos]
        dwa_ref, dba_ref, dwx_ref, dbx_ref, dlam_ref = refs[pos + 1:pos + 6]
        pos += 6
        if last:
            dcw_ref, dcb_ref = refs[pos:pos + 2]
            pos += 2
        xc_ref, a_ref, lam_ref, hp_ref, dxc_ref = refs[pos:pos + 5]

        lx = lx_ref[...]
        taps = _conv_taps(dm, lx)
        xc = cb_ref[...]
        for k in range(4):
            xc = xc + taps[k] * cw_ref[k:k + 1, :]
        xc_ref[...] = xc
        wa_, wx_ = wa_ref[...].astype(BF), wx_ref[...].astype(BF)

        def chunk_a(ci, _):
            off = pl.multiple_of(ci * R, 8)
            r, i, a, m = _lru_gates(xc_ref[pl.ds(off, R), :], wa_, ba_ref[...], wx_, bx_ref[...], sp_ref[...])
            a_ref[pl.ds(off, R), :] = a
            return 0

        lax.fori_loop(0, T // R, chunk_a, 0)
        _scan_tiles(dm, direction == 1, direction == 1, a_ref, dh_ref, lam_ref, 'lam')
        t = lax.broadcasted_iota(jnp.int32, (T, 1), 0)
        hv = h_ref[...]
        if direction == 0:
            hp_ref[...] = jnp.where(t == 0, 0.0, pltpu.roll(hv, 1, 0))
        else:
            nxt = pltpu.roll(hv, T - 1, 0)
            hp_ref[...] = jnp.where(t == C - 1, 0.0, jnp.where(t == T - 1, jnp.broadcast_to(hv[0:1, :], hv.shape), nxt))

        def chunk_b(ci, carry):
            dwa, dwx, dba, dbx, dlam = carry
            off = pl.multiple_of(ci * R, 8)
            x = xc_ref[pl.ds(off, R), :]
            r, i, a, m = _lru_gates(x, wa_, ba_ref[...], wx_, bx_ref[...], sp_ref[...])
            lam = lam_ref[pl.ds(off, R), :]
            ix = i * x
            da = lam * hp_ref[pl.ds(off, R), :] - lam * ix * a / m
            dloga = da * a
            dza = dloga * (-LRU_C) * sp_ref[...] * r * (1.0 - r)
            dzx = lam * m * x * i * (1.0 - i)
            dzab, dzxb = dza.astype(BF), dzx.astype(BF)
            xb = x.astype(BF)
            dxc = lam * m * i
            dxc = dxc + lax.dot_general(dzab, wa_, (NT, ((), ())), preferred_element_type=F32)
            dxc = dxc + lax.dot_general(dzxb, wx_, (NT, ((), ())), preferred_element_type=F32)
            dxc_ref[pl.ds(off, R), :] = dxc
            dwa = dwa + lax.dot_general(xb, dzab, (TN, ((), ())), preferred_element_type=F32)
            dwx = dwx + lax.dot_general(xb, dzxb, (TN, ((), ())), preferred_element_type=F32)
            dba = dba + jnp.sum(dza, axis=0, keepdims=True)
            dbx = dbx + jnp.sum(dzx, axis=0, keepdims=True)
            dlam = dlam + jnp.sum(dloga * LRU_C * r, axis=0, keepdims=True)
            return dwa, dwx, dba, dbx, dlam

        z = jnp.zeros((W, W), F32)
        zv = jnp.zeros((1, W), F32)
        dwa, dwx, dba, dbx, dlam = lax.fori_loop(0, T // R, chunk_b, (z, z, zv, zv, zv))
        dwa_ref[...] = dwa
        dwx_ref[...] = dwx
        dba_ref[...] = dba
        dbx_ref[...] = dbx
        dlam_ref[...] = dlam * sg_ref[...]
        if not last:
            out0[...] = dxc_ref[...]
        else:
            dxc = dxc_ref[...] + dxin_ref[...]
            dcb_ref[...] = jnp.sum(dxc, axis=0, keepdims=True)
            rows = [jnp.sum(dxc * taps[k], axis=0, keepdims=True) for k in range(4)]
            dcw_ref[...] = jnp.concatenate(rows, axis=0)
            valid = [(t < T - 2) & ((t >= C) | (t < C - 2)), (t < T - 1) & ((t >= C) | (t < C - 1)), None,
                     (t != 0) & (t != C)]
            shifts = [T - 2, T - 1, 0, 1]
            dlx = dxc * cw_ref[2:3, :]
            for k in (0, 1, 3):
                dlx = dlx + jnp.where(valid[k], pltpu.roll(dxc, shifts[k], 0), 0.0) * cw_ref[k:k + 1, :]
            out0[...] = dlx.astype(out0.dtype)

    strip = lambda j: (0, j)
    sspec = pl.BlockSpec((T, W), strip)
    vec = pl.BlockSpec((None, 1, W), lambda j: (direction, 0, j))
    mat = pl.BlockSpec((None, None, W, W), lambda j: (direction, j, 0, 0))
    ins = [P, dh, h, conv_w, conv_b, wa, ba, wx, bx, sp, sg]
    in_specs = [pl.BlockSpec((T, W), lambda j: (0, lxb + j)), sspec, sspec,
                pl.BlockSpec((4, W), strip), pl.BlockSpec((1, W), strip), mat, vec, mat, vec, vec, vec]
    omat = pl.BlockSpec((None, W, W), lambda j: (j, 0, 0))
    ovec = pl.BlockSpec((1, W), strip)
    small = [_sds((LB, W, W), F32), _sds((1, D), F32), _sds((LB, W, W), F32), _sds((1, D), F32), _sds((1, D), F32)]
    small_specs = [omat, ovec, omat, ovec, ovec]
    if last:
        ins += [dxc_in, dP]
        in_specs += [sspec, ANY]
        out_shape = [_sds(dP.shape, BF)] + small + [_sds((4, D), F32), _sds((1, D), F32)]
        out_specs = [pl.BlockSpec((T, W), lambda j: (0, lxb + j))] + small_specs + [pl.BlockSpec((4, W), strip), ovec]
        aliases = {12: 0}
    else:
        out_shape = [_sds((T, D), F32)] + small
        out_specs = [sspec] + small_specs
        aliases = {}
    return pl.pallas_call(
        body, grid=(LB,), name="lru_bwd%d" % direction, in_specs=in_specs, out_specs=out_specs, out_shape=out_shape,
        scratch_shapes=[pltpu.VMEM((T, W), F32)] * 5, input_output_aliases=aliases, compiler_params=_cparams(),
    )(*ins)


def merge_fwd(dm, P, attn, hf, hb):
    S, D, bm, cw, nCb = dm.S, dm.D, dm.bm, dm.cw, dm.nCb

    def fn(ids, lg_ref, ga_ref, gl_ref, at_ref, hf_ref, hb_ref):
        ge, _ = _gelu(lg_ref[...])
        lru = (hf_ref[...] + hb_ref[...]) * ge
        return [_sig(ga_ref[...]) * at_ref[...] + _sig(gl_ref[...]) * lru]

    pspec = lambda off: pl.BlockSpec((bm, cw), lambda i, j: (i + nCb, off // cw + j))
    tspec = pl.BlockSpec((bm, cw), lambda i, j: (i + nCb, j))
    sspec = pl.BlockSpec((bm, cw), lambda i, j: (i, j))
    return ew_call(
        "merge_fwd", (dm.nSb, D // cw), fn,
        [(P, pspec(dm.OFF_LG)), (P, pspec(dm.OFF_GA)), (P, pspec(dm.OFF_GL)), (attn, sspec), (hf, tspec), (hb, tspec)],
        [(_sds((S, D), BF), sspec, False)])[0]


def merge_bwd(dm, dmg, P, attn, hf, hb):
    S, T, D, bm, cw, nCb = dm.S, dm.T, dm.D, dm.bm, dm.cw, dm.nCb
    nj = D // cw

    def body(dm_ref, lg_ref, ga_ref, gl_ref, at_ref, hf_ref, hb_ref, dp_ref, da_ref, dh_ref, buf, sems):
        i, j = pl.program_id(0), pl.program_id(1)
        lat = i >= nCb
        d = jnp.where(lat, dm_ref[...].astype(F32), 0.0)
        lg = lg_ref[...]
        ge, th = _gelu(lg)
        hs = hf_ref[...] + hb_ref[...]
        sa, sl = _sig(ga_ref[...]), _sig(gl_ref[...])
        at = jnp.where(lat, at_ref[...], 0.0)
        dlru = d * sl
        buf[0] = (dlru * hs * _gelu_grad(lg, th)).astype(BF)
        buf[1] = (d * at * sa * (1.0 - sa)).astype(BF)
        buf[2] = (d * hs * ge * sl * (1.0 - sl)).astype(BF)
        da_ref[...] = (d * sa).astype(BF)
        dh_ref[...] = dlru * ge
        copies = []
        for g, off in enumerate((dm.OFF_LG, dm.OFF_GA, dm.OFF_GL)):
            col = pl.multiple_of(off + j * cw, 128)
            cp = pltpu.make_async_copy(buf.at[g], dp_ref.at[pl.ds(pl.multiple_of(i * bm, 8), bm), pl.ds(col, cw)],
                                       sems.at[g])
            cp.start()
            copies.append(cp)
        for cp in copies:
            cp.wait()

    pspec = lambda off: pl.BlockSpec((bm, cw), lambda i, j: (i, off // cw + j))
    tspec = pl.BlockSpec((bm, cw), lambda i, j: (i, j))
    lspec = pl.BlockSpec((bm, cw), lambda i, j: (jnp.maximum(i - nCb, 0), j))
    return pl.pallas_call(
        body, grid=(dm.nTb, nj), name="merge_bwd",
        in_specs=[lspec, pspec(dm.OFF_LG), pspec(dm.OFF_GA), pspec(dm.OFF_GL), lspec, tspec, tspec],
        out_specs=[ANY, tspec, tspec],
        out_shape=[_sds((T, dm.INW), BF), _sds((T, D), BF), _sds((T, D), F32)],
        scratch_shapes=[pltpu.VMEM((3, bm, cw), BF), pltpu.SemaphoreType.DMA((3,))],
        compiler_params=_cparams(),
    )(dmg, P, P, P, attn, hf, hb)


def final_loss(dm, x3, gfin, target):
    S, D, bm = dm.S, dm.D, dm.bm

    def fn(ids, x_ref, g_ref, t_ref):
        xv = x_ref[...]
        g = g_ref[...]
        r = lax.rsqrt(jnp.mean(xv * xv, axis=-1, keepdims=True) + EPS)
        xn = xv * r
        err = xn * g - t_ref[...]
        loss = 0.5 * jnp.sum(jnp.mean(err * err, axis=-1, keepdims=True), axis=0, keepdims=True)
        dy = err / D
        dxn = dy * g
        dx = r * (dxn - xn * jnp.mean(dxn * xn, axis=-1, keepdims=True))
        return [jnp.broadcast_to(loss, (1, 128)), dx, jnp.sum(dy * xn, axis=0, keepdims=True)]

    row = pl.BlockSpec((bm, D), lambda i: (i, 0))
    vec = pl.BlockSpec((1, D), lambda i: (0, 0))
    return ew_call(
        "final_loss", (dm.nSb,), fn, [(x3, row), (gfin, vec), (target, row)],
        [(_sds((1, 128), F32), pl.BlockSpec((1, 128), lambda i: (0, 0)), True), (_sds((S, D), F32), row, False),
         (_sds((1, D), F32), vec, True)], first=lambda ids: ids[0] == 0)


def local_step(dm, x, ctx, target, modv, norm_g3, gfin, gq, gk, conv_w, conv_b, wa, ba, wx, bx, lam,
               wg, wu, wd, w_in, w_out):
    S, C, T, D, NS, F4, W4 = dm.S, dm.C, dm.T, dm.D, dm.NS, dm.F4, dm.W4
    wb, nsub = dm.wb, dm.nsub
    xt = jnp.concatenate([ctx, x], axis=0)
    cosf, sinf = rope_tables(dm)
    sp = jax.nn.softplus(-lam)
    sg = jax.nn.sigmoid(-lam)

    h1 = normmod_fwd("nm1", dm, xt, norm_g3, 0, modv, True)
    xt1, a1, u1, s1, f1 = ffn_fwd("ffn1", dm, h1, xt, wg, wu, wd, 0, modv, 2, True)
    h2 = normmod_fwd("nm2", dm, xt1, norm_g3, 1, modv, True)
    mT = dm.mT
    ident = lambda ids, accs, ex: list(accs)
    P = fused_mm(
        "w_in", (T // mT, NS * nsub, 1),
        [(h2, pl.BlockSpec((mT, D), lambda i, j, k: (i, 0))),
         (w_in, pl.BlockSpec((None, D, wb), lambda i, j, k: (j // nsub, 0, j % nsub)))],
        [(0, 1, NN, 0)], [(mT, wb)], ident,
        [(_sds((T, dm.INW), F32), pl.BlockSpec((mT, wb), lambda i, j, k: (i, j)))])[0]
    qr, kr, vb = qk_prep(dm, P, gq, gk, cosf, sinf)
    attn = attention_fwd(dm, qr, kr, vb)
    hf = lru_fwd(dm, P, conv_w, conv_b, wa, ba, wx, bx, sp, 0)
    hb = lru_fwd(dm, P, conv_w, conv_b, wa, ba, wx, bx, sp, 1)
    mg = merge_fwd(dm, P, attn, hf, hb)
    mS = dm.mS
    bn = _pick(D, [1024, 512, 256, 128])
    bk = _pick(D, [512, 256, 128])

    def epi_o(ids, accs, ex):
        o = accs[0]
        return [ex[0][...] + ex[1][...] * o, o]

    x1_lat = xt1[C:]
    x2, o2 = fused_mm(
        "w_out", (S // mS, D // bn, D // bk),
        [(mg, pl.BlockSpec((mS, bk), lambda i, j, k: (i, k))), (w_out, pl.BlockSpec((bk, bn), lambda i, j, k: (k, j)))],
        [(0, 1, NN, 0)], [(mS, bn)], epi_o,
        [(_sds((S, D), F32), pl.BlockSpec((mS, bn), lambda i, j, k: (i, j))),
         (_sds((S, D), BF), pl.BlockSpec((mS, bn), lambda i, j, k: (i, j)))],
        extras=[(x1_lat, pl.BlockSpec((mS, bn), lambda i, j, k: (i, j))),
                (modv, pl.BlockSpec((None, None, 1, bn), lambda i, j, k: (1, 5, 0, j)))])
    h3 = normmod_fwd("nm3", dm, x2, norm_g3, 2, modv, False)
    x3, a3, u3, s3, f3 = ffn_fwd("ffn2", dm, h3, x2, wg, wu, wd, 1, modv, 8, False)
    loss, dx3, dgfin = final_loss(dm, x3, gfin, target)

    df3, dg3 = gate_bwd("gate3", dm, dx3, f3, modv, 8, FFN_RES, False)
    dh3, dwg1, dwu1, dwd1 = ffn_bwd("ffn2b", dm, df3, h3, a3, u3, s3, wg, wu, wd, 1, False)
    dx2, dsh3, dsc3, dgn3 = normmod_bwd("nm3b", dm, dh3, x2, dx3, norm_g3, 2, modv, False, False)
    do2, dg2 = gate_bwd("gate2", dm, dx2, o2, modv, 5, 1.0, False)
    dmg = fused_mm(
        "w_out_dx", (S // mS, D // bn, D // bk),
        [(do2, pl.BlockSpec((mS, bk), lambda i, j, k: (i, k))), (w_out, pl.BlockSpec((bn, bk), lambda i, j, k: (j, k)))],
        [(0, 1, NT, 0)], [(mS, bn)], ident,
        [(_sds((S, D), BF), pl.BlockSpec((mS, bn), lambda i, j, k: (i, j)))])[0]
    dw_out = fused_mm(
        "w_out_dw", (D // bn, D // bn, S // mS),
        [(mg, pl.BlockSpec((mS, bn), lambda i, j, k: (k, i))), (do2, pl.BlockSpec((mS, bn), lambda i, j, k: (k, j)))],
        [(0, 1, TN, 0)], [(bn, bn)], ident,
        [(_sds((D, D), BF), pl.BlockSpec((bn, bn), lambda i, j, k: (i, j)))])[0]
    dP, dattn, dhs = merge_bwd(dm, dmg, P, attn, hf, hb)
    dxc0, dwa0, dba0, dwx0, dbx0, dlam0 = lru_bwd(dm, P, dhs, hf, conv_w, conv_b, wa, ba, wx, bx, sp, sg, 0, None, None)
    dP, dwa1, dba1, dwx1, dbx1, dlam1, dcw, dcb = lru_bwd(dm, P, dhs, hb, conv_w, conv_b, wa, ba, wx, bx, sp, sg, 1,
                                                          dxc0, dP)
    dq, dk, dv = attention_bwd(dm, qr, kr, vb, dattn)
    dP, dgq, dgk = qk_prep_bwd(dm, dq, dk, dv, P, gq, gk, cosf, sinf, dP)
    nkb = NS * nsub
    dh2 = fused_mm(
        "w_in_dx", (T // mT, D // bn, nkb),
        [(dP, pl.BlockSpec((mT, wb), lambda i, j, k: (i, k))),
         (w_in, pl.BlockSpec((None, bn, wb), lambda i, j, k: (k // nsub, j, k % nsub)))],
        [(0, 1, NT, 0)], [(mT, bn)], ident,
        [(_sds((T, D), F32), pl.BlockSpec((mT, bn), lambda i, j, k: (i, j)))])[0]
    dw_in = fused_mm(
        "w_in_dw", (D // bn, nkb, T // mT),
        [(h2, pl.BlockSpec((mT, bn), lambda i, j, k: (k, i))), (dP, pl.BlockSpec((mT, wb), lambda i, j, k: (k, j)))],
        [(0, 1, TN, 0)], [(bn, wb)], ident,
        [(_sds((NS, D, W4), BF), pl.BlockSpec((None, bn, wb), lambda i, j, k: (j // nsub, i, j % nsub)))])[0]
    dxt1, dsh2, dsc2, dgn2 = normmod_bwd("nm2b", dm, dh2, xt1, dx2, norm_g3, 1, modv, True, True)
    df1, dg1 = gate_bwd("gate1", dm, dxt1, f1, modv, 2, FFN_RES, True)
    dh1, dwg0, dwu0, dwd0 = ffn_bwd("ffn1b", dm, df1, h1, a1, u1, s1, wg, wu, wd, 0, True)
    dxt, dsh1, dsc1, dgn1 = normmod_bwd("nm1b", dm, dh1, xt, dxt1, norm_g3, 0, modv, True, False)
    grad_x = dxt[C:]

    dmod = jnp.concatenate([dsh1, dsc1, dg1, dsh2, dsc2, _lat(dg2), _lat(dsh3), _lat(dsc3), _lat(dg3)], axis=1)
    dnorm = jnp.stack([dgn1[0, 0] + dgn1[1, 0], dgn2[0, 0] + dgn2[1, 0], dgn3[1, 0]], axis=0)
    small = dict(norm_g=dnorm, q_norm_g=dgq, k_norm_g=dgk, conv_w=dcw, conv_b=dcb,
                 lru_wa=jnp.stack([dwa0, dwa1]), lru_ba=jnp.concatenate([dba0, dba1], axis=0),
                 lru_wx=jnp.stack([dwx0, dwx1]), lru_bx=jnp.concatenate([dbx0, dbx1], axis=0),
                 lru_lambda=jnp.concatenate([dlam0, dlam1], axis=0), final_norm_g=dgfin)
    big = dict(ffn_wg=[dwg0, dwg1], ffn_wu=[dwu0, dwu1], ffn_wd=[dwd0, dwd1], w_in=dw_in, w_out=dw_out)
    return loss, grad_x, dmod, small, big


def _lat(v):
    return jnp.concatenate([jnp.zeros_like(v[:1]), v[1:]], axis=0)


def _me():
    return lax.axis_index("x"), lax.axis_index("y"), lax.axis_index("c")


def allgather8(name, v):
    def body(v_ref, out_ref, send_sems, recv_sems, local_sem):
        x, y, c = _me()
        me = 4 * x + 2 * y + c
        mine = pltpu.make_async_copy(v_ref, out_ref.at[me], local_sem)
        mine.start()
        copies = []
        for k in range(1, 8):
            peer = (x ^ ((k >> 2) & 1), y ^ ((k >> 1) & 1), c ^ (k & 1))
            cp = pltpu.make_async_remote_copy(src_ref=v_ref, dst_ref=out_ref.at[me], send_sem=send_sems.at[k - 1],
                                              recv_sem=recv_sems.at[k - 1], device_id=peer, device_id_type=MESH)
            cp.start()
            copies.append(cp)
        for k in range(1, 8):
            peer = (x ^ ((k >> 2) & 1), y ^ ((k >> 1) & 1), c ^ (k & 1))
            pltpu.make_async_remote_copy(src_ref=v_ref, dst_ref=out_ref.at[me ^ k], send_sem=send_sems.at[k - 1],
                                         recv_sem=recv_sems.at[k - 1], device_id=peer, device_id_type=MESH).wait_recv()
        for cp in copies:
            cp.wait_send()
        mine.wait()

    return pl.pallas_call(
        body, name=name, out_shape=_sds((8,) + v.shape, v.dtype), in_specs=[ANY], out_specs=ANY,
        scratch_shapes=[pltpu.SemaphoreType.DMA((7,)), pltpu.SemaphoreType.DMA((7,)), pltpu.SemaphoreType.DMA],
    )(v)


def _chips(x, y):
    chips = [(1 - x, y), (x, 1 - y), (1 - x, 1 - y)]
    return chips, [2 * cx + cy for cx, cy in chips]


def allgather_weights(tensors):
    n = len(tensors)

    def body(*refs):
        ins, outs = refs[:n], refs[n:2 * n]
        send_sems, recv_sems, local_sems = refs[2 * n:]
        x, y, c = _me()
        s = 2 * x + y
        sib = (x, y, 1 - c)
        chips, slots = _chips(x, y)
        locals_, sends, fwds = [], [], []
        for t in range(n):
            cp = pltpu.make_async_copy(ins[t], outs[t].at[s], local_sems.at[t])
            cp.start()
            locals_.append(cp)
        for t in range(n):
            H = ins[t].shape[0] // 2
            half = pl.ds(c * H, H)
            for j, chip in enumerate(chips):
                cp = pltpu.make_async_remote_copy(
                    src_ref=ins[t].at[half], dst_ref=outs[t].at[s, half], send_sem=send_sems.at[6 * t + j],
                    recv_sem=recv_sems.at[6 * t + j], device_id=(chip[0], chip[1], c), device_id_type=MESH)
                cp.start()
                sends.append(cp)
        for t in range(n):
            H = ins[t].shape[0] // 2
            half = pl.ds(c * H, H)
            for j, chip in enumerate(chips):
                landed = outs[t].at[slots[j], half]
                pltpu.make_async_remote_copy(
                    src_ref=ins[t].at[half], dst_ref=landed, send_sem=send_sems.at[6 * t + j],
                    recv_sem=recv_sems.at[6 * t + j], device_id=(chip[0], chip[1], c), device_id_type=MESH).wait_recv()
                fw = pltpu.make_async_remote_copy(
                    src_ref=landed, dst_ref=landed, send_sem=send_sems.at[6 * t + 3 + j],
                    recv_sem=recv_sems.at[6 * t + 3 + j], device_id=sib, device_id_type=MESH)
                fw.start()
                fwds.append(fw)
        for t in range(n):
            H = ins[t].shape[0] // 2
            other = pl.ds((1 - c) * H, H)
            for j in range(3):
                got = outs[t].at[slots[j], other]
                pltpu.make_async_remote_copy(
                    src_ref=got, dst_ref=got, send_sem=send_sems.at[6 * t + 3 + j],
                    recv_sem=recv_sems.at[6 * t + 3 + j], device_id=sib, device_id_type=MESH).wait_recv()
        for cp in sends + fwds:
            cp.wait_send()
        for cp in locals_:
            cp.wait()

    return pl.pallas_call(
        body, name="allgather_weights", out_shape=[_sds((4,) + t.shape, t.dtype) for t in tensors],
        in_specs=[ANY] * n, out_specs=[ANY] * n,
        scratch_shapes=[pltpu.SemaphoreType.DMA((6 * n,)), pltpu.SemaphoreType.DMA((6 * n,)),
                        pltpu.SemaphoreType.DMA((n,))],
    )(*tensors)


def rs_sibling_exchange(tensors):
    n = len(tensors)

    def body(*refs):
        ins, keeps, gots = refs[:n], refs[n:2 * n], refs[2 * n:3 * n]
        send_sems, recv_sems, local_sems = refs[3 * n:]
        x, y, c = _me()
        sib = (x, y, 1 - c)
        work = []
        for t in range(n):
            H = ins[t].shape[1] // 2
            lc = pltpu.make_async_copy(ins[t].at[:, pl.ds(c * H, H)], keeps[t], local_sems.at[t])
            lc.start()
            cp = pltpu.make_async_remote_copy(
                src_ref=ins[t].at[:, pl.ds((1 - c) * H, H)], dst_ref=gots[t], send_sem=send_sems.at[t],
                recv_sem=recv_sems.at[t], device_id=sib, device_id_type=MESH)
            cp.start()
            work.append((lc, cp))
        for lc, cp in work:
            cp.wait_recv()
        for lc, cp in work:
            cp.wait_send()
            lc.wait()

    half = lambda t: _sds((t.shape[0], t.shape[1] // 2) + t.shape[2:], t.dtype)
    res = pl.pallas_call(
        body, name="rs_sibling_exchange", out_shape=[half(t) for t in tensors] * 2,
        in_specs=[ANY] * n, out_specs=[ANY] * (2 * n),
        scratch_shapes=[pltpu.SemaphoreType.DMA((n,)), pltpu.SemaphoreType.DMA((n,)), pltpu.SemaphoreType.DMA((n,))],
    )(*tensors)
    return res[:n], res[n:]


def rs_chip_exchange(tensors):
    n = len(tensors)

    def body(*refs):
        ins, outs = refs[:n], refs[n:2 * n]
        send_sems, recv_sems, local_sems = refs[2 * n:]
        x, y, c = _me()
        s = 2 * x + y
        chips, slots = _chips(x, y)
        sends, locals_ = [], []
        for t in range(n):
            lc = pltpu.make_async_copy(ins[t].at[s], outs[t].at[s], local_sems.at[t])
            lc.start()
            locals_.append(lc)
            for j, chip in enumerate(chips):
                cp = pltpu.make_async_remote_copy(
                    src_ref=ins[t].at[slots[j]], dst_ref=outs[t].at[s], send_sem=send_sems.at[3 * t + j],
                    recv_sem=recv_sems.at[3 * t + j], device_id=(chip[0], chip[1], c), device_id_type=MESH)
                cp.start()
                sends.append(cp)
        for t in range(n):
            for j, chip in enumerate(chips):
                pltpu.make_async_remote_copy(
                    src_ref=ins[t].at[slots[j]], dst_ref=outs[t].at[slots[j]], send_sem=send_sems.at[3 * t + j],
                    recv_sem=recv_sems.at[3 * t + j], device_id=(chip[0], chip[1], c), device_id_type=MESH).wait_recv()
        for cp in sends:
            cp.wait_send()
        for lc in locals_:
            lc.wait()

    return pl.pallas_call(
        body, name="rs_chip_exchange", out_shape=[_sds(t.shape, t.dtype) for t in tensors],
        in_specs=[ANY] * n, out_specs=[ANY] * n,
        scratch_shapes=[pltpu.SemaphoreType.DMA((3 * n,)), pltpu.SemaphoreType.DMA((3 * n,)),
                        pltpu.SemaphoreType.DMA((n,))],
    )(*tensors)


def rs_sibling_share(tensors, groups, out_shapes):
    n, m = len(tensors), len(out_shapes)

    def body(*refs):
        ins, outs = refs[:n], refs[n:n + m]
        send_sems, recv_sems, local_sems = refs[n + m:]
        x, y, c = _me()
        sib = (x, y, 1 - c)
        work = []
        for t in range(n):
            H = ins[t].shape[0]
            oi, li = groups[t]
            dst = outs[oi] if li is None else outs[oi].at[li]
            lc = pltpu.make_async_copy(ins[t], dst.at[pl.ds(c * H, H)], local_sems.at[t])
            lc.start()
            cp = pltpu.make_async_remote_copy(
                src_ref=ins[t], dst_ref=dst.at[pl.ds(c * H, H)], send_sem=send_sems.at[t], recv_sem=recv_sems.at[t],
                device_id=sib, device_id_type=MESH)
            cp.start()
            work.append((lc, cp, dst, H))
        for t, (lc, cp, dst, H) in enumerate(work):
            pltpu.make_async_remote_copy(
                src_ref=ins[t], dst_ref=dst.at[pl.ds((1 - c) * H, H)], send_sem=send_sems.at[t],
                recv_sem=recv_sems.at[t], device_id=sib, device_id_type=MESH).wait_recv()
        for lc, cp, _, _ in work:
            cp.wait_send()
            lc.wait()

    return pl.pallas_call(
        body, name="rs_sibling_share", out_shape=[_sds(s, F32) for s in out_shapes],
        in_specs=[ANY] * n, out_specs=[ANY] * m,
        scratch_shapes=[pltpu.SemaphoreType.DMA((n,)), pltpu.SemaphoreType.DMA((n,)), pltpu.SemaphoreType.DMA((n,))],
    )(*tensors)


def _rows_block(rows, cols, nbytes=1 << 20):
    bm = 8
    while bm * 2 * cols * 4 <= nbytes and rows % (bm * 2) == 0:
        bm *= 2
    return bm


def cast_bf16(name, w):
    shape = w.shape
    w2 = w.reshape(-1, shape[-1])
    bm = _rows_block(*w2.shape)
    spec = pl.BlockSpec((bm, w2.shape[1]), lambda i: (i, 0))
    out = ew_call(name, (w2.shape[0] // bm,), lambda ids, r: [r[...]], [(w2, spec)], [(_sds(w2.shape, BF), spec, False)])[0]
    return out.reshape(shape)


def add_pair(name, a, b):
    shape = a.shape
    a2, b2 = a.reshape(-1, shape[-1]), b.reshape(-1, shape[-1])
    bm = _rows_block(*a2.shape)
    spec = pl.BlockSpec((bm, a2.shape[1]), lambda i: (i, 0))
    out = ew_call(name, (a2.shape[0] // bm,), lambda ids, p, q: [p[...].astype(F32) + q[...].astype(F32)],
                  [(a2, spec), (b2, spec)], [(_sds(a2.shape, BF), spec, False)])[0]
    return out.reshape(shape)


def sum_slots(name, a, out_dtype=F32):
    K, H, W = a.shape
    bm = _rows_block(H, W * K // 2)

    def fn(ids, r):
        acc = r[0].astype(F32)
        for k in range(1, K):
            acc = acc + r[k].astype(F32)
        return [acc]

    return ew_call(name, (H // bm,), fn, [(a, pl.BlockSpec((K, bm, W), lambda i: (0, i, 0)))],
                   [(_sds((H, W), out_dtype), pl.BlockSpec((bm, W), lambda i: (i, 0)), False)])[0]


def adamw(name, w, g, m, v):
    shape = w.shape
    flat = lambda t: t.reshape(-1, shape[-1])
    w2, g2, m2, v2 = flat(w), flat(g), flat(m), flat(v)
    bm = _rows_block(w2.shape[0], w2.shape[1] * 2)
    bc1 = 1.0 - ADAM_B1 ** ADAM_STEP
    bc2 = 1.0 - ADAM_B2 ** ADAM_STEP

    def fn(ids, w_ref, g_ref, m_ref, v_ref):
        gv = g_ref[...]
        mn = ADAM_B1 * m_ref[...] + (1.0 - ADAM_B1) * gv
        vn = ADAM_B2 * v_ref[...] + (1.0 - ADAM_B2) * (gv * gv)
        m_hat = mn / bc1
        v_hat = vn / bc2
        delta = -ADAM_LR * (m_hat / (jnp.sqrt(v_hat) + ADAM_EPS) + ADAM_WD * w_ref[...])
        return [delta, mn, vn]

    spec = pl.BlockSpec((bm, w2.shape[1]), lambda i: (i, 0))
    outs = ew_call(name, (w2.shape[0] // bm,), fn, [(w2, spec), (g2, spec), (m2, spec), (v2, spec)],
                   [(_sds(w2.shape, F32), spec, False)] * 3)
    return [o.reshape(shape) for o in outs]


def dmod_pack(gd):
    N = gd.shape[-1]
    bn = _pick(N, [4608, 2304, 1152, 1024, 512, 256, 128])

    def fn(ids, r):
        lat = [r[d, 1:2, :] for d in range(8)]
        cs = r[0, 0:1, :]
        for d in range(1, 8):
            cs = cs + r[d, 0:1, :]
        tot = cs
        for d in range(8):
            tot = tot + lat[d]
        return [jnp.concatenate(lat + [cs, jnp.zeros((7, bn), F32)], axis=0), tot]

    return ew_call("dmod_pack", (N // bn,), fn, [(gd, pl.BlockSpec((8, 2, bn), lambda j: (0, 0, j)))],
                   [(_sds((16, N), F32), pl.BlockSpec((16, bn), lambda j: (0, j)), False),
                    (_sds((1, N), F32), pl.BlockSpec((1, bn), lambda j: (0, j)), False)])


def _silu(v):
    return v * _sig(v)


def kernel(x, c, ctx, c_ctx, w_mod, b_mod, norm_g, ffn_wg, ffn_wu, ffn_wd, w_in, w_out, q_norm_g, k_norm_g, conv_w, conv_b, lru_wa, lru_ba, lru_wx, lru_bx, lru_lambda, final_norm_g, loss_target, m_c_ctx, m_w_mod, m_b_mod, m_norm_g, m_ffn_wg, m_ffn_wu, m_ffn_wd, m_w_in, m_w_out, m_q_norm_g, m_k_norm_g, m_conv_w, m_conv_b, m_lru_wa, m_lru_ba, m_lru_wx, m_lru_bx, m_lru_lambda, m_final_norm_g, v_c_ctx, v_w_mod, v_b_mod, v_norm_g, v_ffn_wg, v_ffn_wu, v_ffn_wd, v_w_in, v_w_out, v_q_norm_g, v_k_norm_g, v_conv_w, v_conv_b, v_lru_wa, v_lru_ba, v_lru_wx, v_lru_bx, v_lru_lambda, v_final_norm_g):
    given = dict(locals())
    names = ['c_ctx', 'w_mod', 'b_mod', 'norm_g', 'ffn_wg', 'ffn_wu', 'ffn_wd', 'w_in', 'w_out', 'q_norm_g', 'k_norm_g',
             'conv_w', 'conv_b', 'lru_wa', 'lru_ba', 'lru_wx', 'lru_bx', 'lru_lambda', 'final_norm_g']
    S, D = x.shape[1], x.shape[2]
    C = ctx.shape[1]
    NS = 4
    F4, W4, LB = ffn_wg.shape[-1], w_in.shape[-1], lru_wa.shape[2]
    dm = Dims(S, C, D, F4, W4, NS, LB)
    Ds = D // NS
    Wm = w_mod.shape[-1]
    xi, yi, ci = lax.axis_index("x"), lax.axis_index("y"), lax.axis_index("c")
    slot = 2 * xi + yi
    me = 4 * xi + 2 * yi + ci
    ident = lambda ids, accs, ex: list(accs)

    pack1 = jnp.concatenate([c.reshape(-1), norm_g.reshape(-1), conv_w.reshape(-1), lru_ba.reshape(-1),
                             lru_bx.reshape(-1), lru_lambda.reshape(-1)]).reshape(1, -1)
    g1 = allgather8("ag_small_params", pack1)[:, 0]
    c_all = g1[:, :D]

    def unshard(off, k):
        part = g1[0::2, off:off + k * Ds].reshape(NS, k, Ds)
        return jnp.transpose(part, (1, 0, 2)).reshape(k, D)

    norm_g_f = unshard(D, 3)
    conv_w_f = unshard(D + 3 * Ds, 4)
    ba_f = unshard(D + 7 * Ds, 2)
    bx_f = unshard(D + 9 * Ds, 2)
    lam_f = unshard(D + 11 * Ds, 2)

    call16 = jnp.concatenate([c_all, c_ctx.reshape(1, D), jnp.zeros((7, D), F32)], axis=0)
    b_cols = lax.dynamic_slice(b_mod, (0, slot * Wm), (1, Wm))
    bnm = _pick(Wm, [1536, 1152, 768, 512, 384, 256, 128])
    bkm = _pick(D, [512, 256, 128])
    modp = fused_mm(
        "mod_fwd", (1, Wm // bnm, D // bkm),
        [(call16, pl.BlockSpec((16, bkm), lambda i, j, k: (0, k))),
         (w_mod[0], pl.BlockSpec((bkm, bnm), lambda i, j, k: (k, j)))],
        [(0, 1, NN, 0)], [(16, bnm)], lambda ids, accs, ex: [accs[0] + ex[0][...]],
        [(_sds((16, Wm), F32), pl.BlockSpec((16, bnm), lambda i, j, k: (0, j)))],
        extras=[(b_cols, pl.BlockSpec((1, bnm), lambda i, j, k: (0, j)))], pre={0: _silu})[0]
    gm = allgather8("ag_mod", modp)
    mod_full = jnp.concatenate([gm[0], gm[2], gm[4], gm[6]], axis=1)
    mod_x = lax.dynamic_index_in_dim(mod_full, me, axis=0, keepdims=False)
    modv = jnp.stack([mod_full[8], mod_x]).reshape(2, N_MOD, 1, D)

    wg_b = cast_bf16("cast_wg", ffn_wg[0])
    wu_b = cast_bf16("cast_wu", ffn_wu[0])
    wd_b = cast_bf16("cast_wd", ffn_wd[0])
    win_b = cast_bf16("cast_w_in", w_in[0])
    wout_b = cast_bf16("cast_w_out", w_out[0])
    wgG, wuG, wdG, winG, woutG = allgather_weights([wg_b, wu_b, wd_b, win_b, wout_b])

    loss_l, grad_x, dmod, small, big = local_step(
        dm, x[0], ctx[0], loss_target[0], modv, norm_g_f.reshape(3, 1, D), final_norm_g.reshape(1, D),
        q_norm_g, k_norm_g, conv_w_f, conv_b, lru_wa[0], ba_f.reshape(2, 1, D), lru_wx[0], bx_f.reshape(2, 1, D),
        lam_f.reshape(2, 1, D), wgG, wuG, wdG, winG, woutG.reshape(D, D))
    loss = lax.psum(loss_l[0, 0], ("x", "y", "c"))

    tensors = big['ffn_wg'] + big['ffn_wu'] + big['ffn_wd'] + [big['w_in'], big['w_out'].reshape(NS, Ds, D)]
    keeps, gots = rs_sibling_exchange(tensors)
    partial = [add_pair("rs_add%d" % t, k_, g_) for t, (k_, g_) in enumerate(zip(keeps, gots))]
    landed = rs_chip_exchange(partial)
    halves = [sum_slots("rs_sum%d" % t, r) for t, r in enumerate(landed)]
    groups = [(0, 0), (0, 1), (1, 0), (1, 1), (2, 0), (2, 1), (3, None), (4, None)]
    g_wg, g_wu, g_wd, g_win, g_wout = rs_sibling_share(
        halves, groups, [(2, D, F4), (2, D, F4), (2, F4, D), (D, W4), (Ds, D)])
    grads = dict(ffn_wg=g_wg[None], ffn_wu=g_wu[None], ffn_wd=g_wd[None], w_in=g_win[None], w_out=g_wout[None])

    gd = allgather8("ag_dmod", dmod.reshape(2, N_MOD * D))
    dM, g_bmod = dmod_pack(gd)
    dMc = lax.dynamic_slice(dM, (0, slot * Wm), (16, Wm))
    bmm = _pick(D, [512, 256, 128])
    grads['w_mod'] = fused_mm(
        "w_mod_dw", (D // bmm, Wm // bnm, 1),
        [(call16, pl.BlockSpec((16, bmm), lambda i, j, k: (0, i))), (dMc, pl.BlockSpec((16, bnm), lambda i, j, k: (0, j)))],
        [(0, 1, TN, 0)], [(bmm, bnm)], ident,
        [(_sds((D, Wm), F32), pl.BlockSpec((bmm, bnm), lambda i, j, k: (i, j)))], pre={0: _silu})[0][None]
    grads['b_mod'] = g_bmod

    def epi_cc(ids, accs, ex):
        v = ex[0][...]
        sg = _sig(v)
        return [accs[0] * (sg * (1.0 + v * (1.0 - sg)))]

    pcc = fused_mm(
        "c_ctx_partial", (1, D // bmm, Wm // bnm),
        [(dMc, pl.BlockSpec((16, bnm), lambda i, j, k: (0, k))), (w_mod[0], pl.BlockSpec((bmm, bnm), lambda i, j, k: (j, k)))],
        [(0, 1, NT, 0)], [(16, bmm)], epi_cc,
        [(_sds((16, D), F32), pl.BlockSpec((16, bmm), lambda i, j, k: (0, j)))],
        extras=[(c_ctx.reshape(1, D), pl.BlockSpec((1, bmm), lambda i, j, k: (0, j)))])[0]
    pcc_row = jnp.where(ci == 0, pcc[8], 0.0)

    order = ['lru_wa', 'lru_wx', 'q_norm_g', 'k_norm_g', 'conv_b', 'final_norm_g', 'norm_g', 'conv_w', 'lru_ba',
             'lru_bx', 'lru_lambda']
    flat = [small[k].reshape(-1) for k in order] + [pcc_row]
    sizes = [f.shape[0] for f in flat]
    tot = sum(sizes)
    LW = 1024
    padded = -(-tot // (8 * LW)) * (8 * LW)
    vec = jnp.concatenate(flat + [jnp.zeros((padded - tot,), F32)]).reshape(-1, LW)
    summed = sum_slots("small_sum", allgather8("ag_small_grads", vec)).reshape(-1)
    offs = {}
    o = 0
    for k, n_ in zip(order + ['c_ctx'], sizes):
        offs[k] = summed[o:o + n_]
        o += n_
    shard = lambda k, rows: lax.dynamic_slice_in_dim(offs[k].reshape(rows, D), slot * Ds, Ds, axis=1)
    grads['c_ctx'] = offs['c_ctx']
    grads['q_norm_g'] = offs['q_norm_g'].reshape(1, HEAD_DIM)
    grads['k_norm_g'] = offs['k_norm_g'].reshape(1, HEAD_DIM)
    grads['conv_b'] = offs['conv_b'].reshape(1, D)
    grads['final_norm_g'] = offs['final_norm_g']
    grads['lru_wa'] = offs['lru_wa'].reshape(lru_wa.shape)
    grads['lru_wx'] = offs['lru_wx'].reshape(lru_wx.shape)
    grads['norm_g'] = shard('norm_g', 3)[None]
    grads['conv_w'] = shard('conv_w', 4)[None]
    grads['lru_ba'] = shard('lru_ba', 2)[None]
    grads['lru_bx'] = shard('lru_bx', 2)[None]
    grads['lru_lambda'] = shard('lru_lambda', 2)[None]

    big_names = ['w_mod', 'ffn_wg', 'ffn_wu', 'ffn_wd', 'w_in', 'w_out']
    delta, new_m, new_v = {}, {}, {}
    for k in big_names:
        delta[k], new_m[k], new_v[k] = adamw("adamw_" + k, given[k], grads[k], given['m_' + k], given['v_' + k])
    small_names = [k for k in names if k not in big_names]
    ssz = [given[k].size for k in small_names]
    stot = sum(ssz)
    spad = -(-stot // (8 * LW)) * (8 * LW)

    def packed(get):
        return jnp.concatenate([get(k).reshape(-1) for k in small_names] + [jnp.zeros((spad - stot,), F32)]).reshape(-1, LW)

    pw, pg, pm = packed(lambda k: given[k]), packed(lambda k: grads[k]), packed(lambda k: given['m_' + k])
    pv = jnp.concatenate([given['v_' + k].reshape(-1) for k in small_names] + [jnp.ones((spad - stot,), F32)]).reshape(-1, LW)
    sd, sm, sv = [t.reshape(-1) for t in adamw("adamw_small", pw, pg, pm, pv)]
    o = 0
    for k, n_ in zip(small_names, ssz):
        shp = given[k].shape
        delta[k], new_m[k], new_v[k] = sd[o:o + n_].reshape(shp), sm[o:o + n_].reshape(shp), sv[o:o + n_].reshape(shp)
        o += n_

    return (loss, grad_x[None], *[grads[k] for k in names], *[delta[k] for k in names],
            *[new_m[k] for k in names], *[new_v[k] for k in names])
```

```python
import functools

import jax
import jax.numpy as jnp
from jax import lax
from jax.experimental import pallas as pl
from jax.experimental.pallas import tpu as pltpu

F32 = jnp.float32
BF = jnp.bfloat16
EPS = 1e-6
HEAD_DIM = 128
GRID_W = 64
ROPE_THETA = 10000.0
LRU_C = 8.0
FFN_RES = 0.5
N_MOD = 9
ADAM_LR, ADAM_B1, ADAM_B2, ADAM_EPS, ADAM_WD, ADAM_STEP = 0.001, 0.9, 0.999, 1e-08, 0.01, 10
VMEM_LIMIT = 52 * 1024 * 1024
MESH = pl.DeviceIdType.MESH
ANY = pl.BlockSpec(memory_space=pl.ANY)


def _sds(shape, dt):
    return jax.ShapeDtypeStruct(tuple(shape), dt)


def _pick(n, cands):
    for c in cands:
        if n % c == 0:
            return c
    return n


def _cparams(**kw):
    return pltpu.CompilerParams(vmem_limit_bytes=VMEM_LIMIT, **kw)


def _sig(x):
    return 1.0 / (1.0 + jnp.exp(-x))


def _gelu(x):
    t = jnp.tanh(0.7978845608028654 * (x + 0.044715 * x * x * x))
    return 0.5 * x * (1.0 + t), t


def _gelu_grad(x, t):
    return 0.5 * (1.0 + t) + 0.5 * x * (1.0 - t * t) * 0.7978845608028654 * (1.0 + 3.0 * 0.044715 * x * x)


def ew_call(name, grid, fn, ins, outs, first=None, aliases=None):
    n_in = len(ins)

    def body(*refs):
        ids = tuple(pl.program_id(a) for a in range(len(grid)))
        vals = fn(ids, *refs[:n_in])
        for (_, _, acc), o_ref, v in zip(outs, refs[n_in:], vals):
            if v is None:
                continue
            if not acc:
                o_ref[...] = v.astype(o_ref.dtype)
            else:
                is_first = first(ids)

                @pl.when(is_first)
                def _(o_ref=o_ref, v=v):
                    o_ref[...] = v.astype(o_ref.dtype)

                @pl.when(jnp.logical_not(is_first))
                def _(o_ref=o_ref, v=v):
                    o_ref[...] += v.astype(o_ref.dtype)

    res = pl.pallas_call(
        body, grid=grid, name=name,
        in_specs=[s for _, s in ins], out_specs=[s for _, s, _ in outs], out_shape=[o for o, _, _ in outs],
        input_output_aliases=aliases or {}, compiler_params=_cparams(),
    )(*[a for a, _ in ins])
    return res


def fused_mm(name, grid, ins, prods, acc_shapes, epi, outs, extras=(), pre=None):
    n_in, n_ex, n_out = len(ins), len(extras), len(outs)
    nk = grid[-1]
    pre = pre or {}

    def body(*refs):
        in_refs = refs[:n_in]
        ex_refs = refs[n_in:n_in + n_ex]
        out_refs = refs[n_in + n_ex:n_in + n_ex + n_out]
        accs = refs[n_in + n_ex + n_out:]
        ids = tuple(pl.program_id(a) for a in range(len(grid)))
        k = ids[-1]

        @pl.when(k == 0)
        def _():
            for a in accs:
                a[...] = jnp.zeros(a.shape, F32)

        loaded = {}

        def operand(i):
            if i not in loaded:
                v = in_refs[i][...]
                if i in pre:
                    v = pre[i](v)
                loaded[i] = v.astype(BF)
            return loaded[i]

        for ia, ib, dims, ai in prods:
            accs[ai][...] += lax.dot_general(operand(ia), operand(ib), (dims, ((), ())), preferred_element_type=F32)

        @pl.when(k == nk - 1)
        def _():
            vals = epi(ids, [a[...] for a in accs], ex_refs)
            for o_ref, v in zip(out_refs, vals):
                o_ref[...] = v.astype(o_ref.dtype)

    return pl.pallas_call(
        body, grid=grid, name=name,
        in_specs=[s for _, s in ins] + [s for _, s in extras],
        out_specs=[s for _, s in outs], out_shape=[o for o, _ in outs],
        scratch_shapes=[pltpu.VMEM(s, F32) for s in acc_shapes], compiler_params=_cparams(),
    )(*[a for a, _ in ins], *[a for a, _ in extras])


NN = ((1,), (0,))
NT = ((1,), (1,))
TN = ((0,), (0,))


class Dims:
    def __init__(self, S, C, D, F4, W4, NS, LB):
        self.S, self.C, self.D, self.F4, self.W4, self.NS, self.LB = S, C, D, F4, W4, NS, LB
        self.T = S + C
        self.DFF = F4 * NS
        self.INW = W4 * NS
        self.NQ = D // HEAD_DIM
        self.KVW = (self.INW - 5 * D) // 2
        self.NKV = self.KVW // HEAD_DIM
        self.G = self.NQ // self.NKV
        self.OFF_K = D
        self.OFF_V = D + self.KVW
        self.OFF_LX = D + 2 * self.KVW
        self.OFF_LG = self.OFF_LX + D
        self.OFF_GA = self.OFF_LG + D
        self.OFF_GL = self.OFF_GA + D
        self.bm = _pick(C, [256, 128, 64, 32, 16, 8])
        self.nCb = C // self.bm
        self.nTb = self.T // self.bm
        self.nSb = S // self.bm
        self.mT = _pick(self.T, [544, 512, 384, 256, 128])
        self.mS = _pick(S, [512, 256, 128])
        self.cw = _pick(D, [1024, 512, 256, 128]) if (self.OFF_LX % 1024 == 0 and D % 1024 == 0) else _pick(
            self.OFF_LX, [512, 256, 128])
        self.nsub = 2 if (W4 % 256 == 0 and W4 >= 512) else 1
        self.wb = W4 // self.nsub
        self.LBD = D // LB
        self.bq = _pick(C, [256, 128]) if S % _pick(C, [256, 128]) == 0 else 128


def rope_tables(dm):
    rows = dm.S // GRID_W
    row = jnp.repeat(jnp.arange(rows, dtype=F32), GRID_W)
    col = jnp.tile(jnp.arange(GRID_W, dtype=F32), rows)
    axis_dims = HEAD_DIM // 2
    freqs = ROPE_THETA ** (-jnp.arange(0, axis_dims, 2, dtype=F32) / axis_dims)
    ang = jnp.concatenate([row[:, None] * freqs, col[:, None] * freqs], axis=-1)
    cos = jnp.repeat(jnp.cos(ang), 2, axis=-1)
    sin = jnp.repeat(jnp.sin(ang), 2, axis=-1)
    sign = jnp.tile(jnp.array([-1.0, 1.0], F32), HEAD_DIM // 2)
    sin = sin * sign
    cos = jnp.concatenate([jnp.ones((dm.C, HEAD_DIM), F32), cos], axis=0)
    sin = jnp.concatenate([jnp.zeros((dm.C, HEAD_DIM), F32), sin], axis=0)
    return cos, sin


def _pair_swap(y):
    lane = lax.broadcasted_iota(jnp.int32, y.shape, 1)
    nxt = pltpu.roll(y, y.shape[1] - 1, 1)
    prv = pltpu.roll(y, 1, 1)
    return jnp.where((lane & 1) == 0, nxt, prv)


def normmod_fwd(name, dm, x, norm_g3, stage, modv, rows_T):
    D, bm = dm.D, dm.bm
    nb = dm.nTb if rows_T else dm.nSb
    typ = (lambda i: jnp.where(i < dm.nCb, 0, 1)) if rows_T else (lambda i: 1)

    def fn(ids, x_ref, g_ref, sh_ref, sc_ref):
        xv = x_ref[...]
        r = lax.rsqrt(jnp.mean(xv * xv, axis=-1, keepdims=True) + EPS)
        n = xv * r * g_ref[...]
        return [n * (1.0 + sc_ref[...]) + sh_ref[...]]

    return ew_call(
        name, (nb,), fn,
        [(x, pl.BlockSpec((bm, D), lambda i: (i, 0))),
         (norm_g3, pl.BlockSpec((None, 1, D), lambda i: (stage, 0, 0))),
         (modv, pl.BlockSpec((None, None, 1, D), lambda i: (typ(i), 3 * stage, 0, 0))),
         (modv, pl.BlockSpec((None, None, 1, D), lambda i: (typ(i), 3 * stage + 1, 0, 0)))],
        [(_sds(x.shape, BF), pl.BlockSpec((bm, D), lambda i: (i, 0)), False)])[0]


def normmod_bwd(name, dm, dh, x, dres, norm_g3, stage, modv, rows_T, dres_lat_only):
    D, bm = dm.D, dm.bm
    nb = dm.nTb if rows_T else dm.nSb
    nCb = dm.nCb
    typ = (lambda i: jnp.where(i < nCb, 0, 1)) if rows_T else (lambda i: 1)
    if dres_lat_only:
        dres_map = lambda i: (jnp.maximum(i - nCb, 0), 0)
    else:
        dres_map = lambda i: (i, 0)

    def fn(ids, dh_ref, x_ref, dres_ref, g_ref, sc_ref):
        i = ids[0]
        xv = x_ref[...]
        dhv = dh_ref[...].astype(F32)
        r = lax.rsqrt(jnp.mean(xv * xv, axis=-1, keepdims=True) + EPS)
        xn = xv * r
        g = g_ref[...]
        n = xn * g
        dn = dhv * (1.0 + sc_ref[...])
        dxn = dn * g
        dx = r * (dxn - xn * jnp.mean(dxn * xn, axis=-1, keepdims=True))
        dresv = dres_ref[...]
        if dres_lat_only:
            dresv = jnp.where(i >= nCb, dresv, 0.0)
        dsh = jnp.sum(dhv, axis=0, keepdims=True)
        dsc = jnp.sum(dhv * n, axis=0, keepdims=True)
        dg = jnp.sum(dn * xn, axis=0, keepdims=True)
        return [dx + dresv, dsh, dsc, dg]

    if rows_T:
        first = lambda ids: (ids[0] == 0) | (ids[0] == nCb)
    else:
        first = lambda ids: ids[0] == 0
    acc = (_sds((2, 1, D), F32), pl.BlockSpec((None, 1, D), lambda i: (typ(i), 0, 0)), True)
    return ew_call(
        name, (nb,), fn,
        [(dh, pl.BlockSpec((bm, D), lambda i: (i, 0))),
         (x, pl.BlockSpec((bm, D), lambda i: (i, 0))),
         (dres, pl.BlockSpec((bm, D), dres_map)),
         (norm_g3, pl.BlockSpec((None, 1, D), lambda i: (stage, 0, 0))),
         (modv, pl.BlockSpec((None, None, 1, D), lambda i: (typ(i), 3 * stage + 1, 0, 0)))],
        [(_sds(x.shape, F32), pl.BlockSpec((bm, D), lambda i: (i, 0)), False), acc, acc, acc], first=first)


def gate_bwd(name, dm, dx, f, modv, gidx, scale, rows_T):
    D, bm = dm.D, dm.bm
    nb = dm.nTb if rows_T else dm.nSb
    nCb = dm.nCb
    typ = (lambda i: jnp.where(i < nCb, 0, 1)) if rows_T else (lambda i: 1)

    def fn(ids, dx_ref, f_ref, g_ref):
        dxv = dx_ref[...]
        return [scale * g_ref[...] * dxv, jnp.sum(scale * f_ref[...].astype(F32) * dxv, axis=0, keepdims=True)]

    if rows_T:
        first = lambda ids: (ids[0] == 0) | (ids[0] == nCb)
    else:
        first = lambda ids: ids[0] == 0
    return ew_call(
        name, (nb,), fn,
        [(dx, pl.BlockSpec((bm, D), lambda i: (i, 0))),
         (f, pl.BlockSpec((bm, D), lambda i: (i, 0))),
         (modv, pl.BlockSpec((None, None, 1, D), lambda i: (typ(i), gidx, 0, 0)))],
        [(_sds(dx.shape, BF), pl.BlockSpec((bm, D), lambda i: (i, 0)), False),
         (_sds((2, 1, D), F32), pl.BlockSpec((None, 1, D), lambda i: (typ(i), 0, 0)), True)], first=first)


def ffn_fwd(name, dm, h, xres, wg, wu, wd, layer, modv, gidx, rows_T):
    D, F4, NS = dm.D, dm.F4, dm.NS
    M = h.shape[0]
    bm = dm.mT if rows_T else dm.mS
    bk = _pick(D, [512, 256, 128])
    C = dm.C

    def epi_up(ids, accs, ex):
        a, u = accs
        return [a, u, a * _sig(a) * u]

    hspec = pl.BlockSpec((bm, bk), lambda i, j, k: (i, k))
    wspec = pl.BlockSpec((None, None, bk, F4), lambda i, j, k: (j, layer, k, 0))
    ospec = pl.BlockSpec((bm, F4), lambda i, j, k: (i, j))
    a, u, s = fused_mm(
        name + "_up", (M // bm, NS, D // bk), [(h, hspec), (wg, wspec), (wu, wspec)],
        [(0, 1, NN, 0), (0, 2, NN, 1)], [(bm, F4), (bm, F4)], epi_up,
        [(_sds((M, dm.DFF), BF), ospec)] * 3)

    bn = _pick(D, [1024, 512, 256, 128])

    def epi_dn(ids, accs, ex):
        f = accs[0]
        if rows_T:
            row = ids[0] * bm + lax.broadcasted_iota(jnp.int32, (bm, 1), 0)
            gate = jnp.where(row < C, ex[1][...], ex[2][...])
        else:
            gate = ex[2][...]
        return [ex[0][...] + FFN_RES * gate * f, f]

    gspec = lambda t: pl.BlockSpec((None, None, 1, bn), lambda i, j, k: (t, gidx, 0, j))
    xo, f = fused_mm(
        name + "_down", (M // bm, D // bn, NS),
        [(s, pl.BlockSpec((bm, F4), lambda i, j, k: (i, k))),
         (wd, pl.BlockSpec((None, None, F4, bn), lambda i, j, k: (k, layer, 0, j)))],
        [(0, 1, NN, 0)], [(bm, bn)], epi_dn,
        [(_sds((M, D), F32), pl.BlockSpec((bm, bn), lambda i, j, k: (i, j))),
         (_sds((M, D), BF), pl.BlockSpec((bm, bn), lambda i, j, k: (i, j)))],
        extras=[(xres, pl.BlockSpec((bm, bn), lambda i, j, k: (i, j))), (modv, gspec(0)), (modv, gspec(1))])
    return xo, a, u, s, f


def ffn_bwd(name, dm, df, h, a, u, s, wg, wu, wd, layer, rows_T):
    D, F4, NS = dm.D, dm.F4, dm.NS
    M = h.shape[0]
    bm = dm.mT if rows_T else dm.mS
    bk = _pick(D, [512, 256, 128])

    def epi_ds(ids, accs, ex):
        ds = accs[0]
        av = ex[0][...].astype(F32)
        uv = ex[1][...].astype(F32)
        sg = _sig(av)
        return [ds * uv * (sg * (1.0 + av * (1.0 - sg))), ds * av * sg]

    ospec = pl.BlockSpec((bm, F4), lambda i, j, k: (i, j))
    da, du = fused_mm(
        name + "_ds", (M // bm, NS, D // bk),
        [(df, pl.BlockSpec((bm, bk), lambda i, j, k: (i, k))),
         (wd, pl.BlockSpec((None, None, F4, bk), lambda i, j, k: (j, layer, 0, k)))],
        [(0, 1, NT, 0)], [(bm, F4)], epi_ds, [(_sds((M, dm.DFF), BF), ospec)] * 2,
        extras=[(a, ospec), (u, ospec)])

    ident = lambda ids, accs, ex: list(accs)
    bn = _pick(D, [1024, 512, 256, 128])
    dwd = fused_mm(
        name + "_dwd", (NS, D // bn, M // bm),
        [(s, pl.BlockSpec((bm, F4), lambda i, j, k: (k, i))),
         (df, pl.BlockSpec((bm, bn), lambda i, j, k: (k, j)))],
        [(0, 1, TN, 0)], [(F4, bn)], ident,
        [(_sds((NS, F4, D), BF), pl.BlockSpec((None, F4, bn), lambda i, j, k: (i, 0, j)))])[0]

    dwg, dwu = fused_mm(
        name + "_dwgu", (D // bn, NS, M // bm),
        [(h, pl.BlockSpec((bm, bn), lambda i, j, k: (k, i))),
         (da, pl.BlockSpec((bm, F4), lambda i, j, k: (k, j))),
         (du, pl.BlockSpec((bm, F4), lambda i, j, k: (k, j)))],
        [(0, 1, TN, 0), (0, 2, TN, 1)], [(bn, F4), (bn, F4)], ident,
        [(_sds((NS, D, F4), BF), pl.BlockSpec((None, bn, F4), lambda i, j, k: (j, i, 0)))] * 2)

    dh = fused_mm(
        name + "_dh", (M // bm, D // bn, NS),
        [(da, pl.BlockSpec((bm, F4), lambda i, j, k: (i, k))),
         (wg, pl.BlockSpec((None, None, bn, F4), lambda i, j, k: (k, layer, j, 0))),
         (du, pl.BlockSpec((bm, F4), lambda i, j, k: (i, k))),
         (wu, pl.BlockSpec((None, None, bn, F4), lambda i, j, k: (k, layer, j, 0)))],
        [(0, 1, NT, 0), (2, 3, NT, 0)], [(bm, bn)], ident,
        [(_sds((M, D), F32), pl.BlockSpec((bm, bn), lambda i, j, k: (i, j)))])[0]
    return dh, dwg, dwu, dwd


def qk_prep(dm, P, gq, gk, cosf, sinf):
    D, KVW, bm = dm.D, dm.KVW, dm.bm

    def head_norm_rope(xh, g, c, s):
        r = lax.rsqrt(jnp.mean(xh * xh, axis=-1, keepdims=True) + EPS)
        y = xh * r * g
        return y * c + _pair_swap(y) * s

    def fn(ids, q_ref, k_ref, v_ref, gq_ref, gk_ref, c_ref, s_ref):
        c, s = c_ref[...], s_ref[...]
        qs = [head_norm_rope(q_ref[:, h * HEAD_DIM:(h + 1) * HEAD_DIM], gq_ref[...], c, s) for h in range(dm.NQ)]
        ks = [head_norm_rope(k_ref[:, h * HEAD_DIM:(h + 1) * HEAD_DIM], gk_ref[...], c, s) for h in range(dm.NKV)]
        return [jnp.concatenate(qs, axis=1), jnp.concatenate(ks, axis=1), v_ref[...]]

    hspec = pl.BlockSpec((bm, HEAD_DIM), lambda i: (i, 0))
    vec = pl.BlockSpec((1, HEAD_DIM), lambda i: (0, 0))
    return ew_call(
        "qk_prep", (dm.nTb,), fn,
        [(P, pl.BlockSpec((bm, D), lambda i: (i, 0))),
         (P, pl.BlockSpec((bm, KVW), lambda i: (i, dm.OFF_K // KVW))),
         (P, pl.BlockSpec((bm, KVW), lambda i: (i, dm.OFF_V // KVW))),
         (gq, vec), (gk, vec), (cosf, hspec), (sinf, hspec)],
        [(_sds((dm.T, D), BF), pl.BlockSpec((bm, D), lambda i: (i, 0)), False),
         (_sds((dm.T, KVW), BF), pl.BlockSpec((bm, KVW), lambda i: (i, 0)), False),
         (_sds((dm.T, KVW), BF), pl.BlockSpec((bm, KVW), lambda i: (i, 0)), False)])


def qk_prep_bwd(dm, dq, dk, dv, P, gq, gk, cosf, sinf, dP):
    D, KVW, bm, nCb = dm.D, dm.KVW, dm.bm, dm.nCb
    W = D + 2 * KVW

    def head_bwd(d, xh, g, c, s):
        dy = d * c - _pair_swap(d) * s
        r = lax.rsqrt(jnp.mean(xh * xh, axis=-1, keepdims=True) + EPS)
        xn = xh * r
        dg = jnp.sum(dy * xn, axis=0, keepdims=True)
        dxn = dy * g
        return r * (dxn - xn * jnp.mean(dxn * xn, axis=-1, keepdims=True)), dg

    def fn(ids, dq_ref, dk_ref, dv_ref, q_ref, k_ref, gq_ref, gk_ref, c_ref, s_ref, dp_any):
        i = ids[0]
        c, s = c_ref[...], s_ref[...]
        lat = i >= nCb
        outs, dgq = [], jnp.zeros((1, HEAD_DIM), F32)
        for h in range(dm.NQ):
            sl = slice(h * HEAD_DIM, (h + 1) * HEAD_DIM)
            d = jnp.where(lat, dq_ref[:, sl], 0.0)
            dx, dg = head_bwd(d, q_ref[:, sl], gq_ref[...], c, s)
            outs.append(dx)
            dgq = dgq + dg
        dgk = jnp.zeros((1, HEAD_DIM), F32)
        for h in range(dm.NKV):
            sl = slice(h * HEAD_DIM, (h + 1) * HEAD_DIM)
            dx, dg = head_bwd(dk_ref[:, sl], k_ref[:, sl], gk_ref[...], c, s)
            outs.append(dx)
            dgk = dgk + dg
        outs.append(dv_ref[...])
        return [jnp.concatenate(outs, axis=1), dgq, dgk]

    hspec = pl.BlockSpec((bm, HEAD_DIM), lambda i: (i, 0))
    vec = pl.BlockSpec((1, HEAD_DIM), lambda i: (0, 0))
    return ew_call(
        "qk_prep_bwd", (dm.nTb,), fn,
        [(dq, pl.BlockSpec((bm, D), lambda i: (jnp.maximum(i - nCb, 0), 0))),
         (dk, pl.BlockSpec((bm, KVW), lambda i: (i, 0))),
         (dv, pl.BlockSpec((bm, KVW), lambda i: (i, 0))),
         (P, pl.BlockSpec((bm, D), lambda i: (i, 0))),
         (P, pl.BlockSpec((bm, KVW), lambda i: (i, dm.OFF_K // KVW))),
         (gq, vec), (gk, vec), (cosf, hspec), (sinf, hspec), (dP, ANY)],
        [(_sds(dP.shape, BF), pl.BlockSpec((bm, W), lambda i: (i, 0)), False),
         (_sds((1, HEAD_DIM), F32), vec, True), (_sds((1, HEAD_DIM), F32), vec, True)],
        first=lambda ids: ids[0] == 0, aliases={9: 0})


def attention_fwd(dm, qr, kr, vb):
    S, T, D, G, nCb = dm.S, dm.T, dm.D, dm.G, dm.nCb
    bq = dm.bq
    off = dm.C // bq
    scale = HEAD_DIM ** -0.5
    GW = G * HEAD_DIM

    def body(q_ref, k_ref, v_ref, o_ref):
        k = k_ref[...]
        v = v_ref[...]
        for h in range(G):
            sl = slice(h * HEAD_DIM, (h + 1) * HEAD_DIM)
            s = lax.dot_general(q_ref[:, sl], k, (NT, ((), ())), preferred_element_type=F32) * scale
            m = jnp.max(s, axis=-1, keepdims=True)
            p = jnp.exp(s - m)
            l = jnp.sum(p, axis=-1, keepdims=True)
            o = lax.dot_general(p.astype(BF), v, (NN, ((), ())), preferred_element_type=F32)
            o_ref[:, sl] = o / l

    return pl.pallas_call(
        body, grid=(dm.NKV, S // bq), name="attn_fwd",
        in_specs=[pl.BlockSpec((bq, GW), lambda g, i: (i + off, g)),
                  pl.BlockSpec((T, HEAD_DIM), lambda g, i: (0, g)),
                  pl.BlockSpec((T, HEAD_DIM), lambda g, i: (0, g))],
        out_specs=pl.BlockSpec((bq, GW), lambda g, i: (i, g)),
        out_shape=_sds((S, D), F32), compiler_params=_cparams(),
    )(qr, kr, vb)


def attention_bwd(dm, qr, kr, vb, dattn):
    S, T, D, G = dm.S, dm.T, dm.D, dm.G
    bq = dm.bq
    off = dm.C // bq
    scale = HEAD_DIM ** -0.5
    GW = G * HEAD_DIM

    def body(q_ref, k_ref, v_ref, do_ref, dq_ref, dk_ref, dv_ref):
        i = pl.program_id(1)

        @pl.when(i == 0)
        def _():
            dk_ref[...] = jnp.zeros(dk_ref.shape, F32)
            dv_ref[...] = jnp.zeros(dv_ref.shape, F32)

        k = k_ref[...]
        v = v_ref[...]
        for h in range(G):
            sl = slice(h * HEAD_DIM, (h + 1) * HEAD_DIM)
            q = q_ref[:, sl]
            do = do_ref[:, sl]
            s = lax.dot_general(q, k, (NT, ((), ())), preferred_element_type=F32) * scale
            m = jnp.max(s, axis=-1, keepdims=True)
            e = jnp.exp(s - m)
            p = e / jnp.sum(e, axis=-1, keepdims=True)
            pb = p.astype(BF)
            dv_ref[...] += lax.dot_general(pb, do, (TN, ((), ())), preferred_element_type=F32)
            dp = lax.dot_general(do, v, (NT, ((), ())), preferred_element_type=F32)
            ds = p * (dp - jnp.sum(p * dp, axis=-1, keepdims=True)) * scale
            dsb = ds.astype(BF)
            dq_ref[:, sl] = lax.dot_general(dsb, k, (NN, ((), ())), preferred_element_type=F32)
            dk_ref[...] += lax.dot_general(dsb, q, (TN, ((), ())), preferred_element_type=F32)

    return pl.pallas_call(
        body, grid=(dm.NKV, S // bq), name="attn_bwd",
        in_specs=[pl.BlockSpec((bq, GW), lambda g, i: (i + off, g)),
                  pl.BlockSpec((T, HEAD_DIM), lambda g, i: (0, g)),
                  pl.BlockSpec((T, HEAD_DIM), lambda g, i: (0, g)),
                  pl.BlockSpec((bq, GW), lambda g, i: (i + off, g))],
        out_specs=[pl.BlockSpec((bq, GW), lambda g, i: (i, g)),
                   pl.BlockSpec((T, HEAD_DIM), lambda g, i: (0, g)),
                   pl.BlockSpec((T, HEAD_DIM), lambda g, i: (0, g))],
        out_shape=[_sds((S, D), F32), _sds((T, dm.KVW), F32), _sds((T, dm.KVW), F32)],
        compiler_params=_cparams(),
    )(qr, kr, vb, dattn)


def _conv_taps(dm, lx, masks_only=False):
    T, C = dm.T, dm.C
    t = lax.broadcasted_iota(jnp.int32, (T, 1), 0)
    valid = [(t >= 2) & ((t < C) | (t >= C + 2)), (t >= 1) & ((t < C) | (t >= C + 1)), None,
             (t != C - 1) & (t != T - 1)]
    shifts = [2, 1, 0, T - 1]
    taps = []
    for k in range(4):
        if k == 2:
            taps.append(lx)
        else:
            taps.append(jnp.where(valid[k], pltpu.roll(lx, shifts[k], 0), 0.0))
    return taps


def _scan_tiles(dm, asc, split, a_ref, u_ref, out_ref, mode):
    T, C = dm.T, dm.C
    nT, nC = T // 8, C // 8
    row = lax.broadcasted_iota(jnp.int32, (8, HEAD_DIM), 0)

    def tile_of(i):
        if not split:
            return i if asc else nT - 1 - i
        if asc:
            return jnp.where(i < nT - nC, nC + i, i - (nT - nC))
        return jnp.where(i < nC, nC - 1 - i, nT - 1 - (i - nC))

    def body(i, carry):
        off = pl.multiple_of(tile_of(i) * 8, 8)
        a = a_ref[pl.ds(off, 8), :]
        b = u_ref[pl.ds(off, 8), :]
        if mode == 'lam':
            if asc:
                coef = jnp.where(row == 0, 1.0, pltpu.roll(a, 1, 0))
            else:
                coef = jnp.where(row == 7, 1.0, pltpu.roll(a, 7, 0))
        else:
            coef = a
        A, B = coef, b
        for d in (1, 2, 4):
            if asc:
                ok = row >= d
                A_sh = jnp.where(ok, pltpu.roll(A, d, 0), 1.0)
                B_sh = jnp.where(ok, pltpu.roll(B, d, 0), 0.0)
            else:
                ok = row < 8 - d
                A_sh = jnp.where(ok, pltpu.roll(A, 8 - d, 0), 1.0)
                B_sh = jnp.where(ok, pltpu.roll(B, 8 - d, 0), 0.0)
            B = B + A * B_sh
            A = A * A_sh
        h = A * carry + B
        out_ref[pl.ds(off, 8), :] = h
        last = h[7:8, :] if asc else h[0:1, :]
        if mode == 'lam':
            last = last * (a[7:8, :] if asc else a[0:1, :])
        return jnp.broadcast_to(last, (8, HEAD_DIM))

    lax.fori_loop(0, nT, body, jnp.zeros((8, HEAD_DIM), F32))


def _lru_gates(xc, wa, ba, wx, bx, sp):
    xb = xc.astype(BF)
    r = _sig(jnp.dot(xb, wa, preferred_element_type=F32) + ba)
    i = _sig(jnp.dot(xb, wx, preferred_element_type=F32) + bx)
    a = jnp.exp(-LRU_C * r * sp)
    m = jnp.sqrt(1.0 - a * a)
    return r, i, a, m


def lru_fwd(dm, P, conv_w, conv_b, wa, ba, wx, bx, sp, direction):
    T, D, LB = dm.T, dm.D, dm.LB
    W = dm.LBD
    R = _pick(T, [272, 256, 128, 64, 8])
    lxb = dm.OFF_LX // W

    def body(lx_ref, cw_ref, cb_ref, wa_ref, ba_ref, wx_ref, bx_ref, sp_ref, h_ref, xc_ref, a_ref):
        taps = _conv_taps(dm, lx_ref[...])
        xc = cb_ref[...]
        for k in range(4):
            xc = xc + taps[k] * cw_ref[k:k + 1, :]
        xc_ref[...] = xc
        wa_, wx_ = wa_ref[...].astype(BF), wx_ref[...].astype(BF)

        def chunk(ci, _):
            off = pl.multiple_of(ci * R, 8)
            x = xc_ref[pl.ds(off, R), :]
            r, i, a, m = _lru_gates(x, wa_, ba_ref[...], wx_, bx_ref[...], sp_ref[...])
            a_ref[pl.ds(off, R), :] = a
            h_ref[pl.ds(off, R), :] = m * i * x
            return 0

        lax.fori_loop(0, T // R, chunk, 0)
        _scan_tiles(dm, direction == 0, direction == 1, a_ref, h_ref, h_ref, 'h')

    strip = lambda j: (0, j)
    vec = pl.BlockSpec((None, 1, W), lambda j: (direction, 0, j))
    mat = pl.BlockSpec((None, None, W, W), lambda j: (direction, j, 0, 0))
    return pl.pallas_call(
        body, grid=(LB,), name="lru_fwd%d" % direction,
        in_specs=[pl.BlockSpec((T, W), lambda j: (0, lxb + j)),
                  pl.BlockSpec((4, W), strip), pl.BlockSpec((1, W), strip), mat, vec, mat, vec, vec],
        out_specs=pl.BlockSpec((T, W), strip), out_shape=_sds((T, D), F32),
        scratch_shapes=[pltpu.VMEM((T, W), F32), pltpu.VMEM((T, W), F32)], compiler_params=_cparams(),
    )(P, conv_w, conv_b, wa, ba, wx, bx, sp)


def lru_bwd(dm, P, dh, h, conv_w, conv_b, wa, ba, wx, bx, sp, sg, direction, dxc_in, dP):
    T, C, D, LB = dm.T, dm.C, dm.D, dm.LB
    W = dm.LBD
    R = _pick(T, [272, 256, 128, 64, 8])
    lxb = dm.OFF_LX // W
    last = direction == 1

    def body(*refs):
        (lx_ref, dh_ref, h_ref, cw_ref, cb_ref, wa_ref, ba_ref, wx_ref, bx_ref, sp_ref, sg_ref) = refs[:11]
        pos = 11
        if last:
            dxin_ref, _dp_any = refs[pos:pos + 2]
            pos += 2
        out0 = refs[pos]
        dwa_ref, dba_ref, dwx_ref, dbx_ref, dlam_ref = refs[pos + 1:pos + 6]
        pos += 6
        if last:
            dcw_ref, dcb_ref = refs[pos:pos + 2]
            pos += 2
        xc_ref, a_ref, lam_ref, hp_ref, dxc_ref = refs[pos:pos + 5]

        lx = lx_ref[...]
        taps = _conv_taps(dm, lx)
        xc = cb_ref[...]
        for k in range(4):
            xc = xc + taps[k] * cw_ref[k:k + 1, :]
        xc_ref[...] = xc
        wa_, wx_ = wa_ref[...].astype(BF), wx_ref[...].astype(BF)

        def chunk_a(ci, _):
            off = pl.multiple_of(ci * R, 8)
            r, i, a, m = _lru_gates(xc_ref[pl.ds(off, R), :], wa_, ba_ref[...], wx_, bx_ref[...], sp_ref[...])
            a_ref[pl.ds(off, R), :] = a
            return 0

        lax.fori_loop(0, T // R, chunk_a, 0)
        _scan_tiles(dm, direction == 1, direction == 1, a_ref, dh_ref, lam_ref, 'lam')
        t = lax.broadcasted_iota(jnp.int32, (T, 1), 0)
        hv = h_ref[...]
        if direction == 0:
            hp_ref[...] = jnp.where(t == 0, 0.0, pltpu.roll(hv, 1, 0))
        else:
            nxt = pltpu.roll(hv, T - 1, 0)
            hp_ref[...] = jnp.where(t == C - 1, 0.0, jnp.where(t == T - 1, jnp.broadcast_to(hv[0:1, :], hv.shape), nxt))

        def chunk_b(ci, carry):
            dwa, dwx, dba, dbx, dlam = carry
            off = pl.multiple_of(ci * R, 8)
            x = xc_ref[pl.ds(off, R), :]
            r, i, a, m = _lru_gates(x, wa_, ba_ref[...], wx_, bx_ref[...], sp_ref[...])
            lam = lam_ref[pl.ds(off, R), :]
            ix = i * x
            da = lam * hp_ref[pl.ds(off, R), :] - lam * ix * a / m
            dloga = da * a
            dza = dloga * (-LRU_C) * sp_ref[...] * r * (1.0 - r)
            dzx = lam * m * x * i * (1.0 - i)
            dzab, dzxb = dza.astype(BF), dzx.astype(BF)
            xb = x.astype(BF)
            dxc = lam * m * i
            dxc = dxc + lax.dot_general(dzab, wa_, (NT, ((), ())), preferred_element_type=F32)
            dxc = dxc + lax.dot_general(dzxb, wx_, (NT, ((), ())), preferred_element_type=F32)
            dxc_ref[pl.ds(off, R), :] = dxc
            dwa = dwa + lax.dot_general(xb, dzab, (TN, ((), ())), preferred_element_type=F32)
            dwx = dwx + lax.dot_general(xb, dzxb, (TN, ((), ())), preferred_element_type=F32)
            dba = dba + jnp.sum(dza, axis=0, keepdims=True)
            dbx = dbx + jnp.sum(dzx, axis=0, keepdims=True)
            dlam = dlam + jnp.sum(dloga * LRU_C * r, axis=0, keepdims=True)
            return dwa, dwx, dba, dbx, dlam

        z = jnp.zeros((W, W), F32)
        zv = jnp.zeros((1, W), F32)
        dwa, dwx, dba, dbx, dlam = lax.fori_loop(0, T // R, chunk_b, (z, z, zv, zv, zv))
        dwa_ref[...] = dwa
        dwx_ref[...] = dwx
        dba_ref[...] = dba
        dbx_ref[...] = dbx
        dlam_ref[...] = dlam * sg_ref[...]
        if not last:
            out0[...] = dxc_ref[...]
        else:
            dxc = dxc_ref[...] + dxin_ref[...]
            dcb_ref[...] = jnp.sum(dxc, axis=0, keepdims=True)
            rows = [jnp.sum(dxc * taps[k], axis=0, keepdims=True) for k in range(4)]
            dcw_ref[...] = jnp.concatenate(rows, axis=0)
            valid = [(t < T - 2) & ((t >= C) | (t < C - 2)), (t < T - 1) & ((t >= C) | (t < C - 1)), None,
                     (t != 0) & (t != C)]
            shifts = [T - 2, T - 1, 0, 1]
            dlx = dxc * cw_ref[2:3, :]
            for k in (0, 1, 3):
                dlx = dlx + jnp.where(valid[k], pltpu.roll(dxc, shifts[k], 0), 0.0) * cw_ref[k:k + 1, :]
            out0[...] = dlx.astype(out0.dtype)

    strip = lambda j: (0, j)
    sspec = pl.BlockSpec((T, W), strip)
    vec = pl.BlockSpec((None, 1, W), lambda j: (direction, 0, j))
    mat = pl.BlockSpec((None, None, W, W), lambda j: (direction, j, 0, 0))
    ins = [P, dh, h, conv_w, conv_b, wa, ba, wx, bx, sp, sg]
    in_specs = [pl.BlockSpec((T, W), lambda j: (0, lxb + j)), sspec, sspec,
                pl.BlockSpec((4, W), strip), pl.BlockSpec((1, W), strip), mat, vec, mat, vec, vec, vec]
    omat = pl.BlockSpec((None, W, W), lambda j: (j, 0, 0))
    ovec = pl.BlockSpec((1, W), strip)
    small = [_sds((LB, W, W), F32), _sds((1, D), F32), _sds((LB, W, W), F32), _sds((1, D), F32), _sds((1, D), F32)]
    small_specs = [omat, ovec, omat, ovec, ovec]
    if last:
        ins += [dxc_in, dP]
        in_specs += [sspec, ANY]
        out_shape = [_sds(dP.shape, BF)] + small + [_sds((4, D), F32), _sds((1, D), F32)]
        out_specs = [pl.BlockSpec((T, W), lambda j: (0, lxb + j))] + small_specs + [pl.BlockSpec((4, W), strip), ovec]
        aliases = {12: 0}
    else:
        out_shape = [_sds((T, D), F32)] + small
        out_specs = [sspec] + small_specs
        aliases = {}
    return pl.pallas_call(
        body, grid=(LB,), name="lru_bwd%d" % direction, in_specs=in_specs, out_specs=out_specs, out_shape=out_shape,
        scratch_shapes=[pltpu.VMEM((T, W), F32)] * 5, input_output_aliases=aliases, compiler_params=_cparams(),
    )(*ins)


def merge_fwd(dm, P, attn, hf, hb):
    S, D, bm, cw, nCb = dm.S, dm.D, dm.bm, dm.cw, dm.nCb

    def fn(ids, lg_ref, ga_ref, gl_ref, at_ref, hf_ref, hb_ref):
        ge, _ = _gelu(lg_ref[...])
        lru = (hf_ref[...] + hb_ref[...]) * ge
        return [_sig(ga_ref[...]) * at_ref[...] + _sig(gl_ref[...]) * lru]

    pspec = lambda off: pl.BlockSpec((bm, cw), lambda i, j: (i + nCb, off // cw + j))
    tspec = pl.BlockSpec((bm, cw), lambda i, j: (i + nCb, j))
    sspec = pl.BlockSpec((bm, cw), lambda i, j: (i, j))
    return ew_call(
        "merge_fwd", (dm.nSb, D // cw), fn,
        [(P, pspec(dm.OFF_LG)), (P, pspec(dm.OFF_GA)), (P, pspec(dm.OFF_GL)), (attn, sspec), (hf, tspec), (hb, tspec)],
        [(_sds((S, D), BF), sspec, False)])[0]


def merge_bwd(dm, dmg, P, attn, hf, hb):
    S, T, D, bm, cw, nCb = dm.S, dm.T, dm.D, dm.bm, dm.cw, dm.nCb
    nj = D // cw

    def body(dm_ref, lg_ref, ga_ref, gl_ref, at_ref, hf_ref, hb_ref, dp_ref, da_ref, dh_ref, buf, sems):
        i, j = pl.program_id(0), pl.program_id(1)
        lat = i >= nCb
        d = jnp.where(lat, dm_ref[...].astype(F32), 0.0)
        lg = lg_ref[...]
        ge, th = _gelu(lg)
        hs = hf_ref[...] + hb_ref[...]
        sa, sl = _sig(ga_ref[...]), _sig(gl_ref[...])
        at = jnp.where(lat, at_ref[...], 0.0)
        dlru = d * sl
        buf[0] = (dlru * hs * _gelu_grad(lg, th)).astype(BF)
        buf[1] = (d * at * sa * (1.0 - sa)).astype(BF)
        buf[2] = (d * hs * ge * sl * (1.0 - sl)).astype(BF)
        da_ref[...] = (d * sa).astype(BF)
        dh_ref[...] = dlru * ge
        copies = []
        for g, off in enumerate((dm.OFF_LG, dm.OFF_GA, dm.OFF_GL)):
            col = pl.multiple_of(off + j * cw, 128)
            cp = pltpu.make_async_copy(buf.at[g], dp_ref.at[pl.ds(pl.multiple_of(i * bm, 8), bm), pl.ds(col, cw)],
                                       sems.at[g])
            cp.start()
            copies.append(cp)
        for cp in copies:
            cp.wait()

    pspec = lambda off: pl.BlockSpec((bm, cw), lambda i, j: (i, off // cw + j))
    tspec = pl.BlockSpec((bm, cw), lambda i, j: (i, j))
    lspec = pl.BlockSpec((bm, cw), lambda i, j: (jnp.maximum(i - nCb, 0), j))
    return pl.pallas_call(
        body, grid=(dm.nTb, nj), name="merge_bwd",
        in_specs=[lspec, pspec(dm.OFF_LG), pspec(dm.OFF_GA), pspec(dm.OFF_GL), lspec, tspec, tspec],
        out_specs=[ANY, tspec, tspec],
        out_shape=[_sds((T, dm.INW), BF), _sds((T, D), BF), _sds((T, D), F32)],
        scratch_shapes=[pltpu.VMEM((3, bm, cw), BF), pltpu.SemaphoreType.DMA((3,))],
        compiler_params=_cparams(),
    )(dmg, P, P, P, attn, hf, hb)


def final_loss(dm, x3, gfin, target):
    S, D, bm = dm.S, dm.D, dm.bm

    def fn(ids, x_ref, g_ref, t_ref):
        xv = x_ref[...]
        g = g_ref[...]
        r = lax.rsqrt(jnp.mean(xv * xv, axis=-1, keepdims=True) + EPS)
        xn = xv * r
        err = xn * g - t_ref[...]
        loss = 0.5 * jnp.sum(jnp.mean(err * err, axis=-1, keepdims=True), axis=0, keepdims=True)
        dy = err / D
        dxn = dy * g
        dx = r * (dxn - xn * jnp.mean(dxn * xn, axis=-1, keepdims=True))
        return [jnp.broadcast_to(loss, (1, 128)), dx, jnp.sum(dy * xn, axis=0, keepdims=True)]

    row = pl.BlockSpec((bm, D), lambda i: (i, 0))
    vec = pl.BlockSpec((1, D), lambda i: (0, 0))
    return ew_call(
        "final_loss", (dm.nSb,), fn, [(x3, row), (gfin, vec), (target, row)],
        [(_sds((1, 128), F32), pl.BlockSpec((1, 128), lambda i: (0, 0)), True), (_sds((S, D), F32), row, False),
         (_sds((1, D), F32), vec, True)], first=lambda ids: ids[0] == 0)


def local_step(dm, x, ctx, target, modv, norm_g3, gfin, gq, gk, conv_w, conv_b, wa, ba, wx, bx, lam,
               wg, wu, wd, w_in, w_out):
    S, C, T, D, NS, F4, W4 = dm.S, dm.C, dm.T, dm.D, dm.NS, dm.F4, dm.W4
    wb, nsub = dm.wb, dm.nsub
    xt = jnp.concatenate([ctx, x], axis=0)
    cosf, sinf = rope_tables(dm)
    sp = jax.nn.softplus(-lam)
    sg = jax.nn.sigmoid(-lam)

    h1 = normmod_fwd("nm1", dm, xt, norm_g3, 0, modv, True)
    xt1, a1, u1, s1, f1 = ffn_fwd("ffn1", dm, h1, xt, wg, wu, wd, 0, modv, 2, True)
    h2 = normmod_fwd("nm2", dm, xt1, norm_g3, 1, modv, True)
    mT = dm.mT
    ident = lambda ids, accs, ex: list(accs)
    P = fused_mm(
        "w_in", (T // mT, NS * nsub, 1),
        [(h2, pl.BlockSpec((mT, D), lambda i, j, k: (i, 0))),
         (w_in, pl.BlockSpec((None, D, wb), lambda i, j, k: (j // nsub, 0, j % nsub)))],
        [(0, 1, NN, 0)], [(mT, wb)], ident,
        [(_sds((T, dm.INW), F32), pl.BlockSpec((mT, wb), lambda i, j, k: (i, j)))])[0]
    qr, kr, vb = qk_prep(dm, P, gq, gk, cosf, sinf)
    attn = attention_fwd(dm, qr, kr, vb)
    hf = lru_fwd(dm, P, conv_w, conv_b, wa, ba, wx, bx, sp, 0)
    hb = lru_fwd(dm, P, conv_w, conv_b, wa, ba, wx, bx, sp, 1)
    mg = merge_fwd(dm, P, attn, hf, hb)
    mS = dm.mS
    bn = _pick(D, [1024, 512, 256, 128])
    bk = _pick(D, [512, 256, 128])

    def epi_o(ids, accs, ex):
        o = accs[0]
        return [ex[0][...] + ex[1][...] * o, o]

    x1_lat = xt1[C:]
    x2, o2 = fused_mm(
        "w_out", (S // mS, D // bn, D // bk),
        [(mg, pl.BlockSpec((mS, bk), lambda i, j, k: (i, k))), (w_out, pl.BlockSpec((bk, bn), lambda i, j, k: (k, j)))],
        [(0, 1, NN, 0)], [(mS, bn)], epi_o,
        [(_sds((S, D), F32), pl.BlockSpec((mS, bn), lambda i, j, k: (i, j))),
         (_sds((S, D), BF), pl.BlockSpec((mS, bn), lambda i, j, k: (i, j)))],
        extras=[(x1_lat, pl.BlockSpec((mS, bn), lambda i, j, k: (i, j))),
                (modv, pl.BlockSpec((None, None, 1, bn), lambda i, j, k: (1, 5, 0, j)))])
    h3 = normmod_fwd("nm3", dm, x2, norm_g3, 2, modv, False)
    x3, a3, u3, s3, f3 = ffn_fwd("ffn2", dm, h3, x2, wg, wu, wd, 1, modv, 8, False)
    loss, dx3, dgfin = final_loss(dm, x3, gfin, target)

    df3, dg3 = gate_bwd("gate3", dm, dx3, f3, modv, 8, FFN_RES, False)
    dh3, dwg1, dwu1, dwd1 = ffn_bwd("ffn2b", dm, df3, h3, a3, u3, s3, wg, wu, wd, 1, False)
    dx2, dsh3, dsc3, dgn3 = normmod_bwd("nm3b", dm, dh3, x2, dx3, norm_g3, 2, modv, False, False)
    do2, dg2 = gate_bwd("gate2", dm, dx2, o2, modv, 5, 1.0, False)
    dmg = fused_mm(
        "w_out_dx", (S // mS, D // bn, D // bk),
        [(do2, pl.BlockSpec((mS, bk), lambda i, j, k: (i, k))), (w_out, pl.BlockSpec((bn, bk), lambda i, j, k: (j, k)))],
        [(0, 1, NT, 0)], [(mS, bn)], ident,
        [(_sds((S, D), BF), pl.BlockSpec((mS, bn), lambda i, j, k: (i, j)))])[0]
    dw_out = fused_mm(
        "w_out_dw", (D // bn, D // bn, S // mS),
        [(mg, pl.BlockSpec((mS, bn), lambda i, j, k: (k, i))), (do2, pl.BlockSpec((mS, bn), lambda i, j, k: (k, j)))],
        [(0, 1, TN, 0)], [(bn, bn)], ident,
        [(_sds((D, D), BF), pl.BlockSpec((bn, bn), lambda i, j, k: (i, j)))])[0]
    dP, dattn, dhs = merge_bwd(dm, dmg, P, attn, hf, hb)
    dxc0, dwa0, dba0, dwx0, dbx0, dlam0 = lru_bwd(dm, P, dhs, hf, conv_w, conv_b, wa, ba, wx, bx, sp, sg, 0, None, None)
    dP, dwa1, dba1, dwx1, dbx1, dlam1, dcw, dcb = lru_bwd(dm, P, dhs, hb, conv_w, conv_b, wa, ba, wx, bx, sp, sg, 1,
                                                          dxc0, dP)
    dq, dk, dv = attention_bwd(dm, qr, kr, vb, dattn)
    dP, dgq, dgk = qk_prep_bwd(dm, dq, dk, dv, P, gq, gk, cosf, sinf, dP)
    nkb = NS * nsub
    dh2 = fused_mm(
        "w_in_dx", (T // mT, D // bn, nkb),
        [(dP, pl.BlockSpec((mT, wb), lambda i, j, k: (i, k))),
         (w_in, pl.BlockSpec((None, bn, wb), lambda i, j, k: (k // nsub, j, k % nsub)))],
        [(0, 1, NT, 0)], [(mT, bn)], ident,
        [(_sds((T, D), F32), pl.BlockSpec((mT, bn), lambda i, j, k: (i, j)))])[0]
    dw_in = fused_mm(
        "w_in_dw", (D // bn, nkb, T // mT),
        [(h2, pl.BlockSpec((mT, bn), lambda i, j, k: (k, i))), (dP, pl.BlockSpec((mT, wb), lambda i, j, k: (k, j)))],
        [(0, 1, TN, 0)], [(bn, wb)], ident,
        [(_sds((NS, D, W4), BF), pl.BlockSpec((None, bn, wb), lambda i, j, k: (j // nsub, i, j % nsub)))])[0]
    dxt1, dsh2, dsc2, dgn2 = normmod_bwd("nm2b", dm, dh2, xt1, dx2, norm_g3, 1, modv, True, True)
    df1, dg1 = gate_bwd("gate1", dm, dxt1, f1, modv, 2, FFN_RES, True)
    dh1, dwg0, dwu0, dwd0 = ffn_bwd("ffn1b", dm, df1, h1, a1, u1, s1, wg, wu, wd, 0, True)
    dxt, dsh1, dsc1, dgn1 = normmod_bwd("nm1b", dm, dh1, xt, dxt1, norm_g3, 0, modv, True, False)
    grad_x = dxt[C:]

    dmod = jnp.concatenate([dsh1, dsc1, dg1, dsh2, dsc2, _lat(dg2), _lat(dsh3), _lat(dsc3), _lat(dg3)], axis=1)
    dnorm = jnp.stack([dgn1[0, 0] + dgn1[1, 0], dgn2[0, 0] + dgn2[1, 0], dgn3[1, 0]], axis=0)
    small = dict(norm_g=dnorm, q_norm_g=dgq, k_norm_g=dgk, conv_w=dcw, conv_b=dcb,
                 lru_wa=jnp.stack([dwa0, dwa1]), lru_ba=jnp.concatenate([dba0, dba1], axis=0),
                 lru_wx=jnp.stack([dwx0, dwx1]), lru_bx=jnp.concatenate([dbx0, dbx1], axis=0),
                 lru_lambda=jnp.concatenate([dlam0, dlam1], axis=0), final_norm_g=dgfin)
    big = dict(ffn_wg=[dwg0, dwg1], ffn_wu=[dwu0, dwu1], ffn_wd=[dwd0, dwd1], w_in=dw_in, w_out=dw_out)
    return loss, grad_x, dmod, small, big


def _lat(v):
    return jnp.concatenate([jnp.zeros_like(v[:1]), v[1:]], axis=0)


def _me():
    return lax.axis_index("x"), lax.axis_index("y"), lax.axis_index("c")


def allgather8(name, v):
    def body(v_ref, out_ref, send_sems, recv_sems, local_sem):
        x, y, c = _me()
        me = 4 * x + 2 * y + c
        mine = pltpu.make_async_copy(v_ref, out_ref.at[me], local_sem)
        mine.start()
        copies = []
        for k in range(1, 8):
            peer = (x ^ ((k >> 2) & 1), y ^ ((k >> 1) & 1), c ^ (k & 1))
            cp = pltpu.make_async_remote_copy(src_ref=v_ref, dst_ref=out_ref.at[me], send_sem=send_sems.at[k - 1],
                                              recv_sem=recv_sems.at[k - 1], device_id=peer, device_id_type=MESH)
            cp.start()
            copies.append(cp)
        for k in range(1, 8):
            peer = (x ^ ((k >> 2) & 1), y ^ ((k >> 1) & 1), c ^ (k & 1))
            pltpu.make_async_remote_copy(src_ref=v_ref, dst_ref=out_ref.at[me ^ k], send_sem=send_sems.at[k - 1],
                                         recv_sem=recv_sems.at[k - 1], device_id=peer, device_id_type=MESH).wait_recv()
        for cp in copies:
            cp.wait_send()
        mine.wait()

    return pl.pallas_call(
        body, name=name, out_shape=_sds((8,) + v.shape, v.dtype), in_specs=[ANY], out_specs=ANY,
        scratch_shapes=[pltpu.SemaphoreType.DMA((7,)), pltpu.SemaphoreType.DMA((7,)), pltpu.SemaphoreType.DMA],
    )(v)


def _chips(x, y):
    chips = [(1 - x, y), (x, 1 - y), (1 - x, 1 - y)]
    return chips, [2 * cx + cy for cx, cy in chips]


def allgather_weights(bufs):
    n = len(bufs)

    def body(*refs):
        outs = refs[n:2 * n]
        send_sems, recv_sems = refs[2 * n:]
        x, y, c = _me()
        s = 2 * x + y
        sib = (x, y, 1 - c)
        chips, slots = _chips(x, y)
        sends, fwds = [], []
        for t in range(n):
            H = outs[t].shape[1] // 2
            mine = outs[t].at[s, pl.ds(c * H, H)]
            for j, chip in enumerate(chips):
                cp = pltpu.make_async_remote_copy(
                    src_ref=mine, dst_ref=mine, send_sem=send_sems.at[6 * t + j],
                    recv_sem=recv_sems.at[6 * t + j], device_id=(chip[0], chip[1], c), device_id_type=MESH)
                cp.start()
                sends.append(cp)
        for t in range(n):
            H = outs[t].shape[1] // 2
            for j, chip in enumerate(chips):
                landed = outs[t].at[slots[j], pl.ds(c * H, H)]
                pltpu.make_async_remote_copy(
                    src_ref=landed, dst_ref=landed, send_sem=send_sems.at[6 * t + j],
                    recv_sem=recv_sems.at[6 * t + j], device_id=(chip[0], chip[1], c), device_id_type=MESH).wait_recv()
                fw = pltpu.make_async_remote_copy(
                    src_ref=landed, dst_ref=landed, send_sem=send_sems.at[6 * t + 3 + j],
                    recv_sem=recv_sems.at[6 * t + 3 + j], device_id=sib, device_id_type=MESH)
                fw.start()
                fwds.append(fw)
        for t in range(n):
            H = outs[t].shape[1] // 2
            for j in range(3):
                got = outs[t].at[slots[j], pl.ds((1 - c) * H, H)]
                pltpu.make_async_remote_copy(
                    src_ref=got, dst_ref=got, send_sem=send_sems.at[6 * t + 3 + j],
                    recv_sem=recv_sems.at[6 * t + 3 + j], device_id=sib, device_id_type=MESH).wait_recv()
        for cp in sends + fwds:
            cp.wait_send()

    return pl.pallas_call(
        body, name="allgather_weights", out_shape=[_sds(t.shape, t.dtype) for t in bufs],
        in_specs=[ANY] * n, out_specs=[ANY] * n, input_output_aliases={t: t for t in range(n)},
        scratch_shapes=[pltpu.SemaphoreType.DMA((6 * n,)), pltpu.SemaphoreType.DMA((6 * n,))],
    )(*bufs)


def rs_sibling_exchange(tensors):
    n = len(tensors)

    def body(*refs):
        ins, gots = refs[:n], refs[n:2 * n]
        send_sems, recv_sems = refs[2 * n:]
        x, y, c = _me()
        sib = (x, y, 1 - c)
        work = []
        for t in range(n):
            H = ins[t].shape[1] // 2
            cp = pltpu.make_async_remote_copy(
                src_ref=ins[t].at[:, pl.ds((1 - c) * H, H)], dst_ref=gots[t], send_sem=send_sems.at[t],
                recv_sem=recv_sems.at[t], device_id=sib, device_id_type=MESH)
            cp.start()
            work.append(cp)
        for cp in work:
            cp.wait_recv()
        for cp in work:
            cp.wait_send()

    half = lambda t: _sds((t.shape[0], t.shape[1] // 2) + t.shape[2:], t.dtype)
    return pl.pallas_call(
        body, name="rs_sibling_exchange", out_shape=[half(t) for t in tensors],
        in_specs=[ANY] * n, out_specs=[ANY] * n,
        scratch_shapes=[pltpu.SemaphoreType.DMA((n,)), pltpu.SemaphoreType.DMA((n,))],
    )(*tensors)


def rs_chip_exchange(partials, landeds):
    n = len(partials)

    def body(*refs):
        ins, outs = refs[:n], refs[2 * n:3 * n]
        send_sems, recv_sems = refs[3 * n:]
        x, y, c = _me()
        s = 2 * x + y
        chips, slots = _chips(x, y)
        sends = []
        for t in range(n):
            for j, chip in enumerate(chips):
                src = ins[t].at[slots[j]] if ins[t].shape[0] == 4 else ins[t].at[0]
                cp = pltpu.make_async_remote_copy(
                    src_ref=src, dst_ref=outs[t].at[s], send_sem=send_sems.at[3 * t + j],
                    recv_sem=recv_sems.at[3 * t + j], device_id=(chip[0], chip[1], c), device_id_type=MESH)
                cp.start()
                sends.append(cp)
        for t in range(n):
            for j, chip in enumerate(chips):
                dst = outs[t].at[slots[j]]
                pltpu.make_async_remote_copy(
                    src_ref=dst, dst_ref=dst, send_sem=send_sems.at[3 * t + j],
                    recv_sem=recv_sems.at[3 * t + j], device_id=(chip[0], chip[1], c), device_id_type=MESH).wait_recv()
        for cp in sends:
            cp.wait_send()

    return pl.pallas_call(
        body, name="rs_chip_exchange", out_shape=[_sds(t.shape, t.dtype) for t in landeds],
        in_specs=[ANY] * (2 * n), out_specs=[ANY] * n, input_output_aliases={n + t: t for t in range(n)},
        scratch_shapes=[pltpu.SemaphoreType.DMA((3 * n,)), pltpu.SemaphoreType.DMA((3 * n,))],
    )(*partials, *landeds)


def rs_sibling_share(greds, plan):
    n, m = len(plan), len(greds)

    def body(*refs):
        outs = refs[m:2 * m]
        send_sems, recv_sems = refs[2 * m:]
        x, y, c = _me()
        sib = (x, y, 1 - c)
        work = []
        for t, (oi, li, H) in enumerate(plan):
            dst = outs[oi] if li is None else outs[oi].at[li]
            cp = pltpu.make_async_remote_copy(
                src_ref=dst.at[pl.ds(c * H, H)], dst_ref=dst.at[pl.ds(c * H, H)], send_sem=send_sems.at[t],
                recv_sem=recv_sems.at[t], device_id=sib, device_id_type=MESH)
            cp.start()
            work.append(cp)
        for t, (oi, li, H) in enumerate(plan):
            dst = outs[oi] if li is None else outs[oi].at[li]
            other = dst.at[pl.ds((1 - c) * H, H)]
            pltpu.make_async_remote_copy(
                src_ref=other, dst_ref=other, send_sem=send_sems.at[t], recv_sem=recv_sems.at[t],
                device_id=sib, device_id_type=MESH).wait_recv()
        for cp in work:
            cp.wait_send()

    return pl.pallas_call(
        body, name="rs_sibling_share", out_shape=[_sds(g.shape, g.dtype) for g in greds],
        in_specs=[ANY] * m, out_specs=[ANY] * m, input_output_aliases={t: t for t in range(m)},
        scratch_shapes=[pltpu.SemaphoreType.DMA((n,)), pltpu.SemaphoreType.DMA((n,))],
    )(*greds)


def _rows_block(rows, cols, nbytes=1 << 20):
    bm = 8
    while bm * 2 * cols * 4 <= nbytes and rows % (bm * 2) == 0:
        bm *= 2
    return bm


def cast_into_slot(name, w, where):
    shape = w.shape
    w2 = w.reshape(-1, shape[-1])
    rows, W = w2.shape
    bm = _rows_block(rows, W)

    def body(p_ref, w_ref, o_ref):
        o_ref[...] = w_ref[...].astype(BF)

    out = pl.pallas_call(
        body, name=name, out_shape=_sds((4, rows, W), BF), compiler_params=_cparams(),
        grid_spec=pltpu.PrefetchScalarGridSpec(
            num_scalar_prefetch=1, grid=(rows // bm,),
            in_specs=[pl.BlockSpec((bm, W), lambda i, p: (i, 0))],
            out_specs=pl.BlockSpec((None, bm, W), lambda i, p: (p[1], i, 0))),
    )(where, w2)
    return out.reshape((4,) + shape)


def add_pair(name, g, got, where):
    K, R, W = g.shape
    H = R // 2
    bm = _rows_block(H, W)
    nh = H // bm

    def body(p_ref, g_ref, got_ref, part_ref, land_ref):
        k = pl.program_id(1)
        v = (g_ref[...].astype(F32) + got_ref[...].astype(F32)).astype(part_ref.dtype)
        part_ref[...] = v
        own = (k == p_ref[1]) if K == 4 else (k == 0)

        @pl.when(own)
        def _():
            land_ref[...] = v

    return pl.pallas_call(
        body, name=name, out_shape=[_sds((K, H, W), g.dtype), _sds((4, H, W), g.dtype)], compiler_params=_cparams(),
        grid_spec=pltpu.PrefetchScalarGridSpec(
            num_scalar_prefetch=1, grid=(nh, K),
            in_specs=[pl.BlockSpec((None, bm, W), lambda i, k, p: (k, p[0] * nh + i, 0)),
                      pl.BlockSpec((None, bm, W), lambda i, k, p: (k, i, 0))],
            out_specs=[pl.BlockSpec((None, bm, W), lambda i, k, p: (k, i, 0)),
                       pl.BlockSpec((None, bm, W), lambda i, k, p: (p[1], i, 0))]),
    )(where, g, got)


def sum_slots_into(name, landed, where, dest, dest_shape, li):
    K, H, W = landed.shape
    bm = _rows_block(H, 2 * W)
    nh = H // bm

    def body(*refs):
        r, o_ref = refs[1], refs[-1]
        acc = r[0].astype(F32)
        for k in range(1, K):
            acc = acc + r[k].astype(F32)
        o_ref[...] = acc

    if li is None:
        ospec = pl.BlockSpec((bm, W), lambda i, p: (p[0] * nh + i, 0))
    else:
        ospec = pl.BlockSpec((None, bm, W), lambda i, p: (li, p[0] * nh + i, 0))
    in_specs = [pl.BlockSpec((K, bm, W), lambda i, p: (0, i, 0))]
    args = [where, landed]
    aliases = {}
    if dest is not None:
        in_specs.append(ANY)
        args.append(dest)
        aliases = {2: 0}
    return pl.pallas_call(
        body, name=name, out_shape=_sds(dest_shape, F32), compiler_params=_cparams(), input_output_aliases=aliases,
        grid_spec=pltpu.PrefetchScalarGridSpec(num_scalar_prefetch=1, grid=(nh,), in_specs=in_specs, out_specs=ospec),
    )(*args)


def adamw(name, w, g, m, v):
    shape = w.shape
    flat = lambda t: t.reshape(-1, shape[-1])
    w2, g2, m2, v2 = flat(w), flat(g), flat(m), flat(v)
    bm = _rows_block(w2.shape[0], w2.shape[1] * 2)
    bc1 = 1.0 - ADAM_B1 ** ADAM_STEP
    bc2 = 1.0 - ADAM_B2 ** ADAM_STEP

    def fn(ids, w_ref, g_ref, m_ref, v_ref):
        gv = g_ref[...]
        mn = ADAM_B1 * m_ref[...] + (1.0 - ADAM_B1) * gv
        vn = ADAM_B2 * v_ref[...] + (1.0 - ADAM_B2) * (gv * gv)
        m_hat = mn / bc1
        v_hat = vn / bc2
        delta = -ADAM_LR * (m_hat / (jnp.sqrt(v_hat) + ADAM_EPS) + ADAM_WD * w_ref[...])
        return [delta, mn, vn]

    spec = pl.BlockSpec((bm, w2.shape[1]), lambda i: (i, 0))
    outs = ew_call(name, (w2.shape[0] // bm,), fn, [(w2, spec), (g2, spec), (m2, spec), (v2, spec)],
                   [(_sds(w2.shape, F32), spec, False)] * 3)
    return [o.reshape(shape) for o in outs]


def dmod_pack(gd):
    N = gd.shape[-1]
    bn = _pick(N, [4608, 2304, 1152, 1024, 512, 256, 128])

    def fn(ids, r):
        lat = [r[d, 1:2, :] for d in range(8)]
        cs = r[0, 0:1, :]
        for d in range(1, 8):
            cs = cs + r[d, 0:1, :]
        tot = cs
        for d in range(8):
            tot = tot + lat[d]
        return [jnp.concatenate(lat + [cs, jnp.zeros((7, bn), F32)], axis=0), tot]

    return ew_call("dmod_pack", (N // bn,), fn, [(gd, pl.BlockSpec((8, 2, bn), lambda j: (0, 0, j)))],
                   [(_sds((16, N), F32), pl.BlockSpec((16, bn), lambda j: (0, j)), False),
                    (_sds((1, N), F32), pl.BlockSpec((1, bn), lambda j: (0, j)), False)])


def _silu(v):
    return v * _sig(v)


def kernel(x, c, ctx, c_ctx, w_mod, b_mod, norm_g, ffn_wg, ffn_wu, ffn_wd, w_in, w_out, q_norm_g, k_norm_g, conv_w, conv_b, lru_wa, lru_ba, lru_wx, lru_bx, lru_lambda, final_norm_g, loss_target, m_c_ctx, m_w_mod, m_b_mod, m_norm_g, m_ffn_wg, m_ffn_wu, m_ffn_wd, m_w_in, m_w_out, m_q_norm_g, m_k_norm_g, m_conv_w, m_conv_b, m_lru_wa, m_lru_ba, m_lru_wx, m_lru_bx, m_lru_lambda, m_final_norm_g, v_c_ctx, v_w_mod, v_b_mod, v_norm_g, v_ffn_wg, v_ffn_wu, v_ffn_wd, v_w_in, v_w_out, v_q_norm_g, v_k_norm_g, v_conv_w, v_conv_b, v_lru_wa, v_lru_ba, v_lru_wx, v_lru_bx, v_lru_lambda, v_final_norm_g):
    given = dict(locals())
    names = ['c_ctx', 'w_mod', 'b_mod', 'norm_g', 'ffn_wg', 'ffn_wu', 'ffn_wd', 'w_in', 'w_out', 'q_norm_g', 'k_norm_g',
             'conv_w', 'conv_b', 'lru_wa', 'lru_ba', 'lru_wx', 'lru_bx', 'lru_lambda', 'final_norm_g']
    S, D = x.shape[1], x.shape[2]
    C = ctx.shape[1]
    NS = 4
    F4, W4, LB = ffn_wg.shape[-1], w_in.shape[-1], lru_wa.shape[2]
    dm = Dims(S, C, D, F4, W4, NS, LB)
    Ds = D // NS
    Wm = w_mod.shape[-1]
    xi, yi, ci = lax.axis_index("x"), lax.axis_index("y"), lax.axis_index("c")
    slot = 2 * xi + yi
    me = 4 * xi + 2 * yi + ci
    ident = lambda ids, accs, ex: list(accs)

    pack1 = jnp.concatenate([c.reshape(-1), norm_g.reshape(-1), conv_w.reshape(-1), lru_ba.reshape(-1),
                             lru_bx.reshape(-1), lru_lambda.reshape(-1)]).reshape(1, -1)
    g1 = allgather8("ag_small_params", pack1)[:, 0]
    c_all = g1[:, :D]

    def unshard(off, k):
        part = g1[0::2, off:off + k * Ds].reshape(NS, k, Ds)
        return jnp.transpose(part, (1, 0, 2)).reshape(k, D)

    norm_g_f = unshard(D, 3)
    conv_w_f = unshard(D + 3 * Ds, 4)
    ba_f = unshard(D + 7 * Ds, 2)
    bx_f = unshard(D + 9 * Ds, 2)
    lam_f = unshard(D + 11 * Ds, 2)

    call16 = jnp.concatenate([c_all, c_ctx.reshape(1, D), jnp.zeros((7, D), F32)], axis=0)
    b_cols = lax.dynamic_slice(b_mod, (0, slot * Wm), (1, Wm))
    bnm = _pick(Wm, [1536, 1152, 768, 512, 384, 256, 128])
    bkm = _pick(D, [512, 256, 128])
    modp = fused_mm(
        "mod_fwd", (1, Wm // bnm, D // bkm),
        [(call16, pl.BlockSpec((16, bkm), lambda i, j, k: (0, k))),
         (w_mod[0], pl.BlockSpec((bkm, bnm), lambda i, j, k: (k, j)))],
        [(0, 1, NN, 0)], [(16, bnm)], lambda ids, accs, ex: [accs[0] + ex[0][...]],
        [(_sds((16, Wm), F32), pl.BlockSpec((16, bnm), lambda i, j, k: (0, j)))],
        extras=[(b_cols, pl.BlockSpec((1, bnm), lambda i, j, k: (0, j)))], pre={0: _silu})[0]
    gm = allgather8("ag_mod", modp)
    mod_full = jnp.concatenate([gm[0], gm[2], gm[4], gm[6]], axis=1)
    mod_x = lax.dynamic_index_in_dim(mod_full, me, axis=0, keepdims=False)
    modv = jnp.stack([mod_full[8], mod_x]).reshape(2, N_MOD, 1, D)

    where = jnp.stack([ci, slot]).astype(jnp.int32)
    bufs = [cast_into_slot("cast_" + k, given[k][0], where) for k in ('ffn_wg', 'ffn_wu', 'ffn_wd', 'w_in', 'w_out')]
    wgG, wuG, wdG, winG, woutG = allgather_weights(bufs)

    loss_l, grad_x, dmod, small, big = local_step(
        dm, x[0], ctx[0], loss_target[0], modv, norm_g_f.reshape(3, 1, D), final_norm_g.reshape(1, D),
        q_norm_g, k_norm_g, conv_w_f, conv_b, lru_wa[0], ba_f.reshape(2, 1, D), lru_wx[0], bx_f.reshape(2, 1, D),
        lam_f.reshape(2, 1, D), wgG, wuG, wdG, winG, woutG.reshape(D, D))
    loss = lax.psum(loss_l[0, 0], ("x", "y", "c"))

    grads = {}
    gd = allgather8("ag_dmod", dmod.reshape(2, N_MOD * D))
    dM, g_bmod = dmod_pack(gd)
    dMc = lax.dynamic_slice(dM, (0, slot * Wm), (16, Wm))
    bmm = _pick(D, [512, 256, 128])
    grads['w_mod'] = fused_mm(
        "w_mod_dw", (D // bmm, Wm // bnm, 1),
        [(call16, pl.BlockSpec((16, bmm), lambda i, j, k: (0, i))), (dMc, pl.BlockSpec((16, bnm), lambda i, j, k: (0, j)))],
        [(0, 1, TN, 0)], [(bmm, bnm)], ident,
        [(_sds((D, Wm), F32), pl.BlockSpec((bmm, bnm), lambda i, j, k: (i, j)))], pre={0: _silu})[0][None]
    grads['b_mod'] = g_bmod

    def epi_cc(ids, accs, ex):
        v = ex[0][...]
        sg = _sig(v)
        return [accs[0] * (sg * (1.0 + v * (1.0 - sg)))]

    pcc = fused_mm(
        "c_ctx_partial", (1, D // bmm, Wm // bnm),
        [(dMc, pl.BlockSpec((16, bnm), lambda i, j, k: (0, k))), (w_mod[0], pl.BlockSpec((bmm, bnm), lambda i, j, k: (j, k)))],
        [(0, 1, NT, 0)], [(16, bmm)], epi_cc,
        [(_sds((16, D), F32), pl.BlockSpec((16, bmm), lambda i, j, k: (0, j)))],
        extras=[(c_ctx.reshape(1, D), pl.BlockSpec((1, bmm), lambda i, j, k: (0, j)))])[0]
    pcc_row = jnp.where(ci == 0, pcc[8], 0.0)

    order = ['lru_wa', 'lru_wx', 'q_norm_g', 'k_norm_g', 'conv_b', 'final_norm_g', 'norm_g', 'conv_w', 'lru_ba',
             'lru_bx', 'lru_lambda']
    flat = [small[k].reshape(-1) for k in order] + [pcc_row]
    sizes = [f.shape[0] for f in flat]
    tot = sum(sizes)
    LW = 1024
    padded = -(-tot // (16 * LW)) * (16 * LW)
    vec = jnp.concatenate(flat + [jnp.zeros((padded - tot,), F32)]).reshape(1, -1, LW)
    RS = vec.shape[1]

    tensors = big['ffn_wg'] + big['ffn_wu'] + big['ffn_wd'] + [big['w_in'], big['w_out'].reshape(NS, Ds, D), vec]
    gots = rs_sibling_exchange(tensors)
    pairs = [add_pair("rs_add%d" % t, g_, got_, where) for t, (g_, got_) in enumerate(zip(tensors, gots))]
    landeds = rs_chip_exchange([p_[0] for p_ in pairs], [p_[1] for p_ in pairs])
    dest_shapes = [(2, D, F4), (2, D, F4), (2, F4, D), (D, W4), (Ds, D), (RS, LW)]
    groups = [(0, 0), (0, 1), (1, 0), (1, 1), (2, 0), (2, 1), (3, None), (4, None), (5, None)]
    greds = [None] * len(dest_shapes)
    plan = []
    for t, (oi, li) in enumerate(groups):
        greds[oi] = sum_slots_into("rs_sum%d" % t, landeds[t], where, greds[oi], dest_shapes[oi], li)
        plan.append((oi, li, landeds[t].shape[1]))
    g_wg, g_wu, g_wd, g_win, g_wout, g_small = rs_sibling_share(greds, plan)
    grads.update(ffn_wg=g_wg[None], ffn_wu=g_wu[None], ffn_wd=g_wd[None], w_in=g_win[None], w_out=g_wout[None])
    summed = g_small.reshape(-1)
    offs = {}
    o = 0
    for k, n_ in zip(order + ['c_ctx'], sizes):
        offs[k] = summed[o:o + n_]
        o += n_
    shard = lambda k, rows: lax.dynamic_slice_in_dim(offs[k].reshape(rows, D), slot * Ds, Ds, axis=1)
    grads['c_ctx'] = offs['c_ctx']
    grads['q_norm_g'] = offs['q_norm_g'].reshape(1, HEAD_DIM)
    grads['k_norm_g'] = offs['k_norm_g'].reshape(1, HEAD_DIM)
    grads['conv_b'] = offs['conv_b'].reshape(1, D)
    grads['final_norm_g'] = offs['final_norm_g']
    grads['lru_wa'] = offs['lru_wa'].reshape(lru_wa.shape)
    grads['lru_wx'] = offs['lru_wx'].reshape(lru_wx.shape)
    grads['norm_g'] = shard('norm_g', 3)[None]
    grads['conv_w'] = shard('conv_w', 4)[None]
    grads['lru_ba'] = shard('lru_ba', 2)[None]
    grads['lru_bx'] = shard('lru_bx', 2)[None]
    grads['lru_lambda'] = shard('lru_lambda', 2)[None]

    big_names = ['w_mod', 'ffn_wg', 'ffn_wu', 'ffn_wd', 'w_in', 'w_out']
    delta, new_m, new_v = {}, {}, {}
    for k in big_names:
        delta[k], new_m[k], new_v[k] = adamw("adamw_" + k, given[k], grads[k], given['m_' + k], given['v_' + k])
    small_names = [k for k in names if k not in big_names]
    ssz = [given[k].size for k in small_names]
    stot = sum(ssz)
    spad = -(-stot // (8 * LW)) * (8 * LW)

    def packed(get):
        return jnp.concatenate([get(k).reshape(-1) for k in small_names] + [jnp.zeros((spad - stot,), F32)]).reshape(-1, LW)

    pw, pg, pm = packed(lambda k: given[k]), packed(lambda k: grads[k]), packed(lambda k: given['m_' + k])
    pv = jnp.concatenate([given['v_' + k].reshape(-1) for k in small_names] + [jnp.ones((spad - stot,), F32)]).reshape(-1, LW)
    sd, sm, sv = [t.reshape(-1) for t in adamw("adamw_small", pw, pg, pm, pv)]
    o = 0
    for k, n_ in zip(small_names, ssz):
        shp = given[k].shape
        delta[k], new_m[k], new_v[k] = sd[o:o + n_].reshape(shp), sm[o:o + n_].reshape(shp), sv[o:o + n_].reshape(shp)
        o += n_

    return (loss, grad_x[None], *[grads[k] for k in names], *[delta[k] for k in names],
            *[new_m[k] for k in names], *[new_v[k] for k in names])
```

```python
import functools

import jax
import jax.numpy as jnp
from jax import lax
from jax.experimental import pallas as pl
from jax.experimental.pallas import tpu as pltpu

F32 = jnp.float32
BF = jnp.bfloat16
EPS = 1e-6
HEAD_DIM = 128
GRID_W = 64
ROPE_THETA = 10000.0
LRU_C = 8.0
FFN_RES = 0.5
N_MOD = 9
ADAM_LR, ADAM_B1, ADAM_B2, ADAM_EPS, ADAM_WD, ADAM_STEP = 0.001, 0.9, 0.999, 1e-08, 0.01, 10
VMEM_LIMIT = 52 * 1024 * 1024
MESH = pl.DeviceIdType.MESH
ANY = pl.BlockSpec(memory_space=pl.ANY)


def _sds(shape, dt):
    return jax.ShapeDtypeStruct(tuple(shape), dt)


def _pick(n, cands):
    for c in cands:
        if n % c == 0:
            return c
    return n


def _cparams(**kw):
    return pltpu.CompilerParams(vmem_limit_bytes=VMEM_LIMIT, **kw)


def _sig(x):
    return 1.0 / (1.0 + jnp.exp(-x))


def _gelu(x):
    t = jnp.tanh(0.7978845608028654 * (x + 0.044715 * x * x * x))
    return 0.5 * x * (1.0 + t), t


def _gelu_grad(x, t):
    return 0.5 * (1.0 + t) + 0.5 * x * (1.0 - t * t) * 0.7978845608028654 * (1.0 + 3.0 * 0.044715 * x * x)


class Comm:
    def __init__(self, reads, lands, n_sem, start, finish):
        self.reads, self.lands, self.n_sem, self.start, self.finish = list(reads), list(lands), n_sem, start, finish


def hosted_call(body, *, name, grid, in_specs, out_specs, out_shape, args, scratch_shapes=(), aliases=None, comm=None):
    aliases = dict(aliases or {})
    if comm is None:
        return pl.pallas_call(
            body, name=name, grid=grid, in_specs=list(in_specs), out_specs=list(out_specs), out_shape=list(out_shape),
            scratch_shapes=list(scratch_shapes), input_output_aliases=aliases, compiler_params=_cparams())(*args)
    n_in, n_out, n_sc = len(args), len(out_shape), len(scratch_shapes)
    land_in = [(t, l) for t, l in enumerate(comm.lands) if not isinstance(l, jax.ShapeDtypeStruct)]
    nr, nli, nl = len(comm.reads), len(land_in), len(comm.lands)

    def wrapped(*refs):
        a = refs[:n_in]
        r = refs[n_in:n_in + nr]
        pos = n_in + nr + nli
        o = refs[pos:pos + n_out]
        lo = refs[pos + n_out:pos + n_out + nl]
        sc = refs[pos + n_out + nl:pos + n_out + nl + n_sc]
        send_sems, recv_sems = refs[pos + n_out + nl + n_sc:]
        ids = [pl.program_id(d) for d in range(len(grid))]
        first, last = ids[0] == 0, ids[0] == grid[0] - 1
        for d in range(1, len(grid)):
            first = first & (ids[d] == 0)
            last = last & (ids[d] == grid[d] - 1)

        @pl.when(first)
        def _():
            comm.start(r, lo, send_sems, recv_sems)

        body(*a, *o, *sc)

        @pl.when(last)
        def _():
            comm.finish(r, lo, send_sems, recv_sems)

    for q, (t, _) in enumerate(land_in):
        aliases[n_in + nr + q] = n_out + t
    res = pl.pallas_call(
        wrapped, name=name, grid=grid,
        in_specs=list(in_specs) + [ANY] * (nr + nli), out_specs=list(out_specs) + [ANY] * nl,
        out_shape=list(out_shape) + [l if isinstance(l, jax.ShapeDtypeStruct) else _sds(l.shape, l.dtype) for l in comm.lands],
        scratch_shapes=list(scratch_shapes) + [pltpu.SemaphoreType.DMA((comm.n_sem,)), pltpu.SemaphoreType.DMA((comm.n_sem,))],
        input_output_aliases=aliases, compiler_params=_cparams(),
    )(*args, *comm.reads, *[l for _, l in land_in])
    return list(res[:n_out]), list(res[n_out:])


def comm_call(name, comm):
    def body():
        pass

    return hosted_call(body, name=name, grid=(1,), in_specs=[], out_specs=[], out_shape=[], args=[], comm=comm)[1]


def ew_call(name, grid, fn, ins, outs, first=None, aliases=None, comm=None):
    n_in = len(ins)

    def body(*refs):
        ids = tuple(pl.program_id(a) for a in range(len(grid)))
        vals = fn(ids, *refs[:n_in])
        for (_, _, acc), o_ref, v in zip(outs, refs[n_in:], vals):
            if not acc:
                o_ref[...] = v.astype(o_ref.dtype)
            else:
                is_first = first(ids)

                @pl.when(is_first)
                def _(o_ref=o_ref, v=v):
                    o_ref[...] = v.astype(o_ref.dtype)

                @pl.when(jnp.logical_not(is_first))
                def _(o_ref=o_ref, v=v):
                    o_ref[...] += v.astype(o_ref.dtype)

    return hosted_call(body, name=name, grid=grid, in_specs=[s for _, s in ins], out_specs=[s for _, s, _ in outs],
                       out_shape=[o for o, _, _ in outs], args=[a for a, _ in ins], aliases=aliases, comm=comm)


def fused_mm(name, grid, ins, prods, acc_shapes, epi, outs, extras=(), pre=None, comm=None):
    n_in, n_ex, n_out = len(ins), len(extras), len(outs)
    nk = grid[-1]
    pre = pre or {}

    def body(*refs):
        in_refs = refs[:n_in]
        ex_refs = refs[n_in:n_in + n_ex]
        out_refs = refs[n_in + n_ex:n_in + n_ex + n_out]
        accs = refs[n_in + n_ex + n_out:]
        ids = tuple(pl.program_id(a) for a in range(len(grid)))
        k = ids[-1]

        @pl.when(k == 0)
        def _():
            for a in accs:
                a[...] = jnp.zeros(a.shape, F32)

        loaded = {}

        def operand(i):
            if i not in loaded:
                v = in_refs[i][...]
                if i in pre:
                    v = pre[i](v)
                loaded[i] = v.astype(BF)
            return loaded[i]

        for ia, ib, dims, ai in prods:
            accs[ai][...] += lax.dot_general(operand(ia), operand(ib), (dims, ((), ())), preferred_element_type=F32)

        @pl.when(k == nk - 1)
        def _():
            vals = epi(ids, [a[...] for a in accs], ex_refs)
            for o_ref, v in zip(out_refs, vals):
                o_ref[...] = v.astype(o_ref.dtype)

    return hosted_call(
        body, name=name, grid=grid, in_specs=[s for _, s in ins] + [s for _, s in extras],
        out_specs=[s for _, s in outs], out_shape=[o for o, _ in outs],
        scratch_shapes=[pltpu.VMEM(s, F32) for s in acc_shapes],
        args=[a for a, _ in ins] + [a for a, _ in extras], comm=comm)


NN = ((1,), (0,))
NT = ((1,), (1,))
TN = ((0,), (0,))


class Dims:
    def __init__(self, S, C, D, F4, W4, NS, LB):
        self.S, self.C, self.D, self.F4, self.W4, self.NS, self.LB = S, C, D, F4, W4, NS, LB
        self.T = S + C
        self.DFF = F4 * NS
        self.INW = W4 * NS
        self.NQ = D // HEAD_DIM
        self.KVW = (self.INW - 5 * D) // 2
        self.NKV = self.KVW // HEAD_DIM
        self.G = self.NQ // self.NKV
        self.OFF_K = D
        self.OFF_V = D + self.KVW
        self.OFF_LX = D + 2 * self.KVW
        self.OFF_LG = self.OFF_LX + D
        self.OFF_GA = self.OFF_LG + D
        self.OFF_GL = self.OFF_GA + D
        self.bm = _pick(C, [256, 128, 64, 32, 16, 8])
        self.nCb = C // self.bm
        self.nTb = self.T // self.bm
        self.nSb = S // self.bm
        self.mT = _pick(self.T, [544, 512, 384, 256, 128])
        self.mS = _pick(S, [512, 256, 128])
        self.cw = _pick(D, [1024, 512, 256, 128]) if (self.OFF_LX % 1024 == 0 and D % 1024 == 0) else _pick(
            self.OFF_LX, [512, 256, 128])
        self.nsub = 2 if (W4 % 256 == 0 and W4 >= 512) else 1
        self.wb = W4 // self.nsub
        self.LBD = D // LB
        self.bq = _pick(C, [256, 128]) if S % _pick(C, [256, 128]) == 0 else 128


def rope_tables(dm):
    rows = dm.S // GRID_W
    row = jnp.repeat(jnp.arange(rows, dtype=F32), GRID_W)
    col = jnp.tile(jnp.arange(GRID_W, dtype=F32), rows)
    axis_dims = HEAD_DIM // 2
    freqs = ROPE_THETA ** (-jnp.arange(0, axis_dims, 2, dtype=F32) / axis_dims)
    ang = jnp.concatenate([row[:, None] * freqs, col[:, None] * freqs], axis=-1)
    cos = jnp.repeat(jnp.cos(ang), 2, axis=-1)
    sin = jnp.repeat(jnp.sin(ang), 2, axis=-1)
    sign = jnp.tile(jnp.array([-1.0, 1.0], F32), HEAD_DIM // 2)
    sin = sin * sign
    cos = jnp.concatenate([jnp.ones((dm.C, HEAD_DIM), F32), cos], axis=0)
    sin = jnp.concatenate([jnp.zeros((dm.C, HEAD_DIM), F32), sin], axis=0)
    return cos, sin


def _pair_swap(y):
    lane = lax.broadcasted_iota(jnp.int32, y.shape, 1)
    nxt = pltpu.roll(y, y.shape[1] - 1, 1)
    prv = pltpu.roll(y, 1, 1)
    return jnp.where((lane & 1) == 0, nxt, prv)


def normmod_fwd(name, dm, x, norm_g3, stage, modv, rows_T):
    D, bm = dm.D, dm.bm
    nb = dm.nTb if rows_T else dm.nSb
    typ = (lambda i: jnp.where(i < dm.nCb, 0, 1)) if rows_T else (lambda i: 1)

    def fn(ids, x_ref, g_ref, sh_ref, sc_ref):
        xv = x_ref[...]
        r = lax.rsqrt(jnp.mean(xv * xv, axis=-1, keepdims=True) + EPS)
        n = xv * r * g_ref[...]
        return [n * (1.0 + sc_ref[...]) + sh_ref[...]]

    return ew_call(
        name, (nb,), fn,
        [(x, pl.BlockSpec((bm, D), lambda i: (i, 0))),
         (norm_g3, pl.BlockSpec((None, 1, D), lambda i: (stage, 0, 0))),
         (modv, pl.BlockSpec((None, None, 1, D), lambda i: (typ(i), 3 * stage, 0, 0))),
         (modv, pl.BlockSpec((None, None, 1, D), lambda i: (typ(i), 3 * stage + 1, 0, 0)))],
        [(_sds(x.shape, BF), pl.BlockSpec((bm, D), lambda i: (i, 0)), False)])[0]


def normmod_bwd(name, dm, dh, x, dres, norm_g3, stage, modv, rows_T, dres_lat_only):
    D, bm = dm.D, dm.bm
    nb = dm.nTb if rows_T else dm.nSb
    nCb = dm.nCb
    typ = (lambda i: jnp.where(i < nCb, 0, 1)) if rows_T else (lambda i: 1)
    if dres_lat_only:
        dres_map = lambda i: (jnp.maximum(i - nCb, 0), 0)
    else:
        dres_map = lambda i: (i, 0)

    def fn(ids, dh_ref, x_ref, dres_ref, g_ref, sc_ref):
        i = ids[0]
        xv = x_ref[...]
        dhv = dh_ref[...].astype(F32)
        r = lax.rsqrt(jnp.mean(xv * xv, axis=-1, keepdims=True) + EPS)
        xn = xv * r
        g = g_ref[...]
        n = xn * g
        dn = dhv * (1.0 + sc_ref[...])
        dxn = dn * g
        dx = r * (dxn - xn * jnp.mean(dxn * xn, axis=-1, keepdims=True))
        dresv = dres_ref[...]
        if dres_lat_only:
            dresv = jnp.where(i >= nCb, dresv, 0.0)
        dsh = jnp.sum(dhv, axis=0, keepdims=True)
        dsc = jnp.sum(dhv * n, axis=0, keepdims=True)
        dg = jnp.sum(dn * xn, axis=0, keepdims=True)
        return [dx + dresv, dsh, dsc, dg]

    if rows_T:
        first = lambda ids: (ids[0] == 0) | (ids[0] == nCb)
    else:
        first = lambda ids: ids[0] == 0
    acc = (_sds((2, 1, D), F32), pl.BlockSpec((None, 1, D), lambda i: (typ(i), 0, 0)), True)
    return ew_call(
        name, (nb,), fn,
        [(dh, pl.BlockSpec((bm, D), lambda i: (i, 0))),
         (x, pl.BlockSpec((bm, D), lambda i: (i, 0))),
         (dres, pl.BlockSpec((bm, D), dres_map)),
         (norm_g3, pl.BlockSpec((None, 1, D), lambda i: (stage, 0, 0))),
         (modv, pl.BlockSpec((None, None, 1, D), lambda i: (typ(i), 3 * stage + 1, 0, 0)))],
        [(_sds(x.shape, F32), pl.BlockSpec((bm, D), lambda i: (i, 0)), False), acc, acc, acc], first=first)


def gate_bwd(name, dm, dx, f, modv, gidx, scale, rows_T):
    D, bm = dm.D, dm.bm
    nb = dm.nTb if rows_T else dm.nSb
    nCb = dm.nCb
    typ = (lambda i: jnp.where(i < nCb, 0, 1)) if rows_T else (lambda i: 1)

    def fn(ids, dx_ref, f_ref, g_ref):
        dxv = dx_ref[...]
        return [scale * g_ref[...] * dxv, jnp.sum(scale * f_ref[...].astype(F32) * dxv, axis=0, keepdims=True)]

    if rows_T:
        first = lambda ids: (ids[0] == 0) | (ids[0] == nCb)
    else:
        first = lambda ids: ids[0] == 0
    return ew_call(
        name, (nb,), fn,
        [(dx, pl.BlockSpec((bm, D), lambda i: (i, 0))),
         (f, pl.BlockSpec((bm, D), lambda i: (i, 0))),
         (modv, pl.BlockSpec((None, None, 1, D), lambda i: (typ(i), gidx, 0, 0)))],
        [(_sds(dx.shape, BF), pl.BlockSpec((bm, D), lambda i: (i, 0)), False),
         (_sds((2, 1, D), F32), pl.BlockSpec((None, 1, D), lambda i: (typ(i), 0, 0)), True)], first=first)


def ffn_fwd(name, dm, h, xres, wg, wu, wd, modv, gidx, rows_T, comm_up=None, comm_down=None):
    D, F4, NS = dm.D, dm.F4, dm.NS
    M = h.shape[0]
    bm = dm.mT if rows_T else dm.mS
    bk = _pick(D, [512, 256, 128])
    C = dm.C

    def epi_up(ids, accs, ex):
        a, u = accs
        return [a, u, a * _sig(a) * u]

    hspec = pl.BlockSpec((bm, bk), lambda i, j, k: (i, k))
    wspec = pl.BlockSpec((None, bk, F4), lambda i, j, k: (j, k, 0))
    ospec = pl.BlockSpec((bm, F4), lambda i, j, k: (i, j))
    res = fused_mm(
        name + "_up", (M // bm, NS, D // bk), [(h, hspec), (wg, wspec), (wu, wspec)],
        [(0, 1, NN, 0), (0, 2, NN, 1)], [(bm, F4), (bm, F4)], epi_up,
        [(_sds((M, dm.DFF), BF), ospec)] * 3, comm=comm_up)
    (a, u, s), land_up = res if comm_up is not None else (res, None)

    bn = _pick(D, [1024, 512, 256, 128])

    def epi_dn(ids, accs, ex):
        f = accs[0]
        if rows_T:
            row = ids[0] * bm + lax.broadcasted_iota(jnp.int32, (bm, 1), 0)
            gate = jnp.where(row < C, ex[1][...], ex[2][...])
        else:
            gate = ex[2][...]
        return [ex[0][...] + FFN_RES * gate * f, f]

    gspec = lambda t: pl.BlockSpec((None, None, 1, bn), lambda i, j, k: (t, gidx, 0, j))
    res = fused_mm(
        name + "_down", (M // bm, D // bn, NS),
        [(s, pl.BlockSpec((bm, F4), lambda i, j, k: (i, k))),
         (wd, pl.BlockSpec((None, F4, bn), lambda i, j, k: (k, 0, j)))],
        [(0, 1, NN, 0)], [(bm, bn)], epi_dn,
        [(_sds((M, D), F32), pl.BlockSpec((bm, bn), lambda i, j, k: (i, j))),
         (_sds((M, D), BF), pl.BlockSpec((bm, bn), lambda i, j, k: (i, j)))],
        extras=[(xres, pl.BlockSpec((bm, bn), lambda i, j, k: (i, j))), (modv, gspec(0)), (modv, gspec(1))],
        comm=comm_down)
    (xo, f), land_down = res if comm_down is not None else (res, None)
    return xo, a, u, s, f, land_up, land_down


def ffn_bwd(name, dm, df, h, a, u, s, wg, wu, wd, rows_T, comms=None):
    comms = comms or {}
    landed = {}

    def run(key, *args, **kw):
        res = fused_mm(*args, comm=comms.get(key), **kw)
        if key in comms:
            res, landed[key] = res
        return res

    D, F4, NS = dm.D, dm.F4, dm.NS
    M = h.shape[0]
    bm = dm.mT if rows_T else dm.mS
    bk = _pick(D, [512, 256, 128])

    def epi_ds(ids, accs, ex):
        ds = accs[0]
        av = ex[0][...].astype(F32)
        uv = ex[1][...].astype(F32)
        sg = _sig(av)
        return [ds * uv * (sg * (1.0 + av * (1.0 - sg))), ds * av * sg]

    ospec = pl.BlockSpec((bm, F4), lambda i, j, k: (i, j))
    da, du = run(
        'ds', name + "_ds", (M // bm, NS, D // bk),
        [(df, pl.BlockSpec((bm, bk), lambda i, j, k: (i, k))),
         (wd, pl.BlockSpec((None, F4, bk), lambda i, j, k: (j, 0, k)))],
        [(0, 1, NT, 0)], [(bm, F4)], epi_ds, [(_sds((M, dm.DFF), BF), ospec)] * 2,
        extras=[(a, ospec), (u, ospec)])

    ident = lambda ids, accs, ex: list(accs)
    bn = _pick(D, [1024, 512, 256, 128])
    dwd = run(
        'dwd', name + "_dwd", (NS, D // bn, M // bm),
        [(s, pl.BlockSpec((bm, F4), lambda i, j, k: (k, i))),
         (df, pl.BlockSpec((bm, bn), lambda i, j, k: (k, j)))],
        [(0, 1, TN, 0)], [(F4, bn)], ident,
        [(_sds((NS, F4, D), BF), pl.BlockSpec((None, F4, bn), lambda i, j, k: (i, 0, j)))])[0]

    dwg, dwu = run(
        'dwgu', name + "_dwgu", (D // bn, NS, M // bm),
        [(h, pl.BlockSpec((bm, bn), lambda i, j, k: (k, i))),
         (da, pl.BlockSpec((bm, F4), lambda i, j, k: (k, j))),
         (du, pl.BlockSpec((bm, F4), lambda i, j, k: (k, j)))],
        [(0, 1, TN, 0), (0, 2, TN, 1)], [(bn, F4), (bn, F4)], ident,
        [(_sds((NS, D, F4), BF), pl.BlockSpec((None, bn, F4), lambda i, j, k: (j, i, 0)))] * 2)

    dh = run(
        'dh', name + "_dh", (M // bm, D // bn, NS),
        [(da, pl.BlockSpec((bm, F4), lambda i, j, k: (i, k))),
         (wg, pl.BlockSpec((None, bn, F4), lambda i, j, k: (k, j, 0))),
         (du, pl.BlockSpec((bm, F4), lambda i, j, k: (i, k))),
         (wu, pl.BlockSpec((None, bn, F4), lambda i, j, k: (k, j, 0)))],
        [(0, 1, NT, 0), (2, 3, NT, 0)], [(bm, bn)], ident,
        [(_sds((M, D), F32), pl.BlockSpec((bm, bn), lambda i, j, k: (i, j)))])[0]
    return dh, dwg, dwu, dwd, landed


def qk_prep(dm, P, gq, gk, cosf, sinf):
    D, KVW, bm = dm.D, dm.KVW, dm.bm

    def head_norm_rope(xh, g, c, s):
        r = lax.rsqrt(jnp.mean(xh * xh, axis=-1, keepdims=True) + EPS)
        y = xh * r * g
        return y * c + _pair_swap(y) * s

    def fn(ids, q_ref, k_ref, v_ref, gq_ref, gk_ref, c_ref, s_ref):
        c, s = c_ref[...], s_ref[...]
        qs = [head_norm_rope(q_ref[:, h * HEAD_DIM:(h + 1) * HEAD_DIM], gq_ref[...], c, s) for h in range(dm.NQ)]
        ks = [head_norm_rope(k_ref[:, h * HEAD_DIM:(h + 1) * HEAD_DIM], gk_ref[...], c, s) for h in range(dm.NKV)]
        return [jnp.concatenate(qs, axis=1), jnp.concatenate(ks, axis=1), v_ref[...]]

    hspec = pl.BlockSpec((bm, HEAD_DIM), lambda i: (i, 0))
    vec = pl.BlockSpec((1, HEAD_DIM), lambda i: (0, 0))
    return ew_call(
        "qk_prep", (dm.nTb,), fn,
        [(P, pl.BlockSpec((bm, D), lambda i: (i, 0))),
         (P, pl.BlockSpec((bm, KVW), lambda i: (i, dm.OFF_K // KVW))),
         (P, pl.BlockSpec((bm, KVW), lambda i: (i, dm.OFF_V // KVW))),
         (gq, vec), (gk, vec), (cosf, hspec), (sinf, hspec)],
        [(_sds((dm.T, D), BF), pl.BlockSpec((bm, D), lambda i: (i, 0)), False),
         (_sds((dm.T, KVW), BF), pl.BlockSpec((bm, KVW), lambda i: (i, 0)), False),
         (_sds((dm.T, KVW), BF), pl.BlockSpec((bm, KVW), lambda i: (i, 0)), False)])


def qk_prep_bwd(dm, dq, dk, dv, P, gq, gk, cosf, sinf, dP):
    D, KVW, bm, nCb = dm.D, dm.KVW, dm.bm, dm.nCb
    W = D + 2 * KVW

    def head_bwd(d, xh, g, c, s):
        dy = d * c - _pair_swap(d) * s
        r = lax.rsqrt(jnp.mean(xh * xh, axis=-1, keepdims=True) + EPS)
        xn = xh * r
        dg = jnp.sum(dy * xn, axis=0, keepdims=True)
        dxn = dy * g
        return r * (dxn - xn * jnp.mean(dxn * xn, axis=-1, keepdims=True)), dg

    def fn(ids, dq_ref, dk_ref, dv_ref, q_ref, k_ref, gq_ref, gk_ref, c_ref, s_ref, dp_any):
        i = ids[0]
        c, s = c_ref[...], s_ref[...]
        lat = i >= nCb
        outs, dgq = [], jnp.zeros((1, HEAD_DIM), F32)
        for h in range(dm.NQ):
            sl = slice(h * HEAD_DIM, (h + 1) * HEAD_DIM)
            d = jnp.where(lat, dq_ref[:, sl], 0.0)
            dx, dg = head_bwd(d, q_ref[:, sl], gq_ref[...], c, s)
            outs.append(dx)
            dgq = dgq + dg
        dgk = jnp.zeros((1, HEAD_DIM), F32)
        for h in range(dm.NKV):
            sl = slice(h * HEAD_DIM, (h + 1) * HEAD_DIM)
            dx, dg = head_bwd(dk_ref[:, sl], k_ref[:, sl], gk_ref[...], c, s)
            outs.append(dx)
            dgk = dgk + dg
        outs.append(dv_ref[...])
        return [jnp.concatenate(outs, axis=1), dgq, dgk]

    hspec = pl.BlockSpec((bm, HEAD_DIM), lambda i: (i, 0))
    vec = pl.BlockSpec((1, HEAD_DIM), lambda i: (0, 0))
    return ew_call(
        "qk_prep_bwd", (dm.nTb,), fn,
        [(dq, pl.BlockSpec((bm, D), lambda i: (jnp.maximum(i - nCb, 0), 0))),
         (dk, pl.BlockSpec((bm, KVW), lambda i: (i, 0))),
         (dv, pl.BlockSpec((bm, KVW), lambda i: (i, 0))),
         (P, pl.BlockSpec((bm, D), lambda i: (i, 0))),
         (P, pl.BlockSpec((bm, KVW), lambda i: (i, dm.OFF_K // KVW))),
         (gq, vec), (gk, vec), (cosf, hspec), (sinf, hspec), (dP, ANY)],
        [(_sds(dP.shape, BF), pl.BlockSpec((bm, W), lambda i: (i, 0)), False),
         (_sds((1, HEAD_DIM), F32), vec, True), (_sds((1, HEAD_DIM), F32), vec, True)],
        first=lambda ids: ids[0] == 0, aliases={9: 0})


def attention_fwd(dm, qr, kr, vb, comm=None):
    S, T, D, G, nCb = dm.S, dm.T, dm.D, dm.G, dm.nCb
    bq = dm.bq
    off = dm.C // bq
    scale = HEAD_DIM ** -0.5
    GW = G * HEAD_DIM

    def body(q_ref, k_ref, v_ref, o_ref):
        k = k_ref[...]
        v = v_ref[...]
        for h in range(G):
            sl = slice(h * HEAD_DIM, (h + 1) * HEAD_DIM)
            s = lax.dot_general(q_ref[:, sl], k, (NT, ((), ())), preferred_element_type=F32) * scale
            m = jnp.max(s, axis=-1, keepdims=True)
            p = jnp.exp(s - m)
            l = jnp.sum(p, axis=-1, keepdims=True)
            o = lax.dot_general(p.astype(BF), v, (NN, ((), ())), preferred_element_type=F32)
            o_ref[:, sl] = o / l

    return hosted_call(
        body, grid=(dm.NKV, S // bq), name="attn_fwd",
        in_specs=[pl.BlockSpec((bq, GW), lambda g, i: (i + off, g)),
                  pl.BlockSpec((T, HEAD_DIM), lambda g, i: (0, g)),
                  pl.BlockSpec((T, HEAD_DIM), lambda g, i: (0, g))],
        out_specs=[pl.BlockSpec((bq, GW), lambda g, i: (i, g))],
        out_shape=[_sds((S, D), F32)], args=[qr, kr, vb], comm=comm)


def attention_bwd(dm, qr, kr, vb, dattn, comm=None):
    S, T, D, G = dm.S, dm.T, dm.D, dm.G
    bq = dm.bq
    off = dm.C // bq
    scale = HEAD_DIM ** -0.5
    GW = G * HEAD_DIM

    def body(q_ref, k_ref, v_ref, do_ref, dq_ref, dk_ref, dv_ref):
        i = pl.program_id(1)

        @pl.when(i == 0)
        def _():
            dk_ref[...] = jnp.zeros(dk_ref.shape, F32)
            dv_ref[...] = jnp.zeros(dv_ref.shape, F32)

        k = k_ref[...]
        v = v_ref[...]
        for h in range(G):
            sl = slice(h * HEAD_DIM, (h + 1) * HEAD_DIM)
            q = q_ref[:, sl]
            do = do_ref[:, sl]
            s = lax.dot_general(q, k, (NT, ((), ())), preferred_element_type=F32) * scale
            m = jnp.max(s, axis=-1, keepdims=True)
            e = jnp.exp(s - m)
            p = e / jnp.sum(e, axis=-1, keepdims=True)
            pb = p.astype(BF)
            dv_ref[...] += lax.dot_general(pb, do, (TN, ((), ())), preferred_element_type=F32)
            dp = lax.dot_general(do, v, (NT, ((), ())), preferred_element_type=F32)
            ds = p * (dp - jnp.sum(p * dp, axis=-1, keepdims=True)) * scale
            dsb = ds.astype(BF)
            dq_ref[:, sl] = lax.dot_general(dsb, k, (NN, ((), ())), preferred_element_type=F32)
            dk_ref[...] += lax.dot_general(dsb, q, (TN, ((), ())), preferred_element_type=F32)

    return hosted_call(
        body, grid=(dm.NKV, S // bq), name="attn_bwd",
        in_specs=[pl.BlockSpec((bq, GW), lambda g, i: (i + off, g)),
                  pl.BlockSpec((T, HEAD_DIM), lambda g, i: (0, g)),
                  pl.BlockSpec((T, HEAD_DIM), lambda g, i: (0, g)),
                  pl.BlockSpec((bq, GW), lambda g, i: (i + off, g))],
        out_specs=[pl.BlockSpec((bq, GW), lambda g, i: (i, g)),
                   pl.BlockSpec((T, HEAD_DIM), lambda g, i: (0, g)),
                   pl.BlockSpec((T, HEAD_DIM), lambda g, i: (0, g))],
        out_shape=[_sds((S, D), F32), _sds((T, dm.KVW), F32), _sds((T, dm.KVW), F32)],
        args=[qr, kr, vb, dattn], comm=comm)


def _conv_taps(dm, lx, masks_only=False):
    T, C = dm.T, dm.C
    t = lax.broadcasted_iota(jnp.int32, (T, 1), 0)
    valid = [(t >= 2) & ((t < C) | (t >= C + 2)), (t >= 1) & ((t < C) | (t >= C + 1)), None,
             (t != C - 1) & (t != T - 1)]
    shifts = [2, 1, 0, T - 1]
    taps = []
    for k in range(4):
        if k == 2:
            taps.append(lx)
        else:
            taps.append(jnp.where(valid[k], pltpu.roll(lx, shifts[k], 0), 0.0))
    return taps


def _scan_tiles(dm, asc, split, a_ref, u_ref, out_ref, mode):
    T, C = dm.T, dm.C
    nT, nC = T // 8, C // 8
    row = lax.broadcasted_iota(jnp.int32, (8, HEAD_DIM), 0)

    def tile_of(i):
        if not split:
            return i if asc else nT - 1 - i
        if asc:
            return jnp.where(i < nT - nC, nC + i, i - (nT - nC))
        return jnp.where(i < nC, nC - 1 - i, nT - 1 - (i - nC))

    def body(i, carry):
        off = pl.multiple_of(tile_of(i) * 8, 8)
        a = a_ref[pl.ds(off, 8), :]
        b = u_ref[pl.ds(off, 8), :]
        if mode == 'lam':
            if asc:
                coef = jnp.where(row == 0, 1.0, pltpu.roll(a, 1, 0))
            else:
                coef = jnp.where(row == 7, 1.0, pltpu.roll(a, 7, 0))
        else:
            coef = a
        A, B = coef, b
        for d in (1, 2, 4):
            if asc:
                ok = row >= d
                A_sh = jnp.where(ok, pltpu.roll(A, d, 0), 1.0)
                B_sh = jnp.where(ok, pltpu.roll(B, d, 0), 0.0)
            else:
                ok = row < 8 - d
                A_sh = jnp.where(ok, pltpu.roll(A, 8 - d, 0), 1.0)
                B_sh = jnp.where(ok, pltpu.roll(B, 8 - d, 0), 0.0)
            B = B + A * B_sh
            A = A * A_sh
        h = A * carry + B
        out_ref[pl.ds(off, 8), :] = h
        last = h[7:8, :] if asc else h[0:1, :]
        if mode == 'lam':
            last = last * (a[7:8, :] if asc else a[0:1, :])
        return jnp.broadcast_to(last, (8, HEAD_DIM))

    lax.fori_loop(0, nT, body, jnp.zeros((8, HEAD_DIM), F32))


def _lru_gates(xc, wa, ba, wx, bx, sp):
    xb = xc.astype(BF)
    r = _sig(jnp.dot(xb, wa, preferred_element_type=F32) + ba)
    i = _sig(jnp.dot(xb, wx, preferred_element_type=F32) + bx)
    a = jnp.exp(-LRU_C * r * sp)
    m = jnp.sqrt(1.0 - a * a)
    return r, i, a, m


def lru_fwd(dm, P, conv_w, conv_b, wa, ba, wx, bx, sp, direction):
    T, D, LB = dm.T, dm.D, dm.LB
    W = dm.LBD
    R = _pick(T, [272, 256, 128, 64, 8])
    lxb = dm.OFF_LX // W

    def body(lx_ref, cw_ref, cb_ref, wa_ref, ba_ref, wx_ref, bx_ref, sp_ref, h_ref, xc_ref, a_ref):
        taps = _conv_taps(dm, lx_ref[...])
        xc = cb_ref[...]
        for k in range(4):
            xc = xc + taps[k] * cw_ref[k:k + 1, :]
        xc_ref[...] = xc
        wa_, wx_ = wa_ref[...].astype(BF), wx_ref[...].astype(BF)

        def chunk(ci, _):
            off = pl.multiple_of(ci * R, 8)
            x = xc_ref[pl.ds(off, R), :]
            r, i, a, m = _lru_gates(x, wa_, ba_ref[...], wx_, bx_ref[...], sp_ref[...])
            a_ref[pl.ds(off, R), :] = a
            h_ref[pl.ds(off, R), :] = m * i * x
            return 0

        lax.fori_loop(0, T // R, chunk, 0)
        _scan_tiles(dm, direction == 0, direction == 1, a_ref, h_ref, h_ref, 'h')

    strip = lambda j: (0, j)
    vec = pl.BlockSpec((None, 1, W), lambda j: (direction, 0, j))
    mat = pl.BlockSpec((None, None, W, W), lambda j: (direction, j, 0, 0))
    return pl.pallas_call(
        body, grid=(LB,), name="lru_fwd%d" % direction,
        in_specs=[pl.BlockSpec((T, W), lambda j: (0, lxb + j)),
                  pl.BlockSpec((4, W), strip), pl.BlockSpec((1, W), strip), mat, vec, mat, vec, vec],
        out_specs=pl.BlockSpec((T, W), strip), out_shape=_sds((T, D), F32),
        scratch_shapes=[pltpu.VMEM((T, W), F32), pltpu.VMEM((T, W), F32)], compiler_params=_cparams(),
    )(P, conv_w, conv_b, wa, ba, wx, bx, sp)


def lru_bwd(dm, P, dh, h, conv_w, conv_b, wa, ba, wx, bx, sp, sg, direction, dxc_in, dP):
    T, C, D, LB = dm.T, dm.C, dm.D, dm.LB
    W = dm.LBD
    R = _pick(T, [272, 256, 128, 64, 8])
    lxb = dm.OFF_LX // W
    last = direction == 1

    def body(*refs):
        (lx_ref, dh_ref, h_ref, cw_ref, cb_ref, wa_ref, ba_ref, wx_ref, bx_ref, sp_ref, sg_ref) = refs[:11]
        pos = 11
        if last:
            dxin_ref, _dp_any = refs[pos:pos + 2]
            pos += 2
        out0 = refs[pos]
        dwa_ref, dba_ref, dwx_ref, dbx_ref, dlam_ref = refs[pos + 1:pos + 6]
        pos += 6
        if last:
            dcw_ref, dcb_ref = refs[pos:pos + 2]
            pos += 2
        xc_ref, a_ref, lam_ref, hp_ref, dxc_ref = refs[pos:pos + 5]

        lx = lx_ref[...]
        taps = _conv_taps(dm, lx)
        xc = cb_ref[...]
        for k in range(4):
            xc = xc + taps[k] * cw_ref[k:k + 1, :]
        xc_ref[...] = xc
        wa_, wx_ = wa_ref[...].astype(BF), wx_ref[...].astype(BF)

        def chunk_a(ci, _):
            off = pl.multiple_of(ci * R, 8)
            r, i, a, m = _lru_gates(xc_ref[pl.ds(off, R), :], wa_, ba_ref[...], wx_, bx_ref[...], sp_ref[...])
            a_ref[pl.ds(off, R), :] = a
            return 0

        lax.fori_loop(0, T // R, chunk_a, 0)
        _scan_tiles(dm, direction == 1, direction == 1, a_ref, dh_ref, lam_ref, 'lam')
        t = lax.broadcasted_iota(jnp.int32, (T, 1), 0)
        hv = h_ref[...]
        if direction == 0:
            hp_ref[...] = jnp.where(t == 0, 0.0, pltpu.roll(hv, 1, 0))
        else:
            nxt = pltpu.roll(hv, T - 1, 0)
            hp_ref[...] = jnp.where(t == C - 1, 0.0, jnp.where(t == T - 1, jnp.broadcast_to(hv[0:1, :], hv.shape), nxt))

        def chunk_b(ci, carry):
            dwa, dwx, dba, dbx, dlam = carry
            off = pl.multiple_of(ci * R, 8)
            x = xc_ref[pl.ds(off, R), :]
            r, i, a, m = _lru_gates(x, wa_, ba_ref[...], wx_, bx_ref[...], sp_ref[...])
            lam = lam_ref[pl.ds(off, R), :]
            ix = i * x
            da = lam * hp_ref[pl.ds(off, R), :] - lam * ix * a / m
            dloga = da * a
            dza = dloga * (-LRU_C) * sp_ref[...] * r * (1.0 - r)
            dzx = lam * m * x * i * (1.0 - i)
            dzab, dzxb = dza.astype(BF), dzx.astype(BF)
            xb = x.astype(BF)
            dxc = lam * m * i
            dxc = dxc + lax.dot_general(dzab, wa_, (NT, ((), ())), preferred_element_type=F32)
            dxc = dxc + lax.dot_general(dzxb, wx_, (NT, ((), ())), preferred_element_type=F32)
            dxc_ref[pl.ds(off, R), :] = dxc
            dwa = dwa + lax.dot_general(xb, dzab, (TN, ((), ())), preferred_element_type=F32)
            dwx = dwx + lax.dot_general(xb, dzxb, (TN, ((), ())), preferred_element_type=F32)
            dba = dba + jnp.sum(dza, axis=0, keepdims=True)
            dbx = dbx + jnp.sum(dzx, axis=0, keepdims=True)
            dlam = dlam + jnp.sum(dloga * LRU_C * r, axis=0, keepdims=True)
            return dwa, dwx, dba, dbx, dlam

        z = jnp.zeros((W, W), F32)
        zv = jnp.zeros((1, W), F32)
        dwa, dwx, dba, dbx, dlam = lax.fori_loop(0, T // R, chunk_b, (z, z, zv, zv, zv))
        dwa_ref[...] = dwa
        dwx_ref[...] = dwx
        dba_ref[...] = dba
        dbx_ref[...] = dbx
        dlam_ref[...] = dlam * sg_ref[...]
        if not last:
            out0[...] = dxc_ref[...]
        else:
            dxc = dxc_ref[...] + dxin_ref[...]
            dcb_ref[...] = jnp.sum(dxc, axis=0, keepdims=True)
            rows = [jnp.sum(dxc * taps[k], axis=0, keepdims=True) for k in range(4)]
            dcw_ref[...] = jnp.concatenate(rows, axis=0)
            valid = [(t < T - 2) & ((t >= C) | (t < C - 2)), (t < T - 1) & ((t >= C) | (t < C - 1)), None,
                     (t != 0) & (t != C)]
            shifts = [T - 2, T - 1, 0, 1]
            dlx = dxc * cw_ref[2:3, :]
            for k in (0, 1, 3):
                dlx = dlx + jnp.where(valid[k], pltpu.roll(dxc, shifts[k], 0), 0.0) * cw_ref[k:k + 1, :]
            out0[...] = dlx.astype(out0.dtype)

    strip = lambda j: (0, j)
    sspec = pl.BlockSpec((T, W), strip)
    vec = pl.BlockSpec((None, 1, W), lambda j: (direction, 0, j))
    mat = pl.BlockSpec((None, None, W, W), lambda j: (direction, j, 0, 0))
    ins = [P, dh, h, conv_w, conv_b, wa, ba, wx, bx, sp, sg]
    in_specs = [pl.BlockSpec((T, W), lambda j: (0, lxb + j)), sspec, sspec,
                pl.BlockSpec((4, W), strip), pl.BlockSpec((1, W), strip), mat, vec, mat, vec, vec, vec]
    omat = pl.BlockSpec((None, W, W), lambda j: (j, 0, 0))
    ovec = pl.BlockSpec((1, W), strip)
    small = [_sds((LB, W, W), F32), _sds((1, D), F32), _sds((LB, W, W), F32), _sds((1, D), F32), _sds((1, D), F32)]
    small_specs = [omat, ovec, omat, ovec, ovec]
    if last:
        ins += [dxc_in, dP]
        in_specs += [sspec, ANY]
        out_shape = [_sds(dP.shape, BF)] + small + [_sds((4, D), F32), _sds((1, D), F32)]
        out_specs = [pl.BlockSpec((T, W), lambda j: (0, lxb + j))] + small_specs + [pl.BlockSpec((4, W), strip), ovec]
        aliases = {12: 0}
    else:
        out_shape = [_sds((T, D), F32)] + small
        out_specs = [sspec] + small_specs
        aliases = {}
    return pl.pallas_call(
        body, grid=(LB,), name="lru_bwd%d" % direction, in_specs=in_specs, out_specs=out_specs, out_shape=out_shape,
        scratch_shapes=[pltpu.VMEM((T, W), F32)] * 5, input_output_aliases=aliases, compiler_params=_cparams(),
    )(*ins)


def merge_fwd(dm, P, attn, hf, hb):
    S, D, bm, cw, nCb = dm.S, dm.D, dm.bm, dm.cw, dm.nCb

    def fn(ids, lg_ref, ga_ref, gl_ref, at_ref, hf_ref, hb_ref):
        ge, _ = _gelu(lg_ref[...])
        lru = (hf_ref[...] + hb_ref[...]) * ge
        return [_sig(ga_ref[...]) * at_ref[...] + _sig(gl_ref[...]) * lru]

    pspec = lambda off: pl.BlockSpec((bm, cw), lambda i, j: (i + nCb, off // cw + j))
    tspec = pl.BlockSpec((bm, cw), lambda i, j: (i + nCb, j))
    sspec = pl.BlockSpec((bm, cw), lambda i, j: (i, j))
    return ew_call(
        "merge_fwd", (dm.nSb, D // cw), fn,
        [(P, pspec(dm.OFF_LG)), (P, pspec(dm.OFF_GA)), (P, pspec(dm.OFF_GL)), (attn, sspec), (hf, tspec), (hb, tspec)],
        [(_sds((S, D), BF), sspec, False)])[0]


def merge_bwd(dm, dmg, P, attn, hf, hb):
    S, T, D, bm, cw, nCb = dm.S, dm.T, dm.D, dm.bm, dm.cw, dm.nCb
    nj = D // cw

    def body(dm_ref, lg_ref, ga_ref, gl_ref, at_ref, hf_ref, hb_ref, dp_ref, da_ref, dh_ref, buf, sems):
        i, j = pl.program_id(0), pl.program_id(1)
        lat = i >= nCb
        d = jnp.where(lat, dm_ref[...].astype(F32), 0.0)
        lg = lg_ref[...]
        ge, th = _gelu(lg)
        hs = hf_ref[...] + hb_ref[...]
        sa, sl = _sig(ga_ref[...]), _sig(gl_ref[...])
        at = jnp.where(lat, at_ref[...], 0.0)
        dlru = d * sl
        buf[0] = (dlru * hs * _gelu_grad(lg, th)).astype(BF)
        buf[1] = (d * at * sa * (1.0 - sa)).astype(BF)
        buf[2] = (d * hs * ge * sl * (1.0 - sl)).astype(BF)
        da_ref[...] = (d * sa).astype(BF)
        dh_ref[...] = dlru * ge
        copies = []
        for g, off in enumerate((dm.OFF_LG, dm.OFF_GA, dm.OFF_GL)):
            col = pl.multiple_of(off + j * cw, 128)
            cp = pltpu.make_async_copy(buf.at[g], dp_ref.at[pl.ds(pl.multiple_of(i * bm, 8), bm), pl.ds(col, cw)],
                                       sems.at[g])
            cp.start()
            copies.append(cp)
        for cp in copies:
            cp.wait()

    pspec = lambda off: pl.BlockSpec((bm, cw), lambda i, j: (i, off // cw + j))
    tspec = pl.BlockSpec((bm, cw), lambda i, j: (i, j))
    lspec = pl.BlockSpec((bm, cw), lambda i, j: (jnp.maximum(i - nCb, 0), j))
    return pl.pallas_call(
        body, grid=(dm.nTb, nj), name="merge_bwd",
        in_specs=[lspec, pspec(dm.OFF_LG), pspec(dm.OFF_GA), pspec(dm.OFF_GL), lspec, tspec, tspec],
        out_specs=[ANY, tspec, tspec],
        out_shape=[_sds((T, dm.INW), BF), _sds((T, D), BF), _sds((T, D), F32)],
        scratch_shapes=[pltpu.VMEM((3, bm, cw), BF), pltpu.SemaphoreType.DMA((3,))],
        compiler_params=_cparams(),
    )(dmg, P, P, P, attn, hf, hb)


def final_loss(dm, x3, gfin, target):
    S, D, bm = dm.S, dm.D, dm.bm

    def fn(ids, x_ref, g_ref, t_ref):
        xv = x_ref[...]
        g = g_ref[...]
        r = lax.rsqrt(jnp.mean(xv * xv, axis=-1, keepdims=True) + EPS)
        xn = xv * r
        err = xn * g - t_ref[...]
        loss = 0.5 * jnp.sum(jnp.mean(err * err, axis=-1, keepdims=True), axis=0, keepdims=True)
        dy = err / D
        dxn = dy * g
        dx = r * (dxn - xn * jnp.mean(dxn * xn, axis=-1, keepdims=True))
        return [jnp.broadcast_to(loss, (1, 128)), dx, jnp.sum(dy * xn, axis=0, keepdims=True)]

    row = pl.BlockSpec((bm, D), lambda i: (i, 0))
    vec = pl.BlockSpec((1, D), lambda i: (0, 0))
    return ew_call(
        "final_loss", (dm.nSb,), fn, [(x3, row), (gfin, vec), (target, row)],
        [(_sds((1, 128), F32), pl.BlockSpec((1, 128), lambda i: (0, 0)), True), (_sds((S, D), F32), row, False),
         (_sds((1, D), F32), vec, True)], first=lambda ids: ids[0] == 0)


def local_step(dm, x, ctx, target, modv, norm_g3, gfin, gq, gk, conv_w, conv_b, wa, ba, wx, bx, lam, wbuf, where):
    S, C, T, D, NS, F4, W4 = dm.S, dm.C, dm.T, dm.D, dm.NS, dm.F4, dm.W4
    Ds = D // NS
    wb, nsub = dm.wb, dm.nsub
    xt = jnp.concatenate([ctx, x], axis=0)
    cosf, sinf = rope_tables(dm)
    sp = jax.nn.softplus(-lam)
    sg = jax.nn.sigmoid(-lam)
    ident = lambda ids, accs, ex: list(accs)
    mT, mS = dm.mT, dm.mS
    bn = _pick(D, [1024, 512, 256, 128])
    bk = _pick(D, [512, 256, 128])

    wg0, wu0, wd0 = comm_call("ag_ffn1", ag_comm([wbuf['wg0'], wbuf['wu0'], wbuf['wd0']]))
    h1 = normmod_fwd("nm1", dm, xt, norm_g3, 0, modv, True)
    xt1, a1, u1, s1, f1, land, _ = ffn_fwd("ffn1", dm, h1, xt, wg0, wu0, wd0, modv, 2, True,
                                           comm_up=ag_comm([wbuf['w_in'], wbuf['w_out']]))
    w_in, w_out = land[0], land[1].reshape(D, D)
    h2 = normmod_fwd("nm2", dm, xt1, norm_g3, 1, modv, True)
    P = fused_mm(
        "w_in", (T // mT, NS * nsub, 1),
        [(h2, pl.BlockSpec((mT, D), lambda i, j, k: (i, 0))),
         (w_in, pl.BlockSpec((None, D, wb), lambda i, j, k: (j // nsub, 0, j % nsub)))],
        [(0, 1, NN, 0)], [(mT, wb)], ident,
        [(_sds((T, dm.INW), F32), pl.BlockSpec((mT, wb), lambda i, j, k: (i, j)))])[0]
    qr, kr, vb = qk_prep(dm, P, gq, gk, cosf, sinf)
    (attn,), (wg1, wu1, wd1) = attention_fwd(dm, qr, kr, vb, comm=ag_comm([wbuf['wg1'], wbuf['wu1'], wbuf['wd1']]))
    hf = lru_fwd(dm, P, conv_w, conv_b, wa, ba, wx, bx, sp, 0)
    hb = lru_fwd(dm, P, conv_w, conv_b, wa, ba, wx, bx, sp, 1)
    mg = merge_fwd(dm, P, attn, hf, hb)

    def epi_o(ids, accs, ex):
        o = accs[0]
        return [ex[0][...] + ex[1][...] * o, o]

    x1_lat = xt1[C:]
    x2, o2 = fused_mm(
        "w_out", (S // mS, D // bn, D // bk),
        [(mg, pl.BlockSpec((mS, bk), lambda i, j, k: (i, k))), (w_out, pl.BlockSpec((bk, bn), lambda i, j, k: (k, j)))],
        [(0, 1, NN, 0)], [(mS, bn)], epi_o,
        [(_sds((S, D), F32), pl.BlockSpec((mS, bn), lambda i, j, k: (i, j))),
         (_sds((S, D), BF), pl.BlockSpec((mS, bn), lambda i, j, k: (i, j)))],
        extras=[(x1_lat, pl.BlockSpec((mS, bn), lambda i, j, k: (i, j))),
                (modv, pl.BlockSpec((None, None, 1, bn), lambda i, j, k: (1, 5, 0, j)))])
    h3 = normmod_fwd("nm3", dm, x2, norm_g3, 2, modv, False)
    x3, a3, u3, s3, f3, _, _ = ffn_fwd("ffn2", dm, h3, x2, wg1, wu1, wd1, modv, 8, False)
    loss, dx3, dgfin = final_loss(dm, x3, gfin, target)

    df3, dg3 = gate_bwd("gate3", dm, dx3, f3, modv, 8, FFN_RES, False)
    dh3, dwg1, dwu1, dwd1, _ = ffn_bwd("ffn2b", dm, df3, h3, a3, u3, s3, wg1, wu1, wd1, False)
    dx2, dsh3, dsc3, dgn3 = normmod_bwd("nm3b", dm, dh3, x2, dx3, norm_g3, 2, modv, False, False)
    do2, dg2 = gate_bwd("gate2", dm, dx2, o2, modv, 5, 1.0, False)
    keep = {}

    def host_a(key, comm):
        if key == 'p1':
            (keep['dmg'],), landed = fused_mm(
                "w_out_dx", (S // mS, D // bn, D // bk),
                [(do2, pl.BlockSpec((mS, bk), lambda i, j, k: (i, k))),
                 (w_out, pl.BlockSpec((bn, bk), lambda i, j, k: (j, k)))],
                [(0, 1, NT, 0)], [(mS, bn)], ident,
                [(_sds((S, D), BF), pl.BlockSpec((mS, bn), lambda i, j, k: (i, j)))], comm=comm)
            return landed
        keep['dqkv'], landed = attention_bwd(dm, qr, kr, vb, keep['dattn'], comm=comm)
        return landed

    gots_a = host_a('p1', rs_p1_comm([dwg1, dwu1, dwd1]))
    dmg = keep['dmg']
    dw_out = fused_mm(
        "w_out_dw", (D // bn, D // bn, S // mS),
        [(mg, pl.BlockSpec((mS, bn), lambda i, j, k: (k, i))), (do2, pl.BlockSpec((mS, bn), lambda i, j, k: (k, j)))],
        [(0, 1, TN, 0)], [(bn, bn)], ident,
        [(_sds((D, D), BF), pl.BlockSpec((bn, bn), lambda i, j, k: (i, j)))])[0]
    dP, dattn, dhs = merge_bwd(dm, dmg, P, attn, hf, hb)
    keep['dattn'] = dattn
    pairs_a = [add_pair("rs_add_" + n_, g_, got_, where)
               for n_, g_, got_ in zip(('wg1', 'wu1', 'wd1'), (dwg1, dwu1, dwd1), gots_a)]
    land_a = host_a('p2', rs_p2_comm([p_[0] for p_ in pairs_a], [p_[1] for p_ in pairs_a]))
    dq, dk, dv = keep['dqkv']
    dxc0, dwa0, dba0, dwx0, dbx0, dlam0 = lru_bwd(dm, P, dhs, hf, conv_w, conv_b, wa, ba, wx, bx, sp, sg, 0, None, None)
    dP, dwa1, dba1, dwx1, dbx1, dlam1, dcw, dcb = lru_bwd(dm, P, dhs, hb, conv_w, conv_b, wa, ba, wx, bx, sp, sg, 1,
                                                          dxc0, dP)
    dP, dgq, dgk = qk_prep_bwd(dm, dq, dk, dv, P, gq, gk, cosf, sinf, dP)
    g_wg = sum_slots_into("rs_sum_wg1", land_a[0], where, None, (2, D, F4), 1)
    g_wu = sum_slots_into("rs_sum_wu1", land_a[1], where, None, (2, D, F4), 1)
    g_wd = sum_slots_into("rs_sum_wd1", land_a[2], where, None, (2, F4, D), 1)
    nkb = NS * nsub
    (dh2,), (g_wg, g_wu, g_wd) = fused_mm(
        "w_in_dx", (T // mT, D // bn, nkb),
        [(dP, pl.BlockSpec((mT, wb), lambda i, j, k: (i, k))),
         (w_in, pl.BlockSpec((None, bn, wb), lambda i, j, k: (k // nsub, j, k % nsub)))],
        [(0, 1, NT, 0)], [(mT, bn)], ident,
        [(_sds((T, D), F32), pl.BlockSpec((mT, bn), lambda i, j, k: (i, j)))],
        comm=rs_p3_comm([g_wg, g_wu, g_wd], [(0, 1, D // 2), (1, 1, D // 2), (2, 1, F4 // 2)]))
    dw_in = fused_mm(
        "w_in_dw", (D // bn, nkb, T // mT),
        [(h2, pl.BlockSpec((mT, bn), lambda i, j, k: (k, i))), (dP, pl.BlockSpec((mT, wb), lambda i, j, k: (k, j)))],
        [(0, 1, TN, 0)], [(bn, wb)], ident,
        [(_sds((NS, D, W4), BF), pl.BlockSpec((None, bn, wb), lambda i, j, k: (j // nsub, i, j % nsub)))])[0]
    dxt1, dsh2, dsc2, dgn2 = normmod_bwd("nm2b", dm, dh2, xt1, dx2, norm_g3, 1, modv, True, True)
    df1, dg1 = gate_bwd("gate1", dm, dxt1, f1, modv, 2, FFN_RES, True)

    tens_b = [dw_in, dw_out.reshape(NS, Ds, D)]
    dh1, dwg0, dwu0, dwd0, g_win, g_wout = ffn1_bwd_hosted(dm, df1, h1, a1, u1, s1, wg0, wu0, wd0, tens_b, where)
    dxt, dsh1, dsc1, dgn1 = normmod_bwd("nm1b", dm, dh1, xt, dxt1, norm_g3, 0, modv, True, False)
    grad_x = dxt[C:]

    dmod = jnp.concatenate([dsh1, dsc1, dg1, dsh2, dsc2, _lat(dg2), _lat(dsh3), _lat(dsc3), _lat(dg3)], axis=1)
    dnorm = jnp.stack([dgn1[0, 0] + dgn1[1, 0], dgn2[0, 0] + dgn2[1, 0], dgn3[1, 0]], axis=0)
    small = dict(norm_g=dnorm, q_norm_g=dgq, k_norm_g=dgk, conv_w=dcw, conv_b=dcb,
                 lru_wa=jnp.stack([dwa0, dwa1]), lru_ba=jnp.concatenate([dba0, dba1], axis=0),
                 lru_wx=jnp.stack([dwx0, dwx1]), lru_bx=jnp.concatenate([dbx0, dbx1], axis=0),
                 lru_lambda=jnp.concatenate([dlam0, dlam1], axis=0), final_norm_g=dgfin)
    reduced = dict(ffn_wg=g_wg, ffn_wu=g_wu, ffn_wd=g_wd, w_in=g_win, w_out=g_wout)
    return loss, grad_x, dmod, small, reduced, [dwg0, dwu0, dwd0]


def ffn1_bwd_hosted(dm, df, h, a, u, s, wg, wu, wd, tensors, where):
    D, F4, NS = dm.D, dm.F4, dm.NS
    Ds = D // NS
    M = h.shape[0]
    bm = dm.mT
    bk = _pick(D, [512, 256, 128])
    bn = _pick(D, [1024, 512, 256, 128])
    ident = lambda ids, accs, ex: list(accs)

    def epi_ds(ids, accs, ex):
        ds = accs[0]
        av = ex[0][...].astype(F32)
        uv = ex[1][...].astype(F32)
        sg = _sig(av)
        return [ds * uv * (sg * (1.0 + av * (1.0 - sg))), ds * av * sg]

    ospec = pl.BlockSpec((bm, F4), lambda i, j, k: (i, j))
    (da, du), gots = fused_mm(
        "ffn1b_ds", (M // bm, NS, D // bk),
        [(df, pl.BlockSpec((bm, bk), lambda i, j, k: (i, k))),
         (wd, pl.BlockSpec((None, F4, bk), lambda i, j, k: (j, 0, k)))],
        [(0, 1, NT, 0)], [(bm, F4)], epi_ds, [(_sds((M, dm.DFF), BF), ospec)] * 2,
        extras=[(a, ospec), (u, ospec)], comm=rs_p1_comm(tensors))
    dwd = fused_mm(
        "ffn1b_dwd", (NS, D // bn, M // bm),
        [(s, pl.BlockSpec((bm, F4), lambda i, j, k: (k, i))),
         (df, pl.BlockSpec((bm, bn), lambda i, j, k: (k, j)))],
        [(0, 1, TN, 0)], [(F4, bn)], ident,
        [(_sds((NS, F4, D), BF), pl.BlockSpec((None, F4, bn), lambda i, j, k: (i, 0, j)))])[0]
    pairs = [add_pair("rs_add_" + n_, g_, got_, where) for n_, g_, got_ in zip(('w_in', 'w_out'), tensors, gots)]
    (dwg, dwu), landed = fused_mm(
        "ffn1b_dwgu", (D // bn, NS, M // bm),
        [(h, pl.BlockSpec((bm, bn), lambda i, j, k: (k, i))),
         (da, pl.BlockSpec((bm, F4), lambda i, j, k: (k, j))),
         (du, pl.BlockSpec((bm, F4), lambda i, j, k: (k, j)))],
        [(0, 1, TN, 0), (0, 2, TN, 1)], [(bn, F4), (bn, F4)], ident,
        [(_sds((NS, D, F4), BF), pl.BlockSpec((None, bn, F4), lambda i, j, k: (j, i, 0)))] * 2,
        comm=rs_p2_comm([p_[0] for p_ in pairs], [p_[1] for p_ in pairs]))
    g_win = sum_slots_into("rs_sum_w_in", landed[0], where, None, (D, dm.W4), None)
    g_wout = sum_slots_into("rs_sum_w_out", landed[1], where, None, (Ds, D), None)
    (dh,), (g_win, g_wout) = fused_mm(
        "ffn1b_dh", (M // bm, D // bn, NS),
        [(da, pl.BlockSpec((bm, F4), lambda i, j, k: (i, k))),
         (wg, pl.BlockSpec((None, bn, F4), lambda i, j, k: (k, j, 0))),
         (du, pl.BlockSpec((bm, F4), lambda i, j, k: (i, k))),
         (wu, pl.BlockSpec((None, bn, F4), lambda i, j, k: (k, j, 0)))],
        [(0, 1, NT, 0), (2, 3, NT, 0)], [(bm, bn)], ident,
        [(_sds((M, D), F32), pl.BlockSpec((bm, bn), lambda i, j, k: (i, j)))],
        comm=rs_p3_comm([g_win, g_wout], [(0, None, D // 2), (1, None, Ds // 2)]))
    return dh, dwg, dwu, dwd, g_win, g_wout


def _lat(v):
    return jnp.concatenate([jnp.zeros_like(v[:1]), v[1:]], axis=0)


def _me():
    return lax.axis_index("x"), lax.axis_index("y"), lax.axis_index("c")


def allgather8(name, v):
    def body(v_ref, out_ref, send_sems, recv_sems, local_sem):
        x, y, c = _me()
        me = 4 * x + 2 * y + c
        mine = pltpu.make_async_copy(v_ref, out_ref.at[me], local_sem)
        mine.start()
        copies = []
        for k in range(1, 8):
            peer = (x ^ ((k >> 2) & 1), y ^ ((k >> 1) & 1), c ^ (k & 1))
            cp = pltpu.make_async_remote_copy(src_ref=v_ref, dst_ref=out_ref.at[me], send_sem=send_sems.at[k - 1],
                                              recv_sem=recv_sems.at[k - 1], device_id=peer, device_id_type=MESH)
            cp.start()
            copies.append(cp)
        for k in range(1, 8):
            peer = (x ^ ((k >> 2) & 1), y ^ ((k >> 1) & 1), c ^ (k & 1))
            pltpu.make_async_remote_copy(src_ref=v_ref, dst_ref=out_ref.at[me ^ k], send_sem=send_sems.at[k - 1],
                                         recv_sem=recv_sems.at[k - 1], device_id=peer, device_id_type=MESH).wait_recv()
        for cp in copies:
            cp.wait_send()
        mine.wait()

    return pl.pallas_call(
        body, name=name, out_shape=_sds((8,) + v.shape, v.dtype), in_specs=[ANY], out_specs=ANY,
        scratch_shapes=[pltpu.SemaphoreType.DMA((7,)), pltpu.SemaphoreType.DMA((7,)), pltpu.SemaphoreType.DMA],
    )(v)


def _chips(x, y):
    chips = [(1 - x, y), (x, 1 - y), (1 - x, 1 - y)]
    return chips, [2 * cx + cy for cx, cy in chips]


def ag_comm(bufs):
    n = len(bufs)

    def parts(outs):
        x, y, c = _me()
        chips, slots = _chips(x, y)
        return x, y, c, 2 * x + y, (x, y, 1 - c), chips, slots

    def ici(outs, t, j, send_sems, recv_sems, src_slot):
        x, y, c, s, sib, chips, slots = parts(outs)
        H = outs[t].shape[1] // 2
        blk = outs[t].at[src_slot, pl.ds(c * H, H)]
        return pltpu.make_async_remote_copy(
            src_ref=blk, dst_ref=blk, send_sem=send_sems.at[6 * t + j], recv_sem=recv_sems.at[6 * t + j],
            device_id=(chips[j][0], chips[j][1], c), device_id_type=MESH)

    def d2d(outs, t, j, send_sems, recv_sems, half):
        x, y, c, s, sib, chips, slots = parts(outs)
        H = outs[t].shape[1] // 2
        blk = outs[t].at[slots[j], pl.ds(half * H, H)]
        return pltpu.make_async_remote_copy(
            src_ref=blk, dst_ref=blk, send_sem=send_sems.at[6 * t + 3 + j], recv_sem=recv_sems.at[6 * t + 3 + j],
            device_id=sib, device_id_type=MESH)

    def start(reads, outs, send_sems, recv_sems):
        x, y, c, s, sib, chips, slots = parts(outs)
        for t in range(n):
            for j in range(3):
                ici(outs, t, j, send_sems, recv_sems, s).start()

    def finish(reads, outs, send_sems, recv_sems):
        x, y, c, s, sib, chips, slots = parts(outs)
        for t in range(n):
            for j in range(3):
                ici(outs, t, j, send_sems, recv_sems, slots[j]).wait_recv()
                d2d(outs, t, j, send_sems, recv_sems, c).start()
        for t in range(n):
            for j in range(3):
                d2d(outs, t, j, send_sems, recv_sems, 1 - c).wait_recv()
        for t in range(n):
            for j in range(3):
                ici(outs, t, j, send_sems, recv_sems, s).wait_send()
                d2d(outs, t, j, send_sems, recv_sems, c).wait_send()

    return Comm([], bufs, 6 * n, start, finish)


def rs_p1_comm(tensors):
    n = len(tensors)

    def copy(ins, gots, t, send_sems, recv_sems):
        x, y, c = _me()
        H = ins[t].shape[1] // 2
        return pltpu.make_async_remote_copy(
            src_ref=ins[t].at[:, pl.ds((1 - c) * H, H)], dst_ref=gots[t], send_sem=send_sems.at[t],
            recv_sem=recv_sems.at[t], device_id=(x, y, 1 - c), device_id_type=MESH)

    def start(ins, gots, send_sems, recv_sems):
        for t in range(n):
            copy(ins, gots, t, send_sems, recv_sems).start()

    def finish(ins, gots, send_sems, recv_sems):
        for t in range(n):
            copy(ins, gots, t, send_sems, recv_sems).wait_recv()
        for t in range(n):
            copy(ins, gots, t, send_sems, recv_sems).wait_send()

    half = lambda t: _sds((t.shape[0], t.shape[1] // 2) + t.shape[2:], t.dtype)
    return Comm(tensors, [half(t) for t in tensors], n, start, finish)


def rs_p2_comm(partials, landeds):
    n = len(partials)

    def start(ins, outs, send_sems, recv_sems):
        x, y, c = _me()
        s = 2 * x + y
        chips, slots = _chips(x, y)
        for t in range(n):
            for j, chip in enumerate(chips):
                src = ins[t].at[slots[j]] if ins[t].shape[0] == 4 else ins[t].at[0]
                pltpu.make_async_remote_copy(
                    src_ref=src, dst_ref=outs[t].at[s], send_sem=send_sems.at[3 * t + j],
                    recv_sem=recv_sems.at[3 * t + j], device_id=(chip[0], chip[1], c), device_id_type=MESH).start()

    def finish(ins, outs, send_sems, recv_sems):
        x, y, c = _me()
        s = 2 * x + y
        chips, slots = _chips(x, y)
        for t in range(n):
            for j, chip in enumerate(chips):
                dst = outs[t].at[slots[j]]
                pltpu.make_async_remote_copy(
                    src_ref=dst, dst_ref=dst, send_sem=send_sems.at[3 * t + j],
                    recv_sem=recv_sems.at[3 * t + j], device_id=(chip[0], chip[1], c), device_id_type=MESH).wait_recv()
        for t in range(n):
            for j, chip in enumerate(chips):
                src = ins[t].at[slots[j]] if ins[t].shape[0] == 4 else ins[t].at[0]
                pltpu.make_async_remote_copy(
                    src_ref=src, dst_ref=outs[t].at[s], send_sem=send_sems.at[3 * t + j],
                    recv_sem=recv_sems.at[3 * t + j], device_id=(chip[0], chip[1], c), device_id_type=MESH).wait_send()

    return Comm(partials, landeds, 3 * n, start, finish)


def rs_p3_comm(greds, plan):
    n = len(plan)

    def copy(outs, t, send_sems, recv_sems, half):
        x, y, c = _me()
        oi, li, H = plan[t]
        dst = outs[oi] if li is None else outs[oi].at[li]
        blk = dst.at[pl.ds((c if half == 0 else 1 - c) * H, H)]
        return pltpu.make_async_remote_copy(
            src_ref=blk, dst_ref=blk, send_sem=send_sems.at[t], recv_sem=recv_sems.at[t],
            device_id=(x, y, 1 - c), device_id_type=MESH)

    def start(reads, outs, send_sems, recv_sems):
        for t in range(n):
            copy(outs, t, send_sems, recv_sems, 0).start()

    def finish(reads, outs, send_sems, recv_sems):
        for t in range(n):
            copy(outs, t, send_sems, recv_sems, 1).wait_recv()
        for t in range(n):
            copy(outs, t, send_sems, recv_sems, 0).wait_send()

    return Comm([], greds, n, start, finish)


def _rows_block(rows, cols, nbytes=1 << 20):
    bm = 8
    while bm * 2 * cols * 4 <= nbytes and rows % (bm * 2) == 0:
        bm *= 2
    return bm


def cast_into_slot(name, w, where, layer=None):
    rows, W = w.shape[-2:]
    bm = _rows_block(rows, W)

    def body(p_ref, w_ref, o_ref):
        o_ref[...] = w_ref[...].astype(BF)

    if layer is None:
        ispec = pl.BlockSpec((bm, W), lambda i, p: (i, 0))
    else:
        ispec = pl.BlockSpec((None, bm, W), lambda i, p: (layer, i, 0))
    return pl.pallas_call(
        body, name=name, out_shape=_sds((4, rows, W), BF), compiler_params=_cparams(),
        grid_spec=pltpu.PrefetchScalarGridSpec(
            num_scalar_prefetch=1, grid=(rows // bm,), in_specs=[ispec],
            out_specs=pl.BlockSpec((None, bm, W), lambda i, p: (p[1], i, 0))),
    )(where, w)


def add_pair(name, g, got, where):
    K, R, W = g.shape
    H = R // 2
    bm = _rows_block(H, W)
    nh = H // bm

    def body(p_ref, g_ref, got_ref, part_ref, land_ref):
        k = pl.program_id(1)
        v = (g_ref[...].astype(F32) + got_ref[...].astype(F32)).astype(part_ref.dtype)
        part_ref[...] = v
        own = (k == p_ref[1]) if K == 4 else (k == 0)

        @pl.when(own)
        def _():
            land_ref[...] = v

    return pl.pallas_call(
        body, name=name, out_shape=[_sds((K, H, W), g.dtype), _sds((4, H, W), g.dtype)], compiler_params=_cparams(),
        grid_spec=pltpu.PrefetchScalarGridSpec(
            num_scalar_prefetch=1, grid=(nh, K),
            in_specs=[pl.BlockSpec((None, bm, W), lambda i, k, p: (k, p[0] * nh + i, 0)),
                      pl.BlockSpec((None, bm, W), lambda i, k, p: (k, i, 0))],
            out_specs=[pl.BlockSpec((None, bm, W), lambda i, k, p: (k, i, 0)),
                       pl.BlockSpec((None, bm, W), lambda i, k, p: (p[1], i, 0))]),
    )(where, g, got)


def sum_slots_into(name, landed, where, dest, dest_shape, li):
    K, H, W = landed.shape
    bm = _rows_block(H, 2 * W)
    nh = H // bm

    def body(*refs):
        r, o_ref = refs[1], refs[-1]
        acc = r[0].astype(F32)
        for k in range(1, K):
            acc = acc + r[k].astype(F32)
        o_ref[...] = acc

    if li is None:
        ospec = pl.BlockSpec((bm, W), lambda i, p: (p[0] * nh + i, 0))
    else:
        ospec = pl.BlockSpec((None, bm, W), lambda i, p: (li, p[0] * nh + i, 0))
    in_specs = [pl.BlockSpec((K, bm, W), lambda i, p: (0, i, 0))]
    args = [where, landed]
    aliases = {}
    if dest is not None:
        in_specs.append(ANY)
        args.append(dest)
        aliases = {2: 0}
    return pl.pallas_call(
        body, name=name, out_shape=_sds(dest_shape, F32), compiler_params=_cparams(), input_output_aliases=aliases,
        grid_spec=pltpu.PrefetchScalarGridSpec(num_scalar_prefetch=1, grid=(nh,), in_specs=in_specs, out_specs=ospec),
    )(*args)


def adamw(name, w, g, m, v):
    shape = w.shape
    flat = lambda t: t.reshape(-1, shape[-1])
    w2, g2, m2, v2 = flat(w), flat(g), flat(m), flat(v)
    bm = _rows_block(w2.shape[0], w2.shape[1] * 2)
    bc1 = 1.0 - ADAM_B1 ** ADAM_STEP
    bc2 = 1.0 - ADAM_B2 ** ADAM_STEP

    def fn(ids, w_ref, g_ref, m_ref, v_ref):
        gv = g_ref[...]
        mn = ADAM_B1 * m_ref[...] + (1.0 - ADAM_B1) * gv
        vn = ADAM_B2 * v_ref[...] + (1.0 - ADAM_B2) * (gv * gv)
        m_hat = mn / bc1
        v_hat = vn / bc2
        delta = -ADAM_LR * (m_hat / (jnp.sqrt(v_hat) + ADAM_EPS) + ADAM_WD * w_ref[...])
        return [delta, mn, vn]

    spec = pl.BlockSpec((bm, w2.shape[1]), lambda i: (i, 0))
    outs = ew_call(name, (w2.shape[0] // bm,), fn, [(w2, spec), (g2, spec), (m2, spec), (v2, spec)],
                   [(_sds(w2.shape, F32), spec, False)] * 3)
    return [o.reshape(shape) for o in outs]


def dmod_pack(gd):
    N = gd.shape[-1]
    bn = _pick(N, [4608, 2304, 1152, 1024, 512, 256, 128])

    def fn(ids, r):
        lat = [r[d, 1:2, :] for d in range(8)]
        cs = r[0, 0:1, :]
        for d in range(1, 8):
            cs = cs + r[d, 0:1, :]
        tot = cs
        for d in range(8):
            tot = tot + lat[d]
        return [jnp.concatenate(lat + [cs, jnp.zeros((7, bn), F32)], axis=0), tot]

    return ew_call("dmod_pack", (N // bn,), fn, [(gd, pl.BlockSpec((8, 2, bn), lambda j: (0, 0, j)))],
                   [(_sds((16, N), F32), pl.BlockSpec((16, bn), lambda j: (0, j)), False),
                    (_sds((1, N), F32), pl.BlockSpec((1, bn), lambda j: (0, j)), False)])


def _silu(v):
    return v * _sig(v)


def kernel(x, c, ctx, c_ctx, w_mod, b_mod, norm_g, ffn_wg, ffn_wu, ffn_wd, w_in, w_out, q_norm_g, k_norm_g, conv_w, conv_b, lru_wa, lru_ba, lru_wx, lru_bx, lru_lambda, final_norm_g, loss_target, m_c_ctx, m_w_mod, m_b_mod, m_norm_g, m_ffn_wg, m_ffn_wu, m_ffn_wd, m_w_in, m_w_out, m_q_norm_g, m_k_norm_g, m_conv_w, m_conv_b, m_lru_wa, m_lru_ba, m_lru_wx, m_lru_bx, m_lru_lambda, m_final_norm_g, v_c_ctx, v_w_mod, v_b_mod, v_norm_g, v_ffn_wg, v_ffn_wu, v_ffn_wd, v_w_in, v_w_out, v_q_norm_g, v_k_norm_g, v_conv_w, v_conv_b, v_lru_wa, v_lru_ba, v_lru_wx, v_lru_bx, v_lru_lambda, v_final_norm_g):
    given = dict(locals())
    names = ['c_ctx', 'w_mod', 'b_mod', 'norm_g', 'ffn_wg', 'ffn_wu', 'ffn_wd', 'w_in', 'w_out', 'q_norm_g', 'k_norm_g',
             'conv_w', 'conv_b', 'lru_wa', 'lru_ba', 'lru_wx', 'lru_bx', 'lru_lambda', 'final_norm_g']
    S, D = x.shape[1], x.shape[2]
    C = ctx.shape[1]
    NS = 4
    F4, W4, LB = ffn_wg.shape[-1], w_in.shape[-1], lru_wa.shape[2]
    dm = Dims(S, C, D, F4, W4, NS, LB)
    Ds = D // NS
    Wm = w_mod.shape[-1]
    xi, yi, ci = lax.axis_index("x"), lax.axis_index("y"), lax.axis_index("c")
    slot = 2 * xi + yi
    me = 4 * xi + 2 * yi + ci
    ident = lambda ids, accs, ex: list(accs)

    pack1 = jnp.concatenate([c.reshape(-1), norm_g.reshape(-1), conv_w.reshape(-1), lru_ba.reshape(-1),
                             lru_bx.reshape(-1), lru_lambda.reshape(-1)]).reshape(1, -1)
    g1 = allgather8("ag_small_params", pack1)[:, 0]
    c_all = g1[:, :D]

    def unshard(off, k):
        part = g1[0::2, off:off + k * Ds].reshape(NS, k, Ds)
        return jnp.transpose(part, (1, 0, 2)).reshape(k, D)

    norm_g_f = unshard(D, 3)
    conv_w_f = unshard(D + 3 * Ds, 4)
    ba_f = unshard(D + 7 * Ds, 2)
    bx_f = unshard(D + 9 * Ds, 2)
    lam_f = unshard(D + 11 * Ds, 2)

    call16 = jnp.concatenate([c_all, c_ctx.reshape(1, D), jnp.zeros((7, D), F32)], axis=0)
    b_cols = lax.dynamic_slice(b_mod, (0, slot * Wm), (1, Wm))
    bnm = _pick(Wm, [1536, 1152, 768, 512, 384, 256, 128])
    bkm = _pick(D, [512, 256, 128])
    modp = fused_mm(
        "mod_fwd", (1, Wm // bnm, D // bkm),
        [(call16, pl.BlockSpec((16, bkm), lambda i, j, k: (0, k))),
         (w_mod[0], pl.BlockSpec((bkm, bnm), lambda i, j, k: (k, j)))],
        [(0, 1, NN, 0)], [(16, bnm)], lambda ids, accs, ex: [accs[0] + ex[0][...]],
        [(_sds((16, Wm), F32), pl.BlockSpec((16, bnm), lambda i, j, k: (0, j)))],
        extras=[(b_cols, pl.BlockSpec((1, bnm), lambda i, j, k: (0, j)))], pre={0: _silu})[0]
    gm = allgather8("ag_mod", modp)
    mod_full = jnp.concatenate([gm[0], gm[2], gm[4], gm[6]], axis=1)
    mod_x = lax.dynamic_index_in_dim(mod_full, me, axis=0, keepdims=False)
    modv = jnp.stack([mod_full[8], mod_x]).reshape(2, N_MOD, 1, D)

    where = jnp.stack([ci, slot]).astype(jnp.int32)
    wbuf = {}
    for key, short in (('ffn_wg', 'wg'), ('ffn_wu', 'wu'), ('ffn_wd', 'wd')):
        for l in range(2):
            wbuf[short + str(l)] = cast_into_slot("cast_%s%d" % (short, l), given[key][0], where, l)
    wbuf['w_in'] = cast_into_slot("cast_w_in", w_in[0], where)
    wbuf['w_out'] = cast_into_slot("cast_w_out", w_out[0], where)

    loss_l, grad_x, dmod, small, reduced, first_ffn = local_step(
        dm, x[0], ctx[0], loss_target[0], modv, norm_g_f.reshape(3, 1, D), final_norm_g.reshape(1, D),
        q_norm_g, k_norm_g, conv_w_f, conv_b, lru_wa[0], ba_f.reshape(2, 1, D), lru_wx[0], bx_f.reshape(2, 1, D),
        lam_f.reshape(2, 1, D), wbuf, where)
    loss = lax.psum(loss_l[0, 0], ("x", "y", "c"))

    grads = {}
    gd = allgather8("ag_dmod", dmod.reshape(2, N_MOD * D))
    dM, g_bmod = dmod_pack(gd)
    dMc = lax.dynamic_slice(dM, (0, slot * Wm), (16, Wm))
    bmm = _pick(D, [512, 256, 128])
    grads['w_mod'] = fused_mm(
        "w_mod_dw", (D // bmm, Wm // bnm, 1),
        [(call16, pl.BlockSpec((16, bmm), lambda i, j, k: (0, i))), (dMc, pl.BlockSpec((16, bnm), lambda i, j, k: (0, j)))],
        [(0, 1, TN, 0)], [(bmm, bnm)], ident,
        [(_sds((D, Wm), F32), pl.BlockSpec((bmm, bnm), lambda i, j, k: (i, j)))], pre={0: _silu})[0][None]
    grads['b_mod'] = g_bmod

    def epi_cc(ids, accs, ex):
        v = ex[0][...]
        sg = _sig(v)
        return [accs[0] * (sg * (1.0 + v * (1.0 - sg)))]

    pcc = fused_mm(
        "c_ctx_partial", (1, D // bmm, Wm // bnm),
        [(dMc, pl.BlockSpec((16, bnm), lambda i, j, k: (0, k))), (w_mod[0], pl.BlockSpec((bmm, bnm), lambda i, j, k: (j, k)))],
        [(0, 1, NT, 0)], [(16, bmm)], epi_cc,
        [(_sds((16, D), F32), pl.BlockSpec((16, bmm), lambda i, j, k: (0, j)))],
        extras=[(c_ctx.reshape(1, D), pl.BlockSpec((1, bmm), lambda i, j, k: (0, j)))])[0]
    pcc_row = jnp.where(ci == 0, pcc[8], 0.0)

    order = ['lru_wa', 'lru_wx', 'q_norm_g', 'k_norm_g', 'conv_b', 'final_norm_g', 'norm_g', 'conv_w', 'lru_ba',
             'lru_bx', 'lru_lambda']
    flat = [small[k].reshape(-1) for k in order] + [pcc_row]
    sizes = [f.shape[0] for f in flat]
    tot = sum(sizes)
    LW = 1024
    padded = -(-tot // (16 * LW)) * (16 * LW)
    vec = jnp.concatenate(flat + [jnp.zeros((padded - tot,), F32)]).reshape(1, -1, LW)
    RS = vec.shape[1]

    tensors = first_ffn + [vec]
    tnames = ['wg0', 'wu0', 'wd0', 'small']
    gots = comm_call("rs_tail_p1", rs_p1_comm(tensors))
    pairs = [add_pair("rs_add_" + n_, g_, got_, where) for n_, g_, got_ in zip(tnames, tensors, gots)]
    landeds = comm_call("rs_tail_p2", rs_p2_comm([p_[0] for p_ in pairs], [p_[1] for p_ in pairs]))
    g_wg = sum_slots_into("rs_sum_wg0", landeds[0], where, reduced['ffn_wg'], (2, D, F4), 0)
    g_wu = sum_slots_into("rs_sum_wu0", landeds[1], where, reduced['ffn_wu'], (2, D, F4), 0)
    g_wd = sum_slots_into("rs_sum_wd0", landeds[2], where, reduced['ffn_wd'], (2, F4, D), 0)
    g_small = sum_slots_into("rs_sum_small", landeds[3], where, None, (RS, LW), None)
    g_wg, g_wu, g_wd, g_small = comm_call("rs_tail_p3", rs_p3_comm(
        [g_wg, g_wu, g_wd, g_small], [(0, 0, D // 2), (1, 0, D // 2), (2, 0, F4 // 2), (3, None, RS // 2)]))
    grads.update(ffn_wg=g_wg[None], ffn_wu=g_wu[None], ffn_wd=g_wd[None], w_in=reduced['w_in'][None],
                 w_out=reduced['w_out'][None])
    summed = g_small.reshape(-1)
    offs = {}
    o = 0
    for k, n_ in zip(order + ['c_ctx'], sizes):
        offs[k] = summed[o:o + n_]
        o += n_
    shard = lambda k, rows: lax.dynamic_slice_in_dim(offs[k].reshape(rows, D), slot * Ds, Ds, axis=1)
    grads['c_ctx'] = offs['c_ctx']
    grads['q_norm_g'] = offs['q_norm_g'].reshape(1, HEAD_DIM)
    grads['k_norm_g'] = offs['k_norm_g'].reshape(1, HEAD_DIM)
    grads['conv_b'] = offs['conv_b'].reshape(1, D)
    grads['final_norm_g'] = offs['final_norm_g']
    grads['lru_wa'] = offs['lru_wa'].reshape(lru_wa.shape)
    grads['lru_wx'] = offs['lru_wx'].reshape(lru_wx.shape)
    grads['norm_g'] = shard('norm_g', 3)[None]
    grads['conv_w'] = shard('conv_w', 4)[None]
    grads['lru_ba'] = shard('lru_ba', 2)[None]
    grads['lru_bx'] = shard('lru_bx', 2)[None]
    grads['lru_lambda'] = shard('lru_lambda', 2)[None]

    big_names = ['w_mod', 'ffn_wg', 'ffn_wu', 'ffn_wd', 'w_in', 'w_out']
    delta, new_m, new_v = {}, {}, {}
    for k in big_names:
        delta[k], new_m[k], new_v[k] = adamw("adamw_" + k, given[k], grads[k], given['m_' + k], given['v_' + k])
    small_names = [k for k in names if k not in big_names]
    ssz = [given[k].size for k in small_names]
    stot = sum(ssz)
    spad = -(-stot // (8 * LW)) * (8 * LW)

    def packed(get):
        return jnp.concatenate([get(k).reshape(-1) for k in small_names] + [jnp.zeros((spad - stot,), F32)]).reshape(-1, LW)

    pw, pg, pm = packed(lambda k: given[k]), packed(lambda k: grads[k]), packed(lambda k: given['m_' + k])
    pv = jnp.concatenate([given['v_' + k].reshape(-1) for k in small_names] + [jnp.ones((spad - stot,), F32)]).reshape(-1, LW)
    sd, sm, sv = [t.reshape(-1) for t in adamw("adamw_small", pw, pg, pm, pv)]
    o = 0
    for k, n_ in zip(small_names, ssz):
        shp = given[k].shape
        delta[k], new_m[k], new_v[k] = sd[o:o + n_].reshape(shp), sm[o:o + n_].reshape(shp), sv[o:o + n_].reshape(shp)
        o += n_

    return (loss, grad_x[None], *[grads[k] for k in names], *[delta[k] for k in names],
            *[new_m[k] for k in names], *[new_v[k] for k in names])
```

```python
import functools

import jax
import jax.numpy as jnp
from jax import lax
from jax.experimental import pallas as pl
from jax.experimental.pallas import tpu as pltpu

F32 = jnp.float32
BF = jnp.bfloat16
EPS = 1e-6
HEAD_DIM = 128
GRID_W = 64
ROPE_THETA = 10000.0
LRU_C = 8.0
FFN_RES = 0.5
N_MOD = 9
ADAM_LR, ADAM_B1, ADAM_B2, ADAM_EPS, ADAM_WD, ADAM_STEP = 0.001, 0.9, 0.999, 1e-08, 0.01, 10
VMEM_LIMIT = 52 * 1024 * 1024
MESH = pl.DeviceIdType.MESH
ANY = pl.BlockSpec(memory_space=pl.ANY)


def _sds(shape, dt):
    return jax.ShapeDtypeStruct(tuple(shape), dt)


def _pick(n, cands):
    for c in cands:
        if n % c == 0:
            return c
    return n


def _cparams(**kw):
    return pltpu.CompilerParams(vmem_limit_bytes=VMEM_LIMIT, **kw)


def _sig(x):
    return 1.0 / (1.0 + jnp.exp(-x))


def _gelu(x):
    t = jnp.tanh(0.7978845608028654 * (x + 0.044715 * x * x * x))
    return 0.5 * x * (1.0 + t), t


def _gelu_grad(x, t):
    return 0.5 * (1.0 + t) + 0.5 * x * (1.0 - t * t) * 0.7978845608028654 * (1.0 + 3.0 * 0.044715 * x * x)


class Comm:
    def __init__(self, reads, lands, n_sem, start, finish):
        self.reads, self.lands, self.n_sem, self.start, self.finish = list(reads), list(lands), n_sem, start, finish


def hosted_call(body, *, name, grid, in_specs, out_specs, out_shape, args, scratch_shapes=(), aliases=None, comm=None):
    aliases = dict(aliases or {})
    if comm is None:
        return pl.pallas_call(
            body, name=name, grid=grid, in_specs=list(in_specs), out_specs=list(out_specs), out_shape=list(out_shape),
            scratch_shapes=list(scratch_shapes), input_output_aliases=aliases, compiler_params=_cparams())(*args)
    n_in, n_out, n_sc = len(args), len(out_shape), len(scratch_shapes)
    land_in = [(t, l) for t, l in enumerate(comm.lands) if not isinstance(l, jax.ShapeDtypeStruct)]
    nr, nli, nl = len(comm.reads), len(land_in), len(comm.lands)

    def wrapped(*refs):
        a = refs[:n_in]
        r = refs[n_in:n_in + nr]
        pos = n_in + nr + nli
        o = refs[pos:pos + n_out]
        lo = refs[pos + n_out:pos + n_out + nl]
        sc = refs[pos + n_out + nl:pos + n_out + nl + n_sc]
        send_sems, recv_sems = refs[pos + n_out + nl + n_sc:]
        ids = [pl.program_id(d) for d in range(len(grid))]
        first, last = ids[0] == 0, ids[0] == grid[0] - 1
        for d in range(1, len(grid)):
            first = first & (ids[d] == 0)
            last = last & (ids[d] == grid[d] - 1)

        @pl.when(first)
        def _():
            comm.start(r, lo, send_sems, recv_sems)

        body(*a, *o, *sc)

        @pl.when(last)
        def _():
            comm.finish(r, lo, send_sems, recv_sems)

    for q, (t, _) in enumerate(land_in):
        aliases[n_in + nr + q] = n_out + t
    res = pl.pallas_call(
        wrapped, name=name, grid=grid,
        in_specs=list(in_specs) + [ANY] * (nr + nli), out_specs=list(out_specs) + [ANY] * nl,
        out_shape=list(out_shape) + [l if isinstance(l, jax.ShapeDtypeStruct) else _sds(l.shape, l.dtype) for l in comm.lands],
        scratch_shapes=list(scratch_shapes) + [pltpu.SemaphoreType.DMA((comm.n_sem,)), pltpu.SemaphoreType.DMA((comm.n_sem,))],
        input_output_aliases=aliases, compiler_params=_cparams(),
    )(*args, *comm.reads, *[l for _, l in land_in])
    return list(res[:n_out]), list(res[n_out:])


def comm_call(name, comm):
    def body():
        pass

    return hosted_call(body, name=name, grid=(1,), in_specs=[], out_specs=[], out_shape=[], args=[], comm=comm)[1]


def ew_call(name, grid, fn, ins, outs, first=None, aliases=None, comm=None):
    n_in = len(ins)

    def body(*refs):
        ids = tuple(pl.program_id(a) for a in range(len(grid)))
        vals = fn(ids, *refs[:n_in])
        for (_, _, acc), o_ref, v in zip(outs, refs[n_in:], vals):
            if not acc:
                o_ref[...] = v.astype(o_ref.dtype)
            else:
                is_first = first(ids)

                @pl.when(is_first)
                def _(o_ref=o_ref, v=v):
                    o_ref[...] = v.astype(o_ref.dtype)

                @pl.when(jnp.logical_not(is_first))
                def _(o_ref=o_ref, v=v):
                    o_ref[...] += v.astype(o_ref.dtype)

    return hosted_call(body, name=name, grid=grid, in_specs=[s for _, s in ins], out_specs=[s for _, s, _ in outs],
                       out_shape=[o for o, _, _ in outs], args=[a for a, _ in ins], aliases=aliases, comm=comm)


def fused_mm(name, grid, ins, prods, acc_shapes, epi, outs, extras=(), pre=None, comm=None):
    n_in, n_ex, n_out = len(ins), len(extras), len(outs)
    nk = grid[-1]
    pre = pre or {}
    n_acc = len(acc_shapes)

    def body(*refs):
        in_refs = refs[:n_in]
        ex_refs = refs[n_in:n_in + n_ex]
        out_refs = refs[n_in + n_ex:n_in + n_ex + n_out]
        accs = refs[n_in + n_ex + n_out:]
        ids = tuple(pl.program_id(a) for a in range(len(grid)))
        k = ids[-1]
        loaded = {}

        def operand(i):
            if i not in loaded:
                v = in_refs[i][...]
                if i in pre:
                    v = pre[i](v)
                loaded[i] = v.astype(BF)
            return loaded[i]

        def product(ia, ib, dims):
            return lax.dot_general(operand(ia), operand(ib), (dims, ((), ())), preferred_element_type=F32)

        if nk == 1:
            sums = [None] * n_acc
            for ia, ib, dims, ai in prods:
                d = product(ia, ib, dims)
                sums[ai] = d if sums[ai] is None else sums[ai] + d
            for o_ref, v in zip(out_refs, epi(ids, sums, ex_refs)):
                o_ref[...] = v.astype(o_ref.dtype)
            return

        @pl.when(k == 0)
        def _():
            for a in accs:
                a[...] = jnp.zeros(a.shape, F32)

        for ia, ib, dims, ai in prods:
            accs[ai][...] += product(ia, ib, dims)

        @pl.when(k == nk - 1)
        def _():
            vals = epi(ids, [a[...] for a in accs], ex_refs)
            for o_ref, v in zip(out_refs, vals):
                o_ref[...] = v.astype(o_ref.dtype)

    return hosted_call(
        body, name=name, grid=grid, in_specs=[s for _, s in ins] + [s for _, s in extras],
        out_specs=[s for _, s in outs], out_shape=[o for o, _ in outs],
        scratch_shapes=[pltpu.VMEM(s, F32) for s in acc_shapes] if nk > 1 else [],
        args=[a for a, _ in ins] + [a for a, _ in extras], comm=comm)


NN = ((1,), (0,))
NT = ((1,), (1,))
TN = ((0,), (0,))


class Dims:
    def __init__(self, S, C, D, F4, W4, NS, LB):
        self.S, self.C, self.D, self.F4, self.W4, self.NS, self.LB = S, C, D, F4, W4, NS, LB
        self.T = S + C
        self.DFF = F4 * NS
        self.INW = W4 * NS
        self.NQ = D // HEAD_DIM
        self.KVW = (self.INW - 5 * D) // 2
        self.NKV = self.KVW // HEAD_DIM
        self.G = self.NQ // self.NKV
        self.OFF_K = D
        self.OFF_V = D + self.KVW
        self.OFF_LX = D + 2 * self.KVW
        self.OFF_LG = self.OFF_LX + D
        self.OFF_GA = self.OFF_LG + D
        self.OFF_GL = self.OFF_GA + D
        self.bm = _pick(C, [256, 128, 64, 32, 16, 8])
        self.nCb = C // self.bm
        self.nTb = self.T // self.bm
        self.nSb = S // self.bm
        self.mT = _pick(self.T, [544, 512, 384, 256, 128])
        self.mS = _pick(S, [512, 256, 128])
        self.kT = _pick(self.T, [1088, 1024, 768, 544, 512, 384, 256, 128])
        self.kS = _pick(S, [1024, 512, 256, 128])
        self.cw = _pick(D, [1024, 512, 256, 128]) if (self.OFF_LX % 1024 == 0 and D % 1024 == 0) else _pick(
            self.OFF_LX, [512, 256, 128])
        self.nsub = 2 if (W4 % 256 == 0 and W4 >= 512) else 1
        self.wb = W4 // self.nsub
        self.LBD = D // LB
        self.bq = _pick(C, [256, 128]) if S % _pick(C, [256, 128]) == 0 else 128


def rope_tables(dm):
    rows = dm.S // GRID_W
    row = jnp.repeat(jnp.arange(rows, dtype=F32), GRID_W)
    col = jnp.tile(jnp.arange(GRID_W, dtype=F32), rows)
    axis_dims = HEAD_DIM // 2
    freqs = ROPE_THETA ** (-jnp.arange(0, axis_dims, 2, dtype=F32) / axis_dims)
    ang = jnp.concatenate([row[:, None] * freqs, col[:, None] * freqs], axis=-1)
    cos = jnp.repeat(jnp.cos(ang), 2, axis=-1)
    sin = jnp.repeat(jnp.sin(ang), 2, axis=-1)
    sign = jnp.tile(jnp.array([-1.0, 1.0], F32), HEAD_DIM // 2)
    sin = sin * sign
    cos = jnp.concatenate([jnp.ones((dm.C, HEAD_DIM), F32), cos], axis=0)
    sin = jnp.concatenate([jnp.zeros((dm.C, HEAD_DIM), F32), sin], axis=0)
    return cos, sin


def _pair_swap(y):
    lane = lax.broadcasted_iota(jnp.int32, y.shape, 1)
    nxt = pltpu.roll(y, y.shape[1] - 1, 1)
    prv = pltpu.roll(y, 1, 1)
    return jnp.where((lane & 1) == 0, nxt, prv)


def normmod_fwd(name, dm, x, norm_g3, stage, modv, rows_T):
    D, bm = dm.D, dm.bm
    nb = dm.nTb if rows_T else dm.nSb
    typ = (lambda i: jnp.where(i < dm.nCb, 0, 1)) if rows_T else (lambda i: 1)

    def fn(ids, x_ref, g_ref, sh_ref, sc_ref):
        xv = x_ref[...]
        r = lax.rsqrt(jnp.mean(xv * xv, axis=-1, keepdims=True) + EPS)
        n = xv * r * g_ref[...]
        return [n * (1.0 + sc_ref[...]) + sh_ref[...]]

    return ew_call(
        name, (nb,), fn,
        [(x, pl.BlockSpec((bm, D), lambda i: (i, 0))),
         (norm_g3, pl.BlockSpec((None, 1, D), lambda i: (stage, 0, 0))),
         (modv, pl.BlockSpec((None, None, 1, D), lambda i: (typ(i), 3 * stage, 0, 0))),
         (modv, pl.BlockSpec((None, None, 1, D), lambda i: (typ(i), 3 * stage + 1, 0, 0)))],
        [(_sds(x.shape, BF), pl.BlockSpec((bm, D), lambda i: (i, 0)), False)])[0]


def normmod_concat_fwd(name, dm, ctx, x, norm_g3, modv):
    D, bm, nCb = dm.D, dm.bm, dm.nCb
    typ = lambda i: jnp.where(i < nCb, 0, 1)

    def fn(ids, c_ref, x_ref, g_ref, sh_ref, sc_ref):
        xv = jnp.where(ids[0] < nCb, c_ref[...], x_ref[...])
        r = lax.rsqrt(jnp.mean(xv * xv, axis=-1, keepdims=True) + EPS)
        n = xv * r * g_ref[...]
        return [n * (1.0 + sc_ref[...]) + sh_ref[...], xv]

    row = pl.BlockSpec((bm, D), lambda i: (i, 0))
    return ew_call(
        name, (dm.nTb,), fn,
        [(ctx, pl.BlockSpec((bm, D), lambda i: (jnp.minimum(i, nCb - 1), 0))),
         (x, pl.BlockSpec((bm, D), lambda i: (jnp.maximum(i - nCb, 0), 0))),
         (norm_g3, pl.BlockSpec((None, 1, D), lambda i: (0, 0, 0))),
         (modv, pl.BlockSpec((None, None, 1, D), lambda i: (typ(i), 0, 0, 0))),
         (modv, pl.BlockSpec((None, None, 1, D), lambda i: (typ(i), 1, 0, 0)))],
        [(_sds((dm.T, D), BF), row, False), (_sds((dm.T, D), F32), row, False)])


def normmod_bwd(name, dm, dh, x, dres, norm_g3, stage, modv, rows_T, dres_lat_only, out_lat_only=False):
    D, bm = dm.D, dm.bm
    nb = dm.nTb if rows_T else dm.nSb
    nCb = dm.nCb
    typ = (lambda i: jnp.where(i < nCb, 0, 1)) if rows_T else (lambda i: 1)
    if dres_lat_only:
        dres_map = lambda i: (jnp.maximum(i - nCb, 0), 0)
    else:
        dres_map = lambda i: (i, 0)

    def fn(ids, dh_ref, x_ref, dres_ref, g_ref, sc_ref):
        i = ids[0]
        xv = x_ref[...]
        dhv = dh_ref[...].astype(F32)
        r = lax.rsqrt(jnp.mean(xv * xv, axis=-1, keepdims=True) + EPS)
        xn = xv * r
        g = g_ref[...]
        n = xn * g
        dn = dhv * (1.0 + sc_ref[...])
        dxn = dn * g
        dx = r * (dxn - xn * jnp.mean(dxn * xn, axis=-1, keepdims=True))
        dresv = dres_ref[...]
        if dres_lat_only:
            dresv = jnp.where(i >= nCb, dresv, 0.0)
        dsh = jnp.sum(dhv, axis=0, keepdims=True)
        dsc = jnp.sum(dhv * n, axis=0, keepdims=True)
        dg = jnp.sum(dn * xn, axis=0, keepdims=True)
        return [dx + dresv, dsh, dsc, dg]

    if rows_T:
        first = lambda ids: (ids[0] == 0) | (ids[0] == nCb)
    else:
        first = lambda ids: ids[0] == 0
    acc = (_sds((2, 1, D), F32), pl.BlockSpec((None, 1, D), lambda i: (typ(i), 0, 0)), True)
    return ew_call(
        name, (nb,), fn,
        [(dh, pl.BlockSpec((bm, D), lambda i: (i, 0))),
         (x, pl.BlockSpec((bm, D), lambda i: (i, 0))),
         (dres, pl.BlockSpec((bm, D), dres_map)),
         (norm_g3, pl.BlockSpec((None, 1, D), lambda i: (stage, 0, 0))),
         (modv, pl.BlockSpec((None, None, 1, D), lambda i: (typ(i), 3 * stage + 1, 0, 0)))],
        [(_sds((dm.S, D) if out_lat_only else x.shape, F32),
          pl.BlockSpec((bm, D), (lambda i: (jnp.maximum(i - nCb, 0), 0)) if out_lat_only else (lambda i: (i, 0))), False),
         acc, acc, acc], first=first)


def gate_bwd(name, dm, dx, f, modv, gidx, scale, rows_T):
    D, bm = dm.D, dm.bm
    nb = dm.nTb if rows_T else dm.nSb
    nCb = dm.nCb
    typ = (lambda i: jnp.where(i < nCb, 0, 1)) if rows_T else (lambda i: 1)

    def fn(ids, dx_ref, f_ref, g_ref):
        dxv = dx_ref[...]
        return [scale * g_ref[...] * dxv, jnp.sum(scale * f_ref[...].astype(F32) * dxv, axis=0, keepdims=True)]

    if rows_T:
        first = lambda ids: (ids[0] == 0) | (ids[0] == nCb)
    else:
        first = lambda ids: ids[0] == 0
    return ew_call(
        name, (nb,), fn,
        [(dx, pl.BlockSpec((bm, D), lambda i: (i, 0))),
         (f, pl.BlockSpec((bm, D), lambda i: (i, 0))),
         (modv, pl.BlockSpec((None, None, 1, D), lambda i: (typ(i), gidx, 0, 0)))],
        [(_sds(dx.shape, BF), pl.BlockSpec((bm, D), lambda i: (i, 0)), False),
         (_sds((2, 1, D), F32), pl.BlockSpec((None, 1, D), lambda i: (typ(i), 0, 0)), True)], first=first)


def ffn_fwd(name, dm, h, xres, wg, wu, wd, modv, gidx, rows_T, comm_up=None, comm_down=None):
    D, F4, NS = dm.D, dm.F4, dm.NS
    M = h.shape[0]
    bm = dm.mT if rows_T else dm.mS
    C = dm.C

    def epi_up(ids, accs, ex):
        a, u = accs
        return [a, u, a * _sig(a) * u]

    hspec = pl.BlockSpec((bm, D), lambda j, i, k: (i, 0))
    wspec = pl.BlockSpec((None, D, F4), lambda j, i, k: (j, 0, 0))
    ospec = pl.BlockSpec((bm, F4), lambda j, i, k: (i, j))
    res = fused_mm(
        name + "_up", (NS, M // bm, 1), [(h, hspec), (wg, wspec), (wu, wspec)],
        [(0, 1, NN, 0), (0, 2, NN, 1)], [(bm, F4), (bm, F4)], epi_up,
        [(_sds((M, dm.DFF), BF), ospec)] * 3, comm=comm_up)
    (a, u, s), land_up = res if comm_up is not None else (res, None)

    bn = _pick(D, [1024, 512, 256, 128])

    def epi_dn(ids, accs, ex):
        f = accs[0]
        if rows_T:
            row = ids[0] * bm + lax.broadcasted_iota(jnp.int32, (bm, 1), 0)
            gate = jnp.where(row < C, ex[1][...], ex[2][...])
        else:
            gate = ex[2][...]
        return [ex[0][...] + FFN_RES * gate * f, f]

    gspec = lambda t: pl.BlockSpec((None, None, 1, bn), lambda i, j, k: (t, gidx, 0, j))
    res = fused_mm(
        name + "_down", (M // bm, D // bn, NS),
        [(s, pl.BlockSpec((bm, F4), lambda i, j, k: (i, k))),
         (wd, pl.BlockSpec((None, F4, bn), lambda i, j, k: (k, 0, j)))],
        [(0, 1, NN, 0)], [(bm, bn)], epi_dn,
        [(_sds((M, D), F32), pl.BlockSpec((bm, bn), lambda i, j, k: (i, j))),
         (_sds((M, D), BF), pl.BlockSpec((bm, bn), lambda i, j, k: (i, j)))],
        extras=[(xres, pl.BlockSpec((bm, bn), lambda i, j, k: (i, j))), (modv, gspec(0)), (modv, gspec(1))],
        comm=comm_down)
    (xo, f), land_down = res if comm_down is not None else (res, None)
    return xo, a, u, s, f, land_up, land_down


def ffn_bwd(name, dm, df, h, a, u, s, wg, wu, wd, rows_T, comms=None):
    comms = comms or {}
    landed = {}

    def run(key, *args, **kw):
        comm = comms[key](landed) if key in comms else None
        res = fused_mm(*args, comm=comm, **kw)
        if comm is not None:
            res, landed[key] = res
        return res

    D, F4, NS = dm.D, dm.F4, dm.NS
    M = h.shape[0]
    bm = dm.mT if rows_T else dm.mS
    bkr = dm.kT if rows_T else dm.kS

    def epi_ds(ids, accs, ex):
        ds = accs[0]
        av = ex[0][...].astype(F32)
        uv = ex[1][...].astype(F32)
        sg = _sig(av)
        return [ds * uv * (sg * (1.0 + av * (1.0 - sg))), ds * av * sg]

    ospec = pl.BlockSpec((bm, F4), lambda j, i, k: (i, j))
    da, du = run(
        'ds', name + "_ds", (NS, M // bm, 1),
        [(df, pl.BlockSpec((bm, D), lambda j, i, k: (i, 0))),
         (wd, pl.BlockSpec((None, F4, D), lambda j, i, k: (j, 0, 0)))],
        [(0, 1, NT, 0)], [(bm, F4)], epi_ds, [(_sds((M, dm.DFF), BF), ospec)] * 2,
        extras=[(a, ospec), (u, ospec)])

    ident = lambda ids, accs, ex: list(accs)
    bn = _pick(D, [1024, 512, 256, 128])
    dwd = run(
        'dwd', name + "_dwd", (NS, D // bn, M // bkr),
        [(s, pl.BlockSpec((bkr, F4), lambda i, j, k: (k, i))),
         (df, pl.BlockSpec((bkr, bn), lambda i, j, k: (k, j)))],
        [(0, 1, TN, 0)], [(F4, bn)], ident,
        [(_sds((NS, F4, D), BF), pl.BlockSpec((None, F4, bn), lambda i, j, k: (i, 0, j)))])[0]

    dwg, dwu = run(
        'dwgu', name + "_dwgu", (D // bn, NS, M // bkr),
        [(h, pl.BlockSpec((bkr, bn), lambda i, j, k: (k, i))),
         (da, pl.BlockSpec((bkr, F4), lambda i, j, k: (k, j))),
         (du, pl.BlockSpec((bkr, F4), lambda i, j, k: (k, j)))],
        [(0, 1, TN, 0), (0, 2, TN, 1)], [(bn, F4), (bn, F4)], ident,
        [(_sds((NS, D, F4), BF), pl.BlockSpec((None, bn, F4), lambda i, j, k: (j, i, 0)))] * 2)

    dh = run(
        'dh', name + "_dh", (M // bm, D // bn, NS),
        [(da, pl.BlockSpec((bm, F4), lambda i, j, k: (i, k))),
         (wg, pl.BlockSpec((None, bn, F4), lambda i, j, k: (k, j, 0))),
         (du, pl.BlockSpec((bm, F4), lambda i, j, k: (i, k))),
         (wu, pl.BlockSpec((None, bn, F4), lambda i, j, k: (k, j, 0)))],
        [(0, 1, NT, 0), (2, 3, NT, 0)], [(bm, bn)], ident,
        [(_sds((M, D), F32), pl.BlockSpec((bm, bn), lambda i, j, k: (i, j)))])[0]
    return dh, dwg, dwu, dwd, landed


def qk_prep(dm, P, gq, gk, cosf, sinf):
    D, KVW, bm = dm.D, dm.KVW, dm.bm

    def head_norm_rope(xh, g, c, s):
        r = lax.rsqrt(jnp.mean(xh * xh, axis=-1, keepdims=True) + EPS)
        y = xh * r * g
        return y * c + _pair_swap(y) * s

    def fn(ids, q_ref, k_ref, v_ref, gq_ref, gk_ref, c_ref, s_ref):
        c, s = c_ref[...], s_ref[...]
        qs = [head_norm_rope(q_ref[:, h * HEAD_DIM:(h + 1) * HEAD_DIM], gq_ref[...], c, s) for h in range(dm.NQ)]
        ks = [head_norm_rope(k_ref[:, h * HEAD_DIM:(h + 1) * HEAD_DIM], gk_ref[...], c, s) for h in range(dm.NKV)]
        return [jnp.concatenate(qs, axis=1), jnp.concatenate(ks, axis=1), v_ref[...]]

    hspec = pl.BlockSpec((bm, HEAD_DIM), lambda i: (i, 0))
    vec = pl.BlockSpec((1, HEAD_DIM), lambda i: (0, 0))
    return ew_call(
        "qk_prep", (dm.nTb,), fn,
        [(P, pl.BlockSpec((bm, D), lambda i: (i, 0))),
         (P, pl.BlockSpec((bm, KVW), lambda i: (i, dm.OFF_K // KVW))),
         (P, pl.BlockSpec((bm, KVW), lambda i: (i, dm.OFF_V // KVW))),
         (gq, vec), (gk, vec), (cosf, hspec), (sinf, hspec)],
        [(_sds((dm.T, D), BF), pl.BlockSpec((bm, D), lambda i: (i, 0)), False),
         (_sds((dm.T, KVW), BF), pl.BlockSpec((bm, KVW), lambda i: (i, 0)), False),
         (_sds((dm.T, KVW), BF), pl.BlockSpec((bm, KVW), lambda i: (i, 0)), False)])


def qk_prep_bwd(dm, dq, dk, dv, P, gq, gk, cosf, sinf, dP):
    D, KVW, bm, nCb = dm.D, dm.KVW, dm.bm, dm.nCb
    W = D + 2 * KVW

    def head_bwd(d, xh, g, c, s):
        dy = d * c - _pair_swap(d) * s
        r = lax.rsqrt(jnp.mean(xh * xh, axis=-1, keepdims=True) + EPS)
        xn = xh * r
        dg = jnp.sum(dy * xn, axis=0, keepdims=True)
        dxn = dy * g
        return r * (dxn - xn * jnp.mean(dxn * xn, axis=-1, keepdims=True)), dg

    def fn(ids, dq_ref, dk_ref, dv_ref, q_ref, k_ref, gq_ref, gk_ref, c_ref, s_ref, dp_any):
        i = ids[0]
        c, s = c_ref[...], s_ref[...]
        lat = i >= nCb
        outs, dgq = [], jnp.zeros((1, HEAD_DIM), F32)
        for h in range(dm.NQ):
            sl = slice(h * HEAD_DIM, (h + 1) * HEAD_DIM)
            d = jnp.where(lat, dq_ref[:, sl], 0.0)
            dx, dg = head_bwd(d, q_ref[:, sl], gq_ref[...], c, s)
            outs.append(dx)
            dgq = dgq + dg
        dgk = jnp.zeros((1, HEAD_DIM), F32)
        for h in range(dm.NKV):
            sl = slice(h * HEAD_DIM, (h + 1) * HEAD_DIM)
            dx, dg = head_bwd(dk_ref[:, sl], k_ref[:, sl], gk_ref[...], c, s)
            outs.append(dx)
            dgk = dgk + dg
        outs.append(dv_ref[...])
        return [jnp.concatenate(outs, axis=1), dgq, dgk]

    hspec = pl.BlockSpec((bm, HEAD_DIM), lambda i: (i, 0))
    vec = pl.BlockSpec((1, HEAD_DIM), lambda i: (0, 0))
    return ew_call(
        "qk_prep_bwd", (dm.nTb,), fn,
        [(dq, pl.BlockSpec((bm, D), lambda i: (jnp.maximum(i - nCb, 0), 0))),
         (dk, pl.BlockSpec((bm, KVW), lambda i: (i, 0))),
         (dv, pl.BlockSpec((bm, KVW), lambda i: (i, 0))),
         (P, pl.BlockSpec((bm, D), lambda i: (i, 0))),
         (P, pl.BlockSpec((bm, KVW), lambda i: (i, dm.OFF_K // KVW))),
         (gq, vec), (gk, vec), (cosf, hspec), (sinf, hspec), (dP, ANY)],
        [(_sds(dP.shape, BF), pl.BlockSpec((bm, W), lambda i: (i, 0)), False),
         (_sds((1, HEAD_DIM), F32), vec, True), (_sds((1, HEAD_DIM), F32), vec, True)],
        first=lambda ids: ids[0] == 0, aliases={9: 0})


def attention_fwd(dm, qr, kr, vb, comm=None):
    S, T, D, G, nCb = dm.S, dm.T, dm.D, dm.G, dm.nCb
    bq = dm.bq
    off = dm.C // bq
    scale = HEAD_DIM ** -0.5
    GW = G * HEAD_DIM

    def body(q_ref, k_ref, v_ref, o_ref):
        k = k_ref[...]
        v = v_ref[...]
        for h in range(G):
            sl = slice(h * HEAD_DIM, (h + 1) * HEAD_DIM)
            s = lax.dot_general(q_ref[:, sl], k, (NT, ((), ())), preferred_element_type=F32) * scale
            m = jnp.max(s, axis=-1, keepdims=True)
            p = jnp.exp(s - m)
            l = jnp.sum(p, axis=-1, keepdims=True)
            o = lax.dot_general(p.astype(BF), v, (NN, ((), ())), preferred_element_type=F32)
            o_ref[:, sl] = o / l

    return hosted_call(
        body, grid=(dm.NKV, S // bq), name="attn_fwd",
        in_specs=[pl.BlockSpec((bq, GW), lambda g, i: (i + off, g)),
                  pl.BlockSpec((T, HEAD_DIM), lambda g, i: (0, g)),
                  pl.BlockSpec((T, HEAD_DIM), lambda g, i: (0, g))],
        out_specs=[pl.BlockSpec((bq, GW), lambda g, i: (i, g))],
        out_shape=[_sds((S, D), F32)], args=[qr, kr, vb], comm=comm)


def attention_bwd(dm, qr, kr, vb, dattn, comm=None):
    S, T, D, G = dm.S, dm.T, dm.D, dm.G
    bq = dm.bq
    off = dm.C // bq
    scale = HEAD_DIM ** -0.5
    GW = G * HEAD_DIM

    def body(q_ref, k_ref, v_ref, do_ref, dq_ref, dk_ref, dv_ref):
        i = pl.program_id(1)

        @pl.when(i == 0)
        def _():
            dk_ref[...] = jnp.zeros(dk_ref.shape, F32)
            dv_ref[...] = jnp.zeros(dv_ref.shape, F32)

        k = k_ref[...]
        v = v_ref[...]
        for h in range(G):
            sl = slice(h * HEAD_DIM, (h + 1) * HEAD_DIM)
            q = q_ref[:, sl]
            do = do_ref[:, sl]
            s = lax.dot_general(q, k, (NT, ((), ())), preferred_element_type=F32) * scale
            m = jnp.max(s, axis=-1, keepdims=True)
            e = jnp.exp(s - m)
            p = e / jnp.sum(e, axis=-1, keepdims=True)
            pb = p.astype(BF)
            dv_ref[...] += lax.dot_general(pb, do, (TN, ((), ())), preferred_element_type=F32)
            dp = lax.dot_general(do, v, (NT, ((), ())), preferred_element_type=F32)
            ds = p * (dp - jnp.sum(p * dp, axis=-1, keepdims=True)) * scale
            dsb = ds.astype(BF)
            dq_ref[:, sl] = lax.dot_general(dsb, k, (NN, ((), ())), preferred_element_type=F32)
            dk_ref[...] += lax.dot_general(dsb, q, (TN, ((), ())), preferred_element_type=F32)

    return hosted_call(
        body, grid=(dm.NKV, S // bq), name="attn_bwd",
        in_specs=[pl.BlockSpec((bq, GW), lambda g, i: (i + off, g)),
                  pl.BlockSpec((T, HEAD_DIM), lambda g, i: (0, g)),
                  pl.BlockSpec((T, HEAD_DIM), lambda g, i: (0, g)),
                  pl.BlockSpec((bq, GW), lambda g, i: (i + off, g))],
        out_specs=[pl.BlockSpec((bq, GW), lambda g, i: (i, g)),
                   pl.BlockSpec((T, HEAD_DIM), lambda g, i: (0, g)),
                   pl.BlockSpec((T, HEAD_DIM), lambda g, i: (0, g))],
        out_shape=[_sds((S, D), F32), _sds((T, dm.KVW), F32), _sds((T, dm.KVW), F32)],
        args=[qr, kr, vb, dattn], comm=comm)


def _conv_taps(dm, lx, masks_only=False):
    T, C = dm.T, dm.C
    t = lax.broadcasted_iota(jnp.int32, (T, 1), 0)
    valid = [(t >= 2) & ((t < C) | (t >= C + 2)), (t >= 1) & ((t < C) | (t >= C + 1)), None,
             (t != C - 1) & (t != T - 1)]
    shifts = [2, 1, 0, T - 1]
    taps = []
    for k in range(4):
        if k == 2:
            taps.append(lx)
        else:
            taps.append(jnp.where(valid[k], pltpu.roll(lx, shifts[k], 0), 0.0))
    return taps


def _scan_tiles(dm, asc, split, a_ref, u_ref, out_ref, mode):
    T, C = dm.T, dm.C
    nT, nC = T // 8, C // 8
    row = lax.broadcasted_iota(jnp.int32, (8, HEAD_DIM), 0)

    def tile_of(i):
        if not split:
            return i if asc else nT - 1 - i
        if asc:
            return jnp.where(i < nT - nC, nC + i, i - (nT - nC))
        return jnp.where(i < nC, nC - 1 - i, nT - 1 - (i - nC))

    def body(i, carry):
        off = pl.multiple_of(tile_of(i) * 8, 8)
        a = a_ref[pl.ds(off, 8), :]
        b = u_ref[pl.ds(off, 8), :]
        if mode == 'lam':
            if asc:
                coef = jnp.where(row == 0, 1.0, pltpu.roll(a, 1, 0))
            else:
                coef = jnp.where(row == 7, 1.0, pltpu.roll(a, 7, 0))
        else:
            coef = a
        A, B = coef, b
        for d in (1, 2, 4):
            if asc:
                ok = row >= d
                A_sh = jnp.where(ok, pltpu.roll(A, d, 0), 1.0)
                B_sh = jnp.where(ok, pltpu.roll(B, d, 0), 0.0)
            else:
                ok = row < 8 - d
                A_sh = jnp.where(ok, pltpu.roll(A, 8 - d, 0), 1.0)
                B_sh = jnp.where(ok, pltpu.roll(B, 8 - d, 0), 0.0)
            B = B + A * B_sh
            A = A * A_sh
        h = A * carry + B
        out_ref[pl.ds(off, 8), :] = h
        last = h[7:8, :] if asc else h[0:1, :]
        if mode == 'lam':
            last = last * (a[7:8, :] if asc else a[0:1, :])
        return jnp.broadcast_to(last, (8, HEAD_DIM))

    lax.fori_loop(0, nT, body, jnp.zeros((8, HEAD_DIM), F32))


def _lru_gates(xc, wa, ba, wx, bx, sp):
    xb = xc.astype(BF)
    r = _sig(jnp.dot(xb, wa, preferred_element_type=F32) + ba)
    i = _sig(jnp.dot(xb, wx, preferred_element_type=F32) + bx)
    a = jnp.exp(-LRU_C * r * sp)
    m = jnp.sqrt(1.0 - a * a)
    return r, i, a, m


def lru_fwd(dm, P, conv_w, conv_b, wa, ba, wx, bx, sp, direction):
    T, D, LB = dm.T, dm.D, dm.LB
    W = dm.LBD
    R = _pick(T, [272, 256, 128, 64, 8])
    lxb = dm.OFF_LX // W

    def body(lx_ref, cw_ref, cb_ref, wa_ref, ba_ref, wx_ref, bx_ref, sp_ref, h_ref, xc_ref, a_ref):
        taps = _conv_taps(dm, lx_ref[...])
        xc = cb_ref[...]
        for k in range(4):
            xc = xc + taps[k] * cw_ref[k:k + 1, :]
        xc_ref[...] = xc
        wa_, wx_ = wa_ref[...].astype(BF), wx_ref[...].astype(BF)

        def chunk(ci, _):
            off = pl.multiple_of(ci * R, 8)
            x = xc_ref[pl.ds(off, R), :]
            r, i, a, m = _lru_gates(x, wa_, ba_ref[...], wx_, bx_ref[...], sp_ref[...])
            a_ref[pl.ds(off, R), :] = a
            h_ref[pl.ds(off, R), :] = m * i * x
            return 0

        lax.fori_loop(0, T // R, chunk, 0)
        _scan_tiles(dm, direction == 0, direction == 1, a_ref, h_ref, h_ref, 'h')

    strip = lambda j: (0, j)
    vec = pl.BlockSpec((None, 1, W), lambda j: (direction, 0, j))
    mat = pl.BlockSpec((None, None, W, W), lambda j: (direction, j, 0, 0))
    return pl.pallas_call(
        body, grid=(LB,), name="lru_fwd%d" % direction,
        in_specs=[pl.BlockSpec((T, W), lambda j: (0, lxb + j)),
                  pl.BlockSpec((4, W), strip), pl.BlockSpec((1, W), strip), mat, vec, mat, vec, vec],
        out_specs=pl.BlockSpec((T, W), strip), out_shape=_sds((T, D), F32),
        scratch_shapes=[pltpu.VMEM((T, W), F32), pltpu.VMEM((T, W), F32)], compiler_params=_cparams(),
    )(P, conv_w, conv_b, wa, ba, wx, bx, sp)


def lru_bwd(dm, P, dh, h, conv_w, conv_b, wa, ba, wx, bx, sp, sg, direction, dxc_in, dP):
    T, C, D, LB = dm.T, dm.C, dm.D, dm.LB
    W = dm.LBD
    R = _pick(T, [272, 256, 128, 64, 8])
    lxb = dm.OFF_LX // W
    last = direction == 1

    def body(*refs):
        (lx_ref, dh_ref, h_ref, cw_ref, cb_ref, wa_ref, ba_ref, wx_ref, bx_ref, sp_ref, sg_ref) = refs[:11]
        pos = 11
        if last:
            dxin_ref, _dp_any = refs[pos:pos + 2]
            pos += 2
        out0 = refs[pos]
        dwa_ref, dba_ref, dwx_ref, dbx_ref, dlam_ref = refs[pos + 1:pos + 6]
        pos += 6
        if last:
            dcw_ref, dcb_ref = refs[pos:pos + 2]
            pos += 2
        xc_ref, a_ref, lam_ref, hp_ref, dxc_ref = refs[pos:pos + 5]

        lx = lx_ref[...]
        taps = _conv_taps(dm, lx)
        xc = cb_ref[...]
        for k in range(4):
            xc = xc + taps[k] * cw_ref[k:k + 1, :]
        xc_ref[...] = xc
        wa_, wx_ = wa_ref[...].astype(BF), wx_ref[...].astype(BF)

        def chunk_a(ci, _):
            off = pl.multiple_of(ci * R, 8)
            r, i, a, m = _lru_gates(xc_ref[pl.ds(off, R), :], wa_, ba_ref[...], wx_, bx_ref[...], sp_ref[...])
            a_ref[pl.ds(off, R), :] = a
            return 0

        lax.fori_loop(0, T // R, chunk_a, 0)
        _scan_tiles(dm, direction == 1, direction == 1, a_ref, dh_ref, lam_ref, 'lam')
        t = lax.broadcasted_iota(jnp.int32, (T, 1), 0)
        hv = h_ref[...]
        if direction == 0:
            hp_ref[...] = jnp.where(t == 0, 0.0, pltpu.roll(hv, 1, 0))
        else:
            nxt = pltpu.roll(hv, T - 1, 0)
            hp_ref[...] = jnp.where(t == C - 1, 0.0, jnp.where(t == T - 1, jnp.broadcast_to(hv[0:1, :], hv.shape), nxt))

        def chunk_b(ci, carry):
            dwa, dwx, dba, dbx, dlam = carry
            off = pl.multiple_of(ci * R, 8)
            x = xc_ref[pl.ds(off, R), :]
            r, i, a, m = _lru_gates(x, wa_, ba_ref[...], wx_, bx_ref[...], sp_ref[...])
            lam = lam_ref[pl.ds(off, R), :]
            ix = i * x
            da = lam * hp_ref[pl.ds(off, R), :] - lam * ix * a / m
            dloga = da * a
            dza = dloga * (-LRU_C) * sp_ref[...] * r * (1.0 - r)
            dzx = lam * m * x * i * (1.0 - i)
            dzab, dzxb = dza.astype(BF), dzx.astype(BF)
            xb = x.astype(BF)
            dxc = lam * m * i
            dxc = dxc + lax.dot_general(dzab, wa_, (NT, ((), ())), preferred_element_type=F32)
            dxc = dxc + lax.dot_general(dzxb, wx_, (NT, ((), ())), preferred_element_type=F32)
            dxc_ref[pl.ds(off, R), :] = dxc
            dwa = dwa + lax.dot_general(xb, dzab, (TN, ((), ())), preferred_element_type=F32)
            dwx = dwx + lax.dot_general(xb, dzxb, (TN, ((), ())), preferred_element_type=F32)
            dba = dba + jnp.sum(dza, axis=0, keepdims=True)
            dbx = dbx + jnp.sum(dzx, axis=0, keepdims=True)
            dlam = dlam + jnp.sum(dloga * LRU_C * r, axis=0, keepdims=True)
            return dwa, dwx, dba, dbx, dlam

        z = jnp.zeros((W, W), F32)
        zv = jnp.zeros((1, W), F32)
        dwa, dwx, dba, dbx, dlam = lax.fori_loop(0, T // R, chunk_b, (z, z, zv, zv, zv))
        dwa_ref[...] = dwa
        dwx_ref[...] = dwx
        dba_ref[...] = dba
        dbx_ref[...] = dbx
        dlam_ref[...] = dlam * sg_ref[...]
        if not last:
            out0[...] = dxc_ref[...]
        else:
            dxc = dxc_ref[...] + dxin_ref[...]
            dcb_ref[...] = jnp.sum(dxc, axis=0, keepdims=True)
            rows = [jnp.sum(dxc * taps[k], axis=0, keepdims=True) for k in range(4)]
            dcw_ref[...] = jnp.concatenate(rows, axis=0)
            valid = [(t < T - 2) & ((t >= C) | (t < C - 2)), (t < T - 1) & ((t >= C) | (t < C - 1)), None,
                     (t != 0) & (t != C)]
            shifts = [T - 2, T - 1, 0, 1]
            dlx = dxc * cw_ref[2:3, :]
            for k in (0, 1, 3):
                dlx = dlx + jnp.where(valid[k], pltpu.roll(dxc, shifts[k], 0), 0.0) * cw_ref[k:k + 1, :]
            out0[...] = dlx.astype(out0.dtype)

    strip = lambda j: (0, j)
    sspec = pl.BlockSpec((T, W), strip)
    vec = pl.BlockSpec((None, 1, W), lambda j: (direction, 0, j))
    mat = pl.BlockSpec((None, None, W, W), lambda j: (direction, j, 0, 0))
    ins = [P, dh, h, conv_w, conv_b, wa, ba, wx, bx, sp, sg]
    in_specs = [pl.BlockSpec((T, W), lambda j: (0, lxb + j)), sspec, sspec,
                pl.BlockSpec((4, W), strip), pl.BlockSpec((1, W), strip), mat, vec, mat, vec, vec, vec]
    omat = pl.BlockSpec((None, W, W), lambda j: (j, 0, 0))
    ovec = pl.BlockSpec((1, W), strip)
    small = [_sds((LB, W, W), F32), _sds((1, D), F32), _sds((LB, W, W), F32), _sds((1, D), F32), _sds((1, D), F32)]
    small_specs = [omat, ovec, omat, ovec, ovec]
    if last:
        ins += [dxc_in, dP]
        in_specs += [sspec, ANY]
        out_shape = [_sds(dP.shape, BF)] + small + [_sds((4, D), F32), _sds((1, D), F32)]
        out_specs = [pl.BlockSpec((T, W), lambda j: (0, lxb + j))] + small_specs + [pl.BlockSpec((4, W), strip), ovec]
        aliases = {12: 0}
    else:
        out_shape = [_sds((T, D), F32)] + small
        out_specs = [sspec] + small_specs
        aliases = {}
    return pl.pallas_call(
        body, grid=(LB,), name="lru_bwd%d" % direction, in_specs=in_specs, out_specs=out_specs, out_shape=out_shape,
        scratch_shapes=[pltpu.VMEM((T, W), F32)] * 5, input_output_aliases=aliases, compiler_params=_cparams(),
    )(*ins)


def merge_fwd(dm, P, attn, hf, hb):
    S, D, bm, cw, nCb = dm.S, dm.D, dm.bm, dm.cw, dm.nCb

    def fn(ids, lg_ref, ga_ref, gl_ref, at_ref, hf_ref, hb_ref):
        ge, _ = _gelu(lg_ref[...])
        lru = (hf_ref[...] + hb_ref[...]) * ge
        return [_sig(ga_ref[...]) * at_ref[...] + _sig(gl_ref[...]) * lru]

    pspec = lambda off: pl.BlockSpec((bm, cw), lambda i, j: (i + nCb, off // cw + j))
    tspec = pl.BlockSpec((bm, cw), lambda i, j: (i + nCb, j))
    sspec = pl.BlockSpec((bm, cw), lambda i, j: (i, j))
    return ew_call(
        "merge_fwd", (dm.nSb, D // cw), fn,
        [(P, pspec(dm.OFF_LG)), (P, pspec(dm.OFF_GA)), (P, pspec(dm.OFF_GL)), (attn, sspec), (hf, tspec), (hb, tspec)],
        [(_sds((S, D), BF), sspec, False)])[0]


def merge_bwd(dm, dmg, P, attn, hf, hb):
    S, T, D, bm, cw, nCb = dm.S, dm.T, dm.D, dm.bm, dm.cw, dm.nCb
    nj = D // cw

    def body(dm_ref, lg_ref, ga_ref, gl_ref, at_ref, hf_ref, hb_ref, dp_ref, da_ref, dh_ref, buf, sems):
        i, j = pl.program_id(0), pl.program_id(1)
        lat = i >= nCb
        d = jnp.where(lat, dm_ref[...].astype(F32), 0.0)
        lg = lg_ref[...]
        ge, th = _gelu(lg)
        hs = hf_ref[...] + hb_ref[...]
        sa, sl = _sig(ga_ref[...]), _sig(gl_ref[...])
        at = jnp.where(lat, at_ref[...], 0.0)
        dlru = d * sl
        buf[0] = (dlru * hs * _gelu_grad(lg, th)).astype(BF)
        buf[1] = (d * at * sa * (1.0 - sa)).astype(BF)
        buf[2] = (d * hs * ge * sl * (1.0 - sl)).astype(BF)
        da_ref[...] = (d * sa).astype(BF)
        dh_ref[...] = dlru * ge
        copies = []
        for g, off in enumerate((dm.OFF_LG, dm.OFF_GA, dm.OFF_GL)):
            col = pl.multiple_of(off + j * cw, 128)
            cp = pltpu.make_async_copy(buf.at[g], dp_ref.at[pl.ds(pl.multiple_of(i * bm, 8), bm), pl.ds(col, cw)],
                                       sems.at[g])
            cp.start()
            copies.append(cp)
        for cp in copies:
            cp.wait()

    pspec = lambda off: pl.BlockSpec((bm, cw), lambda i, j: (i, off // cw + j))
    tspec = pl.BlockSpec((bm, cw), lambda i, j: (i, j))
    lspec = pl.BlockSpec((bm, cw), lambda i, j: (jnp.maximum(i - nCb, 0), j))
    return pl.pallas_call(
        body, grid=(dm.nTb, nj), name="merge_bwd",
        in_specs=[lspec, pspec(dm.OFF_LG), pspec(dm.OFF_GA), pspec(dm.OFF_GL), lspec, tspec, tspec],
        out_specs=[ANY, tspec, tspec],
        out_shape=[_sds((T, dm.INW), BF), _sds((T, D), BF), _sds((T, D), F32)],
        scratch_shapes=[pltpu.VMEM((3, bm, cw), BF), pltpu.SemaphoreType.DMA((3,))],
        compiler_params=_cparams(),
    )(dmg, P, P, P, attn, hf, hb)


def final_loss(dm, x3, gfin, target):
    S, D, bm = dm.S, dm.D, dm.bm

    def fn(ids, x_ref, g_ref, t_ref):
        xv = x_ref[...]
        g = g_ref[...]
        r = lax.rsqrt(jnp.mean(xv * xv, axis=-1, keepdims=True) + EPS)
        xn = xv * r
        err = xn * g - t_ref[...]
        loss = 0.5 * jnp.sum(jnp.mean(err * err, axis=-1, keepdims=True), axis=0, keepdims=True)
        dy = err / D
        dxn = dy * g
        dx = r * (dxn - xn * jnp.mean(dxn * xn, axis=-1, keepdims=True))
        return [jnp.broadcast_to(loss, (1, 128)), dx, jnp.sum(dy * xn, axis=0, keepdims=True)]

    row = pl.BlockSpec((bm, D), lambda i: (i, 0))
    vec = pl.BlockSpec((1, D), lambda i: (0, 0))
    return ew_call(
        "final_loss", (dm.nSb,), fn, [(x3, row), (gfin, vec), (target, row)],
        [(_sds((1, 128), F32), pl.BlockSpec((1, 128), lambda i: (0, 0)), True), (_sds((S, D), F32), row, False),
         (_sds((1, D), F32), vec, True)], first=lambda ids: ids[0] == 0)


def local_step(dm, x, ctx, target, modv, norm_g3, gfin, gq, gk, conv_w, conv_b, wa, ba, wx, bx, lam, wbuf, where):
    S, C, T, D, NS, F4, W4 = dm.S, dm.C, dm.T, dm.D, dm.NS, dm.F4, dm.W4
    Ds = D // NS
    wb, nsub = dm.wb, dm.nsub
    cosf, sinf = rope_tables(dm)
    sp = jax.nn.softplus(-lam)
    sg = jax.nn.sigmoid(-lam)
    ident = lambda ids, accs, ex: list(accs)
    mT, mS, kT, kS = dm.mT, dm.mS, dm.kT, dm.kS
    bn = _pick(D, [1024, 512, 256, 128])
    bk = _pick(D, [512, 256, 128])

    wg0, wu0, wd0 = comm_call("ag_ffn1", ag_comm([wbuf['wg0'], wbuf['wu0'], wbuf['wd0']]))
    h1, xt = normmod_concat_fwd("nm1", dm, ctx, x, norm_g3, modv)
    xt1, a1, u1, s1, f1, land, _ = ffn_fwd("ffn1", dm, h1, xt, wg0, wu0, wd0, modv, 2, True,
                                           comm_up=ag_comm([wbuf['w_in'], wbuf['w_out']]))
    w_in, w_out = land[0], land[1].reshape(D, D)
    h2 = normmod_fwd("nm2", dm, xt1, norm_g3, 1, modv, True)
    P = fused_mm(
        "w_in", (T // mT, NS * nsub, 1),
        [(h2, pl.BlockSpec((mT, D), lambda i, j, k: (i, 0))),
         (w_in, pl.BlockSpec((None, D, wb), lambda i, j, k: (j // nsub, 0, j % nsub)))],
        [(0, 1, NN, 0)], [(mT, wb)], ident,
        [(_sds((T, dm.INW), F32), pl.BlockSpec((mT, wb), lambda i, j, k: (i, j)))])[0]
    qr, kr, vb = qk_prep(dm, P, gq, gk, cosf, sinf)
    (attn,), (wg1, wu1, wd1) = attention_fwd(dm, qr, kr, vb, comm=ag_comm([wbuf['wg1'], wbuf['wu1'], wbuf['wd1']]))
    hf = lru_fwd(dm, P, conv_w, conv_b, wa, ba, wx, bx, sp, 0)
    hb = lru_fwd(dm, P, conv_w, conv_b, wa, ba, wx, bx, sp, 1)
    mg = merge_fwd(dm, P, attn, hf, hb)

    def epi_o(ids, accs, ex):
        o = accs[0]
        return [ex[0][...] + ex[1][...] * o, o]

    rb, nCb = dm.bm, dm.nCb
    x2, o2 = fused_mm(
        "w_out", (D // bn, S // rb, 1),
        [(mg, pl.BlockSpec((rb, D), lambda j, i, k: (i, 0))), (w_out, pl.BlockSpec((D, bn), lambda j, i, k: (0, j)))],
        [(0, 1, NN, 0)], [(rb, bn)], epi_o,
        [(_sds((S, D), F32), pl.BlockSpec((rb, bn), lambda j, i, k: (i, j))),
         (_sds((S, D), BF), pl.BlockSpec((rb, bn), lambda j, i, k: (i, j)))],
        extras=[(xt1, pl.BlockSpec((rb, bn), lambda j, i, k: (i + nCb, j))),
                (modv, pl.BlockSpec((None, None, 1, bn), lambda j, i, k: (1, 5, 0, j)))])
    h3 = normmod_fwd("nm3", dm, x2, norm_g3, 2, modv, False)
    x3, a3, u3, s3, f3, _, _ = ffn_fwd("ffn2", dm, h3, x2, wg1, wu1, wd1, modv, 8, False)
    loss, dx3, dgfin = final_loss(dm, x3, gfin, target)

    df3, dg3 = gate_bwd("gate3", dm, dx3, f3, modv, 8, FFN_RES, False)
    dh3, dwg1, dwu1, dwd1, _ = ffn_bwd("ffn2b", dm, df3, h3, a3, u3, s3, wg1, wu1, wd1, False)
    dx2, dsh3, dsc3, dgn3 = normmod_bwd("nm3b", dm, dh3, x2, dx3, norm_g3, 2, modv, False, False)
    do2, dg2 = gate_bwd("gate2", dm, dx2, o2, modv, 5, 1.0, False)
    keep = {}

    def host_a(key, comm):
        if key == 'p1':
            (keep['dmg'],), landed = fused_mm(
                "w_out_dx", (S // mS, D // bn, 1),
                [(do2, pl.BlockSpec((mS, D), lambda i, j, k: (i, 0))),
                 (w_out, pl.BlockSpec((bn, D), lambda i, j, k: (j, 0)))],
                [(0, 1, NT, 0)], [(mS, bn)], ident,
                [(_sds((S, D), BF), pl.BlockSpec((mS, bn), lambda i, j, k: (i, j)))], comm=comm)
            return landed
        keep['dqkv'], landed = attention_bwd(dm, qr, kr, vb, keep['dattn'], comm=comm)
        return landed

    gots_a = host_a('p1', rs_p1_comm([dwg1, dwu1, dwd1]))
    dmg = keep['dmg']
    dw_out = fused_mm(
        "w_out_dw", (D // bn, D // bn, S // kS),
        [(mg, pl.BlockSpec((kS, bn), lambda i, j, k: (k, i))), (do2, pl.BlockSpec((kS, bn), lambda i, j, k: (k, j)))],
        [(0, 1, TN, 0)], [(bn, bn)], ident,
        [(_sds((D, D), BF), pl.BlockSpec((bn, bn), lambda i, j, k: (i, j)))])[0]
    dP, dattn, dhs = merge_bwd(dm, dmg, P, attn, hf, hb)
    keep['dattn'] = dattn
    pairs_a = [add_pair("rs_add_" + n_, g_, got_, where)
               for n_, g_, got_ in zip(('wg1', 'wu1', 'wd1'), (dwg1, dwu1, dwd1), gots_a)]
    land_a = host_a('p2', rs_p2_comm([p_[0] for p_ in pairs_a], [p_[1] for p_ in pairs_a]))
    dq, dk, dv = keep['dqkv']
    dxc0, dwa0, dba0, dwx0, dbx0, dlam0 = lru_bwd(dm, P, dhs, hf, conv_w, conv_b, wa, ba, wx, bx, sp, sg, 0, None, None)
    dP, dwa1, dba1, dwx1, dbx1, dlam1, dcw, dcb = lru_bwd(dm, P, dhs, hb, conv_w, conv_b, wa, ba, wx, bx, sp, sg, 1,
                                                          dxc0, dP)
    dP, dgq, dgk = qk_prep_bwd(dm, dq, dk, dv, P, gq, gk, cosf, sinf, dP)
    g_wg = sum_slots_into("rs_sum_wg1", land_a[0], where, None, (2, D, F4), 1)
    g_wu = sum_slots_into("rs_sum_wu1", land_a[1], where, None, (2, D, F4), 1)
    g_wd = sum_slots_into("rs_sum_wd1", land_a[2], where, None, (2, F4, D), 1)
    (dh2,), (g_wg, g_wu, g_wd) = fused_mm(
        "w_in_dx", (T // mT, D // bn, NS),
        [(dP, pl.BlockSpec((mT, W4), lambda i, j, k: (i, k))),
         (w_in, pl.BlockSpec((None, bn, W4), lambda i, j, k: (k, j, 0)))],
        [(0, 1, NT, 0)], [(mT, bn)], ident,
        [(_sds((T, D), F32), pl.BlockSpec((mT, bn), lambda i, j, k: (i, j)))],
        comm=rs_p3_comm([g_wg, g_wu, g_wd], [(0, 1, D // 2), (1, 1, D // 2), (2, 1, F4 // 2)]))
    dw_in = fused_mm(
        "w_in_dw", (D // bn, NS, T // kT),
        [(h2, pl.BlockSpec((kT, bn), lambda i, j, k: (k, i))), (dP, pl.BlockSpec((kT, W4), lambda i, j, k: (k, j)))],
        [(0, 1, TN, 0)], [(bn, W4)], ident,
        [(_sds((NS, D, W4), BF), pl.BlockSpec((None, bn, W4), lambda i, j, k: (j, i, 0)))])[0]
    dxt1, dsh2, dsc2, dgn2 = normmod_bwd("nm2b", dm, dh2, xt1, dx2, norm_g3, 1, modv, True, True)
    df1, dg1 = gate_bwd("gate1", dm, dxt1, f1, modv, 2, FFN_RES, True)

    tens_b = [dw_in, dw_out.reshape(NS, Ds, D)]

    def phase2(landed):
        pairs = [add_pair("rs_add_" + n_, g_, got_, where) for n_, g_, got_ in zip(('w_in', 'w_out'), tens_b, landed['ds'])]
        return rs_p2_comm([p_[0] for p_ in pairs], [p_[1] for p_ in pairs])

    def phase3(landed):
        g_win = sum_slots_into("rs_sum_w_in", landed['dwgu'][0], where, None, (D, W4), None)
        g_wout = sum_slots_into("rs_sum_w_out", landed['dwgu'][1], where, None, (Ds, D), None)
        return rs_p3_comm([g_win, g_wout], [(0, None, D // 2), (1, None, Ds // 2)])

    dh1, dwg0, dwu0, dwd0, landed = ffn_bwd("ffn1b", dm, df1, h1, a1, u1, s1, wg0, wu0, wd0, True,
                                            comms={'ds': lambda L: rs_p1_comm(tens_b), 'dwgu': phase2, 'dh': phase3})
    g_win, g_wout = landed['dh']
    grad_x, dsh1, dsc1, dgn1 = normmod_bwd("nm1b", dm, dh1, xt, dxt1, norm_g3, 0, modv, True, False, out_lat_only=True)

    dmod = jnp.concatenate([dsh1, dsc1, dg1, dsh2, dsc2, _lat(dg2), _lat(dsh3), _lat(dsc3), _lat(dg3)], axis=1)
    dnorm = jnp.stack([dgn1[0, 0] + dgn1[1, 0], dgn2[0, 0] + dgn2[1, 0], dgn3[1, 0]], axis=0)
    small = dict(norm_g=dnorm, q_norm_g=dgq, k_norm_g=dgk, conv_w=dcw, conv_b=dcb,
                 lru_wa=jnp.stack([dwa0, dwa1]), lru_ba=jnp.concatenate([dba0, dba1], axis=0),
                 lru_wx=jnp.stack([dwx0, dwx1]), lru_bx=jnp.concatenate([dbx0, dbx1], axis=0),
                 lru_lambda=jnp.concatenate([dlam0, dlam1], axis=0), final_norm_g=dgfin)
    reduced = dict(ffn_wg=g_wg, ffn_wu=g_wu, ffn_wd=g_wd, w_in=g_win, w_out=g_wout)
    return loss, grad_x, dmod, small, reduced, [dwg0, dwu0, dwd0]


def _lat(v):
    return jnp.concatenate([jnp.zeros_like(v[:1]), v[1:]], axis=0)


def _me():
    return lax.axis_index("x"), lax.axis_index("y"), lax.axis_index("c")


def allgather8(name, v):
    def body(v_ref, out_ref, send_sems, recv_sems, local_sem):
        x, y, c = _me()
        me = 4 * x + 2 * y + c
        mine = pltpu.make_async_copy(v_ref, out_ref.at[me], local_sem)
        mine.start()
        copies = []
        for k in range(1, 8):
            peer = (x ^ ((k >> 2) & 1), y ^ ((k >> 1) & 1), c ^ (k & 1))
            cp = pltpu.make_async_remote_copy(src_ref=v_ref, dst_ref=out_ref.at[me], send_sem=send_sems.at[k - 1],
                                              recv_sem=recv_sems.at[k - 1], device_id=peer, device_id_type=MESH)
            cp.start()
            copies.append(cp)
        for k in range(1, 8):
            peer = (x ^ ((k >> 2) & 1), y ^ ((k >> 1) & 1), c ^ (k & 1))
            pltpu.make_async_remote_copy(src_ref=v_ref, dst_ref=out_ref.at[me ^ k], send_sem=send_sems.at[k - 1],
                                         recv_sem=recv_sems.at[k - 1], device_id=peer, device_id_type=MESH).wait_recv()
        for cp in copies:
            cp.wait_send()
        mine.wait()

    return pl.pallas_call(
        body, name=name, out_shape=_sds((8,) + v.shape, v.dtype), in_specs=[ANY], out_specs=ANY,
        scratch_shapes=[pltpu.SemaphoreType.DMA((7,)), pltpu.SemaphoreType.DMA((7,)), pltpu.SemaphoreType.DMA],
    )(v)


def _chips(x, y):
    chips = [(1 - x, y), (x, 1 - y), (1 - x, 1 - y)]
    return chips, [2 * cx + cy for cx, cy in chips]


def ag_comm(bufs):
    n = len(bufs)

    def parts(outs):
        x, y, c = _me()
        chips, slots = _chips(x, y)
        return x, y, c, 2 * x + y, (x, y, 1 - c), chips, slots

    def ici(outs, t, j, send_sems, recv_sems, src_slot):
        x, y, c, s, sib, chips, slots = parts(outs)
        H = outs[t].shape[1] // 2
        blk = outs[t].at[src_slot, pl.ds(c * H, H)]
        return pltpu.make_async_remote_copy(
            src_ref=blk, dst_ref=blk, send_sem=send_sems.at[6 * t + j], recv_sem=recv_sems.at[6 * t + j],
            device_id=(chips[j][0], chips[j][1], c), device_id_type=MESH)

    def d2d(outs, t, j, send_sems, recv_sems, half):
        x, y, c, s, sib, chips, slots = parts(outs)
        H = outs[t].shape[1] // 2
        blk = outs[t].at[slots[j], pl.ds(half * H, H)]
        return pltpu.make_async_remote_copy(
            src_ref=blk, dst_ref=blk, send_sem=send_sems.at[6 * t + 3 + j], recv_sem=recv_sems.at[6 * t + 3 + j],
            device_id=sib, device_id_type=MESH)

    def start(reads, outs, send_sems, recv_sems):
        x, y, c, s, sib, chips, slots = parts(outs)
        for t in range(n):
            for j in range(3):
                ici(outs, t, j, send_sems, recv_sems, s).start()

    def finish(reads, outs, send_sems, recv_sems):
        x, y, c, s, sib, chips, slots = parts(outs)
        for t in range(n):
            for j in range(3):
                ici(outs, t, j, send_sems, recv_sems, slots[j]).wait_recv()
                d2d(outs, t, j, send_sems, recv_sems, c).start()
        for t in range(n):
            for j in range(3):
                d2d(outs, t, j, send_sems, recv_sems, 1 - c).wait_recv()
        for t in range(n):
            for j in range(3):
                ici(outs, t, j, send_sems, recv_sems, s).wait_send()
                d2d(outs, t, j, send_sems, recv_sems, c).wait_send()

    return Comm([], bufs, 6 * n, start, finish)


def rs_p1_comm(tensors):
    n = len(tensors)

    def copy(ins, gots, t, send_sems, recv_sems):
        x, y, c = _me()
        H = ins[t].shape[1] // 2
        return pltpu.make_async_remote_copy(
            src_ref=ins[t].at[:, pl.ds((1 - c) * H, H)], dst_ref=gots[t], send_sem=send_sems.at[t],
            recv_sem=recv_sems.at[t], device_id=(x, y, 1 - c), device_id_type=MESH)

    def start(ins, gots, send_sems, recv_sems):
        for t in range(n):
            copy(ins, gots, t, send_sems, recv_sems).start()

    def finish(ins, gots, send_sems, recv_sems):
        for t in range(n):
            copy(ins, gots, t, send_sems, recv_sems).wait_recv()
        for t in range(n):
            copy(ins, gots, t, send_sems, recv_sems).wait_send()

    half = lambda t: _sds((t.shape[0], t.shape[1] // 2) + t.shape[2:], t.dtype)
    return Comm(tensors, [half(t) for t in tensors], n, start, finish)


def rs_p2_comm(partials, landeds):
    n = len(partials)

    def start(ins, outs, send_sems, recv_sems):
        x, y, c = _me()
        s = 2 * x + y
        chips, slots = _chips(x, y)
        for t in range(n):
            for j, chip in enumerate(chips):
                src = ins[t].at[slots[j]] if ins[t].shape[0] == 4 else ins[t].at[0]
                pltpu.make_async_remote_copy(
                    src_ref=src, dst_ref=outs[t].at[s], send_sem=send_sems.at[3 * t + j],
                    recv_sem=recv_sems.at[3 * t + j], device_id=(chip[0], chip[1], c), device_id_type=MESH).start()

    def finish(ins, outs, send_sems, recv_sems):
        x, y, c = _me()
        s = 2 * x + y
        chips, slots = _chips(x, y)
        for t in range(n):
            for j, chip in enumerate(chips):
                dst = outs[t].at[slots[j]]
                pltpu.make_async_remote_copy(
                    src_ref=dst, dst_ref=dst, send_sem=send_sems.at[3 * t + j],
                    recv_sem=recv_sems.at[3 * t + j], device_id=(chip[0], chip[1], c), device_id_type=MESH).wait_recv()
        for t in range(n):
            for j, chip in enumerate(chips):
                src = ins[t].at[slots[j]] if ins[t].shape[0] == 4 else ins[t].at[0]
                pltpu.make_async_remote_copy(
                    src_ref=src, dst_ref=outs[t].at[s], send_sem=send_sems.at[3 * t + j],
                    recv_sem=recv_sems.at[3 * t + j], device_id=(chip[0], chip[1], c), device_id_type=MESH).wait_send()

    return Comm(partials, landeds, 3 * n, start, finish)


def rs_p3_comm(greds, plan):
    n = len(plan)

    def copy(outs, t, send_sems, recv_sems, half):
        x, y, c = _me()
        oi, li, H = plan[t]
        dst = outs[oi] if li is None else outs[oi].at[li]
        blk = dst.at[pl.ds((c if half == 0 else 1 - c) * H, H)]
        return pltpu.make_async_remote_copy(
            src_ref=blk, dst_ref=blk, send_sem=send_sems.at[t], recv_sem=recv_sems.at[t],
            device_id=(x, y, 1 - c), device_id_type=MESH)

    def start(reads, outs, send_sems, recv_sems):
        for t in range(n):
            copy(outs, t, send_sems, recv_sems, 0).start()

    def finish(reads, outs, send_sems, recv_sems):
        for t in range(n):
            copy(outs, t, send_sems, recv_sems, 1).wait_recv()
        for t in range(n):
            copy(outs, t, send_sems, recv_sems, 0).wait_send()

    return Comm([], greds, n, start, finish)


def _rows_block(rows, cols, nbytes=1 << 20):
    bm = 8
    while bm * 2 * cols * 4 <= nbytes and rows % (bm * 2) == 0:
        bm *= 2
    return bm


def cast_into_slot(name, w, where, layer=None):
    rows, W = w.shape[-2:]
    bm = _rows_block(rows, W)

    def body(p_ref, w_ref, o_ref):
        o_ref[...] = w_ref[...].astype(BF)

    if layer is None:
        ispec = pl.BlockSpec((bm, W), lambda i, p: (i, 0))
    else:
        ispec = pl.BlockSpec((None, bm, W), lambda i, p: (layer, i, 0))
    return pl.pallas_call(
        body, name=name, out_shape=_sds((4, rows, W), BF), compiler_params=_cparams(),
        grid_spec=pltpu.PrefetchScalarGridSpec(
            num_scalar_prefetch=1, grid=(rows // bm,), in_specs=[ispec],
            out_specs=pl.BlockSpec((None, bm, W), lambda i, p: (p[1], i, 0))),
    )(where, w)


def add_pair(name, g, got, where):
    K, R, W = g.shape
    H = R // 2
    bm = _rows_block(H, W)
    nh = H // bm

    def body(p_ref, g_ref, got_ref, part_ref, land_ref):
        k = pl.program_id(1)
        v = (g_ref[...].astype(F32) + got_ref[...].astype(F32)).astype(part_ref.dtype)
        part_ref[...] = v
        own = (k == p_ref[1]) if K == 4 else (k == 0)

        @pl.when(own)
        def _():
            land_ref[...] = v

    return pl.pallas_call(
        body, name=name, out_shape=[_sds((K, H, W), g.dtype), _sds((4, H, W), g.dtype)], compiler_params=_cparams(),
        grid_spec=pltpu.PrefetchScalarGridSpec(
            num_scalar_prefetch=1, grid=(nh, K),
            in_specs=[pl.BlockSpec((None, bm, W), lambda i, k, p: (k, p[0] * nh + i, 0)),
                      pl.BlockSpec((None, bm, W), lambda i, k, p: (k, i, 0))],
            out_specs=[pl.BlockSpec((None, bm, W), lambda i, k, p: (k, i, 0)),
                       pl.BlockSpec((None, bm, W), lambda i, k, p: (p[1], i, 0))]),
    )(where, g, got)


def sum_slots_into(name, landed, where, dest, dest_shape, li):
    K, H, W = landed.shape
    bm = _rows_block(H, 2 * W)
    nh = H // bm

    def body(*refs):
        r, o_ref = refs[1], refs[-1]
        acc = r[0].astype(F32)
        for k in range(1, K):
            acc = acc + r[k].astype(F32)
        o_ref[...] = acc

    if li is None:
        ospec = pl.BlockSpec((bm, W), lambda i, p: (p[0] * nh + i, 0))
    else:
        ospec = pl.BlockSpec((None, bm, W), lambda i, p: (li, p[0] * nh + i, 0))
    in_specs = [pl.BlockSpec((K, bm, W), lambda i, p: (0, i, 0))]
    args = [where, landed]
    aliases = {}
    if dest is not None:
        in_specs.append(ANY)
        args.append(dest)
        aliases = {2: 0}
    return pl.pallas_call(
        body, name=name, out_shape=_sds(dest_shape, F32), compiler_params=_cparams(), input_output_aliases=aliases,
        grid_spec=pltpu.PrefetchScalarGridSpec(num_scalar_prefetch=1, grid=(nh,), in_specs=in_specs, out_specs=ospec),
    )(*args)


def _adamw_math(w, g, m, v):
    bc1 = 1.0 - ADAM_B1 ** ADAM_STEP
    bc2 = 1.0 - ADAM_B2 ** ADAM_STEP
    mn = ADAM_B1 * m + (1.0 - ADAM_B1) * g
    vn = ADAM_B2 * v + (1.0 - ADAM_B2) * (g * g)
    m_hat = mn / bc1
    v_hat = vn / bc2
    delta = -ADAM_LR * (m_hat / (jnp.sqrt(v_hat) + ADAM_EPS) + ADAM_WD * w)
    return delta, mn, vn


def adamw(name, w, g, m, v):
    shape = w.shape
    flat = lambda t: t.reshape(-1, shape[-1])
    w2, g2, m2, v2 = flat(w), flat(g), flat(m), flat(v)
    bm = _rows_block(w2.shape[0], w2.shape[1] * 2)

    def fn(ids, w_ref, g_ref, m_ref, v_ref):
        return list(_adamw_math(w_ref[...], g_ref[...], m_ref[...], v_ref[...]))

    spec = pl.BlockSpec((bm, w2.shape[1]), lambda i: (i, 0))
    outs = ew_call(name, (w2.shape[0] // bm,), fn, [(w2, spec), (g2, spec), (m2, spec), (v2, spec)],
                   [(_sds(w2.shape, F32), spec, False)] * 3)
    return [o.reshape(shape) for o in outs]


def adamw_many(name, params):
    n = len(params)
    shapes = [p_[0].shape for p_ in params]
    two_d = lambda t: t.reshape(-1, t.shape[-1])
    args = [two_d(t) for p_ in params for t in p_]

    def body(*refs):
        ins, outs = refs[:4 * n], refs[4 * n:]
        for q in range(n):
            w_ref, g_ref, m_ref, v_ref = ins[4 * q:4 * q + 4]
            for o_ref, val in zip(outs[3 * q:3 * q + 3], _adamw_math(w_ref[...], g_ref[...], m_ref[...], v_ref[...])):
                o_ref[...] = val

    full = lambda a: pl.BlockSpec(a.shape, lambda i: (0, 0))
    out_shape = [_sds(args[4 * q].shape, F32) for q in range(n) for _ in range(3)]
    res = hosted_call(body, name=name, grid=(1,), in_specs=[full(a) for a in args],
                      out_specs=[full(o) for o in out_shape], out_shape=out_shape, args=args)
    return [tuple(res[3 * q + r].reshape(shapes[q]) for r in range(3)) for q in range(n)]


def dmod_pack(gd):
    N = gd.shape[-1]
    bn = _pick(N, [4608, 2304, 1152, 1024, 512, 256, 128])

    def fn(ids, r):
        lat = [r[d, 1:2, :] for d in range(8)]
        cs = r[0, 0:1, :]
        for d in range(1, 8):
            cs = cs + r[d, 0:1, :]
        tot = cs
        for d in range(8):
            tot = tot + lat[d]
        return [jnp.concatenate(lat + [cs, jnp.zeros((7, bn), F32)], axis=0), tot]

    return ew_call("dmod_pack", (N // bn,), fn, [(gd, pl.BlockSpec((8, 2, bn), lambda j: (0, 0, j)))],
                   [(_sds((16, N), F32), pl.BlockSpec((16, bn), lambda j: (0, j)), False),
                    (_sds((1, N), F32), pl.BlockSpec((1, bn), lambda j: (0, j)), False)])


def _silu(v):
    return v * _sig(v)


def kernel(x, c, ctx, c_ctx, w_mod, b_mod, norm_g, ffn_wg, ffn_wu, ffn_wd, w_in, w_out, q_norm_g, k_norm_g, conv_w, conv_b, lru_wa, lru_ba, lru_wx, lru_bx, lru_lambda, final_norm_g, loss_target, m_c_ctx, m_w_mod, m_b_mod, m_norm_g, m_ffn_wg, m_ffn_wu, m_ffn_wd, m_w_in, m_w_out, m_q_norm_g, m_k_norm_g, m_conv_w, m_conv_b, m_lru_wa, m_lru_ba, m_lru_wx, m_lru_bx, m_lru_lambda, m_final_norm_g, v_c_ctx, v_w_mod, v_b_mod, v_norm_g, v_ffn_wg, v_ffn_wu, v_ffn_wd, v_w_in, v_w_out, v_q_norm_g, v_k_norm_g, v_conv_w, v_conv_b, v_lru_wa, v_lru_ba, v_lru_wx, v_lru_bx, v_lru_lambda, v_final_norm_g):
    given = dict(locals())
    names = ['c_ctx', 'w_mod', 'b_mod', 'norm_g', 'ffn_wg', 'ffn_wu', 'ffn_wd', 'w_in', 'w_out', 'q_norm_g', 'k_norm_g',
             'conv_w', 'conv_b', 'lru_wa', 'lru_ba', 'lru_wx', 'lru_bx', 'lru_lambda', 'final_norm_g']
    S, D = x.shape[1], x.shape[2]
    C = ctx.shape[1]
    NS = 4
    F4, W4, LB = ffn_wg.shape[-1], w_in.shape[-1], lru_wa.shape[2]
    dm = Dims(S, C, D, F4, W4, NS, LB)
    Ds = D // NS
    Wm = w_mod.shape[-1]
    xi, yi, ci = lax.axis_index("x"), lax.axis_index("y"), lax.axis_index("c")
    slot = 2 * xi + yi
    me = 4 * xi + 2 * yi + ci
    ident = lambda ids, accs, ex: list(accs)

    pack1 = jnp.concatenate([c.reshape(-1), norm_g.reshape(-1), conv_w.reshape(-1), lru_ba.reshape(-1),
                             lru_bx.reshape(-1), lru_lambda.reshape(-1)]).reshape(1, -1)
    g1 = allgather8("ag_small_params", pack1)[:, 0]
    c_all = g1[:, :D]

    def unshard(off, k):
        part = g1[0::2, off:off + k * Ds].reshape(NS, k, Ds)
        return jnp.transpose(part, (1, 0, 2)).reshape(k, D)

    norm_g_f = unshard(D, 3)
    conv_w_f = unshard(D + 3 * Ds, 4)
    ba_f = unshard(D + 7 * Ds, 2)
    bx_f = unshard(D + 9 * Ds, 2)
    lam_f = unshard(D + 11 * Ds, 2)

    call16 = jnp.concatenate([c_all, c_ctx.reshape(1, D), jnp.zeros((7, D), F32)], axis=0)
    b_cols = lax.dynamic_slice(b_mod, (0, slot * Wm), (1, Wm))
    bnm = _pick(Wm, [1536, 1152, 768, 512, 384, 256, 128])
    bkm = _pick(D, [512, 256, 128])
    modp = fused_mm(
        "mod_fwd", (1, Wm // bnm, D // bkm),
        [(call16, pl.BlockSpec((16, bkm), lambda i, j, k: (0, k))),
         (w_mod[0], pl.BlockSpec((bkm, bnm), lambda i, j, k: (k, j)))],
        [(0, 1, NN, 0)], [(16, bnm)], lambda ids, accs, ex: [accs[0] + ex[0][...]],
        [(_sds((16, Wm), F32), pl.BlockSpec((16, bnm), lambda i, j, k: (0, j)))],
        extras=[(b_cols, pl.BlockSpec((1, bnm), lambda i, j, k: (0, j)))], pre={0: _silu})[0]
    gm = allgather8("ag_mod", modp)
    mod_full = jnp.concatenate([gm[0], gm[2], gm[4], gm[6]], axis=1)
    mod_x = lax.dynamic_index_in_dim(mod_full, me, axis=0, keepdims=False)
    modv = jnp.stack([mod_full[8], mod_x]).reshape(2, N_MOD, 1, D)

    where = jnp.stack([ci, slot]).astype(jnp.int32)
    wbuf = {}
    for key, short in (('ffn_wg', 'wg'), ('ffn_wu', 'wu'), ('ffn_wd', 'wd')):
        for l in range(2):
            wbuf[short + str(l)] = cast_into_slot("cast_%s%d" % (short, l), given[key][0], where, l)
    wbuf['w_in'] = cast_into_slot("cast_w_in", w_in[0], where)
    wbuf['w_out'] = cast_into_slot("cast_w_out", w_out[0], where)

    loss_l, grad_x, dmod, small, reduced, first_ffn = local_step(
        dm, x[0], ctx[0], loss_target[0], modv, norm_g_f.reshape(3, 1, D), final_norm_g.reshape(1, D),
        q_norm_g, k_norm_g, conv_w_f, conv_b, lru_wa[0], ba_f.reshape(2, 1, D), lru_wx[0], bx_f.reshape(2, 1, D),
        lam_f.reshape(2, 1, D), wbuf, where)
    loss = lax.psum(loss_l[0, 0], ("x", "y", "c"))

    grads = {}
    gd = allgather8("ag_dmod", dmod.reshape(2, N_MOD * D))
    dM, g_bmod = dmod_pack(gd)
    dMc = lax.dynamic_slice(dM, (0, slot * Wm), (16, Wm))
    bmm = _pick(D, [512, 256, 128])
    grads['w_mod'] = fused_mm(
        "w_mod_dw", (D // bmm, Wm // bnm, 1),
        [(call16, pl.BlockSpec((16, bmm), lambda i, j, k: (0, i))), (dMc, pl.BlockSpec((16, bnm), lambda i, j, k: (0, j)))],
        [(0, 1, TN, 0)], [(bmm, bnm)], ident,
        [(_sds((D, Wm), F32), pl.BlockSpec((bmm, bnm), lambda i, j, k: (i, j)))], pre={0: _silu})[0][None]
    grads['b_mod'] = g_bmod

    def epi_cc(ids, accs, ex):
        v = ex[0][...]
        sg = _sig(v)
        return [accs[0] * (sg * (1.0 + v * (1.0 - sg)))]

    pcc = fused_mm(
        "c_ctx_partial", (1, D // bmm, Wm // bnm),
        [(dMc, pl.BlockSpec((16, bnm), lambda i, j, k: (0, k))), (w_mod[0], pl.BlockSpec((bmm, bnm), lambda i, j, k: (j, k)))],
        [(0, 1, NT, 0)], [(16, bmm)], epi_cc,
        [(_sds((16, D), F32), pl.BlockSpec((16, bmm), lambda i, j, k: (0, j)))],
        extras=[(c_ctx.reshape(1, D), pl.BlockSpec((1, bmm), lambda i, j, k: (0, j)))])[0]
    pcc_row = jnp.where(ci == 0, pcc[8], 0.0)

    order = ['q_norm_g', 'k_norm_g', 'conv_b', 'final_norm_g', 'norm_g', 'conv_w', 'lru_ba', 'lru_bx', 'lru_lambda']
    flat = [small[k].reshape(-1) for k in order] + [pcc_row]
    sizes = [f.shape[0] for f in flat]
    tot = sum(sizes)
    LW = 1024
    padded = -(-tot // (16 * LW)) * (16 * LW)
    tiny = jnp.concatenate(flat + [jnp.zeros((padded - tot,), F32)]).reshape(1, -1, LW)
    RS = tiny.shape[1]
    LBD = D // LB
    mats = [small['lru_wa'].reshape(1, 2 * LB * LBD, LBD), small['lru_wx'].reshape(1, 2 * LB * LBD, LBD)]
    tensors = first_ffn + mats + [tiny]
    tnames = ['wg0', 'wu0', 'wd0', 'lru_wa', 'lru_wx', 'tiny']
    gots = comm_call("rs_tail_p1", rs_p1_comm(tensors))
    pairs = [add_pair("rs_add_" + n_, g_, got_, where) for n_, g_, got_ in zip(tnames, tensors, gots)]
    landeds = comm_call("rs_tail_p2", rs_p2_comm([p_[0] for p_ in pairs], [p_[1] for p_ in pairs]))
    g_wg = sum_slots_into("rs_sum_wg0", landeds[0], where, reduced['ffn_wg'], (2, D, F4), 0)
    g_wu = sum_slots_into("rs_sum_wu0", landeds[1], where, reduced['ffn_wu'], (2, D, F4), 0)
    g_wd = sum_slots_into("rs_sum_wd0", landeds[2], where, reduced['ffn_wd'], (2, F4, D), 0)
    g_wa = sum_slots_into("rs_sum_lru_wa", landeds[3], where, None, (2 * LB * LBD, LBD), None)
    g_wx = sum_slots_into("rs_sum_lru_wx", landeds[4], where, None, (2 * LB * LBD, LBD), None)
    g_tiny = sum_slots_into("rs_sum_tiny", landeds[5], where, None, (RS, LW), None)
    g_wg, g_wu, g_wd, g_wa, g_wx, g_tiny = comm_call("rs_tail_p3", rs_p3_comm(
        [g_wg, g_wu, g_wd, g_wa, g_wx, g_tiny],
        [(0, 0, D // 2), (1, 0, D // 2), (2, 0, F4 // 2), (3, None, LB * LBD), (4, None, LB * LBD), (5, None, RS // 2)]))
    grads.update(ffn_wg=g_wg[None], ffn_wu=g_wu[None], ffn_wd=g_wd[None], w_in=reduced['w_in'][None],
                 w_out=reduced['w_out'][None], lru_wa=g_wa.reshape(lru_wa.shape), lru_wx=g_wx.reshape(lru_wx.shape))
    summed = g_tiny.reshape(-1)
    offs = {}
    o = 0
    for k, n_ in zip(order + ['c_ctx'], sizes):
        offs[k] = summed[o:o + n_]
        o += n_
    shard = lambda k, rows: lax.dynamic_slice_in_dim(offs[k].reshape(rows, D), slot * Ds, Ds, axis=1)
    grads['c_ctx'] = offs['c_ctx']
    grads['q_norm_g'] = offs['q_norm_g'].reshape(1, HEAD_DIM)
    grads['k_norm_g'] = offs['k_norm_g'].reshape(1, HEAD_DIM)
    grads['conv_b'] = offs['conv_b'].reshape(1, D)
    grads['final_norm_g'] = offs['final_norm_g']
    grads['norm_g'] = shard('norm_g', 3)[None]
    grads['conv_w'] = shard('conv_w', 4)[None]
    grads['lru_ba'] = shard('lru_ba', 2)[None]
    grads['lru_bx'] = shard('lru_bx', 2)[None]
    grads['lru_lambda'] = shard('lru_lambda', 2)[None]

    big_names = ['w_mod', 'ffn_wg', 'ffn_wu', 'ffn_wd', 'w_in', 'w_out', 'lru_wa', 'lru_wx']
    delta, new_m, new_v = {}, {}, {}
    for k in big_names:
        delta[k], new_m[k], new_v[k] = adamw("adamw_" + k, given[k], grads[k], given['m_' + k], given['v_' + k])
    tiny_names = [k for k in names if k not in big_names]
    res = adamw_many("adamw_tiny", [(given[k], grads[k], given['m_' + k], given['v_' + k]) for k in tiny_names])
    for k, (d_, m_, v_) in zip(tiny_names, res):
        delta[k], new_m[k], new_v[k] = d_, m_, v_

    return (loss, grad_x[None], *[grads[k] for k in names], *[delta[k] for k in names],
            *[new_m[k] for k in names], *[new_v[k] for k in names])
```

```python
import functools

import jax
import jax.numpy as jnp
from jax import lax
from jax.experimental import pallas as pl
from jax.experimental.pallas import tpu as pltpu

F32 = jnp.float32
BF = jnp.bfloat16
EPS = 1e-6
HEAD_DIM = 128
GRID_W = 64
ROPE_THETA = 10000.0
LRU_C = 8.0
FFN_RES = 0.5
N_MOD = 9
ADAM_LR, ADAM_B1, ADAM_B2, ADAM_EPS, ADAM_WD, ADAM_STEP = 0.001, 0.9, 0.999, 1e-08, 0.01, 10
VMEM_LIMIT = 52 * 1024 * 1024
MESH = pl.DeviceIdType.MESH
ANY = pl.BlockSpec(memory_space=pl.ANY)


def _sds(shape, dt):
    return jax.ShapeDtypeStruct(tuple(shape), dt)


def _pick(n, cands):
    for c in cands:
        if n % c == 0:
            return c
    return n


def _cparams(**kw):
    return pltpu.CompilerParams(vmem_limit_bytes=VMEM_LIMIT, **kw)


def _sig(x):
    return 1.0 / (1.0 + jnp.exp(-x))


def _gelu(x):
    t = jnp.tanh(0.7978845608028654 * (x + 0.044715 * x * x * x))
    return 0.5 * x * (1.0 + t), t


def _gelu_grad(x, t):
    return 0.5 * (1.0 + t) + 0.5 * x * (1.0 - t * t) * 0.7978845608028654 * (1.0 + 3.0 * 0.044715 * x * x)


class Comm:
    def __init__(self, reads, lands, n_sem, start, finish):
        self.reads, self.lands, self.n_sem, self.start, self.finish = list(reads), list(lands), n_sem, start, finish


def hosted_call(body, *, name, grid, in_specs, out_specs, out_shape, args, scratch_shapes=(), aliases=None, comm=None):
    aliases = dict(aliases or {})
    if comm is None:
        return pl.pallas_call(
            body, name=name, grid=grid, in_specs=list(in_specs), out_specs=list(out_specs), out_shape=list(out_shape),
            scratch_shapes=list(scratch_shapes), input_output_aliases=aliases, compiler_params=_cparams())(*args)
    n_in, n_out, n_sc = len(args), len(out_shape), len(scratch_shapes)
    land_in = [(t, l) for t, l in enumerate(comm.lands) if not isinstance(l, jax.ShapeDtypeStruct)]
    nr, nli, nl = len(comm.reads), len(land_in), len(comm.lands)

    def wrapped(*refs):
        a = refs[:n_in]
        r = refs[n_in:n_in + nr]
        pos = n_in + nr + nli
        o = refs[pos:pos + n_out]
        lo = refs[pos + n_out:pos + n_out + nl]
        sc = refs[pos + n_out + nl:pos + n_out + nl + n_sc]
        send_sems, recv_sems = refs[pos + n_out + nl + n_sc:]
        ids = [pl.program_id(d) for d in range(len(grid))]
        first, last = ids[0] == 0, ids[0] == grid[0] - 1
        for d in range(1, len(grid)):
            first = first & (ids[d] == 0)
            last = last & (ids[d] == grid[d] - 1)

        @pl.when(first)
        def _():
            comm.start(r, lo, send_sems, recv_sems)

        body(*a, *o, *sc)

        @pl.when(last)
        def _():
            comm.finish(r, lo, send_sems, recv_sems)

    for q, (t, _) in enumerate(land_in):
        aliases[n_in + nr + q] = n_out + t
    res = pl.pallas_call(
        wrapped, name=name, grid=grid,
        in_specs=list(in_specs) + [ANY] * (nr + nli), out_specs=list(out_specs) + [ANY] * nl,
        out_shape=list(out_shape) + [l if isinstance(l, jax.ShapeDtypeStruct) else _sds(l.shape, l.dtype) for l in comm.lands],
        scratch_shapes=list(scratch_shapes) + [pltpu.SemaphoreType.DMA((comm.n_sem,)), pltpu.SemaphoreType.DMA((comm.n_sem,))],
        input_output_aliases=aliases, compiler_params=_cparams(),
    )(*args, *comm.reads, *[l for _, l in land_in])
    return list(res[:n_out]), list(res[n_out:])


def comm_call(name, comm):
    def body():
        pass

    return hosted_call(body, name=name, grid=(1,), in_specs=[], out_specs=[], out_shape=[], args=[], comm=comm)[1]


def ew_call(name, grid, fn, ins, outs, first=None, aliases=None, comm=None):
    n_in = len(ins)

    def body(*refs):
        ids = tuple(pl.program_id(a) for a in range(len(grid)))
        vals = fn(ids, *refs[:n_in])
        for (_, _, acc), o_ref, v in zip(outs, refs[n_in:], vals):
            if not acc:
                o_ref[...] = v.astype(o_ref.dtype)
            else:
                is_first = first(ids)

                @pl.when(is_first)
                def _(o_ref=o_ref, v=v):
                    o_ref[...] = v.astype(o_ref.dtype)

                @pl.when(jnp.logical_not(is_first))
                def _(o_ref=o_ref, v=v):
                    o_ref[...] += v.astype(o_ref.dtype)

    return hosted_call(body, name=name, grid=grid, in_specs=[s for _, s in ins], out_specs=[s for _, s, _ in outs],
                       out_shape=[o for o, _, _ in outs], args=[a for a, _ in ins], aliases=aliases, comm=comm)


def fused_mm(name, grid, ins, prods, acc_shapes, epi, outs, extras=(), pre=None, comm=None):
    n_in, n_ex, n_out = len(ins), len(extras), len(outs)
    nk = grid[-1]
    pre = pre or {}
    n_acc = len(acc_shapes)

    def body(*refs):
        in_refs = refs[:n_in]
        ex_refs = refs[n_in:n_in + n_ex]
        out_refs = refs[n_in + n_ex:n_in + n_ex + n_out]
        accs = refs[n_in + n_ex + n_out:]
        ids = tuple(pl.program_id(a) for a in range(len(grid)))
        k = ids[-1]
        loaded = {}

        def operand(i):
            if i not in loaded:
                v = in_refs[i][...]
                if i in pre:
                    v = pre[i](v)
                loaded[i] = v.astype(BF)
            return loaded[i]

        def product(ia, ib, dims):
            return lax.dot_general(operand(ia), operand(ib), (dims, ((), ())), preferred_element_type=F32)

        if nk == 1:
            sums = [None] * n_acc
            for ia, ib, dims, ai in prods:
                d = product(ia, ib, dims)
                sums[ai] = d if sums[ai] is None else sums[ai] + d
            for o_ref, v in zip(out_refs, epi(ids, sums, ex_refs)):
                o_ref[...] = v.astype(o_ref.dtype)
            return

        @pl.when(k == 0)
        def _():
            for a in accs:
                a[...] = jnp.zeros(a.shape, F32)

        for ia, ib, dims, ai in prods:
            accs[ai][...] += product(ia, ib, dims)

        @pl.when(k == nk - 1)
        def _():
            vals = epi(ids, [a[...] for a in accs], ex_refs)
            for o_ref, v in zip(out_refs, vals):
                o_ref[...] = v.astype(o_ref.dtype)

    return hosted_call(
        body, name=name, grid=grid, in_specs=[s for _, s in ins] + [s for _, s in extras],
        out_specs=[s for _, s in outs], out_shape=[o for o, _ in outs],
        scratch_shapes=[pltpu.VMEM(s, F32) for s in acc_shapes] if nk > 1 else [],
        args=[a for a, _ in ins] + [a for a, _ in extras], comm=comm)


NN = ((1,), (0,))
NT = ((1,), (1,))
TN = ((0,), (0,))


class Dims:
    def __init__(self, S, C, D, F4, W4, NS, LB):
        self.S, self.C, self.D, self.F4, self.W4, self.NS, self.LB = S, C, D, F4, W4, NS, LB
        self.T = S + C
        self.DFF = F4 * NS
        self.INW = W4 * NS
        self.NQ = D // HEAD_DIM
        self.KVW = (self.INW - 5 * D) // 2
        self.NKV = self.KVW // HEAD_DIM
        self.G = self.NQ // self.NKV
        self.OFF_K = D
        self.OFF_V = D + self.KVW
        self.OFF_LX = D + 2 * self.KVW
        self.OFF_LG = self.OFF_LX + D
        self.OFF_GA = self.OFF_LG + D
        self.OFF_GL = self.OFF_GA + D
        self.bm = _pick(C, [256, 128, 64, 32, 16, 8])
        self.nCb = C // self.bm
        self.nTb = self.T // self.bm
        self.nSb = S // self.bm
        self.mT = _pick(self.T, [544, 512, 384, 256, 128])
        self.mS = _pick(S, [512, 256, 128])
        self.kT = _pick(self.T, [1088, 1024, 768, 544, 512, 384, 256, 128])
        self.kS = _pick(S, [1024, 512, 256, 128])
        self.cw = _pick(D, [1024, 512, 256, 128]) if (self.OFF_LX % 1024 == 0 and D % 1024 == 0) else _pick(
            self.OFF_LX, [512, 256, 128])
        self.nsub = 2 if (W4 % 256 == 0 and W4 >= 512) else 1
        self.wb = W4 // self.nsub
        self.LBD = D // LB
        self.bq = _pick(C, [256, 128]) if S % _pick(C, [256, 128]) == 0 else 128


def rope_tables(dm):
    rows = dm.S // GRID_W
    row = jnp.repeat(jnp.arange(rows, dtype=F32), GRID_W)
    col = jnp.tile(jnp.arange(GRID_W, dtype=F32), rows)
    axis_dims = HEAD_DIM // 2
    freqs = ROPE_THETA ** (-jnp.arange(0, axis_dims, 2, dtype=F32) / axis_dims)
    ang = jnp.concatenate([row[:, None] * freqs, col[:, None] * freqs], axis=-1)
    cos = jnp.repeat(jnp.cos(ang), 2, axis=-1)
    sin = jnp.repeat(jnp.sin(ang), 2, axis=-1)
    sign = jnp.tile(jnp.array([-1.0, 1.0], F32), HEAD_DIM // 2)
    sin = sin * sign
    cos = jnp.concatenate([jnp.ones((dm.C, HEAD_DIM), F32), cos], axis=0)
    sin = jnp.concatenate([jnp.zeros((dm.C, HEAD_DIM), F32), sin], axis=0)
    return cos, sin


def _pair_swap(y):
    lane = lax.broadcasted_iota(jnp.int32, y.shape, 1)
    nxt = pltpu.roll(y, y.shape[1] - 1, 1)
    prv = pltpu.roll(y, 1, 1)
    return jnp.where((lane & 1) == 0, nxt, prv)


def normmod_fwd(name, dm, x, norm_g3, stage, modv, rows_T):
    D, bm = dm.D, dm.bm
    nb = dm.nTb if rows_T else dm.nSb
    typ = (lambda i: jnp.where(i < dm.nCb, 0, 1)) if rows_T else (lambda i: 1)

    def fn(ids, x_ref, g_ref, sh_ref, sc_ref):
        xv = x_ref[...]
        r = lax.rsqrt(jnp.mean(xv * xv, axis=-1, keepdims=True) + EPS)
        n = xv * r * g_ref[...]
        return [n * (1.0 + sc_ref[...]) + sh_ref[...]]

    return ew_call(
        name, (nb,), fn,
        [(x, pl.BlockSpec((bm, D), lambda i: (i, 0))),
         (norm_g3, pl.BlockSpec((None, 1, D), lambda i: (stage, 0, 0))),
         (modv, pl.BlockSpec((None, None, 1, D), lambda i: (typ(i), 3 * stage, 0, 0))),
         (modv, pl.BlockSpec((None, None, 1, D), lambda i: (typ(i), 3 * stage + 1, 0, 0)))],
        [(_sds(x.shape, BF), pl.BlockSpec((bm, D), lambda i: (i, 0)), False)])[0]


def normmod_concat_fwd(name, dm, ctx, x, norm_g3, modv):
    D, bm, nCb = dm.D, dm.bm, dm.nCb
    typ = lambda i: jnp.where(i < nCb, 0, 1)

    def fn(ids, c_ref, x_ref, g_ref, sh_ref, sc_ref):
        xv = jnp.where(ids[0] < nCb, c_ref[...], x_ref[...])
        r = lax.rsqrt(jnp.mean(xv * xv, axis=-1, keepdims=True) + EPS)
        n = xv * r * g_ref[...]
        return [n * (1.0 + sc_ref[...]) + sh_ref[...], xv]

    row = pl.BlockSpec((bm, D), lambda i: (i, 0))
    return ew_call(
        name, (dm.nTb,), fn,
        [(ctx, pl.BlockSpec((bm, D), lambda i: (jnp.minimum(i, nCb - 1), 0))),
         (x, pl.BlockSpec((bm, D), lambda i: (jnp.maximum(i - nCb, 0), 0))),
         (norm_g3, pl.BlockSpec((None, 1, D), lambda i: (0, 0, 0))),
         (modv, pl.BlockSpec((None, None, 1, D), lambda i: (typ(i), 0, 0, 0))),
         (modv, pl.BlockSpec((None, None, 1, D), lambda i: (typ(i), 1, 0, 0)))],
        [(_sds((dm.T, D), BF), row, False), (_sds((dm.T, D), F32), row, False)])


def normmod_bwd(name, dm, dh, x, dres, norm_g3, stage, modv, rows_T, dres_lat_only, out_lat_only=False):
    D, bm = dm.D, dm.bm
    nb = dm.nTb if rows_T else dm.nSb
    nCb = dm.nCb
    typ = (lambda i: jnp.where(i < nCb, 0, 1)) if rows_T else (lambda i: 1)
    if dres_lat_only:
        dres_map = lambda i: (jnp.maximum(i - nCb, 0), 0)
    else:
        dres_map = lambda i: (i, 0)

    def fn(ids, dh_ref, x_ref, dres_ref, g_ref, sc_ref):
        i = ids[0]
        xv = x_ref[...]
        dhv = dh_ref[...].astype(F32)
        r = lax.rsqrt(jnp.mean(xv * xv, axis=-1, keepdims=True) + EPS)
        xn = xv * r
        g = g_ref[...]
        n = xn * g
        dn = dhv * (1.0 + sc_ref[...])
        dxn = dn * g
        dx = r * (dxn - xn * jnp.mean(dxn * xn, axis=-1, keepdims=True))
        dresv = dres_ref[...]
        if dres_lat_only:
            dresv = jnp.where(i >= nCb, dresv, 0.0)
        dsh = jnp.sum(dhv, axis=0, keepdims=True)
        dsc = jnp.sum(dhv * n, axis=0, keepdims=True)
        dg = jnp.sum(dn * xn, axis=0, keepdims=True)
        return [dx + dresv, dsh, dsc, dg]

    if rows_T:
        first = lambda ids: (ids[0] == 0) | (ids[0] == nCb)
    else:
        first = lambda ids: ids[0] == 0
    acc = (_sds((2, 1, D), F32), pl.BlockSpec((None, 1, D), lambda i: (typ(i), 0, 0)), True)
    return ew_call(
        name, (nb,), fn,
        [(dh, pl.BlockSpec((bm, D), lambda i: (i, 0))),
         (x, pl.BlockSpec((bm, D), lambda i: (i, 0))),
         (dres, pl.BlockSpec((bm, D), dres_map)),
         (norm_g3, pl.BlockSpec((None, 1, D), lambda i: (stage, 0, 0))),
         (modv, pl.BlockSpec((None, None, 1, D), lambda i: (typ(i), 3 * stage + 1, 0, 0)))],
        [(_sds((dm.S, D) if out_lat_only else x.shape, F32),
          pl.BlockSpec((bm, D), (lambda i: (jnp.maximum(i - nCb, 0), 0)) if out_lat_only else (lambda i: (i, 0))), False),
         acc, acc, acc], first=first)


def gate_bwd(name, dm, dx, f, modv, gidx, scale, rows_T):
    D, bm = dm.D, dm.bm
    nb = dm.nTb if rows_T else dm.nSb
    nCb = dm.nCb
    typ = (lambda i: jnp.where(i < nCb, 0, 1)) if rows_T else (lambda i: 1)

    def fn(ids, dx_ref, f_ref, g_ref):
        dxv = dx_ref[...]
        return [scale * g_ref[...] * dxv, jnp.sum(scale * f_ref[...].astype(F32) * dxv, axis=0, keepdims=True)]

    if rows_T:
        first = lambda ids: (ids[0] == 0) | (ids[0] == nCb)
    else:
        first = lambda ids: ids[0] == 0
    return ew_call(
        name, (nb,), fn,
        [(dx, pl.BlockSpec((bm, D), lambda i: (i, 0))),
         (f, pl.BlockSpec((bm, D), lambda i: (i, 0))),
         (modv, pl.BlockSpec((None, None, 1, D), lambda i: (typ(i), gidx, 0, 0)))],
        [(_sds(dx.shape, BF), pl.BlockSpec((bm, D), lambda i: (i, 0)), False),
         (_sds((2, 1, D), F32), pl.BlockSpec((None, 1, D), lambda i: (typ(i), 0, 0)), True)], first=first)


def ffn_fwd(name, dm, h, xres, wg, wu, wd, modv, gidx, rows_T, comm_up=None, comm_down=None):
    D, F4, NS = dm.D, dm.F4, dm.NS
    M = h.shape[0]
    bm = dm.mT if rows_T else dm.mS
    C = dm.C

    def epi_up(ids, accs, ex):
        a, u = accs
        return [a, u, a * _sig(a) * u]

    hspec = pl.BlockSpec((bm, D), lambda j, i, k: (i, 0))
    wspec = pl.BlockSpec((None, D, F4), lambda j, i, k: (j, 0, 0))
    ospec = pl.BlockSpec((bm, F4), lambda j, i, k: (i, j))
    res = fused_mm(
        name + "_up", (NS, M // bm, 1), [(h, hspec), (wg, wspec), (wu, wspec)],
        [(0, 1, NN, 0), (0, 2, NN, 1)], [(bm, F4), (bm, F4)], epi_up,
        [(_sds((M, dm.DFF), BF), ospec)] * 3, comm=comm_up)
    (a, u, s), land_up = res if comm_up is not None else (res, None)

    bn = _pick(D, [1024, 512, 256, 128])

    def epi_dn(ids, accs, ex):
        f = accs[0]
        if rows_T:
            row = ids[0] * bm + lax.broadcasted_iota(jnp.int32, (bm, 1), 0)
            gate = jnp.where(row < C, ex[1][...], ex[2][...])
        else:
            gate = ex[2][...]
        return [ex[0][...] + FFN_RES * gate * f, f]

    gspec = lambda t: pl.BlockSpec((None, None, 1, bn), lambda i, j, k: (t, gidx, 0, j))
    res = fused_mm(
        name + "_down", (M // bm, D // bn, NS),
        [(s, pl.BlockSpec((bm, F4), lambda i, j, k: (i, k))),
         (wd, pl.BlockSpec((None, F4, bn), lambda i, j, k: (k, 0, j)))],
        [(0, 1, NN, 0)], [(bm, bn)], epi_dn,
        [(_sds((M, D), F32), pl.BlockSpec((bm, bn), lambda i, j, k: (i, j))),
         (_sds((M, D), BF), pl.BlockSpec((bm, bn), lambda i, j, k: (i, j)))],
        extras=[(xres, pl.BlockSpec((bm, bn), lambda i, j, k: (i, j))), (modv, gspec(0)), (modv, gspec(1))],
        comm=comm_down)
    (xo, f), land_down = res if comm_down is not None else (res, None)
    return xo, a, u, s, f, land_up, land_down


def ffn_bwd(name, dm, df, h, a, u, s, wg, wu, wd, rows_T, comms=None):
    comms = comms or {}
    landed = {}

    def run(key, *args, **kw):
        comm = comms[key](landed) if key in comms else None
        res = fused_mm(*args, comm=comm, **kw)
        if comm is not None:
            res, landed[key] = res
        return res

    D, F4, NS = dm.D, dm.F4, dm.NS
    M = h.shape[0]
    bm = dm.mT if rows_T else dm.mS
    bkr = dm.kT if rows_T else dm.kS

    def epi_ds(ids, accs, ex):
        ds = accs[0]
        av = ex[0][...].astype(F32)
        uv = ex[1][...].astype(F32)
        sg = _sig(av)
        return [ds * uv * (sg * (1.0 + av * (1.0 - sg))), ds * av * sg]

    ospec = pl.BlockSpec((bm, F4), lambda j, i, k: (i, j))
    da, du = run(
        'ds', name + "_ds", (NS, M // bm, 1),
        [(df, pl.BlockSpec((bm, D), lambda j, i, k: (i, 0))),
         (wd, pl.BlockSpec((None, F4, D), lambda j, i, k: (j, 0, 0)))],
        [(0, 1, NT, 0)], [(bm, F4)], epi_ds, [(_sds((M, dm.DFF), BF), ospec)] * 2,
        extras=[(a, ospec), (u, ospec)])

    ident = lambda ids, accs, ex: list(accs)
    bn = _pick(D, [1024, 512, 256, 128])
    dwd = run(
        'dwd', name + "_dwd", (NS, D // bn, M // bkr),
        [(s, pl.BlockSpec((bkr, F4), lambda i, j, k: (k, i))),
         (df, pl.BlockSpec((bkr, bn), lambda i, j, k: (k, j)))],
        [(0, 1, TN, 0)], [(F4, bn)], ident,
        [(_sds((NS, F4, D), BF), pl.BlockSpec((None, F4, bn), lambda i, j, k: (i, 0, j)))])[0]

    dwg, dwu = run(
        'dwgu', name + "_dwgu", (D // bn, NS, M // bkr),
        [(h, pl.BlockSpec((bkr, bn), lambda i, j, k: (k, i))),
         (da, pl.BlockSpec((bkr, F4), lambda i, j, k: (k, j))),
         (du, pl.BlockSpec((bkr, F4), lambda i, j, k: (k, j)))],
        [(0, 1, TN, 0), (0, 2, TN, 1)], [(bn, F4), (bn, F4)], ident,
        [(_sds((NS, D, F4), BF), pl.BlockSpec((None, bn, F4), lambda i, j, k: (j, i, 0)))] * 2)

    dh = run(
        'dh', name + "_dh", (M // bm, D // bn, NS),
        [(da, pl.BlockSpec((bm, F4), lambda i, j, k: (i, k))),
         (wg, pl.BlockSpec((None, bn, F4), lambda i, j, k: (k, j, 0))),
         (du, pl.BlockSpec((bm, F4), lambda i, j, k: (i, k))),
         (wu, pl.BlockSpec((None, bn, F4), lambda i, j, k: (k, j, 0)))],
        [(0, 1, NT, 0), (2, 3, NT, 0)], [(bm, bn)], ident,
        [(_sds((M, D), F32), pl.BlockSpec((bm, bn), lambda i, j, k: (i, j)))])[0]
    return dh, dwg, dwu, dwd, landed


def qk_prep(dm, P, gq, gk, cosf, sinf):
    D, KVW, bm = dm.D, dm.KVW, dm.bm

    def head_norm_rope(xh, g, c, s):
        r = lax.rsqrt(jnp.mean(xh * xh, axis=-1, keepdims=True) + EPS)
        y = xh * r * g
        return y * c + _pair_swap(y) * s

    def fn(ids, q_ref, k_ref, v_ref, gq_ref, gk_ref, c_ref, s_ref):
        c, s = c_ref[...], s_ref[...]
        qs = [head_norm_rope(q_ref[:, h * HEAD_DIM:(h + 1) * HEAD_DIM], gq_ref[...], c, s) for h in range(dm.NQ)]
        ks = [head_norm_rope(k_ref[:, h * HEAD_DIM:(h + 1) * HEAD_DIM], gk_ref[...], c, s) for h in range(dm.NKV)]
        return [jnp.concatenate(qs, axis=1), jnp.concatenate(ks, axis=1), v_ref[...]]

    hspec = pl.BlockSpec((bm, HEAD_DIM), lambda i: (i, 0))
    vec = pl.BlockSpec((1, HEAD_DIM), lambda i: (0, 0))
    return ew_call(
        "qk_prep", (dm.nTb,), fn,
        [(P, pl.BlockSpec((bm, D), lambda i: (i, 0))),
         (P, pl.BlockSpec((bm, KVW), lambda i: (i, dm.OFF_K // KVW))),
         (P, pl.BlockSpec((bm, KVW), lambda i: (i, dm.OFF_V // KVW))),
         (gq, vec), (gk, vec), (cosf, hspec), (sinf, hspec)],
        [(_sds((dm.T, D), BF), pl.BlockSpec((bm, D), lambda i: (i, 0)), False),
         (_sds((dm.T, KVW), BF), pl.BlockSpec((bm, KVW), lambda i: (i, 0)), False),
         (_sds((dm.T, KVW), BF), pl.BlockSpec((bm, KVW), lambda i: (i, 0)), False)])


def qk_prep_bwd(dm, dq, dk, dv, P, gq, gk, cosf, sinf, dP):
    D, KVW, bm, nCb = dm.D, dm.KVW, dm.bm, dm.nCb
    W = D + 2 * KVW

    def head_bwd(d, xh, g, c, s):
        dy = d * c - _pair_swap(d) * s
        r = lax.rsqrt(jnp.mean(xh * xh, axis=-1, keepdims=True) + EPS)
        xn = xh * r
        dg = jnp.sum(dy * xn, axis=0, keepdims=True)
        dxn = dy * g
        return r * (dxn - xn * jnp.mean(dxn * xn, axis=-1, keepdims=True)), dg

    def fn(ids, dq_ref, dk_ref, dv_ref, q_ref, k_ref, gq_ref, gk_ref, c_ref, s_ref, dp_any):
        i = ids[0]
        c, s = c_ref[...], s_ref[...]
        lat = i >= nCb
        outs, dgq = [], jnp.zeros((1, HEAD_DIM), F32)
        for h in range(dm.NQ):
            sl = slice(h * HEAD_DIM, (h + 1) * HEAD_DIM)
            d = jnp.where(lat, dq_ref[:, sl], 0.0)
            dx, dg = head_bwd(d, q_ref[:, sl], gq_ref[...], c, s)
            outs.append(dx)
            dgq = dgq + dg
        dgk = jnp.zeros((1, HEAD_DIM), F32)
        for h in range(dm.NKV):
            sl = slice(h * HEAD_DIM, (h + 1) * HEAD_DIM)
            dx, dg = head_bwd(dk_ref[:, sl], k_ref[:, sl], gk_ref[...], c, s)
            outs.append(dx)
            dgk = dgk + dg
        outs.append(dv_ref[...])
        return [jnp.concatenate(outs, axis=1), dgq, dgk]

    hspec = pl.BlockSpec((bm, HEAD_DIM), lambda i: (i, 0))
    vec = pl.BlockSpec((1, HEAD_DIM), lambda i: (0, 0))
    return ew_call(
        "qk_prep_bwd", (dm.nTb,), fn,
        [(dq, pl.BlockSpec((bm, D), lambda i: (jnp.maximum(i - nCb, 0), 0))),
         (dk, pl.BlockSpec((bm, KVW), lambda i: (i, 0))),
         (dv, pl.BlockSpec((bm, KVW), lambda i: (i, 0))),
         (P, pl.BlockSpec((bm, D), lambda i: (i, 0))),
         (P, pl.BlockSpec((bm, KVW), lambda i: (i, dm.OFF_K // KVW))),
         (gq, vec), (gk, vec), (cosf, hspec), (sinf, hspec), (dP, ANY)],
        [(_sds(dP.shape, BF), pl.BlockSpec((bm, W), lambda i: (i, 0)), False),
         (_sds((1, HEAD_DIM), F32), vec, True), (_sds((1, HEAD_DIM), F32), vec, True)],
        first=lambda ids: ids[0] == 0, aliases={9: 0})


def _softmax_numerators(s_ref, eb_ref, mb_ref, scale):
    rows, T = s_ref.shape
    m = jnp.max(s_ref[...], axis=-1, keepdims=True)
    mb_ref[...] = jnp.broadcast_to(m, (rows, HEAD_DIM))
    lacc = jnp.zeros((rows, HEAD_DIM), F32)
    for c in range(T // HEAD_DIM):
        cs = slice(c * HEAD_DIM, (c + 1) * HEAD_DIM)
        e = jnp.exp((s_ref[:, cs] - mb_ref[...]) * scale)
        lacc = lacc + e
        eb_ref[:, cs] = e.astype(BF)
    return jnp.sum(lacc, axis=-1, keepdims=True)


def attention_fwd(dm, qr, kr, vb, comm=None):
    S, T, D, G, nCb = dm.S, dm.T, dm.D, dm.G, dm.nCb
    bq = dm.bq
    off = dm.C // bq
    scale = HEAD_DIM ** -0.5
    GW = G * HEAD_DIM

    def body(q_ref, k_ref, v_ref, o_ref, s_ref, eb_ref, mb_ref):
        k = k_ref[...]
        v = v_ref[...]
        for h in range(G):
            sl = slice(h * HEAD_DIM, (h + 1) * HEAD_DIM)
            s_ref[...] = lax.dot_general(q_ref[:, sl], k, (NT, ((), ())), preferred_element_type=F32)
            l = _softmax_numerators(s_ref, eb_ref, mb_ref, scale)
            o = lax.dot_general(eb_ref[...], v, (NN, ((), ())), preferred_element_type=F32)
            o_ref[:, sl] = o / l

    return hosted_call(
        body, grid=(dm.NKV, S // bq), name="attn_fwd",
        in_specs=[pl.BlockSpec((bq, GW), lambda g, i: (i + off, g)),
                  pl.BlockSpec((T, HEAD_DIM), lambda g, i: (0, g)),
                  pl.BlockSpec((T, HEAD_DIM), lambda g, i: (0, g))],
        out_specs=[pl.BlockSpec((bq, GW), lambda g, i: (i, g))],
        out_shape=[_sds((S, D), F32)], args=[qr, kr, vb],
        scratch_shapes=[pltpu.VMEM((bq, T), F32), pltpu.VMEM((bq, T), BF), pltpu.VMEM((bq, HEAD_DIM), F32)], comm=comm)


def attention_bwd(dm, qr, kr, vb, attn, dattn, comm=None):
    S, T, D, G = dm.S, dm.T, dm.D, dm.G
    bq = dm.bq
    off = dm.C // bq
    scale = HEAD_DIM ** -0.5
    GW = G * HEAD_DIM

    def body(q_ref, k_ref, v_ref, o_ref, do_ref, dq_ref, dk_ref, dv_ref, s_ref, eb_ref, mb_ref):
        i = pl.program_id(1)

        @pl.when(i == 0)
        def _():
            dk_ref[...] = jnp.zeros(dk_ref.shape, F32)
            dv_ref[...] = jnp.zeros(dv_ref.shape, F32)

        k = k_ref[...]
        v = v_ref[...]
        for h in range(G):
            sl = slice(h * HEAD_DIM, (h + 1) * HEAD_DIM)
            q = q_ref[:, sl]
            do = do_ref[:, sl]
            dof = do.astype(F32)
            s_ref[...] = lax.dot_general(q, k, (NT, ((), ())), preferred_element_type=F32)
            l = _softmax_numerators(s_ref, eb_ref, mb_ref, scale)
            rl = 1.0 / l
            dv_ref[...] += lax.dot_general(eb_ref[...], (dof * rl).astype(BF), (TN, ((), ())), preferred_element_type=F32)
            s_ref[...] = lax.dot_general(do, v, (NT, ((), ())), preferred_element_type=F32)
            delta = jnp.sum(dof * o_ref[:, sl], axis=-1, keepdims=True)
            mb_ref[...] = jnp.broadcast_to(delta, (bq, HEAD_DIM))
            for c in range(T // HEAD_DIM):
                cs = slice(c * HEAD_DIM, (c + 1) * HEAD_DIM)
                eb_ref[:, cs] = (eb_ref[:, cs].astype(F32) * (s_ref[:, cs] - mb_ref[...])).astype(BF)
            w = scale * rl
            dq_ref[:, sl] = lax.dot_general(eb_ref[...], k, (NN, ((), ())), preferred_element_type=F32) * w
            dk_ref[...] += lax.dot_general(eb_ref[...], (q.astype(F32) * w).astype(BF), (TN, ((), ())),
                                           preferred_element_type=F32)

    return hosted_call(
        body, grid=(dm.NKV, S // bq), name="attn_bwd",
        in_specs=[pl.BlockSpec((bq, GW), lambda g, i: (i + off, g)),
                  pl.BlockSpec((T, HEAD_DIM), lambda g, i: (0, g)),
                  pl.BlockSpec((T, HEAD_DIM), lambda g, i: (0, g)),
                  pl.BlockSpec((bq, GW), lambda g, i: (i, g)),
                  pl.BlockSpec((bq, GW), lambda g, i: (i + off, g))],
        out_specs=[pl.BlockSpec((bq, GW), lambda g, i: (i, g)),
                   pl.BlockSpec((T, HEAD_DIM), lambda g, i: (0, g)),
                   pl.BlockSpec((T, HEAD_DIM), lambda g, i: (0, g))],
        out_shape=[_sds((S, D), F32), _sds((T, dm.KVW), F32), _sds((T, dm.KVW), F32)],
        args=[qr, kr, vb, attn, dattn],
        scratch_shapes=[pltpu.VMEM((bq, T), F32), pltpu.VMEM((bq, T), BF), pltpu.VMEM((bq, HEAD_DIM), F32)], comm=comm)


def _conv_taps(dm, lx, masks_only=False):
    T, C = dm.T, dm.C
    t = lax.broadcasted_iota(jnp.int32, (T, 1), 0)
    valid = [(t >= 2) & ((t < C) | (t >= C + 2)), (t >= 1) & ((t < C) | (t >= C + 1)), None,
             (t != C - 1) & (t != T - 1)]
    shifts = [2, 1, 0, T - 1]
    taps = []
    for k in range(4):
        if k == 2:
            taps.append(lx)
        else:
            taps.append(jnp.where(valid[k], pltpu.roll(lx, shifts[k], 0), 0.0))
    return taps


def _scan_tiles(dm, chains):
    T, C = dm.T, dm.C
    nT, nC = T // 8, C // 8
    row = lax.broadcasted_iota(jnp.int32, (8, HEAD_DIM), 0)

    def tile_of(i, asc, split):
        if not split:
            return i if asc else nT - 1 - i
        if asc:
            return jnp.where(i < nT - nC, nC + i, i - (nT - nC))
        return jnp.where(i < nC, nC - 1 - i, nT - 1 - (i - nC))

    def step(i, carry, asc, split, a_ref, u_ref, out_ref, mode):
        off = pl.multiple_of(tile_of(i, asc, split) * 8, 8)
        a = a_ref[pl.ds(off, 8), :]
        b = u_ref[pl.ds(off, 8), :]
        if mode == 'lam':
            if asc:
                coef = jnp.where(row == 0, 1.0, pltpu.roll(a, 1, 0))
            else:
                coef = jnp.where(row == 7, 1.0, pltpu.roll(a, 7, 0))
        else:
            coef = a
        A, B = coef, b
        for d in (1, 2, 4):
            if asc:
                ok = row >= d
                A_sh = jnp.where(ok, pltpu.roll(A, d, 0), 1.0)
                B_sh = jnp.where(ok, pltpu.roll(B, d, 0), 0.0)
            else:
                ok = row < 8 - d
                A_sh = jnp.where(ok, pltpu.roll(A, 8 - d, 0), 1.0)
                B_sh = jnp.where(ok, pltpu.roll(B, 8 - d, 0), 0.0)
            B = B + A * B_sh
            A = A * A_sh
        h = A * carry + B
        out_ref[pl.ds(off, 8), :] = h
        last = h[7:8, :] if asc else h[0:1, :]
        if mode == 'lam':
            last = last * (a[7:8, :] if asc else a[0:1, :])
        return jnp.broadcast_to(last, (8, HEAD_DIM))

    def body(i, carries):
        return tuple(step(i, c_, *ch) for c_, ch in zip(carries, chains))

    lax.fori_loop(0, nT, body, tuple(jnp.zeros((8, HEAD_DIM), F32) for _ in chains))


def _lru_gates(xc, wa, ba, wx, bx, sp):
    xb = xc.astype(BF)
    r = _sig(jnp.dot(xb, wa, preferred_element_type=F32) + ba)
    i = _sig(jnp.dot(xb, wx, preferred_element_type=F32) + bx)
    a = jnp.exp(-LRU_C * r * sp)
    m = jnp.sqrt(1.0 - a * a)
    return r, i, a, m


def lru_fwd(dm, P, conv_w, conv_b, wa, ba, wx, bx, sp):
    T, D, LB = dm.T, dm.D, dm.LB
    W = dm.LBD
    R = _pick(T, [272, 256, 128, 64, 8])
    lxb = dm.OFF_LX // W

    def body(lx_ref, cw_ref, cb_ref, wa_ref, ba_ref, wx_ref, bx_ref, sp_ref, hf_ref, hb_ref, xc_ref, a_ref):
        taps = _conv_taps(dm, lx_ref[...])
        xc = cb_ref[...]
        for k in range(4):
            xc = xc + taps[k] * cw_ref[k:k + 1, :]
        xc_ref[...] = xc
        h_refs = (hf_ref, hb_ref)

        def chunk(ci, _):
            off = pl.multiple_of(ci * R, 8)
            x = xc_ref[pl.ds(off, R), :]
            for d in range(2):
                r, i, a, m = _lru_gates(x, wa_ref[d].astype(BF), ba_ref[d], wx_ref[d].astype(BF), bx_ref[d], sp_ref[d])
                a_ref[d, pl.ds(off, R), :] = a
                h_refs[d][pl.ds(off, R), :] = m * i * x
            return 0

        lax.fori_loop(0, T // R, chunk, 0)
        _scan_tiles(dm, [(True, False, a_ref.at[0], hf_ref, hf_ref, 'h'), (False, True, a_ref.at[1], hb_ref, hb_ref, 'h')])

    strip = lambda j: (0, j)
    vec = pl.BlockSpec((2, 1, W), lambda j: (0, 0, j))
    mat = pl.BlockSpec((2, None, W, W), lambda j: (0, j, 0, 0))
    return pl.pallas_call(
        body, grid=(LB,), name="lru_fwd",
        in_specs=[pl.BlockSpec((T, W), lambda j: (0, lxb + j)),
                  pl.BlockSpec((4, W), strip), pl.BlockSpec((1, W), strip), mat, vec, mat, vec, vec],
        out_specs=[pl.BlockSpec((T, W), strip)] * 2, out_shape=[_sds((T, D), F32)] * 2,
        scratch_shapes=[pltpu.VMEM((T, W), F32), pltpu.VMEM((2, T, W), F32)], compiler_params=_cparams(),
    )(P, conv_w, conv_b, wa, ba, wx, bx, sp)


def lru_bwd(dm, P, dh, hf, hb, conv_w, conv_b, wa, ba, wx, bx, sp, sg, dP):
    T, C, D, LB = dm.T, dm.C, dm.D, dm.LB
    W = dm.LBD
    R = _pick(T, [272, 256, 128, 64, 8])
    lxb = dm.OFF_LX // W

    def body(lx_ref, dh_ref, hf_ref, hb_ref, cw_ref, cb_ref, wa_ref, ba_ref, wx_ref, bx_ref, sp_ref, sg_ref, _dp_any,
             dlx_ref, dwa_ref, dba_ref, dwx_ref, dbx_ref, dlam_ref, dcw_ref, dcb_ref,
             xc_ref, a_ref, lam_ref, hp_ref, dxc_ref):
        lx = lx_ref[...]
        taps = _conv_taps(dm, lx)
        xc = cb_ref[...]
        for k in range(4):
            xc = xc + taps[k] * cw_ref[k:k + 1, :]
        xc_ref[...] = xc

        def gates(d, x):
            return _lru_gates(x, wa_ref[d].astype(BF), ba_ref[d], wx_ref[d].astype(BF), bx_ref[d], sp_ref[d])

        def chunk_a(ci, _):
            off = pl.multiple_of(ci * R, 8)
            x = xc_ref[pl.ds(off, R), :]
            for d in range(2):
                a_ref[d, pl.ds(off, R), :] = gates(d, x)[2]
            return 0

        lax.fori_loop(0, T // R, chunk_a, 0)
        _scan_tiles(dm, [(False, False, a_ref.at[0], dh_ref, lam_ref.at[0], 'lam'),
                         (True, True, a_ref.at[1], dh_ref, lam_ref.at[1], 'lam')])
        t = lax.broadcasted_iota(jnp.int32, (T, 1), 0)
        hp_ref[0] = jnp.where(t == 0, 0.0, pltpu.roll(hf_ref[...], 1, 0))
        hv = hb_ref[...]
        hp_ref[1] = jnp.where(t == C - 1, 0.0, jnp.where(t == T - 1, jnp.broadcast_to(hv[0:1, :], hv.shape),
                                                         pltpu.roll(hv, T - 1, 0)))

        def chunk_b(d):
            wa_, wx_ = wa_ref[d].astype(BF), wx_ref[d].astype(BF)

            def run(ci, carry):
                dwa, dwx, dba, dbx, dlam = carry
                off = pl.multiple_of(ci * R, 8)
                x = xc_ref[pl.ds(off, R), :]
                r, i, a, m = gates(d, x)
                lam = lam_ref[d, pl.ds(off, R), :]
                da = lam * hp_ref[d, pl.ds(off, R), :] - lam * (i * x) * a / m
                dloga = da * a
                dza = dloga * (-LRU_C) * sp_ref[d] * r * (1.0 - r)
                dzx = lam * m * x * i * (1.0 - i)
                dzab, dzxb = dza.astype(BF), dzx.astype(BF)
                xb = x.astype(BF)
                dxc = lam * m * i
                dxc = dxc + lax.dot_general(dzab, wa_, (NT, ((), ())), preferred_element_type=F32)
                dxc = dxc + lax.dot_general(dzxb, wx_, (NT, ((), ())), preferred_element_type=F32)
                if d == 0:
                    dxc_ref[pl.ds(off, R), :] = dxc
                else:
                    dxc_ref[pl.ds(off, R), :] += dxc
                dwa = dwa + lax.dot_general(xb, dzab, (TN, ((), ())), preferred_element_type=F32)
                dwx = dwx + lax.dot_general(xb, dzxb, (TN, ((), ())), preferred_element_type=F32)
                dba = dba + jnp.sum(dza, axis=0, keepdims=True)
                dbx = dbx + jnp.sum(dzx, axis=0, keepdims=True)
                dlam = dlam + jnp.sum(dloga * LRU_C * r, axis=0, keepdims=True)
                return dwa, dwx, dba, dbx, dlam

            z = jnp.zeros((W, W), F32)
            zv = jnp.zeros((1, W), F32)
            dwa, dwx, dba, dbx, dlam = lax.fori_loop(0, T // R, run, (z, z, zv, zv, zv))
            dwa_ref[d] = dwa
            dwx_ref[d] = dwx
            dba_ref[d] = dba
            dbx_ref[d] = dbx
            dlam_ref[d] = dlam * sg_ref[d]

        chunk_b(0)
        chunk_b(1)
        dxc = dxc_ref[...]
        dcb_ref[...] = jnp.sum(dxc, axis=0, keepdims=True)
        dcw_ref[...] = jnp.concatenate([jnp.sum(dxc * taps[k], axis=0, keepdims=True) for k in range(4)], axis=0)
        valid = [(t < T - 2) & ((t >= C) | (t < C - 2)), (t < T - 1) & ((t >= C) | (t < C - 1)), None,
                 (t != 0) & (t != C)]
        shifts = [T - 2, T - 1, 0, 1]
        dlx = dxc * cw_ref[2:3, :]
        for k in (0, 1, 3):
            dlx = dlx + jnp.where(valid[k], pltpu.roll(dxc, shifts[k], 0), 0.0) * cw_ref[k:k + 1, :]
        dlx_ref[...] = dlx.astype(dlx_ref.dtype)

    strip = lambda j: (0, j)
    sspec = pl.BlockSpec((T, W), strip)
    vec = pl.BlockSpec((2, 1, W), lambda j: (0, 0, j))
    mat = pl.BlockSpec((2, None, W, W), lambda j: (0, j, 0, 0))
    ovec = pl.BlockSpec((2, 1, W), lambda j: (0, 0, j))
    return pl.pallas_call(
        body, grid=(LB,), name="lru_bwd",
        in_specs=[pl.BlockSpec((T, W), lambda j: (0, lxb + j)), sspec, sspec, sspec,
                  pl.BlockSpec((4, W), strip), pl.BlockSpec((1, W), strip), mat, vec, mat, vec, vec, vec, ANY],
        out_specs=[pl.BlockSpec((T, W), lambda j: (0, lxb + j)), mat, ovec, mat, ovec, ovec,
                   pl.BlockSpec((4, W), strip), pl.BlockSpec((1, W), strip)],
        out_shape=[_sds(dP.shape, BF), _sds((2, LB, W, W), F32), _sds((2, 1, D), F32), _sds((2, LB, W, W), F32),
                   _sds((2, 1, D), F32), _sds((2, 1, D), F32), _sds((4, D), F32), _sds((1, D), F32)],
        scratch_shapes=[pltpu.VMEM((T, W), F32), pltpu.VMEM((2, T, W), F32), pltpu.VMEM((2, T, W), F32),
                        pltpu.VMEM((2, T, W), F32), pltpu.VMEM((T, W), F32)],
        input_output_aliases={12: 0}, compiler_params=_cparams(),
    )(P, dh, hf, hb, conv_w, conv_b, wa, ba, wx, bx, sp, sg, dP)


def merge_fwd(dm, P, attn, hf, hb):
    S, D, bm, cw, nCb = dm.S, dm.D, dm.bm, dm.cw, dm.nCb

    def fn(ids, lg_ref, ga_ref, gl_ref, at_ref, hf_ref, hb_ref):
        ge, _ = _gelu(lg_ref[...])
        lru = (hf_ref[...] + hb_ref[...]) * ge
        return [_sig(ga_ref[...]) * at_ref[...] + _sig(gl_ref[...]) * lru]

    pspec = lambda off: pl.BlockSpec((bm, cw), lambda i, j: (i + nCb, off // cw + j))
    tspec = pl.BlockSpec((bm, cw), lambda i, j: (i + nCb, j))
    sspec = pl.BlockSpec((bm, cw), lambda i, j: (i, j))
    return ew_call(
        "merge_fwd", (dm.nSb, D // cw), fn,
        [(P, pspec(dm.OFF_LG)), (P, pspec(dm.OFF_GA)), (P, pspec(dm.OFF_GL)), (attn, sspec), (hf, tspec), (hb, tspec)],
        [(_sds((S, D), BF), sspec, False)])[0]


def merge_bwd(dm, dmg, P, attn, hf, hb):
    S, T, D, bm, cw, nCb = dm.S, dm.T, dm.D, dm.bm, dm.cw, dm.nCb
    nj = D // cw

    def body(dm_ref, lg_ref, ga_ref, gl_ref, at_ref, hf_ref, hb_ref, dp_ref, da_ref, dh_ref, buf, sems):
        i, j = pl.program_id(0), pl.program_id(1)
        lat = i >= nCb
        d = jnp.where(lat, dm_ref[...].astype(F32), 0.0)
        lg = lg_ref[...]
        ge, th = _gelu(lg)
        hs = hf_ref[...] + hb_ref[...]
        sa, sl = _sig(ga_ref[...]), _sig(gl_ref[...])
        at = jnp.where(lat, at_ref[...], 0.0)
        dlru = d * sl
        buf[0] = (dlru * hs * _gelu_grad(lg, th)).astype(BF)
        buf[1] = (d * at * sa * (1.0 - sa)).astype(BF)
        buf[2] = (d * hs * ge * sl * (1.0 - sl)).astype(BF)
        da_ref[...] = (d * sa).astype(BF)
        dh_ref[...] = dlru * ge
        copies = []
        for g, off in enumerate((dm.OFF_LG, dm.OFF_GA, dm.OFF_GL)):
            col = pl.multiple_of(off + j * cw, 128)
            cp = pltpu.make_async_copy(buf.at[g], dp_ref.at[pl.ds(pl.multiple_of(i * bm, 8), bm), pl.ds(col, cw)],
                                       sems.at[g])
            cp.start()
            copies.append(cp)
        for cp in copies:
            cp.wait()

    pspec = lambda off: pl.BlockSpec((bm, cw), lambda i, j: (i, off // cw + j))
    tspec = pl.BlockSpec((bm, cw), lambda i, j: (i, j))
    lspec = pl.BlockSpec((bm, cw), lambda i, j: (jnp.maximum(i - nCb, 0), j))
    return pl.pallas_call(
        body, grid=(dm.nTb, nj), name="merge_bwd",
        in_specs=[lspec, pspec(dm.OFF_LG), pspec(dm.OFF_GA), pspec(dm.OFF_GL), lspec, tspec, tspec],
        out_specs=[ANY, tspec, tspec],
        out_shape=[_sds((T, dm.INW), BF), _sds((T, D), BF), _sds((T, D), F32)],
        scratch_shapes=[pltpu.VMEM((3, bm, cw), BF), pltpu.SemaphoreType.DMA((3,))],
        compiler_params=_cparams(),
    )(dmg, P, P, P, attn, hf, hb)


def final_loss(dm, x3, gfin, target):
    S, D, bm = dm.S, dm.D, dm.bm

    def fn(ids, x_ref, g_ref, t_ref):
        xv = x_ref[...]
        g = g_ref[...]
        r = lax.rsqrt(jnp.mean(xv * xv, axis=-1, keepdims=True) + EPS)
        xn = xv * r
        err = xn * g - t_ref[...]
        loss = 0.5 * jnp.sum(jnp.mean(err * err, axis=-1, keepdims=True), axis=0, keepdims=True)
        dy = err / D
        dxn = dy * g
        dx = r * (dxn - xn * jnp.mean(dxn * xn, axis=-1, keepdims=True))
        return [jnp.broadcast_to(loss, (1, 128)), dx, jnp.sum(dy * xn, axis=0, keepdims=True)]

    row = pl.BlockSpec((bm, D), lambda i: (i, 0))
    vec = pl.BlockSpec((1, D), lambda i: (0, 0))
    return ew_call(
        "final_loss", (dm.nSb,), fn, [(x3, row), (gfin, vec), (target, row)],
        [(_sds((1, 128), F32), pl.BlockSpec((1, 128), lambda i: (0, 0)), True), (_sds((S, D), F32), row, False),
         (_sds((1, D), F32), vec, True)], first=lambda ids: ids[0] == 0)


def local_step(dm, x, ctx, target, modv, norm_g3, gfin, gq, gk, conv_w, conv_b, wa, ba, wx, bx, lam, wbuf, where):
    S, C, T, D, NS, F4, W4 = dm.S, dm.C, dm.T, dm.D, dm.NS, dm.F4, dm.W4
    Ds = D // NS
    wb, nsub = dm.wb, dm.nsub
    cosf, sinf = rope_tables(dm)
    sp = jax.nn.softplus(-lam)
    sg = jax.nn.sigmoid(-lam)
    ident = lambda ids, accs, ex: list(accs)
    mT, mS, kT, kS = dm.mT, dm.mS, dm.kT, dm.kS
    bn = _pick(D, [1024, 512, 256, 128])
    bk = _pick(D, [512, 256, 128])

    wg0, wu0, wd0 = comm_call("ag_ffn1", ag_comm([wbuf['wg0'], wbuf['wu0'], wbuf['wd0']]))
    h1, xt = normmod_concat_fwd("nm1", dm, ctx, x, norm_g3, modv)
    xt1, a1, u1, s1, f1, land, _ = ffn_fwd("ffn1", dm, h1, xt, wg0, wu0, wd0, modv, 2, True,
                                           comm_up=ag_comm([wbuf['w_in'], wbuf['w_out']]))
    w_in, w_out = land[0], land[1].reshape(D, D)
    h2 = normmod_fwd("nm2", dm, xt1, norm_g3, 1, modv, True)
    P = fused_mm(
        "w_in", (T // mT, NS * nsub, 1),
        [(h2, pl.BlockSpec((mT, D), lambda i, j, k: (i, 0))),
         (w_in, pl.BlockSpec((None, D, wb), lambda i, j, k: (j // nsub, 0, j % nsub)))],
        [(0, 1, NN, 0)], [(mT, wb)], ident,
        [(_sds((T, dm.INW), F32), pl.BlockSpec((mT, wb), lambda i, j, k: (i, j)))])[0]
    qr, kr, vb = qk_prep(dm, P, gq, gk, cosf, sinf)
    (attn,), (wg1, wu1, wd1) = attention_fwd(dm, qr, kr, vb, comm=ag_comm([wbuf['wg1'], wbuf['wu1'], wbuf['wd1']]))
    hf, hb = lru_fwd(dm, P, conv_w, conv_b, wa, ba, wx, bx, sp)
    mg = merge_fwd(dm, P, attn, hf, hb)

    def epi_o(ids, accs, ex):
        o = accs[0]
        return [ex[0][...] + ex[1][...] * o, o]

    rb, nCb = dm.bm, dm.nCb
    x2, o2 = fused_mm(
        "w_out", (D // bn, S // rb, 1),
        [(mg, pl.BlockSpec((rb, D), lambda j, i, k: (i, 0))), (w_out, pl.BlockSpec((D, bn), lambda j, i, k: (0, j)))],
        [(0, 1, NN, 0)], [(rb, bn)], epi_o,
        [(_sds((S, D), F32), pl.BlockSpec((rb, bn), lambda j, i, k: (i, j))),
         (_sds((S, D), BF), pl.BlockSpec((rb, bn), lambda j, i, k: (i, j)))],
        extras=[(xt1, pl.BlockSpec((rb, bn), lambda j, i, k: (i + nCb, j))),
                (modv, pl.BlockSpec((None, None, 1, bn), lambda j, i, k: (1, 5, 0, j)))])
    h3 = normmod_fwd("nm3", dm, x2, norm_g3, 2, modv, False)
    x3, a3, u3, s3, f3, _, _ = ffn_fwd("ffn2", dm, h3, x2, wg1, wu1, wd1, modv, 8, False)
    loss, dx3, dgfin = final_loss(dm, x3, gfin, target)

    df3, dg3 = gate_bwd("gate3", dm, dx3, f3, modv, 8, FFN_RES, False)
    dh3, dwg1, dwu1, dwd1, _ = ffn_bwd("ffn2b", dm, df3, h3, a3, u3, s3, wg1, wu1, wd1, False)
    dx2, dsh3, dsc3, dgn3 = normmod_bwd("nm3b", dm, dh3, x2, dx3, norm_g3, 2, modv, False, False)
    do2, dg2 = gate_bwd("gate2", dm, dx2, o2, modv, 5, 1.0, False)
    keep = {}

    def host_a(key, comm):
        if key == 'p1':
            (keep['dmg'],), landed = fused_mm(
                "w_out_dx", (S // mS, D // bn, 1),
                [(do2, pl.BlockSpec((mS, D), lambda i, j, k: (i, 0))),
                 (w_out, pl.BlockSpec((bn, D), lambda i, j, k: (j, 0)))],
                [(0, 1, NT, 0)], [(mS, bn)], ident,
                [(_sds((S, D), BF), pl.BlockSpec((mS, bn), lambda i, j, k: (i, j)))], comm=comm)
            return landed
        keep['dqkv'], landed = attention_bwd(dm, qr, kr, vb, attn, keep['dattn'], comm=comm)
        return landed

    gots_a = host_a('p1', rs_p1_comm([dwg1, dwu1, dwd1]))
    dmg = keep['dmg']
    dw_out = fused_mm(
        "w_out_dw", (D // bn, D // bn, S // kS),
        [(mg, pl.BlockSpec((kS, bn), lambda i, j, k: (k, i))), (do2, pl.BlockSpec((kS, bn), lambda i, j, k: (k, j)))],
        [(0, 1, TN, 0)], [(bn, bn)], ident,
        [(_sds((D, D), BF), pl.BlockSpec((bn, bn), lambda i, j, k: (i, j)))])[0]
    dP, dattn, dhs = merge_bwd(dm, dmg, P, attn, hf, hb)
    keep['dattn'] = dattn
    pairs_a = [add_pair("rs_add_" + n_, g_, got_, where)
               for n_, g_, got_ in zip(('wg1', 'wu1', 'wd1'), (dwg1, dwu1, dwd1), gots_a)]
    land_a = host_a('p2', rs_p2_comm([p_[0] for p_ in pairs_a], [p_[1] for p_ in pairs_a]))
    dq, dk, dv = keep['dqkv']
    dP, dwa, dba, dwx, dbx, dlam, dcw, dcb = lru_bwd(dm, P, dhs, hf, hb, conv_w, conv_b, wa, ba, wx, bx, sp, sg, dP)
    dP, dgq, dgk = qk_prep_bwd(dm, dq, dk, dv, P, gq, gk, cosf, sinf, dP)
    g_wg = sum_slots_into("rs_sum_wg1", land_a[0], where, None, (2, D, F4), 1)
    g_wu = sum_slots_into("rs_sum_wu1", land_a[1], where, None, (2, D, F4), 1)
    g_wd = sum_slots_into("rs_sum_wd1", land_a[2], where, None, (2, F4, D), 1)
    (dh2,), (g_wg, g_wu, g_wd) = fused_mm(
        "w_in_dx", (T // mT, D // bn, NS),
        [(dP, pl.BlockSpec((mT, W4), lambda i, j, k: (i, k))),
         (w_in, pl.BlockSpec((None, bn, W4), lambda i, j, k: (k, j, 0)))],
        [(0, 1, NT, 0)], [(mT, bn)], ident,
        [(_sds((T, D), F32), pl.BlockSpec((mT, bn), lambda i, j, k: (i, j)))],
        comm=rs_p3_comm([g_wg, g_wu, g_wd], [(0, 1, D // 2), (1, 1, D // 2), (2, 1, F4 // 2)]))
    dw_in = fused_mm(
        "w_in_dw", (D // bn, NS, T // kT),
        [(h2, pl.BlockSpec((kT, bn), lambda i, j, k: (k, i))), (dP, pl.BlockSpec((kT, W4), lambda i, j, k: (k, j)))],
        [(0, 1, TN, 0)], [(bn, W4)], ident,
        [(_sds((NS, D, W4), BF), pl.BlockSpec((None, bn, W4), lambda i, j, k: (j, i, 0)))])[0]
    dxt1, dsh2, dsc2, dgn2 = normmod_bwd("nm2b", dm, dh2, xt1, dx2, norm_g3, 1, modv, True, True)
    df1, dg1 = gate_bwd("gate1", dm, dxt1, f1, modv, 2, FFN_RES, True)

    tens_b = [dw_in, dw_out.reshape(NS, Ds, D)]

    def phase2(landed):
        pairs = [add_pair("rs_add_" + n_, g_, got_, where) for n_, g_, got_ in zip(('w_in', 'w_out'), tens_b, landed['ds'])]
        return rs_p2_comm([p_[0] for p_ in pairs], [p_[1] for p_ in pairs])

    def phase3(landed):
        g_win = sum_slots_into("rs_sum_w_in", landed['dwgu'][0], where, None, (D, W4), None)
        g_wout = sum_slots_into("rs_sum_w_out", landed['dwgu'][1], where, None, (Ds, D), None)
        return rs_p3_comm([g_win, g_wout], [(0, None, D // 2), (1, None, Ds // 2)])

    dh1, dwg0, dwu0, dwd0, landed = ffn_bwd("ffn1b", dm, df1, h1, a1, u1, s1, wg0, wu0, wd0, True,
                                            comms={'ds': lambda L: rs_p1_comm(tens_b), 'dwgu': phase2, 'dh': phase3})
    g_win, g_wout = landed['dh']
    grad_x, dsh1, dsc1, dgn1 = normmod_bwd("nm1b", dm, dh1, xt, dxt1, norm_g3, 0, modv, True, False, out_lat_only=True)

    dmod = jnp.concatenate([dsh1, dsc1, dg1, dsh2, dsc2, _lat(dg2), _lat(dsh3), _lat(dsc3), _lat(dg3)], axis=1)
    dnorm = jnp.stack([dgn1[0, 0] + dgn1[1, 0], dgn2[0, 0] + dgn2[1, 0], dgn3[1, 0]], axis=0)
    small = dict(norm_g=dnorm, q_norm_g=dgq, k_norm_g=dgk, conv_w=dcw, conv_b=dcb,
                 lru_wa=dwa, lru_ba=dba.reshape(2, D), lru_wx=dwx, lru_bx=dbx.reshape(2, D),
                 lru_lambda=dlam.reshape(2, D), final_norm_g=dgfin)
    reduced = dict(ffn_wg=g_wg, ffn_wu=g_wu, ffn_wd=g_wd, w_in=g_win, w_out=g_wout)
    return loss, grad_x, dmod, small, reduced, [dwg0, dwu0, dwd0]


def _lat(v):
    return jnp.concatenate([jnp.zeros_like(v[:1]), v[1:]], axis=0)


def _me():
    return lax.axis_index("x"), lax.axis_index("y"), lax.axis_index("c")


def allgather8(name, v):
    def body(v_ref, out_ref, send_sems, recv_sems, local_sem):
        x, y, c = _me()
        me = 4 * x + 2 * y + c
        mine = pltpu.make_async_copy(v_ref, out_ref.at[me], local_sem)
        mine.start()
        copies = []
        for k in range(1, 8):
            peer = (x ^ ((k >> 2) & 1), y ^ ((k >> 1) & 1), c ^ (k & 1))
            cp = pltpu.make_async_remote_copy(src_ref=v_ref, dst_ref=out_ref.at[me], send_sem=send_sems.at[k - 1],
                                              recv_sem=recv_sems.at[k - 1], device_id=peer, device_id_type=MESH)
            cp.start()
            copies.append(cp)
        for k in range(1, 8):
            peer = (x ^ ((k >> 2) & 1), y ^ ((k >> 1) & 1), c ^ (k & 1))
            pltpu.make_async_remote_copy(src_ref=v_ref, dst_ref=out_ref.at[me ^ k], send_sem=send_sems.at[k - 1],
                                         recv_sem=recv_sems.at[k - 1], device_id=peer, device_id_type=MESH).wait_recv()
        for cp in copies:
            cp.wait_send()
        mine.wait()

    return pl.pallas_call(
        body, name=name, out_shape=_sds((8,) + v.shape, v.dtype), in_specs=[ANY], out_specs=ANY,
        scratch_shapes=[pltpu.SemaphoreType.DMA((7,)), pltpu.SemaphoreType.DMA((7,)), pltpu.SemaphoreType.DMA],
    )(v)


def _chips(x, y):
    chips = [(1 - x, y), (x, 1 - y), (1 - x, 1 - y)]
    return chips, [2 * cx + cy for cx, cy in chips]


def ag_comm(bufs):
    n = len(bufs)

    def parts(outs):
        x, y, c = _me()
        chips, slots = _chips(x, y)
        return x, y, c, 2 * x + y, (x, y, 1 - c), chips, slots

    def ici(outs, t, j, send_sems, recv_sems, src_slot):
        x, y, c, s, sib, chips, slots = parts(outs)
        H = outs[t].shape[1] // 2
        blk = outs[t].at[src_slot, pl.ds(c * H, H)]
        return pltpu.make_async_remote_copy(
            src_ref=blk, dst_ref=blk, send_sem=send_sems.at[6 * t + j], recv_sem=recv_sems.at[6 * t + j],
            device_id=(chips[j][0], chips[j][1], c), device_id_type=MESH)

    def d2d(outs, t, j, send_sems, recv_sems, half):
        x, y, c, s, sib, chips, slots = parts(outs)
        H = outs[t].shape[1] // 2
        blk = outs[t].at[slots[j], pl.ds(half * H, H)]
        return pltpu.make_async_remote_copy(
            src_ref=blk, dst_ref=blk, send_sem=send_sems.at[6 * t + 3 + j], recv_sem=recv_sems.at[6 * t + 3 + j],
            device_id=sib, device_id_type=MESH)

    def start(reads, outs, send_sems, recv_sems):
        x, y, c, s, sib, chips, slots = parts(outs)
        for t in range(n):
            for j in range(3):
                ici(outs, t, j, send_sems, recv_sems, s).start()

    def finish(reads, outs, send_sems, recv_sems):
        x, y, c, s, sib, chips, slots = parts(outs)
        for t in range(n):
            for j in range(3):
                ici(outs, t, j, send_sems, recv_sems, slots[j]).wait_recv()
                d2d(outs, t, j, send_sems, recv_sems, c).start()
        for t in range(n):
            for j in range(3):
                d2d(outs, t, j, send_sems, recv_sems, 1 - c).wait_recv()
        for t in range(n):
            for j in range(3):
                ici(outs, t, j, send_sems, recv_sems, s).wait_send()
                d2d(outs, t, j, send_sems, recv_sems, c).wait_send()

    return Comm([], bufs, 6 * n, start, finish)


def rs_p1_comm(tensors):
    n = len(tensors)

    def copy(ins, gots, t, send_sems, recv_sems):
        x, y, c = _me()
        H = ins[t].shape[1] // 2
        return pltpu.make_async_remote_copy(
            src_ref=ins[t].at[:, pl.ds((1 - c) * H, H)], dst_ref=gots[t], send_sem=send_sems.at[t],
            recv_sem=recv_sems.at[t], device_id=(x, y, 1 - c), device_id_type=MESH)

    def start(ins, gots, send_sems, recv_sems):
        for t in range(n):
            copy(ins, gots, t, send_sems, recv_sems).start()

    def finish(ins, gots, send_sems, recv_sems):
        for t in range(n):
            copy(ins, gots, t, send_sems, recv_sems).wait_recv()
        for t in range(n):
            copy(ins, gots, t, send_sems, recv_sems).wait_send()

    half = lambda t: _sds((t.shape[0], t.shape[1] // 2) + t.shape[2:], t.dtype)
    return Comm(tensors, [half(t) for t in tensors], n, start, finish)


def rs_p2_comm(partials, landeds):
    n = len(partials)

    def start(ins, outs, send_sems, recv_sems):
        x, y, c = _me()
        s = 2 * x + y
        chips, slots = _chips(x, y)
        for t in range(n):
            for j, chip in enumerate(chips):
                src = ins[t].at[slots[j]] if ins[t].shape[0] == 4 else ins[t].at[0]
                pltpu.make_async_remote_copy(
                    src_ref=src, dst_ref=outs[t].at[s], send_sem=send_sems.at[3 * t + j],
                    recv_sem=recv_sems.at[3 * t + j], device_id=(chip[0], chip[1], c), device_id_type=MESH).start()

    def finish(ins, outs, send_sems, recv_sems):
        x, y, c = _me()
        s = 2 * x + y
        chips, slots = _chips(x, y)
        for t in range(n):
            for j, chip in enumerate(chips):
                dst = outs[t].at[slots[j]]
                pltpu.make_async_remote_copy(
                    src_ref=dst, dst_ref=dst, send_sem=send_sems.at[3 * t + j],
                    recv_sem=recv_sems.at[3 * t + j], device_id=(chip[0], chip[1], c), device_id_type=MESH).wait_recv()
        for t in range(n):
            for j, chip in enumerate(chips):
                src = ins[t].at[slots[j]] if ins[t].shape[0] == 4 else ins[t].at[0]
                pltpu.make_async_remote_copy(
                    src_ref=src, dst_ref=outs[t].at[s], send_sem=send_sems.at[3 * t + j],
                    recv_sem=recv_sems.at[3 * t + j], device_id=(chip[0], chip[1], c), device_id_type=MESH).wait_send()

    return Comm(partials, landeds, 3 * n, start, finish)


def rs_p3_comm(greds, plan):
    n = len(plan)

    def copy(outs, t, send_sems, recv_sems, half):
        x, y, c = _me()
        oi, li, H = plan[t]
        dst = outs[oi] if li is None else outs[oi].at[li]
        blk = dst.at[pl.ds((c if half == 0 else 1 - c) * H, H)]
        return pltpu.make_async_remote_copy(
            src_ref=blk, dst_ref=blk, send_sem=send_sems.at[t], recv_sem=recv_sems.at[t],
            device_id=(x, y, 1 - c), device_id_type=MESH)

    def start(reads, outs, send_sems, recv_sems):
        for t in range(n):
            copy(outs, t, send_sems, recv_sems, 0).start()

    def finish(reads, outs, send_sems, recv_sems):
        for t in range(n):
            copy(outs, t, send_sems, recv_sems, 1).wait_recv()
        for t in range(n):
            copy(outs, t, send_sems, recv_sems, 0).wait_send()

    return Comm([], greds, n, start, finish)


def _rows_block(rows, cols, nbytes=1 << 20):
    bm = 8
    while bm * 2 * cols * 4 <= nbytes and rows % (bm * 2) == 0:
        bm *= 2
    return bm


def cast_into_slot(name, w, where, layer=None):
    rows, W = w.shape[-2:]
    bm = _rows_block(rows, W)

    def body(p_ref, w_ref, o_ref):
        o_ref[...] = w_ref[...].astype(BF)

    if layer is None:
        ispec = pl.BlockSpec((bm, W), lambda i, p: (i, 0))
    else:
        ispec = pl.BlockSpec((None, bm, W), lambda i, p: (layer, i, 0))
    return pl.pallas_call(
        body, name=name, out_shape=_sds((4, rows, W), BF), compiler_params=_cparams(),
        grid_spec=pltpu.PrefetchScalarGridSpec(
            num_scalar_prefetch=1, grid=(rows // bm,), in_specs=[ispec],
            out_specs=pl.BlockSpec((None, bm, W), lambda i, p: (p[1], i, 0))),
    )(where, w)


def add_pair(name, g, got, where):
    K, R, W = g.shape
    H = R // 2
    bm = _rows_block(H, W)
    nh = H // bm

    def body(p_ref, g_ref, got_ref, part_ref, land_ref):
        k = pl.program_id(1)
        v = (g_ref[...].astype(F32) + got_ref[...].astype(F32)).astype(part_ref.dtype)
        part_ref[...] = v
        own = (k == p_ref[1]) if K == 4 else (k == 0)

        @pl.when(own)
        def _():
            land_ref[...] = v

    return pl.pallas_call(
        body, name=name, out_shape=[_sds((K, H, W), g.dtype), _sds((4, H, W), g.dtype)], compiler_params=_cparams(),
        grid_spec=pltpu.PrefetchScalarGridSpec(
            num_scalar_prefetch=1, grid=(nh, K),
            in_specs=[pl.BlockSpec((None, bm, W), lambda i, k, p: (k, p[0] * nh + i, 0)),
                      pl.BlockSpec((None, bm, W), lambda i, k, p: (k, i, 0))],
            out_specs=[pl.BlockSpec((None, bm, W), lambda i, k, p: (k, i, 0)),
                       pl.BlockSpec((None, bm, W), lambda i, k, p: (p[1], i, 0))]),
    )(where, g, got)


def sum_slots_into(name, landed, where, dest, dest_shape, li):
    K, H, W = landed.shape
    bm = _rows_block(H, 2 * W)
    nh = H // bm

    def body(*refs):
        r, o_ref = refs[1], refs[-1]
        acc = r[0].astype(F32)
        for k in range(1, K):
            acc = acc + r[k].astype(F32)
        o_ref[...] = acc

    if li is None:
        ospec = pl.BlockSpec((bm, W), lambda i, p: (p[0] * nh + i, 0))
    else:
        ospec = pl.BlockSpec((None, bm, W), lambda i, p: (li, p[0] * nh + i, 0))
    in_specs = [pl.BlockSpec((K, bm, W), lambda i, p: (0, i, 0))]
    args = [where, landed]
    aliases = {}
    if dest is not None:
        in_specs.append(ANY)
        args.append(dest)
        aliases = {2: 0}
    return pl.pallas_call(
        body, name=name, out_shape=_sds(dest_shape, F32), compiler_params=_cparams(), input_output_aliases=aliases,
        grid_spec=pltpu.PrefetchScalarGridSpec(num_scalar_prefetch=1, grid=(nh,), in_specs=in_specs, out_specs=ospec),
    )(*args)


def _adamw_math(w, g, m, v):
    bc1 = 1.0 - ADAM_B1 ** ADAM_STEP
    bc2 = 1.0 - ADAM_B2 ** ADAM_STEP
    mn = ADAM_B1 * m + (1.0 - ADAM_B1) * g
    vn = ADAM_B2 * v + (1.0 - ADAM_B2) * (g * g)
    m_hat = mn / bc1
    v_hat = vn / bc2
    delta = -ADAM_LR * (m_hat / (jnp.sqrt(v_hat) + ADAM_EPS) + ADAM_WD * w)
    return delta, mn, vn


def adamw(name, w, g, m, v):
    shape = w.shape
    flat = lambda t: t.reshape(-1, shape[-1])
    w2, g2, m2, v2 = flat(w), flat(g), flat(m), flat(v)
    bm = _rows_block(w2.shape[0], w2.shape[1] * 2)

    def fn(ids, w_ref, g_ref, m_ref, v_ref):
        return list(_adamw_math(w_ref[...], g_ref[...], m_ref[...], v_ref[...]))

    spec = pl.BlockSpec((bm, w2.shape[1]), lambda i: (i, 0))
    outs = ew_call(name, (w2.shape[0] // bm,), fn, [(w2, spec), (g2, spec), (m2, spec), (v2, spec)],
                   [(_sds(w2.shape, F32), spec, False)] * 3)
    return [o.reshape(shape) for o in outs]


def adamw_many(name, params):
    n = len(params)
    shapes = [p_[0].shape for p_ in params]
    two_d = lambda t: t.reshape(-1, t.shape[-1])
    args = [two_d(t) for p_ in params for t in p_]

    def body(*refs):
        ins, outs = refs[:4 * n], refs[4 * n:]
        for q in range(n):
            w_ref, g_ref, m_ref, v_ref = ins[4 * q:4 * q + 4]
            for o_ref, val in zip(outs[3 * q:3 * q + 3], _adamw_math(w_ref[...], g_ref[...], m_ref[...], v_ref[...])):
                o_ref[...] = val

    full = lambda a: pl.BlockSpec(a.shape, lambda i: (0, 0))
    out_shape = [_sds(args[4 * q].shape, F32) for q in range(n) for _ in range(3)]
    res = hosted_call(body, name=name, grid=(1,), in_specs=[full(a) for a in args],
                      out_specs=[full(o) for o in out_shape], out_shape=out_shape, args=args)
    return [tuple(res[3 * q + r].reshape(shapes[q]) for r in range(3)) for q in range(n)]


def dmod_pack(gd):
    N = gd.shape[-1]
    bn = _pick(N, [4608, 2304, 1152, 1024, 512, 256, 128])

    def fn(ids, r):
        lat = [r[d, 1:2, :] for d in range(8)]
        cs = r[0, 0:1, :]
        for d in range(1, 8):
            cs = cs + r[d, 0:1, :]
        tot = cs
        for d in range(8):
            tot = tot + lat[d]
        return [jnp.concatenate(lat + [cs, jnp.zeros((7, bn), F32)], axis=0), tot]

    return ew_call("dmod_pack", (N // bn,), fn, [(gd, pl.BlockSpec((8, 2, bn), lambda j: (0, 0, j)))],
                   [(_sds((16, N), F32), pl.BlockSpec((16, bn), lambda j: (0, j)), False),
                    (_sds((1, N), F32), pl.BlockSpec((1, bn), lambda j: (0, j)), False)])


def _silu(v):
    return v * _sig(v)


def kernel(x, c, ctx, c_ctx, w_mod, b_mod, norm_g, ffn_wg, ffn_wu, ffn_wd, w_in, w_out, q_norm_g, k_norm_g, conv_w, conv_b, lru_wa, lru_ba, lru_wx, lru_bx, lru_lambda, final_norm_g, loss_target, m_c_ctx, m_w_mod, m_b_mod, m_norm_g, m_ffn_wg, m_ffn_wu, m_ffn_wd, m_w_in, m_w_out, m_q_norm_g, m_k_norm_g, m_conv_w, m_conv_b, m_lru_wa, m_lru_ba, m_lru_wx, m_lru_bx, m_lru_lambda, m_final_norm_g, v_c_ctx, v_w_mod, v_b_mod, v_norm_g, v_ffn_wg, v_ffn_wu, v_ffn_wd, v_w_in, v_w_out, v_q_norm_g, v_k_norm_g, v_conv_w, v_conv_b, v_lru_wa, v_lru_ba, v_lru_wx, v_lru_bx, v_lru_lambda, v_final_norm_g):
    given = dict(locals())
    names = ['c_ctx', 'w_mod', 'b_mod', 'norm_g', 'ffn_wg', 'ffn_wu', 'ffn_wd', 'w_in', 'w_out', 'q_norm_g', 'k_norm_g',
             'conv_w', 'conv_b', 'lru_wa', 'lru_ba', 'lru_wx', 'lru_bx', 'lru_lambda', 'final_norm_g']
    S, D = x.shape[1], x.shape[2]
    C = ctx.shape[1]
    NS = 4
    F4, W4, LB = ffn_wg.shape[-1], w_in.shape[-1], lru_wa.shape[2]
    dm = Dims(S, C, D, F4, W4, NS, LB)
    Ds = D // NS
    Wm = w_mod.shape[-1]
    xi, yi, ci = lax.axis_index("x"), lax.axis_index("y"), lax.axis_index("c")
    slot = 2 * xi + yi
    me = 4 * xi + 2 * yi + ci
    ident = lambda ids, accs, ex: list(accs)

    pack1 = jnp.concatenate([c.reshape(-1), norm_g.reshape(-1), conv_w.reshape(-1), lru_ba.reshape(-1),
                             lru_bx.reshape(-1), lru_lambda.reshape(-1)]).reshape(1, -1)
    g1 = allgather8("ag_small_params", pack1)[:, 0]
    c_all = g1[:, :D]

    def unshard(off, k):
        part = g1[0::2, off:off + k * Ds].reshape(NS, k, Ds)
        return jnp.transpose(part, (1, 0, 2)).reshape(k, D)

    norm_g_f = unshard(D, 3)
    conv_w_f = unshard(D + 3 * Ds, 4)
    ba_f = unshard(D + 7 * Ds, 2)
    bx_f = unshard(D + 9 * Ds, 2)
    lam_f = unshard(D + 11 * Ds, 2)

    call16 = jnp.concatenate([c_all, c_ctx.reshape(1, D), jnp.zeros((7, D), F32)], axis=0)
    b_cols = lax.dynamic_slice(b_mod, (0, slot * Wm), (1, Wm))
    bnm = _pick(Wm, [1536, 1152, 768, 512, 384, 256, 128])
    bkm = _pick(D, [512, 256, 128])
    modp = fused_mm(
        "mod_fwd", (1, Wm // bnm, D // bkm),
        [(call16, pl.BlockSpec((16, bkm), lambda i, j, k: (0, k))),
         (w_mod[0], pl.BlockSpec((bkm, bnm), lambda i, j, k: (k, j)))],
        [(0, 1, NN, 0)], [(16, bnm)], lambda ids, accs, ex: [accs[0] + ex[0][...]],
        [(_sds((16, Wm), F32), pl.BlockSpec((16, bnm), lambda i, j, k: (0, j)))],
        extras=[(b_cols, pl.BlockSpec((1, bnm), lambda i, j, k: (0, j)))], pre={0: _silu})[0]
    gm = allgather8("ag_mod", modp)
    mod_full = jnp.concatenate([gm[0], gm[2], gm[4], gm[6]], axis=1)
    mod_x = lax.dynamic_index_in_dim(mod_full, me, axis=0, keepdims=False)
    modv = jnp.stack([mod_full[8], mod_x]).reshape(2, N_MOD, 1, D)

    where = jnp.stack([ci, slot]).astype(jnp.int32)
    wbuf = {}
    for key, short in (('ffn_wg', 'wg'), ('ffn_wu', 'wu'), ('ffn_wd', 'wd')):
        for l in range(2):
            wbuf[short + str(l)] = cast_into_slot("cast_%s%d" % (short, l), given[key][0], where, l)
    wbuf['w_in'] = cast_into_slot("cast_w_in", w_in[0], where)
    wbuf['w_out'] = cast_into_slot("cast_w_out", w_out[0], where)

    loss_l, grad_x, dmod, small, reduced, first_ffn = local_step(
        dm, x[0], ctx[0], loss_target[0], modv, norm_g_f.reshape(3, 1, D), final_norm_g.reshape(1, D),
        q_norm_g, k_norm_g, conv_w_f, conv_b, lru_wa[0], ba_f.reshape(2, 1, D), lru_wx[0], bx_f.reshape(2, 1, D),
        lam_f.reshape(2, 1, D), wbuf, where)
    loss = lax.psum(loss_l[0, 0], ("x", "y", "c"))

    grads = {}
    gd = allgather8("ag_dmod", dmod.reshape(2, N_MOD * D))
    dM, g_bmod = dmod_pack(gd)
    dMc = lax.dynamic_slice(dM, (0, slot * Wm), (16, Wm))
    bmm = _pick(D, [512, 256, 128])
    grads['w_mod'] = fused_mm(
        "w_mod_dw", (D // bmm, Wm // bnm, 1),
        [(call16, pl.BlockSpec((16, bmm), lambda i, j, k: (0, i))), (dMc, pl.BlockSpec((16, bnm), lambda i, j, k: (0, j)))],
        [(0, 1, TN, 0)], [(bmm, bnm)], ident,
        [(_sds((D, Wm), F32), pl.BlockSpec((bmm, bnm), lambda i, j, k: (i, j)))], pre={0: _silu})[0][None]
    grads['b_mod'] = g_bmod

    def epi_cc(ids, accs, ex):
        v = ex[0][...]
        sg = _sig(v)
        return [accs[0] * (sg * (1.0 + v * (1.0 - sg)))]

    pcc = fused_mm(
        "c_ctx_partial", (1, D // bmm, Wm // bnm),
        [(dMc, pl.BlockSpec((16, bnm), lambda i, j, k: (0, k))), (w_mod[0], pl.BlockSpec((bmm, bnm), lambda i, j, k: (j, k)))],
        [(0, 1, NT, 0)], [(16, bmm)], epi_cc,
        [(_sds((16, D), F32), pl.BlockSpec((16, bmm), lambda i, j, k: (0, j)))],
        extras=[(c_ctx.reshape(1, D), pl.BlockSpec((1, bmm), lambda i, j, k: (0, j)))])[0]
    pcc_row = jnp.where(ci == 0, pcc[8], 0.0)

    order = ['q_norm_g', 'k_norm_g', 'conv_b', 'final_norm_g', 'norm_g', 'conv_w', 'lru_ba', 'lru_bx', 'lru_lambda']
    flat = [small[k].reshape(-1) for k in order] + [pcc_row]
    sizes = [f.shape[0] for f in flat]
    tot = sum(sizes)
    LW = 1024
    padded = -(-tot // (16 * LW)) * (16 * LW)
    tiny = jnp.concatenate(flat + [jnp.zeros((padded - tot,), F32)]).reshape(1, -1, LW)
    RS = tiny.shape[1]
    LBD = D // LB
    mats = [small['lru_wa'].reshape(1, 2 * LB * LBD, LBD), small['lru_wx'].reshape(1, 2 * LB * LBD, LBD)]
    tensors = first_ffn + mats + [tiny]
    tnames = ['wg0', 'wu0', 'wd0', 'lru_wa', 'lru_wx', 'tiny']
    gots = comm_call("rs_tail_p1", rs_p1_comm(tensors))
    pairs = [add_pair("rs_add_" + n_, g_, got_, where) for n_, g_, got_ in zip(tnames, tensors, gots)]
    landeds = comm_call("rs_tail_p2", rs_p2_comm([p_[0] for p_ in pairs], [p_[1] for p_ in pairs]))
    g_wg = sum_slots_into("rs_sum_wg0", landeds[0], where, reduced['ffn_wg'], (2, D, F4), 0)
    g_wu = sum_slots_into("rs_sum_wu0", landeds[1], where, reduced['ffn_wu'], (2, D, F4), 0)
    g_wd = sum_slots_into("rs_sum_wd0", landeds[2], where, reduced['ffn_wd'], (2, F4, D), 0)
    g_wa = sum_slots_into("rs_sum_lru_wa", landeds[3], where, None, (2 * LB * LBD, LBD), None)
    g_wx = sum_slots_into("rs_sum_lru_wx", landeds[4], where, None, (2 * LB * LBD, LBD), None)
    g_tiny = sum_slots_into("rs_sum_tiny", landeds[5], where, None, (RS, LW), None)
    g_wg, g_wu, g_wd, g_wa, g_wx, g_tiny = comm_call("rs_tail_p3", rs_p3_comm(
        [g_wg, g_wu, g_wd, g_wa, g_wx, g_tiny],
        [(0, 0, D // 2), (1, 0, D // 2), (2, 0, F4 // 2), (3, None, LB * LBD), (4, None, LB * LBD), (5, None, RS // 2)]))
    grads.update(ffn_wg=g_wg[None], ffn_wu=g_wu[None], ffn_wd=g_wd[None], w_in=reduced['w_in'][None],
                 w_out=reduced['w_out'][None], lru_wa=g_wa.reshape(lru_wa.shape), lru_wx=g_wx.reshape(lru_wx.shape))
    summed = g_tiny.reshape(-1)
    offs = {}
    o = 0
    for k, n_ in zip(order + ['c_ctx'], sizes):
        offs[k] = summed[o:o + n_]
        o += n_
    shard = lambda k, rows: lax.dynamic_slice_in_dim(offs[k].reshape(rows, D), slot * Ds, Ds, axis=1)
    grads['c_ctx'] = offs['c_ctx']
    grads['q_norm_g'] = offs['q_norm_g'].reshape(1, HEAD_DIM)
    grads['k_norm_g'] = offs['k_norm_g'].reshape(1, HEAD_DIM)
    grads['conv_b'] = offs['conv_b'].reshape(1, D)
    grads['final_norm_g'] = offs['final_norm_g']
    grads['norm_g'] = shard('norm_g', 3)[None]
    grads['conv_w'] = shard('conv_w', 4)[None]
    grads['lru_ba'] = shard('lru_ba', 2)[None]
    grads['lru_bx'] = shard('lru_bx', 2)[None]
    grads['lru_lambda'] = shard('lru_lambda', 2)[None]

    big_names = ['w_mod', 'ffn_wg', 'ffn_wu', 'ffn_wd', 'w_in', 'w_out', 'lru_wa', 'lru_wx']
    delta, new_m, new_v = {}, {}, {}
    for k in big_names:
        delta[k], new_m[k], new_v[k] = adamw("adamw_" + k, given[k], grads[k], given['m_' + k], given['v_' + k])
    tiny_names = [k for k in names if k not in big_names]
    res = adamw_many("adamw_tiny", [(given[k], grads[k], given['m_' + k], given['v_' + k]) for k in tiny_names])
    for k, (d_, m_, v_) in zip(tiny_names, res):
        delta[k], new_m[k], new_v[k] = d_, m_, v_

    return (loss, grad_x[None], *[grads[k] for k in names], *[delta[k] for k in names],
            *[new_m[k] for k in names], *[new_v[k] for k in names])
```

```python
import functools

import jax
import jax.numpy as jnp
from jax import lax
from jax.experimental import pallas as pl
from jax.experimental.pallas import tpu as pltpu

F32 = jnp.float32
BF = jnp.bfloat16
EPS = 1e-6
HEAD_DIM = 128
GRID_W = 64
ROPE_THETA = 10000.0
LRU_C = 8.0
FFN_RES = 0.5
N_MOD = 9
ADAM_LR, ADAM_B1, ADAM_B2, ADAM_EPS, ADAM_WD, ADAM_STEP = 0.001, 0.9, 0.999, 1e-08, 0.01, 10
VMEM_LIMIT = 52 * 1024 * 1024
MESH = pl.DeviceIdType.MESH
ANY = pl.BlockSpec(memory_space=pl.ANY)


def _sds(shape, dt):
    return jax.ShapeDtypeStruct(tuple(shape), dt)


def _pick(n, cands):
    for c in cands:
        if n % c == 0:
            return c
    return n


def _cparams(**kw):
    return pltpu.CompilerParams(vmem_limit_bytes=VMEM_LIMIT, **kw)


def _sig(x):
    return 1.0 / (1.0 + jnp.exp(-x))


def _gelu(x):
    t = jnp.tanh(0.7978845608028654 * (x + 0.044715 * x * x * x))
    return 0.5 * x * (1.0 + t), t


def _gelu_grad(x, t):
    return 0.5 * (1.0 + t) + 0.5 * x * (1.0 - t * t) * 0.7978845608028654 * (1.0 + 3.0 * 0.044715 * x * x)


class Comm:
    def __init__(self, reads, lands, n_sem, start, finish):
        self.reads, self.lands, self.n_sem, self.start, self.finish = list(reads), list(lands), n_sem, start, finish


def merge_comms(comms):
    reads = [r for c in comms for r in c.reads]
    lands = [l for c in comms for l in c.lands]

    def run(which):
        def fn(r, lo, send_sems, recv_sems, off=0):
            ro = lo_ = so = 0
            for c in comms:
                getattr(c, which)(r[ro:ro + len(c.reads)], lo[lo_:lo_ + len(c.lands)], send_sems, recv_sems, off + so)
                ro, lo_, so = ro + len(c.reads), lo_ + len(c.lands), so + c.n_sem
        return fn

    return Comm(reads, lands, sum(c.n_sem for c in comms), run('start'), run('finish'))


def hosted_call(body, *, name, grid, in_specs, out_specs, out_shape, args, scratch_shapes=(), aliases=None, comm=None):
    aliases = dict(aliases or {})
    if comm is None:
        return pl.pallas_call(
            body, name=name, grid=grid, in_specs=list(in_specs), out_specs=list(out_specs), out_shape=list(out_shape),
            scratch_shapes=list(scratch_shapes), input_output_aliases=aliases, compiler_params=_cparams())(*args)
    n_in, n_out, n_sc = len(args), len(out_shape), len(scratch_shapes)
    land_in = [(t, l) for t, l in enumerate(comm.lands) if not isinstance(l, jax.ShapeDtypeStruct)]
    nr, nli, nl = len(comm.reads), len(land_in), len(comm.lands)

    def wrapped(*refs):
        a = refs[:n_in]
        r = refs[n_in:n_in + nr]
        pos = n_in + nr + nli
        o = refs[pos:pos + n_out]
        lo = refs[pos + n_out:pos + n_out + nl]
        sc = refs[pos + n_out + nl:pos + n_out + nl + n_sc]
        send_sems, recv_sems = refs[pos + n_out + nl + n_sc:]
        ids = [pl.program_id(d) for d in range(len(grid))]
        first, last = ids[0] == 0, ids[0] == grid[0] - 1
        for d in range(1, len(grid)):
            first = first & (ids[d] == 0)
            last = last & (ids[d] == grid[d] - 1)

        @pl.when(first)
        def _():
            comm.start(r, lo, send_sems, recv_sems)

        body(*a, *o, *sc)

        @pl.when(last)
        def _():
            comm.finish(r, lo, send_sems, recv_sems)

    for q, (t, _) in enumerate(land_in):
        aliases[n_in + nr + q] = n_out + t
    res = pl.pallas_call(
        wrapped, name=name, grid=grid,
        in_specs=list(in_specs) + [ANY] * (nr + nli), out_specs=list(out_specs) + [ANY] * nl,
        out_shape=list(out_shape) + [l if isinstance(l, jax.ShapeDtypeStruct) else _sds(l.shape, l.dtype) for l in comm.lands],
        scratch_shapes=list(scratch_shapes) + [pltpu.SemaphoreType.DMA((comm.n_sem,)), pltpu.SemaphoreType.DMA((comm.n_sem,))],
        input_output_aliases=aliases, compiler_params=_cparams(),
    )(*args, *comm.reads, *[l for _, l in land_in])
    return list(res[:n_out]), list(res[n_out:])


def comm_call(name, comm):
    def body():
        pass

    return hosted_call(body, name=name, grid=(1,), in_specs=[], out_specs=[], out_shape=[], args=[], comm=comm)[1]


def ew_call(name, grid, fn, ins, outs, first=None, aliases=None, comm=None):
    n_in = len(ins)

    def body(*refs):
        ids = tuple(pl.program_id(a) for a in range(len(grid)))
        vals = fn(ids, *refs[:n_in])
        for (_, _, acc), o_ref, v in zip(outs, refs[n_in:], vals):
            if not acc:
                o_ref[...] = v.astype(o_ref.dtype)
            else:
                is_first = first(ids)

                @pl.when(is_first)
                def _(o_ref=o_ref, v=v):
                    o_ref[...] = v.astype(o_ref.dtype)

                @pl.when(jnp.logical_not(is_first))
                def _(o_ref=o_ref, v=v):
                    o_ref[...] += v.astype(o_ref.dtype)

    return hosted_call(body, name=name, grid=grid, in_specs=[s for _, s in ins], out_specs=[s for _, s, _ in outs],
                       out_shape=[o for o, _, _ in outs], args=[a for a, _ in ins], aliases=aliases, comm=comm)


def fused_mm(name, grid, ins, prods, acc_shapes, epi, outs, extras=(), pre=None, comm=None):
    n_in, n_ex, n_out = len(ins), len(extras), len(outs)
    nk = grid[-1]
    pre = pre or {}
    n_acc = len(acc_shapes)

    def body(*refs):
        in_refs = refs[:n_in]
        ex_refs = refs[n_in:n_in + n_ex]
        out_refs = refs[n_in + n_ex:n_in + n_ex + n_out]
        accs = refs[n_in + n_ex + n_out:]
        ids = tuple(pl.program_id(a) for a in range(len(grid)))
        k = ids[-1]
        loaded = {}

        def operand(i):
            if i not in loaded:
                v = in_refs[i][...]
                if i in pre:
                    v = pre[i](v)
                loaded[i] = v.astype(BF)
            return loaded[i]

        def product(ia, ib, dims):
            return lax.dot_general(operand(ia), operand(ib), (dims, ((), ())), preferred_element_type=F32)

        if nk == 1:
            sums = [None] * n_acc
            for ia, ib, dims, ai in prods:
                d = product(ia, ib, dims)
                sums[ai] = d if sums[ai] is None else sums[ai] + d
            for o_ref, v in zip(out_refs, epi(ids, sums, ex_refs)):
                o_ref[...] = v.astype(o_ref.dtype)
            return

        @pl.when(k == 0)
        def _():
            for a in accs:
                a[...] = jnp.zeros(a.shape, F32)

        for ia, ib, dims, ai in prods:
            accs[ai][...] += product(ia, ib, dims)

        @pl.when(k == nk - 1)
        def _():
            vals = epi(ids, [a[...] for a in accs], ex_refs)
            for o_ref, v in zip(out_refs, vals):
                o_ref[...] = v.astype(o_ref.dtype)

    return hosted_call(
        body, name=name, grid=grid, in_specs=[s for _, s in ins] + [s for _, s in extras],
        out_specs=[s for _, s in outs], out_shape=[o for o, _ in outs],
        scratch_shapes=[pltpu.VMEM(s, F32) for s in acc_shapes] if nk > 1 else [],
        args=[a for a, _ in ins] + [a for a, _ in extras], comm=comm)


NN = ((1,), (0,))
NT = ((1,), (1,))
TN = ((0,), (0,))


class Dims:
    def __init__(self, S, C, D, F4, W4, NS, LB):
        self.S, self.C, self.D, self.F4, self.W4, self.NS, self.LB = S, C, D, F4, W4, NS, LB
        self.T = S + C
        self.DFF = F4 * NS
        self.INW = W4 * NS
        self.NQ = D // HEAD_DIM
        self.KVW = (self.INW - 5 * D) // 2
        self.NKV = self.KVW // HEAD_DIM
        self.G = self.NQ // self.NKV
        self.OFF_K = D
        self.OFF_V = D + self.KVW
        self.OFF_LX = D + 2 * self.KVW
        self.OFF_LG = self.OFF_LX + D
        self.OFF_GA = self.OFF_LG + D
        self.OFF_GL = self.OFF_GA + D
        self.bm = _pick(C, [256, 128, 64, 32, 16, 8])
        self.nCb = C // self.bm
        self.nTb = self.T // self.bm
        self.nSb = S // self.bm
        self.mT = _pick(self.T, [544, 512, 384, 256, 128])
        self.mS = _pick(S, [512, 256, 128])
        self.kT = _pick(self.T, [1088, 1024, 768, 544, 512, 384, 256, 128])
        self.kS = _pick(S, [1024, 512, 256, 128])
        self.cw = _pick(D, [1024, 512, 256, 128]) if (self.OFF_LX % 1024 == 0 and D % 1024 == 0) else _pick(
            self.OFF_LX, [512, 256, 128])
        self.nsub = 2 if (W4 % 256 == 0 and W4 >= 512) else 1
        self.wb = W4 // self.nsub
        self.LBD = D // LB
        self.bq = _pick(C, [256, 128]) if S % _pick(C, [256, 128]) == 0 else 128


def rope_tables(dm):
    rows = dm.S // GRID_W
    row = jnp.repeat(jnp.arange(rows, dtype=F32), GRID_W)
    col = jnp.tile(jnp.arange(GRID_W, dtype=F32), rows)
    axis_dims = HEAD_DIM // 2
    freqs = ROPE_THETA ** (-jnp.arange(0, axis_dims, 2, dtype=F32) / axis_dims)
    ang = jnp.concatenate([row[:, None] * freqs, col[:, None] * freqs], axis=-1)
    cos = jnp.repeat(jnp.cos(ang), 2, axis=-1)
    sin = jnp.repeat(jnp.sin(ang), 2, axis=-1)
    sign = jnp.tile(jnp.array([-1.0, 1.0], F32), HEAD_DIM // 2)
    sin = sin * sign
    cos = jnp.concatenate([jnp.ones((dm.C, HEAD_DIM), F32), cos], axis=0)
    sin = jnp.concatenate([jnp.zeros((dm.C, HEAD_DIM), F32), sin], axis=0)
    return cos, sin


def _pair_swap(y):
    lane = lax.broadcasted_iota(jnp.int32, y.shape, 1)
    nxt = pltpu.roll(y, y.shape[1] - 1, 1)
    prv = pltpu.roll(y, 1, 1)
    return jnp.where((lane & 1) == 0, nxt, prv)


def normmod_fwd(name, dm, x, norm_g3, stage, modv, rows_T):
    D, bm = dm.D, dm.bm
    nb = dm.nTb if rows_T else dm.nSb
    typ = (lambda i: jnp.where(i < dm.nCb, 0, 1)) if rows_T else (lambda i: 1)

    def fn(ids, x_ref, g_ref, sh_ref, sc_ref):
        xv = x_ref[...]
        r = lax.rsqrt(jnp.mean(xv * xv, axis=-1, keepdims=True) + EPS)
        n = xv * r * g_ref[...]
        return [n * (1.0 + sc_ref[...]) + sh_ref[...]]

    return ew_call(
        name, (nb,), fn,
        [(x, pl.BlockSpec((bm, D), lambda i: (i, 0))),
         (norm_g3, pl.BlockSpec((None, 1, D), lambda i: (stage, 0, 0))),
         (modv, pl.BlockSpec((None, None, 1, D), lambda i: (typ(i), 3 * stage, 0, 0))),
         (modv, pl.BlockSpec((None, None, 1, D), lambda i: (typ(i), 3 * stage + 1, 0, 0)))],
        [(_sds(x.shape, BF), pl.BlockSpec((bm, D), lambda i: (i, 0)), False)])[0]


def normmod_concat_fwd(name, dm, ctx, x, norm_g3, modv):
    D, bm, nCb = dm.D, dm.bm, dm.nCb
    typ = lambda i: jnp.where(i < nCb, 0, 1)

    def fn(ids, c_ref, x_ref, g_ref, sh_ref, sc_ref):
        xv = jnp.where(ids[0] < nCb, c_ref[...], x_ref[...])
        r = lax.rsqrt(jnp.mean(xv * xv, axis=-1, keepdims=True) + EPS)
        n = xv * r * g_ref[...]
        return [n * (1.0 + sc_ref[...]) + sh_ref[...], xv]

    row = pl.BlockSpec((bm, D), lambda i: (i, 0))
    return ew_call(
        name, (dm.nTb,), fn,
        [(ctx, pl.BlockSpec((bm, D), lambda i: (jnp.minimum(i, nCb - 1), 0))),
         (x, pl.BlockSpec((bm, D), lambda i: (jnp.maximum(i - nCb, 0), 0))),
         (norm_g3, pl.BlockSpec((None, 1, D), lambda i: (0, 0, 0))),
         (modv, pl.BlockSpec((None, None, 1, D), lambda i: (typ(i), 0, 0, 0))),
         (modv, pl.BlockSpec((None, None, 1, D), lambda i: (typ(i), 1, 0, 0)))],
        [(_sds((dm.T, D), BF), row, False), (_sds((dm.T, D), F32), row, False)])


def normmod_bwd(name, dm, dh, x, dres, norm_g3, stage, modv, rows_T, dres_lat_only, out_lat_only=False):
    D, bm = dm.D, dm.bm
    nb = dm.nTb if rows_T else dm.nSb
    nCb = dm.nCb
    typ = (lambda i: jnp.where(i < nCb, 0, 1)) if rows_T else (lambda i: 1)
    if dres_lat_only:
        dres_map = lambda i: (jnp.maximum(i - nCb, 0), 0)
    else:
        dres_map = lambda i: (i, 0)

    def fn(ids, dh_ref, x_ref, dres_ref, g_ref, sc_ref):
        i = ids[0]
        xv = x_ref[...]
        dhv = dh_ref[...].astype(F32)
        r = lax.rsqrt(jnp.mean(xv * xv, axis=-1, keepdims=True) + EPS)
        xn = xv * r
        g = g_ref[...]
        n = xn * g
        dn = dhv * (1.0 + sc_ref[...])
        dxn = dn * g
        dx = r * (dxn - xn * jnp.mean(dxn * xn, axis=-1, keepdims=True))
        dresv = dres_ref[...]
        if dres_lat_only:
            dresv = jnp.where(i >= nCb, dresv, 0.0)
        dsh = jnp.sum(dhv, axis=0, keepdims=True)
        dsc = jnp.sum(dhv * n, axis=0, keepdims=True)
        dg = jnp.sum(dn * xn, axis=0, keepdims=True)
        return [dx + dresv, dsh, dsc, dg]

    if rows_T:
        first = lambda ids: (ids[0] == 0) | (ids[0] == nCb)
    else:
        first = lambda ids: ids[0] == 0
    acc = (_sds((2, 1, D), F32), pl.BlockSpec((None, 1, D), lambda i: (typ(i), 0, 0)), True)
    return ew_call(
        name, (nb,), fn,
        [(dh, pl.BlockSpec((bm, D), lambda i: (i, 0))),
         (x, pl.BlockSpec((bm, D), lambda i: (i, 0))),
         (dres, pl.BlockSpec((bm, D), dres_map)),
         (norm_g3, pl.BlockSpec((None, 1, D), lambda i: (stage, 0, 0))),
         (modv, pl.BlockSpec((None, None, 1, D), lambda i: (typ(i), 3 * stage + 1, 0, 0)))],
        [(_sds((dm.S, D) if out_lat_only else x.shape, F32),
          pl.BlockSpec((bm, D), (lambda i: (jnp.maximum(i - nCb, 0), 0)) if out_lat_only else (lambda i: (i, 0))), False),
         acc, acc, acc], first=first)


def gate_bwd(name, dm, dx, f, modv, gidx, scale, rows_T):
    D, bm = dm.D, dm.bm
    nb = dm.nTb if rows_T else dm.nSb
    nCb = dm.nCb
    typ = (lambda i: jnp.where(i < nCb, 0, 1)) if rows_T else (lambda i: 1)

    def fn(ids, dx_ref, f_ref, g_ref):
        dxv = dx_ref[...]
        return [scale * g_ref[...] * dxv, jnp.sum(scale * f_ref[...].astype(F32) * dxv, axis=0, keepdims=True)]

    if rows_T:
        first = lambda ids: (ids[0] == 0) | (ids[0] == nCb)
    else:
        first = lambda ids: ids[0] == 0
    return ew_call(
        name, (nb,), fn,
        [(dx, pl.BlockSpec((bm, D), lambda i: (i, 0))),
         (f, pl.BlockSpec((bm, D), lambda i: (i, 0))),
         (modv, pl.BlockSpec((None, None, 1, D), lambda i: (typ(i), gidx, 0, 0)))],
        [(_sds(dx.shape, BF), pl.BlockSpec((bm, D), lambda i: (i, 0)), False),
         (_sds((2, 1, D), F32), pl.BlockSpec((None, 1, D), lambda i: (typ(i), 0, 0)), True)], first=first)


def ffn_fwd(name, dm, h, xres, wg, wu, wd, modv, gidx, rows_T, comm_up=None, comm_down=None):
    D, F4, NS = dm.D, dm.F4, dm.NS
    M = h.shape[0]
    bm = dm.mT if rows_T else dm.mS
    C = dm.C

    def epi_up(ids, accs, ex):
        a, u = accs
        return [a, u, a * _sig(a) * u]

    hspec = pl.BlockSpec((bm, D), lambda j, i, k: (i, 0))
    wspec = pl.BlockSpec((None, D, F4), lambda j, i, k: (j, 0, 0))
    ospec = pl.BlockSpec((bm, F4), lambda j, i, k: (i, j))
    res = fused_mm(
        name + "_up", (NS, M // bm, 1), [(h, hspec), (wg, wspec), (wu, wspec)],
        [(0, 1, NN, 0), (0, 2, NN, 1)], [(bm, F4), (bm, F4)], epi_up,
        [(_sds((M, dm.DFF), BF), ospec)] * 3, comm=comm_up)
    (a, u, s), land_up = res if comm_up is not None else (res, None)

    bn = _pick(D, [1024, 512, 256, 128])

    def epi_dn(ids, accs, ex):
        f = accs[0]
        if rows_T:
            row = ids[0] * bm + lax.broadcasted_iota(jnp.int32, (bm, 1), 0)
            gate = jnp.where(row < C, ex[1][...], ex[2][...])
        else:
            gate = ex[2][...]
        return [ex[0][...] + FFN_RES * gate * f, f]

    gspec = lambda t: pl.BlockSpec((None, None, 1, bn), lambda i, j, k: (t, gidx, 0, j))
    res = fused_mm(
        name + "_down", (M // bm, D // bn, NS),
        [(s, pl.BlockSpec((bm, F4), lambda i, j, k: (i, k))),
         (wd, pl.BlockSpec((None, F4, bn), lambda i, j, k: (k, 0, j)))],
        [(0, 1, NN, 0)], [(bm, bn)], epi_dn,
        [(_sds((M, D), F32), pl.BlockSpec((bm, bn), lambda i, j, k: (i, j))),
         (_sds((M, D), BF), pl.BlockSpec((bm, bn), lambda i, j, k: (i, j)))],
        extras=[(xres, pl.BlockSpec((bm, bn), lambda i, j, k: (i, j))), (modv, gspec(0)), (modv, gspec(1))],
        comm=comm_down)
    (xo, f), land_down = res if comm_down is not None else (res, None)
    return xo, a, u, s, f, land_up, land_down


def ffn_bwd(name, dm, df, h, a, u, s, wg, wu, wd, rows_T, comms=None):
    comms = comms or {}
    landed, made = {}, {}

    def run(key, *args, **kw):
        comm = comms[key](landed, made) if key in comms else None
        res = fused_mm(*args, comm=comm, **kw)
        if comm is not None:
            res, landed[key] = res
        return res

    D, F4, NS = dm.D, dm.F4, dm.NS
    M = h.shape[0]
    bm = dm.mT if rows_T else dm.mS
    bkr = dm.kT if rows_T else dm.kS

    def epi_ds(ids, accs, ex):
        ds = accs[0]
        av = ex[0][...].astype(F32)
        uv = ex[1][...].astype(F32)
        sg = _sig(av)
        return [ds * uv * (sg * (1.0 + av * (1.0 - sg))), ds * av * sg]

    ospec = pl.BlockSpec((bm, F4), lambda j, i, k: (i, j))
    da, du = run(
        'ds', name + "_ds", (NS, M // bm, 1),
        [(df, pl.BlockSpec((bm, D), lambda j, i, k: (i, 0))),
         (wd, pl.BlockSpec((None, F4, D), lambda j, i, k: (j, 0, 0)))],
        [(0, 1, NT, 0)], [(bm, F4)], epi_ds, [(_sds((M, dm.DFF), BF), ospec)] * 2,
        extras=[(a, ospec), (u, ospec)])

    ident = lambda ids, accs, ex: list(accs)
    bn = _pick(D, [1024, 512, 256, 128])
    dwd = run(
        'dwd', name + "_dwd", (NS, D // bn, M // bkr),
        [(s, pl.BlockSpec((bkr, F4), lambda i, j, k: (k, i))),
         (df, pl.BlockSpec((bkr, bn), lambda i, j, k: (k, j)))],
        [(0, 1, TN, 0)], [(F4, bn)], ident,
        [(_sds((NS, F4, D), BF), pl.BlockSpec((None, F4, bn), lambda i, j, k: (i, 0, j)))])[0]
    made['dwd'] = dwd

    dwg, dwu = run(
        'dwgu', name + "_dwgu", (D // bn, NS, M // bkr),
        [(h, pl.BlockSpec((bkr, bn), lambda i, j, k: (k, i))),
         (da, pl.BlockSpec((bkr, F4), lambda i, j, k: (k, j))),
         (du, pl.BlockSpec((bkr, F4), lambda i, j, k: (k, j)))],
        [(0, 1, TN, 0), (0, 2, TN, 1)], [(bn, F4), (bn, F4)], ident,
        [(_sds((NS, D, F4), BF), pl.BlockSpec((None, bn, F4), lambda i, j, k: (j, i, 0)))] * 2)
    made['dwg'], made['dwu'] = dwg, dwu

    dh = run(
        'dh', name + "_dh", (M // bm, D // bn, NS),
        [(da, pl.BlockSpec((bm, F4), lambda i, j, k: (i, k))),
         (wg, pl.BlockSpec((None, bn, F4), lambda i, j, k: (k, j, 0))),
         (du, pl.BlockSpec((bm, F4), lambda i, j, k: (i, k))),
         (wu, pl.BlockSpec((None, bn, F4), lambda i, j, k: (k, j, 0)))],
        [(0, 1, NT, 0), (2, 3, NT, 0)], [(bm, bn)], ident,
        [(_sds((M, D), F32), pl.BlockSpec((bm, bn), lambda i, j, k: (i, j)))])[0]
    return dh, dwg, dwu, dwd, landed


def qk_prep(dm, P, gq, gk, cosf, sinf):
    D, KVW, bm = dm.D, dm.KVW, dm.bm

    def head_norm_rope(xh, g, c, s):
        r = lax.rsqrt(jnp.mean(xh * xh, axis=-1, keepdims=True) + EPS)
        y = xh * r * g
        return y * c + _pair_swap(y) * s

    def fn(ids, q_ref, k_ref, v_ref, gq_ref, gk_ref, c_ref, s_ref):
        c, s = c_ref[...], s_ref[...]
        qs = [head_norm_rope(q_ref[:, h * HEAD_DIM:(h + 1) * HEAD_DIM], gq_ref[...], c, s) for h in range(dm.NQ)]
        ks = [head_norm_rope(k_ref[:, h * HEAD_DIM:(h + 1) * HEAD_DIM], gk_ref[...], c, s) for h in range(dm.NKV)]
        return [jnp.concatenate(qs, axis=1), jnp.concatenate(ks, axis=1), v_ref[...]]

    hspec = pl.BlockSpec((bm, HEAD_DIM), lambda i: (i, 0))
    vec = pl.BlockSpec((1, HEAD_DIM), lambda i: (0, 0))
    return ew_call(
        "qk_prep", (dm.nTb,), fn,
        [(P, pl.BlockSpec((bm, D), lambda i: (i, 0))),
         (P, pl.BlockSpec((bm, KVW), lambda i: (i, dm.OFF_K // KVW))),
         (P, pl.BlockSpec((bm, KVW), lambda i: (i, dm.OFF_V // KVW))),
         (gq, vec), (gk, vec), (cosf, hspec), (sinf, hspec)],
        [(_sds((dm.T, D), BF), pl.BlockSpec((bm, D), lambda i: (i, 0)), False),
         (_sds((dm.T, KVW), BF), pl.BlockSpec((bm, KVW), lambda i: (i, 0)), False),
         (_sds((dm.T, KVW), BF), pl.BlockSpec((bm, KVW), lambda i: (i, 0)), False)])


def qk_prep_bwd(dm, dq, dk, dv, P, gq, gk, cosf, sinf, dP):
    D, KVW, bm, nCb = dm.D, dm.KVW, dm.bm, dm.nCb
    W = D + 2 * KVW

    def head_bwd(d, xh, g, c, s):
        dy = d * c - _pair_swap(d) * s
        r = lax.rsqrt(jnp.mean(xh * xh, axis=-1, keepdims=True) + EPS)
        xn = xh * r
        dg = jnp.sum(dy * xn, axis=0, keepdims=True)
        dxn = dy * g
        return r * (dxn - xn * jnp.mean(dxn * xn, axis=-1, keepdims=True)), dg

    def fn(ids, dq_ref, dk_ref, dv_ref, q_ref, k_ref, gq_ref, gk_ref, c_ref, s_ref, dp_any):
        i = ids[0]
        c, s = c_ref[...], s_ref[...]
        lat = i >= nCb
        outs, dgq = [], jnp.zeros((1, HEAD_DIM), F32)
        for h in range(dm.NQ):
            sl = slice(h * HEAD_DIM, (h + 1) * HEAD_DIM)
            d = jnp.where(lat, dq_ref[:, sl], 0.0)
            dx, dg = head_bwd(d, q_ref[:, sl], gq_ref[...], c, s)
            outs.append(dx)
            dgq = dgq + dg
        dgk = jnp.zeros((1, HEAD_DIM), F32)
        for h in range(dm.NKV):
            sl = slice(h * HEAD_DIM, (h + 1) * HEAD_DIM)
            dx, dg = head_bwd(dk_ref[:, sl], k_ref[:, sl], gk_ref[...], c, s)
            outs.append(dx)
            dgk = dgk + dg
        outs.append(dv_ref[...])
        return [jnp.concatenate(outs, axis=1), dgq, dgk]

    hspec = pl.BlockSpec((bm, HEAD_DIM), lambda i: (i, 0))
    vec = pl.BlockSpec((1, HEAD_DIM), lambda i: (0, 0))
    return ew_call(
        "qk_prep_bwd", (dm.nTb,), fn,
        [(dq, pl.BlockSpec((bm, D), lambda i: (jnp.maximum(i - nCb, 0), 0))),
         (dk, pl.BlockSpec((bm, KVW), lambda i: (i, 0))),
         (dv, pl.BlockSpec((bm, KVW), lambda i: (i, 0))),
         (P, pl.BlockSpec((bm, D), lambda i: (i, 0))),
         (P, pl.BlockSpec((bm, KVW), lambda i: (i, dm.OFF_K // KVW))),
         (gq, vec), (gk, vec), (cosf, hspec), (sinf, hspec), (dP, ANY)],
        [(_sds(dP.shape, BF), pl.BlockSpec((bm, W), lambda i: (i, 0)), False),
         (_sds((1, HEAD_DIM), F32), vec, True), (_sds((1, HEAD_DIM), F32), vec, True)],
        first=lambda ids: ids[0] == 0, aliases={9: 0})


def _softmax_numerators(s_ref, eb_ref, mb_ref, scale):
    rows, T = s_ref.shape
    m = jnp.max(s_ref[...], axis=-1, keepdims=True)
    mb_ref[...] = jnp.broadcast_to(m, (rows, HEAD_DIM))
    lacc = jnp.zeros((rows, HEAD_DIM), F32)
    for c in range(T // HEAD_DIM):
        cs = slice(c * HEAD_DIM, (c + 1) * HEAD_DIM)
        e = jnp.exp((s_ref[:, cs] - mb_ref[...]) * scale)
        lacc = lacc + e
        eb_ref[:, cs] = e.astype(BF)
    return jnp.sum(lacc, axis=-1, keepdims=True)


def attention_fwd(dm, qr, kr, vb, comm=None):
    S, T, D, G, nCb = dm.S, dm.T, dm.D, dm.G, dm.nCb
    bq = dm.bq
    off = dm.C // bq
    scale = HEAD_DIM ** -0.5
    GW = G * HEAD_DIM

    def body(q_ref, k_ref, v_ref, o_ref):
        k = k_ref[...]
        v = v_ref[...]
        for h in range(G):
            sl = slice(h * HEAD_DIM, (h + 1) * HEAD_DIM)
            s = lax.dot_general(q_ref[:, sl], k, (NT, ((), ())), preferred_element_type=F32) * scale
            m = jnp.max(s, axis=-1, keepdims=True)
            p = jnp.exp(s - m)
            l = jnp.sum(p, axis=-1, keepdims=True)
            o = lax.dot_general(p.astype(BF), v, (NN, ((), ())), preferred_element_type=F32)
            o_ref[:, sl] = o / l

    return hosted_call(
        body, grid=(dm.NKV, S // bq), name="attn_fwd",
        in_specs=[pl.BlockSpec((bq, GW), lambda g, i: (i + off, g)),
                  pl.BlockSpec((T, HEAD_DIM), lambda g, i: (0, g)),
                  pl.BlockSpec((T, HEAD_DIM), lambda g, i: (0, g))],
        out_specs=[pl.BlockSpec((bq, GW), lambda g, i: (i, g))],
        out_shape=[_sds((S, D), F32)], args=[qr, kr, vb], comm=comm)


def attention_bwd(dm, qr, kr, vb, attn, dattn, comm=None):
    S, T, D, G = dm.S, dm.T, dm.D, dm.G
    bq = dm.bq
    off = dm.C // bq
    scale = HEAD_DIM ** -0.5
    GW = G * HEAD_DIM

    def body(q_ref, k_ref, v_ref, o_ref, do_ref, dq_ref, dk_ref, dv_ref, s_ref, eb_ref, mb_ref):
        i = pl.program_id(1)

        @pl.when(i == 0)
        def _():
            dk_ref[...] = jnp.zeros(dk_ref.shape, F32)
            dv_ref[...] = jnp.zeros(dv_ref.shape, F32)

        k = k_ref[...]
        v = v_ref[...]
        for h in range(G):
            sl = slice(h * HEAD_DIM, (h + 1) * HEAD_DIM)
            q = q_ref[:, sl]
            do = do_ref[:, sl]
            dof = do.astype(F32)
            s_ref[...] = lax.dot_general(q, k, (NT, ((), ())), preferred_element_type=F32)
            l = _softmax_numerators(s_ref, eb_ref, mb_ref, scale)
            rl = 1.0 / l
            dv_ref[...] += lax.dot_general(eb_ref[...], (dof * rl).astype(BF), (TN, ((), ())), preferred_element_type=F32)
            s_ref[...] = lax.dot_general(do, v, (NT, ((), ())), preferred_element_type=F32)
            delta = jnp.sum(dof * o_ref[:, sl], axis=-1, keepdims=True)
            mb_ref[...] = jnp.broadcast_to(delta, (bq, HEAD_DIM))
            for c in range(T // HEAD_DIM):
                cs = slice(c * HEAD_DIM, (c + 1) * HEAD_DIM)
                eb_ref[:, cs] = (eb_ref[:, cs].astype(F32) * (s_ref[:, cs] - mb_ref[...])).astype(BF)
            w = scale * rl
            dq_ref[:, sl] = lax.dot_general(eb_ref[...], k, (NN, ((), ())), preferred_element_type=F32) * w
            dk_ref[...] += lax.dot_general(eb_ref[...], (q.astype(F32) * w).astype(BF), (TN, ((), ())),
                                           preferred_element_type=F32)

    return hosted_call(
        body, grid=(dm.NKV, S // bq), name="attn_bwd",
        in_specs=[pl.BlockSpec((bq, GW), lambda g, i: (i + off, g)),
                  pl.BlockSpec((T, HEAD_DIM), lambda g, i: (0, g)),
                  pl.BlockSpec((T, HEAD_DIM), lambda g, i: (0, g)),
                  pl.BlockSpec((bq, GW), lambda g, i: (i, g)),
                  pl.BlockSpec((bq, GW), lambda g, i: (i + off, g))],
        out_specs=[pl.BlockSpec((bq, GW), lambda g, i: (i, g)),
                   pl.BlockSpec((T, HEAD_DIM), lambda g, i: (0, g)),
                   pl.BlockSpec((T, HEAD_DIM), lambda g, i: (0, g))],
        out_shape=[_sds((S, D), F32), _sds((T, dm.KVW), F32), _sds((T, dm.KVW), F32)],
        args=[qr, kr, vb, attn, dattn],
        scratch_shapes=[pltpu.VMEM((bq, T), F32), pltpu.VMEM((bq, T), BF), pltpu.VMEM((bq, HEAD_DIM), F32)], comm=comm)


def _conv_taps(dm, lx, masks_only=False):
    T, C = dm.T, dm.C
    t = lax.broadcasted_iota(jnp.int32, (T, 1), 0)
    valid = [(t >= 2) & ((t < C) | (t >= C + 2)), (t >= 1) & ((t < C) | (t >= C + 1)), None,
             (t != C - 1) & (t != T - 1)]
    shifts = [2, 1, 0, T - 1]
    taps = []
    for k in range(4):
        if k == 2:
            taps.append(lx)
        else:
            taps.append(jnp.where(valid[k], pltpu.roll(lx, shifts[k], 0), 0.0))
    return taps


def _scan_tiles(dm, chains):
    T, C = dm.T, dm.C
    nT, nC = T // 8, C // 8
    row = lax.broadcasted_iota(jnp.int32, (8, HEAD_DIM), 0)

    def tile_of(i, asc, split):
        if not split:
            return i if asc else nT - 1 - i
        if asc:
            return jnp.where(i < nT - nC, nC + i, i - (nT - nC))
        return jnp.where(i < nC, nC - 1 - i, nT - 1 - (i - nC))

    def step(i, carry, asc, split, a_ref, u_ref, out_ref, mode):
        off = pl.multiple_of(tile_of(i, asc, split) * 8, 8)
        a = a_ref[pl.ds(off, 8), :]
        b = u_ref[pl.ds(off, 8), :]
        if mode == 'lam':
            if asc:
                coef = jnp.where(row == 0, 1.0, pltpu.roll(a, 1, 0))
            else:
                coef = jnp.where(row == 7, 1.0, pltpu.roll(a, 7, 0))
        else:
            coef = a
        A, B = coef, b
        for d in (1, 2, 4):
            if asc:
                ok = row >= d
                A_sh = jnp.where(ok, pltpu.roll(A, d, 0), 1.0)
                B_sh = jnp.where(ok, pltpu.roll(B, d, 0), 0.0)
            else:
                ok = row < 8 - d
                A_sh = jnp.where(ok, pltpu.roll(A, 8 - d, 0), 1.0)
                B_sh = jnp.where(ok, pltpu.roll(B, 8 - d, 0), 0.0)
            B = B + A * B_sh
            A = A * A_sh
        h = A * carry + B
        out_ref[pl.ds(off, 8), :] = h
        last = h[7:8, :] if asc else h[0:1, :]
        if mode == 'lam':
            last = last * (a[7:8, :] if asc else a[0:1, :])
        return jnp.broadcast_to(last, (8, HEAD_DIM))

    def body(i, carries):
        return tuple(step(i, c_, *ch) for c_, ch in zip(carries, chains))

    lax.fori_loop(0, nT, body, tuple(jnp.zeros((8, HEAD_DIM), F32) for _ in chains))


def _lru_gates(xc, wa, ba, wx, bx, sp):
    xb = xc.astype(BF)
    r = _sig(jnp.dot(xb, wa, preferred_element_type=F32) + ba)
    i = _sig(jnp.dot(xb, wx, preferred_element_type=F32) + bx)
    a = jnp.exp(-LRU_C * r * sp)
    m = jnp.sqrt(1.0 - a * a)
    return r, i, a, m


def lru_fwd(dm, P, conv_w, conv_b, wa, ba, wx, bx, sp):
    T, D, LB = dm.T, dm.D, dm.LB
    W = dm.LBD
    R = _pick(T, [272, 256, 128, 64, 8])
    lxb = dm.OFF_LX // W

    def body(lx_ref, cw_ref, cb_ref, wa_ref, ba_ref, wx_ref, bx_ref, sp_ref, hf_ref, hb_ref, xc_ref, a_ref):
        taps = _conv_taps(dm, lx_ref[...])
        xc = cb_ref[...]
        for k in range(4):
            xc = xc + taps[k] * cw_ref[k:k + 1, :]
        xc_ref[...] = xc
        h_refs = (hf_ref, hb_ref)

        def chunk(ci, _):
            off = pl.multiple_of(ci * R, 8)
            x = xc_ref[pl.ds(off, R), :]
            for d in range(2):
                r, i, a, m = _lru_gates(x, wa_ref[d].astype(BF), ba_ref[d], wx_ref[d].astype(BF), bx_ref[d], sp_ref[d])
                a_ref[d, pl.ds(off, R), :] = a
                h_refs[d][pl.ds(off, R), :] = m * i * x
            return 0

        lax.fori_loop(0, T // R, chunk, 0)
        _scan_tiles(dm, [(True, False, a_ref.at[0], hf_ref, hf_ref, 'h'), (False, True, a_ref.at[1], hb_ref, hb_ref, 'h')])

    strip = lambda j: (0, j)
    vec = pl.BlockSpec((2, 1, W), lambda j: (0, 0, j))
    mat = pl.BlockSpec((2, None, W, W), lambda j: (0, j, 0, 0))
    return pl.pallas_call(
        body, grid=(LB,), name="lru_fwd",
        in_specs=[pl.BlockSpec((T, W), lambda j: (0, lxb + j)),
                  pl.BlockSpec((4, W), strip), pl.BlockSpec((1, W), strip), mat, vec, mat, vec, vec],
        out_specs=[pl.BlockSpec((T, W), strip)] * 2, out_shape=[_sds((T, D), F32)] * 2,
        scratch_shapes=[pltpu.VMEM((T, W), F32), pltpu.VMEM((2, T, W), F32)], compiler_params=_cparams(),
    )(P, conv_w, conv_b, wa, ba, wx, bx, sp)


def lru_bwd(dm, P, dh, hf, hb, conv_w, conv_b, wa, ba, wx, bx, sp, sg, dP):
    T, C, D, LB = dm.T, dm.C, dm.D, dm.LB
    W = dm.LBD
    R = _pick(T, [272, 256, 128, 64, 8])
    lxb = dm.OFF_LX // W

    def body(lx_ref, dh_ref, hf_ref, hb_ref, cw_ref, cb_ref, wa_ref, ba_ref, wx_ref, bx_ref, sp_ref, sg_ref, _dp_any,
             dlx_ref, dwa_ref, dba_ref, dwx_ref, dbx_ref, dlam_ref, dcw_ref, dcb_ref,
             xc_ref, a_ref, lam_ref, hp_ref, dxc_ref):
        lx = lx_ref[...]
        taps = _conv_taps(dm, lx)
        xc = cb_ref[...]
        for k in range(4):
            xc = xc + taps[k] * cw_ref[k:k + 1, :]
        xc_ref[...] = xc

        def gates(d, x):
            return _lru_gates(x, wa_ref[d].astype(BF), ba_ref[d], wx_ref[d].astype(BF), bx_ref[d], sp_ref[d])

        def chunk_a(ci, _):
            off = pl.multiple_of(ci * R, 8)
            x = xc_ref[pl.ds(off, R), :]
            for d in range(2):
                a_ref[d, pl.ds(off, R), :] = gates(d, x)[2]
            return 0

        lax.fori_loop(0, T // R, chunk_a, 0)
        _scan_tiles(dm, [(False, False, a_ref.at[0], dh_ref, lam_ref.at[0], 'lam'),
                         (True, True, a_ref.at[1], dh_ref, lam_ref.at[1], 'lam')])
        t = lax.broadcasted_iota(jnp.int32, (T, 1), 0)
        hp_ref[0] = jnp.where(t == 0, 0.0, pltpu.roll(hf_ref[...], 1, 0))
        hv = hb_ref[...]
        hp_ref[1] = jnp.where(t == C - 1, 0.0, jnp.where(t == T - 1, jnp.broadcast_to(hv[0:1, :], hv.shape),
                                                         pltpu.roll(hv, T - 1, 0)))

        def chunk_b(d):
            wa_, wx_ = wa_ref[d].astype(BF), wx_ref[d].astype(BF)

            def run(ci, carry):
                dwa, dwx, dba, dbx, dlam = carry
                off = pl.multiple_of(ci * R, 8)
                x = xc_ref[pl.ds(off, R), :]
                r, i, a, m = gates(d, x)
                lam = lam_ref[d, pl.ds(off, R), :]
                da = lam * hp_ref[d, pl.ds(off, R), :] - lam * (i * x) * a / m
                dloga = da * a
                dza = dloga * (-LRU_C) * sp_ref[d] * r * (1.0 - r)
                dzx = lam * m * x * i * (1.0 - i)
                dzab, dzxb = dza.astype(BF), dzx.astype(BF)
                xb = x.astype(BF)
                dxc = lam * m * i
                dxc = dxc + lax.dot_general(dzab, wa_, (NT, ((), ())), preferred_element_type=F32)
                dxc = dxc + lax.dot_general(dzxb, wx_, (NT, ((), ())), preferred_element_type=F32)
                if d == 0:
                    dxc_ref[pl.ds(off, R), :] = dxc
                else:
                    dxc_ref[pl.ds(off, R), :] += dxc
                dwa = dwa + lax.dot_general(xb, dzab, (TN, ((), ())), preferred_element_type=F32)
                dwx = dwx + lax.dot_general(xb, dzxb, (TN, ((), ())), preferred_element_type=F32)
                dba = dba + jnp.sum(dza, axis=0, keepdims=True)
                dbx = dbx + jnp.sum(dzx, axis=0, keepdims=True)
                dlam = dlam + jnp.sum(dloga * LRU_C * r, axis=0, keepdims=True)
                return dwa, dwx, dba, dbx, dlam

            z = jnp.zeros((W, W), F32)
            zv = jnp.zeros((1, W), F32)
            dwa, dwx, dba, dbx, dlam = lax.fori_loop(0, T // R, run, (z, z, zv, zv, zv))
            dwa_ref[d] = dwa
            dwx_ref[d] = dwx
            dba_ref[d] = dba
            dbx_ref[d] = dbx
            dlam_ref[d] = dlam * sg_ref[d]

        chunk_b(0)
        chunk_b(1)
        dxc = dxc_ref[...]
        dcb_ref[...] = jnp.sum(dxc, axis=0, keepdims=True)
        dcw_ref[...] = jnp.concatenate([jnp.sum(dxc * taps[k], axis=0, keepdims=True) for k in range(4)], axis=0)
        valid = [(t < T - 2) & ((t >= C) | (t < C - 2)), (t < T - 1) & ((t >= C) | (t < C - 1)), None,
                 (t != 0) & (t != C)]
        shifts = [T - 2, T - 1, 0, 1]
        dlx = dxc * cw_ref[2:3, :]
        for k in (0, 1, 3):
            dlx = dlx + jnp.where(valid[k], pltpu.roll(dxc, shifts[k], 0), 0.0) * cw_ref[k:k + 1, :]
        dlx_ref[...] = dlx.astype(dlx_ref.dtype)

    strip = lambda j: (0, j)
    sspec = pl.BlockSpec((T, W), strip)
    vec = pl.BlockSpec((2, 1, W), lambda j: (0, 0, j))
    mat = pl.BlockSpec((2, None, W, W), lambda j: (0, j, 0, 0))
    ovec = pl.BlockSpec((2, 1, W), lambda j: (0, 0, j))
    return pl.pallas_call(
        body, grid=(LB,), name="lru_bwd",
        in_specs=[pl.BlockSpec((T, W), lambda j: (0, lxb + j)), sspec, sspec, sspec,
                  pl.BlockSpec((4, W), strip), pl.BlockSpec((1, W), strip), mat, vec, mat, vec, vec, vec, ANY],
        out_specs=[pl.BlockSpec((T, W), lambda j: (0, lxb + j)), mat, ovec, mat, ovec, ovec,
                   pl.BlockSpec((4, W), strip), pl.BlockSpec((1, W), strip)],
        out_shape=[_sds(dP.shape, BF), _sds((2, LB, W, W), F32), _sds((2, 1, D), F32), _sds((2, LB, W, W), F32),
                   _sds((2, 1, D), F32), _sds((2, 1, D), F32), _sds((4, D), F32), _sds((1, D), F32)],
        scratch_shapes=[pltpu.VMEM((T, W), F32), pltpu.VMEM((2, T, W), F32), pltpu.VMEM((2, T, W), F32),
                        pltpu.VMEM((2, T, W), F32), pltpu.VMEM((T, W), F32)],
        input_output_aliases={12: 0}, compiler_params=_cparams(),
    )(P, dh, hf, hb, conv_w, conv_b, wa, ba, wx, bx, sp, sg, dP)


def merge_fwd(dm, P, attn, hf, hb):
    S, D, bm, cw, nCb = dm.S, dm.D, dm.bm, dm.cw, dm.nCb

    def fn(ids, lg_ref, ga_ref, gl_ref, at_ref, hf_ref, hb_ref):
        ge, _ = _gelu(lg_ref[...])
        lru = (hf_ref[...] + hb_ref[...]) * ge
        return [_sig(ga_ref[...]) * at_ref[...] + _sig(gl_ref[...]) * lru]

    pspec = lambda off: pl.BlockSpec((bm, cw), lambda i, j: (i + nCb, off // cw + j))
    tspec = pl.BlockSpec((bm, cw), lambda i, j: (i + nCb, j))
    sspec = pl.BlockSpec((bm, cw), lambda i, j: (i, j))
    return ew_call(
        "merge_fwd", (dm.nSb, D // cw), fn,
        [(P, pspec(dm.OFF_LG)), (P, pspec(dm.OFF_GA)), (P, pspec(dm.OFF_GL)), (attn, sspec), (hf, tspec), (hb, tspec)],
        [(_sds((S, D), BF), sspec, False)])[0]


def merge_bwd(dm, dmg, P, attn, hf, hb):
    S, T, D, bm, cw, nCb = dm.S, dm.T, dm.D, dm.bm, dm.cw, dm.nCb
    nj = D // cw

    def body(dm_ref, lg_ref, ga_ref, gl_ref, at_ref, hf_ref, hb_ref, dp_ref, da_ref, dh_ref, buf, sems):
        i, j = pl.program_id(0), pl.program_id(1)
        lat = i >= nCb
        d = jnp.where(lat, dm_ref[...].astype(F32), 0.0)
        lg = lg_ref[...]
        ge, th = _gelu(lg)
        hs = hf_ref[...] + hb_ref[...]
        sa, sl = _sig(ga_ref[...]), _sig(gl_ref[...])
        at = jnp.where(lat, at_ref[...], 0.0)
        dlru = d * sl
        buf[0] = (dlru * hs * _gelu_grad(lg, th)).astype(BF)
        buf[1] = (d * at * sa * (1.0 - sa)).astype(BF)
        buf[2] = (d * hs * ge * sl * (1.0 - sl)).astype(BF)
        da_ref[...] = (d * sa).astype(BF)
        dh_ref[...] = dlru * ge
        copies = []
        for g, off in enumerate((dm.OFF_LG, dm.OFF_GA, dm.OFF_GL)):
            col = pl.multiple_of(off + j * cw, 128)
            cp = pltpu.make_async_copy(buf.at[g], dp_ref.at[pl.ds(pl.multiple_of(i * bm, 8), bm), pl.ds(col, cw)],
                                       sems.at[g])
            cp.start()
            copies.append(cp)
        for cp in copies:
            cp.wait()

    pspec = lambda off: pl.BlockSpec((bm, cw), lambda i, j: (i, off // cw + j))
    tspec = pl.BlockSpec((bm, cw), lambda i, j: (i, j))
    lspec = pl.BlockSpec((bm, cw), lambda i, j: (jnp.maximum(i - nCb, 0), j))
    return pl.pallas_call(
        body, grid=(dm.nTb, nj), name="merge_bwd",
        in_specs=[lspec, pspec(dm.OFF_LG), pspec(dm.OFF_GA), pspec(dm.OFF_GL), lspec, tspec, tspec],
        out_specs=[ANY, tspec, tspec],
        out_shape=[_sds((T, dm.INW), BF), _sds((T, D), BF), _sds((T, D), F32)],
        scratch_shapes=[pltpu.VMEM((3, bm, cw), BF), pltpu.SemaphoreType.DMA((3,))],
        compiler_params=_cparams(),
    )(dmg, P, P, P, attn, hf, hb)


def final_loss(dm, x3, gfin, target):
    S, D, bm = dm.S, dm.D, dm.bm

    def fn(ids, x_ref, g_ref, t_ref):
        xv = x_ref[...]
        g = g_ref[...]
        r = lax.rsqrt(jnp.mean(xv * xv, axis=-1, keepdims=True) + EPS)
        xn = xv * r
        err = xn * g - t_ref[...]
        loss = 0.5 * jnp.sum(jnp.mean(err * err, axis=-1, keepdims=True), axis=0, keepdims=True)
        dy = err / D
        dxn = dy * g
        dx = r * (dxn - xn * jnp.mean(dxn * xn, axis=-1, keepdims=True))
        return [jnp.broadcast_to(loss, (1, 128)), dx, jnp.sum(dy * xn, axis=0, keepdims=True)]

    row = pl.BlockSpec((bm, D), lambda i: (i, 0))
    vec = pl.BlockSpec((1, D), lambda i: (0, 0))
    return ew_call(
        "final_loss", (dm.nSb,), fn, [(x3, row), (gfin, vec), (target, row)],
        [(_sds((1, 128), F32), pl.BlockSpec((1, 128), lambda i: (0, 0)), True), (_sds((S, D), F32), row, False),
         (_sds((1, D), F32), vec, True)], first=lambda ids: ids[0] == 0)


def local_step(dm, x, ctx, target, modv, norm_g3, gfin, gq, gk, conv_w, conv_b, wa, ba, wx, bx, lam, wbuf, where):
    S, C, T, D, NS, F4, W4 = dm.S, dm.C, dm.T, dm.D, dm.NS, dm.F4, dm.W4
    Ds = D // NS
    wb, nsub = dm.wb, dm.nsub
    cosf, sinf = rope_tables(dm)
    sp = jax.nn.softplus(-lam)
    sg = jax.nn.sigmoid(-lam)
    ident = lambda ids, accs, ex: list(accs)
    mT, mS, kT, kS = dm.mT, dm.mS, dm.kT, dm.kS
    bn = _pick(D, [1024, 512, 256, 128])
    bk = _pick(D, [512, 256, 128])

    wg0, wu0, wd0 = comm_call("ag_ffn1", ag_comm([wbuf['wg0'], wbuf['wu0'], wbuf['wd0']]))
    h1, xt = normmod_concat_fwd("nm1", dm, ctx, x, norm_g3, modv)
    xt1, a1, u1, s1, f1, land, _ = ffn_fwd("ffn1", dm, h1, xt, wg0, wu0, wd0, modv, 2, True,
                                           comm_up=ag_comm([wbuf['w_in'], wbuf['w_out']]))
    w_in, w_out = land[0], land[1].reshape(D, D)
    h2 = normmod_fwd("nm2", dm, xt1, norm_g3, 1, modv, True)
    P = fused_mm(
        "w_in", (T // mT, NS * nsub, 1),
        [(h2, pl.BlockSpec((mT, D), lambda i, j, k: (i, 0))),
         (w_in, pl.BlockSpec((None, D, wb), lambda i, j, k: (j // nsub, 0, j % nsub)))],
        [(0, 1, NN, 0)], [(mT, wb)], ident,
        [(_sds((T, dm.INW), F32), pl.BlockSpec((mT, wb), lambda i, j, k: (i, j)))])[0]
    qr, kr, vb = qk_prep(dm, P, gq, gk, cosf, sinf)
    (attn,), (wg1, wu1, wd1) = attention_fwd(dm, qr, kr, vb, comm=ag_comm([wbuf['wg1'], wbuf['wu1'], wbuf['wd1']]))
    hf, hb = lru_fwd(dm, P, conv_w, conv_b, wa, ba, wx, bx, sp)
    mg = merge_fwd(dm, P, attn, hf, hb)

    def epi_o(ids, accs, ex):
        o = accs[0]
        return [ex[0][...] + ex[1][...] * o, o]

    rb, nCb = dm.bm, dm.nCb
    x2, o2 = fused_mm(
        "w_out", (D // bn, S // rb, 1),
        [(mg, pl.BlockSpec((rb, D), lambda j, i, k: (i, 0))), (w_out, pl.BlockSpec((D, bn), lambda j, i, k: (0, j)))],
        [(0, 1, NN, 0)], [(rb, bn)], epi_o,
        [(_sds((S, D), F32), pl.BlockSpec((rb, bn), lambda j, i, k: (i, j))),
         (_sds((S, D), BF), pl.BlockSpec((rb, bn), lambda j, i, k: (i, j)))],
        extras=[(xt1, pl.BlockSpec((rb, bn), lambda j, i, k: (i + nCb, j))),
                (modv, pl.BlockSpec((None, None, 1, bn), lambda j, i, k: (1, 5, 0, j)))])
    h3 = normmod_fwd("nm3", dm, x2, norm_g3, 2, modv, False)
    x3, a3, u3, s3, f3, _, _ = ffn_fwd("ffn2", dm, h3, x2, wg1, wu1, wd1, modv, 8, False)
    loss, dx3, dgfin = final_loss(dm, x3, gfin, target)

    df3, dg3 = gate_bwd("gate3", dm, dx3, f3, modv, 8, FFN_RES, False)
    dh3, dwg1, dwu1, dwd1, _ = ffn_bwd("ffn2b", dm, df3, h3, a3, u3, s3, wg1, wu1, wd1, False)
    dx2, dsh3, dsc3, dgn3 = normmod_bwd("nm3b", dm, dh3, x2, dx3, norm_g3, 2, modv, False, False)
    do2, dg2 = gate_bwd("gate2", dm, dx2, o2, modv, 5, 1.0, False)
    keep = {}

    def host_a(key, comm):
        if key == 'p1':
            (keep['dmg'],), landed = fused_mm(
                "w_out_dx", (S // mS, D // bn, 1),
                [(do2, pl.BlockSpec((mS, D), lambda i, j, k: (i, 0))),
                 (w_out, pl.BlockSpec((bn, D), lambda i, j, k: (j, 0)))],
                [(0, 1, NT, 0)], [(mS, bn)], ident,
                [(_sds((S, D), BF), pl.BlockSpec((mS, bn), lambda i, j, k: (i, j)))], comm=comm)
            return landed
        keep['dqkv'], landed = attention_bwd(dm, qr, kr, vb, attn, keep['dattn'], comm=comm)
        return landed

    gots_a = host_a('p1', rs_p1_comm([dwg1, dwu1, dwd1]))
    dmg = keep['dmg']
    dw_out = fused_mm(
        "w_out_dw", (D // bn, D // bn, S // kS),
        [(mg, pl.BlockSpec((kS, bn), lambda i, j, k: (k, i))), (do2, pl.BlockSpec((kS, bn), lambda i, j, k: (k, j)))],
        [(0, 1, TN, 0)], [(bn, bn)], ident,
        [(_sds((D, D), BF), pl.BlockSpec((bn, bn), lambda i, j, k: (i, j)))])[0]
    dP, dattn, dhs = merge_bwd(dm, dmg, P, attn, hf, hb)
    keep['dattn'] = dattn
    pairs_a = [add_pair("rs_add_" + n_, g_, got_, where)
               for n_, g_, got_ in zip(('wg1', 'wu1', 'wd1'), (dwg1, dwu1, dwd1), gots_a)]
    land_a = host_a('p2', rs_p2_comm([p_[0] for p_ in pairs_a], [p_[1] for p_ in pairs_a]))
    dq, dk, dv = keep['dqkv']
    dP, dwa, dba, dwx, dbx, dlam, dcw, dcb = lru_bwd(dm, P, dhs, hf, hb, conv_w, conv_b, wa, ba, wx, bx, sp, sg, dP)
    dP, dgq, dgk = qk_prep_bwd(dm, dq, dk, dv, P, gq, gk, cosf, sinf, dP)
    g_wg = sum_slots_into("rs_sum_wg1", land_a[0], where, None, (2, D, F4), 1)
    g_wu = sum_slots_into("rs_sum_wu1", land_a[1], where, None, (2, D, F4), 1)
    g_wd = sum_slots_into("rs_sum_wd1", land_a[2], where, None, (2, F4, D), 1)
    (dh2,), (g_wg, g_wu, g_wd) = fused_mm(
        "w_in_dx", (T // mT, D // bn, NS),
        [(dP, pl.BlockSpec((mT, W4), lambda i, j, k: (i, k))),
         (w_in, pl.BlockSpec((None, bn, W4), lambda i, j, k: (k, j, 0)))],
        [(0, 1, NT, 0)], [(mT, bn)], ident,
        [(_sds((T, D), F32), pl.BlockSpec((mT, bn), lambda i, j, k: (i, j)))],
        comm=rs_p3_comm([g_wg, g_wu, g_wd], [(0, 1, D // 2), (1, 1, D // 2), (2, 1, F4 // 2)]))
    dw_in = fused_mm(
        "w_in_dw", (D // bn, NS, T // kT),
        [(h2, pl.BlockSpec((kT, bn), lambda i, j, k: (k, i))), (dP, pl.BlockSpec((kT, W4), lambda i, j, k: (k, j)))],
        [(0, 1, TN, 0)], [(bn, W4)], ident,
        [(_sds((NS, D, W4), BF), pl.BlockSpec((None, bn, W4), lambda i, j, k: (j, i, 0)))])[0]
    dxt1, dsh2, dsc2, dgn2 = normmod_bwd("nm2b", dm, dh2, xt1, dx2, norm_g3, 1, modv, True, True)
    df1, dg1 = gate_bwd("gate1", dm, dxt1, f1, modv, 2, FFN_RES, True)

    tens_b = [dw_in, dw_out.reshape(NS, Ds, D)]

    LBD = D // dm.LB
    mats = [dwa.reshape(1, 2 * dm.LB * LBD, LBD), dwx.reshape(1, 2 * dm.LB * LBD, LBD)]

    def host_dwgu(landed, made):
        pairs = [add_pair("rs_add_" + n_, g_, got_, where) for n_, g_, got_ in zip(('w_in', 'w_out'), tens_b, landed['ds'])]
        return merge_comms([rs_p2_comm([p_[0] for p_ in pairs], [p_[1] for p_ in pairs]), rs_p1_comm([made['dwd']])])

    def host_dh(landed, made):
        g_win = sum_slots_into("rs_sum_w_in", landed['dwgu'][0], where, None, (D, W4), None)
        g_wout = sum_slots_into("rs_sum_w_out", landed['dwgu'][1], where, None, (Ds, D), None)
        part, land = add_pair("rs_add_wd0", made['dwd'], landed['dwgu'][2], where)
        return merge_comms([rs_p3_comm([g_win, g_wout], [(0, None, D // 2), (1, None, Ds // 2)]),
                            rs_p1_comm([made['dwg'], made['dwu']] + mats), rs_p2_comm([part], [land])])

    dh1, dwg0, dwu0, dwd0, landed = ffn_bwd("ffn1b", dm, df1, h1, a1, u1, s1, wg0, wu0, wd0, True,
                                            comms={'ds': lambda L, M: rs_p1_comm(tens_b), 'dwgu': host_dwgu, 'dh': host_dh})
    g_win, g_wout = landed['dh'][:2]
    late = [add_pair("rs_add_" + n_, g_, got_, where)
            for n_, g_, got_ in zip(('wg0', 'wu0', 'lru_wa', 'lru_wx'), [dwg0, dwu0] + mats, landed['dh'][2:6])]
    g_wd = sum_slots_into("rs_sum_wd0", landed['dh'][6], where, g_wd, (2, F4, D), 0)
    grad_x, dsh1, dsc1, dgn1 = normmod_bwd("nm1b", dm, dh1, xt, dxt1, norm_g3, 0, modv, True, False, out_lat_only=True)

    dmod = jnp.concatenate([dsh1, dsc1, dg1, dsh2, dsc2, _lat(dg2), _lat(dsh3), _lat(dsc3), _lat(dg3)], axis=1)
    dnorm = jnp.stack([dgn1[0, 0] + dgn1[1, 0], dgn2[0, 0] + dgn2[1, 0], dgn3[1, 0]], axis=0)
    small = dict(norm_g=dnorm, q_norm_g=dgq, k_norm_g=dgk, conv_w=dcw, conv_b=dcb,
                 lru_ba=dba.reshape(2, D), lru_bx=dbx.reshape(2, D), lru_lambda=dlam.reshape(2, D), final_norm_g=dgfin)
    reduced = dict(ffn_wg=g_wg, ffn_wu=g_wu, ffn_wd=g_wd, w_in=g_win, w_out=g_wout)
    return loss, grad_x, dmod, small, reduced, late


def _lat(v):
    return jnp.concatenate([jnp.zeros_like(v[:1]), v[1:]], axis=0)


def _me():
    return lax.axis_index("x"), lax.axis_index("y"), lax.axis_index("c")


def allgather8(name, v):
    def body(v_ref, out_ref, send_sems, recv_sems, local_sem):
        x, y, c = _me()
        me = 4 * x + 2 * y + c
        mine = pltpu.make_async_copy(v_ref, out_ref.at[me], local_sem)
        mine.start()
        copies = []
        for k in range(1, 8):
            peer = (x ^ ((k >> 2) & 1), y ^ ((k >> 1) & 1), c ^ (k & 1))
            cp = pltpu.make_async_remote_copy(src_ref=v_ref, dst_ref=out_ref.at[me], send_sem=send_sems.at[k - 1],
                                              recv_sem=recv_sems.at[k - 1], device_id=peer, device_id_type=MESH)
            cp.start()
            copies.append(cp)
        for k in range(1, 8):
            peer = (x ^ ((k >> 2) & 1), y ^ ((k >> 1) & 1), c ^ (k & 1))
            pltpu.make_async_remote_copy(src_ref=v_ref, dst_ref=out_ref.at[me ^ k], send_sem=send_sems.at[k - 1],
                                         recv_sem=recv_sems.at[k - 1], device_id=peer, device_id_type=MESH).wait_recv()
        for cp in copies:
            cp.wait_send()
        mine.wait()

    return pl.pallas_call(
        body, name=name, out_shape=_sds((8,) + v.shape, v.dtype), in_specs=[ANY], out_specs=ANY,
        scratch_shapes=[pltpu.SemaphoreType.DMA((7,)), pltpu.SemaphoreType.DMA((7,)), pltpu.SemaphoreType.DMA],
    )(v)


def _chips(x, y):
    chips = [(1 - x, y), (x, 1 - y), (1 - x, 1 - y)]
    return chips, [2 * cx + cy for cx, cy in chips]


def ag_comm(bufs):
    n = len(bufs)

    def parts(outs):
        x, y, c = _me()
        chips, slots = _chips(x, y)
        return x, y, c, 2 * x + y, (x, y, 1 - c), chips, slots

    def ici(outs, t, j, send_sems, recv_sems, src_slot, off):
        x, y, c, s, sib, chips, slots = parts(outs)
        H = outs[t].shape[1] // 2
        blk = outs[t].at[src_slot, pl.ds(c * H, H)]
        return pltpu.make_async_remote_copy(
            src_ref=blk, dst_ref=blk, send_sem=send_sems.at[off + 6 * t + j], recv_sem=recv_sems.at[off + 6 * t + j],
            device_id=(chips[j][0], chips[j][1], c), device_id_type=MESH)

    def d2d(outs, t, j, send_sems, recv_sems, half, off):
        x, y, c, s, sib, chips, slots = parts(outs)
        H = outs[t].shape[1] // 2
        blk = outs[t].at[slots[j], pl.ds(half * H, H)]
        return pltpu.make_async_remote_copy(
            src_ref=blk, dst_ref=blk, send_sem=send_sems.at[off + 6 * t + 3 + j],
            recv_sem=recv_sems.at[off + 6 * t + 3 + j], device_id=sib, device_id_type=MESH)

    def start(reads, outs, send_sems, recv_sems, off=0):
        x, y, c, s, sib, chips, slots = parts(outs)
        for t in range(n):
            for j in range(3):
                ici(outs, t, j, send_sems, recv_sems, s, off).start()

    def finish(reads, outs, send_sems, recv_sems, off=0):
        x, y, c, s, sib, chips, slots = parts(outs)
        for t in range(n):
            for j in range(3):
                ici(outs, t, j, send_sems, recv_sems, slots[j], off).wait_recv()
                d2d(outs, t, j, send_sems, recv_sems, c, off).start()
        for t in range(n):
            for j in range(3):
                d2d(outs, t, j, send_sems, recv_sems, 1 - c, off).wait_recv()
        for t in range(n):
            for j in range(3):
                ici(outs, t, j, send_sems, recv_sems, s, off).wait_send()
                d2d(outs, t, j, send_sems, recv_sems, c, off).wait_send()

    return Comm([], bufs, 6 * n, start, finish)


def rs_p1_comm(tensors):
    n = len(tensors)

    def copy(ins, gots, t, send_sems, recv_sems, off):
        x, y, c = _me()
        H = ins[t].shape[1] // 2
        return pltpu.make_async_remote_copy(
            src_ref=ins[t].at[:, pl.ds((1 - c) * H, H)], dst_ref=gots[t], send_sem=send_sems.at[off + t],
            recv_sem=recv_sems.at[off + t], device_id=(x, y, 1 - c), device_id_type=MESH)

    def start(ins, gots, send_sems, recv_sems, off=0):
        for t in range(n):
            copy(ins, gots, t, send_sems, recv_sems, off).start()

    def finish(ins, gots, send_sems, recv_sems, off=0):
        for t in range(n):
            copy(ins, gots, t, send_sems, recv_sems, off).wait_recv()
        for t in range(n):
            copy(ins, gots, t, send_sems, recv_sems, off).wait_send()

    half = lambda t: _sds((t.shape[0], t.shape[1] // 2) + t.shape[2:], t.dtype)
    return Comm(tensors, [half(t) for t in tensors], n, start, finish)


def rs_p2_comm(partials, landeds):
    n = len(partials)

    def start(ins, outs, send_sems, recv_sems, off=0):
        x, y, c = _me()
        s = 2 * x + y
        chips, slots = _chips(x, y)
        for t in range(n):
            for j, chip in enumerate(chips):
                src = ins[t].at[slots[j]] if ins[t].shape[0] == 4 else ins[t].at[0]
                pltpu.make_async_remote_copy(
                    src_ref=src, dst_ref=outs[t].at[s], send_sem=send_sems.at[off + 3 * t + j],
                    recv_sem=recv_sems.at[off + 3 * t + j], device_id=(chip[0], chip[1], c), device_id_type=MESH).start()

    def finish(ins, outs, send_sems, recv_sems, off=0):
        x, y, c = _me()
        s = 2 * x + y
        chips, slots = _chips(x, y)
        for t in range(n):
            for j, chip in enumerate(chips):
                dst = outs[t].at[slots[j]]
                pltpu.make_async_remote_copy(
                    src_ref=dst, dst_ref=dst, send_sem=send_sems.at[off + 3 * t + j],
                    recv_sem=recv_sems.at[off + 3 * t + j], device_id=(chip[0], chip[1], c), device_id_type=MESH).wait_recv()
        for t in range(n):
            for j, chip in enumerate(chips):
                src = ins[t].at[slots[j]] if ins[t].shape[0] == 4 else ins[t].at[0]
                pltpu.make_async_remote_copy(
                    src_ref=src, dst_ref=outs[t].at[s], send_sem=send_sems.at[off + 3 * t + j],
                    recv_sem=recv_sems.at[off + 3 * t + j], device_id=(chip[0], chip[1], c), device_id_type=MESH).wait_send()

    return Comm(partials, landeds, 3 * n, start, finish)


def rs_p3_comm(greds, plan):
    n = len(plan)

    def copy(outs, t, send_sems, recv_sems, half, off):
        x, y, c = _me()
        oi, li, H = plan[t]
        dst = outs[oi] if li is None else outs[oi].at[li]
        blk = dst.at[pl.ds((c if half == 0 else 1 - c) * H, H)]
        return pltpu.make_async_remote_copy(
            src_ref=blk, dst_ref=blk, send_sem=send_sems.at[off + t], recv_sem=recv_sems.at[off + t],
            device_id=(x, y, 1 - c), device_id_type=MESH)

    def start(reads, outs, send_sems, recv_sems, off=0):
        for t in range(n):
            copy(outs, t, send_sems, recv_sems, 0, off).start()

    def finish(reads, outs, send_sems, recv_sems, off=0):
        for t in range(n):
            copy(outs, t, send_sems, recv_sems, 1, off).wait_recv()
        for t in range(n):
            copy(outs, t, send_sems, recv_sems, 0, off).wait_send()

    return Comm([], greds, n, start, finish)


def _rows_block(rows, cols, nbytes=1 << 20):
    bm = 8
    while bm * 2 * cols * 4 <= nbytes and rows % (bm * 2) == 0:
        bm *= 2
    return bm


def cast_into_slot(name, w, where, layer=None):
    rows, W = w.shape[-2:]
    bm = _rows_block(rows, W)

    def body(p_ref, w_ref, o_ref):
        o_ref[...] = w_ref[...].astype(BF)

    if layer is None:
        ispec = pl.BlockSpec((bm, W), lambda i, p: (i, 0))
    else:
        ispec = pl.BlockSpec((None, bm, W), lambda i, p: (layer, i, 0))
    return pl.pallas_call(
        body, name=name, out_shape=_sds((4, rows, W), BF), compiler_params=_cparams(),
        grid_spec=pltpu.PrefetchScalarGridSpec(
            num_scalar_prefetch=1, grid=(rows // bm,), in_specs=[ispec],
            out_specs=pl.BlockSpec((None, bm, W), lambda i, p: (p[1], i, 0))),
    )(where, w)


def add_pair(name, g, got, where):
    K, R, W = g.shape
    H = R // 2
    bm = _rows_block(H, W)
    nh = H // bm

    def body(p_ref, g_ref, got_ref, part_ref, land_ref):
        k = pl.program_id(1)
        v = (g_ref[...].astype(F32) + got_ref[...].astype(F32)).astype(part_ref.dtype)
        part_ref[...] = v
        own = (k == p_ref[1]) if K == 4 else (k == 0)

        @pl.when(own)
        def _():
            land_ref[...] = v

    return pl.pallas_call(
        body, name=name, out_shape=[_sds((K, H, W), g.dtype), _sds((4, H, W), g.dtype)], compiler_params=_cparams(),
        grid_spec=pltpu.PrefetchScalarGridSpec(
            num_scalar_prefetch=1, grid=(nh, K),
            in_specs=[pl.BlockSpec((None, bm, W), lambda i, k, p: (k, p[0] * nh + i, 0)),
                      pl.BlockSpec((None, bm, W), lambda i, k, p: (k, i, 0))],
            out_specs=[pl.BlockSpec((None, bm, W), lambda i, k, p: (k, i, 0)),
                       pl.BlockSpec((None, bm, W), lambda i, k, p: (p[1], i, 0))]),
    )(where, g, got)


def sum_slots_into(name, landed, where, dest, dest_shape, li):
    K, H, W = landed.shape
    bm = _rows_block(H, 2 * W)
    nh = H // bm

    def body(*refs):
        r, o_ref = refs[1], refs[-1]
        acc = r[0].astype(F32)
        for k in range(1, K):
            acc = acc + r[k].astype(F32)
        o_ref[...] = acc

    if li is None:
        ospec = pl.BlockSpec((bm, W), lambda i, p: (p[0] * nh + i, 0))
    else:
        ospec = pl.BlockSpec((None, bm, W), lambda i, p: (li, p[0] * nh + i, 0))
    in_specs = [pl.BlockSpec((K, bm, W), lambda i, p: (0, i, 0))]
    args = [where, landed]
    aliases = {}
    if dest is not None:
        in_specs.append(ANY)
        args.append(dest)
        aliases = {2: 0}
    return pl.pallas_call(
        body, name=name, out_shape=_sds(dest_shape, F32), compiler_params=_cparams(), input_output_aliases=aliases,
        grid_spec=pltpu.PrefetchScalarGridSpec(num_scalar_prefetch=1, grid=(nh,), in_specs=in_specs, out_specs=ospec),
    )(*args)


def sum_slots(name, a):
    K, H, W = a.shape
    bm = _rows_block(H, W * K // 2)

    def fn(ids, r):
        acc = r[0]
        for k in range(1, K):
            acc = acc + r[k]
        return [acc]

    return ew_call(name, (H // bm,), fn, [(a, pl.BlockSpec((K, bm, W), lambda i: (0, i, 0)))],
                   [(_sds((H, W), F32), pl.BlockSpec((bm, W), lambda i: (i, 0)), False)])[0]


def _adamw_math(w, g, m, v):
    bc1 = 1.0 - ADAM_B1 ** ADAM_STEP
    bc2 = 1.0 - ADAM_B2 ** ADAM_STEP
    mn = ADAM_B1 * m + (1.0 - ADAM_B1) * g
    vn = ADAM_B2 * v + (1.0 - ADAM_B2) * (g * g)
    m_hat = mn / bc1
    v_hat = vn / bc2
    delta = -ADAM_LR * (m_hat / (jnp.sqrt(v_hat) + ADAM_EPS) + ADAM_WD * w)
    return delta, mn, vn


def adamw(name, w, g, m, v, comm=None):
    shape = w.shape
    flat = lambda t: t.reshape(-1, shape[-1])
    w2, g2, m2, v2 = flat(w), flat(g), flat(m), flat(v)
    bm = _rows_block(w2.shape[0], w2.shape[1])

    def fn(ids, w_ref, g_ref, m_ref, v_ref):
        return list(_adamw_math(w_ref[...], g_ref[...], m_ref[...], v_ref[...]))

    spec = pl.BlockSpec((bm, w2.shape[1]), lambda i: (i, 0))
    res = ew_call(name, (w2.shape[0] // bm,), fn, [(w2, spec), (g2, spec), (m2, spec), (v2, spec)],
                  [(_sds(w2.shape, F32), spec, False)] * 3, comm=comm)
    if comm is not None:
        return [o.reshape(shape) for o in res[0]], res[1]
    return [o.reshape(shape) for o in res]


def adamw_many(name, params):
    n = len(params)
    shapes = [p_[0].shape for p_ in params]
    two_d = lambda t: t.reshape(-1, t.shape[-1])
    args = [two_d(t) for p_ in params for t in p_]

    def body(*refs):
        ins, outs = refs[:4 * n], refs[4 * n:]
        for q in range(n):
            w_ref, g_ref, m_ref, v_ref = ins[4 * q:4 * q + 4]
            for o_ref, val in zip(outs[3 * q:3 * q + 3], _adamw_math(w_ref[...], g_ref[...], m_ref[...], v_ref[...])):
                o_ref[...] = val

    full = lambda a: pl.BlockSpec(a.shape, lambda i: (0, 0))
    out_shape = [_sds(args[4 * q].shape, F32) for q in range(n) for _ in range(3)]
    res = hosted_call(body, name=name, grid=(1,), in_specs=[full(a) for a in args],
                      out_specs=[full(o) for o in out_shape], out_shape=out_shape, args=args)
    return [tuple(res[3 * q + r].reshape(shapes[q]) for r in range(3)) for q in range(n)]


def dmod_pack(gd):
    N = gd.shape[-1]
    bn = _pick(N, [4608, 2304, 1152, 1024, 512, 256, 128])

    def fn(ids, r):
        lat = [r[d, 1:2, :] for d in range(8)]
        cs = r[0, 0:1, :]
        for d in range(1, 8):
            cs = cs + r[d, 0:1, :]
        tot = cs
        for d in range(8):
            tot = tot + lat[d]
        return [jnp.concatenate(lat + [cs, jnp.zeros((7, bn), F32)], axis=0), tot]

    return ew_call("dmod_pack", (N // bn,), fn, [(gd, pl.BlockSpec((8, 2, bn), lambda j: (0, 0, j)))],
                   [(_sds((16, N), F32), pl.BlockSpec((16, bn), lambda j: (0, j)), False),
                    (_sds((1, N), F32), pl.BlockSpec((1, bn), lambda j: (0, j)), False)])


def _silu(v):
    return v * _sig(v)


def kernel(x, c, ctx, c_ctx, w_mod, b_mod, norm_g, ffn_wg, ffn_wu, ffn_wd, w_in, w_out, q_norm_g, k_norm_g, conv_w, conv_b, lru_wa, lru_ba, lru_wx, lru_bx, lru_lambda, final_norm_g, loss_target, m_c_ctx, m_w_mod, m_b_mod, m_norm_g, m_ffn_wg, m_ffn_wu, m_ffn_wd, m_w_in, m_w_out, m_q_norm_g, m_k_norm_g, m_conv_w, m_conv_b, m_lru_wa, m_lru_ba, m_lru_wx, m_lru_bx, m_lru_lambda, m_final_norm_g, v_c_ctx, v_w_mod, v_b_mod, v_norm_g, v_ffn_wg, v_ffn_wu, v_ffn_wd, v_w_in, v_w_out, v_q_norm_g, v_k_norm_g, v_conv_w, v_conv_b, v_lru_wa, v_lru_ba, v_lru_wx, v_lru_bx, v_lru_lambda, v_final_norm_g):
    given = dict(locals())
    names = ['c_ctx', 'w_mod', 'b_mod', 'norm_g', 'ffn_wg', 'ffn_wu', 'ffn_wd', 'w_in', 'w_out', 'q_norm_g', 'k_norm_g',
             'conv_w', 'conv_b', 'lru_wa', 'lru_ba', 'lru_wx', 'lru_bx', 'lru_lambda', 'final_norm_g']
    S, D = x.shape[1], x.shape[2]
    C = ctx.shape[1]
    NS = 4
    F4, W4, LB = ffn_wg.shape[-1], w_in.shape[-1], lru_wa.shape[2]
    dm = Dims(S, C, D, F4, W4, NS, LB)
    Ds = D // NS
    Wm = w_mod.shape[-1]
    xi, yi, ci = lax.axis_index("x"), lax.axis_index("y"), lax.axis_index("c")
    slot = 2 * xi + yi
    me = 4 * xi + 2 * yi + ci
    ident = lambda ids, accs, ex: list(accs)

    pack1 = jnp.concatenate([c.reshape(-1), norm_g.reshape(-1), conv_w.reshape(-1), lru_ba.reshape(-1),
                             lru_bx.reshape(-1), lru_lambda.reshape(-1)]).reshape(1, -1)
    g1 = allgather8("ag_small_params", pack1)[:, 0]
    c_all = g1[:, :D]

    def unshard(off, k):
        part = g1[0::2, off:off + k * Ds].reshape(NS, k, Ds)
        return jnp.transpose(part, (1, 0, 2)).reshape(k, D)

    norm_g_f = unshard(D, 3)
    conv_w_f = unshard(D + 3 * Ds, 4)
    ba_f = unshard(D + 7 * Ds, 2)
    bx_f = unshard(D + 9 * Ds, 2)
    lam_f = unshard(D + 11 * Ds, 2)

    call16 = jnp.concatenate([c_all, c_ctx.reshape(1, D), jnp.zeros((7, D), F32)], axis=0)
    b_cols = lax.dynamic_slice(b_mod, (0, slot * Wm), (1, Wm))
    bnm = _pick(Wm, [1536, 1152, 768, 512, 384, 256, 128])
    bkm = _pick(D, [512, 256, 128])
    modp = fused_mm(
        "mod_fwd", (1, Wm // bnm, D // bkm),
        [(call16, pl.BlockSpec((16, bkm), lambda i, j, k: (0, k))),
         (w_mod[0], pl.BlockSpec((bkm, bnm), lambda i, j, k: (k, j)))],
        [(0, 1, NN, 0)], [(16, bnm)], lambda ids, accs, ex: [accs[0] + ex[0][...]],
        [(_sds((16, Wm), F32), pl.BlockSpec((16, bnm), lambda i, j, k: (0, j)))],
        extras=[(b_cols, pl.BlockSpec((1, bnm), lambda i, j, k: (0, j)))], pre={0: _silu})[0]
    gm = allgather8("ag_mod", modp)
    mod_full = jnp.concatenate([gm[0], gm[2], gm[4], gm[6]], axis=1)
    mod_x = lax.dynamic_index_in_dim(mod_full, me, axis=0, keepdims=False)
    modv = jnp.stack([mod_full[8], mod_x]).reshape(2, N_MOD, 1, D)

    where = jnp.stack([ci, slot]).astype(jnp.int32)
    wbuf = {}
    for key, short in (('ffn_wg', 'wg'), ('ffn_wu', 'wu'), ('ffn_wd', 'wd')):
        for l in range(2):
            wbuf[short + str(l)] = cast_into_slot("cast_%s%d" % (short, l), given[key][0], where, l)
    wbuf['w_in'] = cast_into_slot("cast_w_in", w_in[0], where)
    wbuf['w_out'] = cast_into_slot("cast_w_out", w_out[0], where)

    loss_l, grad_x, dmod, small, reduced, late = local_step(
        dm, x[0], ctx[0], loss_target[0], modv, norm_g_f.reshape(3, 1, D), final_norm_g.reshape(1, D),
        q_norm_g, k_norm_g, conv_w_f, conv_b, lru_wa[0], ba_f.reshape(2, 1, D), lru_wx[0], bx_f.reshape(2, 1, D),
        lam_f.reshape(2, 1, D), wbuf, where)
    loss = lax.psum(loss_l[0, 0], ("x", "y", "c"))

    grads = {}
    gd = allgather8("ag_dmod", dmod.reshape(2, N_MOD * D))
    dM, g_bmod = dmod_pack(gd)
    dMc = lax.dynamic_slice(dM, (0, slot * Wm), (16, Wm))
    bmm = _pick(D, [512, 256, 128])
    grads['w_mod'] = fused_mm(
        "w_mod_dw", (D // bmm, Wm // bnm, 1),
        [(call16, pl.BlockSpec((16, bmm), lambda i, j, k: (0, i))), (dMc, pl.BlockSpec((16, bnm), lambda i, j, k: (0, j)))],
        [(0, 1, TN, 0)], [(bmm, bnm)], ident,
        [(_sds((D, Wm), F32), pl.BlockSpec((bmm, bnm), lambda i, j, k: (i, j)))], pre={0: _silu})[0][None]
    grads['b_mod'] = g_bmod

    def epi_cc(ids, accs, ex):
        v = ex[0][...]
        sg = _sig(v)
        return [accs[0] * (sg * (1.0 + v * (1.0 - sg)))]

    pcc = fused_mm(
        "c_ctx_partial", (1, D // bmm, Wm // bnm),
        [(dMc, pl.BlockSpec((16, bnm), lambda i, j, k: (0, k))), (w_mod[0], pl.BlockSpec((bmm, bnm), lambda i, j, k: (j, k)))],
        [(0, 1, NT, 0)], [(16, bmm)], epi_cc,
        [(_sds((16, D), F32), pl.BlockSpec((16, bmm), lambda i, j, k: (0, j)))],
        extras=[(c_ctx.reshape(1, D), pl.BlockSpec((1, bmm), lambda i, j, k: (0, j)))])[0]
    pcc_row = jnp.where(ci == 0, pcc[8], 0.0)

    order = ['q_norm_g', 'k_norm_g', 'conv_b', 'final_norm_g', 'norm_g', 'conv_w', 'lru_ba', 'lru_bx', 'lru_lambda']
    flat = [small[k].reshape(-1) for k in order] + [pcc_row]
    sizes = [f.shape[0] for f in flat]
    tot = sum(sizes)
    LW = 1024
    padded = -(-tot // (8 * LW)) * (8 * LW)
    tiny = jnp.concatenate(flat + [jnp.zeros((padded - tot,), F32)]).reshape(-1, LW)
    summed = sum_slots("tiny_sum", allgather8("ag_tiny_grads", tiny)).reshape(-1)
    offs = {}
    o = 0
    for k, n_ in zip(order + ['c_ctx'], sizes):
        offs[k] = summed[o:o + n_]
        o += n_
    shard = lambda k, rows: lax.dynamic_slice_in_dim(offs[k].reshape(rows, D), slot * Ds, Ds, axis=1)
    grads['c_ctx'] = offs['c_ctx']
    grads['q_norm_g'] = offs['q_norm_g'].reshape(1, HEAD_DIM)
    grads['k_norm_g'] = offs['k_norm_g'].reshape(1, HEAD_DIM)
    grads['conv_b'] = offs['conv_b'].reshape(1, D)
    grads['final_norm_g'] = offs['final_norm_g']
    grads['norm_g'] = shard('norm_g', 3)[None]
    grads['conv_w'] = shard('conv_w', 4)[None]
    grads['lru_ba'] = shard('lru_ba', 2)[None]
    grads['lru_bx'] = shard('lru_bx', 2)[None]
    grads['lru_lambda'] = shard('lru_lambda', 2)[None]

    LBD = D // LB
    delta, new_m, new_v = {}, {}, {}

    def update(k, comm=None):
        res = adamw("adamw_" + k, given[k], grads[k], given['m_' + k], given['v_' + k], comm=comm)
        if comm is not None:
            res, landed = res
        delta[k], new_m[k], new_v[k] = res
        return landed if comm is not None else None

    landeds = update('w_mod', rs_p2_comm([p_[0] for p_ in late], [p_[1] for p_ in late]))
    g_wg = sum_slots_into("rs_sum_wg0", landeds[0], where, reduced['ffn_wg'], (2, D, F4), 0)
    g_wu = sum_slots_into("rs_sum_wu0", landeds[1], where, reduced['ffn_wu'], (2, D, F4), 0)
    g_wa = sum_slots_into("rs_sum_lru_wa", landeds[2], where, None, (2 * LB * LBD, LBD), None)
    g_wx = sum_slots_into("rs_sum_lru_wx", landeds[3], where, None, (2 * LB * LBD, LBD), None)
    g_wg, g_wu, g_wd, g_wa, g_wx = comm_call("rs_tail_p3", rs_p3_comm(
        [g_wg, g_wu, reduced['ffn_wd'], g_wa, g_wx],
        [(0, 0, D // 2), (1, 0, D // 2), (2, 0, F4 // 2), (3, None, LB * LBD), (4, None, LB * LBD)]))
    grads.update(ffn_wg=g_wg[None], ffn_wu=g_wu[None], ffn_wd=g_wd[None], w_in=reduced['w_in'][None],
                 w_out=reduced['w_out'][None], lru_wa=g_wa.reshape(lru_wa.shape), lru_wx=g_wx.reshape(lru_wx.shape))
    big_names = ['w_mod', 'ffn_wg', 'ffn_wu', 'ffn_wd', 'w_in', 'w_out', 'lru_wa', 'lru_wx']
    for k in big_names[1:]:
        update(k)
    tiny_names = [k for k in names if k not in big_names]
    res = adamw_many("adamw_tiny", [(given[k], grads[k], given['m_' + k], given['v_' + k]) for k in tiny_names])
    for k, (d_, m_, v_) in zip(tiny_names, res):
        delta[k], new_m[k], new_v[k] = d_, m_, v_

    return (loss, grad_x[None], *[grads[k] for k in names], *[delta[k] for k in names],
            *[new_m[k] for k in names], *[new_v[k] for k in names])
```

```python
import functools

import jax
import jax.numpy as jnp
from jax import lax
from jax.experimental import pallas as pl
from jax.experimental.pallas import tpu as pltpu

F32 = jnp.float32
BF = jnp.bfloat16
EPS = 1e-6
HEAD_DIM = 128
GRID_W = 64
ROPE_THETA = 10000.0
LRU_C = 8.0
FFN_RES = 0.5
N_MOD = 9
LOG2_E = 1.4426950408889634
ADAM_LR, ADAM_B1, ADAM_B2, ADAM_EPS, ADAM_WD, ADAM_STEP = 0.001, 0.9, 0.999, 1e-08, 0.01, 10
VMEM_LIMIT = 52 * 1024 * 1024
MESH = pl.DeviceIdType.MESH
ANY = pl.BlockSpec(memory_space=pl.ANY)


def _sds(shape, dt):
    return jax.ShapeDtypeStruct(tuple(shape), dt)


def _pick(n, cands):
    for c in cands:
        if n % c == 0:
            return c
    return n


def _cparams(**kw):
    return pltpu.CompilerParams(vmem_limit_bytes=VMEM_LIMIT, **kw)


def _sig(x):
    return 1.0 / (1.0 + jnp.exp(-x))


def _gelu(x):
    t = jnp.tanh(0.7978845608028654 * (x + 0.044715 * x * x * x))
    return 0.5 * x * (1.0 + t), t


def _gelu_grad(x, t):
    return 0.5 * (1.0 + t) + 0.5 * x * (1.0 - t * t) * 0.7978845608028654 * (1.0 + 3.0 * 0.044715 * x * x)


class Comm:
    def __init__(self, reads, lands, n_sem, start, finish):
        self.reads, self.lands, self.n_sem, self.start, self.finish = list(reads), list(lands), n_sem, start, finish


def merge_comms(comms):
    reads = [r for c in comms for r in c.reads]
    lands = [l for c in comms for l in c.lands]

    def run(which):
        def fn(r, lo, send_sems, recv_sems, off=0):
            ro = lo_ = so = 0
            for c in comms:
                getattr(c, which)(r[ro:ro + len(c.reads)], lo[lo_:lo_ + len(c.lands)], send_sems, recv_sems, off + so)
                ro, lo_, so = ro + len(c.reads), lo_ + len(c.lands), so + c.n_sem
        return fn

    return Comm(reads, lands, sum(c.n_sem for c in comms), run('start'), run('finish'))


def hosted_call(body, *, name, grid, in_specs, out_specs, out_shape, args, scratch_shapes=(), aliases=None, comm=None):
    aliases = dict(aliases or {})
    if comm is None:
        return pl.pallas_call(
            body, name=name, grid=grid, in_specs=list(in_specs), out_specs=list(out_specs), out_shape=list(out_shape),
            scratch_shapes=list(scratch_shapes), input_output_aliases=aliases, compiler_params=_cparams())(*args)
    n_in, n_out, n_sc = len(args), len(out_shape), len(scratch_shapes)
    land_in = [(t, l) for t, l in enumerate(comm.lands) if not isinstance(l, jax.ShapeDtypeStruct)]
    nr, nli, nl = len(comm.reads), len(land_in), len(comm.lands)

    def wrapped(*refs):
        a = refs[:n_in]
        r = refs[n_in:n_in + nr]
        pos = n_in + nr + nli
        o = refs[pos:pos + n_out]
        lo = refs[pos + n_out:pos + n_out + nl]
        sc = refs[pos + n_out + nl:pos + n_out + nl + n_sc]
        send_sems, recv_sems = refs[pos + n_out + nl + n_sc:]
        ids = [pl.program_id(d) for d in range(len(grid))]
        first, last = ids[0] == 0, ids[0] == grid[0] - 1
        for d in range(1, len(grid)):
            first = first & (ids[d] == 0)
            last = last & (ids[d] == grid[d] - 1)

        @pl.when(first)
        def _():
            comm.start(r, lo, send_sems, recv_sems)

        body(*a, *o, *sc)

        @pl.when(last)
        def _():
            comm.finish(r, lo, send_sems, recv_sems)

    for q, (t, _) in enumerate(land_in):
        aliases[n_in + nr + q] = n_out + t
    res = pl.pallas_call(
        wrapped, name=name, grid=grid,
        in_specs=list(in_specs) + [ANY] * (nr + nli), out_specs=list(out_specs) + [ANY] * nl,
        out_shape=list(out_shape) + [l if isinstance(l, jax.ShapeDtypeStruct) else _sds(l.shape, l.dtype) for l in comm.lands],
        scratch_shapes=list(scratch_shapes) + [pltpu.SemaphoreType.DMA((comm.n_sem,)), pltpu.SemaphoreType.DMA((comm.n_sem,))],
        input_output_aliases=aliases, compiler_params=_cparams(),
    )(*args, *comm.reads, *[l for _, l in land_in])
    return list(res[:n_out]), list(res[n_out:])


def comm_call(name, comm):
    def body():
        pass

    return hosted_call(body, name=name, grid=(1,), in_specs=[], out_specs=[], out_shape=[], args=[], comm=comm)[1]


def ew_call(name, grid, fn, ins, outs, first=None, aliases=None, comm=None):
    n_in = len(ins)

    def body(*refs):
        ids = tuple(pl.program_id(a) for a in range(len(grid)))
        vals = fn(ids, *refs[:n_in])
        for (_, _, acc), o_ref, v in zip(outs, refs[n_in:], vals):
            if not acc:
                o_ref[...] = v.astype(o_ref.dtype)
            else:
                is_first = first(ids)

                @pl.when(is_first)
                def _(o_ref=o_ref, v=v):
                    o_ref[...] = v.astype(o_ref.dtype)

                @pl.when(jnp.logical_not(is_first))
                def _(o_ref=o_ref, v=v):
                    o_ref[...] += v.astype(o_ref.dtype)

    return hosted_call(body, name=name, grid=grid, in_specs=[s for _, s in ins], out_specs=[s for _, s, _ in outs],
                       out_shape=[o for o, _, _ in outs], args=[a for a, _ in ins], aliases=aliases, comm=comm)


def fused_mm(name, grid, ins, prods, acc_shapes, epi, outs, extras=(), pre=None, comm=None):
    n_in, n_ex, n_out = len(ins), len(extras), len(outs)
    nk = grid[-1]
    pre = pre or {}
    n_acc = len(acc_shapes)

    def body(*refs):
        in_refs = refs[:n_in]
        ex_refs = refs[n_in:n_in + n_ex]
        out_refs = refs[n_in + n_ex:n_in + n_ex + n_out]
        accs = refs[n_in + n_ex + n_out:]
        ids = tuple(pl.program_id(a) for a in range(len(grid)))
        k = ids[-1]
        loaded = {}

        def operand(i):
            if i not in loaded:
                v = in_refs[i][...]
                if i in pre:
                    v = pre[i](v)
                loaded[i] = v.astype(BF)
            return loaded[i]

        def product(ia, ib, dims):
            return lax.dot_general(operand(ia), operand(ib), (dims, ((), ())), preferred_element_type=F32)

        if nk == 1:
            sums = [None] * n_acc
            for ia, ib, dims, ai in prods:
                d = product(ia, ib, dims)
                sums[ai] = d if sums[ai] is None else sums[ai] + d
            for o_ref, v in zip(out_refs, epi(ids, sums, ex_refs)):
                o_ref[...] = v.astype(o_ref.dtype)
            return

        @pl.when(k == 0)
        def _():
            for a in accs:
                a[...] = jnp.zeros(a.shape, F32)

        for ia, ib, dims, ai in prods:
            accs[ai][...] += product(ia, ib, dims)

        @pl.when(k == nk - 1)
        def _():
            vals = epi(ids, [a[...] for a in accs], ex_refs)
            for o_ref, v in zip(out_refs, vals):
                o_ref[...] = v.astype(o_ref.dtype)

    return hosted_call(
        body, name=name, grid=grid, in_specs=[s for _, s in ins] + [s for _, s in extras],
        out_specs=[s for _, s in outs], out_shape=[o for o, _ in outs],
        scratch_shapes=[pltpu.VMEM(s, F32) for s in acc_shapes] if nk > 1 else [],
        args=[a for a, _ in ins] + [a for a, _ in extras], comm=comm)


NN = ((1,), (0,))
NT = ((1,), (1,))
TN = ((0,), (0,))


class Dims:
    def __init__(self, S, C, D, F4, W4, NS, LB):
        self.S, self.C, self.D, self.F4, self.W4, self.NS, self.LB = S, C, D, F4, W4, NS, LB
        self.T = S + C
        self.DFF = F4 * NS
        self.INW = W4 * NS
        self.NQ = D // HEAD_DIM
        self.KVW = (self.INW - 5 * D) // 2
        self.NKV = self.KVW // HEAD_DIM
        self.G = self.NQ // self.NKV
        self.OFF_K = D
        self.OFF_V = D + self.KVW
        self.OFF_LX = D + 2 * self.KVW
        self.OFF_LG = self.OFF_LX + D
        self.OFF_GA = self.OFF_LG + D
        self.OFF_GL = self.OFF_GA + D
        self.bm = _pick(C, [256, 128, 64, 32, 16, 8])
        self.nCb = C // self.bm
        self.nTb = self.T // self.bm
        self.nSb = S // self.bm
        self.mT = _pick(self.T, [544, 512, 384, 256, 128])
        self.mS = _pick(S, [512, 256, 128])
        self.kT = _pick(self.T, [1088, 1024, 768, 544, 512, 384, 256, 128])
        self.kS = _pick(S, [1024, 512, 256, 128])
        self.cw = _pick(D, [1024, 512, 256, 128]) if (self.OFF_LX % 1024 == 0 and D % 1024 == 0) else _pick(
            self.OFF_LX, [512, 256, 128])
        self.nsub = 2 if (W4 % 256 == 0 and W4 >= 512) else 1
        self.wb = W4 // self.nsub
        self.LBD = D // LB
        self.bq = _pick(C, [256, 128]) if S % _pick(C, [256, 128]) == 0 else 128


def rope_tables(dm):
    rows = dm.S // GRID_W
    row = jnp.repeat(jnp.arange(rows, dtype=F32), GRID_W)
    col = jnp.tile(jnp.arange(GRID_W, dtype=F32), rows)
    axis_dims = HEAD_DIM // 2
    freqs = ROPE_THETA ** (-jnp.arange(0, axis_dims, 2, dtype=F32) / axis_dims)
    ang = jnp.concatenate([row[:, None] * freqs, col[:, None] * freqs], axis=-1)
    cos = jnp.repeat(jnp.cos(ang), 2, axis=-1)
    sin = jnp.repeat(jnp.sin(ang), 2, axis=-1)
    sign = jnp.tile(jnp.array([-1.0, 1.0], F32), HEAD_DIM // 2)
    sin = sin * sign
    cos = jnp.concatenate([jnp.ones((dm.C, HEAD_DIM), F32), cos], axis=0)
    sin = jnp.concatenate([jnp.zeros((dm.C, HEAD_DIM), F32), sin], axis=0)
    return cos, sin


def _pair_swap(y):
    lane = lax.broadcasted_iota(jnp.int32, y.shape, 1)
    nxt = pltpu.roll(y, y.shape[1] - 1, 1)
    prv = pltpu.roll(y, 1, 1)
    return jnp.where((lane & 1) == 0, nxt, prv)


def normmod_fwd(name, dm, x, norm_g3, stage, modv, rows_T):
    D, bm = dm.D, dm.bm
    nb = dm.nTb if rows_T else dm.nSb
    typ = (lambda i: jnp.where(i < dm.nCb, 0, 1)) if rows_T else (lambda i: 1)

    def fn(ids, x_ref, g_ref, sh_ref, sc_ref):
        xv = x_ref[...]
        r = lax.rsqrt(jnp.mean(xv * xv, axis=-1, keepdims=True) + EPS)
        n = xv * r * g_ref[...]
        return [n * (1.0 + sc_ref[...]) + sh_ref[...]]

    return ew_call(
        name, (nb,), fn,
        [(x, pl.BlockSpec((bm, D), lambda i: (i, 0))),
         (norm_g3, pl.BlockSpec((None, 1, D), lambda i: (stage, 0, 0))),
         (modv, pl.BlockSpec((None, None, 1, D), lambda i: (typ(i), 3 * stage, 0, 0))),
         (modv, pl.BlockSpec((None, None, 1, D), lambda i: (typ(i), 3 * stage + 1, 0, 0)))],
        [(_sds(x.shape, BF), pl.BlockSpec((bm, D), lambda i: (i, 0)), False)])[0]


def normmod_concat_fwd(name, dm, ctx, x, norm_g3, modv):
    D, bm, nCb = dm.D, dm.bm, dm.nCb
    typ = lambda i: jnp.where(i < nCb, 0, 1)

    def fn(ids, c_ref, x_ref, g_ref, sh_ref, sc_ref):
        xv = jnp.where(ids[0] < nCb, c_ref[...], x_ref[...])
        r = lax.rsqrt(jnp.mean(xv * xv, axis=-1, keepdims=True) + EPS)
        n = xv * r * g_ref[...]
        return [n * (1.0 + sc_ref[...]) + sh_ref[...], xv]

    row = pl.BlockSpec((bm, D), lambda i: (i, 0))
    return ew_call(
        name, (dm.nTb,), fn,
        [(ctx, pl.BlockSpec((bm, D), lambda i: (jnp.minimum(i, nCb - 1), 0))),
         (x, pl.BlockSpec((bm, D), lambda i: (jnp.maximum(i - nCb, 0), 0))),
         (norm_g3, pl.BlockSpec((None, 1, D), lambda i: (0, 0, 0))),
         (modv, pl.BlockSpec((None, None, 1, D), lambda i: (typ(i), 0, 0, 0))),
         (modv, pl.BlockSpec((None, None, 1, D), lambda i: (typ(i), 1, 0, 0)))],
        [(_sds((dm.T, D), BF), row, False), (_sds((dm.T, D), F32), row, False)])


def normmod_bwd(name, dm, dh, x, dres, norm_g3, stage, modv, rows_T, dres_lat_only, out_lat_only=False):
    D, bm = dm.D, dm.bm
    nb = dm.nTb if rows_T else dm.nSb
    nCb = dm.nCb
    typ = (lambda i: jnp.where(i < nCb, 0, 1)) if rows_T else (lambda i: 1)
    if dres_lat_only:
        dres_map = lambda i: (jnp.maximum(i - nCb, 0), 0)
    else:
        dres_map = lambda i: (i, 0)

    def fn(ids, dh_ref, x_ref, dres_ref, g_ref, sc_ref):
        i = ids[0]
        xv = x_ref[...]
        dhv = dh_ref[...].astype(F32)
        r = lax.rsqrt(jnp.mean(xv * xv, axis=-1, keepdims=True) + EPS)
        xn = xv * r
        g = g_ref[...]
        n = xn * g
        dn = dhv * (1.0 + sc_ref[...])
        dxn = dn * g
        dx = r * (dxn - xn * jnp.mean(dxn * xn, axis=-1, keepdims=True))
        dresv = dres_ref[...]
        if dres_lat_only:
            dresv = jnp.where(i >= nCb, dresv, 0.0)
        dsh = jnp.sum(dhv, axis=0, keepdims=True)
        dsc = jnp.sum(dhv * n, axis=0, keepdims=True)
        dg = jnp.sum(dn * xn, axis=0, keepdims=True)
        return [dx + dresv, dsh, dsc, dg]

    if rows_T:
        first = lambda ids: (ids[0] == 0) | (ids[0] == nCb)
    else:
        first = lambda ids: ids[0] == 0
    acc = (_sds((2, 1, D), F32), pl.BlockSpec((None, 1, D), lambda i: (typ(i), 0, 0)), True)
    return ew_call(
        name, (nb,), fn,
        [(dh, pl.BlockSpec((bm, D), lambda i: (i, 0))),
         (x, pl.BlockSpec((bm, D), lambda i: (i, 0))),
         (dres, pl.BlockSpec((bm, D), dres_map)),
         (norm_g3, pl.BlockSpec((None, 1, D), lambda i: (stage, 0, 0))),
         (modv, pl.BlockSpec((None, None, 1, D), lambda i: (typ(i), 3 * stage + 1, 0, 0)))],
        [(_sds((dm.S, D) if out_lat_only else x.shape, F32),
          pl.BlockSpec((bm, D), (lambda i: (jnp.maximum(i - nCb, 0), 0)) if out_lat_only else (lambda i: (i, 0))), False),
         acc, acc, acc], first=first)


def gate_bwd(name, dm, dx, f, modv, gidx, scale, rows_T):
    D, bm = dm.D, dm.bm
    nb = dm.nTb if rows_T else dm.nSb
    nCb = dm.nCb
    typ = (lambda i: jnp.where(i < nCb, 0, 1)) if rows_T else (lambda i: 1)

    def fn(ids, dx_ref, f_ref, g_ref):
        dxv = dx_ref[...]
        return [scale * g_ref[...] * dxv, jnp.sum(scale * f_ref[...].astype(F32) * dxv, axis=0, keepdims=True)]

    if rows_T:
        first = lambda ids: (ids[0] == 0) | (ids[0] == nCb)
    else:
        first = lambda ids: ids[0] == 0
    return ew_call(
        name, (nb,), fn,
        [(dx, pl.BlockSpec((bm, D), lambda i: (i, 0))),
         (f, pl.BlockSpec((bm, D), lambda i: (i, 0))),
         (modv, pl.BlockSpec((None, None, 1, D), lambda i: (typ(i), gidx, 0, 0)))],
        [(_sds(dx.shape, BF), pl.BlockSpec((bm, D), lambda i: (i, 0)), False),
         (_sds((2, 1, D), F32), pl.BlockSpec((None, 1, D), lambda i: (typ(i), 0, 0)), True)], first=first)


def ffn_fwd(name, dm, h, xres, wg, wu, wd, modv, gidx, rows_T, comm_up=None, comm_down=None):
    D, F4, NS = dm.D, dm.F4, dm.NS
    M = h.shape[0]
    bm = dm.mT if rows_T else dm.mS
    C = dm.C

    def epi_up(ids, accs, ex):
        a, u = accs
        return [a, u, a * _sig(a) * u]

    hspec = pl.BlockSpec((bm, D), lambda j, i, k: (i, 0))
    wspec = pl.BlockSpec((None, D, F4), lambda j, i, k: (j, 0, 0))
    ospec = pl.BlockSpec((bm, F4), lambda j, i, k: (i, j))
    res = fused_mm(
        name + "_up", (NS, M // bm, 1), [(h, hspec), (wg, wspec), (wu, wspec)],
        [(0, 1, NN, 0), (0, 2, NN, 1)], [(bm, F4), (bm, F4)], epi_up,
        [(_sds((M, dm.DFF), BF), ospec)] * 3, comm=comm_up)
    (a, u, s), land_up = res if comm_up is not None else (res, None)
    if wd is None:
        wd = land_up[0]

    bn = _pick(D, [1024, 512, 256, 128])

    def epi_dn(ids, accs, ex):
        f = accs[0]
        if rows_T:
            row = ids[0] * bm + lax.broadcasted_iota(jnp.int32, (bm, 1), 0)
            gate = jnp.where(row < C, ex[1][...], ex[2][...])
        else:
            gate = ex[2][...]
        return [ex[0][...] + FFN_RES * gate * f, f]

    gspec = lambda t: pl.BlockSpec((None, None, 1, bn), lambda i, j, k: (t, gidx, 0, j))
    res = fused_mm(
        name + "_down", (M // bm, D // bn, NS),
        [(s, pl.BlockSpec((bm, F4), lambda i, j, k: (i, k))),
         (wd, pl.BlockSpec((None, F4, bn), lambda i, j, k: (k, 0, j)))],
        [(0, 1, NN, 0)], [(bm, bn)], epi_dn,
        [(_sds((M, D), F32), pl.BlockSpec((bm, bn), lambda i, j, k: (i, j))),
         (_sds((M, D), BF), pl.BlockSpec((bm, bn), lambda i, j, k: (i, j)))],
        extras=[(xres, pl.BlockSpec((bm, bn), lambda i, j, k: (i, j))), (modv, gspec(0)), (modv, gspec(1))],
        comm=comm_down)
    (xo, f), land_down = res if comm_down is not None else (res, None)
    return xo, a, u, s, f, land_up, land_down


def ffn_bwd(name, dm, df, h, a, u, s, wg, wu, wd, rows_T, comms=None):
    comms = comms or {}
    landed, made = {}, {}

    def run(key, *args, **kw):
        comm = comms[key](landed, made) if key in comms else None
        res = fused_mm(*args, comm=comm, **kw)
        if comm is not None:
            res, landed[key] = res
        return res

    D, F4, NS = dm.D, dm.F4, dm.NS
    M = h.shape[0]
    bm = dm.mT if rows_T else dm.mS
    bkr = dm.kT if rows_T else dm.kS

    def epi_ds(ids, accs, ex):
        ds = accs[0]
        av = ex[0][...].astype(F32)
        uv = ex[1][...].astype(F32)
        sg = _sig(av)
        return [ds * uv * (sg * (1.0 + av * (1.0 - sg))), ds * av * sg]

    ospec = pl.BlockSpec((bm, F4), lambda j, i, k: (i, j))
    da, du = run(
        'ds', name + "_ds", (NS, M // bm, 1),
        [(df, pl.BlockSpec((bm, D), lambda j, i, k: (i, 0))),
         (wd, pl.BlockSpec((None, F4, D), lambda j, i, k: (j, 0, 0)))],
        [(0, 1, NT, 0)], [(bm, F4)], epi_ds, [(_sds((M, dm.DFF), BF), ospec)] * 2,
        extras=[(a, ospec), (u, ospec)])

    ident = lambda ids, accs, ex: list(accs)
    bn = _pick(D, [1024, 512, 256, 128])
    dwd = run(
        'dwd', name + "_dwd", (NS, D // bn, M // bkr),
        [(s, pl.BlockSpec((bkr, F4), lambda i, j, k: (k, i))),
         (df, pl.BlockSpec((bkr, bn), lambda i, j, k: (k, j)))],
        [(0, 1, TN, 0)], [(F4, bn)], ident,
        [(_sds((NS, F4, D), BF), pl.BlockSpec((None, F4, bn), lambda i, j, k: (i, 0, j)))])[0]
    made['dwd'] = dwd

    dwg, dwu = run(
        'dwgu', name + "_dwgu", (D // bn, NS, M // bkr),
        [(h, pl.BlockSpec((bkr, bn), lambda i, j, k: (k, i))),
         (da, pl.BlockSpec((bkr, F4), lambda i, j, k: (k, j))),
         (du, pl.BlockSpec((bkr, F4), lambda i, j, k: (k, j)))],
        [(0, 1, TN, 0), (0, 2, TN, 1)], [(bn, F4), (bn, F4)], ident,
        [(_sds((NS, D, F4), BF), pl.BlockSpec((None, bn, F4), lambda i, j, k: (j, i, 0)))] * 2)
    made['dwg'], made['dwu'] = dwg, dwu

    dh = run(
        'dh', name + "_dh", (M // bm, D // bn, NS),
        [(da, pl.BlockSpec((bm, F4), lambda i, j, k: (i, k))),
         (wg, pl.BlockSpec((None, bn, F4), lambda i, j, k: (k, j, 0))),
         (du, pl.BlockSpec((bm, F4), lambda i, j, k: (i, k))),
         (wu, pl.BlockSpec((None, bn, F4), lambda i, j, k: (k, j, 0)))],
        [(0, 1, NT, 0), (2, 3, NT, 0)], [(bm, bn)], ident,
        [(_sds((M, D), F32), pl.BlockSpec((bm, bn), lambda i, j, k: (i, j)))])[0]
    return dh, dwg, dwu, dwd, landed


def qk_prep(dm, P, gq, gk, cosf, sinf):
    D, KVW, bm = dm.D, dm.KVW, dm.bm

    def head_norm_rope(xh, g, c, s):
        r = lax.rsqrt(jnp.mean(xh * xh, axis=-1, keepdims=True) + EPS)
        y = xh * r * g
        return y * c + _pair_swap(y) * s

    def fn(ids, q_ref, k_ref, v_ref, gq_ref, gk_ref, c_ref, s_ref):
        c, s = c_ref[...], s_ref[...]
        qs = [head_norm_rope(q_ref[:, h * HEAD_DIM:(h + 1) * HEAD_DIM], gq_ref[...], c, s) for h in range(dm.NQ)]
        ks = [head_norm_rope(k_ref[:, h * HEAD_DIM:(h + 1) * HEAD_DIM], gk_ref[...], c, s) for h in range(dm.NKV)]
        return [jnp.concatenate(qs, axis=1), jnp.concatenate(ks, axis=1), v_ref[...]]

    hspec = pl.BlockSpec((bm, HEAD_DIM), lambda i: (i, 0))
    vec = pl.BlockSpec((1, HEAD_DIM), lambda i: (0, 0))
    return ew_call(
        "qk_prep", (dm.nTb,), fn,
        [(P, pl.BlockSpec((bm, D), lambda i: (i, 0))),
         (P, pl.BlockSpec((bm, KVW), lambda i: (i, dm.OFF_K // KVW))),
         (P, pl.BlockSpec((bm, KVW), lambda i: (i, dm.OFF_V // KVW))),
         (gq, vec), (gk, vec), (cosf, hspec), (sinf, hspec)],
        [(_sds((dm.T, D), BF), pl.BlockSpec((bm, D), lambda i: (i, 0)), False),
         (_sds((dm.T, KVW), BF), pl.BlockSpec((bm, KVW), lambda i: (i, 0)), False),
         (_sds((dm.T, KVW), BF), pl.BlockSpec((bm, KVW), lambda i: (i, 0)), False)])


def qk_prep_bwd(dm, dq, dk, dv, P, gq, gk, cosf, sinf, dP):
    D, KVW, bm, nCb = dm.D, dm.KVW, dm.bm, dm.nCb
    W = D + 2 * KVW

    def head_bwd(d, xh, g, c, s):
        dy = d * c - _pair_swap(d) * s
        r = lax.rsqrt(jnp.mean(xh * xh, axis=-1, keepdims=True) + EPS)
        xn = xh * r
        dg = jnp.sum(dy * xn, axis=0, keepdims=True)
        dxn = dy * g
        return r * (dxn - xn * jnp.mean(dxn * xn, axis=-1, keepdims=True)), dg

    def fn(ids, dq_ref, dk_ref, dv_ref, q_ref, k_ref, gq_ref, gk_ref, c_ref, s_ref, dp_any):
        i = ids[0]
        c, s = c_ref[...], s_ref[...]
        lat = i >= nCb
        outs, dgq = [], jnp.zeros((1, HEAD_DIM), F32)
        for h in range(dm.NQ):
            sl = slice(h * HEAD_DIM, (h + 1) * HEAD_DIM)
            d = jnp.where(lat, dq_ref[:, sl], 0.0)
            dx, dg = head_bwd(d, q_ref[:, sl], gq_ref[...], c, s)
            outs.append(dx)
            dgq = dgq + dg
        dgk = jnp.zeros((1, HEAD_DIM), F32)
        for h in range(dm.NKV):
            sl = slice(h * HEAD_DIM, (h + 1) * HEAD_DIM)
            dx, dg = head_bwd(dk_ref[:, sl], k_ref[:, sl], gk_ref[...], c, s)
            outs.append(dx)
            dgk = dgk + dg
        outs.append(dv_ref[...])
        return [jnp.concatenate(outs, axis=1), dgq, dgk]

    hspec = pl.BlockSpec((bm, HEAD_DIM), lambda i: (i, 0))
    vec = pl.BlockSpec((1, HEAD_DIM), lambda i: (0, 0))
    return ew_call(
        "qk_prep_bwd", (dm.nTb,), fn,
        [(dq, pl.BlockSpec((bm, D), lambda i: (jnp.maximum(i - nCb, 0), 0))),
         (dk, pl.BlockSpec((bm, KVW), lambda i: (i, 0))),
         (dv, pl.BlockSpec((bm, KVW), lambda i: (i, 0))),
         (P, pl.BlockSpec((bm, D), lambda i: (i, 0))),
         (P, pl.BlockSpec((bm, KVW), lambda i: (i, dm.OFF_K // KVW))),
         (gq, vec), (gk, vec), (cosf, hspec), (sinf, hspec), (dP, ANY)],
        [(_sds(dP.shape, BF), pl.BlockSpec((bm, W), lambda i: (i, 0)), False),
         (_sds((1, HEAD_DIM), F32), vec, True), (_sds((1, HEAD_DIM), F32), vec, True)],
        first=lambda ids: ids[0] == 0, aliases={9: 0})


def _softmax_numerators(s_ref, eb_ref, mb_ref, scale):
    rows, T = s_ref.shape
    m = jnp.max(s_ref[...], axis=-1, keepdims=True)
    mb_ref[...] = jnp.broadcast_to(m, (rows, HEAD_DIM))
    lacc = jnp.zeros((rows, HEAD_DIM), F32)
    for c in range(T // HEAD_DIM):
        cs = slice(c * HEAD_DIM, (c + 1) * HEAD_DIM)
        e = jnp.exp2((s_ref[:, cs] - mb_ref[...]) * (scale * LOG2_E))
        lacc = lacc + e
        eb_ref[:, cs] = e.astype(BF)
    return jnp.sum(lacc, axis=-1, keepdims=True)


def attention_fwd(dm, qr, kr, vb, comm=None):
    S, T, D, G, nCb = dm.S, dm.T, dm.D, dm.G, dm.nCb
    bq = dm.bq
    off = dm.C // bq
    scale = HEAD_DIM ** -0.5
    GW = G * HEAD_DIM

    def body(q_ref, k_ref, v_ref, o_ref):
        k = k_ref[...]
        v = v_ref[...]
        head = lambda h: slice(h * HEAD_DIM, (h + 1) * HEAD_DIM)
        scores = lambda h: lax.dot_general(q_ref[:, head(h)], k, (NT, ((), ())), preferred_element_type=F32)
        s_next = scores(0)
        for h in range(G):
            s = s_next
            if h + 1 < G:
                s_next = scores(h + 1)
            m = jnp.max(s, axis=-1, keepdims=True)
            p = jnp.exp2((s - m) * (scale * LOG2_E))
            l = jnp.sum(p, axis=-1, keepdims=True)
            o = lax.dot_general(p.astype(BF), v, (NN, ((), ())), preferred_element_type=F32)
            o_ref[:, head(h)] = o / l

    return hosted_call(
        body, grid=(dm.NKV, S // bq), name="attn_fwd",
        in_specs=[pl.BlockSpec((bq, GW), lambda g, i: (i + off, g)),
                  pl.BlockSpec((T, HEAD_DIM), lambda g, i: (0, g)),
                  pl.BlockSpec((T, HEAD_DIM), lambda g, i: (0, g))],
        out_specs=[pl.BlockSpec((bq, GW), lambda g, i: (i, g))],
        out_shape=[_sds((S, D), F32)], args=[qr, kr, vb], comm=comm)


def attention_bwd(dm, qr, kr, vb, attn, dattn, comm=None):
    S, T, D, G = dm.S, dm.T, dm.D, dm.G
    bq = dm.bq
    off = dm.C // bq
    scale = HEAD_DIM ** -0.5
    GW = G * HEAD_DIM

    def body(q_ref, k_ref, v_ref, o_ref, do_ref, dq_ref, dk_ref, dv_ref, s2_ref, dp_ref, eb_ref, tb_ref, mb_ref):
        i = pl.program_id(1)

        @pl.when(i == 0)
        def _():
            dk_ref[...] = jnp.zeros(dk_ref.shape, F32)
            dv_ref[...] = jnp.zeros(dv_ref.shape, F32)

        k = k_ref[...]
        v = v_ref[...]
        head = lambda h: slice(h * HEAD_DIM, (h + 1) * HEAD_DIM)

        def finish(h, w):
            dq_ref[:, head(h)] = lax.dot_general(tb_ref[...], k, (NN, ((), ())), preferred_element_type=F32) * w
            dk_ref[...] += lax.dot_general(tb_ref[...], (q_ref[:, head(h)].astype(F32) * w).astype(BF), (TN, ((), ())),
                                           preferred_element_type=F32)

        s2_ref[0] = lax.dot_general(q_ref[:, head(0)], k, (NT, ((), ())), preferred_element_type=F32)
        w_prev = None
        for h in range(G):
            s_ref = s2_ref.at[h % 2]
            do = do_ref[:, head(h)]
            dof = do.astype(F32)
            if h + 1 < G:
                s2_ref[(h + 1) % 2] = lax.dot_general(q_ref[:, head(h + 1)], k, (NT, ((), ())),
                                                      preferred_element_type=F32)
            if h > 0:
                finish(h - 1, w_prev)
            l = _softmax_numerators(s_ref, eb_ref, mb_ref, scale)
            rl = 1.0 / l
            dp_ref[...] = lax.dot_general(do, v, (NT, ((), ())), preferred_element_type=F32)
            dv_ref[...] += lax.dot_general(eb_ref[...], (dof * rl).astype(BF), (TN, ((), ())), preferred_element_type=F32)
            delta = jnp.sum(dof * o_ref[:, head(h)], axis=-1, keepdims=True)
            mb_ref[...] = jnp.broadcast_to(delta, (bq, HEAD_DIM))
            for c in range(T // HEAD_DIM):
                cs = slice(c * HEAD_DIM, (c + 1) * HEAD_DIM)
                tb_ref[:, cs] = (eb_ref[:, cs].astype(F32) * (dp_ref[:, cs] - mb_ref[...])).astype(BF)
            w_prev = scale * rl
        finish(G - 1, w_prev)

    return hosted_call(
        body, grid=(dm.NKV, S // bq), name="attn_bwd",
        in_specs=[pl.BlockSpec((bq, GW), lambda g, i: (i + off, g)),
                  pl.BlockSpec((T, HEAD_DIM), lambda g, i: (0, g)),
                  pl.BlockSpec((T, HEAD_DIM), lambda g, i: (0, g)),
                  pl.BlockSpec((bq, GW), lambda g, i: (i, g)),
                  pl.BlockSpec((bq, GW), lambda g, i: (i + off, g))],
        out_specs=[pl.BlockSpec((bq, GW), lambda g, i: (i, g)),
                   pl.BlockSpec((T, HEAD_DIM), lambda g, i: (0, g)),
                   pl.BlockSpec((T, HEAD_DIM), lambda g, i: (0, g))],
        out_shape=[_sds((S, D), F32), _sds((T, dm.KVW), F32), _sds((T, dm.KVW), F32)],
        args=[qr, kr, vb, attn, dattn],
        scratch_shapes=[pltpu.VMEM((2, bq, T), F32), pltpu.VMEM((bq, T), F32), pltpu.VMEM((bq, T), BF),
                        pltpu.VMEM((bq, T), BF), pltpu.VMEM((bq, HEAD_DIM), F32)], comm=comm)


def _conv_taps(dm, lx, masks_only=False):
    T, C = dm.T, dm.C
    t = lax.broadcasted_iota(jnp.int32, (T, 1), 0)
    valid = [(t >= 2) & ((t < C) | (t >= C + 2)), (t >= 1) & ((t < C) | (t >= C + 1)), None,
             (t != C - 1) & (t != T - 1)]
    shifts = [2, 1, 0, T - 1]
    taps = []
    for k in range(4):
        if k == 2:
            taps.append(lx)
        else:
            taps.append(jnp.where(valid[k], pltpu.roll(lx, shifts[k], 0), 0.0))
    return taps


def _scan_tiles(dm, chains):
    T, C = dm.T, dm.C
    nT, nC = T // 8, C // 8
    row = lax.broadcasted_iota(jnp.int32, (8, HEAD_DIM), 0)

    def tile_of(i, asc, split):
        if not split:
            return i if asc else nT - 1 - i
        if asc:
            return jnp.where(i < nT - nC, nC + i, i - (nT - nC))
        return jnp.where(i < nC, nC - 1 - i, nT - 1 - (i - nC))

    def step(i, carry, asc, split, a_ref, u_ref, out_ref, mode):
        off = pl.multiple_of(tile_of(i, asc, split) * 8, 8)
        a = a_ref[pl.ds(off, 8), :]
        b = u_ref[pl.ds(off, 8), :]
        if mode == 'lam':
            if asc:
                coef = jnp.where(row == 0, 1.0, pltpu.roll(a, 1, 0))
            else:
                coef = jnp.where(row == 7, 1.0, pltpu.roll(a, 7, 0))
        else:
            coef = a
        A, B = coef, b
        for d in (1, 2, 4):
            if asc:
                ok = row >= d
                A_sh = jnp.where(ok, pltpu.roll(A, d, 0), 1.0)
                B_sh = jnp.where(ok, pltpu.roll(B, d, 0), 0.0)
            else:
                ok = row < 8 - d
                A_sh = jnp.where(ok, pltpu.roll(A, 8 - d, 0), 1.0)
                B_sh = jnp.where(ok, pltpu.roll(B, 8 - d, 0), 0.0)
            B = B + A * B_sh
            A = A * A_sh
        h = A * carry + B
        out_ref[pl.ds(off, 8), :] = h
        last = h[7:8, :] if asc else h[0:1, :]
        if mode == 'lam':
            last = last * (a[7:8, :] if asc else a[0:1, :])
        return jnp.broadcast_to(last, (8, HEAD_DIM))

    def body(i, carries):
        return tuple(step(i, c_, *ch) for c_, ch in zip(carries, chains))

    lax.fori_loop(0, nT, body, tuple(jnp.zeros((8, HEAD_DIM), F32) for _ in chains))


def _lru_gates(xc, wa, ba, wx, bx, sp):
    xb = xc.astype(BF)
    r = _sig(jnp.dot(xb, wa, preferred_element_type=F32) + ba)
    i = _sig(jnp.dot(xb, wx, preferred_element_type=F32) + bx)
    a = jnp.exp(-LRU_C * r * sp)
    m = jnp.sqrt(1.0 - a * a)
    return r, i, a, m


def lru_fwd(dm, P, conv_w, conv_b, wa, ba, wx, bx, sp):
    T, D, LB = dm.T, dm.D, dm.LB
    W = dm.LBD
    R = _pick(T, [272, 256, 128, 64, 8])
    lxb = dm.OFF_LX // W

    def body(lx_ref, cw_ref, cb_ref, wa_ref, ba_ref, wx_ref, bx_ref, sp_ref, hf_ref, hb_ref, xc_ref, a_ref):
        taps = _conv_taps(dm, lx_ref[...])
        xc = cb_ref[...]
        for k in range(4):
            xc = xc + taps[k] * cw_ref[k:k + 1, :]
        xc_ref[...] = xc
        h_refs = (hf_ref, hb_ref)

        def chunk(ci, _):
            off = pl.multiple_of(ci * R, 8)
            x = xc_ref[pl.ds(off, R), :]
            for d in range(2):
                r, i, a, m = _lru_gates(x, wa_ref[d].astype(BF), ba_ref[d], wx_ref[d].astype(BF), bx_ref[d], sp_ref[d])
                a_ref[d, pl.ds(off, R), :] = a
                h_refs[d][pl.ds(off, R), :] = m * i * x
            return 0

        lax.fori_loop(0, T // R, chunk, 0)
        _scan_tiles(dm, [(True, False, a_ref.at[0], hf_ref, hf_ref, 'h'), (False, True, a_ref.at[1], hb_ref, hb_ref, 'h')])

    strip = lambda j: (0, j)
    vec = pl.BlockSpec((2, 1, W), lambda j: (0, 0, j))
    mat = pl.BlockSpec((2, None, W, W), lambda j: (0, j, 0, 0))
    return pl.pallas_call(
        body, grid=(LB,), name="lru_fwd",
        in_specs=[pl.BlockSpec((T, W), lambda j: (0, lxb + j)),
                  pl.BlockSpec((4, W), strip), pl.BlockSpec((1, W), strip), mat, vec, mat, vec, vec],
        out_specs=[pl.BlockSpec((T, W), strip)] * 2, out_shape=[_sds((T, D), F32)] * 2,
        scratch_shapes=[pltpu.VMEM((T, W), F32), pltpu.VMEM((2, T, W), F32)], compiler_params=_cparams(),
    )(P, conv_w, conv_b, wa, ba, wx, bx, sp)


def lru_bwd(dm, P, dh, hf, hb, conv_w, conv_b, wa, ba, wx, bx, sp, sg, dP):
    T, C, D, LB = dm.T, dm.C, dm.D, dm.LB
    W = dm.LBD
    R = _pick(T, [272, 256, 128, 64, 8])
    lxb = dm.OFF_LX // W

    def body(lx_ref, dh_ref, hf_ref, hb_ref, cw_ref, cb_ref, wa_ref, ba_ref, wx_ref, bx_ref, sp_ref, sg_ref, _dp_any,
             dlx_ref, dwa_ref, dba_ref, dwx_ref, dbx_ref, dlam_ref, dcw_ref, dcb_ref,
             xc_ref, a_ref, lam_ref, hp_ref, dxc_ref):
        lx = lx_ref[...]
        taps = _conv_taps(dm, lx)
        xc = cb_ref[...]
        for k in range(4):
            xc = xc + taps[k] * cw_ref[k:k + 1, :]
        xc_ref[...] = xc

        def gates(d, x):
            return _lru_gates(x, wa_ref[d].astype(BF), ba_ref[d], wx_ref[d].astype(BF), bx_ref[d], sp_ref[d])

        def chunk_a(ci, _):
            off = pl.multiple_of(ci * R, 8)
            x = xc_ref[pl.ds(off, R), :]
            for d in range(2):
                a_ref[d, pl.ds(off, R), :] = gates(d, x)[2]
            return 0

        lax.fori_loop(0, T // R, chunk_a, 0)
        _scan_tiles(dm, [(False, False, a_ref.at[0], dh_ref, lam_ref.at[0], 'lam'),
                         (True, True, a_ref.at[1], dh_ref, lam_ref.at[1], 'lam')])
        t = lax.broadcasted_iota(jnp.int32, (T, 1), 0)
        hp_ref[0] = jnp.where(t == 0, 0.0, pltpu.roll(hf_ref[...], 1, 0))
        hv = hb_ref[...]
        hp_ref[1] = jnp.where(t == C - 1, 0.0, jnp.where(t == T - 1, jnp.broadcast_to(hv[0:1, :], hv.shape),
                                                         pltpu.roll(hv, T - 1, 0)))

        def chunk_b(d):
            wa_, wx_ = wa_ref[d].astype(BF), wx_ref[d].astype(BF)

            def run(ci, carry):
                dwa, dwx, dba, dbx, dlam = carry
                off = pl.multiple_of(ci * R, 8)
                x = xc_ref[pl.ds(off, R), :]
                r, i, a, m = gates(d, x)
                lam = lam_ref[d, pl.ds(off, R), :]
                da = lam * hp_ref[d, pl.ds(off, R), :] - lam * (i * x) * a / m
                dloga = da * a
                dza = dloga * (-LRU_C) * sp_ref[d] * r * (1.0 - r)
                dzx = lam * m * x * i * (1.0 - i)
                dzab, dzxb = dza.astype(BF), dzx.astype(BF)
                xb = x.astype(BF)
                dxc = lam * m * i
                dxc = dxc + lax.dot_general(dzab, wa_, (NT, ((), ())), preferred_element_type=F32)
                dxc = dxc + lax.dot_general(dzxb, wx_, (NT, ((), ())), preferred_element_type=F32)
                if d == 0:
                    dxc_ref[pl.ds(off, R), :] = dxc
                else:
                    dxc_ref[pl.ds(off, R), :] += dxc
                dwa = dwa + lax.dot_general(xb, dzab, (TN, ((), ())), preferred_element_type=F32)
                dwx = dwx + lax.dot_general(xb, dzxb, (TN, ((), ())), preferred_element_type=F32)
                dba = dba + jnp.sum(dza, axis=0, keepdims=True)
                dbx = dbx + jnp.sum(dzx, axis=0, keepdims=True)
                dlam = dlam + jnp.sum(dloga * LRU_C * r, axis=0, keepdims=True)
                return dwa, dwx, dba, dbx, dlam

            z = jnp.zeros((W, W), F32)
            zv = jnp.zeros((1, W), F32)
            dwa, dwx, dba, dbx, dlam = lax.fori_loop(0, T // R, run, (z, z, zv, zv, zv))
            dwa_ref[d] = dwa
            dwx_ref[d] = dwx
            dba_ref[d] = dba
            dbx_ref[d] = dbx
            dlam_ref[d] = dlam * sg_ref[d]

        chunk_b(0)
        chunk_b(1)
        dxc = dxc_ref[...]
        dcb_ref[...] = jnp.sum(dxc, axis=0, keepdims=True)
        dcw_ref[...] = jnp.concatenate([jnp.sum(dxc * taps[k], axis=0, keepdims=True) for k in range(4)], axis=0)
        valid = [(t < T - 2) & ((t >= C) | (t < C - 2)), (t < T - 1) & ((t >= C) | (t < C - 1)), None,
                 (t != 0) & (t != C)]
        shifts = [T - 2, T - 1, 0, 1]
        dlx = dxc * cw_ref[2:3, :]
        for k in (0, 1, 3):
            dlx = dlx + jnp.where(valid[k], pltpu.roll(dxc, shifts[k], 0), 0.0) * cw_ref[k:k + 1, :]
        dlx_ref[...] = dlx.astype(dlx_ref.dtype)

    strip = lambda j: (0, j)
    sspec = pl.BlockSpec((T, W), strip)
    vec = pl.BlockSpec((2, 1, W), lambda j: (0, 0, j))
    mat = pl.BlockSpec((2, None, W, W), lambda j: (0, j, 0, 0))
    ovec = pl.BlockSpec((2, 1, W), lambda j: (0, 0, j))
    return pl.pallas_call(
        body, grid=(LB,), name="lru_bwd",
        in_specs=[pl.BlockSpec((T, W), lambda j: (0, lxb + j)), sspec, sspec, sspec,
                  pl.BlockSpec((4, W), strip), pl.BlockSpec((1, W), strip), mat, vec, mat, vec, vec, vec, ANY],
        out_specs=[pl.BlockSpec((T, W), lambda j: (0, lxb + j)), mat, ovec, mat, ovec, ovec,
                   pl.BlockSpec((4, W), strip), pl.BlockSpec((1, W), strip)],
        out_shape=[_sds(dP.shape, BF), _sds((2, LB, W, W), F32), _sds((2, 1, D), F32), _sds((2, LB, W, W), F32),
                   _sds((2, 1, D), F32), _sds((2, 1, D), F32), _sds((4, D), F32), _sds((1, D), F32)],
        scratch_shapes=[pltpu.VMEM((T, W), F32), pltpu.VMEM((2, T, W), F32), pltpu.VMEM((2, T, W), F32),
                        pltpu.VMEM((2, T, W), F32), pltpu.VMEM((T, W), F32)],
        input_output_aliases={12: 0}, compiler_params=_cparams(),
    )(P, dh, hf, hb, conv_w, conv_b, wa, ba, wx, bx, sp, sg, dP)


def merge_fwd(dm, P, attn, hf, hb):
    S, D, bm, cw, nCb = dm.S, dm.D, dm.bm, dm.cw, dm.nCb

    def fn(ids, lg_ref, ga_ref, gl_ref, at_ref, hf_ref, hb_ref):
        ge, _ = _gelu(lg_ref[...])
        lru = (hf_ref[...] + hb_ref[...]) * ge
        return [_sig(ga_ref[...]) * at_ref[...] + _sig(gl_ref[...]) * lru]

    pspec = lambda off: pl.BlockSpec((bm, cw), lambda i, j: (i + nCb, off // cw + j))
    tspec = pl.BlockSpec((bm, cw), lambda i, j: (i + nCb, j))
    sspec = pl.BlockSpec((bm, cw), lambda i, j: (i, j))
    return ew_call(
        "merge_fwd", (dm.nSb, D // cw), fn,
        [(P, pspec(dm.OFF_LG)), (P, pspec(dm.OFF_GA)), (P, pspec(dm.OFF_GL)), (attn, sspec), (hf, tspec), (hb, tspec)],
        [(_sds((S, D), BF), sspec, False)])[0]


def merge_bwd(dm, dmg, P, attn, hf, hb):
    S, T, D, bm, cw, nCb = dm.S, dm.T, dm.D, dm.bm, dm.cw, dm.nCb
    nj = D // cw

    def body(dm_ref, lg_ref, ga_ref, gl_ref, at_ref, hf_ref, hb_ref, dp_ref, da_ref, dh_ref, buf, sems):
        i, j = pl.program_id(0), pl.program_id(1)
        lat = i >= nCb
        d = jnp.where(lat, dm_ref[...].astype(F32), 0.0)
        lg = lg_ref[...]
        ge, th = _gelu(lg)
        hs = hf_ref[...] + hb_ref[...]
        sa, sl = _sig(ga_ref[...]), _sig(gl_ref[...])
        at = jnp.where(lat, at_ref[...], 0.0)
        dlru = d * sl
        buf[0] = (dlru * hs * _gelu_grad(lg, th)).astype(BF)
        buf[1] = (d * at * sa * (1.0 - sa)).astype(BF)
        buf[2] = (d * hs * ge * sl * (1.0 - sl)).astype(BF)
        da_ref[...] = (d * sa).astype(BF)
        dh_ref[...] = dlru * ge
        copies = []
        for g, off in enumerate((dm.OFF_LG, dm.OFF_GA, dm.OFF_GL)):
            col = pl.multiple_of(off + j * cw, 128)
            cp = pltpu.make_async_copy(buf.at[g], dp_ref.at[pl.ds(pl.multiple_of(i * bm, 8), bm), pl.ds(col, cw)],
                                       sems.at[g])
            cp.start()
            copies.append(cp)
        for cp in copies:
            cp.wait()

    pspec = lambda off: pl.BlockSpec((bm, cw), lambda i, j: (i, off // cw + j))
    tspec = pl.BlockSpec((bm, cw), lambda i, j: (i, j))
    lspec = pl.BlockSpec((bm, cw), lambda i, j: (jnp.maximum(i - nCb, 0), j))
    return pl.pallas_call(
        body, grid=(dm.nTb, nj), name="merge_bwd",
        in_specs=[lspec, pspec(dm.OFF_LG), pspec(dm.OFF_GA), pspec(dm.OFF_GL), lspec, tspec, tspec],
        out_specs=[ANY, tspec, tspec],
        out_shape=[_sds((T, dm.INW), BF), _sds((T, D), BF), _sds((T, D), F32)],
        scratch_shapes=[pltpu.VMEM((3, bm, cw), BF), pltpu.SemaphoreType.DMA((3,))],
        compiler_params=_cparams(),
    )(dmg, P, P, P, attn, hf, hb)


def final_loss(dm, x3, gfin, target):
    S, D, bm = dm.S, dm.D, dm.bm

    def fn(ids, x_ref, g_ref, t_ref):
        xv = x_ref[...]
        g = g_ref[...]
        r = lax.rsqrt(jnp.mean(xv * xv, axis=-1, keepdims=True) + EPS)
        xn = xv * r
        err = xn * g - t_ref[...]
        loss = 0.5 * jnp.sum(jnp.mean(err * err, axis=-1, keepdims=True), axis=0, keepdims=True)
        dy = err / D
        dxn = dy * g
        dx = r * (dxn - xn * jnp.mean(dxn * xn, axis=-1, keepdims=True))
        return [jnp.broadcast_to(loss, (1, 128)), dx, jnp.sum(dy * xn, axis=0, keepdims=True)]

    row = pl.BlockSpec((bm, D), lambda i: (i, 0))
    vec = pl.BlockSpec((1, D), lambda i: (0, 0))
    return ew_call(
        "final_loss", (dm.nSb,), fn, [(x3, row), (gfin, vec), (target, row)],
        [(_sds((1, 128), F32), pl.BlockSpec((1, 128), lambda i: (0, 0)), True), (_sds((S, D), F32), row, False),
         (_sds((1, D), F32), vec, True)], first=lambda ids: ids[0] == 0)


def local_step(dm, x, ctx, target, modv, norm_g3, gfin, gq, gk, conv_w, conv_b, wa, ba, wx, bx, lam, wbuf, where):
    S, C, T, D, NS, F4, W4 = dm.S, dm.C, dm.T, dm.D, dm.NS, dm.F4, dm.W4
    Ds = D // NS
    wb, nsub = dm.wb, dm.nsub
    cosf, sinf = rope_tables(dm)
    sp = jax.nn.softplus(-lam)
    sg = jax.nn.sigmoid(-lam)
    ident = lambda ids, accs, ex: list(accs)
    mT, mS, kT, kS = dm.mT, dm.mS, dm.kT, dm.kS
    bn = _pick(D, [1024, 512, 256, 128])
    bk = _pick(D, [512, 256, 128])

    wg0, wu0 = comm_call("ag_ffn1", ag_comm([wbuf['wg0'], wbuf['wu0']]))
    h1, xt = normmod_concat_fwd("nm1", dm, ctx, x, norm_g3, modv)
    xt1, a1, u1, s1, f1, land, _ = ffn_fwd("ffn1", dm, h1, xt, wg0, wu0, None, modv, 2, True,
                                           comm_up=ag_comm([wbuf['wd0'], wbuf['w_in'], wbuf['w_out']]))
    wd0, w_in, w_out = land[0], land[1], land[2].reshape(D, D)
    h2 = normmod_fwd("nm2", dm, xt1, norm_g3, 1, modv, True)
    P = fused_mm(
        "w_in", (T // mT, NS * nsub, 1),
        [(h2, pl.BlockSpec((mT, D), lambda i, j, k: (i, 0))),
         (w_in, pl.BlockSpec((None, D, wb), lambda i, j, k: (j // nsub, 0, j % nsub)))],
        [(0, 1, NN, 0)], [(mT, wb)], ident,
        [(_sds((T, dm.INW), F32), pl.BlockSpec((mT, wb), lambda i, j, k: (i, j)))])[0]
    qr, kr, vb = qk_prep(dm, P, gq, gk, cosf, sinf)
    (attn,), (wg1, wu1, wd1) = attention_fwd(dm, qr, kr, vb, comm=ag_comm([wbuf['wg1'], wbuf['wu1'], wbuf['wd1']]))
    hf, hb = lru_fwd(dm, P, conv_w, conv_b, wa, ba, wx, bx, sp)
    mg = merge_fwd(dm, P, attn, hf, hb)

    def epi_o(ids, accs, ex):
        o = accs[0]
        return [ex[0][...] + ex[1][...] * o, o]

    rb, nCb = dm.bm, dm.nCb
    x2, o2 = fused_mm(
        "w_out", (D // bn, S // rb, 1),
        [(mg, pl.BlockSpec((rb, D), lambda j, i, k: (i, 0))), (w_out, pl.BlockSpec((D, bn), lambda j, i, k: (0, j)))],
        [(0, 1, NN, 0)], [(rb, bn)], epi_o,
        [(_sds((S, D), F32), pl.BlockSpec((rb, bn), lambda j, i, k: (i, j))),
         (_sds((S, D), BF), pl.BlockSpec((rb, bn), lambda j, i, k: (i, j)))],
        extras=[(xt1, pl.BlockSpec((rb, bn), lambda j, i, k: (i + nCb, j))),
                (modv, pl.BlockSpec((None, None, 1, bn), lambda j, i, k: (1, 5, 0, j)))])
    h3 = normmod_fwd("nm3", dm, x2, norm_g3, 2, modv, False)
    x3, a3, u3, s3, f3, _, _ = ffn_fwd("ffn2", dm, h3, x2, wg1, wu1, wd1, modv, 8, False)
    loss, dx3, dgfin = final_loss(dm, x3, gfin, target)

    df3, dg3 = gate_bwd("gate3", dm, dx3, f3, modv, 8, FFN_RES, False)
    dh3, dwg1, dwu1, dwd1, _ = ffn_bwd("ffn2b", dm, df3, h3, a3, u3, s3, wg1, wu1, wd1, False)
    dx2, dsh3, dsc3, dgn3 = normmod_bwd("nm3b", dm, dh3, x2, dx3, norm_g3, 2, modv, False, False)
    do2, dg2 = gate_bwd("gate2", dm, dx2, o2, modv, 5, 1.0, False)
    keep = {}

    def host_a(key, comm):
        if key == 'p1':
            (keep['dmg'],), landed = fused_mm(
                "w_out_dx", (S // mS, D // bn, 1),
                [(do2, pl.BlockSpec((mS, D), lambda i, j, k: (i, 0))),
                 (w_out, pl.BlockSpec((bn, D), lambda i, j, k: (j, 0)))],
                [(0, 1, NT, 0)], [(mS, bn)], ident,
                [(_sds((S, D), BF), pl.BlockSpec((mS, bn), lambda i, j, k: (i, j)))], comm=comm)
            return landed
        keep['dqkv'], landed = attention_bwd(dm, qr, kr, vb, attn, keep['dattn'], comm=comm)
        return landed

    gots_a = host_a('p1', rs_p1_comm([dwg1, dwu1, dwd1]))
    dmg = keep['dmg']
    dw_out = fused_mm(
        "w_out_dw", (D // bn, D // bn, S // kS),
        [(mg, pl.BlockSpec((kS, bn), lambda i, j, k: (k, i))), (do2, pl.BlockSpec((kS, bn), lambda i, j, k: (k, j)))],
        [(0, 1, TN, 0)], [(bn, bn)], ident,
        [(_sds((D, D), BF), pl.BlockSpec((bn, bn), lambda i, j, k: (i, j)))])[0]
    dP, dattn, dhs = merge_bwd(dm, dmg, P, attn, hf, hb)
    keep['dattn'] = dattn
    pairs_a = [add_pair("rs_add_" + n_, g_, got_, where)
               for n_, g_, got_ in zip(('wg1', 'wu1', 'wd1'), (dwg1, dwu1, dwd1), gots_a)]
    land_a = host_a('p2', rs_p2_comm([p_[0] for p_ in pairs_a], [p_[1] for p_ in pairs_a]))
    dq, dk, dv = keep['dqkv']
    dP, dwa, dba, dwx, dbx, dlam, dcw, dcb = lru_bwd(dm, P, dhs, hf, hb, conv_w, conv_b, wa, ba, wx, bx, sp, sg, dP)
    dP, dgq, dgk = qk_prep_bwd(dm, dq, dk, dv, P, gq, gk, cosf, sinf, dP)
    g_wg = sum_slots_into("rs_sum_wg1", land_a[0], where, None, (2, D, F4), 1)
    g_wu = sum_slots_into("rs_sum_wu1", land_a[1], where, None, (2, D, F4), 1)
    g_wd = sum_slots_into("rs_sum_wd1", land_a[2], where, None, (2, F4, D), 1)
    (dh2,), (g_wg, g_wu, g_wd) = fused_mm(
        "w_in_dx", (T // mT, D // bn, NS),
        [(dP, pl.BlockSpec((mT, W4), lambda i, j, k: (i, k))),
         (w_in, pl.BlockSpec((None, bn, W4), lambda i, j, k: (k, j, 0)))],
        [(0, 1, NT, 0)], [(mT, bn)], ident,
        [(_sds((T, D), F32), pl.BlockSpec((mT, bn), lambda i, j, k: (i, j)))],
        comm=rs_p3_comm([g_wg, g_wu, g_wd], [(0, 1, D // 2), (1, 1, D // 2), (2, 1, F4 // 2)]))
    dw_in = fused_mm(
        "w_in_dw", (D // bn, NS, T // kT),
        [(h2, pl.BlockSpec((kT, bn), lambda i, j, k: (k, i))), (dP, pl.BlockSpec((kT, W4), lambda i, j, k: (k, j)))],
        [(0, 1, TN, 0)], [(bn, W4)], ident,
        [(_sds((NS, D, W4), BF), pl.BlockSpec((None, bn, W4), lambda i, j, k: (j, i, 0)))])[0]
    dxt1, dsh2, dsc2, dgn2 = normmod_bwd("nm2b", dm, dh2, xt1, dx2, norm_g3, 1, modv, True, True)
    df1, dg1 = gate_bwd("gate1", dm, dxt1, f1, modv, 2, FFN_RES, True)

    tens_b = [dw_in, dw_out.reshape(NS, Ds, D)]

    LBD = D // dm.LB
    mats = [dwa.reshape(1, 2 * dm.LB * LBD, LBD), dwx.reshape(1, 2 * dm.LB * LBD, LBD)]

    def host_dwgu(landed, made):
        pairs = [add_pair("rs_add_" + n_, g_, got_, where) for n_, g_, got_ in zip(('w_in', 'w_out'), tens_b, landed['ds'])]
        return merge_comms([rs_p2_comm([p_[0] for p_ in pairs], [p_[1] for p_ in pairs]), rs_p1_comm([made['dwd']])])

    def host_dh(landed, made):
        g_win = sum_slots_into("rs_sum_w_in", landed['dwgu'][0], where, None, (D, W4), None)
        g_wout = sum_slots_into("rs_sum_w_out", landed['dwgu'][1], where, None, (Ds, D), None)
        part, land = add_pair("rs_add_wd0", made['dwd'], landed['dwgu'][2], where)
        return merge_comms([rs_p3_comm([g_win, g_wout], [(0, None, D // 2), (1, None, Ds // 2)]),
                            rs_p1_comm([made['dwg'], made['dwu']] + mats), rs_p2_comm([part], [land])])

    dh1, dwg0, dwu0, dwd0, landed = ffn_bwd("ffn1b", dm, df1, h1, a1, u1, s1, wg0, wu0, wd0, True,
                                            comms={'ds': lambda L, M: rs_p1_comm(tens_b), 'dwgu': host_dwgu, 'dh': host_dh})
    g_win, g_wout = landed['dh'][:2]
    late = [add_pair("rs_add_" + n_, g_, got_, where)
            for n_, g_, got_ in zip(('wg0', 'wu0', 'lru_wa', 'lru_wx'), [dwg0, dwu0] + mats, landed['dh'][2:6])]
    g_wd = sum_slots_into("rs_sum_wd0", landed['dh'][6], where, g_wd, (2, F4, D), 0)
    grad_x, dsh1, dsc1, dgn1 = normmod_bwd("nm1b", dm, dh1, xt, dxt1, norm_g3, 0, modv, True, False, out_lat_only=True)

    dmod = jnp.concatenate([dsh1, dsc1, dg1, dsh2, dsc2, _lat(dg2), _lat(dsh3), _lat(dsc3), _lat(dg3)], axis=1)
    dnorm = jnp.stack([dgn1[0, 0] + dgn1[1, 0], dgn2[0, 0] + dgn2[1, 0], dgn3[1, 0]], axis=0)
    small = dict(norm_g=dnorm, q_norm_g=dgq, k_norm_g=dgk, conv_w=dcw, conv_b=dcb,
                 lru_ba=dba.reshape(2, D), lru_bx=dbx.reshape(2, D), lru_lambda=dlam.reshape(2, D), final_norm_g=dgfin)
    reduced = dict(ffn_wg=g_wg, ffn_wu=g_wu, ffn_wd=g_wd, w_in=g_win, w_out=g_wout)
    return loss, grad_x, dmod, small, reduced, late


def _lat(v):
    return jnp.concatenate([jnp.zeros_like(v[:1]), v[1:]], axis=0)


def _me():
    return lax.axis_index("x"), lax.axis_index("y"), lax.axis_index("c")


def allgather8(name, v):
    def body(v_ref, out_ref, send_sems, recv_sems, local_sem):
        x, y, c = _me()
        me = 4 * x + 2 * y + c
        mine = pltpu.make_async_copy(v_ref, out_ref.at[me], local_sem)
        mine.start()
        copies = []
        for k in range(1, 8):
            peer = (x ^ ((k >> 2) & 1), y ^ ((k >> 1) & 1), c ^ (k & 1))
            cp = pltpu.make_async_remote_copy(src_ref=v_ref, dst_ref=out_ref.at[me], send_sem=send_sems.at[k - 1],
                                              recv_sem=recv_sems.at[k - 1], device_id=peer, device_id_type=MESH)
            cp.start()
            copies.append(cp)
        for k in range(1, 8):
            peer = (x ^ ((k >> 2) & 1), y ^ ((k >> 1) & 1), c ^ (k & 1))
            pltpu.make_async_remote_copy(src_ref=v_ref, dst_ref=out_ref.at[me ^ k], send_sem=send_sems.at[k - 1],
                                         recv_sem=recv_sems.at[k - 1], device_id=peer, device_id_type=MESH).wait_recv()
        for cp in copies:
            cp.wait_send()
        mine.wait()

    return pl.pallas_call(
        body, name=name, out_shape=_sds((8,) + v.shape, v.dtype), in_specs=[ANY], out_specs=ANY,
        scratch_shapes=[pltpu.SemaphoreType.DMA((7,)), pltpu.SemaphoreType.DMA((7,)), pltpu.SemaphoreType.DMA],
    )(v)


def _chips(x, y):
    chips = [(1 - x, y), (x, 1 - y), (1 - x, 1 - y)]
    return chips, [2 * cx + cy for cx, cy in chips]


def ag_comm(bufs):
    n = len(bufs)

    def parts(outs):
        x, y, c = _me()
        chips, slots = _chips(x, y)
        return x, y, c, 2 * x + y, (x, y, 1 - c), chips, slots

    def ici(outs, t, j, send_sems, recv_sems, src_slot, off):
        x, y, c, s, sib, chips, slots = parts(outs)
        H = outs[t].shape[1] // 2
        blk = outs[t].at[src_slot, pl.ds(c * H, H)]
        return pltpu.make_async_remote_copy(
            src_ref=blk, dst_ref=blk, send_sem=send_sems.at[off + 6 * t + j], recv_sem=recv_sems.at[off + 6 * t + j],
            device_id=(chips[j][0], chips[j][1], c), device_id_type=MESH)

    def d2d(outs, t, j, send_sems, recv_sems, half, off):
        x, y, c, s, sib, chips, slots = parts(outs)
        H = outs[t].shape[1] // 2
        blk = outs[t].at[slots[j], pl.ds(half * H, H)]
        return pltpu.make_async_remote_copy(
            src_ref=blk, dst_ref=blk, send_sem=send_sems.at[off + 6 * t + 3 + j],
            recv_sem=recv_sems.at[off + 6 * t + 3 + j], device_id=sib, device_id_type=MESH)

    def start(reads, outs, send_sems, recv_sems, off=0):
        x, y, c, s, sib, chips, slots = parts(outs)
        for t in range(n):
            for j in range(3):
                ici(outs, t, j, send_sems, recv_sems, s, off).start()

    def finish(reads, outs, send_sems, recv_sems, off=0):
        x, y, c, s, sib, chips, slots = parts(outs)
        for t in range(n):
            for j in range(3):
                ici(outs, t, j, send_sems, recv_sems, slots[j], off).wait_recv()
                d2d(outs, t, j, send_sems, recv_sems, c, off).start()
        for t in range(n):
            for j in range(3):
                d2d(outs, t, j, send_sems, recv_sems, 1 - c, off).wait_recv()
        for t in range(n):
            for j in range(3):
                ici(outs, t, j, send_sems, recv_sems, s, off).wait_send()
                d2d(outs, t, j, send_sems, recv_sems, c, off).wait_send()

    return Comm([], bufs, 6 * n, start, finish)


def rs_p1_comm(tensors):
    n = len(tensors)

    def copy(ins, gots, t, send_sems, recv_sems, off):
        x, y, c = _me()
        H = ins[t].shape[1] // 2
        return pltpu.make_async_remote_copy(
            src_ref=ins[t].at[:, pl.ds((1 - c) * H, H)], dst_ref=gots[t], send_sem=send_sems.at[off + t],
            recv_sem=recv_sems.at[off + t], device_id=(x, y, 1 - c), device_id_type=MESH)

    def start(ins, gots, send_sems, recv_sems, off=0):
        for t in range(n):
            copy(ins, gots, t, send_sems, recv_sems, off).start()

    def finish(ins, gots, send_sems, recv_sems, off=0):
        for t in range(n):
            copy(ins, gots, t, send_sems, recv_sems, off).wait_recv()
        for t in range(n):
            copy(ins, gots, t, send_sems, recv_sems, off).wait_send()

    half = lambda t: _sds((t.shape[0], t.shape[1] // 2) + t.shape[2:], t.dtype)
    return Comm(tensors, [half(t) for t in tensors], n, start, finish)


def rs_p2_comm(partials, landeds):
    n = len(partials)

    def start(ins, outs, send_sems, recv_sems, off=0):
        x, y, c = _me()
        s = 2 * x + y
        chips, slots = _chips(x, y)
        for t in range(n):
            for j, chip in enumerate(chips):
                src = ins[t].at[slots[j]] if ins[t].shape[0] == 4 else ins[t].at[0]
                pltpu.make_async_remote_copy(
                    src_ref=src, dst_ref=outs[t].at[s], send_sem=send_sems.at[off + 3 * t + j],
                    recv_sem=recv_sems.at[off + 3 * t + j], device_id=(chip[0], chip[1], c), device_id_type=MESH).start()

    def finish(ins, outs, send_sems, recv_sems, off=0):
        x, y, c = _me()
        s = 2 * x + y
        chips, slots = _chips(x, y)
        for t in range(n):
            for j, chip in enumerate(chips):
                dst = outs[t].at[slots[j]]
                pltpu.make_async_remote_copy(
                    src_ref=dst, dst_ref=dst, send_sem=send_sems.at[off + 3 * t + j],
                    recv_sem=recv_sems.at[off + 3 * t + j], device_id=(chip[0], chip[1], c), device_id_type=MESH).wait_recv()
        for t in range(n):
            for j, chip in enumerate(chips):
                src = ins[t].at[slots[j]] if ins[t].shape[0] == 4 else ins[t].at[0]
                pltpu.make_async_remote_copy(
                    src_ref=src, dst_ref=outs[t].at[s], send_sem=send_sems.at[off + 3 * t + j],
                    recv_sem=recv_sems.at[off + 3 * t + j], device_id=(chip[0], chip[1], c), device_id_type=MESH).wait_send()

    return Comm(partials, landeds, 3 * n, start, finish)


def rs_p3_comm(greds, plan):
    n = len(plan)

    def copy(outs, t, send_sems, recv_sems, half, off):
        x, y, c = _me()
        oi, li, H = plan[t]
        dst = outs[oi] if li is None else outs[oi].at[li]
        blk = dst.at[pl.ds((c if half == 0 else 1 - c) * H, H)]
        return pltpu.make_async_remote_copy(
            src_ref=blk, dst_ref=blk, send_sem=send_sems.at[off + t], recv_sem=recv_sems.at[off + t],
            device_id=(x, y, 1 - c), device_id_type=MESH)

    def start(reads, outs, send_sems, recv_sems, off=0):
        for t in range(n):
            copy(outs, t, send_sems, recv_sems, 0, off).start()

    def finish(reads, outs, send_sems, recv_sems, off=0):
        for t in range(n):
            copy(outs, t, send_sems, recv_sems, 1, off).wait_recv()
        for t in range(n):
            copy(outs, t, send_sems, recv_sems, 0, off).wait_send()

    return Comm([], greds, n, start, finish)


def _rows_block(rows, cols, nbytes=1 << 20):
    bm = 8
    while bm * 2 * cols * 4 <= nbytes and rows % (bm * 2) == 0:
        bm *= 2
    return bm


def cast_into_slot(name, w, where, layer=None):
    rows, W = w.shape[-2:]
    bm = _rows_block(rows, W)

    def body(p_ref, w_ref, o_ref):
        o_ref[...] = w_ref[...].astype(BF)

    if layer is None:
        ispec = pl.BlockSpec((bm, W), lambda i, p: (i, 0))
    else:
        ispec = pl.BlockSpec((None, bm, W), lambda i, p: (layer, i, 0))
    return pl.pallas_call(
        body, name=name, out_shape=_sds((4, rows, W), BF), compiler_params=_cparams(),
        grid_spec=pltpu.PrefetchScalarGridSpec(
            num_scalar_prefetch=1, grid=(rows // bm,), in_specs=[ispec],
            out_specs=pl.BlockSpec((None, bm, W), lambda i, p: (p[1], i, 0))),
    )(where, w)


def add_pair(name, g, got, where):
    K, R, W = g.shape
    H = R // 2
    bm = _rows_block(H, W)
    nh = H // bm

    def body(p_ref, g_ref, got_ref, part_ref, land_ref):
        k = pl.program_id(1)
        v = (g_ref[...].astype(F32) + got_ref[...].astype(F32)).astype(part_ref.dtype)
        part_ref[...] = v
        own = (k == p_ref[1]) if K == 4 else (k == 0)

        @pl.when(own)
        def _():
            land_ref[...] = v

    return pl.pallas_call(
        body, name=name, out_shape=[_sds((K, H, W), g.dtype), _sds((4, H, W), g.dtype)], compiler_params=_cparams(),
        grid_spec=pltpu.PrefetchScalarGridSpec(
            num_scalar_prefetch=1, grid=(nh, K),
            in_specs=[pl.BlockSpec((None, bm, W), lambda i, k, p: (k, p[0] * nh + i, 0)),
                      pl.BlockSpec((None, bm, W), lambda i, k, p: (k, i, 0))],
            out_specs=[pl.BlockSpec((None, bm, W), lambda i, k, p: (k, i, 0)),
                       pl.BlockSpec((None, bm, W), lambda i, k, p: (p[1], i, 0))]),
    )(where, g, got)


def sum_slots_into(name, landed, where, dest, dest_shape, li):
    K, H, W = landed.shape
    bm = _rows_block(H, 2 * W)
    nh = H // bm

    def body(*refs):
        r, o_ref = refs[1], refs[-1]
        acc = r[0].astype(F32)
        for k in range(1, K):
            acc = acc + r[k].astype(F32)
        o_ref[...] = acc

    if li is None:
        ospec = pl.BlockSpec((bm, W), lambda i, p: (p[0] * nh + i, 0))
    else:
        ospec = pl.BlockSpec((None, bm, W), lambda i, p: (li, p[0] * nh + i, 0))
    in_specs = [pl.BlockSpec((K, bm, W), lambda i, p: (0, i, 0))]
    args = [where, landed]
    aliases = {}
    if dest is not None:
        in_specs.append(ANY)
        args.append(dest)
        aliases = {2: 0}
    return pl.pallas_call(
        body, name=name, out_shape=_sds(dest_shape, F32), compiler_params=_cparams(), input_output_aliases=aliases,
        grid_spec=pltpu.PrefetchScalarGridSpec(num_scalar_prefetch=1, grid=(nh,), in_specs=in_specs, out_specs=ospec),
    )(*args)


def sum_slots(name, a):
    K, H, W = a.shape
    bm = _rows_block(H, W * K // 2)

    def fn(ids, r):
        acc = r[0]
        for k in range(1, K):
            acc = acc + r[k]
        return [acc]

    return ew_call(name, (H // bm,), fn, [(a, pl.BlockSpec((K, bm, W), lambda i: (0, i, 0)))],
                   [(_sds((H, W), F32), pl.BlockSpec((bm, W), lambda i: (i, 0)), False)])[0]


def _adamw_math(w, g, m, v):
    bc1 = 1.0 - ADAM_B1 ** ADAM_STEP
    bc2 = 1.0 - ADAM_B2 ** ADAM_STEP
    mn = ADAM_B1 * m + (1.0 - ADAM_B1) * g
    vn = ADAM_B2 * v + (1.0 - ADAM_B2) * (g * g)
    m_hat = mn / bc1
    v_hat = vn / bc2
    delta = -ADAM_LR * (m_hat / (jnp.sqrt(v_hat) + ADAM_EPS) + ADAM_WD * w)
    return delta, mn, vn


def adamw(name, w, g, m, v, comm=None):
    shape = w.shape
    flat = lambda t: t.reshape(-1, shape[-1])
    w2, g2, m2, v2 = flat(w), flat(g), flat(m), flat(v)
    bm = _rows_block(w2.shape[0], w2.shape[1])

    def fn(ids, w_ref, g_ref, m_ref, v_ref):
        return list(_adamw_math(w_ref[...], g_ref[...], m_ref[...], v_ref[...]))

    spec = pl.BlockSpec((bm, w2.shape[1]), lambda i: (i, 0))
    res = ew_call(name, (w2.shape[0] // bm,), fn, [(w2, spec), (g2, spec), (m2, spec), (v2, spec)],
                  [(_sds(w2.shape, F32), spec, False)] * 3, comm=comm)
    if comm is not None:
        return [o.reshape(shape) for o in res[0]], res[1]
    return [o.reshape(shape) for o in res]


def adamw_many(name, params):
    n = len(params)
    shapes = [p_[0].shape for p_ in params]
    two_d = lambda t: t.reshape(-1, t.shape[-1])
    args = [two_d(t) for p_ in params for t in p_]

    def body(*refs):
        ins, outs = refs[:4 * n], refs[4 * n:]
        for q in range(n):
            w_ref, g_ref, m_ref, v_ref = ins[4 * q:4 * q + 4]
            for o_ref, val in zip(outs[3 * q:3 * q + 3], _adamw_math(w_ref[...], g_ref[...], m_ref[...], v_ref[...])):
                o_ref[...] = val

    full = lambda a: pl.BlockSpec(a.shape, lambda i: (0, 0))
    out_shape = [_sds(args[4 * q].shape, F32) for q in range(n) for _ in range(3)]
    res = hosted_call(body, name=name, grid=(1,), in_specs=[full(a) for a in args],
                      out_specs=[full(o) for o in out_shape], out_shape=out_shape, args=args)
    return [tuple(res[3 * q + r].reshape(shapes[q]) for r in range(3)) for q in range(n)]


def dmod_pack(gd):
    N = gd.shape[-1]
    bn = _pick(N, [4608, 2304, 1152, 1024, 512, 256, 128])

    def fn(ids, r):
        lat = [r[d, 1:2, :] for d in range(8)]
        cs = r[0, 0:1, :]
        for d in range(1, 8):
            cs = cs + r[d, 0:1, :]
        tot = cs
        for d in range(8):
            tot = tot + lat[d]
        return [jnp.concatenate(lat + [cs, jnp.zeros((7, bn), F32)], axis=0), tot]

    return ew_call("dmod_pack", (N // bn,), fn, [(gd, pl.BlockSpec((8, 2, bn), lambda j: (0, 0, j)))],
                   [(_sds((16, N), F32), pl.BlockSpec((16, bn), lambda j: (0, j)), False),
                    (_sds((1, N), F32), pl.BlockSpec((1, bn), lambda j: (0, j)), False)])


def _silu(v):
    return v * _sig(v)


def kernel(x, c, ctx, c_ctx, w_mod, b_mod, norm_g, ffn_wg, ffn_wu, ffn_wd, w_in, w_out, q_norm_g, k_norm_g, conv_w, conv_b, lru_wa, lru_ba, lru_wx, lru_bx, lru_lambda, final_norm_g, loss_target, m_c_ctx, m_w_mod, m_b_mod, m_norm_g, m_ffn_wg, m_ffn_wu, m_ffn_wd, m_w_in, m_w_out, m_q_norm_g, m_k_norm_g, m_conv_w, m_conv_b, m_lru_wa, m_lru_ba, m_lru_wx, m_lru_bx, m_lru_lambda, m_final_norm_g, v_c_ctx, v_w_mod, v_b_mod, v_norm_g, v_ffn_wg, v_ffn_wu, v_ffn_wd, v_w_in, v_w_out, v_q_norm_g, v_k_norm_g, v_conv_w, v_conv_b, v_lru_wa, v_lru_ba, v_lru_wx, v_lru_bx, v_lru_lambda, v_final_norm_g):
    given = dict(locals())
    names = ['c_ctx', 'w_mod', 'b_mod', 'norm_g', 'ffn_wg', 'ffn_wu', 'ffn_wd', 'w_in', 'w_out', 'q_norm_g', 'k_norm_g',
             'conv_w', 'conv_b', 'lru_wa', 'lru_ba', 'lru_wx', 'lru_bx', 'lru_lambda', 'final_norm_g']
    S, D = x.shape[1], x.shape[2]
    C = ctx.shape[1]
    NS = 4
    F4, W4, LB = ffn_wg.shape[-1], w_in.shape[-1], lru_wa.shape[2]
    dm = Dims(S, C, D, F4, W4, NS, LB)
    Ds = D // NS
    Wm = w_mod.shape[-1]
    xi, yi, ci = lax.axis_index("x"), lax.axis_index("y"), lax.axis_index("c")
    slot = 2 * xi + yi
    me = 4 * xi + 2 * yi + ci
    ident = lambda ids, accs, ex: list(accs)

    pack1 = jnp.concatenate([c.reshape(-1), norm_g.reshape(-1), conv_w.reshape(-1), lru_ba.reshape(-1),
                             lru_bx.reshape(-1), lru_lambda.reshape(-1)]).reshape(1, -1)
    g1 = allgather8("ag_small_params", pack1)[:, 0]
    c_all = g1[:, :D]

    def unshard(off, k):
        part = g1[0::2, off:off + k * Ds].reshape(NS, k, Ds)
        return jnp.transpose(part, (1, 0, 2)).reshape(k, D)

    norm_g_f = unshard(D, 3)
    conv_w_f = unshard(D + 3 * Ds, 4)
    ba_f = unshard(D + 7 * Ds, 2)
    bx_f = unshard(D + 9 * Ds, 2)
    lam_f = unshard(D + 11 * Ds, 2)

    call16 = jnp.concatenate([c_all, c_ctx.reshape(1, D), jnp.zeros((7, D), F32)], axis=0)
    b_cols = lax.dynamic_slice(b_mod, (0, slot * Wm), (1, Wm))
    bnm = _pick(Wm, [1536, 1152, 768, 512, 384, 256, 128])
    bkm = _pick(D, [512, 256, 128])
    modp = fused_mm(
        "mod_fwd", (1, Wm // bnm, D // bkm),
        [(call16, pl.BlockSpec((16, bkm), lambda i, j, k: (0, k))),
         (w_mod[0], pl.BlockSpec((bkm, bnm), lambda i, j, k: (k, j)))],
        [(0, 1, NN, 0)], [(16, bnm)], lambda ids, accs, ex: [accs[0] + ex[0][...]],
        [(_sds((16, Wm), F32), pl.BlockSpec((16, bnm), lambda i, j, k: (0, j)))],
        extras=[(b_cols, pl.BlockSpec((1, bnm), lambda i, j, k: (0, j)))], pre={0: _silu})[0]
    gm = allgather8("ag_mod", modp)
    mod_full = jnp.concatenate([gm[0], gm[2], gm[4], gm[6]], axis=1)
    mod_x = lax.dynamic_index_in_dim(mod_full, me, axis=0, keepdims=False)
    modv = jnp.stack([mod_full[8], mod_x]).reshape(2, N_MOD, 1, D)

    where = jnp.stack([ci, slot]).astype(jnp.int32)
    wbuf = {}
    for key, short in (('ffn_wg', 'wg'), ('ffn_wu', 'wu'), ('ffn_wd', 'wd')):
        for l in range(2):
            wbuf[short + str(l)] = cast_into_slot("cast_%s%d" % (short, l), given[key][0], where, l)
    wbuf['w_in'] = cast_into_slot("cast_w_in", w_in[0], where)
    wbuf['w_out'] = cast_into_slot("cast_w_out", w_out[0], where)

    loss_l, grad_x, dmod, small, reduced, late = local_step(
        dm, x[0], ctx[0], loss_target[0], modv, norm_g_f.reshape(3, 1, D), final_norm_g.reshape(1, D),
        q_norm_g, k_norm_g, conv_w_f, conv_b, lru_wa[0], ba_f.reshape(2, 1, D), lru_wx[0], bx_f.reshape(2, 1, D),
        lam_f.reshape(2, 1, D), wbuf, where)
    loss = lax.psum(loss_l[0, 0], ("x", "y", "c"))

    grads = {}
    gd = allgather8("ag_dmod", dmod.reshape(2, N_MOD * D))
    dM, g_bmod = dmod_pack(gd)
    dMc = lax.dynamic_slice(dM, (0, slot * Wm), (16, Wm))
    bmm = _pick(D, [512, 256, 128])
    grads['w_mod'] = fused_mm(
        "w_mod_dw", (D // bmm, Wm // bnm, 1),
        [(call16, pl.BlockSpec((16, bmm), lambda i, j, k: (0, i))), (dMc, pl.BlockSpec((16, bnm), lambda i, j, k: (0, j)))],
        [(0, 1, TN, 0)], [(bmm, bnm)], ident,
        [(_sds((D, Wm), F32), pl.BlockSpec((bmm, bnm), lambda i, j, k: (i, j)))], pre={0: _silu})[0][None]
    grads['b_mod'] = g_bmod

    def epi_cc(ids, accs, ex):
        v = ex[0][...]
        sg = _sig(v)
        return [accs[0] * (sg * (1.0 + v * (1.0 - sg)))]

    pcc = fused_mm(
        "c_ctx_partial", (1, D // bmm, Wm // bnm),
        [(dMc, pl.BlockSpec((16, bnm), lambda i, j, k: (0, k))), (w_mod[0], pl.BlockSpec((bmm, bnm), lambda i, j, k: (j, k)))],
        [(0, 1, NT, 0)], [(16, bmm)], epi_cc,
        [(_sds((16, D), F32), pl.BlockSpec((16, bmm), lambda i, j, k: (0, j)))],
        extras=[(c_ctx.reshape(1, D), pl.BlockSpec((1, bmm), lambda i, j, k: (0, j)))])[0]
    pcc_row = jnp.where(ci == 0, pcc[8], 0.0)

    order = ['q_norm_g', 'k_norm_g', 'conv_b', 'final_norm_g', 'norm_g', 'conv_w', 'lru_ba', 'lru_bx', 'lru_lambda']
    flat = [small[k].reshape(-1) for k in order] + [pcc_row]
    sizes = [f.shape[0] for f in flat]
    tot = sum(sizes)
    LW = 1024
    padded = -(-tot // (8 * LW)) * (8 * LW)
    tiny = jnp.concatenate(flat + [jnp.zeros((padded - tot,), F32)]).reshape(-1, LW)
    summed = sum_slots("tiny_sum", allgather8("ag_tiny_grads", tiny)).reshape(-1)
    offs = {}
    o = 0
    for k, n_ in zip(order + ['c_ctx'], sizes):
        offs[k] = summed[o:o + n_]
        o += n_
    shard = lambda k, rows: lax.dynamic_slice_in_dim(offs[k].reshape(rows, D), slot * Ds, Ds, axis=1)
    grads['c_ctx'] = offs['c_ctx']
    grads['q_norm_g'] = offs['q_norm_g'].reshape(1, HEAD_DIM)
    grads['k_norm_g'] = offs['k_norm_g'].reshape(1, HEAD_DIM)
    grads['conv_b'] = offs['conv_b'].reshape(1, D)
    grads['final_norm_g'] = offs['final_norm_g']
    grads['norm_g'] = shard('norm_g', 3)[None]
    grads['conv_w'] = shard('conv_w', 4)[None]
    grads['lru_ba'] = shard('lru_ba', 2)[None]
    grads['lru_bx'] = shard('lru_bx', 2)[None]
    grads['lru_lambda'] = shard('lru_lambda', 2)[None]

    LBD = D // LB
    delta, new_m, new_v = {}, {}, {}

    def update(k, comm=None):
        res = adamw("adamw_" + k, given[k], grads[k], given['m_' + k], given['v_' + k], comm=comm)
        if comm is not None:
            res, landed = res
        delta[k], new_m[k], new_v[k] = res
        return landed if comm is not None else None

    grads.update(w_in=reduced['w_in'][None], w_out=reduced['w_out'][None])
    first, second = [late[0], late[2], late[3]], [late[1]]
    la = update('w_mod', rs_p2_comm([p_[0] for p_ in first], [p_[1] for p_ in first]))
    lb = update('w_in', rs_p2_comm([p_[0] for p_ in second], [p_[1] for p_ in second]))
    landeds = [la[0], lb[0], la[1], la[2]]
    g_wg = sum_slots_into("rs_sum_wg0", landeds[0], where, reduced['ffn_wg'], (2, D, F4), 0)
    g_wu = sum_slots_into("rs_sum_wu0", landeds[1], where, reduced['ffn_wu'], (2, D, F4), 0)
    g_wa = sum_slots_into("rs_sum_lru_wa", landeds[2], where, None, (2 * LB * LBD, LBD), None)
    g_wx = sum_slots_into("rs_sum_lru_wx", landeds[3], where, None, (2 * LB * LBD, LBD), None)
    g_wg, g_wu, g_wd, g_wa, g_wx = comm_call("rs_tail_p3", rs_p3_comm(
        [g_wg, g_wu, reduced['ffn_wd'], g_wa, g_wx],
        [(0, 0, D // 2), (1, 0, D // 2), (2, 0, F4 // 2), (3, None, LB * LBD), (4, None, LB * LBD)]))
    grads.update(ffn_wg=g_wg[None], ffn_wu=g_wu[None], ffn_wd=g_wd[None],
                 lru_wa=g_wa.reshape(lru_wa.shape), lru_wx=g_wx.reshape(lru_wx.shape))
    big_names = ['w_mod', 'w_in', 'ffn_wg', 'ffn_wu', 'ffn_wd', 'w_out', 'lru_wa', 'lru_wx']
    for k in big_names[2:]:
        update(k)
    tiny_names = [k for k in names if k not in big_names]
    res = adamw_many("adamw_tiny", [(given[k], grads[k], given['m_' + k], given['v_' + k]) for k in tiny_names])
    for k, (d_, m_, v_) in zip(tiny_names, res):
        delta[k], new_m[k], new_v[k] = d_, m_, v_

    return (loss, grad_x[None], *[grads[k] for k in names], *[delta[k] for k in names],
            *[new_m[k] for k in names], *[new_v[k] for k in names])
```

```python
import functools

import jax
import jax.numpy as jnp
from jax import lax
from jax.experimental import pallas as pl
from jax.experimental.pallas import tpu as pltpu

F32 = jnp.float32
BF = jnp.bfloat16
EPS = 1e-6
HEAD_DIM = 128
GRID_W = 64
ROPE_THETA = 10000.0
LRU_C = 8.0
FFN_RES = 0.5
N_MOD = 9
LOG2_E = 1.4426950408889634
ADAM_LR, ADAM_B1, ADAM_B2, ADAM_EPS, ADAM_WD, ADAM_STEP = 0.001, 0.9, 0.999, 1e-08, 0.01, 10
VMEM_LIMIT = 52 * 1024 * 1024
MESH = pl.DeviceIdType.MESH
ANY = pl.BlockSpec(memory_space=pl.ANY)


def _sds(shape, dt):
    return jax.ShapeDtypeStruct(tuple(shape), dt)


def _pick(n, cands):
    for c in cands:
        if n % c == 0:
            return c
    return n


def _cparams(**kw):
    return pltpu.CompilerParams(vmem_limit_bytes=VMEM_LIMIT, **kw)


def _sig(x):
    return 1.0 / (1.0 + jnp.exp(-x))


def _gelu(x):
    t = jnp.tanh(0.7978845608028654 * (x + 0.044715 * x * x * x))
    return 0.5 * x * (1.0 + t), t


def _gelu_grad(x, t):
    return 0.5 * (1.0 + t) + 0.5 * x * (1.0 - t * t) * 0.7978845608028654 * (1.0 + 3.0 * 0.044715 * x * x)


class Comm:
    def __init__(self, reads, lands, n_sem, start, finish):
        self.reads, self.lands, self.n_sem, self.start, self.finish = list(reads), list(lands), n_sem, start, finish


def merge_comms(comms):
    reads = [r for c in comms for r in c.reads]
    lands = [l for c in comms for l in c.lands]

    def run(which):
        def fn(r, lo, send_sems, recv_sems, off=0):
            ro = lo_ = so = 0
            for c in comms:
                getattr(c, which)(r[ro:ro + len(c.reads)], lo[lo_:lo_ + len(c.lands)], send_sems, recv_sems, off + so)
                ro, lo_, so = ro + len(c.reads), lo_ + len(c.lands), so + c.n_sem
        return fn

    return Comm(reads, lands, sum(c.n_sem for c in comms), run('start'), run('finish'))


def hosted_call(body, *, name, grid, in_specs, out_specs, out_shape, args, scratch_shapes=(), aliases=None, comm=None):
    aliases = dict(aliases or {})
    if comm is None:
        return pl.pallas_call(
            body, name=name, grid=grid, in_specs=list(in_specs), out_specs=list(out_specs), out_shape=list(out_shape),
            scratch_shapes=list(scratch_shapes), input_output_aliases=aliases, compiler_params=_cparams())(*args)
    n_in, n_out, n_sc = len(args), len(out_shape), len(scratch_shapes)
    land_in = [(t, l) for t, l in enumerate(comm.lands) if not isinstance(l, jax.ShapeDtypeStruct)]
    nr, nli, nl = len(comm.reads), len(land_in), len(comm.lands)

    def wrapped(*refs):
        a = refs[:n_in]
        r = refs[n_in:n_in + nr]
        pos = n_in + nr + nli
        o = refs[pos:pos + n_out]
        lo = refs[pos + n_out:pos + n_out + nl]
        sc = refs[pos + n_out + nl:pos + n_out + nl + n_sc]
        send_sems, recv_sems = refs[pos + n_out + nl + n_sc:]
        ids = [pl.program_id(d) for d in range(len(grid))]
        first, last = ids[0] == 0, ids[0] == grid[0] - 1
        for d in range(1, len(grid)):
            first = first & (ids[d] == 0)
            last = last & (ids[d] == grid[d] - 1)

        @pl.when(first)
        def _():
            comm.start(r, lo, send_sems, recv_sems)

        body(*a, *o, *sc)

        @pl.when(last)
        def _():
            comm.finish(r, lo, send_sems, recv_sems)

    for q, (t, _) in enumerate(land_in):
        aliases[n_in + nr + q] = n_out + t
    res = pl.pallas_call(
        wrapped, name=name, grid=grid,
        in_specs=list(in_specs) + [ANY] * (nr + nli), out_specs=list(out_specs) + [ANY] * nl,
        out_shape=list(out_shape) + [l if isinstance(l, jax.ShapeDtypeStruct) else _sds(l.shape, l.dtype) for l in comm.lands],
        scratch_shapes=list(scratch_shapes) + [pltpu.SemaphoreType.DMA((comm.n_sem,)), pltpu.SemaphoreType.DMA((comm.n_sem,))],
        input_output_aliases=aliases, compiler_params=_cparams(),
    )(*args, *comm.reads, *[l for _, l in land_in])
    return list(res[:n_out]), list(res[n_out:])


def comm_call(name, comm):
    def body():
        pass

    return hosted_call(body, name=name, grid=(1,), in_specs=[], out_specs=[], out_shape=[], args=[], comm=comm)[1]


def ew_call(name, grid, fn, ins, outs, first=None, aliases=None, comm=None):
    n_in = len(ins)

    def body(*refs):
        ids = tuple(pl.program_id(a) for a in range(len(grid)))
        vals = fn(ids, *refs[:n_in])
        for (_, _, acc), o_ref, v in zip(outs, refs[n_in:], vals):
            if not acc:
                o_ref[...] = v.astype(o_ref.dtype)
            else:
                is_first = first(ids)

                @pl.when(is_first)
                def _(o_ref=o_ref, v=v):
                    o_ref[...] = v.astype(o_ref.dtype)

                @pl.when(jnp.logical_not(is_first))
                def _(o_ref=o_ref, v=v):
                    o_ref[...] += v.astype(o_ref.dtype)

    return hosted_call(body, name=name, grid=grid, in_specs=[s for _, s in ins], out_specs=[s for _, s, _ in outs],
                       out_shape=[o for o, _, _ in outs], args=[a for a, _ in ins], aliases=aliases, comm=comm)


def fused_mm(name, grid, ins, prods, acc_shapes, epi, outs, extras=(), pre=None, comm=None):
    n_in, n_ex, n_out = len(ins), len(extras), len(outs)
    nk = grid[-1]
    pre = pre or {}
    n_acc = len(acc_shapes)

    def body(*refs):
        in_refs = refs[:n_in]
        ex_refs = refs[n_in:n_in + n_ex]
        out_refs = refs[n_in + n_ex:n_in + n_ex + n_out]
        accs = refs[n_in + n_ex + n_out:]
        ids = tuple(pl.program_id(a) for a in range(len(grid)))
        k = ids[-1]
        loaded = {}

        def operand(i):
            if i not in loaded:
                v = in_refs[i][...]
                if i in pre:
                    v = pre[i](v)
                loaded[i] = v.astype(BF)
            return loaded[i]

        def product(ia, ib, dims):
            return lax.dot_general(operand(ia), operand(ib), (dims, ((), ())), preferred_element_type=F32)

        if nk == 1:
            sums = [None] * n_acc
            for ia, ib, dims, ai in prods:
                d = product(ia, ib, dims)
                sums[ai] = d if sums[ai] is None else sums[ai] + d
            for o_ref, v in zip(out_refs, epi(ids, sums, ex_refs)):
                o_ref[...] = v.astype(o_ref.dtype)
            return

        @pl.when(k == 0)
        def _():
            for a in accs:
                a[...] = jnp.zeros(a.shape, F32)

        for ia, ib, dims, ai in prods:
            accs[ai][...] += product(ia, ib, dims)

        @pl.when(k == nk - 1)
        def _():
            vals = epi(ids, [a[...] for a in accs], ex_refs)
            for o_ref, v in zip(out_refs, vals):
                o_ref[...] = v.astype(o_ref.dtype)

    return hosted_call(
        body, name=name, grid=grid, in_specs=[s for _, s in ins] + [s for _, s in extras],
        out_specs=[s for _, s in outs], out_shape=[o for o, _ in outs],
        scratch_shapes=[pltpu.VMEM(s, F32) for s in acc_shapes] if nk > 1 else [],
        args=[a for a, _ in ins] + [a for a, _ in extras], comm=comm)


NN = ((1,), (0,))
NT = ((1,), (1,))
TN = ((0,), (0,))


class Dims:
    def __init__(self, S, C, D, F4, W4, NS, LB):
        self.S, self.C, self.D, self.F4, self.W4, self.NS, self.LB = S, C, D, F4, W4, NS, LB
        self.T = S + C
        self.DFF = F4 * NS
        self.INW = W4 * NS
        self.NQ = D // HEAD_DIM
        self.KVW = (self.INW - 5 * D) // 2
        self.NKV = self.KVW // HEAD_DIM
        self.G = self.NQ // self.NKV
        self.OFF_K = D
        self.OFF_V = D + self.KVW
        self.OFF_LX = D + 2 * self.KVW
        self.OFF_LG = self.OFF_LX + D
        self.OFF_GA = self.OFF_LG + D
        self.OFF_GL = self.OFF_GA + D
        self.bm = _pick(C, [256, 128, 64, 32, 16, 8])
        self.nCb = C // self.bm
        self.nTb = self.T // self.bm
        self.nSb = S // self.bm
        self.mT = _pick(self.T, [544, 512, 384, 256, 128])
        self.mS = _pick(S, [512, 256, 128])
        self.kT = _pick(self.T, [1088, 1024, 768, 544, 512, 384, 256, 128])
        self.kS = _pick(S, [1024, 512, 256, 128])
        self.cw = _pick(D, [1024, 512, 256, 128]) if (self.OFF_LX % 1024 == 0 and D % 1024 == 0) else _pick(
            self.OFF_LX, [512, 256, 128])
        self.nsub = 2 if (W4 % 256 == 0 and W4 >= 512) else 1
        self.wb = W4 // self.nsub
        self.LBD = D // LB
        self.bq = _pick(C, [256, 128]) if S % _pick(C, [256, 128]) == 0 else 128


def rope_tables(dm):
    rows = dm.S // GRID_W
    row = jnp.repeat(jnp.arange(rows, dtype=F32), GRID_W)
    col = jnp.tile(jnp.arange(GRID_W, dtype=F32), rows)
    axis_dims = HEAD_DIM // 2
    freqs = ROPE_THETA ** (-jnp.arange(0, axis_dims, 2, dtype=F32) / axis_dims)
    ang = jnp.concatenate([row[:, None] * freqs, col[:, None] * freqs], axis=-1)
    cos = jnp.repeat(jnp.cos(ang), 2, axis=-1)
    sin = jnp.repeat(jnp.sin(ang), 2, axis=-1)
    sign = jnp.tile(jnp.array([-1.0, 1.0], F32), HEAD_DIM // 2)
    sin = sin * sign
    cos = jnp.concatenate([jnp.ones((dm.C, HEAD_DIM), F32), cos], axis=0)
    sin = jnp.concatenate([jnp.zeros((dm.C, HEAD_DIM), F32), sin], axis=0)
    return cos, sin


def _pair_swap(y):
    lane = lax.broadcasted_iota(jnp.int32, y.shape, 1)
    nxt = pltpu.roll(y, y.shape[1] - 1, 1)
    prv = pltpu.roll(y, 1, 1)
    return jnp.where((lane & 1) == 0, nxt, prv)


def normmod_fwd(name, dm, x, norm_g3, stage, modv, rows_T):
    D, bm = dm.D, dm.bm
    nb = dm.nTb if rows_T else dm.nSb
    typ = (lambda i: jnp.where(i < dm.nCb, 0, 1)) if rows_T else (lambda i: 1)

    def fn(ids, x_ref, g_ref, sh_ref, sc_ref):
        xv = x_ref[...]
        r = lax.rsqrt(jnp.mean(xv * xv, axis=-1, keepdims=True) + EPS)
        n = xv * r * g_ref[...]
        return [n * (1.0 + sc_ref[...]) + sh_ref[...]]

    return ew_call(
        name, (nb,), fn,
        [(x, pl.BlockSpec((bm, D), lambda i: (i, 0))),
         (norm_g3, pl.BlockSpec((None, 1, D), lambda i: (stage, 0, 0))),
         (modv, pl.BlockSpec((None, None, 1, D), lambda i: (typ(i), 3 * stage, 0, 0))),
         (modv, pl.BlockSpec((None, None, 1, D), lambda i: (typ(i), 3 * stage + 1, 0, 0)))],
        [(_sds(x.shape, BF), pl.BlockSpec((bm, D), lambda i: (i, 0)), False)])[0]


def normmod_concat_fwd(name, dm, ctx, x, norm_g3, modv):
    D, bm, nCb = dm.D, dm.bm, dm.nCb
    typ = lambda i: jnp.where(i < nCb, 0, 1)

    def fn(ids, c_ref, x_ref, g_ref, sh_ref, sc_ref):
        xv = jnp.where(ids[0] < nCb, c_ref[...], x_ref[...])
        r = lax.rsqrt(jnp.mean(xv * xv, axis=-1, keepdims=True) + EPS)
        n = xv * r * g_ref[...]
        return [n * (1.0 + sc_ref[...]) + sh_ref[...], xv]

    row = pl.BlockSpec((bm, D), lambda i: (i, 0))
    return ew_call(
        name, (dm.nTb,), fn,
        [(ctx, pl.BlockSpec((bm, D), lambda i: (jnp.minimum(i, nCb - 1), 0))),
         (x, pl.BlockSpec((bm, D), lambda i: (jnp.maximum(i - nCb, 0), 0))),
         (norm_g3, pl.BlockSpec((None, 1, D), lambda i: (0, 0, 0))),
         (modv, pl.BlockSpec((None, None, 1, D), lambda i: (typ(i), 0, 0, 0))),
         (modv, pl.BlockSpec((None, None, 1, D), lambda i: (typ(i), 1, 0, 0)))],
        [(_sds((dm.T, D), BF), row, False), (_sds((dm.T, D), F32), row, False)])


def normmod_bwd(name, dm, dh, x, dres, norm_g3, stage, modv, rows_T, dres_lat_only, out_lat_only=False):
    D, bm = dm.D, dm.bm
    nb = dm.nTb if rows_T else dm.nSb
    nCb = dm.nCb
    typ = (lambda i: jnp.where(i < nCb, 0, 1)) if rows_T else (lambda i: 1)
    if dres_lat_only:
        dres_map = lambda i: (jnp.maximum(i - nCb, 0), 0)
    else:
        dres_map = lambda i: (i, 0)

    def fn(ids, dh_ref, x_ref, dres_ref, g_ref, sc_ref):
        i = ids[0]
        xv = x_ref[...]
        dhv = dh_ref[...].astype(F32)
        r = lax.rsqrt(jnp.mean(xv * xv, axis=-1, keepdims=True) + EPS)
        xn = xv * r
        g = g_ref[...]
        n = xn * g
        dn = dhv * (1.0 + sc_ref[...])
        dxn = dn * g
        dx = r * (dxn - xn * jnp.mean(dxn * xn, axis=-1, keepdims=True))
        dresv = dres_ref[...]
        if dres_lat_only:
            dresv = jnp.where(i >= nCb, dresv, 0.0)
        dsh = jnp.sum(dhv, axis=0, keepdims=True)
        dsc = jnp.sum(dhv * n, axis=0, keepdims=True)
        dg = jnp.sum(dn * xn, axis=0, keepdims=True)
        return [dx + dresv, dsh, dsc, dg]

    if rows_T:
        first = lambda ids: (ids[0] == 0) | (ids[0] == nCb)
    else:
        first = lambda ids: ids[0] == 0
    acc = (_sds((2, 1, D), F32), pl.BlockSpec((None, 1, D), lambda i: (typ(i), 0, 0)), True)
    return ew_call(
        name, (nb,), fn,
        [(dh, pl.BlockSpec((bm, D), lambda i: (i, 0))),
         (x, pl.BlockSpec((bm, D), lambda i: (i, 0))),
         (dres, pl.BlockSpec((bm, D), dres_map)),
         (norm_g3, pl.BlockSpec((None, 1, D), lambda i: (stage, 0, 0))),
         (modv, pl.BlockSpec((None, None, 1, D), lambda i: (typ(i), 3 * stage + 1, 0, 0)))],
        [(_sds((dm.S, D) if out_lat_only else x.shape, F32),
          pl.BlockSpec((bm, D), (lambda i: (jnp.maximum(i - nCb, 0), 0)) if out_lat_only else (lambda i: (i, 0))), False),
         acc, acc, acc], first=first)


def gate_bwd(name, dm, dx, f, modv, gidx, scale, rows_T):
    D, bm = dm.D, dm.bm
    nb = dm.nTb if rows_T else dm.nSb
    nCb = dm.nCb
    typ = (lambda i: jnp.where(i < nCb, 0, 1)) if rows_T else (lambda i: 1)

    def fn(ids, dx_ref, f_ref, g_ref):
        dxv = dx_ref[...]
        return [scale * g_ref[...] * dxv, jnp.sum(scale * f_ref[...].astype(F32) * dxv, axis=0, keepdims=True)]

    if rows_T:
        first = lambda ids: (ids[0] == 0) | (ids[0] == nCb)
    else:
        first = lambda ids: ids[0] == 0
    return ew_call(
        name, (nb,), fn,
        [(dx, pl.BlockSpec((bm, D), lambda i: (i, 0))),
         (f, pl.BlockSpec((bm, D), lambda i: (i, 0))),
         (modv, pl.BlockSpec((None, None, 1, D), lambda i: (typ(i), gidx, 0, 0)))],
        [(_sds(dx.shape, BF), pl.BlockSpec((bm, D), lambda i: (i, 0)), False),
         (_sds((2, 1, D), F32), pl.BlockSpec((None, 1, D), lambda i: (typ(i), 0, 0)), True)], first=first)


def ffn_fwd(name, dm, h, xres, wg, wu, wd, modv, gidx, rows_T, comm_up=None, comm_down=None):
    D, F4, NS = dm.D, dm.F4, dm.NS
    M = h.shape[0]
    bm = dm.mT if rows_T else dm.mS
    C = dm.C

    def epi_up(ids, accs, ex):
        a, u = accs
        return [a, u, a * _sig(a) * u]

    hspec = pl.BlockSpec((bm, D), lambda j, i, k: (i, 0))
    wspec = pl.BlockSpec((None, D, F4), lambda j, i, k: (j, 0, 0))
    ospec = pl.BlockSpec((bm, F4), lambda j, i, k: (i, j))
    res = fused_mm(
        name + "_up", (NS, M // bm, 1), [(h, hspec), (wg, wspec), (wu, wspec)],
        [(0, 1, NN, 0), (0, 2, NN, 1)], [(bm, F4), (bm, F4)], epi_up,
        [(_sds((M, dm.DFF), BF), ospec)] * 3, comm=comm_up)
    (a, u, s), land_up = res if comm_up is not None else (res, None)
    if wd is None:
        wd = land_up[0]

    bn = _pick(D, [1024, 512, 256, 128])

    def epi_dn(ids, accs, ex):
        f = accs[0]
        if rows_T:
            row = ids[0] * bm + lax.broadcasted_iota(jnp.int32, (bm, 1), 0)
            gate = jnp.where(row < C, ex[1][...], ex[2][...])
        else:
            gate = ex[2][...]
        return [ex[0][...] + FFN_RES * gate * f, f]

    gspec = lambda t: pl.BlockSpec((None, None, 1, bn), lambda i, j, k: (t, gidx, 0, j))
    res = fused_mm(
        name + "_down", (M // bm, D // bn, NS),
        [(s, pl.BlockSpec((bm, F4), lambda i, j, k: (i, k))),
         (wd, pl.BlockSpec((None, F4, bn), lambda i, j, k: (k, 0, j)))],
        [(0, 1, NN, 0)], [(bm, bn)], epi_dn,
        [(_sds((M, D), F32), pl.BlockSpec((bm, bn), lambda i, j, k: (i, j))),
         (_sds((M, D), BF), pl.BlockSpec((bm, bn), lambda i, j, k: (i, j)))],
        extras=[(xres, pl.BlockSpec((bm, bn), lambda i, j, k: (i, j))), (modv, gspec(0)), (modv, gspec(1))],
        comm=comm_down)
    (xo, f), land_down = res if comm_down is not None else (res, None)
    return xo, a, u, s, f, land_up, land_down


def ffn_bwd(name, dm, df, h, a, u, s, wg, wu, wd, rows_T, comms=None):
    comms = comms or {}
    landed, made = {}, {}

    def run(key, *args, **kw):
        comm = comms[key](landed, made) if key in comms else None
        res = fused_mm(*args, comm=comm, **kw)
        if comm is not None:
            res, landed[key] = res
        return res

    D, F4, NS = dm.D, dm.F4, dm.NS
    M = h.shape[0]
    bm = dm.mT if rows_T else dm.mS
    bkr = dm.kT if rows_T else dm.kS

    def epi_ds(ids, accs, ex):
        ds = accs[0]
        av = ex[0][...].astype(F32)
        uv = ex[1][...].astype(F32)
        sg = _sig(av)
        return [ds * uv * (sg * (1.0 + av * (1.0 - sg))), ds * av * sg]

    ospec = pl.BlockSpec((bm, F4), lambda j, i, k: (i, j))
    da, du = run(
        'ds', name + "_ds", (NS, M // bm, 1),
        [(df, pl.BlockSpec((bm, D), lambda j, i, k: (i, 0))),
         (wd, pl.BlockSpec((None, F4, D), lambda j, i, k: (j, 0, 0)))],
        [(0, 1, NT, 0)], [(bm, F4)], epi_ds, [(_sds((M, dm.DFF), BF), ospec)] * 2,
        extras=[(a, ospec), (u, ospec)])

    ident = lambda ids, accs, ex: list(accs)
    bn = _pick(D, [1024, 512, 256, 128])
    dwg, dwu = run(
        'dwgu', name + "_dwgu", (D // bn, NS, M // bkr),
        [(h, pl.BlockSpec((bkr, bn), lambda i, j, k: (k, i))),
         (da, pl.BlockSpec((bkr, F4), lambda i, j, k: (k, j))),
         (du, pl.BlockSpec((bkr, F4), lambda i, j, k: (k, j)))],
        [(0, 1, TN, 0), (0, 2, TN, 1)], [(bn, F4), (bn, F4)], ident,
        [(_sds((NS, D, F4), BF), pl.BlockSpec((None, bn, F4), lambda i, j, k: (j, i, 0)))] * 2)
    made['dwg'], made['dwu'] = dwg, dwu

    dwd = run(
        'dwd', name + "_dwd", (NS, D // bn, M // bkr),
        [(s, pl.BlockSpec((bkr, F4), lambda i, j, k: (k, i))),
         (df, pl.BlockSpec((bkr, bn), lambda i, j, k: (k, j)))],
        [(0, 1, TN, 0)], [(F4, bn)], ident,
        [(_sds((NS, F4, D), BF), pl.BlockSpec((None, F4, bn), lambda i, j, k: (i, 0, j)))])[0]
    made['dwd'] = dwd

    dh = run(
        'dh', name + "_dh", (M // bm, D // bn, NS),
        [(da, pl.BlockSpec((bm, F4), lambda i, j, k: (i, k))),
         (wg, pl.BlockSpec((None, bn, F4), lambda i, j, k: (k, j, 0))),
         (du, pl.BlockSpec((bm, F4), lambda i, j, k: (i, k))),
         (wu, pl.BlockSpec((None, bn, F4), lambda i, j, k: (k, j, 0)))],
        [(0, 1, NT, 0), (2, 3, NT, 0)], [(bm, bn)], ident,
        [(_sds((M, D), F32), pl.BlockSpec((bm, bn), lambda i, j, k: (i, j)))])[0]
    return dh, dwg, dwu, dwd, landed


def qk_prep(dm, P, gq, gk, cosf, sinf):
    D, KVW, bm = dm.D, dm.KVW, dm.bm

    def head_norm_rope(xh, g, c, s):
        r = lax.rsqrt(jnp.mean(xh * xh, axis=-1, keepdims=True) + EPS)
        y = xh * r * g
        return y * c + _pair_swap(y) * s

    def fn(ids, q_ref, k_ref, v_ref, gq_ref, gk_ref, c_ref, s_ref):
        c, s = c_ref[...], s_ref[...]
        qs = [head_norm_rope(q_ref[:, h * HEAD_DIM:(h + 1) * HEAD_DIM], gq_ref[...], c, s) for h in range(dm.NQ)]
        ks = [head_norm_rope(k_ref[:, h * HEAD_DIM:(h + 1) * HEAD_DIM], gk_ref[...], c, s) for h in range(dm.NKV)]
        return [jnp.concatenate(qs, axis=1), jnp.concatenate(ks, axis=1), v_ref[...]]

    hspec = pl.BlockSpec((bm, HEAD_DIM), lambda i: (i, 0))
    vec = pl.BlockSpec((1, HEAD_DIM), lambda i: (0, 0))
    return ew_call(
        "qk_prep", (dm.nTb,), fn,
        [(P, pl.BlockSpec((bm, D), lambda i: (i, 0))),
         (P, pl.BlockSpec((bm, KVW), lambda i: (i, dm.OFF_K // KVW))),
         (P, pl.BlockSpec((bm, KVW), lambda i: (i, dm.OFF_V // KVW))),
         (gq, vec), (gk, vec), (cosf, hspec), (sinf, hspec)],
        [(_sds((dm.T, D), BF), pl.BlockSpec((bm, D), lambda i: (i, 0)), False),
         (_sds((dm.T, KVW), BF), pl.BlockSpec((bm, KVW), lambda i: (i, 0)), False),
         (_sds((dm.T, KVW), BF), pl.BlockSpec((bm, KVW), lambda i: (i, 0)), False)])


def qk_prep_bwd(dm, dq, dk, dv, P, gq, gk, cosf, sinf, dP):
    D, KVW, bm, nCb = dm.D, dm.KVW, dm.bm, dm.nCb
    W = D + 2 * KVW

    def head_bwd(d, xh, g, c, s):
        dy = d * c - _pair_swap(d) * s
        r = lax.rsqrt(jnp.mean(xh * xh, axis=-1, keepdims=True) + EPS)
        xn = xh * r
        dg = jnp.sum(dy * xn, axis=0, keepdims=True)
        dxn = dy * g
        return r * (dxn - xn * jnp.mean(dxn * xn, axis=-1, keepdims=True)), dg

    def fn(ids, dq_ref, dk_ref, dv_ref, q_ref, k_ref, gq_ref, gk_ref, c_ref, s_ref, dp_any):
        i = ids[0]
        c, s = c_ref[...], s_ref[...]
        lat = i >= nCb
        outs, dgq = [], jnp.zeros((1, HEAD_DIM), F32)
        for h in range(dm.NQ):
            sl = slice(h * HEAD_DIM, (h + 1) * HEAD_DIM)
            d = jnp.where(lat, dq_ref[:, sl], 0.0)
            dx, dg = head_bwd(d, q_ref[:, sl], gq_ref[...], c, s)
            outs.append(dx)
            dgq = dgq + dg
        dgk = jnp.zeros((1, HEAD_DIM), F32)
        for h in range(dm.NKV):
            sl = slice(h * HEAD_DIM, (h + 1) * HEAD_DIM)
            dx, dg = head_bwd(dk_ref[:, sl], k_ref[:, sl], gk_ref[...], c, s)
            outs.append(dx)
            dgk = dgk + dg
        outs.append(dv_ref[...])
        return [jnp.concatenate(outs, axis=1), dgq, dgk]

    hspec = pl.BlockSpec((bm, HEAD_DIM), lambda i: (i, 0))
    vec = pl.BlockSpec((1, HEAD_DIM), lambda i: (0, 0))
    return ew_call(
        "qk_prep_bwd", (dm.nTb,), fn,
        [(dq, pl.BlockSpec((bm, D), lambda i: (jnp.maximum(i - nCb, 0), 0))),
         (dk, pl.BlockSpec((bm, KVW), lambda i: (i, 0))),
         (dv, pl.BlockSpec((bm, KVW), lambda i: (i, 0))),
         (P, pl.BlockSpec((bm, D), lambda i: (i, 0))),
         (P, pl.BlockSpec((bm, KVW), lambda i: (i, dm.OFF_K // KVW))),
         (gq, vec), (gk, vec), (cosf, hspec), (sinf, hspec), (dP, ANY)],
        [(_sds(dP.shape, BF), pl.BlockSpec((bm, W), lambda i: (i, 0)), False),
         (_sds((1, HEAD_DIM), F32), vec, True), (_sds((1, HEAD_DIM), F32), vec, True)],
        first=lambda ids: ids[0] == 0, aliases={9: 0})


def _softmax_numerators(s_ref, eb_ref, mb_ref, scale):
    rows, T = s_ref.shape
    m = jnp.max(s_ref[...], axis=-1, keepdims=True)
    mb_ref[...] = jnp.broadcast_to(m, (rows, HEAD_DIM))
    lacc = jnp.zeros((rows, HEAD_DIM), F32)
    for c in range(T // HEAD_DIM):
        cs = slice(c * HEAD_DIM, (c + 1) * HEAD_DIM)
        e = jnp.exp2((s_ref[:, cs] - mb_ref[...]) * (scale * LOG2_E))
        lacc = lacc + e
        eb_ref[:, cs] = e.astype(BF)
    return jnp.sum(lacc, axis=-1, keepdims=True)


def attention_fwd(dm, qr, kr, vb, comm=None):
    S, T, D, G, nCb = dm.S, dm.T, dm.D, dm.G, dm.nCb
    bq = dm.bq
    off = dm.C // bq
    scale = HEAD_DIM ** -0.5
    GW = G * HEAD_DIM

    def body(q_ref, k_ref, v_ref, o_ref):
        k = k_ref[...]
        v = v_ref[...]
        head = lambda h: slice(h * HEAD_DIM, (h + 1) * HEAD_DIM)
        scores = lambda h: lax.dot_general(q_ref[:, head(h)], k, (NT, ((), ())), preferred_element_type=F32)
        s_next = scores(0)
        for h in range(G):
            s = s_next
            if h + 1 < G:
                s_next = scores(h + 1)
            m = jnp.max(s, axis=-1, keepdims=True)
            p = jnp.exp2((s - m) * (scale * LOG2_E))
            l = jnp.sum(p, axis=-1, keepdims=True)
            o = lax.dot_general(p.astype(BF), v, (NN, ((), ())), preferred_element_type=F32)
            o_ref[:, head(h)] = o / l

    return hosted_call(
        body, grid=(dm.NKV, S // bq), name="attn_fwd",
        in_specs=[pl.BlockSpec((bq, GW), lambda g, i: (i + off, g)),
                  pl.BlockSpec((T, HEAD_DIM), lambda g, i: (0, g)),
                  pl.BlockSpec((T, HEAD_DIM), lambda g, i: (0, g))],
        out_specs=[pl.BlockSpec((bq, GW), lambda g, i: (i, g))],
        out_shape=[_sds((S, D), F32)], args=[qr, kr, vb], comm=comm)


def attention_bwd(dm, qr, kr, vb, attn, dattn, comm=None):
    S, T, D, G = dm.S, dm.T, dm.D, dm.G
    bq = dm.bq
    off = dm.C // bq
    scale = HEAD_DIM ** -0.5
    GW = G * HEAD_DIM

    def body(q_ref, k_ref, v_ref, o_ref, do_ref, dq_ref, dk_ref, dv_ref, s2_ref, dp_ref, eb_ref, tb_ref, mb_ref):
        i = pl.program_id(1)

        @pl.when(i == 0)
        def _():
            dk_ref[...] = jnp.zeros(dk_ref.shape, F32)
            dv_ref[...] = jnp.zeros(dv_ref.shape, F32)

        k = k_ref[...]
        v = v_ref[...]
        head = lambda h: slice(h * HEAD_DIM, (h + 1) * HEAD_DIM)

        def finish(h, w):
            dq_ref[:, head(h)] = lax.dot_general(tb_ref[...], k, (NN, ((), ())), preferred_element_type=F32) * w
            dk_ref[...] += lax.dot_general(tb_ref[...], (q_ref[:, head(h)].astype(F32) * w).astype(BF), (TN, ((), ())),
                                           preferred_element_type=F32)

        s2_ref[0] = lax.dot_general(q_ref[:, head(0)], k, (NT, ((), ())), preferred_element_type=F32)
        w_prev = None
        for h in range(G):
            s_ref = s2_ref.at[h % 2]
            do = do_ref[:, head(h)]
            dof = do.astype(F32)
            if h + 1 < G:
                s2_ref[(h + 1) % 2] = lax.dot_general(q_ref[:, head(h + 1)], k, (NT, ((), ())),
                                                      preferred_element_type=F32)
            if h > 0:
                finish(h - 1, w_prev)
            l = _softmax_numerators(s_ref, eb_ref, mb_ref, scale)
            rl = 1.0 / l
            dp_ref[...] = lax.dot_general(do, v, (NT, ((), ())), preferred_element_type=F32)
            dv_ref[...] += lax.dot_general(eb_ref[...], (dof * rl).astype(BF), (TN, ((), ())), preferred_element_type=F32)
            delta = jnp.sum(dof * o_ref[:, head(h)], axis=-1, keepdims=True)
            mb_ref[...] = jnp.broadcast_to(delta, (bq, HEAD_DIM))
            for c in range(T // HEAD_DIM):
                cs = slice(c * HEAD_DIM, (c + 1) * HEAD_DIM)
                tb_ref[:, cs] = (eb_ref[:, cs].astype(F32) * (dp_ref[:, cs] - mb_ref[...])).astype(BF)
            w_prev = scale * rl
        finish(G - 1, w_prev)

    return hosted_call(
        body, grid=(dm.NKV, S // bq), name="attn_bwd",
        in_specs=[pl.BlockSpec((bq, GW), lambda g, i: (i + off, g)),
                  pl.BlockSpec((T, HEAD_DIM), lambda g, i: (0, g)),
                  pl.BlockSpec((T, HEAD_DIM), lambda g, i: (0, g)),
                  pl.BlockSpec((bq, GW), lambda g, i: (i, g)),
                  pl.BlockSpec((bq, GW), lambda g, i: (i + off, g))],
        out_specs=[pl.BlockSpec((bq, GW), lambda g, i: (i, g)),
                   pl.BlockSpec((T, HEAD_DIM), lambda g, i: (0, g)),
                   pl.BlockSpec((T, HEAD_DIM), lambda g, i: (0, g))],
        out_shape=[_sds((S, D), F32), _sds((T, dm.KVW), F32), _sds((T, dm.KVW), F32)],
        args=[qr, kr, vb, attn, dattn],
        scratch_shapes=[pltpu.VMEM((2, bq, T), F32), pltpu.VMEM((bq, T), F32), pltpu.VMEM((bq, T), BF),
                        pltpu.VMEM((bq, T), BF), pltpu.VMEM((bq, HEAD_DIM), F32)], comm=comm)


def _conv_taps(dm, lx, masks_only=False):
    T, C = dm.T, dm.C
    t = lax.broadcasted_iota(jnp.int32, (T, 1), 0)
    valid = [(t >= 2) & ((t < C) | (t >= C + 2)), (t >= 1) & ((t < C) | (t >= C + 1)), None,
             (t != C - 1) & (t != T - 1)]
    shifts = [2, 1, 0, T - 1]
    taps = []
    for k in range(4):
        if k == 2:
            taps.append(lx)
        else:
            taps.append(jnp.where(valid[k], pltpu.roll(lx, shifts[k], 0), 0.0))
    return taps


def _scan_tiles(dm, chains):
    T, C = dm.T, dm.C
    nT, nC = T // 8, C // 8
    row = lax.broadcasted_iota(jnp.int32, (8, HEAD_DIM), 0)

    def tile_of(i, asc, split):
        if not split:
            return i if asc else nT - 1 - i
        if asc:
            return jnp.where(i < nT - nC, nC + i, i - (nT - nC))
        return jnp.where(i < nC, nC - 1 - i, nT - 1 - (i - nC))

    def step(i, carry, asc, split, a_ref, u_ref, out_ref, mode):
        off = pl.multiple_of(tile_of(i, asc, split) * 8, 8)
        a = a_ref[pl.ds(off, 8), :]
        b = u_ref[pl.ds(off, 8), :]
        if mode == 'lam':
            if asc:
                coef = jnp.where(row == 0, 1.0, pltpu.roll(a, 1, 0))
            else:
                coef = jnp.where(row == 7, 1.0, pltpu.roll(a, 7, 0))
        else:
            coef = a
        A, B = coef, b
        for d in (1, 2, 4):
            if asc:
                ok = row >= d
                A_sh = jnp.where(ok, pltpu.roll(A, d, 0), 1.0)
                B_sh = jnp.where(ok, pltpu.roll(B, d, 0), 0.0)
            else:
                ok = row < 8 - d
                A_sh = jnp.where(ok, pltpu.roll(A, 8 - d, 0), 1.0)
                B_sh = jnp.where(ok, pltpu.roll(B, 8 - d, 0), 0.0)
            B = B + A * B_sh
            A = A * A_sh
        h = A * carry + B
        out_ref[pl.ds(off, 8), :] = h
        last = h[7:8, :] if asc else h[0:1, :]
        if mode == 'lam':
            last = last * (a[7:8, :] if asc else a[0:1, :])
        return jnp.broadcast_to(last, (8, HEAD_DIM))

    def body(i, carries):
        return tuple(step(i, c_, *ch) for c_, ch in zip(carries, chains))

    lax.fori_loop(0, nT, body, tuple(jnp.zeros((8, HEAD_DIM), F32) for _ in chains))


def _lru_gates(xc, wa, ba, wx, bx, sp):
    xb = xc.astype(BF)
    r = _sig(jnp.dot(xb, wa, preferred_element_type=F32) + ba)
    i = _sig(jnp.dot(xb, wx, preferred_element_type=F32) + bx)
    a = jnp.exp(-LRU_C * r * sp)
    m = jnp.sqrt(1.0 - a * a)
    return r, i, a, m


def lru_fwd(dm, P, conv_w, conv_b, wa, ba, wx, bx, sp):
    T, D, LB = dm.T, dm.D, dm.LB
    W = dm.LBD
    R = _pick(T, [272, 256, 128, 64, 8])
    lxb = dm.OFF_LX // W

    def body(lx_ref, cw_ref, cb_ref, wa_ref, ba_ref, wx_ref, bx_ref, sp_ref, hf_ref, hb_ref, xc_ref, a_ref):
        taps = _conv_taps(dm, lx_ref[...])
        xc = cb_ref[...]
        for k in range(4):
            xc = xc + taps[k] * cw_ref[k:k + 1, :]
        xc_ref[...] = xc
        h_refs = (hf_ref, hb_ref)

        def chunk(ci, _):
            off = pl.multiple_of(ci * R, 8)
            x = xc_ref[pl.ds(off, R), :]
            for d in range(2):
                r, i, a, m = _lru_gates(x, wa_ref[d].astype(BF), ba_ref[d], wx_ref[d].astype(BF), bx_ref[d], sp_ref[d])
                a_ref[d, pl.ds(off, R), :] = a
                h_refs[d][pl.ds(off, R), :] = m * i * x
            return 0

        lax.fori_loop(0, T // R, chunk, 0)
        _scan_tiles(dm, [(True, False, a_ref.at[0], hf_ref, hf_ref, 'h'), (False, True, a_ref.at[1], hb_ref, hb_ref, 'h')])

    strip = lambda j: (0, j)
    vec = pl.BlockSpec((2, 1, W), lambda j: (0, 0, j))
    mat = pl.BlockSpec((2, None, W, W), lambda j: (0, j, 0, 0))
    return pl.pallas_call(
        body, grid=(LB,), name="lru_fwd",
        in_specs=[pl.BlockSpec((T, W), lambda j: (0, lxb + j)),
                  pl.BlockSpec((4, W), strip), pl.BlockSpec((1, W), strip), mat, vec, mat, vec, vec],
        out_specs=[pl.BlockSpec((T, W), strip)] * 2, out_shape=[_sds((T, D), F32)] * 2,
        scratch_shapes=[pltpu.VMEM((T, W), F32), pltpu.VMEM((2, T, W), F32)], compiler_params=_cparams(),
    )(P, conv_w, conv_b, wa, ba, wx, bx, sp)


def lru_bwd(dm, P, dh, hf, hb, conv_w, conv_b, wa, ba, wx, bx, sp, sg, dP):
    T, C, D, LB = dm.T, dm.C, dm.D, dm.LB
    W = dm.LBD
    R = _pick(T, [272, 256, 128, 64, 8])
    lxb = dm.OFF_LX // W

    def body(lx_ref, dh_ref, hf_ref, hb_ref, cw_ref, cb_ref, wa_ref, ba_ref, wx_ref, bx_ref, sp_ref, sg_ref, _dp_any,
             dlx_ref, dwa_ref, dba_ref, dwx_ref, dbx_ref, dlam_ref, dcw_ref, dcb_ref,
             xc_ref, a_ref, lam_ref, hp_ref, dxc_ref):
        lx = lx_ref[...]
        taps = _conv_taps(dm, lx)
        xc = cb_ref[...]
        for k in range(4):
            xc = xc + taps[k] * cw_ref[k:k + 1, :]
        xc_ref[...] = xc

        def gates(d, x):
            return _lru_gates(x, wa_ref[d].astype(BF), ba_ref[d], wx_ref[d].astype(BF), bx_ref[d], sp_ref[d])

        def chunk_a(ci, _):
            off = pl.multiple_of(ci * R, 8)
            x = xc_ref[pl.ds(off, R), :]
            for d in range(2):
                a_ref[d, pl.ds(off, R), :] = gates(d, x)[2]
            return 0

        lax.fori_loop(0, T // R, chunk_a, 0)
        _scan_tiles(dm, [(False, False, a_ref.at[0], dh_ref, lam_ref.at[0], 'lam'),
                         (True, True, a_ref.at[1], dh_ref, lam_ref.at[1], 'lam')])
        t = lax.broadcasted_iota(jnp.int32, (T, 1), 0)
        hp_ref[0] = jnp.where(t == 0, 0.0, pltpu.roll(hf_ref[...], 1, 0))
        hv = hb_ref[...]
        hp_ref[1] = jnp.where(t == C - 1, 0.0, jnp.where(t == T - 1, jnp.broadcast_to(hv[0:1, :], hv.shape),
                                                         pltpu.roll(hv, T - 1, 0)))

        def chunk_b(d):
            wa_, wx_ = wa_ref[d].astype(BF), wx_ref[d].astype(BF)

            def run(ci, carry):
                dwa, dwx, dba, dbx, dlam = carry
                off = pl.multiple_of(ci * R, 8)
                x = xc_ref[pl.ds(off, R), :]
                r, i, a, m = gates(d, x)
                lam = lam_ref[d, pl.ds(off, R), :]
                da = lam * hp_ref[d, pl.ds(off, R), :] - lam * (i * x) * a / m
                dloga = da * a
                dza = dloga * (-LRU_C) * sp_ref[d] * r * (1.0 - r)
                dzx = lam * m * x * i * (1.0 - i)
                dzab, dzxb = dza.astype(BF), dzx.astype(BF)
                xb = x.astype(BF)
                dxc = lam * m * i
                dxc = dxc + lax.dot_general(dzab, wa_, (NT, ((), ())), preferred_element_type=F32)
                dxc = dxc + lax.dot_general(dzxb, wx_, (NT, ((), ())), preferred_element_type=F32)
                if d == 0:
                    dxc_ref[pl.ds(off, R), :] = dxc
                else:
                    dxc_ref[pl.ds(off, R), :] += dxc
                dwa = dwa + lax.dot_general(xb, dzab, (TN, ((), ())), preferred_element_type=F32)
                dwx = dwx + lax.dot_general(xb, dzxb, (TN, ((), ())), preferred_element_type=F32)
                dba = dba + jnp.sum(dza, axis=0, keepdims=True)
                dbx = dbx + jnp.sum(dzx, axis=0, keepdims=True)
                dlam = dlam + jnp.sum(dloga * LRU_C * r, axis=0, keepdims=True)
                return dwa, dwx, dba, dbx, dlam

            z = jnp.zeros((W, W), F32)
            zv = jnp.zeros((1, W), F32)
            dwa, dwx, dba, dbx, dlam = lax.fori_loop(0, T // R, run, (z, z, zv, zv, zv))
            dwa_ref[d] = dwa
            dwx_ref[d] = dwx
            dba_ref[d] = dba
            dbx_ref[d] = dbx
            dlam_ref[d] = dlam * sg_ref[d]

        chunk_b(0)
        chunk_b(1)
        dxc = dxc_ref[...]
        dcb_ref[...] = jnp.sum(dxc, axis=0, keepdims=True)
        dcw_ref[...] = jnp.concatenate([jnp.sum(dxc * taps[k], axis=0, keepdims=True) for k in range(4)], axis=0)
        valid = [(t < T - 2) & ((t >= C) | (t < C - 2)), (t < T - 1) & ((t >= C) | (t < C - 1)), None,
                 (t != 0) & (t != C)]
        shifts = [T - 2, T - 1, 0, 1]
        dlx = dxc * cw_ref[2:3, :]
        for k in (0, 1, 3):
            dlx = dlx + jnp.where(valid[k], pltpu.roll(dxc, shifts[k], 0), 0.0) * cw_ref[k:k + 1, :]
        dlx_ref[...] = dlx.astype(dlx_ref.dtype)

    strip = lambda j: (0, j)
    sspec = pl.BlockSpec((T, W), strip)
    vec = pl.BlockSpec((2, 1, W), lambda j: (0, 0, j))
    mat = pl.BlockSpec((2, None, W, W), lambda j: (0, j, 0, 0))
    ovec = pl.BlockSpec((2, 1, W), lambda j: (0, 0, j))
    return pl.pallas_call(
        body, grid=(LB,), name="lru_bwd",
        in_specs=[pl.BlockSpec((T, W), lambda j: (0, lxb + j)), sspec, sspec, sspec,
                  pl.BlockSpec((4, W), strip), pl.BlockSpec((1, W), strip), mat, vec, mat, vec, vec, vec, ANY],
        out_specs=[pl.BlockSpec((T, W), lambda j: (0, lxb + j)), mat, ovec, mat, ovec, ovec,
                   pl.BlockSpec((4, W), strip), pl.BlockSpec((1, W), strip)],
        out_shape=[_sds(dP.shape, BF), _sds((2, LB, W, W), F32), _sds((2, 1, D), F32), _sds((2, LB, W, W), F32),
                   _sds((2, 1, D), F32), _sds((2, 1, D), F32), _sds((4, D), F32), _sds((1, D), F32)],
        scratch_shapes=[pltpu.VMEM((T, W), F32), pltpu.VMEM((2, T, W), F32), pltpu.VMEM((2, T, W), F32),
                        pltpu.VMEM((2, T, W), F32), pltpu.VMEM((T, W), F32)],
        input_output_aliases={12: 0}, compiler_params=_cparams(),
    )(P, dh, hf, hb, conv_w, conv_b, wa, ba, wx, bx, sp, sg, dP)


def merge_fwd(dm, P, attn, hf, hb):
    S, D, bm, cw, nCb = dm.S, dm.D, dm.bm, dm.cw, dm.nCb

    def fn(ids, lg_ref, ga_ref, gl_ref, at_ref, hf_ref, hb_ref):
        ge, _ = _gelu(lg_ref[...])
        lru = (hf_ref[...] + hb_ref[...]) * ge
        return [_sig(ga_ref[...]) * at_ref[...] + _sig(gl_ref[...]) * lru]

    pspec = lambda off: pl.BlockSpec((bm, cw), lambda i, j: (i + nCb, off // cw + j))
    tspec = pl.BlockSpec((bm, cw), lambda i, j: (i + nCb, j))
    sspec = pl.BlockSpec((bm, cw), lambda i, j: (i, j))
    return ew_call(
        "merge_fwd", (dm.nSb, D // cw), fn,
        [(P, pspec(dm.OFF_LG)), (P, pspec(dm.OFF_GA)), (P, pspec(dm.OFF_GL)), (attn, sspec), (hf, tspec), (hb, tspec)],
        [(_sds((S, D), BF), sspec, False)])[0]


def merge_bwd(dm, dmg, P, attn, hf, hb):
    S, T, D, bm, cw, nCb = dm.S, dm.T, dm.D, dm.bm, dm.cw, dm.nCb
    nj = D // cw

    def body(dm_ref, lg_ref, ga_ref, gl_ref, at_ref, hf_ref, hb_ref, dp_ref, da_ref, dh_ref, buf, sems):
        i, j = pl.program_id(0), pl.program_id(1)
        lat = i >= nCb
        d = jnp.where(lat, dm_ref[...].astype(F32), 0.0)
        lg = lg_ref[...]
        ge, th = _gelu(lg)
        hs = hf_ref[...] + hb_ref[...]
        sa, sl = _sig(ga_ref[...]), _sig(gl_ref[...])
        at = jnp.where(lat, at_ref[...], 0.0)
        dlru = d * sl
        buf[0] = (dlru * hs * _gelu_grad(lg, th)).astype(BF)
        buf[1] = (d * at * sa * (1.0 - sa)).astype(BF)
        buf[2] = (d * hs * ge * sl * (1.0 - sl)).astype(BF)
        da_ref[...] = (d * sa).astype(BF)
        dh_ref[...] = dlru * ge
        copies = []
        for g, off in enumerate((dm.OFF_LG, dm.OFF_GA, dm.OFF_GL)):
            col = pl.multiple_of(off + j * cw, 128)
            cp = pltpu.make_async_copy(buf.at[g], dp_ref.at[pl.ds(pl.multiple_of(i * bm, 8), bm), pl.ds(col, cw)],
                                       sems.at[g])
            cp.start()
            copies.append(cp)
        for cp in copies:
            cp.wait()

    pspec = lambda off: pl.BlockSpec((bm, cw), lambda i, j: (i, off // cw + j))
    tspec = pl.BlockSpec((bm, cw), lambda i, j: (i, j))
    lspec = pl.BlockSpec((bm, cw), lambda i, j: (jnp.maximum(i - nCb, 0), j))
    return pl.pallas_call(
        body, grid=(dm.nTb, nj), name="merge_bwd",
        in_specs=[lspec, pspec(dm.OFF_LG), pspec(dm.OFF_GA), pspec(dm.OFF_GL), lspec, tspec, tspec],
        out_specs=[ANY, tspec, tspec],
        out_shape=[_sds((T, dm.INW), BF), _sds((T, D), BF), _sds((T, D), F32)],
        scratch_shapes=[pltpu.VMEM((3, bm, cw), BF), pltpu.SemaphoreType.DMA((3,))],
        compiler_params=_cparams(),
    )(dmg, P, P, P, attn, hf, hb)


def final_loss(dm, x3, gfin, target):
    S, D, bm = dm.S, dm.D, dm.bm

    def fn(ids, x_ref, g_ref, t_ref):
        xv = x_ref[...]
        g = g_ref[...]
        r = lax.rsqrt(jnp.mean(xv * xv, axis=-1, keepdims=True) + EPS)
        xn = xv * r
        err = xn * g - t_ref[...]
        loss = 0.5 * jnp.sum(jnp.mean(err * err, axis=-1, keepdims=True), axis=0, keepdims=True)
        dy = err / D
        dxn = dy * g
        dx = r * (dxn - xn * jnp.mean(dxn * xn, axis=-1, keepdims=True))
        return [jnp.broadcast_to(loss, (1, 128)), dx, jnp.sum(dy * xn, axis=0, keepdims=True)]

    row = pl.BlockSpec((bm, D), lambda i: (i, 0))
    vec = pl.BlockSpec((1, D), lambda i: (0, 0))
    return ew_call(
        "final_loss", (dm.nSb,), fn, [(x3, row), (gfin, vec), (target, row)],
        [(_sds((1, 128), F32), pl.BlockSpec((1, 128), lambda i: (0, 0)), True), (_sds((S, D), F32), row, False),
         (_sds((1, D), F32), vec, True)], first=lambda ids: ids[0] == 0)


def local_step(dm, x, ctx, target, modv, norm_g3, gfin, gq, gk, conv_w, conv_b, wa, ba, wx, bx, lam, wbuf, where):
    S, C, T, D, NS, F4, W4 = dm.S, dm.C, dm.T, dm.D, dm.NS, dm.F4, dm.W4
    Ds = D // NS
    wb, nsub = dm.wb, dm.nsub
    cosf, sinf = rope_tables(dm)
    sp = jax.nn.softplus(-lam)
    sg = jax.nn.sigmoid(-lam)
    ident = lambda ids, accs, ex: list(accs)
    mT, mS, kT, kS = dm.mT, dm.mS, dm.kT, dm.kS
    bn = _pick(D, [1024, 512, 256, 128])
    bk = _pick(D, [512, 256, 128])

    wg0, wu0 = comm_call("ag_ffn1", ag_comm([wbuf['wg0'], wbuf['wu0']]))
    h1, xt = normmod_concat_fwd("nm1", dm, ctx, x, norm_g3, modv)
    xt1, a1, u1, s1, f1, land, _ = ffn_fwd("ffn1", dm, h1, xt, wg0, wu0, None, modv, 2, True,
                                           comm_up=ag_comm([wbuf['wd0'], wbuf['w_in']]))
    wd0, w_in = land
    h2 = normmod_fwd("nm2", dm, xt1, norm_g3, 1, modv, True)
    P = fused_mm(
        "w_in", (T // mT, NS * nsub, 1),
        [(h2, pl.BlockSpec((mT, D), lambda i, j, k: (i, 0))),
         (w_in, pl.BlockSpec((None, D, wb), lambda i, j, k: (j // nsub, 0, j % nsub)))],
        [(0, 1, NN, 0)], [(mT, wb)], ident,
        [(_sds((T, dm.INW), F32), pl.BlockSpec((mT, wb), lambda i, j, k: (i, j)))])[0]
    qr, kr, vb = qk_prep(dm, P, gq, gk, cosf, sinf)
    (attn,), (wg1, wu1, wd1, w_out) = attention_fwd(
        dm, qr, kr, vb, comm=ag_comm([wbuf['wg1'], wbuf['wu1'], wbuf['wd1'], wbuf['w_out']]))
    w_out = w_out.reshape(D, D)
    hf, hb = lru_fwd(dm, P, conv_w, conv_b, wa, ba, wx, bx, sp)
    mg = merge_fwd(dm, P, attn, hf, hb)

    def epi_o(ids, accs, ex):
        o = accs[0]
        return [ex[0][...] + ex[1][...] * o, o]

    rb, nCb = dm.bm, dm.nCb
    x2, o2 = fused_mm(
        "w_out", (D // bn, S // rb, 1),
        [(mg, pl.BlockSpec((rb, D), lambda j, i, k: (i, 0))), (w_out, pl.BlockSpec((D, bn), lambda j, i, k: (0, j)))],
        [(0, 1, NN, 0)], [(rb, bn)], epi_o,
        [(_sds((S, D), F32), pl.BlockSpec((rb, bn), lambda j, i, k: (i, j))),
         (_sds((S, D), BF), pl.BlockSpec((rb, bn), lambda j, i, k: (i, j)))],
        extras=[(xt1, pl.BlockSpec((rb, bn), lambda j, i, k: (i + nCb, j))),
                (modv, pl.BlockSpec((None, None, 1, bn), lambda j, i, k: (1, 5, 0, j)))])
    h3 = normmod_fwd("nm3", dm, x2, norm_g3, 2, modv, False)
    x3, a3, u3, s3, f3, _, _ = ffn_fwd("ffn2", dm, h3, x2, wg1, wu1, wd1, modv, 8, False)
    loss, dx3, dgfin = final_loss(dm, x3, gfin, target)

    df3, dg3 = gate_bwd("gate3", dm, dx3, f3, modv, 8, FFN_RES, False)
    dh3, dwg1, dwu1, dwd1, _ = ffn_bwd("ffn2b", dm, df3, h3, a3, u3, s3, wg1, wu1, wd1, False)
    dx2, dsh3, dsc3, dgn3 = normmod_bwd("nm3b", dm, dh3, x2, dx3, norm_g3, 2, modv, False, False)
    do2, dg2 = gate_bwd("gate2", dm, dx2, o2, modv, 5, 1.0, False)
    keep = {}

    def host_a(key, comm):
        if key == 'p1':
            (keep['dmg'],), landed = fused_mm(
                "w_out_dx", (S // mS, D // bn, 1),
                [(do2, pl.BlockSpec((mS, D), lambda i, j, k: (i, 0))),
                 (w_out, pl.BlockSpec((bn, D), lambda i, j, k: (j, 0)))],
                [(0, 1, NT, 0)], [(mS, bn)], ident,
                [(_sds((S, D), BF), pl.BlockSpec((mS, bn), lambda i, j, k: (i, j)))], comm=comm)
            return landed
        keep['dqkv'], landed = attention_bwd(dm, qr, kr, vb, attn, keep['dattn'], comm=comm)
        return landed

    gots_a = host_a('p1', rs_p1_comm([dwg1, dwu1, dwd1]))
    dmg = keep['dmg']
    dw_out = fused_mm(
        "w_out_dw", (D // bn, D // bn, S // kS),
        [(mg, pl.BlockSpec((kS, bn), lambda i, j, k: (k, i))), (do2, pl.BlockSpec((kS, bn), lambda i, j, k: (k, j)))],
        [(0, 1, TN, 0)], [(bn, bn)], ident,
        [(_sds((D, D), BF), pl.BlockSpec((bn, bn), lambda i, j, k: (i, j)))])[0]
    dP, dattn, dhs = merge_bwd(dm, dmg, P, attn, hf, hb)
    keep['dattn'] = dattn
    pairs_a = [add_pair("rs_add_" + n_, g_, got_, where)
               for n_, g_, got_ in zip(('wg1', 'wu1', 'wd1'), (dwg1, dwu1, dwd1), gots_a)]
    land_a = host_a('p2', rs_p2_comm([p_[0] for p_ in pairs_a], [p_[1] for p_ in pairs_a]))
    dq, dk, dv = keep['dqkv']
    dP, dwa, dba, dwx, dbx, dlam, dcw, dcb = lru_bwd(dm, P, dhs, hf, hb, conv_w, conv_b, wa, ba, wx, bx, sp, sg, dP)
    dP, dgq, dgk = qk_prep_bwd(dm, dq, dk, dv, P, gq, gk, cosf, sinf, dP)
    g_wg = sum_slots_into("rs_sum_wg1", land_a[0], where, None, (2, D, F4), 1)
    g_wu = sum_slots_into("rs_sum_wu1", land_a[1], where, None, (2, D, F4), 1)
    g_wd = sum_slots_into("rs_sum_wd1", land_a[2], where, None, (2, F4, D), 1)
    LBD = D // dm.LB
    NM = 2 * dm.LB * LBD
    mats = [dwa.reshape(1, NM, LBD), dwx.reshape(1, NM, LBD)]
    (dh2,), landed_x = fused_mm(
        "w_in_dx", (T // mT, D // bn, NS),
        [(dP, pl.BlockSpec((mT, W4), lambda i, j, k: (i, k))),
         (w_in, pl.BlockSpec((None, bn, W4), lambda i, j, k: (k, j, 0)))],
        [(0, 1, NT, 0)], [(mT, bn)], ident,
        [(_sds((T, D), F32), pl.BlockSpec((mT, bn), lambda i, j, k: (i, j)))],
        comm=merge_comms([rs_p3_comm([g_wg, g_wu, g_wd], [(0, 1, D // 2), (1, 1, D // 2), (2, 1, F4 // 2)]),
                          rs_p1_comm(mats)]))
    g_wg, g_wu, g_wd = landed_x[:3]
    pairs_m = [add_pair("rs_add_" + n_, g_, got_, where) for n_, g_, got_ in zip(('lru_wa', 'lru_wx'), mats, landed_x[3:])]
    (dw_in,), land_m = fused_mm(
        "w_in_dw", (D // bn, NS, T // kT),
        [(h2, pl.BlockSpec((kT, bn), lambda i, j, k: (k, i))), (dP, pl.BlockSpec((kT, W4), lambda i, j, k: (k, j)))],
        [(0, 1, TN, 0)], [(bn, W4)], ident,
        [(_sds((NS, D, W4), BF), pl.BlockSpec((None, bn, W4), lambda i, j, k: (j, i, 0)))],
        comm=rs_p2_comm([p_[0] for p_ in pairs_m], [p_[1] for p_ in pairs_m]))
    g_wa = sum_slots_into("rs_sum_lru_wa", land_m[0], where, None, (NM, LBD), None)
    g_wx = sum_slots_into("rs_sum_lru_wx", land_m[1], where, None, (NM, LBD), None)
    dxt1, dsh2, dsc2, dgn2 = normmod_bwd("nm2b", dm, dh2, xt1, dx2, norm_g3, 1, modv, True, True)
    df1, dg1 = gate_bwd("gate1", dm, dxt1, f1, modv, 2, FFN_RES, True)

    tens_b = [dw_in, dw_out.reshape(NS, Ds, D)]

    def host_ds(landed, made):
        return merge_comms([rs_p1_comm(tens_b), rs_p3_comm([g_wa, g_wx], [(0, None, NM // 2), (1, None, NM // 2)])])

    def host_dwgu(landed, made):
        pairs = [add_pair("rs_add_" + n_, g_, got_, where) for n_, g_, got_ in zip(('w_in', 'w_out'), tens_b, landed['ds'][:2])]
        return rs_p2_comm([p_[0] for p_ in pairs], [p_[1] for p_ in pairs])

    def host_dwd(landed, made):
        g_win = sum_slots_into("rs_sum_w_in", landed['dwgu'][0], where, None, (D, W4), None)
        g_wout = sum_slots_into("rs_sum_w_out", landed['dwgu'][1], where, None, (Ds, D), None)
        return merge_comms([rs_p3_comm([g_win, g_wout], [(0, None, D // 2), (1, None, Ds // 2)]),
                            rs_p1_comm([made['dwg'], made['dwu']])])

    def host_dh(landed, made):
        pairs = [add_pair("rs_add_" + n_, g_, got_, where)
                 for n_, g_, got_ in zip(('wg0', 'wu0'), (made['dwg'], made['dwu']), landed['dwd'][2:])]
        return merge_comms([rs_p2_comm([p_[0] for p_ in pairs], [p_[1] for p_ in pairs]), rs_p1_comm([made['dwd']])])

    dh1, dwg0, dwu0, dwd0, landed = ffn_bwd(
        "ffn1b", dm, df1, h1, a1, u1, s1, wg0, wu0, wd0, True,
        comms={'ds': host_ds, 'dwgu': host_dwgu, 'dwd': host_dwd, 'dh': host_dh})
    g_wa, g_wx = landed['ds'][2:]
    g_win, g_wout = landed['dwd'][:2]
    g_wg = sum_slots_into("rs_sum_wg0", landed['dh'][0], where, g_wg, (2, D, F4), 0)
    g_wu = sum_slots_into("rs_sum_wu0", landed['dh'][1], where, g_wu, (2, D, F4), 0)
    late = add_pair("rs_add_wd0", dwd0, landed['dh'][2], where)
    grad_x, dsh1, dsc1, dgn1 = normmod_bwd("nm1b", dm, dh1, xt, dxt1, norm_g3, 0, modv, True, False, out_lat_only=True)

    dmod = jnp.concatenate([dsh1, dsc1, dg1, dsh2, dsc2, _lat(dg2), _lat(dsh3), _lat(dsc3), _lat(dg3)], axis=1)
    dnorm = jnp.stack([dgn1[0, 0] + dgn1[1, 0], dgn2[0, 0] + dgn2[1, 0], dgn3[1, 0]], axis=0)
    small = dict(norm_g=dnorm, q_norm_g=dgq, k_norm_g=dgk, conv_w=dcw, conv_b=dcb,
                 lru_ba=dba.reshape(2, D), lru_bx=dbx.reshape(2, D), lru_lambda=dlam.reshape(2, D), final_norm_g=dgfin)
    reduced = dict(ffn_wg=g_wg, ffn_wu=g_wu, ffn_wd=g_wd, w_in=g_win, w_out=g_wout, lru_wa=g_wa, lru_wx=g_wx)
    return loss, grad_x, dmod, small, reduced, late


def _lat(v):
    return jnp.concatenate([jnp.zeros_like(v[:1]), v[1:]], axis=0)


def _me():
    return lax.axis_index("x"), lax.axis_index("y"), lax.axis_index("c")


def allgather8(name, v):
    def body(v_ref, out_ref, send_sems, recv_sems, local_sem):
        x, y, c = _me()
        me = 4 * x + 2 * y + c
        mine = pltpu.make_async_copy(v_ref, out_ref.at[me], local_sem)
        mine.start()
        copies = []
        for k in range(1, 8):
            peer = (x ^ ((k >> 2) & 1), y ^ ((k >> 1) & 1), c ^ (k & 1))
            cp = pltpu.make_async_remote_copy(src_ref=v_ref, dst_ref=out_ref.at[me], send_sem=send_sems.at[k - 1],
                                              recv_sem=recv_sems.at[k - 1], device_id=peer, device_id_type=MESH)
            cp.start()
            copies.append(cp)
        for k in range(1, 8):
            peer = (x ^ ((k >> 2) & 1), y ^ ((k >> 1) & 1), c ^ (k & 1))
            pltpu.make_async_remote_copy(src_ref=v_ref, dst_ref=out_ref.at[me ^ k], send_sem=send_sems.at[k - 1],
                                         recv_sem=recv_sems.at[k - 1], device_id=peer, device_id_type=MESH).wait_recv()
        for cp in copies:
            cp.wait_send()
        mine.wait()

    return pl.pallas_call(
        body, name=name, out_shape=_sds((8,) + v.shape, v.dtype), in_specs=[ANY], out_specs=ANY,
        scratch_shapes=[pltpu.SemaphoreType.DMA((7,)), pltpu.SemaphoreType.DMA((7,)), pltpu.SemaphoreType.DMA],
    )(v)


def _chips(x, y):
    chips = [(1 - x, y), (x, 1 - y), (1 - x, 1 - y)]
    return chips, [2 * cx + cy for cx, cy in chips]


def ag_comm(bufs):
    n = len(bufs)

    def parts(outs):
        x, y, c = _me()
        chips, slots = _chips(x, y)
        return x, y, c, 2 * x + y, (x, y, 1 - c), chips, slots

    def ici(outs, t, j, send_sems, recv_sems, src_slot, off):
        x, y, c, s, sib, chips, slots = parts(outs)
        H = outs[t].shape[1] // 2
        blk = outs[t].at[src_slot, pl.ds(c * H, H)]
        return pltpu.make_async_remote_copy(
            src_ref=blk, dst_ref=blk, send_sem=send_sems.at[off + 6 * t + j], recv_sem=recv_sems.at[off + 6 * t + j],
            device_id=(chips[j][0], chips[j][1], c), device_id_type=MESH)

    def d2d(outs, t, j, send_sems, recv_sems, half, off):
        x, y, c, s, sib, chips, slots = parts(outs)
        H = outs[t].shape[1] // 2
        blk = outs[t].at[slots[j], pl.ds(half * H, H)]
        return pltpu.make_async_remote_copy(
            src_ref=blk, dst_ref=blk, send_sem=send_sems.at[off + 6 * t + 3 + j],
            recv_sem=recv_sems.at[off + 6 * t + 3 + j], device_id=sib, device_id_type=MESH)

    def start(reads, outs, send_sems, recv_sems, off=0):
        x, y, c, s, sib, chips, slots = parts(outs)
        for t in range(n):
            for j in range(3):
                ici(outs, t, j, send_sems, recv_sems, s, off).start()

    def finish(reads, outs, send_sems, recv_sems, off=0):
        x, y, c, s, sib, chips, slots = parts(outs)
        for t in range(n):
            for j in range(3):
                ici(outs, t, j, send_sems, recv_sems, slots[j], off).wait_recv()
                d2d(outs, t, j, send_sems, recv_sems, c, off).start()
        for t in range(n):
            for j in range(3):
                d2d(outs, t, j, send_sems, recv_sems, 1 - c, off).wait_recv()
        for t in range(n):
            for j in range(3):
                ici(outs, t, j, send_sems, recv_sems, s, off).wait_send()
                d2d(outs, t, j, send_sems, recv_sems, c, off).wait_send()

    return Comm([], bufs, 6 * n, start, finish)


def rs_p1_comm(tensors):
    n = len(tensors)

    def copy(ins, gots, t, send_sems, recv_sems, off):
        x, y, c = _me()
        H = ins[t].shape[1] // 2
        return pltpu.make_async_remote_copy(
            src_ref=ins[t].at[:, pl.ds((1 - c) * H, H)], dst_ref=gots[t], send_sem=send_sems.at[off + t],
            recv_sem=recv_sems.at[off + t], device_id=(x, y, 1 - c), device_id_type=MESH)

    def start(ins, gots, send_sems, recv_sems, off=0):
        for t in range(n):
            copy(ins, gots, t, send_sems, recv_sems, off).start()

    def finish(ins, gots, send_sems, recv_sems, off=0):
        for t in range(n):
            copy(ins, gots, t, send_sems, recv_sems, off).wait_recv()
        for t in range(n):
            copy(ins, gots, t, send_sems, recv_sems, off).wait_send()

    half = lambda t: _sds((t.shape[0], t.shape[1] // 2) + t.shape[2:], t.dtype)
    return Comm(tensors, [half(t) for t in tensors], n, start, finish)


def rs_p2_comm(partials, landeds):
    n = len(partials)

    def start(ins, outs, send_sems, recv_sems, off=0):
        x, y, c = _me()
        s = 2 * x + y
        chips, slots = _chips(x, y)
        for t in range(n):
            for j, chip in enumerate(chips):
                src = ins[t].at[slots[j]] if ins[t].shape[0] == 4 else ins[t].at[0]
                pltpu.make_async_remote_copy(
                    src_ref=src, dst_ref=outs[t].at[s], send_sem=send_sems.at[off + 3 * t + j],
                    recv_sem=recv_sems.at[off + 3 * t + j], device_id=(chip[0], chip[1], c), device_id_type=MESH).start()

    def finish(ins, outs, send_sems, recv_sems, off=0):
        x, y, c = _me()
        s = 2 * x + y
        chips, slots = _chips(x, y)
        for t in range(n):
            for j, chip in enumerate(chips):
                dst = outs[t].at[slots[j]]
                pltpu.make_async_remote_copy(
                    src_ref=dst, dst_ref=dst, send_sem=send_sems.at[off + 3 * t + j],
                    recv_sem=recv_sems.at[off + 3 * t + j], device_id=(chip[0], chip[1], c), device_id_type=MESH).wait_recv()
        for t in range(n):
            for j, chip in enumerate(chips):
                src = ins[t].at[slots[j]] if ins[t].shape[0] == 4 else ins[t].at[0]
                pltpu.make_async_remote_copy(
                    src_ref=src, dst_ref=outs[t].at[s], send_sem=send_sems.at[off + 3 * t + j],
                    recv_sem=recv_sems.at[off + 3 * t + j], device_id=(chip[0], chip[1], c), device_id_type=MESH).wait_send()

    return Comm(partials, landeds, 3 * n, start, finish)


def rs_p3_comm(greds, plan):
    n = len(plan)

    def copy(outs, t, send_sems, recv_sems, half, off):
        x, y, c = _me()
        oi, li, H = plan[t]
        dst = outs[oi] if li is None else outs[oi].at[li]
        blk = dst.at[pl.ds((c if half == 0 else 1 - c) * H, H)]
        return pltpu.make_async_remote_copy(
            src_ref=blk, dst_ref=blk, send_sem=send_sems.at[off + t], recv_sem=recv_sems.at[off + t],
            device_id=(x, y, 1 - c), device_id_type=MESH)

    def start(reads, outs, send_sems, recv_sems, off=0):
        for t in range(n):
            copy(outs, t, send_sems, recv_sems, 0, off).start()

    def finish(reads, outs, send_sems, recv_sems, off=0):
        for t in range(n):
            copy(outs, t, send_sems, recv_sems, 1, off).wait_recv()
        for t in range(n):
            copy(outs, t, send_sems, recv_sems, 0, off).wait_send()

    return Comm([], greds, n, start, finish)


def _rows_block(rows, cols, nbytes=1 << 20):
    bm = 8
    while bm * 2 * cols * 4 <= nbytes and rows % (bm * 2) == 0:
        bm *= 2
    return bm


def cast_into_slot(name, w, where, layer=None):
    rows, W = w.shape[-2:]
    bm = _rows_block(rows, W)

    def body(p_ref, w_ref, o_ref):
        o_ref[...] = w_ref[...].astype(BF)

    if layer is None:
        ispec = pl.BlockSpec((bm, W), lambda i, p: (i, 0))
    else:
        ispec = pl.BlockSpec((None, bm, W), lambda i, p: (layer, i, 0))
    return pl.pallas_call(
        body, name=name, out_shape=_sds((4, rows, W), BF), compiler_params=_cparams(),
        grid_spec=pltpu.PrefetchScalarGridSpec(
            num_scalar_prefetch=1, grid=(rows // bm,), in_specs=[ispec],
            out_specs=pl.BlockSpec((None, bm, W), lambda i, p: (p[1], i, 0))),
    )(where, w)


def add_pair(name, g, got, where):
    K, R, W = g.shape
    H = R // 2
    bm = _rows_block(H, W)
    nh = H // bm

    def body(p_ref, g_ref, got_ref, part_ref, land_ref):
        k = pl.program_id(1)
        v = (g_ref[...].astype(F32) + got_ref[...].astype(F32)).astype(part_ref.dtype)
        part_ref[...] = v
        own = (k == p_ref[1]) if K == 4 else (k == 0)

        @pl.when(own)
        def _():
            land_ref[...] = v

    return pl.pallas_call(
        body, name=name, out_shape=[_sds((K, H, W), g.dtype), _sds((4, H, W), g.dtype)], compiler_params=_cparams(),
        grid_spec=pltpu.PrefetchScalarGridSpec(
            num_scalar_prefetch=1, grid=(nh, K),
            in_specs=[pl.BlockSpec((None, bm, W), lambda i, k, p: (k, p[0] * nh + i, 0)),
                      pl.BlockSpec((None, bm, W), lambda i, k, p: (k, i, 0))],
            out_specs=[pl.BlockSpec((None, bm, W), lambda i, k, p: (k, i, 0)),
                       pl.BlockSpec((None, bm, W), lambda i, k, p: (p[1], i, 0))]),
    )(where, g, got)


def sum_slots_into(name, landed, where, dest, dest_shape, li):
    K, H, W = landed.shape
    bm = _rows_block(H, 2 * W)
    nh = H // bm

    def body(*refs):
        r, o_ref = refs[1], refs[-1]
        acc = r[0].astype(F32)
        for k in range(1, K):
            acc = acc + r[k].astype(F32)
        o_ref[...] = acc

    if li is None:
        ospec = pl.BlockSpec((bm, W), lambda i, p: (p[0] * nh + i, 0))
    else:
        ospec = pl.BlockSpec((None, bm, W), lambda i, p: (li, p[0] * nh + i, 0))
    in_specs = [pl.BlockSpec((K, bm, W), lambda i, p: (0, i, 0))]
    args = [where, landed]
    aliases = {}
    if dest is not None:
        in_specs.append(ANY)
        args.append(dest)
        aliases = {2: 0}
    return pl.pallas_call(
        body, name=name, out_shape=_sds(dest_shape, F32), compiler_params=_cparams(), input_output_aliases=aliases,
        grid_spec=pltpu.PrefetchScalarGridSpec(num_scalar_prefetch=1, grid=(nh,), in_specs=in_specs, out_specs=ospec),
    )(*args)


def sum_slots(name, a):
    K, H, W = a.shape
    bm = _rows_block(H, W * K // 2)

    def fn(ids, r):
        acc = r[0]
        for k in range(1, K):
            acc = acc + r[k]
        return [acc]

    return ew_call(name, (H // bm,), fn, [(a, pl.BlockSpec((K, bm, W), lambda i: (0, i, 0)))],
                   [(_sds((H, W), F32), pl.BlockSpec((bm, W), lambda i: (i, 0)), False)])[0]


def _adamw_math(w, g, m, v):
    bc1 = 1.0 - ADAM_B1 ** ADAM_STEP
    bc2 = 1.0 - ADAM_B2 ** ADAM_STEP
    mn = ADAM_B1 * m + (1.0 - ADAM_B1) * g
    vn = ADAM_B2 * v + (1.0 - ADAM_B2) * (g * g)
    m_hat = mn / bc1
    v_hat = vn / bc2
    delta = -ADAM_LR * (m_hat / (jnp.sqrt(v_hat) + ADAM_EPS) + ADAM_WD * w)
    return delta, mn, vn


def adamw(name, w, g, m, v, copy_grad=False):
    shape = w.shape
    flat = lambda t: t.reshape(-1, shape[-1])
    w2, g2, m2, v2 = flat(w), flat(g), flat(m), flat(v)
    bm = _rows_block(w2.shape[0], w2.shape[1])

    def fn(ids, w_ref, g_ref, m_ref, v_ref):
        gv = g_ref[...]
        return list(_adamw_math(w_ref[...], gv, m_ref[...], v_ref[...])) + ([gv] if copy_grad else [])

    spec = pl.BlockSpec((bm, w2.shape[1]), lambda i: (i, 0))
    res = ew_call(name, (w2.shape[0] // bm,), fn, [(w2, spec), (g2, spec), (m2, spec), (v2, spec)],
                  [(_sds(w2.shape, F32), spec, False)] * (4 if copy_grad else 3))
    return [o.reshape(shape) for o in res]


def adamw_many(name, params):
    n = len(params)
    shapes = [p_[0].shape for p_ in params]
    two_d = lambda t: t.reshape(-1, t.shape[-1])
    args = [two_d(t) for p_ in params for t in p_]

    def body(*refs):
        ins, outs = refs[:4 * n], refs[4 * n:]
        for q in range(n):
            w_ref, g_ref, m_ref, v_ref = ins[4 * q:4 * q + 4]
            for o_ref, val in zip(outs[3 * q:3 * q + 3], _adamw_math(w_ref[...], g_ref[...], m_ref[...], v_ref[...])):
                o_ref[...] = val

    full = lambda a: pl.BlockSpec(a.shape, lambda i: (0, 0))
    out_shape = [_sds(args[4 * q].shape, F32) for q in range(n) for _ in range(3)]
    res = hosted_call(body, name=name, grid=(1,), in_specs=[full(a) for a in args],
                      out_specs=[full(o) for o in out_shape], out_shape=out_shape, args=args)
    return [tuple(res[3 * q + r].reshape(shapes[q]) for r in range(3)) for q in range(n)]


def dmod_pack(gd):
    N = gd.shape[-1]
    bn = _pick(N, [4608, 2304, 1152, 1024, 512, 256, 128])

    def fn(ids, r):
        lat = [r[d, 1:2, :] for d in range(8)]
        cs = r[0, 0:1, :]
        for d in range(1, 8):
            cs = cs + r[d, 0:1, :]
        tot = cs
        for d in range(8):
            tot = tot + lat[d]
        return [jnp.concatenate(lat + [cs, jnp.zeros((7, bn), F32)], axis=0), tot]

    return ew_call("dmod_pack", (N // bn,), fn, [(gd, pl.BlockSpec((8, 2, bn), lambda j: (0, 0, j)))],
                   [(_sds((16, N), F32), pl.BlockSpec((16, bn), lambda j: (0, j)), False),
                    (_sds((1, N), F32), pl.BlockSpec((1, bn), lambda j: (0, j)), False)])


def _silu(v):
    return v * _sig(v)


def kernel(x, c, ctx, c_ctx, w_mod, b_mod, norm_g, ffn_wg, ffn_wu, ffn_wd, w_in, w_out, q_norm_g, k_norm_g, conv_w, conv_b, lru_wa, lru_ba, lru_wx, lru_bx, lru_lambda, final_norm_g, loss_target, m_c_ctx, m_w_mod, m_b_mod, m_norm_g, m_ffn_wg, m_ffn_wu, m_ffn_wd, m_w_in, m_w_out, m_q_norm_g, m_k_norm_g, m_conv_w, m_conv_b, m_lru_wa, m_lru_ba, m_lru_wx, m_lru_bx, m_lru_lambda, m_final_norm_g, v_c_ctx, v_w_mod, v_b_mod, v_norm_g, v_ffn_wg, v_ffn_wu, v_ffn_wd, v_w_in, v_w_out, v_q_norm_g, v_k_norm_g, v_conv_w, v_conv_b, v_lru_wa, v_lru_ba, v_lru_wx, v_lru_bx, v_lru_lambda, v_final_norm_g):
    given = dict(locals())
    names = ['c_ctx', 'w_mod', 'b_mod', 'norm_g', 'ffn_wg', 'ffn_wu', 'ffn_wd', 'w_in', 'w_out', 'q_norm_g', 'k_norm_g',
             'conv_w', 'conv_b', 'lru_wa', 'lru_ba', 'lru_wx', 'lru_bx', 'lru_lambda', 'final_norm_g']
    S, D = x.shape[1], x.shape[2]
    C = ctx.shape[1]
    NS = 4
    F4, W4, LB = ffn_wg.shape[-1], w_in.shape[-1], lru_wa.shape[2]
    dm = Dims(S, C, D, F4, W4, NS, LB)
    Ds = D // NS
    Wm = w_mod.shape[-1]
    xi, yi, ci = lax.axis_index("x"), lax.axis_index("y"), lax.axis_index("c")
    slot = 2 * xi + yi
    me = 4 * xi + 2 * yi + ci
    ident = lambda ids, accs, ex: list(accs)

    pack1 = jnp.concatenate([c.reshape(-1), norm_g.reshape(-1), conv_w.reshape(-1), lru_ba.reshape(-1),
                             lru_bx.reshape(-1), lru_lambda.reshape(-1)]).reshape(1, -1)
    g1 = allgather8("ag_small_params", pack1)[:, 0]
    c_all = g1[:, :D]

    def unshard(off, k):
        part = g1[0::2, off:off + k * Ds].reshape(NS, k, Ds)
        return jnp.transpose(part, (1, 0, 2)).reshape(k, D)

    norm_g_f = unshard(D, 3)
    conv_w_f = unshard(D + 3 * Ds, 4)
    ba_f = unshard(D + 7 * Ds, 2)
    bx_f = unshard(D + 9 * Ds, 2)
    lam_f = unshard(D + 11 * Ds, 2)

    call16 = jnp.concatenate([c_all, c_ctx.reshape(1, D), jnp.zeros((7, D), F32)], axis=0)
    b_cols = lax.dynamic_slice(b_mod, (0, slot * Wm), (1, Wm))
    bnm = _pick(Wm, [1536, 1152, 768, 512, 384, 256, 128])
    bkm = _pick(D, [512, 256, 128])
    modp = fused_mm(
        "mod_fwd", (1, Wm // bnm, D // bkm),
        [(call16, pl.BlockSpec((16, bkm), lambda i, j, k: (0, k))),
         (w_mod[0], pl.BlockSpec((bkm, bnm), lambda i, j, k: (k, j)))],
        [(0, 1, NN, 0)], [(16, bnm)], lambda ids, accs, ex: [accs[0] + ex[0][...]],
        [(_sds((16, Wm), F32), pl.BlockSpec((16, bnm), lambda i, j, k: (0, j)))],
        extras=[(b_cols, pl.BlockSpec((1, bnm), lambda i, j, k: (0, j)))], pre={0: _silu})[0]
    gm = allgather8("ag_mod", modp)
    mod_full = jnp.concatenate([gm[0], gm[2], gm[4], gm[6]], axis=1)
    mod_x = lax.dynamic_index_in_dim(mod_full, me, axis=0, keepdims=False)
    modv = jnp.stack([mod_full[8], mod_x]).reshape(2, N_MOD, 1, D)

    where = jnp.stack([ci, slot]).astype(jnp.int32)
    wbuf = {}
    for key, short in (('ffn_wg', 'wg'), ('ffn_wu', 'wu'), ('ffn_wd', 'wd')):
        for l in range(2):
            wbuf[short + str(l)] = cast_into_slot("cast_%s%d" % (short, l), given[key][0], where, l)
    wbuf['w_in'] = cast_into_slot("cast_w_in", w_in[0], where)
    wbuf['w_out'] = cast_into_slot("cast_w_out", w_out[0], where)

    loss_l, grad_x, dmod, small, reduced, late = local_step(
        dm, x[0], ctx[0], loss_target[0], modv, norm_g_f.reshape(3, 1, D), final_norm_g.reshape(1, D),
        q_norm_g, k_norm_g, conv_w_f, conv_b, lru_wa[0], ba_f.reshape(2, 1, D), lru_wx[0], bx_f.reshape(2, 1, D),
        lam_f.reshape(2, 1, D), wbuf, where)
    loss = lax.psum(loss_l[0, 0], ("x", "y", "c"))

    grads = {}
    gd = allgather8("ag_dmod", dmod.reshape(2, N_MOD * D))
    dM, g_bmod = dmod_pack(gd)
    dMc = lax.dynamic_slice(dM, (0, slot * Wm), (16, Wm))
    bmm = _pick(D, [512, 256, 128])
    grads['w_mod'] = fused_mm(
        "w_mod_dw", (D // bmm, Wm // bnm, 1),
        [(call16, pl.BlockSpec((16, bmm), lambda i, j, k: (0, i))), (dMc, pl.BlockSpec((16, bnm), lambda i, j, k: (0, j)))],
        [(0, 1, TN, 0)], [(bmm, bnm)], ident,
        [(_sds((D, Wm), F32), pl.BlockSpec((bmm, bnm), lambda i, j, k: (i, j)))], pre={0: _silu})[0][None]
    grads['b_mod'] = g_bmod

    def epi_cc(ids, accs, ex):
        v = ex[0][...]
        sg = _sig(v)
        return [accs[0] * (sg * (1.0 + v * (1.0 - sg)))]

    pcc = fused_mm(
        "c_ctx_partial", (1, D // bmm, Wm // bnm),
        [(dMc, pl.BlockSpec((16, bnm), lambda i, j, k: (0, k))), (w_mod[0], pl.BlockSpec((bmm, bnm), lambda i, j, k: (j, k)))],
        [(0, 1, NT, 0)], [(16, bmm)], epi_cc,
        [(_sds((16, D), F32), pl.BlockSpec((16, bmm), lambda i, j, k: (0, j)))],
        extras=[(c_ctx.reshape(1, D), pl.BlockSpec((1, bmm), lambda i, j, k: (0, j)))])[0]
    pcc_row = jnp.where(ci == 0, pcc[8], 0.0)

    order = ['q_norm_g', 'k_norm_g', 'conv_b', 'final_norm_g', 'norm_g', 'conv_w', 'lru_ba', 'lru_bx', 'lru_lambda']
    flat = [small[k].reshape(-1) for k in order] + [pcc_row]
    sizes = [f.shape[0] for f in flat]
    tot = sum(sizes)
    LW = 1024
    padded = -(-tot // (8 * LW)) * (8 * LW)
    tiny = jnp.concatenate(flat + [jnp.zeros((padded - tot,), F32)]).reshape(-1, LW)
    summed = sum_slots("tiny_sum", allgather8("ag_tiny_grads", tiny)).reshape(-1)
    offs = {}
    o = 0
    for k, n_ in zip(order + ['c_ctx'], sizes):
        offs[k] = summed[o:o + n_]
        o += n_
    shard = lambda k, rows: lax.dynamic_slice_in_dim(offs[k].reshape(rows, D), slot * Ds, Ds, axis=1)
    grads['c_ctx'] = offs['c_ctx']
    grads['q_norm_g'] = offs['q_norm_g'].reshape(1, HEAD_DIM)
    grads['k_norm_g'] = offs['k_norm_g'].reshape(1, HEAD_DIM)
    grads['conv_b'] = offs['conv_b'].reshape(1, D)
    grads['final_norm_g'] = offs['final_norm_g']
    grads['norm_g'] = shard('norm_g', 3)[None]
    grads['conv_w'] = shard('conv_w', 4)[None]
    grads['lru_ba'] = shard('lru_ba', 2)[None]
    grads['lru_bx'] = shard('lru_bx', 2)[None]
    grads['lru_lambda'] = shard('lru_lambda', 2)[None]

    landed = comm_call("rs_tail_p2", rs_p2_comm([late[0]], [late[1]]))
    g_wd = sum_slots_into("rs_sum_wd0", landed[0], where, reduced['ffn_wd'], (2, F4, D), 0)
    g_wg, g_wu, g_wd = comm_call("rs_tail_p3", rs_p3_comm(
        [reduced['ffn_wg'], reduced['ffn_wu'], g_wd], [(0, 0, D // 2), (1, 0, D // 2), (2, 0, F4 // 2)]))
    grads.update(ffn_wg=g_wg[None], ffn_wu=g_wu[None], ffn_wd=g_wd[None], w_in=reduced['w_in'][None],
                 w_out=reduced['w_out'][None], lru_wa=reduced['lru_wa'].reshape(lru_wa.shape),
                 lru_wx=reduced['lru_wx'].reshape(lru_wx.shape))

    delta, new_m, new_v = {}, {}, {}
    big_names = ['w_mod', 'w_in', 'w_out', 'lru_wa', 'lru_wx', 'ffn_wg', 'ffn_wu', 'ffn_wd']
    for k in big_names:
        res = adamw("adamw_" + k, given[k], grads[k], given['m_' + k], given['v_' + k], copy_grad=(k != 'w_mod'))
        delta[k], new_m[k], new_v[k] = res[:3]
        if k != 'w_mod':
            grads[k] = res[3]
    tiny_names = [k for k in names if k not in big_names]
    res = adamw_many("adamw_tiny", [(given[k], grads[k], given['m_' + k], given['v_' + k]) for k in tiny_names])
    for k, (d_, m_, v_) in zip(tiny_names, res):
        delta[k], new_m[k], new_v[k] = d_, m_, v_

    return (loss, grad_x[None], *[grads[k] for k in names], *[delta[k] for k in names],
            *[new_m[k] for k in names], *[new_v[k] for k in names])
```

```python
import functools

import jax
import jax.numpy as jnp
from jax import lax
from jax.experimental import pallas as pl
from jax.experimental.pallas import tpu as pltpu

F32 = jnp.float32
BF = jnp.bfloat16
EPS = 1e-6
HEAD_DIM = 128
GRID_W = 64
ROPE_THETA = 10000.0
LRU_C = 8.0
FFN_RES = 0.5
N_MOD = 9
LOG2_E = 1.4426950408889634
ADAM_LR, ADAM_B1, ADAM_B2, ADAM_EPS, ADAM_WD, ADAM_STEP = 0.001, 0.9, 0.999, 1e-08, 0.01, 10
VMEM_LIMIT = 52 * 1024 * 1024
MESH = pl.DeviceIdType.MESH
ANY = pl.BlockSpec(memory_space=pl.ANY)


def _sds(shape, dt):
    return jax.ShapeDtypeStruct(tuple(shape), dt)


def _pick(n, cands):
    for c in cands:
        if n % c == 0:
            return c
    return n


def _cparams(**kw):
    return pltpu.CompilerParams(vmem_limit_bytes=VMEM_LIMIT, **kw)


def _sig(x):
    return 1.0 / (1.0 + jnp.exp(-x))


def _gelu(x):
    t = jnp.tanh(0.7978845608028654 * (x + 0.044715 * x * x * x))
    return 0.5 * x * (1.0 + t), t


def _gelu_grad(x, t):
    return 0.5 * (1.0 + t) + 0.5 * x * (1.0 - t * t) * 0.7978845608028654 * (1.0 + 3.0 * 0.044715 * x * x)


class Comm:
    def __init__(self, reads, lands, n_sem, start, finish):
        self.reads, self.lands, self.n_sem, self.start, self.finish = list(reads), list(lands), n_sem, start, finish


def merge_comms(comms):
    reads = [r for c in comms for r in c.reads]
    lands = [l for c in comms for l in c.lands]

    def run(which):
        def fn(r, lo, send_sems, recv_sems, off=0):
            ro = lo_ = so = 0
            for c in comms:
                getattr(c, which)(r[ro:ro + len(c.reads)], lo[lo_:lo_ + len(c.lands)], send_sems, recv_sems, off + so)
                ro, lo_, so = ro + len(c.reads), lo_ + len(c.lands), so + c.n_sem
        return fn

    return Comm(reads, lands, sum(c.n_sem for c in comms), run('start'), run('finish'))


def hosted_call(body, *, name, grid, in_specs, out_specs, out_shape, args, scratch_shapes=(), aliases=None, comm=None):
    aliases = dict(aliases or {})
    if comm is None:
        return pl.pallas_call(
            body, name=name, grid=grid, in_specs=list(in_specs), out_specs=list(out_specs), out_shape=list(out_shape),
            scratch_shapes=list(scratch_shapes), input_output_aliases=aliases, compiler_params=_cparams())(*args)
    n_in, n_out, n_sc = len(args), len(out_shape), len(scratch_shapes)
    land_in = [(t, l) for t, l in enumerate(comm.lands) if not isinstance(l, jax.ShapeDtypeStruct)]
    nr, nli, nl = len(comm.reads), len(land_in), len(comm.lands)

    def wrapped(*refs):
        a = refs[:n_in]
        r = refs[n_in:n_in + nr]
        pos = n_in + nr + nli
        o = refs[pos:pos + n_out]
        lo = refs[pos + n_out:pos + n_out + nl]
        sc = refs[pos + n_out + nl:pos + n_out + nl + n_sc]
        send_sems, recv_sems = refs[pos + n_out + nl + n_sc:]
        ids = [pl.program_id(d) for d in range(len(grid))]
        first, last = ids[0] == 0, ids[0] == grid[0] - 1
        for d in range(1, len(grid)):
            first = first & (ids[d] == 0)
            last = last & (ids[d] == grid[d] - 1)

        @pl.when(first)
        def _():
            comm.start(r, lo, send_sems, recv_sems)

        body(*a, *o, *sc)

        @pl.when(last)
        def _():
            comm.finish(r, lo, send_sems, recv_sems)

    for q, (t, _) in enumerate(land_in):
        aliases[n_in + nr + q] = n_out + t
    res = pl.pallas_call(
        wrapped, name=name, grid=grid,
        in_specs=list(in_specs) + [ANY] * (nr + nli), out_specs=list(out_specs) + [ANY] * nl,
        out_shape=list(out_shape) + [l if isinstance(l, jax.ShapeDtypeStruct) else _sds(l.shape, l.dtype) for l in comm.lands],
        scratch_shapes=list(scratch_shapes) + [pltpu.SemaphoreType.DMA((comm.n_sem,)), pltpu.SemaphoreType.DMA((comm.n_sem,))],
        input_output_aliases=aliases, compiler_params=_cparams(),
    )(*args, *comm.reads, *[l for _, l in land_in])
    return list(res[:n_out]), list(res[n_out:])


def comm_call(name, comm):
    def body():
        pass

    return hosted_call(body, name=name, grid=(1,), in_specs=[], out_specs=[], out_shape=[], args=[], comm=comm)[1]


def ew_call(name, grid, fn, ins, outs, first=None, aliases=None, comm=None):
    n_in = len(ins)

    def body(*refs):
        ids = tuple(pl.program_id(a) for a in range(len(grid)))
        vals = fn(ids, *refs[:n_in])
        for (_, _, acc), o_ref, v in zip(outs, refs[n_in:], vals):
            if not acc:
                o_ref[...] = v.astype(o_ref.dtype)
            else:
                is_first = first(ids)

                @pl.when(is_first)
                def _(o_ref=o_ref, v=v):
                    o_ref[...] = v.astype(o_ref.dtype)

                @pl.when(jnp.logical_not(is_first))
                def _(o_ref=o_ref, v=v):
                    o_ref[...] += v.astype(o_ref.dtype)

    return hosted_call(body, name=name, grid=grid, in_specs=[s for _, s in ins], out_specs=[s for _, s, _ in outs],
                       out_shape=[o for o, _, _ in outs], args=[a for a, _ in ins], aliases=aliases, comm=comm)


def fused_mm(name, grid, ins, prods, acc_shapes, epi, outs, extras=(), pre=None, comm=None, row_split=1):
    n_in, n_ex, n_out = len(ins), len(extras), len(outs)
    nk = grid[-1]
    pre = pre or {}
    n_acc = len(acc_shapes)

    def body(*refs):
        in_refs = refs[:n_in]
        ex_refs = refs[n_in:n_in + n_ex]
        out_refs = refs[n_in + n_ex:n_in + n_ex + n_out]
        accs = refs[n_in + n_ex + n_out:]
        ids = tuple(pl.program_id(a) for a in range(len(grid)))
        k = ids[-1]
        loaded = {}

        def operand(i):
            if i not in loaded:
                v = in_refs[i][...]
                if i in pre:
                    v = pre[i](v)
                loaded[i] = v.astype(BF)
            return loaded[i]

        def product(ia, ib, dims):
            return lax.dot_general(operand(ia), operand(ib), (dims, ((), ())), preferred_element_type=F32)

        if nk == 1:
            rows = acc_shapes[0][0]
            step = rows // row_split
            parts = [slice(p * step, (p + 1) * step) for p in range(row_split)]
            all_sums = []
            for rs in parts:
                sums = [None] * n_acc
                for ia, ib, dims, ai in prods:
                    lhs = operand(ia) if row_split == 1 else operand(ia)[rs]
                    d = lax.dot_general(lhs, operand(ib), (dims, ((), ())), preferred_element_type=F32)
                    sums[ai] = d if sums[ai] is None else sums[ai] + d
                all_sums.append(sums)
            for rs, sums in zip(parts, all_sums):
                ex = ex_refs if row_split == 1 else [e.at[rs] if e.shape[0] == rows else e for e in ex_refs]
                for o_ref, v in zip(out_refs, epi(ids, sums, ex)):
                    if row_split == 1:
                        o_ref[...] = v.astype(o_ref.dtype)
                    else:
                        o_ref[rs] = v.astype(o_ref.dtype)
            return

        @pl.when(k == 0)
        def _():
            for a in accs:
                a[...] = jnp.zeros(a.shape, F32)

        for ia, ib, dims, ai in prods:
            accs[ai][...] += product(ia, ib, dims)

        @pl.when(k == nk - 1)
        def _():
            vals = epi(ids, [a[...] for a in accs], ex_refs)
            for o_ref, v in zip(out_refs, vals):
                o_ref[...] = v.astype(o_ref.dtype)

    return hosted_call(
        body, name=name, grid=grid, in_specs=[s for _, s in ins] + [s for _, s in extras],
        out_specs=[s for _, s in outs], out_shape=[o for o, _ in outs],
        scratch_shapes=[pltpu.VMEM(s, F32) for s in acc_shapes] if nk > 1 else [],
        args=[a for a, _ in ins] + [a for a, _ in extras], comm=comm)


NN = ((1,), (0,))
NT = ((1,), (1,))
TN = ((0,), (0,))


class Dims:
    def __init__(self, S, C, D, F4, W4, NS, LB):
        self.S, self.C, self.D, self.F4, self.W4, self.NS, self.LB = S, C, D, F4, W4, NS, LB
        self.T = S + C
        self.DFF = F4 * NS
        self.INW = W4 * NS
        self.NQ = D // HEAD_DIM
        self.KVW = (self.INW - 5 * D) // 2
        self.NKV = self.KVW // HEAD_DIM
        self.G = self.NQ // self.NKV
        self.OFF_K = D
        self.OFF_V = D + self.KVW
        self.OFF_LX = D + 2 * self.KVW
        self.OFF_LG = self.OFF_LX + D
        self.OFF_GA = self.OFF_LG + D
        self.OFF_GL = self.OFF_GA + D
        self.bm = _pick(C, [256, 128, 64, 32, 16, 8])
        self.nCb = C // self.bm
        self.nTb = self.T // self.bm
        self.nSb = S // self.bm
        self.mT = _pick(self.T, [544, 512, 384, 256, 128])
        self.mS = _pick(S, [512, 256, 128])
        self.kT = _pick(self.T, [1088, 1024, 768, 544, 512, 384, 256, 128])
        self.kS = _pick(S, [1024, 512, 256, 128])
        self.cw = _pick(D, [1024, 512, 256, 128]) if (self.OFF_LX % 1024 == 0 and D % 1024 == 0) else _pick(
            self.OFF_LX, [512, 256, 128])
        self.nsub = 2 if (W4 % 256 == 0 and W4 >= 512) else 1
        self.wb = W4 // self.nsub
        self.LBD = D // LB
        self.bq = _pick(C, [256, 128]) if S % _pick(C, [256, 128]) == 0 else 128


def rope_tables(dm):
    rows = dm.S // GRID_W
    row = jnp.repeat(jnp.arange(rows, dtype=F32), GRID_W)
    col = jnp.tile(jnp.arange(GRID_W, dtype=F32), rows)
    axis_dims = HEAD_DIM // 2
    freqs = ROPE_THETA ** (-jnp.arange(0, axis_dims, 2, dtype=F32) / axis_dims)
    ang = jnp.concatenate([row[:, None] * freqs, col[:, None] * freqs], axis=-1)
    cos = jnp.repeat(jnp.cos(ang), 2, axis=-1)
    sin = jnp.repeat(jnp.sin(ang), 2, axis=-1)
    sign = jnp.tile(jnp.array([-1.0, 1.0], F32), HEAD_DIM // 2)
    sin = sin * sign
    cos = jnp.concatenate([jnp.ones((dm.C, HEAD_DIM), F32), cos], axis=0)
    sin = jnp.concatenate([jnp.zeros((dm.C, HEAD_DIM), F32), sin], axis=0)
    return cos, sin


def _pair_swap(y):
    lane = lax.broadcasted_iota(jnp.int32, y.shape, 1)
    nxt = pltpu.roll(y, y.shape[1] - 1, 1)
    prv = pltpu.roll(y, 1, 1)
    return jnp.where((lane & 1) == 0, nxt, prv)


def normmod_fwd(name, dm, x, norm_g3, stage, modv, rows_T):
    D, bm = dm.D, dm.bm
    nb = dm.nTb if rows_T else dm.nSb
    typ = (lambda i: jnp.where(i < dm.nCb, 0, 1)) if rows_T else (lambda i: 1)

    def fn(ids, x_ref, g_ref, sh_ref, sc_ref):
        xv = x_ref[...]
        r = lax.rsqrt(jnp.mean(xv * xv, axis=-1, keepdims=True) + EPS)
        n = xv * r * g_ref[...]
        return [n * (1.0 + sc_ref[...]) + sh_ref[...]]

    return ew_call(
        name, (nb,), fn,
        [(x, pl.BlockSpec((bm, D), lambda i: (i, 0))),
         (norm_g3, pl.BlockSpec((None, 1, D), lambda i: (stage, 0, 0))),
         (modv, pl.BlockSpec((None, None, 1, D), lambda i: (typ(i), 3 * stage, 0, 0))),
         (modv, pl.BlockSpec((None, None, 1, D), lambda i: (typ(i), 3 * stage + 1, 0, 0)))],
        [(_sds(x.shape, BF), pl.BlockSpec((bm, D), lambda i: (i, 0)), False)])[0]


def normmod_concat_fwd(name, dm, ctx, x, norm_g3, modv):
    D, bm, nCb = dm.D, dm.bm, dm.nCb
    typ = lambda i: jnp.where(i < nCb, 0, 1)

    def fn(ids, c_ref, x_ref, g_ref, sh_ref, sc_ref):
        xv = jnp.where(ids[0] < nCb, c_ref[...], x_ref[...])
        r = lax.rsqrt(jnp.mean(xv * xv, axis=-1, keepdims=True) + EPS)
        n = xv * r * g_ref[...]
        return [n * (1.0 + sc_ref[...]) + sh_ref[...], xv]

    row = pl.BlockSpec((bm, D), lambda i: (i, 0))
    return ew_call(
        name, (dm.nTb,), fn,
        [(ctx, pl.BlockSpec((bm, D), lambda i: (jnp.minimum(i, nCb - 1), 0))),
         (x, pl.BlockSpec((bm, D), lambda i: (jnp.maximum(i - nCb, 0), 0))),
         (norm_g3, pl.BlockSpec((None, 1, D), lambda i: (0, 0, 0))),
         (modv, pl.BlockSpec((None, None, 1, D), lambda i: (typ(i), 0, 0, 0))),
         (modv, pl.BlockSpec((None, None, 1, D), lambda i: (typ(i), 1, 0, 0)))],
        [(_sds((dm.T, D), BF), row, False), (_sds((dm.T, D), F32), row, False)])


def normmod_bwd(name, dm, dh, x, dres, norm_g3, stage, modv, rows_T, dres_lat_only, out_lat_only=False):
    D, bm = dm.D, dm.bm
    nb = dm.nTb if rows_T else dm.nSb
    nCb = dm.nCb
    typ = (lambda i: jnp.where(i < nCb, 0, 1)) if rows_T else (lambda i: 1)
    if dres_lat_only:
        dres_map = lambda i: (jnp.maximum(i - nCb, 0), 0)
    else:
        dres_map = lambda i: (i, 0)

    def fn(ids, dh_ref, x_ref, dres_ref, g_ref, sc_ref):
        i = ids[0]
        xv = x_ref[...]
        dhv = dh_ref[...].astype(F32)
        r = lax.rsqrt(jnp.mean(xv * xv, axis=-1, keepdims=True) + EPS)
        xn = xv * r
        g = g_ref[...]
        n = xn * g
        dn = dhv * (1.0 + sc_ref[...])
        dxn = dn * g
        dx = r * (dxn - xn * jnp.mean(dxn * xn, axis=-1, keepdims=True))
        dresv = dres_ref[...]
        if dres_lat_only:
            dresv = jnp.where(i >= nCb, dresv, 0.0)
        dsh = jnp.sum(dhv, axis=0, keepdims=True)
        dsc = jnp.sum(dhv * n, axis=0, keepdims=True)
        dg = jnp.sum(dn * xn, axis=0, keepdims=True)
        return [dx + dresv, dsh, dsc, dg]

    if rows_T:
        first = lambda ids: (ids[0] == 0) | (ids[0] == nCb)
    else:
        first = lambda ids: ids[0] == 0
    acc = (_sds((2, 1, D), F32), pl.BlockSpec((None, 1, D), lambda i: (typ(i), 0, 0)), True)
    return ew_call(
        name, (nb,), fn,
        [(dh, pl.BlockSpec((bm, D), lambda i: (i, 0))),
         (x, pl.BlockSpec((bm, D), lambda i: (i, 0))),
         (dres, pl.BlockSpec((bm, D), dres_map)),
         (norm_g3, pl.BlockSpec((None, 1, D), lambda i: (stage, 0, 0))),
         (modv, pl.BlockSpec((None, None, 1, D), lambda i: (typ(i), 3 * stage + 1, 0, 0)))],
        [(_sds((dm.S, D) if out_lat_only else x.shape, F32),
          pl.BlockSpec((bm, D), (lambda i: (jnp.maximum(i - nCb, 0), 0)) if out_lat_only else (lambda i: (i, 0))), False),
         acc, acc, acc], first=first)


def gate_bwd(name, dm, dx, f, modv, gidx, scale, rows_T):
    D, bm = dm.D, dm.bm
    nb = dm.nTb if rows_T else dm.nSb
    nCb = dm.nCb
    typ = (lambda i: jnp.where(i < nCb, 0, 1)) if rows_T else (lambda i: 1)

    def fn(ids, dx_ref, f_ref, g_ref):
        dxv = dx_ref[...]
        return [scale * g_ref[...] * dxv, jnp.sum(scale * f_ref[...].astype(F32) * dxv, axis=0, keepdims=True)]

    if rows_T:
        first = lambda ids: (ids[0] == 0) | (ids[0] == nCb)
    else:
        first = lambda ids: ids[0] == 0
    return ew_call(
        name, (nb,), fn,
        [(dx, pl.BlockSpec((bm, D), lambda i: (i, 0))),
         (f, pl.BlockSpec((bm, D), lambda i: (i, 0))),
         (modv, pl.BlockSpec((None, None, 1, D), lambda i: (typ(i), gidx, 0, 0)))],
        [(_sds(dx.shape, BF), pl.BlockSpec((bm, D), lambda i: (i, 0)), False),
         (_sds((2, 1, D), F32), pl.BlockSpec((None, 1, D), lambda i: (typ(i), 0, 0)), True)], first=first)


def ffn_fwd(name, dm, h, xres, wg, wu, wd, modv, gidx, rows_T, comm_up=None, comm_down=None):
    D, F4, NS = dm.D, dm.F4, dm.NS
    M = h.shape[0]
    bm = dm.mT if rows_T else dm.mS
    C = dm.C

    def epi_up(ids, accs, ex):
        a, u = accs
        return [a, u, a * _sig(a) * u]

    hspec = pl.BlockSpec((bm, D), lambda j, i, k: (i, 0))
    wspec = pl.BlockSpec((None, D, F4), lambda j, i, k: (j, 0, 0))
    ospec = pl.BlockSpec((bm, F4), lambda j, i, k: (i, j))
    res = fused_mm(
        name + "_up", (NS, M // bm, 1), [(h, hspec), (wg, wspec), (wu, wspec)],
        [(0, 1, NN, 0), (0, 2, NN, 1)], [(bm, F4), (bm, F4)], epi_up,
        [(_sds((M, dm.DFF), BF), ospec)] * 3, comm=comm_up)
    (a, u, s), land_up = res if comm_up is not None else (res, None)
    if wd is None:
        wd = land_up[0]

    bn = _pick(D, [1024, 512, 256, 128])

    def epi_dn(ids, accs, ex):
        f = accs[0]
        if rows_T:
            row = ids[0] * bm + lax.broadcasted_iota(jnp.int32, (bm, 1), 0)
            gate = jnp.where(row < C, ex[1][...], ex[2][...])
        else:
            gate = ex[2][...]
        return [ex[0][...] + FFN_RES * gate * f, f]

    gspec = lambda t: pl.BlockSpec((None, None, 1, bn), lambda i, j, k: (t, gidx, 0, j))
    res = fused_mm(
        name + "_down", (M // bm, D // bn, NS // 2),
        [(s, pl.BlockSpec((bm, F4), lambda i, j, k: (i, 2 * k))),
         (wd, pl.BlockSpec((None, F4, bn), lambda i, j, k: (2 * k, 0, j))),
         (s, pl.BlockSpec((bm, F4), lambda i, j, k: (i, 2 * k + 1))),
         (wd, pl.BlockSpec((None, F4, bn), lambda i, j, k: (2 * k + 1, 0, j)))],
        [(0, 1, NN, 0), (2, 3, NN, 0)], [(bm, bn)], epi_dn,
        [(_sds((M, D), F32), pl.BlockSpec((bm, bn), lambda i, j, k: (i, j))),
         (_sds((M, D), BF), pl.BlockSpec((bm, bn), lambda i, j, k: (i, j)))],
        extras=[(xres, pl.BlockSpec((bm, bn), lambda i, j, k: (i, j))), (modv, gspec(0)), (modv, gspec(1))],
        comm=comm_down)
    (xo, f), land_down = res if comm_down is not None else (res, None)
    return xo, a, u, s, f, land_up, land_down


def ffn_bwd(name, dm, df, h, a, u, s, wg, wu, wd, rows_T, comms=None):
    comms = comms or {}
    landed, made = {}, {}

    def run(key, *args, **kw):
        comm = comms[key](landed, made) if key in comms else None
        res = fused_mm(*args, comm=comm, **kw)
        if comm is not None:
            res, landed[key] = res
        return res

    D, F4, NS = dm.D, dm.F4, dm.NS
    M = h.shape[0]
    bm = dm.mT if rows_T else dm.mS
    bkr = dm.kT if rows_T else dm.kS

    def epi_ds(ids, accs, ex):
        ds = accs[0]
        av = ex[0][...].astype(F32)
        uv = ex[1][...].astype(F32)
        sg = _sig(av)
        return [ds * uv * (sg * (1.0 + av * (1.0 - sg))), ds * av * sg]

    ospec = pl.BlockSpec((bm, F4), lambda j, i, k: (i, j))
    da, du = run(
        'ds', name + "_ds", (NS, M // bm, 1),
        [(df, pl.BlockSpec((bm, D), lambda j, i, k: (i, 0))),
         (wd, pl.BlockSpec((None, F4, D), lambda j, i, k: (j, 0, 0)))],
        [(0, 1, NT, 0)], [(bm, F4)], epi_ds, [(_sds((M, dm.DFF), BF), ospec)] * 2,
        extras=[(a, ospec), (u, ospec)], row_split=2)

    ident = lambda ids, accs, ex: list(accs)
    bn = _pick(D, [1024, 512, 256, 128])
    dwg, dwu = run(
        'dwgu', name + "_dwgu", (D // bn, NS, M // bkr),
        [(h, pl.BlockSpec((bkr, bn), lambda i, j, k: (k, i))),
         (da, pl.BlockSpec((bkr, F4), lambda i, j, k: (k, j))),
         (du, pl.BlockSpec((bkr, F4), lambda i, j, k: (k, j)))],
        [(0, 1, TN, 0), (0, 2, TN, 1)], [(bn, F4), (bn, F4)], ident,
        [(_sds((NS, D, F4), BF), pl.BlockSpec((None, bn, F4), lambda i, j, k: (j, i, 0)))] * 2)
    made['dwg'], made['dwu'] = dwg, dwu

    bk2 = 2 * bkr if M % (2 * bkr) == 0 else bkr
    dwd = run(
        'dwd', name + "_dwd", (NS, D // bn, M // bk2),
        [(s, pl.BlockSpec((bk2, F4), lambda i, j, k: (k, i))),
         (df, pl.BlockSpec((bk2, bn), lambda i, j, k: (k, j)))],
        [(0, 1, TN, 0)], [(F4, bn)], ident,
        [(_sds((NS, F4, D), BF), pl.BlockSpec((None, F4, bn), lambda i, j, k: (i, 0, j)))])[0]
    made['dwd'] = dwd

    dh = run(
        'dh', name + "_dh", (M // bm, D // bn, NS),
        [(da, pl.BlockSpec((bm, F4), lambda i, j, k: (i, k))),
         (wg, pl.BlockSpec((None, bn, F4), lambda i, j, k: (k, j, 0))),
         (du, pl.BlockSpec((bm, F4), lambda i, j, k: (i, k))),
         (wu, pl.BlockSpec((None, bn, F4), lambda i, j, k: (k, j, 0)))],
        [(0, 1, NT, 0), (2, 3, NT, 0)], [(bm, bn)], ident,
        [(_sds((M, D), F32), pl.BlockSpec((bm, bn), lambda i, j, k: (i, j)))])[0]
    return dh, dwg, dwu, dwd, landed


def qk_prep(dm, P, gq, gk, cosf, sinf):
    D, KVW, bm = dm.D, dm.KVW, dm.bm

    def head_norm_rope(xh, g, c, s):
        r = lax.rsqrt(jnp.mean(xh * xh, axis=-1, keepdims=True) + EPS)
        y = xh * r * g
        return y * c + _pair_swap(y) * s

    def fn(ids, q_ref, k_ref, v_ref, gq_ref, gk_ref, c_ref, s_ref):
        c, s = c_ref[...], s_ref[...]
        qs = [head_norm_rope(q_ref[:, h * HEAD_DIM:(h + 1) * HEAD_DIM], gq_ref[...], c, s) for h in range(dm.NQ)]
        ks = [head_norm_rope(k_ref[:, h * HEAD_DIM:(h + 1) * HEAD_DIM], gk_ref[...], c, s) for h in range(dm.NKV)]
        return [jnp.concatenate(qs, axis=1), jnp.concatenate(ks, axis=1), v_ref[...]]

    hspec = pl.BlockSpec((bm, HEAD_DIM), lambda i: (i, 0))
    vec = pl.BlockSpec((1, HEAD_DIM), lambda i: (0, 0))
    return ew_call(
        "qk_prep", (dm.nTb,), fn,
        [(P, pl.BlockSpec((bm, D), lambda i: (i, 0))),
         (P, pl.BlockSpec((bm, KVW), lambda i: (i, dm.OFF_K // KVW))),
         (P, pl.BlockSpec((bm, KVW), lambda i: (i, dm.OFF_V // KVW))),
         (gq, vec), (gk, vec), (cosf, hspec), (sinf, hspec)],
        [(_sds((dm.T, D), BF), pl.BlockSpec((bm, D), lambda i: (i, 0)), False),
         (_sds((dm.T, KVW), BF), pl.BlockSpec((bm, KVW), lambda i: (i, 0)), False),
         (_sds((dm.T, KVW), BF), pl.BlockSpec((bm, KVW), lambda i: (i, 0)), False)])


def qk_prep_bwd(dm, dq, dk, dv, P, gq, gk, cosf, sinf, dP):
    D, KVW, bm, nCb = dm.D, dm.KVW, dm.bm, dm.nCb
    W = D + 2 * KVW

    def head_bwd(d, xh, g, c, s):
        dy = d * c - _pair_swap(d) * s
        r = lax.rsqrt(jnp.mean(xh * xh, axis=-1, keepdims=True) + EPS)
        xn = xh * r
        dg = jnp.sum(dy * xn, axis=0, keepdims=True)
        dxn = dy * g
        return r * (dxn - xn * jnp.mean(dxn * xn, axis=-1, keepdims=True)), dg

    def fn(ids, dq_ref, dk_ref, dv_ref, q_ref, k_ref, gq_ref, gk_ref, c_ref, s_ref, dp_any):
        i = ids[0]
        c, s = c_ref[...], s_ref[...]
        lat = i >= nCb
        outs, dgq = [], jnp.zeros((1, HEAD_DIM), F32)
        for h in range(dm.NQ):
            sl = slice(h * HEAD_DIM, (h + 1) * HEAD_DIM)
            d = jnp.where(lat, dq_ref[:, sl], 0.0)
            dx, dg = head_bwd(d, q_ref[:, sl], gq_ref[...], c, s)
            outs.append(dx)
            dgq = dgq + dg
        dgk = jnp.zeros((1, HEAD_DIM), F32)
        for h in range(dm.NKV):
            sl = slice(h * HEAD_DIM, (h + 1) * HEAD_DIM)
            dx, dg = head_bwd(dk_ref[:, sl], k_ref[:, sl], gk_ref[...], c, s)
            outs.append(dx)
            dgk = dgk + dg
        outs.append(dv_ref[...])
        return [jnp.concatenate(outs, axis=1), dgq, dgk]

    hspec = pl.BlockSpec((bm, HEAD_DIM), lambda i: (i, 0))
    vec = pl.BlockSpec((1, HEAD_DIM), lambda i: (0, 0))
    return ew_call(
        "qk_prep_bwd", (dm.nTb,), fn,
        [(dq, pl.BlockSpec((bm, D), lambda i: (jnp.maximum(i - nCb, 0), 0))),
         (dk, pl.BlockSpec((bm, KVW), lambda i: (i, 0))),
         (dv, pl.BlockSpec((bm, KVW), lambda i: (i, 0))),
         (P, pl.BlockSpec((bm, D), lambda i: (i, 0))),
         (P, pl.BlockSpec((bm, KVW), lambda i: (i, dm.OFF_K // KVW))),
         (gq, vec), (gk, vec), (cosf, hspec), (sinf, hspec), (dP, ANY)],
        [(_sds(dP.shape, BF), pl.BlockSpec((bm, W), lambda i: (i, 0)), False),
         (_sds((1, HEAD_DIM), F32), vec, True), (_sds((1, HEAD_DIM), F32), vec, True)],
        first=lambda ids: ids[0] == 0, aliases={9: 0})


def _softmax_numerators(s_ref, eb_ref, mb_ref, scale):
    rows, T = s_ref.shape
    m = jnp.max(s_ref[...], axis=-1, keepdims=True)
    mb_ref[...] = jnp.broadcast_to(m, (rows, HEAD_DIM))
    lacc = jnp.zeros((rows, HEAD_DIM), F32)
    for c in range(T // HEAD_DIM):
        cs = slice(c * HEAD_DIM, (c + 1) * HEAD_DIM)
        e = jnp.exp2((s_ref[:, cs] - mb_ref[...]) * (scale * LOG2_E))
        lacc = lacc + e
        eb_ref[:, cs] = e.astype(BF)
    return jnp.sum(lacc, axis=-1, keepdims=True)


def attention_fwd(dm, qr, kr, vb, comm=None):
    S, T, D, G, nCb = dm.S, dm.T, dm.D, dm.G, dm.nCb
    bq = dm.bq
    off = dm.C // bq
    scale = HEAD_DIM ** -0.5
    GW = G * HEAD_DIM

    def body(q_ref, k_ref, v_ref, o_ref):
        k = k_ref[...]
        v = v_ref[...]
        head = lambda h: slice(h * HEAD_DIM, (h + 1) * HEAD_DIM)
        scores = lambda h: lax.dot_general(q_ref[:, head(h)], k, (NT, ((), ())), preferred_element_type=F32)
        s_next = scores(0)
        for h in range(G):
            s = s_next
            if h + 1 < G:
                s_next = scores(h + 1)
            m = jnp.max(s, axis=-1, keepdims=True)
            p = jnp.exp2((s - m) * (scale * LOG2_E))
            l = jnp.sum(p, axis=-1, keepdims=True)
            o = lax.dot_general(p.astype(BF), v, (NN, ((), ())), preferred_element_type=F32)
            o_ref[:, head(h)] = o / l

    return hosted_call(
        body, grid=(dm.NKV, S // bq), name="attn_fwd",
        in_specs=[pl.BlockSpec((bq, GW), lambda g, i: (i + off, g)),
                  pl.BlockSpec((T, HEAD_DIM), lambda g, i: (0, g)),
                  pl.BlockSpec((T, HEAD_DIM), lambda g, i: (0, g))],
        out_specs=[pl.BlockSpec((bq, GW), lambda g, i: (i, g))],
        out_shape=[_sds((S, D), F32)], args=[qr, kr, vb], comm=comm)


def attention_bwd(dm, qr, kr, vb, attn, dattn, comm=None):
    S, T, D, G = dm.S, dm.T, dm.D, dm.G
    bq = dm.bq
    off = dm.C // bq
    scale = HEAD_DIM ** -0.5
    GW = G * HEAD_DIM

    def body(q_ref, k_ref, v_ref, o_ref, do_ref, dq_ref, dk_ref, dv_ref, s2_ref, dp_ref, eb_ref, tb_ref, mb_ref):
        i = pl.program_id(1)

        @pl.when(i == 0)
        def _():
            dk_ref[...] = jnp.zeros(dk_ref.shape, F32)
            dv_ref[...] = jnp.zeros(dv_ref.shape, F32)

        k = k_ref[...]
        v = v_ref[...]
        head = lambda h: slice(h * HEAD_DIM, (h + 1) * HEAD_DIM)

        def finish(h, w):
            dq_ref[:, head(h)] = lax.dot_general(tb_ref[...], k, (NN, ((), ())), preferred_element_type=F32) * w
            dk_ref[...] += lax.dot_general(tb_ref[...], (q_ref[:, head(h)].astype(F32) * w).astype(BF), (TN, ((), ())),
                                           preferred_element_type=F32)

        s2_ref[0] = lax.dot_general(q_ref[:, head(0)], k, (NT, ((), ())), preferred_element_type=F32)
        w_prev = None
        for h in range(G):
            s_ref = s2_ref.at[h % 2]
            do = do_ref[:, head(h)]
            dof = do.astype(F32)
            if h + 1 < G:
                s2_ref[(h + 1) % 2] = lax.dot_general(q_ref[:, head(h + 1)], k, (NT, ((), ())),
                                                      preferred_element_type=F32)
            if h > 0:
                finish(h - 1, w_prev)
            l = _softmax_numerators(s_ref, eb_ref, mb_ref, scale)
            rl = 1.0 / l
            dp_ref[...] = lax.dot_general(do, v, (NT, ((), ())), preferred_element_type=F32)
            dv_ref[...] += lax.dot_general(eb_ref[...], (dof * rl).astype(BF), (TN, ((), ())), preferred_element_type=F32)
            delta = jnp.sum(dof * o_ref[:, head(h)], axis=-1, keepdims=True)
            mb_ref[...] = jnp.broadcast_to(delta, (bq, HEAD_DIM))
            for c in range(T // HEAD_DIM):
                cs = slice(c * HEAD_DIM, (c + 1) * HEAD_DIM)
                tb_ref[:, cs] = (eb_ref[:, cs].astype(F32) * (dp_ref[:, cs] - mb_ref[...])).astype(BF)
            w_prev = scale * rl
        finish(G - 1, w_prev)

    return hosted_call(
        body, grid=(dm.NKV, S // bq), name="attn_bwd",
        in_specs=[pl.BlockSpec((bq, GW), lambda g, i: (i + off, g)),
                  pl.BlockSpec((T, HEAD_DIM), lambda g, i: (0, g)),
                  pl.BlockSpec((T, HEAD_DIM), lambda g, i: (0, g)),
                  pl.BlockSpec((bq, GW), lambda g, i: (i, g)),
                  pl.BlockSpec((bq, GW), lambda g, i: (i + off, g))],
        out_specs=[pl.BlockSpec((bq, GW), lambda g, i: (i, g)),
                   pl.BlockSpec((T, HEAD_DIM), lambda g, i: (0, g)),
                   pl.BlockSpec((T, HEAD_DIM), lambda g, i: (0, g))],
        out_shape=[_sds((S, D), F32), _sds((T, dm.KVW), F32), _sds((T, dm.KVW), F32)],
        args=[qr, kr, vb, attn, dattn],
        scratch_shapes=[pltpu.VMEM((2, bq, T), F32), pltpu.VMEM((bq, T), F32), pltpu.VMEM((bq, T), BF),
                        pltpu.VMEM((bq, T), BF), pltpu.VMEM((bq, HEAD_DIM), F32)], comm=comm)


def _conv_taps(dm, lx, masks_only=False):
    T, C = dm.T, dm.C
    t = lax.broadcasted_iota(jnp.int32, (T, 1), 0)
    valid = [(t >= 2) & ((t < C) | (t >= C + 2)), (t >= 1) & ((t < C) | (t >= C + 1)), None,
             (t != C - 1) & (t != T - 1)]
    shifts = [2, 1, 0, T - 1]
    taps = []
    for k in range(4):
        if k == 2:
            taps.append(lx)
        else:
            taps.append(jnp.where(valid[k], pltpu.roll(lx, shifts[k], 0), 0.0))
    return taps


def _scan_tiles(dm, chains):
    T, C = dm.T, dm.C
    nT, nC = T // 8, C // 8
    row = lax.broadcasted_iota(jnp.int32, (8, HEAD_DIM), 0)

    def tile_of(i, asc, split):
        if not split:
            return i if asc else nT - 1 - i
        if asc:
            return jnp.where(i < nT - nC, nC + i, i - (nT - nC))
        return jnp.where(i < nC, nC - 1 - i, nT - 1 - (i - nC))

    def step(i, carry, asc, split, a_ref, u_ref, out_ref, mode):
        off = pl.multiple_of(tile_of(i, asc, split) * 8, 8)
        a = a_ref[pl.ds(off, 8), :]
        b = u_ref[pl.ds(off, 8), :]
        if mode == 'lam':
            if asc:
                coef = jnp.where(row == 0, 1.0, pltpu.roll(a, 1, 0))
            else:
                coef = jnp.where(row == 7, 1.0, pltpu.roll(a, 7, 0))
        else:
            coef = a
        A, B = coef, b
        for d in (1, 2, 4):
            if asc:
                ok = row >= d
                A_sh = jnp.where(ok, pltpu.roll(A, d, 0), 1.0)
                B_sh = jnp.where(ok, pltpu.roll(B, d, 0), 0.0)
            else:
                ok = row < 8 - d
                A_sh = jnp.where(ok, pltpu.roll(A, 8 - d, 0), 1.0)
                B_sh = jnp.where(ok, pltpu.roll(B, 8 - d, 0), 0.0)
            B = B + A * B_sh
            A = A * A_sh
        h = A * carry + B
        out_ref[pl.ds(off, 8), :] = h
        last = h[7:8, :] if asc else h[0:1, :]
        if mode == 'lam':
            last = last * (a[7:8, :] if asc else a[0:1, :])
        return jnp.broadcast_to(last, (8, HEAD_DIM))

    def body(i, carries):
        return tuple(step(i, c_, *ch) for c_, ch in zip(carries, chains))

    lax.fori_loop(0, nT, body, tuple(jnp.zeros((8, HEAD_DIM), F32) for _ in chains))


def _lru_gates(xc, wa, ba, wx, bx, sp):
    xb = xc.astype(BF)
    r = _sig(jnp.dot(xb, wa, preferred_element_type=F32) + ba)
    i = _sig(jnp.dot(xb, wx, preferred_element_type=F32) + bx)
    a = jnp.exp(-LRU_C * r * sp)
    m = jnp.sqrt(1.0 - a * a)
    return r, i, a, m


def lru_fwd(dm, P, conv_w, conv_b, wa, ba, wx, bx, sp):
    T, D, LB = dm.T, dm.D, dm.LB
    W = dm.LBD
    R = _pick(T, [272, 256, 128, 64, 8])
    lxb = dm.OFF_LX // W

    def body(lx_ref, cw_ref, cb_ref, wa_ref, ba_ref, wx_ref, bx_ref, sp_ref, hf_ref, hb_ref, xc_ref, a_ref):
        taps = _conv_taps(dm, lx_ref[...])
        xc = cb_ref[...]
        for k in range(4):
            xc = xc + taps[k] * cw_ref[k:k + 1, :]
        xc_ref[...] = xc
        h_refs = (hf_ref, hb_ref)

        def chunk(ci, _):
            off = pl.multiple_of(ci * R, 8)
            x = xc_ref[pl.ds(off, R), :]
            for d in range(2):
                r, i, a, m = _lru_gates(x, wa_ref[d].astype(BF), ba_ref[d], wx_ref[d].astype(BF), bx_ref[d], sp_ref[d])
                a_ref[d, pl.ds(off, R), :] = a
                h_refs[d][pl.ds(off, R), :] = m * i * x
            return 0

        lax.fori_loop(0, T // R, chunk, 0)
        _scan_tiles(dm, [(True, False, a_ref.at[0], hf_ref, hf_ref, 'h'), (False, True, a_ref.at[1], hb_ref, hb_ref, 'h')])

    strip = lambda j: (0, j)
    vec = pl.BlockSpec((2, 1, W), lambda j: (0, 0, j))
    mat = pl.BlockSpec((2, None, W, W), lambda j: (0, j, 0, 0))
    return pl.pallas_call(
        body, grid=(LB,), name="lru_fwd",
        in_specs=[pl.BlockSpec((T, W), lambda j: (0, lxb + j)),
                  pl.BlockSpec((4, W), strip), pl.BlockSpec((1, W), strip), mat, vec, mat, vec, vec],
        out_specs=[pl.BlockSpec((T, W), strip)] * 2, out_shape=[_sds((T, D), F32)] * 2,
        scratch_shapes=[pltpu.VMEM((T, W), F32), pltpu.VMEM((2, T, W), F32)], compiler_params=_cparams(),
    )(P, conv_w, conv_b, wa, ba, wx, bx, sp)


def lru_bwd(dm, P, dh, hf, hb, conv_w, conv_b, wa, ba, wx, bx, sp, sg, dP):
    T, C, D, LB = dm.T, dm.C, dm.D, dm.LB
    W = dm.LBD
    R = _pick(T, [272, 256, 128, 64, 8])
    lxb = dm.OFF_LX // W

    def body(lx_ref, dh_ref, hf_ref, hb_ref, cw_ref, cb_ref, wa_ref, ba_ref, wx_ref, bx_ref, sp_ref, sg_ref, _dp_any,
             dlx_ref, dwa_ref, dba_ref, dwx_ref, dbx_ref, dlam_ref, dcw_ref, dcb_ref,
             xc_ref, a_ref, lam_ref, hp_ref, dxc_ref):
        lx = lx_ref[...]
        taps = _conv_taps(dm, lx)
        xc = cb_ref[...]
        for k in range(4):
            xc = xc + taps[k] * cw_ref[k:k + 1, :]
        xc_ref[...] = xc

        def gates(d, x):
            return _lru_gates(x, wa_ref[d].astype(BF), ba_ref[d], wx_ref[d].astype(BF), bx_ref[d], sp_ref[d])

        def chunk_a(ci, _):
            off = pl.multiple_of(ci * R, 8)
            x = xc_ref[pl.ds(off, R), :]
            for d in range(2):
                a_ref[d, pl.ds(off, R), :] = gates(d, x)[2]
            return 0

        lax.fori_loop(0, T // R, chunk_a, 0)
        _scan_tiles(dm, [(False, False, a_ref.at[0], dh_ref, lam_ref.at[0], 'lam'),
                         (True, True, a_ref.at[1], dh_ref, lam_ref.at[1], 'lam')])
        t = lax.broadcasted_iota(jnp.int32, (T, 1), 0)
        hp_ref[0] = jnp.where(t == 0, 0.0, pltpu.roll(hf_ref[...], 1, 0))
        hv = hb_ref[...]
        hp_ref[1] = jnp.where(t == C - 1, 0.0, jnp.where(t == T - 1, jnp.broadcast_to(hv[0:1, :], hv.shape),
                                                         pltpu.roll(hv, T - 1, 0)))

        def chunk_b(d):
            wa_, wx_ = wa_ref[d].astype(BF), wx_ref[d].astype(BF)

            def run(ci, carry):
                dwa, dwx, dba, dbx, dlam = carry
                off = pl.multiple_of(ci * R, 8)
                x = xc_ref[pl.ds(off, R), :]
                r, i, a, m = gates(d, x)
                lam = lam_ref[d, pl.ds(off, R), :]
                da = lam * hp_ref[d, pl.ds(off, R), :] - lam * (i * x) * a / m
                dloga = da * a
                dza = dloga * (-LRU_C) * sp_ref[d] * r * (1.0 - r)
                dzx = lam * m * x * i * (1.0 - i)
                dzab, dzxb = dza.astype(BF), dzx.astype(BF)
                xb = x.astype(BF)
                dxc = lam * m * i
                dxc = dxc + lax.dot_general(dzab, wa_, (NT, ((), ())), preferred_element_type=F32)
                dxc = dxc + lax.dot_general(dzxb, wx_, (NT, ((), ())), preferred_element_type=F32)
                if d == 0:
                    dxc_ref[pl.ds(off, R), :] = dxc
                else:
                    dxc_ref[pl.ds(off, R), :] += dxc
                dwa = dwa + lax.dot_general(xb, dzab, (TN, ((), ())), preferred_element_type=F32)
                dwx = dwx + lax.dot_general(xb, dzxb, (TN, ((), ())), preferred_element_type=F32)
                dba = dba + jnp.sum(dza, axis=0, keepdims=True)
                dbx = dbx + jnp.sum(dzx, axis=0, keepdims=True)
                dlam = dlam + jnp.sum(dloga * LRU_C * r, axis=0, keepdims=True)
                return dwa, dwx, dba, dbx, dlam

            z = jnp.zeros((W, W), F32)
            zv = jnp.zeros((1, W), F32)
            dwa, dwx, dba, dbx, dlam = lax.fori_loop(0, T // R, run, (z, z, zv, zv, zv))
            dwa_ref[d] = dwa
            dwx_ref[d] = dwx
            dba_ref[d] = dba
            dbx_ref[d] = dbx
            dlam_ref[d] = dlam * sg_ref[d]

        chunk_b(0)
        chunk_b(1)
        dxc = dxc_ref[...]
        dcb_ref[...] = jnp.sum(dxc, axis=0, keepdims=True)
        dcw_ref[...] = jnp.concatenate([jnp.sum(dxc * taps[k], axis=0, keepdims=True) for k in range(4)], axis=0)
        valid = [(t < T - 2) & ((t >= C) | (t < C - 2)), (t < T - 1) & ((t >= C) | (t < C - 1)), None,
                 (t != 0) & (t != C)]
        shifts = [T - 2, T - 1, 0, 1]
        dlx = dxc * cw_ref[2:3, :]
        for k in (0, 1, 3):
            dlx = dlx + jnp.where(valid[k], pltpu.roll(dxc, shifts[k], 0), 0.0) * cw_ref[k:k + 1, :]
        dlx_ref[...] = dlx.astype(dlx_ref.dtype)

    strip = lambda j: (0, j)
    sspec = pl.BlockSpec((T, W), strip)
    vec = pl.BlockSpec((2, 1, W), lambda j: (0, 0, j))
    mat = pl.BlockSpec((2, None, W, W), lambda j: (0, j, 0, 0))
    ovec = pl.BlockSpec((2, 1, W), lambda j: (0, 0, j))
    return pl.pallas_call(
        body, grid=(LB,), name="lru_bwd",
        in_specs=[pl.BlockSpec((T, W), lambda j: (0, lxb + j)), sspec, sspec, sspec,
                  pl.BlockSpec((4, W), strip), pl.BlockSpec((1, W), strip), mat, vec, mat, vec, vec, vec, ANY],
        out_specs=[pl.BlockSpec((T, W), lambda j: (0, lxb + j)), mat, ovec, mat, ovec, ovec,
                   pl.BlockSpec((4, W), strip), pl.BlockSpec((1, W), strip)],
        out_shape=[_sds(dP.shape, BF), _sds((2, LB, W, W), F32), _sds((2, 1, D), F32), _sds((2, LB, W, W), F32),
                   _sds((2, 1, D), F32), _sds((2, 1, D), F32), _sds((4, D), F32), _sds((1, D), F32)],
        scratch_shapes=[pltpu.VMEM((T, W), F32), pltpu.VMEM((2, T, W), F32), pltpu.VMEM((2, T, W), F32),
                        pltpu.VMEM((2, T, W), F32), pltpu.VMEM((T, W), F32)],
        input_output_aliases={12: 0}, compiler_params=_cparams(),
    )(P, dh, hf, hb, conv_w, conv_b, wa, ba, wx, bx, sp, sg, dP)


def merge_fwd(dm, P, attn, hf, hb):
    S, D, bm, cw, nCb = dm.S, dm.D, dm.bm, dm.cw, dm.nCb

    def fn(ids, lg_ref, ga_ref, gl_ref, at_ref, hf_ref, hb_ref):
        ge, _ = _gelu(lg_ref[...])
        lru = (hf_ref[...] + hb_ref[...]) * ge
        return [_sig(ga_ref[...]) * at_ref[...] + _sig(gl_ref[...]) * lru]

    pspec = lambda off: pl.BlockSpec((bm, cw), lambda i, j: (i + nCb, off // cw + j))
    tspec = pl.BlockSpec((bm, cw), lambda i, j: (i + nCb, j))
    sspec = pl.BlockSpec((bm, cw), lambda i, j: (i, j))
    return ew_call(
        "merge_fwd", (dm.nSb, D // cw), fn,
        [(P, pspec(dm.OFF_LG)), (P, pspec(dm.OFF_GA)), (P, pspec(dm.OFF_GL)), (attn, sspec), (hf, tspec), (hb, tspec)],
        [(_sds((S, D), BF), sspec, False)])[0]


def merge_bwd(dm, dmg, P, attn, hf, hb):
    S, T, D, bm, cw, nCb = dm.S, dm.T, dm.D, dm.bm, dm.cw, dm.nCb
    nj = D // cw

    def body(dm_ref, lg_ref, ga_ref, gl_ref, at_ref, hf_ref, hb_ref, dp_ref, da_ref, dh_ref, buf, sems):
        i, j = pl.program_id(0), pl.program_id(1)
        lat = i >= nCb
        d = jnp.where(lat, dm_ref[...].astype(F32), 0.0)
        lg = lg_ref[...]
        ge, th = _gelu(lg)
        hs = hf_ref[...] + hb_ref[...]
        sa, sl = _sig(ga_ref[...]), _sig(gl_ref[...])
        at = jnp.where(lat, at_ref[...], 0.0)
        dlru = d * sl
        buf[0] = (dlru * hs * _gelu_grad(lg, th)).astype(BF)
        buf[1] = (d * at * sa * (1.0 - sa)).astype(BF)
        buf[2] = (d * hs * ge * sl * (1.0 - sl)).astype(BF)
        da_ref[...] = (d * sa).astype(BF)
        dh_ref[...] = dlru * ge
        copies = []
        for g, off in enumerate((dm.OFF_LG, dm.OFF_GA, dm.OFF_GL)):
            col = pl.multiple_of(off + j * cw, 128)
            cp = pltpu.make_async_copy(buf.at[g], dp_ref.at[pl.ds(pl.multiple_of(i * bm, 8), bm), pl.ds(col, cw)],
                                       sems.at[g])
            cp.start()
            copies.append(cp)
        for cp in copies:
            cp.wait()

    pspec = lambda off: pl.BlockSpec((bm, cw), lambda i, j: (i, off // cw + j))
    tspec = pl.BlockSpec((bm, cw), lambda i, j: (i, j))
    lspec = pl.BlockSpec((bm, cw), lambda i, j: (jnp.maximum(i - nCb, 0), j))
    return pl.pallas_call(
        body, grid=(dm.nTb, nj), name="merge_bwd",
        in_specs=[lspec, pspec(dm.OFF_LG), pspec(dm.OFF_GA), pspec(dm.OFF_GL), lspec, tspec, tspec],
        out_specs=[ANY, tspec, tspec],
        out_shape=[_sds((T, dm.INW), BF), _sds((T, D), BF), _sds((T, D), F32)],
        scratch_shapes=[pltpu.VMEM((3, bm, cw), BF), pltpu.SemaphoreType.DMA((3,))],
        compiler_params=_cparams(),
    )(dmg, P, P, P, attn, hf, hb)


def final_loss(dm, x3, gfin, target):
    S, D, bm = dm.S, dm.D, dm.bm

    def fn(ids, x_ref, g_ref, t_ref):
        xv = x_ref[...]
        g = g_ref[...]
        r = lax.rsqrt(jnp.mean(xv * xv, axis=-1, keepdims=True) + EPS)
        xn = xv * r
        err = xn * g - t_ref[...]
        loss = 0.5 * jnp.sum(jnp.mean(err * err, axis=-1, keepdims=True), axis=0, keepdims=True)
        dy = err / D
        dxn = dy * g
        dx = r * (dxn - xn * jnp.mean(dxn * xn, axis=-1, keepdims=True))
        return [jnp.broadcast_to(loss, (1, 128)), dx, jnp.sum(dy * xn, axis=0, keepdims=True)]

    row = pl.BlockSpec((bm, D), lambda i: (i, 0))
    vec = pl.BlockSpec((1, D), lambda i: (0, 0))
    return ew_call(
        "final_loss", (dm.nSb,), fn, [(x3, row), (gfin, vec), (target, row)],
        [(_sds((1, 128), F32), pl.BlockSpec((1, 128), lambda i: (0, 0)), True), (_sds((S, D), F32), row, False),
         (_sds((1, D), F32), vec, True)], first=lambda ids: ids[0] == 0)


def local_step(dm, x, ctx, target, modv, norm_g3, gfin, gq, gk, conv_w, conv_b, wa, ba, wx, bx, lam, wbuf, where):
    S, C, T, D, NS, F4, W4 = dm.S, dm.C, dm.T, dm.D, dm.NS, dm.F4, dm.W4
    Ds = D // NS
    wb, nsub = dm.wb, dm.nsub
    cosf, sinf = rope_tables(dm)
    sp = jax.nn.softplus(-lam)
    sg = jax.nn.sigmoid(-lam)
    ident = lambda ids, accs, ex: list(accs)
    mT, mS, kT, kS = dm.mT, dm.mS, dm.kT, dm.kS
    bn = _pick(D, [1024, 512, 256, 128])
    bk = _pick(D, [512, 256, 128])

    wg0, wu0 = comm_call("ag_ffn1", ag_comm([wbuf['wg0'], wbuf['wu0']]))
    h1, xt = normmod_concat_fwd("nm1", dm, ctx, x, norm_g3, modv)
    xt1, a1, u1, s1, f1, land, _ = ffn_fwd("ffn1", dm, h1, xt, wg0, wu0, None, modv, 2, True,
                                           comm_up=ag_comm([wbuf['wd0'], wbuf['w_in']]))
    wd0, w_in = land
    h2 = normmod_fwd("nm2", dm, xt1, norm_g3, 1, modv, True)
    P = fused_mm(
        "w_in", (T // mT, NS * nsub, 1),
        [(h2, pl.BlockSpec((mT, D), lambda i, j, k: (i, 0))),
         (w_in, pl.BlockSpec((None, D, wb), lambda i, j, k: (j // nsub, 0, j % nsub)))],
        [(0, 1, NN, 0)], [(mT, wb)], ident,
        [(_sds((T, dm.INW), F32), pl.BlockSpec((mT, wb), lambda i, j, k: (i, j)))])[0]
    qr, kr, vb = qk_prep(dm, P, gq, gk, cosf, sinf)
    (attn,), (wg1, wu1, wd1, w_out) = attention_fwd(
        dm, qr, kr, vb, comm=ag_comm([wbuf['wg1'], wbuf['wu1'], wbuf['wd1'], wbuf['w_out']]))
    w_out = w_out.reshape(D, D)
    hf, hb = lru_fwd(dm, P, conv_w, conv_b, wa, ba, wx, bx, sp)
    mg = merge_fwd(dm, P, attn, hf, hb)

    def epi_o(ids, accs, ex):
        o = accs[0]
        return [ex[0][...] + ex[1][...] * o, o]

    rb, nCb = dm.bm, dm.nCb
    x2, o2 = fused_mm(
        "w_out", (D // bn, S // rb, 1),
        [(mg, pl.BlockSpec((rb, D), lambda j, i, k: (i, 0))), (w_out, pl.BlockSpec((D, bn), lambda j, i, k: (0, j)))],
        [(0, 1, NN, 0)], [(rb, bn)], epi_o,
        [(_sds((S, D), F32), pl.BlockSpec((rb, bn), lambda j, i, k: (i, j))),
         (_sds((S, D), BF), pl.BlockSpec((rb, bn), lambda j, i, k: (i, j)))],
        extras=[(xt1, pl.BlockSpec((rb, bn), lambda j, i, k: (i + nCb, j))),
                (modv, pl.BlockSpec((None, None, 1, bn), lambda j, i, k: (1, 5, 0, j)))])
    h3 = normmod_fwd("nm3", dm, x2, norm_g3, 2, modv, False)
    x3, a3, u3, s3, f3, _, _ = ffn_fwd("ffn2", dm, h3, x2, wg1, wu1, wd1, modv, 8, False)
    loss, dx3, dgfin = final_loss(dm, x3, gfin, target)

    df3, dg3 = gate_bwd("gate3", dm, dx3, f3, modv, 8, FFN_RES, False)
    dh3, dwg1, dwu1, dwd1, _ = ffn_bwd("ffn2b", dm, df3, h3, a3, u3, s3, wg1, wu1, wd1, False)
    dx2, dsh3, dsc3, dgn3 = normmod_bwd("nm3b", dm, dh3, x2, dx3, norm_g3, 2, modv, False, False)
    do2, dg2 = gate_bwd("gate2", dm, dx2, o2, modv, 5, 1.0, False)
    keep = {}

    def host_a(key, comm):
        if key == 'p1':
            (keep['dmg'],), landed = fused_mm(
                "w_out_dx", (S // mS, D // bn, 1),
                [(do2, pl.BlockSpec((mS, D), lambda i, j, k: (i, 0))),
                 (w_out, pl.BlockSpec((bn, D), lambda i, j, k: (j, 0)))],
                [(0, 1, NT, 0)], [(mS, bn)], ident,
                [(_sds((S, D), BF), pl.BlockSpec((mS, bn), lambda i, j, k: (i, j)))], comm=comm)
            return landed
        keep['dqkv'], landed = attention_bwd(dm, qr, kr, vb, attn, keep['dattn'], comm=comm)
        return landed

    gots_a = host_a('p1', rs_p1_comm([dwg1, dwu1, dwd1]))
    dmg = keep['dmg']
    dw_out = fused_mm(
        "w_out_dw", (D // bn, D // bn, S // kS),
        [(mg, pl.BlockSpec((kS, bn), lambda i, j, k: (k, i))), (do2, pl.BlockSpec((kS, bn), lambda i, j, k: (k, j)))],
        [(0, 1, TN, 0)], [(bn, bn)], ident,
        [(_sds((D, D), BF), pl.BlockSpec((bn, bn), lambda i, j, k: (i, j)))])[0]
    dP, dattn, dhs = merge_bwd(dm, dmg, P, attn, hf, hb)
    keep['dattn'] = dattn
    pairs_a = [add_pair("rs_add_" + n_, g_, got_, where)
               for n_, g_, got_ in zip(('wg1', 'wu1', 'wd1'), (dwg1, dwu1, dwd1), gots_a)]
    land_a = host_a('p2', rs_p2_comm([p_[0] for p_ in pairs_a], [p_[1] for p_ in pairs_a]))
    dq, dk, dv = keep['dqkv']
    dP, dwa, dba, dwx, dbx, dlam, dcw, dcb = lru_bwd(dm, P, dhs, hf, hb, conv_w, conv_b, wa, ba, wx, bx, sp, sg, dP)
    dP, dgq, dgk = qk_prep_bwd(dm, dq, dk, dv, P, gq, gk, cosf, sinf, dP)
    g_wg = sum_slots_into("rs_sum_wg1", land_a[0], where, None, (2, D, F4), 1)
    g_wu = sum_slots_into("rs_sum_wu1", land_a[1], where, None, (2, D, F4), 1)
    g_wd = sum_slots_into("rs_sum_wd1", land_a[2], where, None, (2, F4, D), 1)
    LBD = D // dm.LB
    NM = 2 * dm.LB * LBD
    mats = [dwa.reshape(1, NM, LBD), dwx.reshape(1, NM, LBD)]
    (dh2,), landed_x = fused_mm(
        "w_in_dx", (T // mT, D // bn, NS),
        [(dP, pl.BlockSpec((mT, W4), lambda i, j, k: (i, k))),
         (w_in, pl.BlockSpec((None, bn, W4), lambda i, j, k: (k, j, 0)))],
        [(0, 1, NT, 0)], [(mT, bn)], ident,
        [(_sds((T, D), F32), pl.BlockSpec((mT, bn), lambda i, j, k: (i, j)))],
        comm=merge_comms([rs_p3_comm([g_wg, g_wu, g_wd], [(0, 1, D // 2), (1, 1, D // 2), (2, 1, F4 // 2)]),
                          rs_p1_comm(mats)]))
    g_wg, g_wu, g_wd = landed_x[:3]
    pairs_m = [add_pair("rs_add_" + n_, g_, got_, where) for n_, g_, got_ in zip(('lru_wa', 'lru_wx'), mats, landed_x[3:])]
    (dw_in,), land_m = fused_mm(
        "w_in_dw", (D // bn, NS, T // kT),
        [(h2, pl.BlockSpec((kT, bn), lambda i, j, k: (k, i))), (dP, pl.BlockSpec((kT, W4), lambda i, j, k: (k, j)))],
        [(0, 1, TN, 0)], [(bn, W4)], ident,
        [(_sds((NS, D, W4), BF), pl.BlockSpec((None, bn, W4), lambda i, j, k: (j, i, 0)))],
        comm=rs_p2_comm([p_[0] for p_ in pairs_m], [p_[1] for p_ in pairs_m]))
    g_wa = sum_slots_into("rs_sum_lru_wa", land_m[0], where, None, (NM, LBD), None)
    g_wx = sum_slots_into("rs_sum_lru_wx", land_m[1], where, None, (NM, LBD), None)
    dxt1, dsh2, dsc2, dgn2 = normmod_bwd("nm2b", dm, dh2, xt1, dx2, norm_g3, 1, modv, True, True)
    df1, dg1 = gate_bwd("gate1", dm, dxt1, f1, modv, 2, FFN_RES, True)

    tens_b = [dw_in, dw_out.reshape(NS, Ds, D)]

    def host_ds(landed, made):
        return merge_comms([rs_p1_comm(tens_b), rs_p3_comm([g_wa, g_wx], [(0, None, NM // 2), (1, None, NM // 2)])])

    def host_dwgu(landed, made):
        pairs = [add_pair("rs_add_" + n_, g_, got_, where) for n_, g_, got_ in zip(('w_in', 'w_out'), tens_b, landed['ds'][:2])]
        return rs_p2_comm([p_[0] for p_ in pairs], [p_[1] for p_ in pairs])

    def host_dwd(landed, made):
        g_win = sum_slots_into("rs_sum_w_in", landed['dwgu'][0], where, None, (D, W4), None)
        g_wout = sum_slots_into("rs_sum_w_out", landed['dwgu'][1], where, None, (Ds, D), None)
        return merge_comms([rs_p3_comm([g_win, g_wout], [(0, None, D // 2), (1, None, Ds // 2)]),
                            rs_p1_comm([made['dwg'], made['dwu']])])

    def host_dh(landed, made):
        pairs = [add_pair("rs_add_" + n_, g_, got_, where)
                 for n_, g_, got_ in zip(('wg0', 'wu0'), (made['dwg'], made['dwu']), landed['dwd'][2:])]
        return merge_comms([rs_p2_comm([p_[0] for p_ in pairs], [p_[1] for p_ in pairs]), rs_p1_comm([made['dwd']])])

    dh1, dwg0, dwu0, dwd0, landed = ffn_bwd(
        "ffn1b", dm, df1, h1, a1, u1, s1, wg0, wu0, wd0, True,
        comms={'ds': host_ds, 'dwgu': host_dwgu, 'dwd': host_dwd, 'dh': host_dh})
    g_wa, g_wx = landed['ds'][2:]
    g_win, g_wout = landed['dwd'][:2]
    g_wg = sum_slots_into("rs_sum_wg0", landed['dh'][0], where, g_wg, (2, D, F4), 0)
    g_wu = sum_slots_into("rs_sum_wu0", landed['dh'][1], where, g_wu, (2, D, F4), 0)
    late = add_pair("rs_add_wd0", dwd0, landed['dh'][2], where)
    grad_x, dsh1, dsc1, dgn1 = normmod_bwd("nm1b", dm, dh1, xt, dxt1, norm_g3, 0, modv, True, False, out_lat_only=True)

    dmod = jnp.concatenate([dsh1, dsc1, dg1, dsh2, dsc2, _lat(dg2), _lat(dsh3), _lat(dsc3), _lat(dg3)], axis=1)
    dnorm = jnp.stack([dgn1[0, 0] + dgn1[1, 0], dgn2[0, 0] + dgn2[1, 0], dgn3[1, 0]], axis=0)
    small = dict(norm_g=dnorm, q_norm_g=dgq, k_norm_g=dgk, conv_w=dcw, conv_b=dcb,
                 lru_ba=dba.reshape(2, D), lru_bx=dbx.reshape(2, D), lru_lambda=dlam.reshape(2, D), final_norm_g=dgfin)
    reduced = dict(ffn_wg=g_wg, ffn_wu=g_wu, ffn_wd=g_wd, w_in=g_win, w_out=g_wout, lru_wa=g_wa, lru_wx=g_wx)
    return loss, grad_x, dmod, small, reduced, late


def _lat(v):
    return jnp.concatenate([jnp.zeros_like(v[:1]), v[1:]], axis=0)


def _me():
    return lax.axis_index("x"), lax.axis_index("y"), lax.axis_index("c")


def allgather8(name, v):
    def body(v_ref, out_ref, send_sems, recv_sems, local_sem):
        x, y, c = _me()
        me = 4 * x + 2 * y + c
        mine = pltpu.make_async_copy(v_ref, out_ref.at[me], local_sem)
        mine.start()
        copies = []
        for k in range(1, 8):
            peer = (x ^ ((k >> 2) & 1), y ^ ((k >> 1) & 1), c ^ (k & 1))
            cp = pltpu.make_async_remote_copy(src_ref=v_ref, dst_ref=out_ref.at[me], send_sem=send_sems.at[k - 1],
                                              recv_sem=recv_sems.at[k - 1], device_id=peer, device_id_type=MESH)
            cp.start()
            copies.append(cp)
        for k in range(1, 8):
            peer = (x ^ ((k >> 2) & 1), y ^ ((k >> 1) & 1), c ^ (k & 1))
            pltpu.make_async_remote_copy(src_ref=v_ref, dst_ref=out_ref.at[me ^ k], send_sem=send_sems.at[k - 1],
                                         recv_sem=recv_sems.at[k - 1], device_id=peer, device_id_type=MESH).wait_recv()
        for cp in copies:
            cp.wait_send()
        mine.wait()

    return pl.pallas_call(
        body, name=name, out_shape=_sds((8,) + v.shape, v.dtype), in_specs=[ANY], out_specs=ANY,
        scratch_shapes=[pltpu.SemaphoreType.DMA((7,)), pltpu.SemaphoreType.DMA((7,)), pltpu.SemaphoreType.DMA],
    )(v)


def _chips(x, y):
    chips = [(1 - x, y), (x, 1 - y), (1 - x, 1 - y)]
    return chips, [2 * cx + cy for cx, cy in chips]


def ag_comm(bufs):
    n = len(bufs)

    def parts(outs):
        x, y, c = _me()
        chips, slots = _chips(x, y)
        return x, y, c, 2 * x + y, (x, y, 1 - c), chips, slots

    def ici(outs, t, j, send_sems, recv_sems, src_slot, off):
        x, y, c, s, sib, chips, slots = parts(outs)
        H = outs[t].shape[1] // 2
        blk = outs[t].at[src_slot, pl.ds(c * H, H)]
        return pltpu.make_async_remote_copy(
            src_ref=blk, dst_ref=blk, send_sem=send_sems.at[off + 6 * t + j], recv_sem=recv_sems.at[off + 6 * t + j],
            device_id=(chips[j][0], chips[j][1], c), device_id_type=MESH)

    def d2d(outs, t, j, send_sems, recv_sems, half, off):
        x, y, c, s, sib, chips, slots = parts(outs)
        H = outs[t].shape[1] // 2
        blk = outs[t].at[slots[j], pl.ds(half * H, H)]
        return pltpu.make_async_remote_copy(
            src_ref=blk, dst_ref=blk, send_sem=send_sems.at[off + 6 * t + 3 + j],
            recv_sem=recv_sems.at[off + 6 * t + 3 + j], device_id=sib, device_id_type=MESH)

    def start(reads, outs, send_sems, recv_sems, off=0):
        x, y, c, s, sib, chips, slots = parts(outs)
        for t in range(n):
            for j in range(3):
                ici(outs, t, j, send_sems, recv_sems, s, off).start()

    def finish(reads, outs, send_sems, recv_sems, off=0):
        x, y, c, s, sib, chips, slots = parts(outs)
        for t in range(n):
            for j in range(3):
                ici(outs, t, j, send_sems, recv_sems, slots[j], off).wait_recv()
                d2d(outs, t, j, send_sems, recv_sems, c, off).start()
        for t in range(n):
            for j in range(3):
                d2d(outs, t, j, send_sems, recv_sems, 1 - c, off).wait_recv()
        for t in range(n):
            for j in range(3):
                ici(outs, t, j, send_sems, recv_sems, s, off).wait_send()
                d2d(outs, t, j, send_sems, recv_sems, c, off).wait_send()

    return Comm([], bufs, 6 * n, start, finish)


def rs_p1_comm(tensors):
    n = len(tensors)

    def copy(ins, gots, t, send_sems, recv_sems, off):
        x, y, c = _me()
        H = ins[t].shape[1] // 2
        return pltpu.make_async_remote_copy(
            src_ref=ins[t].at[:, pl.ds((1 - c) * H, H)], dst_ref=gots[t], send_sem=send_sems.at[off + t],
            recv_sem=recv_sems.at[off + t], device_id=(x, y, 1 - c), device_id_type=MESH)

    def start(ins, gots, send_sems, recv_sems, off=0):
        for t in range(n):
            copy(ins, gots, t, send_sems, recv_sems, off).start()

    def finish(ins, gots, send_sems, recv_sems, off=0):
        for t in range(n):
            copy(ins, gots, t, send_sems, recv_sems, off).wait_recv()
        for t in range(n):
            copy(ins, gots, t, send_sems, recv_sems, off).wait_send()

    half = lambda t: _sds((t.shape[0], t.shape[1] // 2) + t.shape[2:], t.dtype)
    return Comm(tensors, [half(t) for t in tensors], n, start, finish)


def rs_p2_comm(partials, landeds):
    n = len(partials)

    def start(ins, outs, send_sems, recv_sems, off=0):
        x, y, c = _me()
        s = 2 * x + y
        chips, slots = _chips(x, y)
        for t in range(n):
            for j, chip in enumerate(chips):
                src = ins[t].at[slots[j]] if ins[t].shape[0] == 4 else ins[t].at[0]
                pltpu.make_async_remote_copy(
                    src_ref=src, dst_ref=outs[t].at[s], send_sem=send_sems.at[off + 3 * t + j],
                    recv_sem=recv_sems.at[off + 3 * t + j], device_id=(chip[0], chip[1], c), device_id_type=MESH).start()

    def finish(ins, outs, send_sems, recv_sems, off=0):
        x, y, c = _me()
        s = 2 * x + y
        chips, slots = _chips(x, y)
        for t in range(n):
            for j, chip in enumerate(chips):
                dst = outs[t].at[slots[j]]
                pltpu.make_async_remote_copy(
                    src_ref=dst, dst_ref=dst, send_sem=send_sems.at[off + 3 * t + j],
                    recv_sem=recv_sems.at[off + 3 * t + j], device_id=(chip[0], chip[1], c), device_id_type=MESH).wait_recv()
        for t in range(n):
            for j, chip in enumerate(chips):
                src = ins[t].at[slots[j]] if ins[t].shape[0] == 4 else ins[t].at[0]
                pltpu.make_async_remote_copy(
                    src_ref=src, dst_ref=outs[t].at[s], send_sem=send_sems.at[off + 3 * t + j],
                    recv_sem=recv_sems.at[off + 3 * t + j], device_id=(chip[0], chip[1], c), device_id_type=MESH).wait_send()

    return Comm(partials, landeds, 3 * n, start, finish)


def rs_p3_comm(greds, plan):
    n = len(plan)

    def copy(outs, t, send_sems, recv_sems, half, off):
        x, y, c = _me()
        oi, li, H = plan[t]
        dst = outs[oi] if li is None else outs[oi].at[li]
        blk = dst.at[pl.ds((c if half == 0 else 1 - c) * H, H)]
        return pltpu.make_async_remote_copy(
            src_ref=blk, dst_ref=blk, send_sem=send_sems.at[off + t], recv_sem=recv_sems.at[off + t],
            device_id=(x, y, 1 - c), device_id_type=MESH)

    def start(reads, outs, send_sems, recv_sems, off=0):
        for t in range(n):
            copy(outs, t, send_sems, recv_sems, 0, off).start()

    def finish(reads, outs, send_sems, recv_sems, off=0):
        for t in range(n):
            copy(outs, t, send_sems, recv_sems, 1, off).wait_recv()
        for t in range(n):
            copy(outs, t, send_sems, recv_sems, 0, off).wait_send()

    return Comm([], greds, n, start, finish)


def _rows_block(rows, cols, nbytes=1 << 20):
    bm = 8
    while bm * 2 * cols * 4 <= nbytes and rows % (bm * 2) == 0:
        bm *= 2
    return bm


def cast_into_slot(name, w, where, layer=None):
    rows, W = w.shape[-2:]
    bm = _rows_block(rows, W, 4 << 20)

    def body(p_ref, w_ref, o_ref):
        o_ref[...] = w_ref[...].astype(BF)

    if layer is None:
        ispec = pl.BlockSpec((bm, W), lambda i, p: (i, 0))
    else:
        ispec = pl.BlockSpec((None, bm, W), lambda i, p: (layer, i, 0))
    return pl.pallas_call(
        body, name=name, out_shape=_sds((4, rows, W), BF), compiler_params=_cparams(),
        grid_spec=pltpu.PrefetchScalarGridSpec(
            num_scalar_prefetch=1, grid=(rows // bm,), in_specs=[ispec],
            out_specs=pl.BlockSpec((None, bm, W), lambda i, p: (p[1], i, 0))),
    )(where, w)


def add_pair(name, g, got, where):
    K, R, W = g.shape
    H = R // 2
    bm = _rows_block(H, W, 4 << 20)
    nh = H // bm

    def body(p_ref, g_ref, got_ref, part_ref, land_ref):
        k = pl.program_id(1)
        v = (g_ref[...].astype(F32) + got_ref[...].astype(F32)).astype(part_ref.dtype)
        part_ref[...] = v
        own = (k == p_ref[1]) if K == 4 else (k == 0)

        @pl.when(own)
        def _():
            land_ref[...] = v

    return pl.pallas_call(
        body, name=name, out_shape=[_sds((K, H, W), g.dtype), _sds((4, H, W), g.dtype)], compiler_params=_cparams(),
        grid_spec=pltpu.PrefetchScalarGridSpec(
            num_scalar_prefetch=1, grid=(nh, K),
            in_specs=[pl.BlockSpec((None, bm, W), lambda i, k, p: (k, p[0] * nh + i, 0)),
                      pl.BlockSpec((None, bm, W), lambda i, k, p: (k, i, 0))],
            out_specs=[pl.BlockSpec((None, bm, W), lambda i, k, p: (k, i, 0)),
                       pl.BlockSpec((None, bm, W), lambda i, k, p: (p[1], i, 0))]),
    )(where, g, got)


def sum_slots_into(name, landed, where, dest, dest_shape, li):
    K, H, W = landed.shape
    bm = _rows_block(H, 2 * W, 4 << 20)
    nh = H // bm

    def body(*refs):
        r, o_ref = refs[1], refs[-1]
        acc = r[0].astype(F32)
        for k in range(1, K):
            acc = acc + r[k].astype(F32)
        o_ref[...] = acc

    if li is None:
        ospec = pl.BlockSpec((bm, W), lambda i, p: (p[0] * nh + i, 0))
    else:
        ospec = pl.BlockSpec((None, bm, W), lambda i, p: (li, p[0] * nh + i, 0))
    in_specs = [pl.BlockSpec((K, bm, W), lambda i, p: (0, i, 0))]
    args = [where, landed]
    aliases = {}
    if dest is not None:
        in_specs.append(ANY)
        args.append(dest)
        aliases = {2: 0}
    return pl.pallas_call(
        body, name=name, out_shape=_sds(dest_shape, F32), compiler_params=_cparams(), input_output_aliases=aliases,
        grid_spec=pltpu.PrefetchScalarGridSpec(num_scalar_prefetch=1, grid=(nh,), in_specs=in_specs, out_specs=ospec),
    )(*args)


def sum_slots(name, a):
    K, H, W = a.shape
    bm = _rows_block(H, W * K // 2)

    def fn(ids, r):
        acc = r[0]
        for k in range(1, K):
            acc = acc + r[k]
        return [acc]

    return ew_call(name, (H // bm,), fn, [(a, pl.BlockSpec((K, bm, W), lambda i: (0, i, 0)))],
                   [(_sds((H, W), F32), pl.BlockSpec((bm, W), lambda i: (i, 0)), False)])[0]


def _adamw_math(w, g, m, v):
    bc1 = 1.0 - ADAM_B1 ** ADAM_STEP
    bc2 = 1.0 - ADAM_B2 ** ADAM_STEP
    mn = ADAM_B1 * m + (1.0 - ADAM_B1) * g
    vn = ADAM_B2 * v + (1.0 - ADAM_B2) * (g * g)
    m_hat = mn / bc1
    v_hat = vn / bc2
    delta = -ADAM_LR * (m_hat / (jnp.sqrt(v_hat) + ADAM_EPS) + ADAM_WD * w)
    return delta, mn, vn


def adamw(name, w, g, m, v, copy_grad=False):
    shape = w.shape
    flat = lambda t: t.reshape(-1, shape[-1])
    w2, g2, m2, v2 = flat(w), flat(g), flat(m), flat(v)
    bm = _rows_block(w2.shape[0], w2.shape[1])

    def fn(ids, w_ref, g_ref, m_ref, v_ref):
        gv = g_ref[...]
        return list(_adamw_math(w_ref[...], gv, m_ref[...], v_ref[...])) + ([gv] if copy_grad else [])

    spec = pl.BlockSpec((bm, w2.shape[1]), lambda i: (i, 0))
    res = ew_call(name, (w2.shape[0] // bm,), fn, [(w2, spec), (g2, spec), (m2, spec), (v2, spec)],
                  [(_sds(w2.shape, F32), spec, False)] * (4 if copy_grad else 3))
    return [o.reshape(shape) for o in res]


def adamw_many(name, params):
    n = len(params)
    shapes = [p_[0].shape for p_ in params]
    two_d = lambda t: t.reshape(-1, t.shape[-1])
    args = [two_d(t) for p_ in params for t in p_]

    def body(*refs):
        ins, outs = refs[:4 * n], refs[4 * n:]
        for q in range(n):
            w_ref, g_ref, m_ref, v_ref = ins[4 * q:4 * q + 4]
            for o_ref, val in zip(outs[3 * q:3 * q + 3], _adamw_math(w_ref[...], g_ref[...], m_ref[...], v_ref[...])):
                o_ref[...] = val

    full = lambda a: pl.BlockSpec(a.shape, lambda i: (0, 0))
    out_shape = [_sds(args[4 * q].shape, F32) for q in range(n) for _ in range(3)]
    res = hosted_call(body, name=name, grid=(1,), in_specs=[full(a) for a in args],
                      out_specs=[full(o) for o in out_shape], out_shape=out_shape, args=args)
    return [tuple(res[3 * q + r].reshape(shapes[q]) for r in range(3)) for q in range(n)]


def dmod_pack(gd):
    N = gd.shape[-1]
    bn = _pick(N, [4608, 2304, 1152, 1024, 512, 256, 128])

    def fn(ids, r):
        lat = [r[d, 1:2, :] for d in range(8)]
        cs = r[0, 0:1, :]
        for d in range(1, 8):
            cs = cs + r[d, 0:1, :]
        tot = cs
        for d in range(8):
            tot = tot + lat[d]
        return [jnp.concatenate(lat + [cs, jnp.zeros((7, bn), F32)], axis=0), tot]

    return ew_call("dmod_pack", (N // bn,), fn, [(gd, pl.BlockSpec((8, 2, bn), lambda j: (0, 0, j)))],
                   [(_sds((16, N), F32), pl.BlockSpec((16, bn), lambda j: (0, j)), False),
                    (_sds((1, N), F32), pl.BlockSpec((1, bn), lambda j: (0, j)), False)])


def _silu(v):
    return v * _sig(v)


def kernel(x, c, ctx, c_ctx, w_mod, b_mod, norm_g, ffn_wg, ffn_wu, ffn_wd, w_in, w_out, q_norm_g, k_norm_g, conv_w, conv_b, lru_wa, lru_ba, lru_wx, lru_bx, lru_lambda, final_norm_g, loss_target, m_c_ctx, m_w_mod, m_b_mod, m_norm_g, m_ffn_wg, m_ffn_wu, m_ffn_wd, m_w_in, m_w_out, m_q_norm_g, m_k_norm_g, m_conv_w, m_conv_b, m_lru_wa, m_lru_ba, m_lru_wx, m_lru_bx, m_lru_lambda, m_final_norm_g, v_c_ctx, v_w_mod, v_b_mod, v_norm_g, v_ffn_wg, v_ffn_wu, v_ffn_wd, v_w_in, v_w_out, v_q_norm_g, v_k_norm_g, v_conv_w, v_conv_b, v_lru_wa, v_lru_ba, v_lru_wx, v_lru_bx, v_lru_lambda, v_final_norm_g):
    given = dict(locals())
    names = ['c_ctx', 'w_mod', 'b_mod', 'norm_g', 'ffn_wg', 'ffn_wu', 'ffn_wd', 'w_in', 'w_out', 'q_norm_g', 'k_norm_g',
             'conv_w', 'conv_b', 'lru_wa', 'lru_ba', 'lru_wx', 'lru_bx', 'lru_lambda', 'final_norm_g']
    S, D = x.shape[1], x.shape[2]
    C = ctx.shape[1]
    NS = 4
    F4, W4, LB = ffn_wg.shape[-1], w_in.shape[-1], lru_wa.shape[2]
    dm = Dims(S, C, D, F4, W4, NS, LB)
    Ds = D // NS
    Wm = w_mod.shape[-1]
    xi, yi, ci = lax.axis_index("x"), lax.axis_index("y"), lax.axis_index("c")
    slot = 2 * xi + yi
    me = 4 * xi + 2 * yi + ci
    ident = lambda ids, accs, ex: list(accs)

    pack1 = jnp.concatenate([c.reshape(-1), norm_g.reshape(-1), conv_w.reshape(-1), lru_ba.reshape(-1),
                             lru_bx.reshape(-1), lru_lambda.reshape(-1)]).reshape(1, -1)
    g1 = allgather8("ag_small_params", pack1)[:, 0]
    c_all = g1[:, :D]

    def unshard(off, k):
        part = g1[0::2, off:off + k * Ds].reshape(NS, k, Ds)
        return jnp.transpose(part, (1, 0, 2)).reshape(k, D)

    norm_g_f = unshard(D, 3)
    conv_w_f = unshard(D + 3 * Ds, 4)
    ba_f = unshard(D + 7 * Ds, 2)
    bx_f = unshard(D + 9 * Ds, 2)
    lam_f = unshard(D + 11 * Ds, 2)

    call16 = jnp.concatenate([c_all, c_ctx.reshape(1, D), jnp.zeros((7, D), F32)], axis=0)
    b_cols = lax.dynamic_slice(b_mod, (0, slot * Wm), (1, Wm))
    bnm = _pick(Wm, [1536, 1152, 768, 512, 384, 256, 128])
    bkm = _pick(D, [512, 256, 128])
    modp = fused_mm(
        "mod_fwd", (1, Wm // bnm, D // bkm),
        [(call16, pl.BlockSpec((16, bkm), lambda i, j, k: (0, k))),
         (w_mod[0], pl.BlockSpec((bkm, bnm), lambda i, j, k: (k, j)))],
        [(0, 1, NN, 0)], [(16, bnm)], lambda ids, accs, ex: [accs[0] + ex[0][...]],
        [(_sds((16, Wm), F32), pl.BlockSpec((16, bnm), lambda i, j, k: (0, j)))],
        extras=[(b_cols, pl.BlockSpec((1, bnm), lambda i, j, k: (0, j)))], pre={0: _silu})[0]
    gm = allgather8("ag_mod", modp)
    mod_full = jnp.concatenate([gm[0], gm[2], gm[4], gm[6]], axis=1)
    mod_x = lax.dynamic_index_in_dim(mod_full, me, axis=0, keepdims=False)
    modv = jnp.stack([mod_full[8], mod_x]).reshape(2, N_MOD, 1, D)

    where = jnp.stack([ci, slot]).astype(jnp.int32)
    wbuf = {}
    for key, short in (('ffn_wg', 'wg'), ('ffn_wu', 'wu'), ('ffn_wd', 'wd')):
        for l in range(2):
            wbuf[short + str(l)] = cast_into_slot("cast_%s%d" % (short, l), given[key][0], where, l)
    wbuf['w_in'] = cast_into_slot("cast_w_in", w_in[0], where)
    wbuf['w_out'] = cast_into_slot("cast_w_out", w_out[0], where)

    loss_l, grad_x, dmod, small, reduced, late = local_step(
        dm, x[0], ctx[0], loss_target[0], modv, norm_g_f.reshape(3, 1, D), final_norm_g.reshape(1, D),
        q_norm_g, k_norm_g, conv_w_f, conv_b, lru_wa[0], ba_f.reshape(2, 1, D), lru_wx[0], bx_f.reshape(2, 1, D),
        lam_f.reshape(2, 1, D), wbuf, where)
    loss = lax.psum(loss_l[0, 0], ("x", "y", "c"))

    grads = {}
    gd = allgather8("ag_dmod", dmod.reshape(2, N_MOD * D))
    dM, g_bmod = dmod_pack(gd)
    dMc = lax.dynamic_slice(dM, (0, slot * Wm), (16, Wm))
    bmm = _pick(D, [512, 256, 128])
    grads['w_mod'] = fused_mm(
        "w_mod_dw", (D // bmm, Wm // bnm, 1),
        [(call16, pl.BlockSpec((16, bmm), lambda i, j, k: (0, i))), (dMc, pl.BlockSpec((16, bnm), lambda i, j, k: (0, j)))],
        [(0, 1, TN, 0)], [(bmm, bnm)], ident,
        [(_sds((D, Wm), F32), pl.BlockSpec((bmm, bnm), lambda i, j, k: (i, j)))], pre={0: _silu})[0][None]
    grads['b_mod'] = g_bmod

    def epi_cc(ids, accs, ex):
        v = ex[0][...]
        sg = _sig(v)
        return [accs[0] * (sg * (1.0 + v * (1.0 - sg)))]

    pcc = fused_mm(
        "c_ctx_partial", (1, D // bmm, Wm // bnm),
        [(dMc, pl.BlockSpec((16, bnm), lambda i, j, k: (0, k))), (w_mod[0], pl.BlockSpec((bmm, bnm), lambda i, j, k: (j, k)))],
        [(0, 1, NT, 0)], [(16, bmm)], epi_cc,
        [(_sds((16, D), F32), pl.BlockSpec((16, bmm), lambda i, j, k: (0, j)))],
        extras=[(c_ctx.reshape(1, D), pl.BlockSpec((1, bmm), lambda i, j, k: (0, j)))])[0]
    pcc_row = jnp.where(ci == 0, pcc[8], 0.0)

    order = ['q_norm_g', 'k_norm_g', 'conv_b', 'final_norm_g', 'norm_g', 'conv_w', 'lru_ba', 'lru_bx', 'lru_lambda']
    flat = [small[k].reshape(-1) for k in order] + [pcc_row]
    sizes = [f.shape[0] for f in flat]
    tot = sum(sizes)
    LW = 1024
    padded = -(-tot // (8 * LW)) * (8 * LW)
    tiny = jnp.concatenate(flat + [jnp.zeros((padded - tot,), F32)]).reshape(-1, LW)
    summed = sum_slots("tiny_sum", allgather8("ag_tiny_grads", tiny)).reshape(-1)
    offs = {}
    o = 0
    for k, n_ in zip(order + ['c_ctx'], sizes):
        offs[k] = summed[o:o + n_]
        o += n_
    shard = lambda k, rows: lax.dynamic_slice_in_dim(offs[k].reshape(rows, D), slot * Ds, Ds, axis=1)
    grads['c_ctx'] = offs['c_ctx']
    grads['q_norm_g'] = offs['q_norm_g'].reshape(1, HEAD_DIM)
    grads['k_norm_g'] = offs['k_norm_g'].reshape(1, HEAD_DIM)
    grads['conv_b'] = offs['conv_b'].reshape(1, D)
    grads['final_norm_g'] = offs['final_norm_g']
    grads['norm_g'] = shard('norm_g', 3)[None]
    grads['conv_w'] = shard('conv_w', 4)[None]
    grads['lru_ba'] = shard('lru_ba', 2)[None]
    grads['lru_bx'] = shard('lru_bx', 2)[None]
    grads['lru_lambda'] = shard('lru_lambda', 2)[None]

    landed = comm_call("rs_tail_p2", rs_p2_comm([late[0]], [late[1]]))
    g_wd = sum_slots_into("rs_sum_wd0", landed[0], where, reduced['ffn_wd'], (2, F4, D), 0)
    g_wg, g_wu, g_wd = comm_call("rs_tail_p3", rs_p3_comm(
        [reduced['ffn_wg'], reduced['ffn_wu'], g_wd], [(0, 0, D // 2), (1, 0, D // 2), (2, 0, F4 // 2)]))
    grads.update(ffn_wg=g_wg[None], ffn_wu=g_wu[None], ffn_wd=g_wd[None], w_in=reduced['w_in'][None],
                 w_out=reduced['w_out'][None], lru_wa=reduced['lru_wa'].reshape(lru_wa.shape),
                 lru_wx=reduced['lru_wx'].reshape(lru_wx.shape))

    delta, new_m, new_v = {}, {}, {}
    big_names = ['w_mod', 'w_in', 'w_out', 'lru_wa', 'lru_wx', 'ffn_wg', 'ffn_wu', 'ffn_wd']
    for k in big_names:
        res = adamw("adamw_" + k, given[k], grads[k], given['m_' + k], given['v_' + k], copy_grad=(k != 'w_mod'))
        delta[k], new_m[k], new_v[k] = res[:3]
        if k != 'w_mod':
            grads[k] = res[3]
    tiny_names = [k for k in names if k not in big_names]
    res = adamw_many("adamw_tiny", [(given[k], grads[k], given['m_' + k], given['v_' + k]) for k in tiny_names])
    for k, (d_, m_, v_) in zip(tiny_names, res):
        delta[k], new_m[k], new_v[k] = d_, m_, v_

    return (loss, grad_x[None], *[grads[k] for k in names], *[delta[k] for k in names],
            *[new_m[k] for k in names], *[new_v[k] for k in names])
```

```python
import functools

import jax
import jax.numpy as jnp
from jax import lax
from jax.experimental import pallas as pl
from jax.experimental.pallas import tpu as pltpu

F32 = jnp.float32
BF = jnp.bfloat16
EPS = 1e-6
HEAD_DIM = 128
GRID_W = 64
ROPE_THETA = 10000.0
LRU_C = 8.0
FFN_RES = 0.5
N_MOD = 9
LOG2_E = 1.4426950408889634
ADAM_LR, ADAM_B1, ADAM_B2, ADAM_EPS, ADAM_WD, ADAM_STEP = 0.001, 0.9, 0.999, 1e-08, 0.01, 10
VMEM_LIMIT = 52 * 1024 * 1024
MESH = pl.DeviceIdType.MESH
ANY = pl.BlockSpec(memory_space=pl.ANY)


def _sds(shape, dt):
    return jax.ShapeDtypeStruct(tuple(shape), dt)


def _pick(n, cands):
    for c in cands:
        if n % c == 0:
            return c
    return n


def _cparams(**kw):
    return pltpu.CompilerParams(vmem_limit_bytes=VMEM_LIMIT, **kw)


def _sig(x):
    return 1.0 / (1.0 + jnp.exp(-x))


def _gelu(x):
    t = jnp.tanh(0.7978845608028654 * (x + 0.044715 * x * x * x))
    return 0.5 * x * (1.0 + t), t


def _gelu_grad(x, t):
    return 0.5 * (1.0 + t) + 0.5 * x * (1.0 - t * t) * 0.7978845608028654 * (1.0 + 3.0 * 0.044715 * x * x)


class Comm:
    def __init__(self, reads, lands, n_sem, start, finish):
        self.reads, self.lands, self.n_sem, self.start, self.finish = list(reads), list(lands), n_sem, start, finish


def merge_comms(comms):
    reads = [r for c in comms for r in c.reads]
    lands = [l for c in comms for l in c.lands]

    def run(which):
        def fn(r, lo, send_sems, recv_sems, off=0):
            ro = lo_ = so = 0
            for c in comms:
                getattr(c, which)(r[ro:ro + len(c.reads)], lo[lo_:lo_ + len(c.lands)], send_sems, recv_sems, off + so)
                ro, lo_, so = ro + len(c.reads), lo_ + len(c.lands), so + c.n_sem
        return fn

    return Comm(reads, lands, sum(c.n_sem for c in comms), run('start'), run('finish'))


def hosted_call(body, *, name, grid, in_specs, out_specs, out_shape, args, scratch_shapes=(), aliases=None, comm=None):
    aliases = dict(aliases or {})
    if comm is None:
        return pl.pallas_call(
            body, name=name, grid=grid, in_specs=list(in_specs), out_specs=list(out_specs), out_shape=list(out_shape),
            scratch_shapes=list(scratch_shapes), input_output_aliases=aliases, compiler_params=_cparams())(*args)
    n_in, n_out, n_sc = len(args), len(out_shape), len(scratch_shapes)
    land_in = [(t, l) for t, l in enumerate(comm.lands) if not isinstance(l, jax.ShapeDtypeStruct)]
    nr, nli, nl = len(comm.reads), len(land_in), len(comm.lands)

    def wrapped(*refs):
        a = refs[:n_in]
        r = refs[n_in:n_in + nr]
        pos = n_in + nr + nli
        o = refs[pos:pos + n_out]
        lo = refs[pos + n_out:pos + n_out + nl]
        sc = refs[pos + n_out + nl:pos + n_out + nl + n_sc]
        send_sems, recv_sems = refs[pos + n_out + nl + n_sc:]
        ids = [pl.program_id(d) for d in range(len(grid))]
        first, last = ids[0] == 0, ids[0] == grid[0] - 1
        for d in range(1, len(grid)):
            first = first & (ids[d] == 0)
            last = last & (ids[d] == grid[d] - 1)

        @pl.when(first)
        def _():
            comm.start(r, lo, send_sems, recv_sems)

        body(*a, *o, *sc)

        @pl.when(last)
        def _():
            comm.finish(r, lo, send_sems, recv_sems)

    for q, (t, _) in enumerate(land_in):
        aliases[n_in + nr + q] = n_out + t
    res = pl.pallas_call(
        wrapped, name=name, grid=grid,
        in_specs=list(in_specs) + [ANY] * (nr + nli), out_specs=list(out_specs) + [ANY] * nl,
        out_shape=list(out_shape) + [l if isinstance(l, jax.ShapeDtypeStruct) else _sds(l.shape, l.dtype) for l in comm.lands],
        scratch_shapes=list(scratch_shapes) + [pltpu.SemaphoreType.DMA((comm.n_sem,)), pltpu.SemaphoreType.DMA((comm.n_sem,))],
        input_output_aliases=aliases, compiler_params=_cparams(),
    )(*args, *comm.reads, *[l for _, l in land_in])
    return list(res[:n_out]), list(res[n_out:])


def comm_call(name, comm):
    def body():
        pass

    return hosted_call(body, name=name, grid=(1,), in_specs=[], out_specs=[], out_shape=[], args=[], comm=comm)[1]


def ew_call(name, grid, fn, ins, outs, first=None, aliases=None, comm=None):
    n_in = len(ins)

    def body(*refs):
        ids = tuple(pl.program_id(a) for a in range(len(grid)))
        vals = fn(ids, *refs[:n_in])
        for (_, _, acc), o_ref, v in zip(outs, refs[n_in:], vals):
            if not acc:
                o_ref[...] = v.astype(o_ref.dtype)
            else:
                is_first = first(ids)

                @pl.when(is_first)
                def _(o_ref=o_ref, v=v):
                    o_ref[...] = v.astype(o_ref.dtype)

                @pl.when(jnp.logical_not(is_first))
                def _(o_ref=o_ref, v=v):
                    o_ref[...] += v.astype(o_ref.dtype)

    return hosted_call(body, name=name, grid=grid, in_specs=[s for _, s in ins], out_specs=[s for _, s, _ in outs],
                       out_shape=[o for o, _, _ in outs], args=[a for a, _ in ins], aliases=aliases, comm=comm)


def fused_mm(name, grid, ins, prods, acc_shapes, epi, outs, extras=(), pre=None, comm=None, row_split=1):
    n_in, n_ex, n_out = len(ins), len(extras), len(outs)
    nk = grid[-1]
    pre = pre or {}
    n_acc = len(acc_shapes)

    def body(*refs):
        in_refs = refs[:n_in]
        ex_refs = refs[n_in:n_in + n_ex]
        out_refs = refs[n_in + n_ex:n_in + n_ex + n_out]
        accs = refs[n_in + n_ex + n_out:]
        ids = tuple(pl.program_id(a) for a in range(len(grid)))
        k = ids[-1]
        loaded = {}

        def operand(i):
            if i not in loaded:
                v = in_refs[i][...]
                if i in pre:
                    v = pre[i](v)
                loaded[i] = v.astype(BF)
            return loaded[i]

        def product(ia, ib, dims):
            return lax.dot_general(operand(ia), operand(ib), (dims, ((), ())), preferred_element_type=F32)

        if nk == 1:
            rows = acc_shapes[0][0]
            step = rows // row_split
            parts = [slice(p * step, (p + 1) * step) for p in range(row_split)]
            all_sums = []
            for rs in parts:
                sums = [None] * n_acc
                for ia, ib, dims, ai in prods:
                    lhs = operand(ia) if row_split == 1 else operand(ia)[rs]
                    d = lax.dot_general(lhs, operand(ib), (dims, ((), ())), preferred_element_type=F32)
                    sums[ai] = d if sums[ai] is None else sums[ai] + d
                all_sums.append(sums)
            for rs, sums in zip(parts, all_sums):
                ex = ex_refs if row_split == 1 else [e.at[rs] if e.shape[0] == rows else e for e in ex_refs]
                for o_ref, v in zip(out_refs, epi(ids, sums, ex)):
                    if row_split == 1:
                        o_ref[...] = v.astype(o_ref.dtype)
                    else:
                        o_ref[rs] = v.astype(o_ref.dtype)
            return

        @pl.when(k == 0)
        def _():
            for a in accs:
                a[...] = jnp.zeros(a.shape, F32)

        for ia, ib, dims, ai in prods:
            accs[ai][...] += product(ia, ib, dims)

        @pl.when(k == nk - 1)
        def _():
            vals = epi(ids, [a[...] for a in accs], ex_refs)
            for o_ref, v in zip(out_refs, vals):
                o_ref[...] = v.astype(o_ref.dtype)

    return hosted_call(
        body, name=name, grid=grid, in_specs=[s for _, s in ins] + [s for _, s in extras],
        out_specs=[s for _, s in outs], out_shape=[o for o, _ in outs],
        scratch_shapes=[pltpu.VMEM(s, F32) for s in acc_shapes] if nk > 1 else [],
        args=[a for a, _ in ins] + [a for a, _ in extras], comm=comm)


NN = ((1,), (0,))
NT = ((1,), (1,))
TN = ((0,), (0,))


class Dims:
    def __init__(self, S, C, D, F4, W4, NS, LB):
        self.S, self.C, self.D, self.F4, self.W4, self.NS, self.LB = S, C, D, F4, W4, NS, LB
        self.T = S + C
        self.DFF = F4 * NS
        self.INW = W4 * NS
        self.NQ = D // HEAD_DIM
        self.KVW = (self.INW - 5 * D) // 2
        self.NKV = self.KVW // HEAD_DIM
        self.G = self.NQ // self.NKV
        self.OFF_K = D
        self.OFF_V = D + self.KVW
        self.OFF_LX = D + 2 * self.KVW
        self.OFF_LG = self.OFF_LX + D
        self.OFF_GA = self.OFF_LG + D
        self.OFF_GL = self.OFF_GA + D
        self.bm = _pick(C, [256, 128, 64, 32, 16, 8])
        self.nCb = C // self.bm
        self.nTb = self.T // self.bm
        self.nSb = S // self.bm
        self.mT = _pick(self.T, [544, 512, 384, 256, 128])
        self.mS = _pick(S, [512, 256, 128])
        self.kT = _pick(self.T, [1088, 1024, 768, 544, 512, 384, 256, 128])
        self.kS = _pick(S, [1024, 512, 256, 128])
        self.cw = _pick(D, [1024, 512, 256, 128]) if (self.OFF_LX % 1024 == 0 and D % 1024 == 0) else _pick(
            self.OFF_LX, [512, 256, 128])
        self.nsub = 2 if (W4 % 256 == 0 and W4 >= 512) else 1
        self.wb = W4 // self.nsub
        self.LBD = D // LB
        self.bq = _pick(C, [256, 128]) if S % _pick(C, [256, 128]) == 0 else 128


def rope_tables(dm):
    rows = dm.S // GRID_W
    row = jnp.repeat(jnp.arange(rows, dtype=F32), GRID_W)
    col = jnp.tile(jnp.arange(GRID_W, dtype=F32), rows)
    axis_dims = HEAD_DIM // 2
    freqs = ROPE_THETA ** (-jnp.arange(0, axis_dims, 2, dtype=F32) / axis_dims)
    ang = jnp.concatenate([row[:, None] * freqs, col[:, None] * freqs], axis=-1)
    cos = jnp.repeat(jnp.cos(ang), 2, axis=-1)
    sin = jnp.repeat(jnp.sin(ang), 2, axis=-1)
    sign = jnp.tile(jnp.array([-1.0, 1.0], F32), HEAD_DIM // 2)
    sin = sin * sign
    cos = jnp.concatenate([jnp.ones((dm.C, HEAD_DIM), F32), cos], axis=0)
    sin = jnp.concatenate([jnp.zeros((dm.C, HEAD_DIM), F32), sin], axis=0)
    return cos, sin


def _pair_swap(y):
    lane = lax.broadcasted_iota(jnp.int32, y.shape, 1)
    nxt = pltpu.roll(y, y.shape[1] - 1, 1)
    prv = pltpu.roll(y, 1, 1)
    return jnp.where((lane & 1) == 0, nxt, prv)


def normmod_fwd(name, dm, x, norm_g3, stage, modv, rows_T):
    D, bm = dm.D, dm.bm
    nb = dm.nTb if rows_T else dm.nSb
    typ = (lambda i: jnp.where(i < dm.nCb, 0, 1)) if rows_T else (lambda i: 1)

    def fn(ids, x_ref, g_ref, sh_ref, sc_ref):
        xv = x_ref[...]
        r = lax.rsqrt(jnp.mean(xv * xv, axis=-1, keepdims=True) + EPS)
        n = xv * r * g_ref[...]
        return [n * (1.0 + sc_ref[...]) + sh_ref[...]]

    return ew_call(
        name, (nb,), fn,
        [(x, pl.BlockSpec((bm, D), lambda i: (i, 0))),
         (norm_g3, pl.BlockSpec((None, 1, D), lambda i: (stage, 0, 0))),
         (modv, pl.BlockSpec((None, None, 1, D), lambda i: (typ(i), 3 * stage, 0, 0))),
         (modv, pl.BlockSpec((None, None, 1, D), lambda i: (typ(i), 3 * stage + 1, 0, 0)))],
        [(_sds(x.shape, BF), pl.BlockSpec((bm, D), lambda i: (i, 0)), False)])[0]


def normmod_concat_fwd(name, dm, ctx, x, norm_g3, modv):
    D, bm, nCb = dm.D, dm.bm, dm.nCb
    typ = lambda i: jnp.where(i < nCb, 0, 1)

    def fn(ids, c_ref, x_ref, g_ref, sh_ref, sc_ref):
        xv = jnp.where(ids[0] < nCb, c_ref[...], x_ref[...])
        r = lax.rsqrt(jnp.mean(xv * xv, axis=-1, keepdims=True) + EPS)
        n = xv * r * g_ref[...]
        return [n * (1.0 + sc_ref[...]) + sh_ref[...], xv]

    row = pl.BlockSpec((bm, D), lambda i: (i, 0))
    return ew_call(
        name, (dm.nTb,), fn,
        [(ctx, pl.BlockSpec((bm, D), lambda i: (jnp.minimum(i, nCb - 1), 0))),
         (x, pl.BlockSpec((bm, D), lambda i: (jnp.maximum(i - nCb, 0), 0))),
         (norm_g3, pl.BlockSpec((None, 1, D), lambda i: (0, 0, 0))),
         (modv, pl.BlockSpec((None, None, 1, D), lambda i: (typ(i), 0, 0, 0))),
         (modv, pl.BlockSpec((None, None, 1, D), lambda i: (typ(i), 1, 0, 0)))],
        [(_sds((dm.T, D), BF), row, False), (_sds((dm.T, D), F32), row, False)])


def normmod_bwd(name, dm, dh, x, dres, norm_g3, stage, modv, rows_T, dres_lat_only, out_lat_only=False, gate=None):
    D, bm = dm.D, dm.bm
    nb = dm.nTb if rows_T else dm.nSb
    nCb = dm.nCb
    typ = (lambda i: jnp.where(i < nCb, 0, 1)) if rows_T else (lambda i: 1)
    if dres_lat_only:
        dres_map = lambda i: (jnp.maximum(i - nCb, 0), 0)
    else:
        dres_map = lambda i: (i, 0)

    def fn(ids, dh_ref, x_ref, dres_ref, g_ref, sc_ref, *gate_refs):
        i = ids[0]
        xv = x_ref[...]
        dhv = dh_ref[...].astype(F32)
        r = lax.rsqrt(jnp.mean(xv * xv, axis=-1, keepdims=True) + EPS)
        xn = xv * r
        g = g_ref[...]
        n = xn * g
        dn = dhv * (1.0 + sc_ref[...])
        dxn = dn * g
        dx = r * (dxn - xn * jnp.mean(dxn * xn, axis=-1, keepdims=True))
        dresv = dres_ref[...]
        if dres_lat_only:
            dresv = jnp.where(i >= nCb, dresv, 0.0)
        dsh = jnp.sum(dhv, axis=0, keepdims=True)
        dsc = jnp.sum(dhv * n, axis=0, keepdims=True)
        dg = jnp.sum(dn * xn, axis=0, keepdims=True)
        dxt = dx + dresv
        res = [dxt, dsh, dsc, dg]
        if gate is not None:
            f_ref, gv_ref = gate_refs
            res += [gate[2] * gv_ref[...] * dxt, jnp.sum(gate[2] * f_ref[...].astype(F32) * dxt, axis=0, keepdims=True)]
        return res

    if rows_T:
        first = lambda ids: (ids[0] == 0) | (ids[0] == nCb)
    else:
        first = lambda ids: ids[0] == 0
    row = pl.BlockSpec((bm, D), lambda i: (i, 0))
    acc = (_sds((2, 1, D), F32), pl.BlockSpec((None, 1, D), lambda i: (typ(i), 0, 0)), True)
    ins = [(dh, row), (x, row), (dres, pl.BlockSpec((bm, D), dres_map)),
           (norm_g3, pl.BlockSpec((None, 1, D), lambda i: (stage, 0, 0))),
           (modv, pl.BlockSpec((None, None, 1, D), lambda i: (typ(i), 3 * stage + 1, 0, 0)))]
    outs = [(_sds((dm.S, D) if out_lat_only else x.shape, F32),
             pl.BlockSpec((bm, D), (lambda i: (jnp.maximum(i - nCb, 0), 0)) if out_lat_only else (lambda i: (i, 0))), False),
            acc, acc, acc]
    if gate is not None:
        ins += [(gate[0], row), (modv, pl.BlockSpec((None, None, 1, D), lambda i: (typ(i), gate[1], 0, 0)))]
        outs += [(_sds(x.shape, BF), row, False), acc]
    return ew_call(name, (nb,), fn, ins, outs, first=first)


def ffn_fwd(name, dm, h, xres, wg, wu, wd, modv, gidx, rows_T, comm_up=None, comm_down=None):
    D, F4, NS = dm.D, dm.F4, dm.NS
    M = h.shape[0]
    bm = dm.mT if rows_T else dm.mS
    C = dm.C

    def epi_up(ids, accs, ex):
        a, u = accs
        return [a, u, a * _sig(a) * u]

    hspec = pl.BlockSpec((bm, D), lambda j, i, k: (i, 0))
    wspec = pl.BlockSpec((None, D, F4), lambda j, i, k: (j, 0, 0))
    ospec = pl.BlockSpec((bm, F4), lambda j, i, k: (i, j))
    res = fused_mm(
        name + "_up", (NS, M // bm, 1), [(h, hspec), (wg, wspec), (wu, wspec)],
        [(0, 1, NN, 0), (0, 2, NN, 1)], [(bm, F4), (bm, F4)], epi_up,
        [(_sds((M, dm.DFF), BF), ospec)] * 3, comm=comm_up)
    (a, u, s), land_up = res if comm_up is not None else (res, None)
    if wd is None:
        wd = land_up[0]

    bn = _pick(D, [1024, 512, 256, 128])

    def epi_dn(ids, accs, ex):
        f = accs[0]
        if rows_T:
            row = ids[0] * bm + lax.broadcasted_iota(jnp.int32, (bm, 1), 0)
            gate = jnp.where(row < C, ex[1][...], ex[2][...])
        else:
            gate = ex[2][...]
        return [ex[0][...] + FFN_RES * gate * f, f]

    gspec = lambda t: pl.BlockSpec((None, None, 1, bn), lambda i, j, k: (t, gidx, 0, j))
    res = fused_mm(
        name + "_down", (M // bm, D // bn, NS // 2),
        [(s, pl.BlockSpec((bm, F4), lambda i, j, k: (i, 2 * k))),
         (wd, pl.BlockSpec((None, F4, bn), lambda i, j, k: (2 * k, 0, j))),
         (s, pl.BlockSpec((bm, F4), lambda i, j, k: (i, 2 * k + 1))),
         (wd, pl.BlockSpec((None, F4, bn), lambda i, j, k: (2 * k + 1, 0, j)))],
        [(0, 1, NN, 0), (2, 3, NN, 0)], [(bm, bn)], epi_dn,
        [(_sds((M, D), F32), pl.BlockSpec((bm, bn), lambda i, j, k: (i, j))),
         (_sds((M, D), BF), pl.BlockSpec((bm, bn), lambda i, j, k: (i, j)))],
        extras=[(xres, pl.BlockSpec((bm, bn), lambda i, j, k: (i, j))), (modv, gspec(0)), (modv, gspec(1))],
        comm=comm_down)
    (xo, f), land_down = res if comm_down is not None else (res, None)
    return xo, a, u, s, f, land_up, land_down


def ffn_bwd(name, dm, df, h, a, u, s, wg, wu, wd, rows_T, comms=None):
    comms = comms or {}
    landed, made = {}, {}

    def run(key, *args, **kw):
        comm = comms[key](landed, made) if key in comms else None
        res = fused_mm(*args, comm=comm, **kw)
        if comm is not None:
            res, landed[key] = res
        return res

    D, F4, NS = dm.D, dm.F4, dm.NS
    M = h.shape[0]
    bm = dm.mT if rows_T else dm.mS
    bkr = dm.kT if rows_T else dm.kS

    def epi_ds(ids, accs, ex):
        ds = accs[0]
        av = ex[0][...].astype(F32)
        uv = ex[1][...].astype(F32)
        sg = _sig(av)
        return [ds * uv * (sg * (1.0 + av * (1.0 - sg))), ds * av * sg]

    ospec = pl.BlockSpec((bm, F4), lambda j, i, k: (i, j))
    da, du = run(
        'ds', name + "_ds", (NS, M // bm, 1),
        [(df, pl.BlockSpec((bm, D), lambda j, i, k: (i, 0))),
         (wd, pl.BlockSpec((None, F4, D), lambda j, i, k: (j, 0, 0)))],
        [(0, 1, NT, 0)], [(bm, F4)], epi_ds, [(_sds((M, dm.DFF), BF), ospec)] * 2,
        extras=[(a, ospec), (u, ospec)], row_split=2)

    ident = lambda ids, accs, ex: list(accs)
    bn = _pick(D, [1024, 512, 256, 128])
    dwg, dwu = run(
        'dwgu', name + "_dwgu", (D // bn, NS, M // bkr),
        [(h, pl.BlockSpec((bkr, bn), lambda i, j, k: (k, i))),
         (da, pl.BlockSpec((bkr, F4), lambda i, j, k: (k, j))),
         (du, pl.BlockSpec((bkr, F4), lambda i, j, k: (k, j)))],
        [(0, 1, TN, 0), (0, 2, TN, 1)], [(bn, F4), (bn, F4)], ident,
        [(_sds((NS, D, F4), BF), pl.BlockSpec((None, bn, F4), lambda i, j, k: (j, i, 0)))] * 2)
    made['dwg'], made['dwu'] = dwg, dwu

    bk2 = 2 * bkr if M % (2 * bkr) == 0 else bkr
    dwd = run(
        'dwd', name + "_dwd", (NS, D // bn, M // bk2),
        [(s, pl.BlockSpec((bk2, F4), lambda i, j, k: (k, i))),
         (df, pl.BlockSpec((bk2, bn), lambda i, j, k: (k, j)))],
        [(0, 1, TN, 0)], [(F4, bn)], ident,
        [(_sds((NS, F4, D), BF), pl.BlockSpec((None, F4, bn), lambda i, j, k: (i, 0, j)))])[0]
    made['dwd'] = dwd

    a_spec = lambda o: pl.BlockSpec((bm, F4), lambda i, j, k: (i, 2 * k + o))
    w_spec = lambda o: pl.BlockSpec((None, bn, F4), lambda i, j, k: (2 * k + o, j, 0))
    dh = run(
        'dh', name + "_dh", (M // bm, D // bn, NS // 2),
        [(da, a_spec(0)), (wg, w_spec(0)), (du, a_spec(0)), (wu, w_spec(0)),
         (da, a_spec(1)), (wg, w_spec(1)), (du, a_spec(1)), (wu, w_spec(1))],
        [(0, 1, NT, 0), (2, 3, NT, 0), (4, 5, NT, 0), (6, 7, NT, 0)], [(bm, bn)], ident,
        [(_sds((M, D), F32), pl.BlockSpec((bm, bn), lambda i, j, k: (i, j)))])[0]
    return dh, dwg, dwu, dwd, landed


def qk_prep(dm, P, gq, gk, cosf, sinf):
    D, KVW, bm = dm.D, dm.KVW, dm.bm

    def head_norm_rope(xh, g, c, s):
        r = lax.rsqrt(jnp.mean(xh * xh, axis=-1, keepdims=True) + EPS)
        y = xh * r * g
        return y * c + _pair_swap(y) * s

    def fn(ids, q_ref, k_ref, v_ref, gq_ref, gk_ref, c_ref, s_ref):
        c, s = c_ref[...], s_ref[...]
        qs = [head_norm_rope(q_ref[:, h * HEAD_DIM:(h + 1) * HEAD_DIM], gq_ref[...], c, s) for h in range(dm.NQ)]
        ks = [head_norm_rope(k_ref[:, h * HEAD_DIM:(h + 1) * HEAD_DIM], gk_ref[...], c, s) for h in range(dm.NKV)]
        return [jnp.concatenate(qs, axis=1), jnp.concatenate(ks, axis=1), v_ref[...]]

    hspec = pl.BlockSpec((bm, HEAD_DIM), lambda i: (i, 0))
    vec = pl.BlockSpec((1, HEAD_DIM), lambda i: (0, 0))
    return ew_call(
        "qk_prep", (dm.nTb,), fn,
        [(P, pl.BlockSpec((bm, D), lambda i: (i, 0))),
         (P, pl.BlockSpec((bm, KVW), lambda i: (i, dm.OFF_K // KVW))),
         (P, pl.BlockSpec((bm, KVW), lambda i: (i, dm.OFF_V // KVW))),
         (gq, vec), (gk, vec), (cosf, hspec), (sinf, hspec)],
        [(_sds((dm.T, D), BF), pl.BlockSpec((bm, D), lambda i: (i, 0)), False),
         (_sds((dm.T, KVW), BF), pl.BlockSpec((bm, KVW), lambda i: (i, 0)), False),
         (_sds((dm.T, KVW), BF), pl.BlockSpec((bm, KVW), lambda i: (i, 0)), False)])


def qk_prep_bwd(dm, dq, dk, dv, P, gq, gk, cosf, sinf, dP):
    D, KVW, bm, nCb = dm.D, dm.KVW, dm.bm, dm.nCb
    W = D + 2 * KVW

    def head_bwd(d, xh, g, c, s):
        dy = d * c - _pair_swap(d) * s
        r = lax.rsqrt(jnp.mean(xh * xh, axis=-1, keepdims=True) + EPS)
        xn = xh * r
        dg = jnp.sum(dy * xn, axis=0, keepdims=True)
        dxn = dy * g
        return r * (dxn - xn * jnp.mean(dxn * xn, axis=-1, keepdims=True)), dg

    def fn(ids, dq_ref, dk_ref, dv_ref, q_ref, k_ref, gq_ref, gk_ref, c_ref, s_ref, dp_any):
        i = ids[0]
        c, s = c_ref[...], s_ref[...]
        lat = i >= nCb
        outs, dgq = [], jnp.zeros((1, HEAD_DIM), F32)
        for h in range(dm.NQ):
            sl = slice(h * HEAD_DIM, (h + 1) * HEAD_DIM)
            d = jnp.where(lat, dq_ref[:, sl], 0.0)
            dx, dg = head_bwd(d, q_ref[:, sl], gq_ref[...], c, s)
            outs.append(dx)
            dgq = dgq + dg
        dgk = jnp.zeros((1, HEAD_DIM), F32)
        for h in range(dm.NKV):
            sl = slice(h * HEAD_DIM, (h + 1) * HEAD_DIM)
            dx, dg = head_bwd(dk_ref[:, sl], k_ref[:, sl], gk_ref[...], c, s)
            outs.append(dx)
            dgk = dgk + dg
        outs.append(dv_ref[...])
        return [jnp.concatenate(outs, axis=1), dgq, dgk]

    hspec = pl.BlockSpec((bm, HEAD_DIM), lambda i: (i, 0))
    vec = pl.BlockSpec((1, HEAD_DIM), lambda i: (0, 0))
    return ew_call(
        "qk_prep_bwd", (dm.nTb,), fn,
        [(dq, pl.BlockSpec((bm, D), lambda i: (jnp.maximum(i - nCb, 0), 0))),
         (dk, pl.BlockSpec((bm, KVW), lambda i: (i, 0))),
         (dv, pl.BlockSpec((bm, KVW), lambda i: (i, 0))),
         (P, pl.BlockSpec((bm, D), lambda i: (i, 0))),
         (P, pl.BlockSpec((bm, KVW), lambda i: (i, dm.OFF_K // KVW))),
         (gq, vec), (gk, vec), (cosf, hspec), (sinf, hspec), (dP, ANY)],
        [(_sds(dP.shape, BF), pl.BlockSpec((bm, W), lambda i: (i, 0)), False),
         (_sds((1, HEAD_DIM), F32), vec, True), (_sds((1, HEAD_DIM), F32), vec, True)],
        first=lambda ids: ids[0] == 0, aliases={9: 0})


def _softmax_numerators(s_ref, eb_ref, mb_ref, scale):
    rows, T = s_ref.shape
    m = jnp.max(s_ref[...], axis=-1, keepdims=True)
    mb_ref[...] = jnp.broadcast_to(m, (rows, HEAD_DIM))
    lacc = jnp.zeros((rows, HEAD_DIM), F32)
    for c in range(T // HEAD_DIM):
        cs = slice(c * HEAD_DIM, (c + 1) * HEAD_DIM)
        e = jnp.exp2((s_ref[:, cs] - mb_ref[...]) * (scale * LOG2_E))
        lacc = lacc + e
        eb_ref[:, cs] = e.astype(BF)
    return jnp.sum(lacc, axis=-1, keepdims=True)


def attention_fwd(dm, qr, kr, vb, comm=None):
    S, T, D, G, nCb = dm.S, dm.T, dm.D, dm.G, dm.nCb
    bq = dm.bq
    off = dm.C // bq
    scale = HEAD_DIM ** -0.5
    GW = G * HEAD_DIM

    def body(q_ref, k_ref, v_ref, o_ref):
        k = k_ref[...]
        v = v_ref[...]
        head = lambda h: slice(h * HEAD_DIM, (h + 1) * HEAD_DIM)
        scores = lambda h: lax.dot_general(q_ref[:, head(h)], k, (NT, ((), ())), preferred_element_type=F32)
        s_next = scores(0)
        for h in range(G):
            s = s_next
            if h + 1 < G:
                s_next = scores(h + 1)
            m = jnp.max(s, axis=-1, keepdims=True)
            p = jnp.exp2((s - m) * (scale * LOG2_E))
            l = jnp.sum(p, axis=-1, keepdims=True)
            o = lax.dot_general(p.astype(BF), v, (NN, ((), ())), preferred_element_type=F32)
            o_ref[:, head(h)] = o / l

    return hosted_call(
        body, grid=(dm.NKV, S // bq), name="attn_fwd",
        in_specs=[pl.BlockSpec((bq, GW), lambda g, i: (i + off, g)),
                  pl.BlockSpec((T, HEAD_DIM), lambda g, i: (0, g)),
                  pl.BlockSpec((T, HEAD_DIM), lambda g, i: (0, g))],
        out_specs=[pl.BlockSpec((bq, GW), lambda g, i: (i, g))],
        out_shape=[_sds((S, D), F32)], args=[qr, kr, vb], comm=comm)


def attention_bwd(dm, qr, kr, vb, attn, dattn, comm=None):
    S, T, D, G = dm.S, dm.T, dm.D, dm.G
    bq = dm.bq
    off = dm.C // bq
    scale = HEAD_DIM ** -0.5
    GW = G * HEAD_DIM

    def body(q_ref, k_ref, v_ref, o_ref, do_ref, dq_ref, dk_ref, dv_ref, s2_ref, dp_ref, eb_ref, tb_ref, mb_ref):
        i = pl.program_id(1)

        @pl.when(i == 0)
        def _():
            dk_ref[...] = jnp.zeros(dk_ref.shape, F32)
            dv_ref[...] = jnp.zeros(dv_ref.shape, F32)

        k = k_ref[...]
        v = v_ref[...]
        head = lambda h: slice(h * HEAD_DIM, (h + 1) * HEAD_DIM)

        def finish(h, w):
            dq_ref[:, head(h)] = lax.dot_general(tb_ref[...], k, (NN, ((), ())), preferred_element_type=F32) * w
            dk_ref[...] += lax.dot_general(tb_ref[...], (q_ref[:, head(h)].astype(F32) * w).astype(BF), (TN, ((), ())),
                                           preferred_element_type=F32)

        s2_ref[0] = lax.dot_general(q_ref[:, head(0)], k, (NT, ((), ())), preferred_element_type=F32)
        w_prev = None
        for h in range(G):
            s_ref = s2_ref.at[h % 2]
            do = do_ref[:, head(h)]
            dof = do.astype(F32)
            if h + 1 < G:
                s2_ref[(h + 1) % 2] = lax.dot_general(q_ref[:, head(h + 1)], k, (NT, ((), ())),
                                                      preferred_element_type=F32)
            if h > 0:
                finish(h - 1, w_prev)
            l = _softmax_numerators(s_ref, eb_ref, mb_ref, scale)
            rl = 1.0 / l
            dp_ref[...] = lax.dot_general(do, v, (NT, ((), ())), preferred_element_type=F32)
            dv_ref[...] += lax.dot_general(eb_ref[...], (dof * rl).astype(BF), (TN, ((), ())), preferred_element_type=F32)
            delta = jnp.sum(dof * o_ref[:, head(h)], axis=-1, keepdims=True)
            mb_ref[...] = jnp.broadcast_to(delta, (bq, HEAD_DIM))
            for c in range(T // HEAD_DIM):
                cs = slice(c * HEAD_DIM, (c + 1) * HEAD_DIM)
                tb_ref[:, cs] = (eb_ref[:, cs].astype(F32) * (dp_ref[:, cs] - mb_ref[...])).astype(BF)
            w_prev = scale * rl
        finish(G - 1, w_prev)

    return hosted_call(
        body, grid=(dm.NKV, S // bq), name="attn_bwd",
        in_specs=[pl.BlockSpec((bq, GW), lambda g, i: (i + off, g)),
                  pl.BlockSpec((T, HEAD_DIM), lambda g, i: (0, g)),
                  pl.BlockSpec((T, HEAD_DIM), lambda g, i: (0, g)),
                  pl.BlockSpec((bq, GW), lambda g, i: (i, g)),
                  pl.BlockSpec((bq, GW), lambda g, i: (i + off, g))],
        out_specs=[pl.BlockSpec((bq, GW), lambda g, i: (i, g)),
                   pl.BlockSpec((T, HEAD_DIM), lambda g, i: (0, g)),
                   pl.BlockSpec((T, HEAD_DIM), lambda g, i: (0, g))],
        out_shape=[_sds((S, D), F32), _sds((T, dm.KVW), F32), _sds((T, dm.KVW), F32)],
        args=[qr, kr, vb, attn, dattn],
        scratch_shapes=[pltpu.VMEM((2, bq, T), F32), pltpu.VMEM((bq, T), F32), pltpu.VMEM((bq, T), BF),
                        pltpu.VMEM((bq, T), BF), pltpu.VMEM((bq, HEAD_DIM), F32)], comm=comm)


def _conv_taps(dm, lx, masks_only=False):
    T, C = dm.T, dm.C
    t = lax.broadcasted_iota(jnp.int32, (T, 1), 0)
    valid = [(t >= 2) & ((t < C) | (t >= C + 2)), (t >= 1) & ((t < C) | (t >= C + 1)), None,
             (t != C - 1) & (t != T - 1)]
    shifts = [2, 1, 0, T - 1]
    taps = []
    for k in range(4):
        if k == 2:
            taps.append(lx)
        else:
            taps.append(jnp.where(valid[k], pltpu.roll(lx, shifts[k], 0), 0.0))
    return taps


def _scan_tiles(dm, chains):
    T, C = dm.T, dm.C
    nT, nC = T // 8, C // 8
    row = lax.broadcasted_iota(jnp.int32, (8, HEAD_DIM), 0)

    def tile_of(i, asc, split):
        if not split:
            return i if asc else nT - 1 - i
        if asc:
            return jnp.where(i < nT - nC, nC + i, i - (nT - nC))
        return jnp.where(i < nC, nC - 1 - i, nT - 1 - (i - nC))

    def step(i, carry, asc, split, a_ref, u_ref, out_ref, mode):
        off = pl.multiple_of(tile_of(i, asc, split) * 8, 8)
        a = a_ref[pl.ds(off, 8), :]
        b = u_ref[pl.ds(off, 8), :]
        if mode == 'lam':
            if asc:
                coef = jnp.where(row == 0, 1.0, pltpu.roll(a, 1, 0))
            else:
                coef = jnp.where(row == 7, 1.0, pltpu.roll(a, 7, 0))
        else:
            coef = a
        A, B = coef, b
        for d in (1, 2, 4):
            if asc:
                ok = row >= d
                A_sh = jnp.where(ok, pltpu.roll(A, d, 0), 1.0)
                B_sh = jnp.where(ok, pltpu.roll(B, d, 0), 0.0)
            else:
                ok = row < 8 - d
                A_sh = jnp.where(ok, pltpu.roll(A, 8 - d, 0), 1.0)
                B_sh = jnp.where(ok, pltpu.roll(B, 8 - d, 0), 0.0)
            B = B + A * B_sh
            A = A * A_sh
        h = A * carry + B
        out_ref[pl.ds(off, 8), :] = h
        last = h[7:8, :] if asc else h[0:1, :]
        if mode == 'lam':
            last = last * (a[7:8, :] if asc else a[0:1, :])
        return jnp.broadcast_to(last, (8, HEAD_DIM))

    def body(i, carries):
        return tuple(step(i, c_, *ch) for c_, ch in zip(carries, chains))

    lax.fori_loop(0, nT, body, tuple(jnp.zeros((8, HEAD_DIM), F32) for _ in chains))


def _lru_gates(xc, wa, ba, wx, bx, sp):
    xb = xc.astype(BF)
    r = _sig(jnp.dot(xb, wa, preferred_element_type=F32) + ba)
    i = _sig(jnp.dot(xb, wx, preferred_element_type=F32) + bx)
    a = jnp.exp(-LRU_C * r * sp)
    m = jnp.sqrt(1.0 - a * a)
    return r, i, a, m


def lru_fwd(dm, P, conv_w, conv_b, wa, ba, wx, bx, sp):
    T, D, LB = dm.T, dm.D, dm.LB
    W = dm.LBD
    R = _pick(T, [272, 256, 128, 64, 8])
    lxb = dm.OFF_LX // W

    def body(lx_ref, cw_ref, cb_ref, wa_ref, ba_ref, wx_ref, bx_ref, sp_ref, hf_ref, hb_ref, xc_ref, a_ref):
        taps = _conv_taps(dm, lx_ref[...])
        xc = cb_ref[...]
        for k in range(4):
            xc = xc + taps[k] * cw_ref[k:k + 1, :]
        xc_ref[...] = xc
        h_refs = (hf_ref, hb_ref)

        def chunk(ci, _):
            off = pl.multiple_of(ci * R, 8)
            x = xc_ref[pl.ds(off, R), :]
            for d in range(2):
                r, i, a, m = _lru_gates(x, wa_ref[d].astype(BF), ba_ref[d], wx_ref[d].astype(BF), bx_ref[d], sp_ref[d])
                a_ref[d, pl.ds(off, R), :] = a
                h_refs[d][pl.ds(off, R), :] = m * i * x
            return 0

        lax.fori_loop(0, T // R, chunk, 0)
        _scan_tiles(dm, [(True, False, a_ref.at[0], hf_ref, hf_ref, 'h'), (False, True, a_ref.at[1], hb_ref, hb_ref, 'h')])

    strip = lambda j: (0, j)
    vec = pl.BlockSpec((2, 1, W), lambda j: (0, 0, j))
    mat = pl.BlockSpec((2, None, W, W), lambda j: (0, j, 0, 0))
    return pl.pallas_call(
        body, grid=(LB,), name="lru_fwd",
        in_specs=[pl.BlockSpec((T, W), lambda j: (0, lxb + j)),
                  pl.BlockSpec((4, W), strip), pl.BlockSpec((1, W), strip), mat, vec, mat, vec, vec],
        out_specs=[pl.BlockSpec((T, W), strip)] * 2, out_shape=[_sds((T, D), F32)] * 2,
        scratch_shapes=[pltpu.VMEM((T, W), F32), pltpu.VMEM((2, T, W), F32)], compiler_params=_cparams(),
    )(P, conv_w, conv_b, wa, ba, wx, bx, sp)


def lru_bwd(dm, P, dh, hf, hb, conv_w, conv_b, wa, ba, wx, bx, sp, sg, dP):
    T, C, D, LB = dm.T, dm.C, dm.D, dm.LB
    W = dm.LBD
    R = _pick(T, [272, 256, 128, 64, 8])
    lxb = dm.OFF_LX // W

    def body(lx_ref, dh_ref, hf_ref, hb_ref, cw_ref, cb_ref, wa_ref, ba_ref, wx_ref, bx_ref, sp_ref, sg_ref, _dp_any,
             dlx_ref, dwa_ref, dba_ref, dwx_ref, dbx_ref, dlam_ref, dcw_ref, dcb_ref,
             xc_ref, a_ref, lam_ref, hp_ref, dxc_ref):
        lx = lx_ref[...]
        taps = _conv_taps(dm, lx)
        xc = cb_ref[...]
        for k in range(4):
            xc = xc + taps[k] * cw_ref[k:k + 1, :]
        xc_ref[...] = xc

        def gates(d, x):
            return _lru_gates(x, wa_ref[d].astype(BF), ba_ref[d], wx_ref[d].astype(BF), bx_ref[d], sp_ref[d])

        def chunk_a(ci, _):
            off = pl.multiple_of(ci * R, 8)
            x = xc_ref[pl.ds(off, R), :]
            for d in range(2):
                a_ref[d, pl.ds(off, R), :] = gates(d, x)[2]
            return 0

        lax.fori_loop(0, T // R, chunk_a, 0)
        _scan_tiles(dm, [(False, False, a_ref.at[0], dh_ref, lam_ref.at[0], 'lam'),
                         (True, True, a_ref.at[1], dh_ref, lam_ref.at[1], 'lam')])
        t = lax.broadcasted_iota(jnp.int32, (T, 1), 0)
        hp_ref[0] = jnp.where(t == 0, 0.0, pltpu.roll(hf_ref[...], 1, 0))
        hv = hb_ref[...]
        hp_ref[1] = jnp.where(t == C - 1, 0.0, jnp.where(t == T - 1, jnp.broadcast_to(hv[0:1, :], hv.shape),
                                                         pltpu.roll(hv, T - 1, 0)))

        def chunk_b(d):
            wa_, wx_ = wa_ref[d].astype(BF), wx_ref[d].astype(BF)

            def run(ci, carry):
                dwa, dwx, dba, dbx, dlam = carry
                off = pl.multiple_of(ci * R, 8)
                x = xc_ref[pl.ds(off, R), :]
                r, i, a, m = gates(d, x)
                lam = lam_ref[d, pl.ds(off, R), :]
                da = lam * hp_ref[d, pl.ds(off, R), :] - lam * (i * x) * a / m
                dloga = da * a
                dza = dloga * (-LRU_C) * sp_ref[d] * r * (1.0 - r)
                dzx = lam * m * x * i * (1.0 - i)
                dzab, dzxb = dza.astype(BF), dzx.astype(BF)
                xb = x.astype(BF)
                dxc = lam * m * i
                dxc = dxc + lax.dot_general(dzab, wa_, (NT, ((), ())), preferred_element_type=F32)
                dxc = dxc + lax.dot_general(dzxb, wx_, (NT, ((), ())), preferred_element_type=F32)
                if d == 0:
                    dxc_ref[pl.ds(off, R), :] = dxc
                else:
                    dxc_ref[pl.ds(off, R), :] += dxc
                dwa = dwa + lax.dot_general(xb, dzab, (TN, ((), ())), preferred_element_type=F32)
                dwx = dwx + lax.dot_general(xb, dzxb, (TN, ((), ())), preferred_element_type=F32)
                dba = dba + jnp.sum(dza, axis=0, keepdims=True)
                dbx = dbx + jnp.sum(dzx, axis=0, keepdims=True)
                dlam = dlam + jnp.sum(dloga * LRU_C * r, axis=0, keepdims=True)
                return dwa, dwx, dba, dbx, dlam

            z = jnp.zeros((W, W), F32)
            zv = jnp.zeros((1, W), F32)
            dwa, dwx, dba, dbx, dlam = lax.fori_loop(0, T // R, run, (z, z, zv, zv, zv))
            dwa_ref[d] = dwa
            dwx_ref[d] = dwx
            dba_ref[d] = dba
            dbx_ref[d] = dbx
            dlam_ref[d] = dlam * sg_ref[d]

        chunk_b(0)
        chunk_b(1)
        dxc = dxc_ref[...]
        dcb_ref[...] = jnp.sum(dxc, axis=0, keepdims=True)
        dcw_ref[...] = jnp.concatenate([jnp.sum(dxc * taps[k], axis=0, keepdims=True) for k in range(4)], axis=0)
        valid = [(t < T - 2) & ((t >= C) | (t < C - 2)), (t < T - 1) & ((t >= C) | (t < C - 1)), None,
                 (t != 0) & (t != C)]
        shifts = [T - 2, T - 1, 0, 1]
        dlx = dxc * cw_ref[2:3, :]
        for k in (0, 1, 3):
            dlx = dlx + jnp.where(valid[k], pltpu.roll(dxc, shifts[k], 0), 0.0) * cw_ref[k:k + 1, :]
        dlx_ref[...] = dlx.astype(dlx_ref.dtype)

    strip = lambda j: (0, j)
    sspec = pl.BlockSpec((T, W), strip)
    vec = pl.BlockSpec((2, 1, W), lambda j: (0, 0, j))
    mat = pl.BlockSpec((2, None, W, W), lambda j: (0, j, 0, 0))
    ovec = pl.BlockSpec((2, 1, W), lambda j: (0, 0, j))
    return pl.pallas_call(
        body, grid=(LB,), name="lru_bwd",
        in_specs=[pl.BlockSpec((T, W), lambda j: (0, lxb + j)), sspec, sspec, sspec,
                  pl.BlockSpec((4, W), strip), pl.BlockSpec((1, W), strip), mat, vec, mat, vec, vec, vec, ANY],
        out_specs=[pl.BlockSpec((T, W), lambda j: (0, lxb + j)), mat, ovec, mat, ovec, ovec,
                   pl.BlockSpec((4, W), strip), pl.BlockSpec((1, W), strip)],
        out_shape=[_sds(dP.shape, BF), _sds((2, LB, W, W), F32), _sds((2, 1, D), F32), _sds((2, LB, W, W), F32),
                   _sds((2, 1, D), F32), _sds((2, 1, D), F32), _sds((4, D), F32), _sds((1, D), F32)],
        scratch_shapes=[pltpu.VMEM((T, W), F32), pltpu.VMEM((2, T, W), F32), pltpu.VMEM((2, T, W), F32),
                        pltpu.VMEM((2, T, W), F32), pltpu.VMEM((T, W), F32)],
        input_output_aliases={12: 0}, compiler_params=_cparams(),
    )(P, dh, hf, hb, conv_w, conv_b, wa, ba, wx, bx, sp, sg, dP)


def merge_fwd(dm, P, attn, hf, hb):
    S, D, bm, cw, nCb = dm.S, dm.D, dm.bm, dm.cw, dm.nCb

    def fn(ids, lg_ref, ga_ref, gl_ref, at_ref, hf_ref, hb_ref):
        ge, _ = _gelu(lg_ref[...])
        lru = (hf_ref[...] + hb_ref[...]) * ge
        return [_sig(ga_ref[...]) * at_ref[...] + _sig(gl_ref[...]) * lru]

    pspec = lambda off: pl.BlockSpec((bm, cw), lambda i, j: (i + nCb, off // cw + j))
    tspec = pl.BlockSpec((bm, cw), lambda i, j: (i + nCb, j))
    sspec = pl.BlockSpec((bm, cw), lambda i, j: (i, j))
    return ew_call(
        "merge_fwd", (dm.nSb, D // cw), fn,
        [(P, pspec(dm.OFF_LG)), (P, pspec(dm.OFF_GA)), (P, pspec(dm.OFF_GL)), (attn, sspec), (hf, tspec), (hb, tspec)],
        [(_sds((S, D), BF), sspec, False)])[0]


def merge_bwd(dm, dmg, P, attn, hf, hb):
    S, T, D, bm, cw, nCb = dm.S, dm.T, dm.D, dm.bm, dm.cw, dm.nCb
    nj = D // cw
    n_steps = dm.nTb * nj

    def body(dm_ref, lg_ref, ga_ref, gl_ref, at_ref, hf_ref, hb_ref, dp_ref, da_ref, dh_ref, buf, sems):
        i, j = pl.program_id(0), pl.program_id(1)
        lat = i >= nCb
        d = jnp.where(lat, dm_ref[...].astype(F32), 0.0)
        lg = lg_ref[...]
        ge, th = _gelu(lg)
        hs = hf_ref[...] + hb_ref[...]
        sa, sl = _sig(ga_ref[...]), _sig(gl_ref[...])
        at = jnp.where(lat, at_ref[...], 0.0)
        dlru = d * sl
        n = i * nj + j
        par = n % 2

        def copies(p):
            out = []
            for g, off in enumerate((dm.OFF_LG, dm.OFF_GA, dm.OFF_GL)):
                col = pl.multiple_of(off + j * cw, 128)
                out.append(pltpu.make_async_copy(
                    buf.at[p, g], dp_ref.at[pl.ds(pl.multiple_of(i * bm, 8), bm), pl.ds(col, cw)], sems.at[p, g]))
            return out

        @pl.when(n >= 2)
        def _():
            for cp in copies(par):
                cp.wait()

        buf[par, 0] = (dlru * hs * _gelu_grad(lg, th)).astype(BF)
        buf[par, 1] = (d * at * sa * (1.0 - sa)).astype(BF)
        buf[par, 2] = (d * hs * ge * sl * (1.0 - sl)).astype(BF)
        da_ref[...] = (d * sa).astype(BF)
        dh_ref[...] = dlru * ge
        for cp in copies(par):
            cp.start()

        @pl.when(n == n_steps - 1)
        def _():
            for cp in copies(par):
                cp.wait()
            if n_steps >= 2:
                for cp in copies(1 - par):
                    cp.wait()

    pspec = lambda off: pl.BlockSpec((bm, cw), lambda i, j: (i, off // cw + j))
    tspec = pl.BlockSpec((bm, cw), lambda i, j: (i, j))
    lspec = pl.BlockSpec((bm, cw), lambda i, j: (jnp.maximum(i - nCb, 0), j))
    return pl.pallas_call(
        body, grid=(dm.nTb, nj), name="merge_bwd",
        in_specs=[lspec, pspec(dm.OFF_LG), pspec(dm.OFF_GA), pspec(dm.OFF_GL), lspec, tspec, tspec],
        out_specs=[ANY, tspec, tspec],
        out_shape=[_sds((T, dm.INW), BF), _sds((T, D), BF), _sds((T, D), F32)],
        scratch_shapes=[pltpu.VMEM((2, 3, bm, cw), BF), pltpu.SemaphoreType.DMA((2, 3))],
        compiler_params=_cparams(),
    )(dmg, P, P, P, attn, hf, hb)


def final_loss(dm, x3, gfin, target, f3, modv):
    S, D, bm = dm.S, dm.D, dm.bm

    def fn(ids, x_ref, g_ref, t_ref, f_ref, gv_ref):
        xv = x_ref[...]
        g = g_ref[...]
        r = lax.rsqrt(jnp.mean(xv * xv, axis=-1, keepdims=True) + EPS)
        xn = xv * r
        err = xn * g - t_ref[...]
        loss = 0.5 * jnp.sum(jnp.mean(err * err, axis=-1, keepdims=True), axis=0, keepdims=True)
        dy = err / D
        dxn = dy * g
        dx = r * (dxn - xn * jnp.mean(dxn * xn, axis=-1, keepdims=True))
        return [jnp.broadcast_to(loss, (1, 128)), dx, jnp.sum(dy * xn, axis=0, keepdims=True),
                FFN_RES * gv_ref[...] * dx, jnp.sum(FFN_RES * f_ref[...].astype(F32) * dx, axis=0, keepdims=True)]

    row = pl.BlockSpec((bm, D), lambda i: (i, 0))
    vec = pl.BlockSpec((1, D), lambda i: (0, 0))
    return ew_call(
        "final_loss", (dm.nSb,), fn,
        [(x3, row), (gfin, vec), (target, row), (f3, row),
         (modv, pl.BlockSpec((None, None, 1, D), lambda i: (1, 8, 0, 0)))],
        [(_sds((1, 128), F32), pl.BlockSpec((1, 128), lambda i: (0, 0)), True), (_sds((S, D), F32), row, False),
         (_sds((1, D), F32), vec, True), (_sds((S, D), BF), row, False),
         (_sds((2, 1, D), F32), pl.BlockSpec((None, 1, D), lambda i: (1, 0, 0)), True)], first=lambda ids: ids[0] == 0)


def local_step(dm, x, ctx, target, modv, norm_g3, gfin, gq, gk, conv_w, conv_b, wa, ba, wx, bx, lam, wbuf, where):
    S, C, T, D, NS, F4, W4 = dm.S, dm.C, dm.T, dm.D, dm.NS, dm.F4, dm.W4
    Ds = D // NS
    wb, nsub = dm.wb, dm.nsub
    cosf, sinf = rope_tables(dm)
    sp = jax.nn.softplus(-lam)
    sg = jax.nn.sigmoid(-lam)
    ident = lambda ids, accs, ex: list(accs)
    mT, mS, kT, kS = dm.mT, dm.mS, dm.kT, dm.kS
    bn = _pick(D, [1024, 512, 256, 128])
    bk = _pick(D, [512, 256, 128])

    wg0, wu0 = comm_call("ag_ffn1", ag_comm([wbuf['wg0'], wbuf['wu0']]))
    h1, xt = normmod_concat_fwd("nm1", dm, ctx, x, norm_g3, modv)
    xt1, a1, u1, s1, f1, land, _ = ffn_fwd("ffn1", dm, h1, xt, wg0, wu0, None, modv, 2, True,
                                           comm_up=ag_comm([wbuf['wd0'], wbuf['w_in']]))
    wd0, w_in = land
    h2 = normmod_fwd("nm2", dm, xt1, norm_g3, 1, modv, True)
    P = fused_mm(
        "w_in", (T // mT, NS * nsub, 1),
        [(h2, pl.BlockSpec((mT, D), lambda i, j, k: (i, 0))),
         (w_in, pl.BlockSpec((None, D, wb), lambda i, j, k: (j // nsub, 0, j % nsub)))],
        [(0, 1, NN, 0)], [(mT, wb)], ident,
        [(_sds((T, dm.INW), F32), pl.BlockSpec((mT, wb), lambda i, j, k: (i, j)))])[0]
    qr, kr, vb = qk_prep(dm, P, gq, gk, cosf, sinf)
    (attn,), (wg1, wu1, wd1, w_out) = attention_fwd(
        dm, qr, kr, vb, comm=ag_comm([wbuf['wg1'], wbuf['wu1'], wbuf['wd1'], wbuf['w_out']]))
    w_out = w_out.reshape(D, D)
    hf, hb = lru_fwd(dm, P, conv_w, conv_b, wa, ba, wx, bx, sp)
    mg = merge_fwd(dm, P, attn, hf, hb)

    def epi_o(ids, accs, ex):
        o = accs[0]
        return [ex[0][...] + ex[1][...] * o, o]

    rb, nCb = dm.bm, dm.nCb
    x2, o2 = fused_mm(
        "w_out", (D // bn, S // rb, 1),
        [(mg, pl.BlockSpec((rb, D), lambda j, i, k: (i, 0))), (w_out, pl.BlockSpec((D, bn), lambda j, i, k: (0, j)))],
        [(0, 1, NN, 0)], [(rb, bn)], epi_o,
        [(_sds((S, D), F32), pl.BlockSpec((rb, bn), lambda j, i, k: (i, j))),
         (_sds((S, D), BF), pl.BlockSpec((rb, bn), lambda j, i, k: (i, j)))],
        extras=[(xt1, pl.BlockSpec((rb, bn), lambda j, i, k: (i + nCb, j))),
                (modv, pl.BlockSpec((None, None, 1, bn), lambda j, i, k: (1, 5, 0, j)))])
    h3 = normmod_fwd("nm3", dm, x2, norm_g3, 2, modv, False)
    x3, a3, u3, s3, f3, _, _ = ffn_fwd("ffn2", dm, h3, x2, wg1, wu1, wd1, modv, 8, False)
    loss, dx3, dgfin, df3, dg3 = final_loss(dm, x3, gfin, target, f3, modv)

    dh3, dwg1, dwu1, dwd1, _ = ffn_bwd("ffn2b", dm, df3, h3, a3, u3, s3, wg1, wu1, wd1, False)
    dx2, dsh3, dsc3, dgn3, do2, dg2 = normmod_bwd("nm3b", dm, dh3, x2, dx3, norm_g3, 2, modv, False, False,
                                                  gate=(o2, 5, 1.0))
    keep = {}

    def host_a(key, comm):
        if key == 'p1':
            (keep['dmg'],), landed = fused_mm(
                "w_out_dx", (S // mS, D // bn, 1),
                [(do2, pl.BlockSpec((mS, D), lambda i, j, k: (i, 0))),
                 (w_out, pl.BlockSpec((bn, D), lambda i, j, k: (j, 0)))],
                [(0, 1, NT, 0)], [(mS, bn)], ident,
                [(_sds((S, D), BF), pl.BlockSpec((mS, bn), lambda i, j, k: (i, j)))], comm=comm)
            return landed
        keep['dqkv'], landed = attention_bwd(dm, qr, kr, vb, attn, keep['dattn'], comm=comm)
        return landed

    gots_a = host_a('p1', rs_p1_comm([dwg1, dwu1, dwd1]))
    dmg = keep['dmg']
    dw_out = fused_mm(
        "w_out_dw", (D // bn, D // bn, S // kS),
        [(mg, pl.BlockSpec((kS, bn), lambda i, j, k: (k, i))), (do2, pl.BlockSpec((kS, bn), lambda i, j, k: (k, j)))],
        [(0, 1, TN, 0)], [(bn, bn)], ident,
        [(_sds((D, D), BF), pl.BlockSpec((bn, bn), lambda i, j, k: (i, j)))])[0]
    dP, dattn, dhs = merge_bwd(dm, dmg, P, attn, hf, hb)
    keep['dattn'] = dattn
    pairs_a = [add_pair("rs_add_" + n_, g_, got_, where)
               for n_, g_, got_ in zip(('wg1', 'wu1', 'wd1'), (dwg1, dwu1, dwd1), gots_a)]
    land_a = host_a('p2', rs_p2_comm([p_[0] for p_ in pairs_a], [p_[1] for p_ in pairs_a]))
    dq, dk, dv = keep['dqkv']
    dP, dwa, dba, dwx, dbx, dlam, dcw, dcb = lru_bwd(dm, P, dhs, hf, hb, conv_w, conv_b, wa, ba, wx, bx, sp, sg, dP)
    dP, dgq, dgk = qk_prep_bwd(dm, dq, dk, dv, P, gq, gk, cosf, sinf, dP)
    g_wg = sum_slots_into("rs_sum_wg1", land_a[0], where, None, (2, D, F4), 1)
    g_wu = sum_slots_into("rs_sum_wu1", land_a[1], where, None, (2, D, F4), 1)
    g_wd = sum_slots_into("rs_sum_wd1", land_a[2], where, None, (2, F4, D), 1)
    LBD = D // dm.LB
    NM = 2 * dm.LB * LBD
    mats = [dwa.reshape(1, NM, LBD), dwx.reshape(1, NM, LBD)]
    (dh2,), landed_x = fused_mm(
        "w_in_dx", (T // mT, D // bn, NS),
        [(dP, pl.BlockSpec((mT, W4), lambda i, j, k: (i, k))),
         (w_in, pl.BlockSpec((None, bn, W4), lambda i, j, k: (k, j, 0)))],
        [(0, 1, NT, 0)], [(mT, bn)], ident,
        [(_sds((T, D), F32), pl.BlockSpec((mT, bn), lambda i, j, k: (i, j)))],
        comm=merge_comms([rs_p3_comm([g_wg, g_wu, g_wd], [(0, 1, D // 2), (1, 1, D // 2), (2, 1, F4 // 2)]),
                          rs_p1_comm(mats)]))
    g_wg, g_wu, g_wd = landed_x[:3]
    pairs_m = [add_pair("rs_add_" + n_, g_, got_, where) for n_, g_, got_ in zip(('lru_wa', 'lru_wx'), mats, landed_x[3:])]
    (dw_in,), land_m = fused_mm(
        "w_in_dw", (D // bn, NS, T // kT),
        [(h2, pl.BlockSpec((kT, bn), lambda i, j, k: (k, i))), (dP, pl.BlockSpec((kT, W4), lambda i, j, k: (k, j)))],
        [(0, 1, TN, 0)], [(bn, W4)], ident,
        [(_sds((NS, D, W4), BF), pl.BlockSpec((None, bn, W4), lambda i, j, k: (j, i, 0)))],
        comm=rs_p2_comm([p_[0] for p_ in pairs_m], [p_[1] for p_ in pairs_m]))
    g_wa = sum_slots_into("rs_sum_lru_wa", land_m[0], where, None, (NM, LBD), None)
    g_wx = sum_slots_into("rs_sum_lru_wx", land_m[1], where, None, (NM, LBD), None)
    dxt1, dsh2, dsc2, dgn2, df1, dg1 = normmod_bwd("nm2b", dm, dh2, xt1, dx2, norm_g3, 1, modv, True, True,
                                                   gate=(f1, 2, FFN_RES))

    tens_b = [dw_in, dw_out.reshape(NS, Ds, D)]

    def host_ds(landed, made):
        return merge_comms([rs_p1_comm(tens_b), rs_p3_comm([g_wa, g_wx], [(0, None, NM // 2), (1, None, NM // 2)])])

    def host_dwgu(landed, made):
        pairs = [add_pair("rs_add_" + n_, g_, got_, where) for n_, g_, got_ in zip(('w_in', 'w_out'), tens_b, landed['ds'][:2])]
        return rs_p2_comm([p_[0] for p_ in pairs], [p_[1] for p_ in pairs])

    def host_dwd(landed, made):
        g_win = sum_slots_into("rs_sum_w_in", landed['dwgu'][0], where, None, (D, W4), None)
        g_wout = sum_slots_into("rs_sum_w_out", landed['dwgu'][1], where, None, (Ds, D), None)
        return merge_comms([rs_p3_comm([g_win, g_wout], [(0, None, D // 2), (1, None, Ds // 2)]),
                            rs_p1_comm([made['dwg'], made['dwu']])])

    def host_dh(landed, made):
        pairs = [add_pair("rs_add_" + n_, g_, got_, where)
                 for n_, g_, got_ in zip(('wg0', 'wu0'), (made['dwg'], made['dwu']), landed['dwd'][2:])]
        return merge_comms([rs_p2_comm([p_[0] for p_ in pairs], [p_[1] for p_ in pairs]), rs_p1_comm([made['dwd']])])

    dh1, dwg0, dwu0, dwd0, landed = ffn_bwd(
        "ffn1b", dm, df1, h1, a1, u1, s1, wg0, wu0, wd0, True,
        comms={'ds': host_ds, 'dwgu': host_dwgu, 'dwd': host_dwd, 'dh': host_dh})
    g_wa, g_wx = landed['ds'][2:]
    g_win, g_wout = landed['dwd'][:2]
    g_wg = sum_slots_into("rs_sum_wg0", landed['dh'][0], where, g_wg, (2, D, F4), 0)
    g_wu = sum_slots_into("rs_sum_wu0", landed['dh'][1], where, g_wu, (2, D, F4), 0)
    late = add_pair("rs_add_wd0", dwd0, landed['dh'][2], where)
    grad_x, dsh1, dsc1, dgn1 = normmod_bwd("nm1b", dm, dh1, xt, dxt1, norm_g3, 0, modv, True, False, out_lat_only=True)

    dmod = jnp.concatenate([dsh1, dsc1, dg1, dsh2, dsc2, _lat(dg2), _lat(dsh3), _lat(dsc3), _lat(dg3)], axis=1)
    dnorm = jnp.stack([dgn1[0, 0] + dgn1[1, 0], dgn2[0, 0] + dgn2[1, 0], dgn3[1, 0]], axis=0)
    small = dict(norm_g=dnorm, q_norm_g=dgq, k_norm_g=dgk, conv_w=dcw, conv_b=dcb,
                 lru_ba=dba.reshape(2, D), lru_bx=dbx.reshape(2, D), lru_lambda=dlam.reshape(2, D), final_norm_g=dgfin)
    reduced = dict(ffn_wg=g_wg, ffn_wu=g_wu, ffn_wd=g_wd, w_in=g_win, w_out=g_wout, lru_wa=g_wa, lru_wx=g_wx)
    return loss, grad_x, dmod, small, reduced, late


def _lat(v):
    return jnp.concatenate([jnp.zeros_like(v[:1]), v[1:]], axis=0)


def _me():
    return lax.axis_index("x"), lax.axis_index("y"), lax.axis_index("c")


def allgather8(name, v):
    def body(v_ref, out_ref, send_sems, recv_sems, local_sem):
        x, y, c = _me()
        me = 4 * x + 2 * y + c
        mine = pltpu.make_async_copy(v_ref, out_ref.at[me], local_sem)
        mine.start()
        copies = []
        for k in range(1, 8):
            peer = (x ^ ((k >> 2) & 1), y ^ ((k >> 1) & 1), c ^ (k & 1))
            cp = pltpu.make_async_remote_copy(src_ref=v_ref, dst_ref=out_ref.at[me], send_sem=send_sems.at[k - 1],
                                              recv_sem=recv_sems.at[k - 1], device_id=peer, device_id_type=MESH)
            cp.start()
            copies.append(cp)
        for k in range(1, 8):
            peer = (x ^ ((k >> 2) & 1), y ^ ((k >> 1) & 1), c ^ (k & 1))
            pltpu.make_async_remote_copy(src_ref=v_ref, dst_ref=out_ref.at[me ^ k], send_sem=send_sems.at[k - 1],
                                         recv_sem=recv_sems.at[k - 1], device_id=peer, device_id_type=MESH).wait_recv()
        for cp in copies:
            cp.wait_send()
        mine.wait()

    return pl.pallas_call(
        body, name=name, out_shape=_sds((8,) + v.shape, v.dtype), in_specs=[ANY], out_specs=ANY,
        scratch_shapes=[pltpu.SemaphoreType.DMA((7,)), pltpu.SemaphoreType.DMA((7,)), pltpu.SemaphoreType.DMA],
    )(v)


def _chips(x, y):
    chips = [(1 - x, y), (x, 1 - y), (1 - x, 1 - y)]
    return chips, [2 * cx + cy for cx, cy in chips]


def ag_comm(bufs):
    n = len(bufs)

    def parts(outs):
        x, y, c = _me()
        chips, slots = _chips(x, y)
        return x, y, c, 2 * x + y, (x, y, 1 - c), chips, slots

    def ici(outs, t, j, send_sems, recv_sems, src_slot, off):
        x, y, c, s, sib, chips, slots = parts(outs)
        H = outs[t].shape[1] // 2
        blk = outs[t].at[src_slot, pl.ds(c * H, H)]
        return pltpu.make_async_remote_copy(
            src_ref=blk, dst_ref=blk, send_sem=send_sems.at[off + 6 * t + j], recv_sem=recv_sems.at[off + 6 * t + j],
            device_id=(chips[j][0], chips[j][1], c), device_id_type=MESH)

    def d2d(outs, t, j, send_sems, recv_sems, half, off):
        x, y, c, s, sib, chips, slots = parts(outs)
        H = outs[t].shape[1] // 2
        blk = outs[t].at[slots[j], pl.ds(half * H, H)]
        return pltpu.make_async_remote_copy(
            src_ref=blk, dst_ref=blk, send_sem=send_sems.at[off + 6 * t + 3 + j],
            recv_sem=recv_sems.at[off + 6 * t + 3 + j], device_id=sib, device_id_type=MESH)

    def start(reads, outs, send_sems, recv_sems, off=0):
        x, y, c, s, sib, chips, slots = parts(outs)
        for t in range(n):
            for j in range(3):
                ici(outs, t, j, send_sems, recv_sems, s, off).start()

    def finish(reads, outs, send_sems, recv_sems, off=0):
        x, y, c, s, sib, chips, slots = parts(outs)
        for t in range(n):
            for j in range(3):
                ici(outs, t, j, send_sems, recv_sems, slots[j], off).wait_recv()
                d2d(outs, t, j, send_sems, recv_sems, c, off).start()
        for t in range(n):
            for j in range(3):
                d2d(outs, t, j, send_sems, recv_sems, 1 - c, off).wait_recv()
        for t in range(n):
            for j in range(3):
                ici(outs, t, j, send_sems, recv_sems, s, off).wait_send()
                d2d(outs, t, j, send_sems, recv_sems, c, off).wait_send()

    return Comm([], bufs, 6 * n, start, finish)


def rs_p1_comm(tensors):
    n = len(tensors)

    def copy(ins, gots, t, send_sems, recv_sems, off):
        x, y, c = _me()
        H = ins[t].shape[1] // 2
        return pltpu.make_async_remote_copy(
            src_ref=ins[t].at[:, pl.ds((1 - c) * H, H)], dst_ref=gots[t], send_sem=send_sems.at[off + t],
            recv_sem=recv_sems.at[off + t], device_id=(x, y, 1 - c), device_id_type=MESH)

    def start(ins, gots, send_sems, recv_sems, off=0):
        for t in range(n):
            copy(ins, gots, t, send_sems, recv_sems, off).start()

    def finish(ins, gots, send_sems, recv_sems, off=0):
        for t in range(n):
            copy(ins, gots, t, send_sems, recv_sems, off).wait_recv()
        for t in range(n):
            copy(ins, gots, t, send_sems, recv_sems, off).wait_send()

    half = lambda t: _sds((t.shape[0], t.shape[1] // 2) + t.shape[2:], t.dtype)
    return Comm(tensors, [half(t) for t in tensors], n, start, finish)


def rs_p2_comm(partials, landeds):
    n = len(partials)

    def start(ins, outs, send_sems, recv_sems, off=0):
        x, y, c = _me()
        s = 2 * x + y
        chips, slots = _chips(x, y)
        for t in range(n):
            for j, chip in enumerate(chips):
                src = ins[t].at[slots[j]] if ins[t].shape[0] == 4 else ins[t].at[0]
                pltpu.make_async_remote_copy(
                    src_ref=src, dst_ref=outs[t].at[s], send_sem=send_sems.at[off + 3 * t + j],
                    recv_sem=recv_sems.at[off + 3 * t + j], device_id=(chip[0], chip[1], c), device_id_type=MESH).start()

    def finish(ins, outs, send_sems, recv_sems, off=0):
        x, y, c = _me()
        s = 2 * x + y
        chips, slots = _chips(x, y)
        for t in range(n):
            for j, chip in enumerate(chips):
                dst = outs[t].at[slots[j]]
                pltpu.make_async_remote_copy(
                    src_ref=dst, dst_ref=dst, send_sem=send_sems.at[off + 3 * t + j],
                    recv_sem=recv_sems.at[off + 3 * t + j], device_id=(chip[0], chip[1], c), device_id_type=MESH).wait_recv()
        for t in range(n):
            for j, chip in enumerate(chips):
                src = ins[t].at[slots[j]] if ins[t].shape[0] == 4 else ins[t].at[0]
                pltpu.make_async_remote_copy(
                    src_ref=src, dst_ref=outs[t].at[s], send_sem=send_sems.at[off + 3 * t + j],
                    recv_sem=recv_sems.at[off + 3 * t + j], device_id=(chip[0], chip[1], c), device_id_type=MESH).wait_send()

    return Comm(partials, landeds, 3 * n, start, finish)


def rs_p3_comm(greds, plan):
    n = len(plan)

    def copy(outs, t, send_sems, recv_sems, half, off):
        x, y, c = _me()
        oi, li, H = plan[t]
        dst = outs[oi] if li is None else outs[oi].at[li]
        blk = dst.at[pl.ds((c if half == 0 else 1 - c) * H, H)]
        return pltpu.make_async_remote_copy(
            src_ref=blk, dst_ref=blk, send_sem=send_sems.at[off + t], recv_sem=recv_sems.at[off + t],
            device_id=(x, y, 1 - c), device_id_type=MESH)

    def start(reads, outs, send_sems, recv_sems, off=0):
        for t in range(n):
            copy(outs, t, send_sems, recv_sems, 0, off).start()

    def finish(reads, outs, send_sems, recv_sems, off=0):
        for t in range(n):
            copy(outs, t, send_sems, recv_sems, 1, off).wait_recv()
        for t in range(n):
            copy(outs, t, send_sems, recv_sems, 0, off).wait_send()

    return Comm([], greds, n, start, finish)


def _rows_block(rows, cols, nbytes=1 << 20):
    unit = 16 if rows % 16 == 0 else 8
    best = unit
    for bm in range(unit, rows + 1, unit):
        if rows % bm == 0 and bm * cols * 4 <= nbytes:
            best = bm
    return best


def cast_into_slot(name, w, where, layer=None):
    rows, W = w.shape[-2:]
    bm = _rows_block(rows, W, 4 << 20)

    def body(p_ref, w_ref, o_ref):
        o_ref[...] = w_ref[...].astype(BF)

    if layer is None:
        ispec = pl.BlockSpec((bm, W), lambda i, p: (i, 0))
    else:
        ispec = pl.BlockSpec((None, bm, W), lambda i, p: (layer, i, 0))
    return pl.pallas_call(
        body, name=name, out_shape=_sds((4, rows, W), BF), compiler_params=_cparams(),
        grid_spec=pltpu.PrefetchScalarGridSpec(
            num_scalar_prefetch=1, grid=(rows // bm,), in_specs=[ispec],
            out_specs=pl.BlockSpec((None, bm, W), lambda i, p: (p[1], i, 0))),
    )(where, w)


def add_pair(name, g, got, where):
    K, R, W = g.shape
    H = R // 2
    bm = _rows_block(H, W, 4 << 20)
    nh = H // bm

    def body(p_ref, g_ref, got_ref, part_ref, land_ref):
        k = pl.program_id(1)
        v = (g_ref[...].astype(F32) + got_ref[...].astype(F32)).astype(part_ref.dtype)
        part_ref[...] = v
        own = (k == p_ref[1]) if K == 4 else (k == 0)

        @pl.when(own)
        def _():
            land_ref[...] = v

    return pl.pallas_call(
        body, name=name, out_shape=[_sds((K, H, W), g.dtype), _sds((4, H, W), g.dtype)], compiler_params=_cparams(),
        grid_spec=pltpu.PrefetchScalarGridSpec(
            num_scalar_prefetch=1, grid=(nh, K),
            in_specs=[pl.BlockSpec((None, bm, W), lambda i, k, p: (k, p[0] * nh + i, 0)),
                      pl.BlockSpec((None, bm, W), lambda i, k, p: (k, i, 0))],
            out_specs=[pl.BlockSpec((None, bm, W), lambda i, k, p: (k, i, 0)),
                       pl.BlockSpec((None, bm, W), lambda i, k, p: (p[1], i, 0))]),
    )(where, g, got)


def sum_slots_into(name, landed, where, dest, dest_shape, li):
    K, H, W = landed.shape
    bm = _rows_block(H, 2 * W, 4 << 20)
    nh = H // bm

    def body(*refs):
        r, o_ref = refs[1], refs[-1]
        acc = r[0].astype(F32)
        for k in range(1, K):
            acc = acc + r[k].astype(F32)
        o_ref[...] = acc

    if li is None:
        ospec = pl.BlockSpec((bm, W), lambda i, p: (p[0] * nh + i, 0))
    else:
        ospec = pl.BlockSpec((None, bm, W), lambda i, p: (li, p[0] * nh + i, 0))
    in_specs = [pl.BlockSpec((K, bm, W), lambda i, p: (0, i, 0))]
    args = [where, landed]
    aliases = {}
    if dest is not None:
        in_specs.append(ANY)
        args.append(dest)
        aliases = {2: 0}
    return pl.pallas_call(
        body, name=name, out_shape=_sds(dest_shape, F32), compiler_params=_cparams(), input_output_aliases=aliases,
        grid_spec=pltpu.PrefetchScalarGridSpec(num_scalar_prefetch=1, grid=(nh,), in_specs=in_specs, out_specs=ospec),
    )(*args)


def sum_slots(name, a):
    K, H, W = a.shape
    bm = _rows_block(H, W * K // 2)

    def fn(ids, r):
        acc = r[0]
        for k in range(1, K):
            acc = acc + r[k]
        return [acc]

    return ew_call(name, (H // bm,), fn, [(a, pl.BlockSpec((K, bm, W), lambda i: (0, i, 0)))],
                   [(_sds((H, W), F32), pl.BlockSpec((bm, W), lambda i: (i, 0)), False)])[0]


def _adamw_math(w, g, m, v):
    bc1 = 1.0 - ADAM_B1 ** ADAM_STEP
    bc2 = 1.0 - ADAM_B2 ** ADAM_STEP
    mn = ADAM_B1 * m + (1.0 - ADAM_B1) * g
    vn = ADAM_B2 * v + (1.0 - ADAM_B2) * (g * g)
    m_hat = mn / bc1
    v_hat = vn / bc2
    delta = -ADAM_LR * (m_hat / (jnp.sqrt(v_hat) + ADAM_EPS) + ADAM_WD * w)
    return delta, mn, vn


def adamw(name, w, g, m, v, copy_grad=False):
    shape = w.shape
    flat = lambda t: t.reshape(-1, shape[-1])
    w2, g2, m2, v2 = flat(w), flat(g), flat(m), flat(v)
    bm = _rows_block(w2.shape[0], w2.shape[1])

    def fn(ids, w_ref, g_ref, m_ref, v_ref):
        gv = g_ref[...]
        return list(_adamw_math(w_ref[...], gv, m_ref[...], v_ref[...])) + ([gv] if copy_grad else [])

    spec = pl.BlockSpec((bm, w2.shape[1]), lambda i: (i, 0))
    res = ew_call(name, (w2.shape[0] // bm,), fn, [(w2, spec), (g2, spec), (m2, spec), (v2, spec)],
                  [(_sds(w2.shape, F32), spec, False)] * (4 if copy_grad else 3))
    return [o.reshape(shape) for o in res]


def adamw_many(name, params):
    n = len(params)
    shapes = [p_[0].shape for p_ in params]
    two_d = lambda t: t.reshape(-1, t.shape[-1])
    args = [two_d(t) for p_ in params for t in p_]

    def body(*refs):
        ins, outs = refs[:4 * n], refs[4 * n:]
        for q in range(n):
            w_ref, g_ref, m_ref, v_ref = ins[4 * q:4 * q + 4]
            for o_ref, val in zip(outs[3 * q:3 * q + 3], _adamw_math(w_ref[...], g_ref[...], m_ref[...], v_ref[...])):
                o_ref[...] = val

    full = lambda a: pl.BlockSpec(a.shape, lambda i: (0, 0))
    out_shape = [_sds(args[4 * q].shape, F32) for q in range(n) for _ in range(3)]
    res = hosted_call(body, name=name, grid=(1,), in_specs=[full(a) for a in args],
                      out_specs=[full(o) for o in out_shape], out_shape=out_shape, args=args)
    return [tuple(res[3 * q + r].reshape(shapes[q]) for r in range(3)) for q in range(n)]


def dmod_pack(gd):
    N = gd.shape[-1]
    bn = _pick(N, [4608, 2304, 1152, 1024, 512, 256, 128])

    def fn(ids, r):
        lat = [r[d, 1:2, :] for d in range(8)]
        cs = r[0, 0:1, :]
        for d in range(1, 8):
            cs = cs + r[d, 0:1, :]
        tot = cs
        for d in range(8):
            tot = tot + lat[d]
        return [jnp.concatenate(lat + [cs, jnp.zeros((7, bn), F32)], axis=0), tot]

    return ew_call("dmod_pack", (N // bn,), fn, [(gd, pl.BlockSpec((8, 2, bn), lambda j: (0, 0, j)))],
                   [(_sds((16, N), F32), pl.BlockSpec((16, bn), lambda j: (0, j)), False),
                    (_sds((1, N), F32), pl.BlockSpec((1, bn), lambda j: (0, j)), False)])


def _silu(v):
    return v * _sig(v)


def kernel(x, c, ctx, c_ctx, w_mod, b_mod, norm_g, ffn_wg, ffn_wu, ffn_wd, w_in, w_out, q_norm_g, k_norm_g, conv_w, conv_b, lru_wa, lru_ba, lru_wx, lru_bx, lru_lambda, final_norm_g, loss_target, m_c_ctx, m_w_mod, m_b_mod, m_norm_g, m_ffn_wg, m_ffn_wu, m_ffn_wd, m_w_in, m_w_out, m_q_norm_g, m_k_norm_g, m_conv_w, m_conv_b, m_lru_wa, m_lru_ba, m_lru_wx, m_lru_bx, m_lru_lambda, m_final_norm_g, v_c_ctx, v_w_mod, v_b_mod, v_norm_g, v_ffn_wg, v_ffn_wu, v_ffn_wd, v_w_in, v_w_out, v_q_norm_g, v_k_norm_g, v_conv_w, v_conv_b, v_lru_wa, v_lru_ba, v_lru_wx, v_lru_bx, v_lru_lambda, v_final_norm_g):
    given = dict(locals())
    names = ['c_ctx', 'w_mod', 'b_mod', 'norm_g', 'ffn_wg', 'ffn_wu', 'ffn_wd', 'w_in', 'w_out', 'q_norm_g', 'k_norm_g',
             'conv_w', 'conv_b', 'lru_wa', 'lru_ba', 'lru_wx', 'lru_bx', 'lru_lambda', 'final_norm_g']
    S, D = x.shape[1], x.shape[2]
    C = ctx.shape[1]
    NS = 4
    F4, W4, LB = ffn_wg.shape[-1], w_in.shape[-1], lru_wa.shape[2]
    dm = Dims(S, C, D, F4, W4, NS, LB)
    Ds = D // NS
    Wm = w_mod.shape[-1]
    xi, yi, ci = lax.axis_index("x"), lax.axis_index("y"), lax.axis_index("c")
    slot = 2 * xi + yi
    me = 4 * xi + 2 * yi + ci
    ident = lambda ids, accs, ex: list(accs)

    pack1 = jnp.concatenate([c.reshape(-1), norm_g.reshape(-1), conv_w.reshape(-1), lru_ba.reshape(-1),
                             lru_bx.reshape(-1), lru_lambda.reshape(-1)]).reshape(1, -1)
    g1 = allgather8("ag_small_params", pack1)[:, 0]
    c_all = g1[:, :D]

    def unshard(off, k):
        part = g1[0::2, off:off + k * Ds].reshape(NS, k, Ds)
        return jnp.transpose(part, (1, 0, 2)).reshape(k, D)

    norm_g_f = unshard(D, 3)
    conv_w_f = unshard(D + 3 * Ds, 4)
    ba_f = unshard(D + 7 * Ds, 2)
    bx_f = unshard(D + 9 * Ds, 2)
    lam_f = unshard(D + 11 * Ds, 2)

    call16 = jnp.concatenate([c_all, c_ctx.reshape(1, D), jnp.zeros((7, D), F32)], axis=0)
    b_cols = lax.dynamic_slice(b_mod, (0, slot * Wm), (1, Wm))
    bnm = _pick(Wm, [1536, 1152, 768, 512, 384, 256, 128])
    bkm = _pick(D, [512, 256, 128])
    modp = fused_mm(
        "mod_fwd", (1, Wm // bnm, D // bkm),
        [(call16, pl.BlockSpec((16, bkm), lambda i, j, k: (0, k))),
         (w_mod[0], pl.BlockSpec((bkm, bnm), lambda i, j, k: (k, j)))],
        [(0, 1, NN, 0)], [(16, bnm)], lambda ids, accs, ex: [accs[0] + ex[0][...]],
        [(_sds((16, Wm), F32), pl.BlockSpec((16, bnm), lambda i, j, k: (0, j)))],
        extras=[(b_cols, pl.BlockSpec((1, bnm), lambda i, j, k: (0, j)))], pre={0: _silu})[0]
    gm = allgather8("ag_mod", modp)
    mod_full = jnp.concatenate([gm[0], gm[2], gm[4], gm[6]], axis=1)
    mod_x = lax.dynamic_index_in_dim(mod_full, me, axis=0, keepdims=False)
    modv = jnp.stack([mod_full[8], mod_x]).reshape(2, N_MOD, 1, D)

    where = jnp.stack([ci, slot]).astype(jnp.int32)
    wbuf = {}
    for key, short in (('ffn_wg', 'wg'), ('ffn_wu', 'wu'), ('ffn_wd', 'wd')):
        for l in range(2):
            wbuf[short + str(l)] = cast_into_slot("cast_%s%d" % (short, l), given[key][0], where, l)
    wbuf['w_in'] = cast_into_slot("cast_w_in", w_in[0], where)
    wbuf['w_out'] = cast_into_slot("cast_w_out", w_out[0], where)

    loss_l, grad_x, dmod, small, reduced, late = local_step(
        dm, x[0], ctx[0], loss_target[0], modv, norm_g_f.reshape(3, 1, D), final_norm_g.reshape(1, D),
        q_norm_g, k_norm_g, conv_w_f, conv_b, lru_wa[0], ba_f.reshape(2, 1, D), lru_wx[0], bx_f.reshape(2, 1, D),
        lam_f.reshape(2, 1, D), wbuf, where)
    loss = lax.psum(loss_l[0, 0], ("x", "y", "c"))

    grads = {}
    gd = allgather8("ag_dmod", dmod.reshape(2, N_MOD * D))
    dM, g_bmod = dmod_pack(gd)
    dMc = lax.dynamic_slice(dM, (0, slot * Wm), (16, Wm))
    bmm = _pick(D, [512, 256, 128])
    grads['w_mod'] = fused_mm(
        "w_mod_dw", (D // bmm, Wm // bnm, 1),
        [(call16, pl.BlockSpec((16, bmm), lambda i, j, k: (0, i))), (dMc, pl.BlockSpec((16, bnm), lambda i, j, k: (0, j)))],
        [(0, 1, TN, 0)], [(bmm, bnm)], ident,
        [(_sds((D, Wm), F32), pl.BlockSpec((bmm, bnm), lambda i, j, k: (i, j)))], pre={0: _silu})[0][None]
    grads['b_mod'] = g_bmod

    def epi_cc(ids, accs, ex):
        v = ex[0][...]
        sg = _sig(v)
        return [accs[0] * (sg * (1.0 + v * (1.0 - sg)))]

    pcc = fused_mm(
        "c_ctx_partial", (1, D // bmm, Wm // bnm),
        [(dMc, pl.BlockSpec((16, bnm), lambda i, j, k: (0, k))), (w_mod[0], pl.BlockSpec((bmm, bnm), lambda i, j, k: (j, k)))],
        [(0, 1, NT, 0)], [(16, bmm)], epi_cc,
        [(_sds((16, D), F32), pl.BlockSpec((16, bmm), lambda i, j, k: (0, j)))],
        extras=[(c_ctx.reshape(1, D), pl.BlockSpec((1, bmm), lambda i, j, k: (0, j)))])[0]
    pcc_row = jnp.where(ci == 0, pcc[8], 0.0)

    order = ['q_norm_g', 'k_norm_g', 'conv_b', 'final_norm_g', 'norm_g', 'conv_w', 'lru_ba', 'lru_bx', 'lru_lambda']
    flat = [small[k].reshape(-1) for k in order] + [pcc_row]
    sizes = [f.shape[0] for f in flat]
    tot = sum(sizes)
    LW = 1024
    padded = -(-tot // (8 * LW)) * (8 * LW)
    tiny = jnp.concatenate(flat + [jnp.zeros((padded - tot,), F32)]).reshape(-1, LW)
    summed = sum_slots("tiny_sum", allgather8("ag_tiny_grads", tiny)).reshape(-1)
    offs = {}
    o = 0
    for k, n_ in zip(order + ['c_ctx'], sizes):
        offs[k] = summed[o:o + n_]
        o += n_
    shard = lambda k, rows: lax.dynamic_slice_in_dim(offs[k].reshape(rows, D), slot * Ds, Ds, axis=1)
    grads['c_ctx'] = offs['c_ctx']
    grads['q_norm_g'] = offs['q_norm_g'].reshape(1, HEAD_DIM)
    grads['k_norm_g'] = offs['k_norm_g'].reshape(1, HEAD_DIM)
    grads['conv_b'] = offs['conv_b'].reshape(1, D)
    grads['final_norm_g'] = offs['final_norm_g']
    grads['norm_g'] = shard('norm_g', 3)[None]
    grads['conv_w'] = shard('conv_w', 4)[None]
    grads['lru_ba'] = shard('lru_ba', 2)[None]
    grads['lru_bx'] = shard('lru_bx', 2)[None]
    grads['lru_lambda'] = shard('lru_lambda', 2)[None]

    landed = comm_call("rs_tail_p2", rs_p2_comm([late[0]], [late[1]]))
    g_wd = sum_slots_into("rs_sum_wd0", landed[0], where, reduced['ffn_wd'], (2, F4, D), 0)
    g_wg, g_wu, g_wd = comm_call("rs_tail_p3", rs_p3_comm(
        [reduced['ffn_wg'], reduced['ffn_wu'], g_wd], [(0, 0, D // 2), (1, 0, D // 2), (2, 0, F4 // 2)]))
    grads.update(ffn_wg=g_wg[None], ffn_wu=g_wu[None], ffn_wd=g_wd[None], w_in=reduced['w_in'][None],
                 w_out=reduced['w_out'][None], lru_wa=reduced['lru_wa'].reshape(lru_wa.shape),
                 lru_wx=reduced['lru_wx'].reshape(lru_wx.shape))

    delta, new_m, new_v = {}, {}, {}
    big_names = ['w_mod', 'w_in', 'w_out', 'lru_wa', 'lru_wx', 'ffn_wg', 'ffn_wu', 'ffn_wd']
    for k in big_names:
        res = adamw("adamw_" + k, given[k], grads[k], given['m_' + k], given['v_' + k], copy_grad=(k != 'w_mod'))
        delta[k], new_m[k], new_v[k] = res[:3]
        if k != 'w_mod':
            grads[k] = res[3]
    tiny_names = [k for k in names if k not in big_names]
    res = adamw_many("adamw_tiny", [(given[k], grads[k], given['m_' + k], given['v_' + k]) for k in tiny_names])
    for k, (d_, m_, v_) in zip(tiny_names, res):
        delta[k], new_m[k], new_v[k] = d_, m_, v_

    return (loss, grad_x[None], *[grads[k] for k in names], *[delta[k] for k in names],
            *[new_m[k] for k in names], *[new_v[k] for k in names])
```

```python
import functools

import jax
import jax.numpy as jnp
from jax import lax
from jax.experimental import pallas as pl
from jax.experimental.pallas import tpu as pltpu

F32 = jnp.float32
BF = jnp.bfloat16
EPS = 1e-6
HEAD_DIM = 128
GRID_W = 64
ROPE_THETA = 10000.0
LRU_C = 8.0
FFN_RES = 0.5
N_MOD = 9
LOG2_E = 1.4426950408889634
ADAM_LR, ADAM_B1, ADAM_B2, ADAM_EPS, ADAM_WD, ADAM_STEP = 0.001, 0.9, 0.999, 1e-08, 0.01, 10
VMEM_LIMIT = 52 * 1024 * 1024
MESH = pl.DeviceIdType.MESH
ANY = pl.BlockSpec(memory_space=pl.ANY)


def _sds(shape, dt):
    return jax.ShapeDtypeStruct(tuple(shape), dt)


def _pick(n, cands):
    for c in cands:
        if n % c == 0:
            return c
    return n


def _cparams(**kw):
    return pltpu.CompilerParams(vmem_limit_bytes=VMEM_LIMIT, **kw)


def _sig(x):
    return 1.0 / (1.0 + jnp.exp(-x))


def _gelu(x):
    t = jnp.tanh(0.7978845608028654 * (x + 0.044715 * x * x * x))
    return 0.5 * x * (1.0 + t), t


def _gelu_grad(x, t):
    return 0.5 * (1.0 + t) + 0.5 * x * (1.0 - t * t) * 0.7978845608028654 * (1.0 + 3.0 * 0.044715 * x * x)


class Comm:
    def __init__(self, reads, lands, n_sem, start, finish):
        self.reads, self.lands, self.n_sem, self.start, self.finish = list(reads), list(lands), n_sem, start, finish


def merge_comms(comms):
    reads = [r for c in comms for r in c.reads]
    lands = [l for c in comms for l in c.lands]

    def run(which):
        def fn(r, lo, send_sems, recv_sems, off=0):
            ro = lo_ = so = 0
            for c in comms:
                getattr(c, which)(r[ro:ro + len(c.reads)], lo[lo_:lo_ + len(c.lands)], send_sems, recv_sems, off + so)
                ro, lo_, so = ro + len(c.reads), lo_ + len(c.lands), so + c.n_sem
        return fn

    return Comm(reads, lands, sum(c.n_sem for c in comms), run('start'), run('finish'))


def hosted_call(body, *, name, grid, in_specs, out_specs, out_shape, args, scratch_shapes=(), aliases=None, comm=None):
    aliases = dict(aliases or {})
    if comm is None:
        return pl.pallas_call(
            body, name=name, grid=grid, in_specs=list(in_specs), out_specs=list(out_specs), out_shape=list(out_shape),
            scratch_shapes=list(scratch_shapes), input_output_aliases=aliases, compiler_params=_cparams())(*args)
    n_in, n_out, n_sc = len(args), len(out_shape), len(scratch_shapes)
    land_in = [(t, l) for t, l in enumerate(comm.lands) if not isinstance(l, jax.ShapeDtypeStruct)]
    nr, nli, nl = len(comm.reads), len(land_in), len(comm.lands)

    def wrapped(*refs):
        a = refs[:n_in]
        r = refs[n_in:n_in + nr]
        pos = n_in + nr + nli
        o = refs[pos:pos + n_out]
        lo = refs[pos + n_out:pos + n_out + nl]
        sc = refs[pos + n_out + nl:pos + n_out + nl + n_sc]
        send_sems, recv_sems = refs[pos + n_out + nl + n_sc:]
        ids = [pl.program_id(d) for d in range(len(grid))]
        first, last = ids[0] == 0, ids[0] == grid[0] - 1
        for d in range(1, len(grid)):
            first = first & (ids[d] == 0)
            last = last & (ids[d] == grid[d] - 1)

        @pl.when(first)
        def _():
            comm.start(r, lo, send_sems, recv_sems)

        body(*a, *o, *sc)

        @pl.when(last)
        def _():
            comm.finish(r, lo, send_sems, recv_sems)

    for q, (t, _) in enumerate(land_in):
        aliases[n_in + nr + q] = n_out + t
    res = pl.pallas_call(
        wrapped, name=name, grid=grid,
        in_specs=list(in_specs) + [ANY] * (nr + nli), out_specs=list(out_specs) + [ANY] * nl,
        out_shape=list(out_shape) + [l if isinstance(l, jax.ShapeDtypeStruct) else _sds(l.shape, l.dtype) for l in comm.lands],
        scratch_shapes=list(scratch_shapes) + [pltpu.SemaphoreType.DMA((comm.n_sem,)), pltpu.SemaphoreType.DMA((comm.n_sem,))],
        input_output_aliases=aliases, compiler_params=_cparams(),
    )(*args, *comm.reads, *[l for _, l in land_in])
    return list(res[:n_out]), list(res[n_out:])


def comm_call(name, comm):
    def body():
        pass

    return hosted_call(body, name=name, grid=(1,), in_specs=[], out_specs=[], out_shape=[], args=[], comm=comm)[1]


def ew_call(name, grid, fn, ins, outs, first=None, aliases=None, comm=None):
    n_in = len(ins)

    def body(*refs):
        ids = tuple(pl.program_id(a) for a in range(len(grid)))
        vals = fn(ids, *refs[:n_in])
        for (_, _, acc), o_ref, v in zip(outs, refs[n_in:], vals):
            if not acc:
                o_ref[...] = v.astype(o_ref.dtype)
            else:
                is_first = first(ids)

                @pl.when(is_first)
                def _(o_ref=o_ref, v=v):
                    o_ref[...] = v.astype(o_ref.dtype)

                @pl.when(jnp.logical_not(is_first))
                def _(o_ref=o_ref, v=v):
                    o_ref[...] += v.astype(o_ref.dtype)

    return hosted_call(body, name=name, grid=grid, in_specs=[s for _, s in ins], out_specs=[s for _, s, _ in outs],
                       out_shape=[o for o, _, _ in outs], args=[a for a, _ in ins], aliases=aliases, comm=comm)


def fused_mm(name, grid, ins, prods, acc_shapes, epi, outs, extras=(), pre=None, comm=None, row_split=1):
    n_in, n_ex, n_out = len(ins), len(extras), len(outs)
    nk = grid[-1]
    pre = pre or {}
    n_acc = len(acc_shapes)

    def body(*refs):
        in_refs = refs[:n_in]
        ex_refs = refs[n_in:n_in + n_ex]
        out_refs = refs[n_in + n_ex:n_in + n_ex + n_out]
        accs = refs[n_in + n_ex + n_out:]
        ids = tuple(pl.program_id(a) for a in range(len(grid)))
        k = ids[-1]
        loaded = {}

        def operand(i):
            if i not in loaded:
                v = in_refs[i][...]
                if i in pre:
                    v = pre[i](v)
                loaded[i] = v.astype(BF)
            return loaded[i]

        def product(ia, ib, dims):
            return lax.dot_general(operand(ia), operand(ib), (dims, ((), ())), preferred_element_type=F32)

        if nk == 1:
            rows = acc_shapes[0][0]
            step = rows // row_split
            parts = [slice(p * step, (p + 1) * step) for p in range(row_split)]
            all_sums = []
            for rs in parts:
                sums = [None] * n_acc
                for ia, ib, dims, ai in prods:
                    lhs = operand(ia) if row_split == 1 else operand(ia)[rs]
                    d = lax.dot_general(lhs, operand(ib), (dims, ((), ())), preferred_element_type=F32)
                    sums[ai] = d if sums[ai] is None else sums[ai] + d
                all_sums.append(sums)
            for rs, sums in zip(parts, all_sums):
                ex = ex_refs if row_split == 1 else [e.at[rs] if e.shape[0] == rows else e for e in ex_refs]
                for o_ref, v in zip(out_refs, epi(ids, sums, ex)):
                    if row_split == 1:
                        o_ref[...] = v.astype(o_ref.dtype)
                    else:
                        o_ref[rs] = v.astype(o_ref.dtype)
            return

        @pl.when(k == 0)
        def _():
            for a in accs:
                a[...] = jnp.zeros(a.shape, F32)

        for ia, ib, dims, ai in prods:
            accs[ai][...] += product(ia, ib, dims)

        @pl.when(k == nk - 1)
        def _():
            vals = epi(ids, [a[...] for a in accs], ex_refs)
            for o_ref, v in zip(out_refs, vals):
                o_ref[...] = v.astype(o_ref.dtype)

    return hosted_call(
        body, name=name, grid=grid, in_specs=[s for _, s in ins] + [s for _, s in extras],
        out_specs=[s for _, s in outs], out_shape=[o for o, _ in outs],
        scratch_shapes=[pltpu.VMEM(s, F32) for s in acc_shapes] if nk > 1 else [],
        args=[a for a, _ in ins] + [a for a, _ in extras], comm=comm)


NN = ((1,), (0,))
NT = ((1,), (1,))
TN = ((0,), (0,))


class Dims:
    def __init__(self, S, C, D, F4, W4, NS, LB):
        self.S, self.C, self.D, self.F4, self.W4, self.NS, self.LB = S, C, D, F4, W4, NS, LB
        self.T = S + C
        self.DFF = F4 * NS
        self.INW = W4 * NS
        self.NQ = D // HEAD_DIM
        self.KVW = (self.INW - 5 * D) // 2
        self.NKV = self.KVW // HEAD_DIM
        self.G = self.NQ // self.NKV
        self.OFF_K = D
        self.OFF_V = D + self.KVW
        self.OFF_LX = D + 2 * self.KVW
        self.OFF_LG = self.OFF_LX + D
        self.OFF_GA = self.OFF_LG + D
        self.OFF_GL = self.OFF_GA + D
        self.bm = _pick(C, [256, 128, 64, 32, 16, 8])
        self.nCb = C // self.bm
        self.nTb = self.T // self.bm
        self.nSb = S // self.bm
        self.mT = _pick(self.T, [544, 512, 384, 256, 128])
        self.mS = _pick(S, [512, 256, 128])
        self.kT = _pick(self.T, [1088, 1024, 768, 544, 512, 384, 256, 128])
        self.kS = _pick(S, [1024, 512, 256, 128])
        self.cw = _pick(D, [1024, 512, 256, 128]) if (self.OFF_LX % 1024 == 0 and D % 1024 == 0) else _pick(
            self.OFF_LX, [512, 256, 128])
        self.nsub = 2 if (W4 % 256 == 0 and W4 >= 512) else 1
        self.wb = W4 // self.nsub
        self.LBD = D // LB
        self.bq = _pick(C, [256, 128]) if S % _pick(C, [256, 128]) == 0 else 128


def rope_tables(dm):
    rows = dm.S // GRID_W
    row = jnp.repeat(jnp.arange(rows, dtype=F32), GRID_W)
    col = jnp.tile(jnp.arange(GRID_W, dtype=F32), rows)
    axis_dims = HEAD_DIM // 2
    freqs = ROPE_THETA ** (-jnp.arange(0, axis_dims, 2, dtype=F32) / axis_dims)
    ang = jnp.concatenate([row[:, None] * freqs, col[:, None] * freqs], axis=-1)
    cos = jnp.repeat(jnp.cos(ang), 2, axis=-1)
    sin = jnp.repeat(jnp.sin(ang), 2, axis=-1)
    sign = jnp.tile(jnp.array([-1.0, 1.0], F32), HEAD_DIM // 2)
    sin = sin * sign
    cos = jnp.concatenate([jnp.ones((dm.C, HEAD_DIM), F32), cos], axis=0)
    sin = jnp.concatenate([jnp.zeros((dm.C, HEAD_DIM), F32), sin], axis=0)
    return cos, sin


def _pair_swap(y):
    lane = lax.broadcasted_iota(jnp.int32, y.shape, 1)
    nxt = pltpu.roll(y, y.shape[1] - 1, 1)
    prv = pltpu.roll(y, 1, 1)
    return jnp.where((lane & 1) == 0, nxt, prv)


def normmod_fwd(name, dm, x, norm_g3, stage, modv, rows_T):
    D, bm = dm.D, dm.bm
    nb = dm.nTb if rows_T else dm.nSb
    typ = (lambda i: jnp.where(i < dm.nCb, 0, 1)) if rows_T else (lambda i: 1)

    def fn(ids, x_ref, g_ref, sh_ref, sc_ref):
        xv = x_ref[...]
        r = lax.rsqrt(jnp.mean(xv * xv, axis=-1, keepdims=True) + EPS)
        n = xv * r * g_ref[...]
        return [n * (1.0 + sc_ref[...]) + sh_ref[...]]

    return ew_call(
        name, (nb,), fn,
        [(x, pl.BlockSpec((bm, D), lambda i: (i, 0))),
         (norm_g3, pl.BlockSpec((None, 1, D), lambda i: (stage, 0, 0))),
         (modv, pl.BlockSpec((None, None, 1, D), lambda i: (typ(i), 3 * stage, 0, 0))),
         (modv, pl.BlockSpec((None, None, 1, D), lambda i: (typ(i), 3 * stage + 1, 0, 0)))],
        [(_sds(x.shape, BF), pl.BlockSpec((bm, D), lambda i: (i, 0)), False)])[0]


def normmod_concat_fwd(name, dm, ctx, x, norm_g3, modv):
    D, bm, nCb = dm.D, dm.bm, dm.nCb
    typ = lambda i: jnp.where(i < nCb, 0, 1)

    def fn(ids, c_ref, x_ref, g_ref, sh_ref, sc_ref):
        xv = jnp.where(ids[0] < nCb, c_ref[...], x_ref[...])
        r = lax.rsqrt(jnp.mean(xv * xv, axis=-1, keepdims=True) + EPS)
        n = xv * r * g_ref[...]
        return [n * (1.0 + sc_ref[...]) + sh_ref[...], xv]

    row = pl.BlockSpec((bm, D), lambda i: (i, 0))
    return ew_call(
        name, (dm.nTb,), fn,
        [(ctx, pl.BlockSpec((bm, D), lambda i: (jnp.minimum(i, nCb - 1), 0))),
         (x, pl.BlockSpec((bm, D), lambda i: (jnp.maximum(i - nCb, 0), 0))),
         (norm_g3, pl.BlockSpec((None, 1, D), lambda i: (0, 0, 0))),
         (modv, pl.BlockSpec((None, None, 1, D), lambda i: (typ(i), 0, 0, 0))),
         (modv, pl.BlockSpec((None, None, 1, D), lambda i: (typ(i), 1, 0, 0)))],
        [(_sds((dm.T, D), BF), row, False), (_sds((dm.T, D), F32), row, False)])


def normmod_bwd(name, dm, dh, x, dres, norm_g3, stage, modv, rows_T, dres_lat_only, out_lat_only=False, gate=None):
    D, bm = dm.D, dm.bm
    nb = dm.nTb if rows_T else dm.nSb
    nCb = dm.nCb
    typ = (lambda i: jnp.where(i < nCb, 0, 1)) if rows_T else (lambda i: 1)
    if dres_lat_only:
        dres_map = lambda i: (jnp.maximum(i - nCb, 0), 0)
    else:
        dres_map = lambda i: (i, 0)

    def fn(ids, dh_ref, x_ref, dres_ref, g_ref, sc_ref, *gate_refs):
        i = ids[0]
        xv = x_ref[...]
        dhv = dh_ref[...].astype(F32)
        r = lax.rsqrt(jnp.mean(xv * xv, axis=-1, keepdims=True) + EPS)
        xn = xv * r
        g = g_ref[...]
        n = xn * g
        dn = dhv * (1.0 + sc_ref[...])
        dxn = dn * g
        dx = r * (dxn - xn * jnp.mean(dxn * xn, axis=-1, keepdims=True))
        dresv = dres_ref[...]
        if dres_lat_only:
            dresv = jnp.where(i >= nCb, dresv, 0.0)
        dsh = jnp.sum(dhv, axis=0, keepdims=True)
        dsc = jnp.sum(dhv * n, axis=0, keepdims=True)
        dg = jnp.sum(dn * xn, axis=0, keepdims=True)
        dxt = dx + dresv
        res = [dxt, dsh, dsc, dg]
        if gate is not None:
            f_ref, gv_ref = gate_refs
            res += [gate[2] * gv_ref[...] * dxt, jnp.sum(gate[2] * f_ref[...].astype(F32) * dxt, axis=0, keepdims=True)]
        return res

    if rows_T:
        first = lambda ids: (ids[0] == 0) | (ids[0] == nCb)
    else:
        first = lambda ids: ids[0] == 0
    row = pl.BlockSpec((bm, D), lambda i: (i, 0))
    acc = (_sds((2, 1, D), F32), pl.BlockSpec((None, 1, D), lambda i: (typ(i), 0, 0)), True)
    ins = [(dh, row), (x, row), (dres, pl.BlockSpec((bm, D), dres_map)),
           (norm_g3, pl.BlockSpec((None, 1, D), lambda i: (stage, 0, 0))),
           (modv, pl.BlockSpec((None, None, 1, D), lambda i: (typ(i), 3 * stage + 1, 0, 0)))]
    outs = [(_sds((dm.S, D) if out_lat_only else x.shape, F32),
             pl.BlockSpec((bm, D), (lambda i: (jnp.maximum(i - nCb, 0), 0)) if out_lat_only else (lambda i: (i, 0))), False),
            acc, acc, acc]
    if gate is not None:
        ins += [(gate[0], row), (modv, pl.BlockSpec((None, None, 1, D), lambda i: (typ(i), gate[1], 0, 0)))]
        outs += [(_sds(x.shape, BF), row, False), acc]
    return ew_call(name, (nb,), fn, ins, outs, first=first)


def ffn_fwd(name, dm, h, xres, wg, wu, wd, modv, gidx, rows_T, comm_up=None, comm_down=None):
    D, F4, NS = dm.D, dm.F4, dm.NS
    M = h.shape[0]
    bm = dm.mT if rows_T else dm.mS
    C = dm.C

    def epi_up(ids, accs, ex):
        a, u = accs
        return [a, u, a * _sig(a) * u]

    hspec = pl.BlockSpec((bm, D), lambda j, i, k: (i, 0))
    wspec = pl.BlockSpec((None, D, F4), lambda j, i, k: (j, 0, 0))
    ospec = pl.BlockSpec((bm, F4), lambda j, i, k: (i, j))
    res = fused_mm(
        name + "_up", (NS, M // bm, 1), [(h, hspec), (wg, wspec), (wu, wspec)],
        [(0, 1, NN, 0), (0, 2, NN, 1)], [(bm, F4), (bm, F4)], epi_up,
        [(_sds((M, dm.DFF), BF), ospec)] * 3, comm=comm_up)
    (a, u, s), land_up = res if comm_up is not None else (res, None)
    if wd is None:
        wd = land_up[0]

    bn = _pick(D, [1024, 512, 256, 128])

    def epi_dn(ids, accs, ex):
        f = accs[0]
        if rows_T:
            row = ids[0] * bm + lax.broadcasted_iota(jnp.int32, (bm, 1), 0)
            gate = jnp.where(row < C, ex[1][...], ex[2][...])
        else:
            gate = ex[2][...]
        return [ex[0][...] + FFN_RES * gate * f, f]

    gspec = lambda t: pl.BlockSpec((None, None, 1, bn), lambda i, j, k: (t, gidx, 0, j))
    res = fused_mm(
        name + "_down", (M // bm, D // bn, NS // 2),
        [(s, pl.BlockSpec((bm, F4), lambda i, j, k: (i, 2 * k))),
         (wd, pl.BlockSpec((None, F4, bn), lambda i, j, k: (2 * k, 0, j))),
         (s, pl.BlockSpec((bm, F4), lambda i, j, k: (i, 2 * k + 1))),
         (wd, pl.BlockSpec((None, F4, bn), lambda i, j, k: (2 * k + 1, 0, j)))],
        [(0, 1, NN, 0), (2, 3, NN, 0)], [(bm, bn)], epi_dn,
        [(_sds((M, D), F32), pl.BlockSpec((bm, bn), lambda i, j, k: (i, j))),
         (_sds((M, D), BF), pl.BlockSpec((bm, bn), lambda i, j, k: (i, j)))],
        extras=[(xres, pl.BlockSpec((bm, bn), lambda i, j, k: (i, j))), (modv, gspec(0)), (modv, gspec(1))],
        comm=comm_down)
    (xo, f), land_down = res if comm_down is not None else (res, None)
    return xo, a, u, s, f, land_up, land_down


def ffn_bwd(name, dm, df, h, a, u, s, wg, wu, wd, rows_T, comms=None):
    comms = comms or {}
    landed, made = {}, {}

    def run(key, *args, **kw):
        comm = comms[key](landed, made) if key in comms else None
        res = fused_mm(*args, comm=comm, **kw)
        if comm is not None:
            res, landed[key] = res
        return res

    D, F4, NS = dm.D, dm.F4, dm.NS
    M = h.shape[0]
    bm = dm.mT if rows_T else dm.mS
    bkr = dm.kT if rows_T else dm.kS

    def epi_ds(ids, accs, ex):
        ds = accs[0]
        av = ex[0][...].astype(F32)
        uv = ex[1][...].astype(F32)
        sg = _sig(av)
        return [ds * uv * (sg * (1.0 + av * (1.0 - sg))), ds * av * sg]

    ospec = pl.BlockSpec((bm, F4), lambda j, i, k: (i, j))
    da, du = run(
        'ds', name + "_ds", (NS, M // bm, 1),
        [(df, pl.BlockSpec((bm, D), lambda j, i, k: (i, 0))),
         (wd, pl.BlockSpec((None, F4, D), lambda j, i, k: (j, 0, 0)))],
        [(0, 1, NT, 0)], [(bm, F4)], epi_ds, [(_sds((M, dm.DFF), BF), ospec)] * 2,
        extras=[(a, ospec), (u, ospec)], row_split=2)

    ident = lambda ids, accs, ex: list(accs)
    bn = _pick(D, [1024, 512, 256, 128])
    dwg, dwu = run(
        'dwgu', name + "_dwgu", (D // bn, NS, M // bkr),
        [(h, pl.BlockSpec((bkr, bn), lambda i, j, k: (k, i))),
         (da, pl.BlockSpec((bkr, F4), lambda i, j, k: (k, j))),
         (du, pl.BlockSpec((bkr, F4), lambda i, j, k: (k, j)))],
        [(0, 1, TN, 0), (0, 2, TN, 1)], [(bn, F4), (bn, F4)], ident,
        [(_sds((NS, D, F4), BF), pl.BlockSpec((None, bn, F4), lambda i, j, k: (j, i, 0)))] * 2)
    made['dwg'], made['dwu'] = dwg, dwu

    bk2 = 2 * bkr if M % (2 * bkr) == 0 else bkr
    dwd = run(
        'dwd', name + "_dwd", (NS, D // bn, M // bk2),
        [(s, pl.BlockSpec((bk2, F4), lambda i, j, k: (k, i))),
         (df, pl.BlockSpec((bk2, bn), lambda i, j, k: (k, j)))],
        [(0, 1, TN, 0)], [(F4, bn)], ident,
        [(_sds((NS, F4, D), BF), pl.BlockSpec((None, F4, bn), lambda i, j, k: (i, 0, j)))])[0]
    made['dwd'] = dwd

    a_spec = lambda o: pl.BlockSpec((bm, F4), lambda i, j, k: (i, 2 * k + o))
    w_spec = lambda o: pl.BlockSpec((None, bn, F4), lambda i, j, k: (2 * k + o, j, 0))
    dh = run(
        'dh', name + "_dh", (M // bm, D // bn, NS // 2),
        [(da, a_spec(0)), (wg, w_spec(0)), (du, a_spec(0)), (wu, w_spec(0)),
         (da, a_spec(1)), (wg, w_spec(1)), (du, a_spec(1)), (wu, w_spec(1))],
        [(0, 1, NT, 0), (2, 3, NT, 0), (4, 5, NT, 0), (6, 7, NT, 0)], [(bm, bn)], ident,
        [(_sds((M, D), F32), pl.BlockSpec((bm, bn), lambda i, j, k: (i, j)))])[0]
    return dh, dwg, dwu, dwd, landed


def qk_prep(dm, P, gq, gk, cosf, sinf):
    D, KVW, bm = dm.D, dm.KVW, dm.bm

    def head_norm_rope(xh, g, c, s):
        r = lax.rsqrt(jnp.mean(xh * xh, axis=-1, keepdims=True) + EPS)
        y = xh * r * g
        return y * c + _pair_swap(y) * s

    def fn(ids, q_ref, k_ref, v_ref, gq_ref, gk_ref, c_ref, s_ref):
        c, s = c_ref[...], s_ref[...]
        qs = [head_norm_rope(q_ref[:, h * HEAD_DIM:(h + 1) * HEAD_DIM], gq_ref[...], c, s) for h in range(dm.NQ)]
        ks = [head_norm_rope(k_ref[:, h * HEAD_DIM:(h + 1) * HEAD_DIM], gk_ref[...], c, s) for h in range(dm.NKV)]
        return [jnp.concatenate(qs, axis=1), jnp.concatenate(ks, axis=1), v_ref[...]]

    hspec = pl.BlockSpec((bm, HEAD_DIM), lambda i: (i, 0))
    vec = pl.BlockSpec((1, HEAD_DIM), lambda i: (0, 0))
    return ew_call(
        "qk_prep", (dm.nTb,), fn,
        [(P, pl.BlockSpec((bm, D), lambda i: (i, 0))),
         (P, pl.BlockSpec((bm, KVW), lambda i: (i, dm.OFF_K // KVW))),
         (P, pl.BlockSpec((bm, KVW), lambda i: (i, dm.OFF_V // KVW))),
         (gq, vec), (gk, vec), (cosf, hspec), (sinf, hspec)],
        [(_sds((dm.T, D), BF), pl.BlockSpec((bm, D), lambda i: (i, 0)), False),
         (_sds((dm.T, KVW), BF), pl.BlockSpec((bm, KVW), lambda i: (i, 0)), False),
         (_sds((dm.T, KVW), BF), pl.BlockSpec((bm, KVW), lambda i: (i, 0)), False)])


def qk_prep_bwd(dm, dq, dk, dv, P, gq, gk, cosf, sinf, dP):
    D, KVW, bm, nCb = dm.D, dm.KVW, dm.bm, dm.nCb
    W = D + 2 * KVW

    def head_bwd(d, xh, g, c, s):
        dy = d * c - _pair_swap(d) * s
        r = lax.rsqrt(jnp.mean(xh * xh, axis=-1, keepdims=True) + EPS)
        xn = xh * r
        dg = jnp.sum(dy * xn, axis=0, keepdims=True)
        dxn = dy * g
        return r * (dxn - xn * jnp.mean(dxn * xn, axis=-1, keepdims=True)), dg

    def fn(ids, dq_ref, dk_ref, dv_ref, q_ref, k_ref, gq_ref, gk_ref, c_ref, s_ref, dp_any):
        i = ids[0]
        c, s = c_ref[...], s_ref[...]
        lat = i >= nCb
        outs, dgq = [], jnp.zeros((1, HEAD_DIM), F32)
        for h in range(dm.NQ):
            sl = slice(h * HEAD_DIM, (h + 1) * HEAD_DIM)
            d = jnp.where(lat, dq_ref[:, sl], 0.0)
            dx, dg = head_bwd(d, q_ref[:, sl], gq_ref[...], c, s)
            outs.append(dx)
            dgq = dgq + dg
        dgk = jnp.zeros((1, HEAD_DIM), F32)
        for h in range(dm.NKV):
            sl = slice(h * HEAD_DIM, (h + 1) * HEAD_DIM)
            dx, dg = head_bwd(dk_ref[:, sl], k_ref[:, sl], gk_ref[...], c, s)
            outs.append(dx)
            dgk = dgk + dg
        outs.append(dv_ref[...])
        return [jnp.concatenate(outs, axis=1), dgq, dgk]

    hspec = pl.BlockSpec((bm, HEAD_DIM), lambda i: (i, 0))
    vec = pl.BlockSpec((1, HEAD_DIM), lambda i: (0, 0))
    return ew_call(
        "qk_prep_bwd", (dm.nTb,), fn,
        [(dq, pl.BlockSpec((bm, D), lambda i: (jnp.maximum(i - nCb, 0), 0))),
         (dk, pl.BlockSpec((bm, KVW), lambda i: (i, 0))),
         (dv, pl.BlockSpec((bm, KVW), lambda i: (i, 0))),
         (P, pl.BlockSpec((bm, D), lambda i: (i, 0))),
         (P, pl.BlockSpec((bm, KVW), lambda i: (i, dm.OFF_K // KVW))),
         (gq, vec), (gk, vec), (cosf, hspec), (sinf, hspec), (dP, ANY)],
        [(_sds(dP.shape, BF), pl.BlockSpec((bm, W), lambda i: (i, 0)), False),
         (_sds((1, HEAD_DIM), F32), vec, True), (_sds((1, HEAD_DIM), F32), vec, True)],
        first=lambda ids: ids[0] == 0, aliases={9: 0})


def _softmax_numerators(s_ref, eb_ref, mb_ref, scale):
    rows, T = s_ref.shape
    m = jnp.max(s_ref[...], axis=-1, keepdims=True)
    mb_ref[...] = jnp.broadcast_to(m, (rows, HEAD_DIM))
    lacc = jnp.zeros((rows, HEAD_DIM), F32)
    for c in range(T // HEAD_DIM):
        cs = slice(c * HEAD_DIM, (c + 1) * HEAD_DIM)
        e = jnp.exp2((s_ref[:, cs] - mb_ref[...]) * (scale * LOG2_E))
        lacc = lacc + e
        eb_ref[:, cs] = e.astype(BF)
    return jnp.sum(lacc, axis=-1, keepdims=True)


def attention_fwd(dm, qr, kr, vb, comm=None):
    S, T, D, G, nCb = dm.S, dm.T, dm.D, dm.G, dm.nCb
    bq = dm.bq
    off = dm.C // bq
    scale = HEAD_DIM ** -0.5
    GW = G * HEAD_DIM

    def body(q_ref, k_ref, v_ref, o_ref):
        k = k_ref[...]
        v = v_ref[...]
        head = lambda h: slice(h * HEAD_DIM, (h + 1) * HEAD_DIM)
        scores = lambda h: lax.dot_general(q_ref[:, head(h)], k, (NT, ((), ())), preferred_element_type=F32)
        s_next = scores(0)
        for h in range(G):
            s = s_next
            if h + 1 < G:
                s_next = scores(h + 1)
            m = jnp.max(s, axis=-1, keepdims=True)
            p = jnp.exp2((s - m) * (scale * LOG2_E))
            l = jnp.sum(p, axis=-1, keepdims=True)
            o = lax.dot_general(p.astype(BF), v, (NN, ((), ())), preferred_element_type=F32)
            o_ref[:, head(h)] = o / l

    return hosted_call(
        body, grid=(dm.NKV, S // bq), name="attn_fwd",
        in_specs=[pl.BlockSpec((bq, GW), lambda g, i: (i + off, g)),
                  pl.BlockSpec((T, HEAD_DIM), lambda g, i: (0, g)),
                  pl.BlockSpec((T, HEAD_DIM), lambda g, i: (0, g))],
        out_specs=[pl.BlockSpec((bq, GW), lambda g, i: (i, g))],
        out_shape=[_sds((S, D), F32)], args=[qr, kr, vb], comm=comm)


def attention_bwd(dm, qr, kr, vb, attn, dattn, comm=None):
    S, T, D, G = dm.S, dm.T, dm.D, dm.G
    bq = dm.bq
    off = dm.C // bq
    scale = HEAD_DIM ** -0.5
    GW = G * HEAD_DIM

    def body(q_ref, k_ref, v_ref, o_ref, do_ref, dq_ref, dk_ref, dv_ref, s2_ref, dp_ref, eb_ref, tb_ref, mb_ref):
        i = pl.program_id(1)

        @pl.when(i == 0)
        def _():
            dk_ref[...] = jnp.zeros(dk_ref.shape, F32)
            dv_ref[...] = jnp.zeros(dv_ref.shape, F32)

        k = k_ref[...]
        v = v_ref[...]
        head = lambda h: slice(h * HEAD_DIM, (h + 1) * HEAD_DIM)

        def finish(h, w):
            dq_ref[:, head(h)] = lax.dot_general(tb_ref[...], k, (NN, ((), ())), preferred_element_type=F32) * w
            dk_ref[...] += lax.dot_general(tb_ref[...], (q_ref[:, head(h)].astype(F32) * w).astype(BF), (TN, ((), ())),
                                           preferred_element_type=F32)

        s2_ref[0] = lax.dot_general(q_ref[:, head(0)], k, (NT, ((), ())), preferred_element_type=F32)
        w_prev = None
        for h in range(G):
            s_ref = s2_ref.at[h % 2]
            do = do_ref[:, head(h)]
            dof = do.astype(F32)
            if h + 1 < G:
                s2_ref[(h + 1) % 2] = lax.dot_general(q_ref[:, head(h + 1)], k, (NT, ((), ())),
                                                      preferred_element_type=F32)
            if h > 0:
                finish(h - 1, w_prev)
            l = _softmax_numerators(s_ref, eb_ref, mb_ref, scale)
            rl = 1.0 / l
            dp_ref[...] = lax.dot_general(do, v, (NT, ((), ())), preferred_element_type=F32)
            dv_ref[...] += lax.dot_general(eb_ref[...], (dof * rl).astype(BF), (TN, ((), ())), preferred_element_type=F32)
            delta = jnp.sum(dof * o_ref[:, head(h)], axis=-1, keepdims=True)
            mb_ref[...] = jnp.broadcast_to(delta, (bq, HEAD_DIM))
            for c in range(T // HEAD_DIM):
                cs = slice(c * HEAD_DIM, (c + 1) * HEAD_DIM)
                tb_ref[:, cs] = (eb_ref[:, cs].astype(F32) * (dp_ref[:, cs] - mb_ref[...])).astype(BF)
            w_prev = scale * rl
        finish(G - 1, w_prev)

    return hosted_call(
        body, grid=(dm.NKV, S // bq), name="attn_bwd",
        in_specs=[pl.BlockSpec((bq, GW), lambda g, i: (i + off, g)),
                  pl.BlockSpec((T, HEAD_DIM), lambda g, i: (0, g)),
                  pl.BlockSpec((T, HEAD_DIM), lambda g, i: (0, g)),
                  pl.BlockSpec((bq, GW), lambda g, i: (i, g)),
                  pl.BlockSpec((bq, GW), lambda g, i: (i + off, g))],
        out_specs=[pl.BlockSpec((bq, GW), lambda g, i: (i, g)),
                   pl.BlockSpec((T, HEAD_DIM), lambda g, i: (0, g)),
                   pl.BlockSpec((T, HEAD_DIM), lambda g, i: (0, g))],
        out_shape=[_sds((S, D), F32), _sds((T, dm.KVW), F32), _sds((T, dm.KVW), F32)],
        args=[qr, kr, vb, attn, dattn],
        scratch_shapes=[pltpu.VMEM((2, bq, T), F32), pltpu.VMEM((bq, T), F32), pltpu.VMEM((bq, T), BF),
                        pltpu.VMEM((bq, T), BF), pltpu.VMEM((bq, HEAD_DIM), F32)], comm=comm)


def _conv_taps(dm, lx, masks_only=False):
    T, C = dm.T, dm.C
    t = lax.broadcasted_iota(jnp.int32, (T, 1), 0)
    valid = [(t >= 2) & ((t < C) | (t >= C + 2)), (t >= 1) & ((t < C) | (t >= C + 1)), None,
             (t != C - 1) & (t != T - 1)]
    shifts = [2, 1, 0, T - 1]
    taps = []
    for k in range(4):
        if k == 2:
            taps.append(lx)
        else:
            taps.append(jnp.where(valid[k], pltpu.roll(lx, shifts[k], 0), 0.0))
    return taps


def _scan_tiles(dm, chains):
    T, C = dm.T, dm.C
    nT, nC = T // 8, C // 8
    row = lax.broadcasted_iota(jnp.int32, (8, HEAD_DIM), 0)

    def tile_of(i, asc, split):
        if not split:
            return i if asc else nT - 1 - i
        if asc:
            return jnp.where(i < nT - nC, nC + i, i - (nT - nC))
        return jnp.where(i < nC, nC - 1 - i, nT - 1 - (i - nC))

    def step(i, carry, asc, split, a_ref, u_ref, out_ref, mode):
        off = pl.multiple_of(tile_of(i, asc, split) * 8, 8)
        a = a_ref[pl.ds(off, 8), :]
        b = u_ref[pl.ds(off, 8), :]
        if mode == 'lam':
            if asc:
                coef = jnp.where(row == 0, 1.0, pltpu.roll(a, 1, 0))
            else:
                coef = jnp.where(row == 7, 1.0, pltpu.roll(a, 7, 0))
        else:
            coef = a
        A, B = coef, b
        for d in (1, 2, 4):
            if asc:
                ok = row >= d
                A_sh = jnp.where(ok, pltpu.roll(A, d, 0), 1.0)
                B_sh = jnp.where(ok, pltpu.roll(B, d, 0), 0.0)
            else:
                ok = row < 8 - d
                A_sh = jnp.where(ok, pltpu.roll(A, 8 - d, 0), 1.0)
                B_sh = jnp.where(ok, pltpu.roll(B, 8 - d, 0), 0.0)
            B = B + A * B_sh
            A = A * A_sh
        h = A * carry + B
        out_ref[pl.ds(off, 8), :] = h
        last = h[7:8, :] if asc else h[0:1, :]
        if mode == 'lam':
            last = last * (a[7:8, :] if asc else a[0:1, :])
        return jnp.broadcast_to(last, (8, HEAD_DIM))

    U = 4 if nT % 4 == 0 else 1

    def body(i, carries):
        for u in range(U):
            carries = tuple(step(i * U + u, c_, *ch) for c_, ch in zip(carries, chains))
        return carries

    lax.fori_loop(0, nT // U, body, tuple(jnp.zeros((8, HEAD_DIM), F32) for _ in chains))


def _lru_gates(xc, wa, ba, wx, bx, sp):
    xb = xc.astype(BF)
    r = _sig(jnp.dot(xb, wa, preferred_element_type=F32) + ba)
    i = _sig(jnp.dot(xb, wx, preferred_element_type=F32) + bx)
    a = jnp.exp(-LRU_C * r * sp)
    m = jnp.sqrt(1.0 - a * a)
    return r, i, a, m


def lru_fwd(dm, P, conv_w, conv_b, wa, ba, wx, bx, sp):
    T, D, LB = dm.T, dm.D, dm.LB
    W = dm.LBD
    R = _pick(T, [1088, 544, 272, 256, 128, 64, 8])
    lxb = dm.OFF_LX // W

    def body(lx_ref, cw_ref, cb_ref, wa_ref, ba_ref, wx_ref, bx_ref, sp_ref, hf_ref, hb_ref, xc_ref, a_ref, u_ref):
        taps = _conv_taps(dm, lx_ref[...])
        xc = cb_ref[...]
        for k in range(4):
            xc = xc + taps[k] * cw_ref[k:k + 1, :]
        xc_ref[...] = xc

        def chunk(ci, _):
            off = pl.multiple_of(ci * R, 8)
            x = xc_ref[pl.ds(off, R), :]
            for d in range(2):
                r, i, a, m = _lru_gates(x, wa_ref[d].astype(BF), ba_ref[d], wx_ref[d].astype(BF), bx_ref[d], sp_ref[d])
                a_ref[d, pl.ds(off, R), :] = a
                u_ref[d, pl.ds(off, R), :] = m * i * x
            return 0

        lax.fori_loop(0, T // R, chunk, 0)
        _scan_tiles(dm, [(True, False, a_ref.at[0], u_ref.at[0], hf_ref, 'h'),
                         (False, True, a_ref.at[1], u_ref.at[1], hb_ref, 'h')])

    strip = lambda j: (0, j)
    vec = pl.BlockSpec((2, 1, W), lambda j: (0, 0, j))
    mat = pl.BlockSpec((2, None, W, W), lambda j: (0, j, 0, 0))
    return pl.pallas_call(
        body, grid=(LB,), name="lru_fwd",
        in_specs=[pl.BlockSpec((T, W), lambda j: (0, lxb + j)),
                  pl.BlockSpec((4, W), strip), pl.BlockSpec((1, W), strip), mat, vec, mat, vec, vec],
        out_specs=[pl.BlockSpec((T, W), strip)] * 2, out_shape=[_sds((T, D), F32)] * 2,
        scratch_shapes=[pltpu.VMEM((T, W), F32), pltpu.VMEM((2, T, W), F32), pltpu.VMEM((2, T, W), F32)],
        compiler_params=_cparams(),
    )(P, conv_w, conv_b, wa, ba, wx, bx, sp)


def lru_bwd(dm, P, dh, hf, hb, conv_w, conv_b, wa, ba, wx, bx, sp, sg, dP):
    T, C, D, LB = dm.T, dm.C, dm.D, dm.LB
    W = dm.LBD
    R = _pick(T, [1088, 544, 272, 256, 128, 64, 8])
    lxb = dm.OFF_LX // W

    def body(lx_ref, dh_ref, hf_ref, hb_ref, cw_ref, cb_ref, wa_ref, ba_ref, wx_ref, bx_ref, sp_ref, sg_ref, _dp_any,
             dlx_ref, dwa_ref, dba_ref, dwx_ref, dbx_ref, dlam_ref, dcw_ref, dcb_ref,
             xc_ref, a_ref, lam_ref, hp_ref, dxc_ref):
        lx = lx_ref[...]
        taps = _conv_taps(dm, lx)
        xc = cb_ref[...]
        for k in range(4):
            xc = xc + taps[k] * cw_ref[k:k + 1, :]
        xc_ref[...] = xc

        def gates(d, x):
            return _lru_gates(x, wa_ref[d].astype(BF), ba_ref[d], wx_ref[d].astype(BF), bx_ref[d], sp_ref[d])

        def chunk_a(ci, _):
            off = pl.multiple_of(ci * R, 8)
            x = xc_ref[pl.ds(off, R), :]
            for d in range(2):
                a_ref[d, pl.ds(off, R), :] = gates(d, x)[2]
            return 0

        lax.fori_loop(0, T // R, chunk_a, 0)
        _scan_tiles(dm, [(False, False, a_ref.at[0], dh_ref, lam_ref.at[0], 'lam'),
                         (True, True, a_ref.at[1], dh_ref, lam_ref.at[1], 'lam')])
        t = lax.broadcasted_iota(jnp.int32, (T, 1), 0)
        hp_ref[0] = jnp.where(t == 0, 0.0, pltpu.roll(hf_ref[...], 1, 0))
        hv = hb_ref[...]
        hp_ref[1] = jnp.where(t == C - 1, 0.0, jnp.where(t == T - 1, jnp.broadcast_to(hv[0:1, :], hv.shape),
                                                         pltpu.roll(hv, T - 1, 0)))

        def chunk_b(d):
            wa_, wx_ = wa_ref[d].astype(BF), wx_ref[d].astype(BF)

            def run(ci, carry):
                dwa, dwx, dba, dbx, dlam = carry
                off = pl.multiple_of(ci * R, 8)
                x = xc_ref[pl.ds(off, R), :]
                r, i, a, m = gates(d, x)
                lam = lam_ref[d, pl.ds(off, R), :]
                da = lam * hp_ref[d, pl.ds(off, R), :] - lam * (i * x) * a / m
                dloga = da * a
                dza = dloga * (-LRU_C) * sp_ref[d] * r * (1.0 - r)
                dzx = lam * m * x * i * (1.0 - i)
                dzab, dzxb = dza.astype(BF), dzx.astype(BF)
                xb = x.astype(BF)
                dxc = lam * m * i
                dxc = dxc + lax.dot_general(dzab, wa_, (NT, ((), ())), preferred_element_type=F32)
                dxc = dxc + lax.dot_general(dzxb, wx_, (NT, ((), ())), preferred_element_type=F32)
                if d == 0:
                    dxc_ref[pl.ds(off, R), :] = dxc
                else:
                    dxc_ref[pl.ds(off, R), :] += dxc
                dwa = dwa + lax.dot_general(xb, dzab, (TN, ((), ())), preferred_element_type=F32)
                dwx = dwx + lax.dot_general(xb, dzxb, (TN, ((), ())), preferred_element_type=F32)
                dba = dba + jnp.sum(dza, axis=0, keepdims=True)
                dbx = dbx + jnp.sum(dzx, axis=0, keepdims=True)
                dlam = dlam + jnp.sum(dloga * LRU_C * r, axis=0, keepdims=True)
                return dwa, dwx, dba, dbx, dlam

            z = jnp.zeros((W, W), F32)
            zv = jnp.zeros((1, W), F32)
            dwa, dwx, dba, dbx, dlam = lax.fori_loop(0, T // R, run, (z, z, zv, zv, zv))
            dwa_ref[d] = dwa
            dwx_ref[d] = dwx
            dba_ref[d] = dba
            dbx_ref[d] = dbx
            dlam_ref[d] = dlam * sg_ref[d]

        chunk_b(0)
        chunk_b(1)
        dxc = dxc_ref[...]
        dcb_ref[...] = jnp.sum(dxc, axis=0, keepdims=True)
        dcw_ref[...] = jnp.concatenate([jnp.sum(dxc * taps[k], axis=0, keepdims=True) for k in range(4)], axis=0)
        valid = [(t < T - 2) & ((t >= C) | (t < C - 2)), (t < T - 1) & ((t >= C) | (t < C - 1)), None,
                 (t != 0) & (t != C)]
        shifts = [T - 2, T - 1, 0, 1]
        dlx = dxc * cw_ref[2:3, :]
        for k in (0, 1, 3):
            dlx = dlx + jnp.where(valid[k], pltpu.roll(dxc, shifts[k], 0), 0.0) * cw_ref[k:k + 1, :]
        dlx_ref[...] = dlx.astype(dlx_ref.dtype)

    strip = lambda j: (0, j)
    sspec = pl.BlockSpec((T, W), strip)
    vec = pl.BlockSpec((2, 1, W), lambda j: (0, 0, j))
    mat = pl.BlockSpec((2, None, W, W), lambda j: (0, j, 0, 0))
    ovec = pl.BlockSpec((2, 1, W), lambda j: (0, 0, j))
    return pl.pallas_call(
        body, grid=(LB,), name="lru_bwd",
        in_specs=[pl.BlockSpec((T, W), lambda j: (0, lxb + j)), sspec, sspec, sspec,
                  pl.BlockSpec((4, W), strip), pl.BlockSpec((1, W), strip), mat, vec, mat, vec, vec, vec, ANY],
        out_specs=[pl.BlockSpec((T, W), lambda j: (0, lxb + j)), mat, ovec, mat, ovec, ovec,
                   pl.BlockSpec((4, W), strip), pl.BlockSpec((1, W), strip)],
        out_shape=[_sds(dP.shape, BF), _sds((2, LB, W, W), F32), _sds((2, 1, D), F32), _sds((2, LB, W, W), F32),
                   _sds((2, 1, D), F32), _sds((2, 1, D), F32), _sds((4, D), F32), _sds((1, D), F32)],
        scratch_shapes=[pltpu.VMEM((T, W), F32), pltpu.VMEM((2, T, W), F32), pltpu.VMEM((2, T, W), F32),
                        pltpu.VMEM((2, T, W), F32), pltpu.VMEM((T, W), F32)],
        input_output_aliases={12: 0}, compiler_params=_cparams(),
    )(P, dh, hf, hb, conv_w, conv_b, wa, ba, wx, bx, sp, sg, dP)


def merge_fwd(dm, P, attn, hf, hb):
    S, D, bm, cw, nCb = dm.S, dm.D, dm.bm, dm.cw, dm.nCb

    def fn(ids, lg_ref, ga_ref, gl_ref, at_ref, hf_ref, hb_ref):
        ge, _ = _gelu(lg_ref[...])
        lru = (hf_ref[...] + hb_ref[...]) * ge
        return [_sig(ga_ref[...]) * at_ref[...] + _sig(gl_ref[...]) * lru]

    pspec = lambda off: pl.BlockSpec((bm, cw), lambda i, j: (i + nCb, off // cw + j))
    tspec = pl.BlockSpec((bm, cw), lambda i, j: (i + nCb, j))
    sspec = pl.BlockSpec((bm, cw), lambda i, j: (i, j))
    return ew_call(
        "merge_fwd", (dm.nSb, D // cw), fn,
        [(P, pspec(dm.OFF_LG)), (P, pspec(dm.OFF_GA)), (P, pspec(dm.OFF_GL)), (attn, sspec), (hf, tspec), (hb, tspec)],
        [(_sds((S, D), BF), sspec, False)])[0]


def merge_bwd(dm, dmg, P, attn, hf, hb):
    S, T, D, bm, cw, nCb = dm.S, dm.T, dm.D, dm.bm, dm.cw, dm.nCb
    nj = D // cw
    n_steps = dm.nTb * nj

    def body(dm_ref, lg_ref, ga_ref, gl_ref, at_ref, hf_ref, hb_ref, dp_ref, da_ref, dh_ref, buf, sems):
        i, j = pl.program_id(0), pl.program_id(1)
        lat = i >= nCb
        d = jnp.where(lat, dm_ref[...].astype(F32), 0.0)
        lg = lg_ref[...]
        ge, th = _gelu(lg)
        hs = hf_ref[...] + hb_ref[...]
        sa, sl = _sig(ga_ref[...]), _sig(gl_ref[...])
        at = jnp.where(lat, at_ref[...], 0.0)
        dlru = d * sl
        n = i * nj + j
        par = n % 2

        def copies(p):
            out = []
            for g, off in enumerate((dm.OFF_LG, dm.OFF_GA, dm.OFF_GL)):
                col = pl.multiple_of(off + j * cw, 128)
                out.append(pltpu.make_async_copy(
                    buf.at[p, g], dp_ref.at[pl.ds(pl.multiple_of(i * bm, 8), bm), pl.ds(col, cw)], sems.at[p, g]))
            return out

        @pl.when(n >= 2)
        def _():
            for cp in copies(par):
                cp.wait()

        buf[par, 0] = (dlru * hs * _gelu_grad(lg, th)).astype(BF)
        buf[par, 1] = (d * at * sa * (1.0 - sa)).astype(BF)
        buf[par, 2] = (d * hs * ge * sl * (1.0 - sl)).astype(BF)
        da_ref[...] = (d * sa).astype(BF)
        dh_ref[...] = dlru * ge
        for cp in copies(par):
            cp.start()

        @pl.when(n == n_steps - 1)
        def _():
            for cp in copies(par):
                cp.wait()
            if n_steps >= 2:
                for cp in copies(1 - par):
                    cp.wait()

    pspec = lambda off: pl.BlockSpec((bm, cw), lambda i, j: (i, off // cw + j))
    tspec = pl.BlockSpec((bm, cw), lambda i, j: (i, j))
    lspec = pl.BlockSpec((bm, cw), lambda i, j: (jnp.maximum(i - nCb, 0), j))
    return pl.pallas_call(
        body, grid=(dm.nTb, nj), name="merge_bwd",
        in_specs=[lspec, pspec(dm.OFF_LG), pspec(dm.OFF_GA), pspec(dm.OFF_GL), lspec, tspec, tspec],
        out_specs=[ANY, tspec, tspec],
        out_shape=[_sds((T, dm.INW), BF), _sds((T, D), BF), _sds((T, D), F32)],
        scratch_shapes=[pltpu.VMEM((2, 3, bm, cw), BF), pltpu.SemaphoreType.DMA((2, 3))],
        compiler_params=_cparams(),
    )(dmg, P, P, P, attn, hf, hb)


def final_loss(dm, x3, gfin, target, f3, modv):
    S, D, bm = dm.S, dm.D, dm.bm

    def fn(ids, x_ref, g_ref, t_ref, f_ref, gv_ref):
        xv = x_ref[...]
        g = g_ref[...]
        r = lax.rsqrt(jnp.mean(xv * xv, axis=-1, keepdims=True) + EPS)
        xn = xv * r
        err = xn * g - t_ref[...]
        loss = 0.5 * jnp.sum(jnp.mean(err * err, axis=-1, keepdims=True), axis=0, keepdims=True)
        dy = err / D
        dxn = dy * g
        dx = r * (dxn - xn * jnp.mean(dxn * xn, axis=-1, keepdims=True))
        return [jnp.broadcast_to(loss, (1, 128)), dx, jnp.sum(dy * xn, axis=0, keepdims=True),
                FFN_RES * gv_ref[...] * dx, jnp.sum(FFN_RES * f_ref[...].astype(F32) * dx, axis=0, keepdims=True)]

    row = pl.BlockSpec((bm, D), lambda i: (i, 0))
    vec = pl.BlockSpec((1, D), lambda i: (0, 0))
    return ew_call(
        "final_loss", (dm.nSb,), fn,
        [(x3, row), (gfin, vec), (target, row), (f3, row),
         (modv, pl.BlockSpec((None, None, 1, D), lambda i: (1, 8, 0, 0)))],
        [(_sds((1, 128), F32), pl.BlockSpec((1, 128), lambda i: (0, 0)), True), (_sds((S, D), F32), row, False),
         (_sds((1, D), F32), vec, True), (_sds((S, D), BF), row, False),
         (_sds((2, 1, D), F32), pl.BlockSpec((None, 1, D), lambda i: (1, 0, 0)), True)], first=lambda ids: ids[0] == 0)


def local_step(dm, x, ctx, target, modv, norm_g3, gfin, gq, gk, conv_w, conv_b, wa, ba, wx, bx, lam, wbuf, where):
    S, C, T, D, NS, F4, W4 = dm.S, dm.C, dm.T, dm.D, dm.NS, dm.F4, dm.W4
    Ds = D // NS
    wb, nsub = dm.wb, dm.nsub
    cosf, sinf = rope_tables(dm)
    sp = jax.nn.softplus(-lam)
    sg = jax.nn.sigmoid(-lam)
    ident = lambda ids, accs, ex: list(accs)
    mT, mS, kT, kS = dm.mT, dm.mS, dm.kT, dm.kS
    bn = _pick(D, [1024, 512, 256, 128])
    bk = _pick(D, [512, 256, 128])

    wg0, wu0 = comm_call("ag_ffn1", ag_comm([wbuf['wg0'], wbuf['wu0']]))
    h1, xt = normmod_concat_fwd("nm1", dm, ctx, x, norm_g3, modv)
    xt1, a1, u1, s1, f1, land, _ = ffn_fwd("ffn1", dm, h1, xt, wg0, wu0, None, modv, 2, True,
                                           comm_up=ag_comm([wbuf['wd0'], wbuf['w_in']]))
    wd0, w_in = land
    h2 = normmod_fwd("nm2", dm, xt1, norm_g3, 1, modv, True)
    P = fused_mm(
        "w_in", (T // mT, NS * nsub, 1),
        [(h2, pl.BlockSpec((mT, D), lambda i, j, k: (i, 0))),
         (w_in, pl.BlockSpec((None, D, wb), lambda i, j, k: (j // nsub, 0, j % nsub)))],
        [(0, 1, NN, 0)], [(mT, wb)], ident,
        [(_sds((T, dm.INW), F32), pl.BlockSpec((mT, wb), lambda i, j, k: (i, j)))])[0]
    qr, kr, vb = qk_prep(dm, P, gq, gk, cosf, sinf)
    (attn,), (wg1, wu1, wd1, w_out) = attention_fwd(
        dm, qr, kr, vb, comm=ag_comm([wbuf['wg1'], wbuf['wu1'], wbuf['wd1'], wbuf['w_out']]))
    w_out = w_out.reshape(D, D)
    hf, hb = lru_fwd(dm, P, conv_w, conv_b, wa, ba, wx, bx, sp)
    mg = merge_fwd(dm, P, attn, hf, hb)

    def epi_o(ids, accs, ex):
        o = accs[0]
        return [ex[0][...] + ex[1][...] * o, o]

    rb, nCb = dm.bm, dm.nCb
    x2, o2 = fused_mm(
        "w_out", (D // bn, S // rb, 1),
        [(mg, pl.BlockSpec((rb, D), lambda j, i, k: (i, 0))), (w_out, pl.BlockSpec((D, bn), lambda j, i, k: (0, j)))],
        [(0, 1, NN, 0)], [(rb, bn)], epi_o,
        [(_sds((S, D), F32), pl.BlockSpec((rb, bn), lambda j, i, k: (i, j))),
         (_sds((S, D), BF), pl.BlockSpec((rb, bn), lambda j, i, k: (i, j)))],
        extras=[(xt1, pl.BlockSpec((rb, bn), lambda j, i, k: (i + nCb, j))),
                (modv, pl.BlockSpec((None, None, 1, bn), lambda j, i, k: (1, 5, 0, j)))])
    h3 = normmod_fwd("nm3", dm, x2, norm_g3, 2, modv, False)
    x3, a3, u3, s3, f3, _, _ = ffn_fwd("ffn2", dm, h3, x2, wg1, wu1, wd1, modv, 8, False)
    loss, dx3, dgfin, df3, dg3 = final_loss(dm, x3, gfin, target, f3, modv)

    dh3, dwg1, dwu1, dwd1, _ = ffn_bwd("ffn2b", dm, df3, h3, a3, u3, s3, wg1, wu1, wd1, False)
    dx2, dsh3, dsc3, dgn3, do2, dg2 = normmod_bwd("nm3b", dm, dh3, x2, dx3, norm_g3, 2, modv, False, False,
                                                  gate=(o2, 5, 1.0))
    keep = {}

    def host_a(key, comm):
        if key == 'p1':
            (keep['dmg'],), landed = fused_mm(
                "w_out_dx", (S // mS, D // bn, 1),
                [(do2, pl.BlockSpec((mS, D), lambda i, j, k: (i, 0))),
                 (w_out, pl.BlockSpec((bn, D), lambda i, j, k: (j, 0)))],
                [(0, 1, NT, 0)], [(mS, bn)], ident,
                [(_sds((S, D), BF), pl.BlockSpec((mS, bn), lambda i, j, k: (i, j)))], comm=comm)
            return landed
        keep['dqkv'], landed = attention_bwd(dm, qr, kr, vb, attn, keep['dattn'], comm=comm)
        return landed

    gots_a = host_a('p1', rs_p1_comm([dwg1, dwu1, dwd1]))
    dmg = keep['dmg']
    dw_out = fused_mm(
        "w_out_dw", (D // bn, D // bn, S // kS),
        [(mg, pl.BlockSpec((kS, bn), lambda i, j, k: (k, i))), (do2, pl.BlockSpec((kS, bn), lambda i, j, k: (k, j)))],
        [(0, 1, TN, 0)], [(bn, bn)], ident,
        [(_sds((D, D), BF), pl.BlockSpec((bn, bn), lambda i, j, k: (i, j)))])[0]
    dP, dattn, dhs = merge_bwd(dm, dmg, P, attn, hf, hb)
    keep['dattn'] = dattn
    pairs_a = [add_pair("rs_add_" + n_, g_, got_, where)
               for n_, g_, got_ in zip(('wg1', 'wu1', 'wd1'), (dwg1, dwu1, dwd1), gots_a)]
    land_a = host_a('p2', rs_p2_comm([p_[0] for p_ in pairs_a], [p_[1] for p_ in pairs_a]))
    dq, dk, dv = keep['dqkv']
    dP, dwa, dba, dwx, dbx, dlam, dcw, dcb = lru_bwd(dm, P, dhs, hf, hb, conv_w, conv_b, wa, ba, wx, bx, sp, sg, dP)
    dP, dgq, dgk = qk_prep_bwd(dm, dq, dk, dv, P, gq, gk, cosf, sinf, dP)
    g_wg = sum_slots_into("rs_sum_wg1", land_a[0], where, None, (2, D, F4), 1)
    g_wu = sum_slots_into("rs_sum_wu1", land_a[1], where, None, (2, D, F4), 1)
    g_wd = sum_slots_into("rs_sum_wd1", land_a[2], where, None, (2, F4, D), 1)
    LBD = D // dm.LB
    NM = 2 * dm.LB * LBD
    mats = [dwa.reshape(1, NM, LBD), dwx.reshape(1, NM, LBD)]
    (dh2,), landed_x = fused_mm(
        "w_in_dx", (T // mT, D // bn, NS),
        [(dP, pl.BlockSpec((mT, W4), lambda i, j, k: (i, k))),
         (w_in, pl.BlockSpec((None, bn, W4), lambda i, j, k: (k, j, 0)))],
        [(0, 1, NT, 0)], [(mT, bn)], ident,
        [(_sds((T, D), F32), pl.BlockSpec((mT, bn), lambda i, j, k: (i, j)))],
        comm=merge_comms([rs_p3_comm([g_wg, g_wu, g_wd], [(0, 1, D // 2), (1, 1, D // 2), (2, 1, F4 // 2)]),
                          rs_p1_comm(mats)]))
    g_wg, g_wu, g_wd = landed_x[:3]
    pairs_m = [add_pair("rs_add_" + n_, g_, got_, where) for n_, g_, got_ in zip(('lru_wa', 'lru_wx'), mats, landed_x[3:])]
    (dw_in,), land_m = fused_mm(
        "w_in_dw", (D // bn, NS, T // kT),
        [(h2, pl.BlockSpec((kT, bn), lambda i, j, k: (k, i))), (dP, pl.BlockSpec((kT, W4), lambda i, j, k: (k, j)))],
        [(0, 1, TN, 0)], [(bn, W4)], ident,
        [(_sds((NS, D, W4), BF), pl.BlockSpec((None, bn, W4), lambda i, j, k: (j, i, 0)))],
        comm=rs_p2_comm([p_[0] for p_ in pairs_m], [p_[1] for p_ in pairs_m]))
    g_wa = sum_slots_into("rs_sum_lru_wa", land_m[0], where, None, (NM, LBD), None)
    g_wx = sum_slots_into("rs_sum_lru_wx", land_m[1], where, None, (NM, LBD), None)
    dxt1, dsh2, dsc2, dgn2, df1, dg1 = normmod_bwd("nm2b", dm, dh2, xt1, dx2, norm_g3, 1, modv, True, True,
                                                   gate=(f1, 2, FFN_RES))

    tens_b = [dw_in, dw_out.reshape(NS, Ds, D)]

    def host_ds(landed, made):
        return merge_comms([rs_p1_comm(tens_b), rs_p3_comm([g_wa, g_wx], [(0, None, NM // 2), (1, None, NM // 2)])])

    def host_dwgu(landed, made):
        pairs = [add_pair("rs_add_" + n_, g_, got_, where) for n_, g_, got_ in zip(('w_in', 'w_out'), tens_b, landed['ds'][:2])]
        return rs_p2_comm([p_[0] for p_ in pairs], [p_[1] for p_ in pairs])

    def host_dwd(landed, made):
        g_win = sum_slots_into("rs_sum_w_in", landed['dwgu'][0], where, None, (D, W4), None)
        g_wout = sum_slots_into("rs_sum_w_out", landed['dwgu'][1], where, None, (Ds, D), None)
        return merge_comms([rs_p3_comm([g_win, g_wout], [(0, None, D // 2), (1, None, Ds // 2)]),
                            rs_p1_comm([made['dwg'], made['dwu']])])

    def host_dh(landed, made):
        pairs = [add_pair("rs_add_" + n_, g_, got_, where)
                 for n_, g_, got_ in zip(('wg0', 'wu0'), (made['dwg'], made['dwu']), landed['dwd'][2:])]
        return merge_comms([rs_p2_comm([p_[0] for p_ in pairs], [p_[1] for p_ in pairs]), rs_p1_comm([made['dwd']])])

    dh1, dwg0, dwu0, dwd0, landed = ffn_bwd(
        "ffn1b", dm, df1, h1, a1, u1, s1, wg0, wu0, wd0, True,
        comms={'ds': host_ds, 'dwgu': host_dwgu, 'dwd': host_dwd, 'dh': host_dh})
    g_wa, g_wx = landed['ds'][2:]
    g_win, g_wout = landed['dwd'][:2]
    g_wg = sum_slots_into("rs_sum_wg0", landed['dh'][0], where, g_wg, (2, D, F4), 0)
    g_wu = sum_slots_into("rs_sum_wu0", landed['dh'][1], where, g_wu, (2, D, F4), 0)
    late = add_pair("rs_add_wd0", dwd0, landed['dh'][2], where)
    grad_x, dsh1, dsc1, dgn1 = normmod_bwd("nm1b", dm, dh1, xt, dxt1, norm_g3, 0, modv, True, False, out_lat_only=True)

    dmod = jnp.concatenate([dsh1, dsc1, dg1, dsh2, dsc2, _lat(dg2), _lat(dsh3), _lat(dsc3), _lat(dg3)], axis=1)
    dnorm = jnp.stack([dgn1[0, 0] + dgn1[1, 0], dgn2[0, 0] + dgn2[1, 0], dgn3[1, 0]], axis=0)
    small = dict(norm_g=dnorm, q_norm_g=dgq, k_norm_g=dgk, conv_w=dcw, conv_b=dcb,
                 lru_ba=dba.reshape(2, D), lru_bx=dbx.reshape(2, D), lru_lambda=dlam.reshape(2, D), final_norm_g=dgfin)
    reduced = dict(ffn_wg=g_wg, ffn_wu=g_wu, ffn_wd=g_wd, w_in=g_win, w_out=g_wout, lru_wa=g_wa, lru_wx=g_wx)
    return loss, grad_x, dmod, small, reduced, late


def _lat(v):
    return jnp.concatenate([jnp.zeros_like(v[:1]), v[1:]], axis=0)


def _me():
    return lax.axis_index("x"), lax.axis_index("y"), lax.axis_index("c")


def allgather8(name, v):
    def body(v_ref, out_ref, send_sems, recv_sems, local_sem):
        x, y, c = _me()
        me = 4 * x + 2 * y + c
        mine = pltpu.make_async_copy(v_ref, out_ref.at[me], local_sem)
        mine.start()
        copies = []
        for k in range(1, 8):
            peer = (x ^ ((k >> 2) & 1), y ^ ((k >> 1) & 1), c ^ (k & 1))
            cp = pltpu.make_async_remote_copy(src_ref=v_ref, dst_ref=out_ref.at[me], send_sem=send_sems.at[k - 1],
                                              recv_sem=recv_sems.at[k - 1], device_id=peer, device_id_type=MESH)
            cp.start()
            copies.append(cp)
        for k in range(1, 8):
            peer = (x ^ ((k >> 2) & 1), y ^ ((k >> 1) & 1), c ^ (k & 1))
            pltpu.make_async_remote_copy(src_ref=v_ref, dst_ref=out_ref.at[me ^ k], send_sem=send_sems.at[k - 1],
                                         recv_sem=recv_sems.at[k - 1], device_id=peer, device_id_type=MESH).wait_recv()
        for cp in copies:
            cp.wait_send()
        mine.wait()

    return pl.pallas_call(
        body, name=name, out_shape=_sds((8,) + v.shape, v.dtype), in_specs=[ANY], out_specs=ANY,
        scratch_shapes=[pltpu.SemaphoreType.DMA((7,)), pltpu.SemaphoreType.DMA((7,)), pltpu.SemaphoreType.DMA],
    )(v)


def _chips(x, y):
    chips = [(1 - x, y), (x, 1 - y), (1 - x, 1 - y)]
    return chips, [2 * cx + cy for cx, cy in chips]


def ag_comm(bufs):
    n = len(bufs)

    def parts(outs):
        x, y, c = _me()
        chips, slots = _chips(x, y)
        return x, y, c, 2 * x + y, (x, y, 1 - c), chips, slots

    def ici(outs, t, j, send_sems, recv_sems, src_slot, off):
        x, y, c, s, sib, chips, slots = parts(outs)
        H = outs[t].shape[1] // 2
        blk = outs[t].at[src_slot, pl.ds(c * H, H)]
        return pltpu.make_async_remote_copy(
            src_ref=blk, dst_ref=blk, send_sem=send_sems.at[off + 6 * t + j], recv_sem=recv_sems.at[off + 6 * t + j],
            device_id=(chips[j][0], chips[j][1], c), device_id_type=MESH)

    def d2d(outs, t, j, send_sems, recv_sems, half, off):
        x, y, c, s, sib, chips, slots = parts(outs)
        H = outs[t].shape[1] // 2
        blk = outs[t].at[slots[j], pl.ds(half * H, H)]
        return pltpu.make_async_remote_copy(
            src_ref=blk, dst_ref=blk, send_sem=send_sems.at[off + 6 * t + 3 + j],
            recv_sem=recv_sems.at[off + 6 * t + 3 + j], device_id=sib, device_id_type=MESH)

    def start(reads, outs, send_sems, recv_sems, off=0):
        x, y, c, s, sib, chips, slots = parts(outs)
        for t in range(n):
            for j in range(3):
                ici(outs, t, j, send_sems, recv_sems, s, off).start()

    def finish(reads, outs, send_sems, recv_sems, off=0):
        x, y, c, s, sib, chips, slots = parts(outs)
        for t in range(n):
            for j in range(3):
                ici(outs, t, j, send_sems, recv_sems, slots[j], off).wait_recv()
                d2d(outs, t, j, send_sems, recv_sems, c, off).start()
        for t in range(n):
            for j in range(3):
                d2d(outs, t, j, send_sems, recv_sems, 1 - c, off).wait_recv()
        for t in range(n):
            for j in range(3):
                ici(outs, t, j, send_sems, recv_sems, s, off).wait_send()
                d2d(outs, t, j, send_sems, recv_sems, c, off).wait_send()

    return Comm([], bufs, 6 * n, start, finish)


def rs_p1_comm(tensors):
    n = len(tensors)

    def copy(ins, gots, t, send_sems, recv_sems, off):
        x, y, c = _me()
        H = ins[t].shape[1] // 2
        return pltpu.make_async_remote_copy(
            src_ref=ins[t].at[:, pl.ds((1 - c) * H, H)], dst_ref=gots[t], send_sem=send_sems.at[off + t],
            recv_sem=recv_sems.at[off + t], device_id=(x, y, 1 - c), device_id_type=MESH)

    def start(ins, gots, send_sems, recv_sems, off=0):
        for t in range(n):
            copy(ins, gots, t, send_sems, recv_sems, off).start()

    def finish(ins, gots, send_sems, recv_sems, off=0):
        for t in range(n):
            copy(ins, gots, t, send_sems, recv_sems, off).wait_recv()
        for t in range(n):
            copy(ins, gots, t, send_sems, recv_sems, off).wait_send()

    half = lambda t: _sds((t.shape[0], t.shape[1] // 2) + t.shape[2:], t.dtype)
    return Comm(tensors, [half(t) for t in tensors], n, start, finish)


def rs_p2_comm(partials, landeds):
    n = len(partials)

    def start(ins, outs, send_sems, recv_sems, off=0):
        x, y, c = _me()
        s = 2 * x + y
        chips, slots = _chips(x, y)
        for t in range(n):
            for j, chip in enumerate(chips):
                src = ins[t].at[slots[j]] if ins[t].shape[0] == 4 else ins[t].at[0]
                pltpu.make_async_remote_copy(
                    src_ref=src, dst_ref=outs[t].at[s], send_sem=send_sems.at[off + 3 * t + j],
                    recv_sem=recv_sems.at[off + 3 * t + j], device_id=(chip[0], chip[1], c), device_id_type=MESH).start()

    def finish(ins, outs, send_sems, recv_sems, off=0):
        x, y, c = _me()
        s = 2 * x + y
        chips, slots = _chips(x, y)
        for t in range(n):
            for j, chip in enumerate(chips):
                dst = outs[t].at[slots[j]]
                pltpu.make_async_remote_copy(
                    src_ref=dst, dst_ref=dst, send_sem=send_sems.at[off + 3 * t + j],
                    recv_sem=recv_sems.at[off + 3 * t + j], device_id=(chip[0], chip[1], c), device_id_type=MESH).wait_recv()
        for t in range(n):
            for j, chip in enumerate(chips):
                src = ins[t].at[slots[j]] if ins[t].shape[0] == 4 else ins[t].at[0]
                pltpu.make_async_remote_copy(
                    src_ref=src, dst_ref=outs[t].at[s], send_sem=send_sems.at[off + 3 * t + j],
                    recv_sem=recv_sems.at[off + 3 * t + j], device_id=(chip[0], chip[1], c), device_id_type=MESH).wait_send()

    return Comm(partials, landeds, 3 * n, start, finish)


def rs_p3_comm(greds, plan):
    n = len(plan)

    def copy(outs, t, send_sems, recv_sems, half, off):
        x, y, c = _me()
        oi, li, H = plan[t]
        dst = outs[oi] if li is None else outs[oi].at[li]
        blk = dst.at[pl.ds((c if half == 0 else 1 - c) * H, H)]
        return pltpu.make_async_remote_copy(
            src_ref=blk, dst_ref=blk, send_sem=send_sems.at[off + t], recv_sem=recv_sems.at[off + t],
            device_id=(x, y, 1 - c), device_id_type=MESH)

    def start(reads, outs, send_sems, recv_sems, off=0):
        for t in range(n):
            copy(outs, t, send_sems, recv_sems, 0, off).start()

    def finish(reads, outs, send_sems, recv_sems, off=0):
        for t in range(n):
            copy(outs, t, send_sems, recv_sems, 1, off).wait_recv()
        for t in range(n):
            copy(outs, t, send_sems, recv_sems, 0, off).wait_send()

    return Comm([], greds, n, start, finish)


def _rows_block(rows, cols, nbytes=1 << 20):
    unit = 16 if rows % 16 == 0 else 8
    best = unit
    for bm in range(unit, rows + 1, unit):
        if rows % bm == 0 and bm * cols * 4 <= nbytes:
            best = bm
    return best


def cast_into_slot(name, w, where, layer=None):
    rows, W = w.shape[-2:]
    bm = _rows_block(rows, W, 4 << 20)

    def body(p_ref, w_ref, o_ref):
        o_ref[...] = w_ref[...].astype(BF)

    if layer is None:
        ispec = pl.BlockSpec((bm, W), lambda i, p: (i, 0))
    else:
        ispec = pl.BlockSpec((None, bm, W), lambda i, p: (layer, i, 0))
    return pl.pallas_call(
        body, name=name, out_shape=_sds((4, rows, W), BF), compiler_params=_cparams(),
        grid_spec=pltpu.PrefetchScalarGridSpec(
            num_scalar_prefetch=1, grid=(rows // bm,), in_specs=[ispec],
            out_specs=pl.BlockSpec((None, bm, W), lambda i, p: (p[1], i, 0))),
    )(where, w)


def add_pair(name, g, got, where):
    K, R, W = g.shape
    H = R // 2
    bm = _rows_block(H, W, 4 << 20)
    nh = H // bm

    def body(p_ref, g_ref, got_ref, part_ref, land_ref):
        k = pl.program_id(1)
        v = (g_ref[...].astype(F32) + got_ref[...].astype(F32)).astype(part_ref.dtype)
        part_ref[...] = v
        own = (k == p_ref[1]) if K == 4 else (k == 0)

        @pl.when(own)
        def _():
            land_ref[...] = v

    return pl.pallas_call(
        body, name=name, out_shape=[_sds((K, H, W), g.dtype), _sds((4, H, W), g.dtype)], compiler_params=_cparams(),
        grid_spec=pltpu.PrefetchScalarGridSpec(
            num_scalar_prefetch=1, grid=(nh, K),
            in_specs=[pl.BlockSpec((None, bm, W), lambda i, k, p: (k, p[0] * nh + i, 0)),
                      pl.BlockSpec((None, bm, W), lambda i, k, p: (k, i, 0))],
            out_specs=[pl.BlockSpec((None, bm, W), lambda i, k, p: (k, i, 0)),
                       pl.BlockSpec((None, bm, W), lambda i, k, p: (p[1], i, 0))]),
    )(where, g, got)


def sum_slots_into(name, landed, where, dest, dest_shape, li):
    K, H, W = landed.shape
    bm = _rows_block(H, 2 * W, 4 << 20)
    nh = H // bm

    def body(*refs):
        r, o_ref = refs[1], refs[-1]
        acc = r[0].astype(F32)
        for k in range(1, K):
            acc = acc + r[k].astype(F32)
        o_ref[...] = acc

    if li is None:
        ospec = pl.BlockSpec((bm, W), lambda i, p: (p[0] * nh + i, 0))
    else:
        ospec = pl.BlockSpec((None, bm, W), lambda i, p: (li, p[0] * nh + i, 0))
    in_specs = [pl.BlockSpec((K, bm, W), lambda i, p: (0, i, 0))]
    args = [where, landed]
    aliases = {}
    if dest is not None:
        in_specs.append(ANY)
        args.append(dest)
        aliases = {2: 0}
    return pl.pallas_call(
        body, name=name, out_shape=_sds(dest_shape, F32), compiler_params=_cparams(), input_output_aliases=aliases,
        grid_spec=pltpu.PrefetchScalarGridSpec(num_scalar_prefetch=1, grid=(nh,), in_specs=in_specs, out_specs=ospec),
    )(*args)


def sum_slots(name, a):
    K, H, W = a.shape
    bm = _rows_block(H, W * K // 2)

    def fn(ids, r):
        acc = r[0]
        for k in range(1, K):
            acc = acc + r[k]
        return [acc]

    return ew_call(name, (H // bm,), fn, [(a, pl.BlockSpec((K, bm, W), lambda i: (0, i, 0)))],
                   [(_sds((H, W), F32), pl.BlockSpec((bm, W), lambda i: (i, 0)), False)])[0]


def _adamw_math(w, g, m, v):
    bc1 = 1.0 - ADAM_B1 ** ADAM_STEP
    bc2 = 1.0 - ADAM_B2 ** ADAM_STEP
    mn = ADAM_B1 * m + (1.0 - ADAM_B1) * g
    vn = ADAM_B2 * v + (1.0 - ADAM_B2) * (g * g)
    m_hat = mn / bc1
    v_hat = vn / bc2
    delta = -ADAM_LR * (m_hat / (jnp.sqrt(v_hat) + ADAM_EPS) + ADAM_WD * w)
    return delta, mn, vn


def adamw(name, w, g, m, v, copy_grad=False):
    shape = w.shape
    flat = lambda t: t.reshape(-1, shape[-1])
    w2, g2, m2, v2 = flat(w), flat(g), flat(m), flat(v)
    bm = _rows_block(w2.shape[0], w2.shape[1])

    def fn(ids, w_ref, g_ref, m_ref, v_ref):
        gv = g_ref[...]
        return list(_adamw_math(w_ref[...], gv, m_ref[...], v_ref[...])) + ([gv] if copy_grad else [])

    spec = pl.BlockSpec((bm, w2.shape[1]), lambda i: (i, 0))
    res = ew_call(name, (w2.shape[0] // bm,), fn, [(w2, spec), (g2, spec), (m2, spec), (v2, spec)],
                  [(_sds(w2.shape, F32), spec, False)] * (4 if copy_grad else 3))
    return [o.reshape(shape) for o in res]


def adamw_many(name, params):
    n = len(params)
    shapes = [p_[0].shape for p_ in params]
    two_d = lambda t: t.reshape(-1, t.shape[-1])
    args = [two_d(t) for p_ in params for t in p_]

    def body(*refs):
        ins, outs = refs[:4 * n], refs[4 * n:]
        for q in range(n):
            w_ref, g_ref, m_ref, v_ref = ins[4 * q:4 * q + 4]
            for o_ref, val in zip(outs[3 * q:3 * q + 3], _adamw_math(w_ref[...], g_ref[...], m_ref[...], v_ref[...])):
                o_ref[...] = val

    full = lambda a: pl.BlockSpec(a.shape, lambda i: (0, 0))
    out_shape = [_sds(args[4 * q].shape, F32) for q in range(n) for _ in range(3)]
    res = hosted_call(body, name=name, grid=(1,), in_specs=[full(a) for a in args],
                      out_specs=[full(o) for o in out_shape], out_shape=out_shape, args=args)
    return [tuple(res[3 * q + r].reshape(shapes[q]) for r in range(3)) for q in range(n)]


def dmod_pack(gd):
    N = gd.shape[-1]
    bn = _pick(N, [4608, 2304, 1152, 1024, 512, 256, 128])

    def fn(ids, r):
        lat = [r[d, 1:2, :] for d in range(8)]
        cs = r[0, 0:1, :]
        for d in range(1, 8):
            cs = cs + r[d, 0:1, :]
        tot = cs
        for d in range(8):
            tot = tot + lat[d]
        return [jnp.concatenate(lat + [cs, jnp.zeros((7, bn), F32)], axis=0), tot]

    return ew_call("dmod_pack", (N // bn,), fn, [(gd, pl.BlockSpec((8, 2, bn), lambda j: (0, 0, j)))],
                   [(_sds((16, N), F32), pl.BlockSpec((16, bn), lambda j: (0, j)), False),
                    (_sds((1, N), F32), pl.BlockSpec((1, bn), lambda j: (0, j)), False)])


def _silu(v):
    return v * _sig(v)


def kernel(x, c, ctx, c_ctx, w_mod, b_mod, norm_g, ffn_wg, ffn_wu, ffn_wd, w_in, w_out, q_norm_g, k_norm_g, conv_w, conv_b, lru_wa, lru_ba, lru_wx, lru_bx, lru_lambda, final_norm_g, loss_target, m_c_ctx, m_w_mod, m_b_mod, m_norm_g, m_ffn_wg, m_ffn_wu, m_ffn_wd, m_w_in, m_w_out, m_q_norm_g, m_k_norm_g, m_conv_w, m_conv_b, m_lru_wa, m_lru_ba, m_lru_wx, m_lru_bx, m_lru_lambda, m_final_norm_g, v_c_ctx, v_w_mod, v_b_mod, v_norm_g, v_ffn_wg, v_ffn_wu, v_ffn_wd, v_w_in, v_w_out, v_q_norm_g, v_k_norm_g, v_conv_w, v_conv_b, v_lru_wa, v_lru_ba, v_lru_wx, v_lru_bx, v_lru_lambda, v_final_norm_g):
    given = dict(locals())
    names = ['c_ctx', 'w_mod', 'b_mod', 'norm_g', 'ffn_wg', 'ffn_wu', 'ffn_wd', 'w_in', 'w_out', 'q_norm_g', 'k_norm_g',
             'conv_w', 'conv_b', 'lru_wa', 'lru_ba', 'lru_wx', 'lru_bx', 'lru_lambda', 'final_norm_g']
    S, D = x.shape[1], x.shape[2]
    C = ctx.shape[1]
    NS = 4
    F4, W4, LB = ffn_wg.shape[-1], w_in.shape[-1], lru_wa.shape[2]
    dm = Dims(S, C, D, F4, W4, NS, LB)
    Ds = D // NS
    Wm = w_mod.shape[-1]
    xi, yi, ci = lax.axis_index("x"), lax.axis_index("y"), lax.axis_index("c")
    slot = 2 * xi + yi
    me = 4 * xi + 2 * yi + ci
    ident = lambda ids, accs, ex: list(accs)

    pack1 = jnp.concatenate([c.reshape(-1), norm_g.reshape(-1), conv_w.reshape(-1), lru_ba.reshape(-1),
                             lru_bx.reshape(-1), lru_lambda.reshape(-1)]).reshape(1, -1)
    g1 = allgather8("ag_small_params", pack1)[:, 0]
    c_all = g1[:, :D]

    def unshard(off, k):
        part = g1[0::2, off:off + k * Ds].reshape(NS, k, Ds)
        return jnp.transpose(part, (1, 0, 2)).reshape(k, D)

    norm_g_f = unshard(D, 3)
    conv_w_f = unshard(D + 3 * Ds, 4)
    ba_f = unshard(D + 7 * Ds, 2)
    bx_f = unshard(D + 9 * Ds, 2)
    lam_f = unshard(D + 11 * Ds, 2)

    call16 = jnp.concatenate([c_all, c_ctx.reshape(1, D), jnp.zeros((7, D), F32)], axis=0)
    b_cols = lax.dynamic_slice(b_mod, (0, slot * Wm), (1, Wm))
    bnm = _pick(Wm, [1536, 1152, 768, 512, 384, 256, 128])
    bkm = _pick(D, [512, 256, 128])
    modp = fused_mm(
        "mod_fwd", (1, Wm // bnm, D // bkm),
        [(call16, pl.BlockSpec((16, bkm), lambda i, j, k: (0, k))),
         (w_mod[0], pl.BlockSpec((bkm, bnm), lambda i, j, k: (k, j)))],
        [(0, 1, NN, 0)], [(16, bnm)], lambda ids, accs, ex: [accs[0] + ex[0][...]],
        [(_sds((16, Wm), F32), pl.BlockSpec((16, bnm), lambda i, j, k: (0, j)))],
        extras=[(b_cols, pl.BlockSpec((1, bnm), lambda i, j, k: (0, j)))], pre={0: _silu})[0]
    gm = allgather8("ag_mod", modp)
    mod_full = jnp.concatenate([gm[0], gm[2], gm[4], gm[6]], axis=1)
    mod_x = lax.dynamic_index_in_dim(mod_full, me, axis=0, keepdims=False)
    modv = jnp.stack([mod_full[8], mod_x]).reshape(2, N_MOD, 1, D)

    where = jnp.stack([ci, slot]).astype(jnp.int32)
    wbuf = {}
    for key, short in (('ffn_wg', 'wg'), ('ffn_wu', 'wu'), ('ffn_wd', 'wd')):
        for l in range(2):
            wbuf[short + str(l)] = cast_into_slot("cast_%s%d" % (short, l), given[key][0], where, l)
    wbuf['w_in'] = cast_into_slot("cast_w_in", w_in[0], where)
    wbuf['w_out'] = cast_into_slot("cast_w_out", w_out[0], where)

    loss_l, grad_x, dmod, small, reduced, late = local_step(
        dm, x[0], ctx[0], loss_target[0], modv, norm_g_f.reshape(3, 1, D), final_norm_g.reshape(1, D),
        q_norm_g, k_norm_g, conv_w_f, conv_b, lru_wa[0], ba_f.reshape(2, 1, D), lru_wx[0], bx_f.reshape(2, 1, D),
        lam_f.reshape(2, 1, D), wbuf, where)
    loss = lax.psum(loss_l[0, 0], ("x", "y", "c"))

    grads = {}
    gd = allgather8("ag_dmod", dmod.reshape(2, N_MOD * D))
    dM, g_bmod = dmod_pack(gd)
    dMc = lax.dynamic_slice(dM, (0, slot * Wm), (16, Wm))
    bmm = _pick(D, [512, 256, 128])
    grads['w_mod'] = fused_mm(
        "w_mod_dw", (D // bmm, Wm // bnm, 1),
        [(call16, pl.BlockSpec((16, bmm), lambda i, j, k: (0, i))), (dMc, pl.BlockSpec((16, bnm), lambda i, j, k: (0, j)))],
        [(0, 1, TN, 0)], [(bmm, bnm)], ident,
        [(_sds((D, Wm), F32), pl.BlockSpec((bmm, bnm), lambda i, j, k: (i, j)))], pre={0: _silu})[0][None]
    grads['b_mod'] = g_bmod

    def epi_cc(ids, accs, ex):
        v = ex[0][...]
        sg = _sig(v)
        return [accs[0] * (sg * (1.0 + v * (1.0 - sg)))]

    pcc = fused_mm(
        "c_ctx_partial", (1, D // bmm, Wm // bnm),
        [(dMc, pl.BlockSpec((16, bnm), lambda i, j, k: (0, k))), (w_mod[0], pl.BlockSpec((bmm, bnm), lambda i, j, k: (j, k)))],
        [(0, 1, NT, 0)], [(16, bmm)], epi_cc,
        [(_sds((16, D), F32), pl.BlockSpec((16, bmm), lambda i, j, k: (0, j)))],
        extras=[(c_ctx.reshape(1, D), pl.BlockSpec((1, bmm), lambda i, j, k: (0, j)))])[0]
    pcc_row = jnp.where(ci == 0, pcc[8], 0.0)

    order = ['q_norm_g', 'k_norm_g', 'conv_b', 'final_norm_g', 'norm_g', 'conv_w', 'lru_ba', 'lru_bx', 'lru_lambda']
    flat = [small[k].reshape(-1) for k in order] + [pcc_row]
    sizes = [f.shape[0] for f in flat]
    tot = sum(sizes)
    LW = 1024
    padded = -(-tot // (8 * LW)) * (8 * LW)
    tiny = jnp.concatenate(flat + [jnp.zeros((padded - tot,), F32)]).reshape(-1, LW)
    summed = sum_slots("tiny_sum", allgather8("ag_tiny_grads", tiny)).reshape(-1)
    offs = {}
    o = 0
    for k, n_ in zip(order + ['c_ctx'], sizes):
        offs[k] = summed[o:o + n_]
        o += n_
    shard = lambda k, rows: lax.dynamic_slice_in_dim(offs[k].reshape(rows, D), slot * Ds, Ds, axis=1)
    grads['c_ctx'] = offs['c_ctx']
    grads['q_norm_g'] = offs['q_norm_g'].reshape(1, HEAD_DIM)
    grads['k_norm_g'] = offs['k_norm_g'].reshape(1, HEAD_DIM)
    grads['conv_b'] = offs['conv_b'].reshape(1, D)
    grads['final_norm_g'] = offs['final_norm_g']
    grads['norm_g'] = shard('norm_g', 3)[None]
    grads['conv_w'] = shard('conv_w', 4)[None]
    grads['lru_ba'] = shard('lru_ba', 2)[None]
    grads['lru_bx'] = shard('lru_bx', 2)[None]
    grads['lru_lambda'] = shard('lru_lambda', 2)[None]

    landed = comm_call("rs_tail_p2", rs_p2_comm([late[0]], [late[1]]))
    g_wd = sum_slots_into("rs_sum_wd0", landed[0], where, reduced['ffn_wd'], (2, F4, D), 0)
    g_wg, g_wu, g_wd = comm_call("rs_tail_p3", rs_p3_comm(
        [reduced['ffn_wg'], reduced['ffn_wu'], g_wd], [(0, 0, D // 2), (1, 0, D // 2), (2, 0, F4 // 2)]))
    grads.update(ffn_wg=g_wg[None], ffn_wu=g_wu[None], ffn_wd=g_wd[None], w_in=reduced['w_in'][None],
                 w_out=reduced['w_out'][None], lru_wa=reduced['lru_wa'].reshape(lru_wa.shape),
                 lru_wx=reduced['lru_wx'].reshape(lru_wx.shape))

    delta, new_m, new_v = {}, {}, {}
    big_names = ['w_mod', 'w_in', 'w_out', 'lru_wa', 'lru_wx', 'ffn_wg', 'ffn_wu', 'ffn_wd']
    for k in big_names:
        res = adamw("adamw_" + k, given[k], grads[k], given['m_' + k], given['v_' + k], copy_grad=(k != 'w_mod'))
        delta[k], new_m[k], new_v[k] = res[:3]
        if k != 'w_mod':
            grads[k] = res[3]
    tiny_names = [k for k in names if k not in big_names]
    res = adamw_many("adamw_tiny", [(given[k], grads[k], given['m_' + k], given['v_' + k]) for k in tiny_names])
    for k, (d_, m_, v_) in zip(tiny_names, res):
        delta[k], new_m[k], new_v[k] = d_, m_, v_

    return (loss, grad_x[None], *[grads[k] for k in names], *[delta[k] for k in names],
            *[new_m[k] for k in names], *[new_v[k] for k in names])
```

```python
import functools

import jax
import jax.numpy as jnp
from jax import lax
from jax.experimental import pallas as pl
from jax.experimental.pallas import tpu as pltpu

F32 = jnp.float32
BF = jnp.bfloat16
EPS = 1e-6
HEAD_DIM = 128
GRID_W = 64
ROPE_THETA = 10000.0
LRU_C = 8.0
FFN_RES = 0.5
N_MOD = 9
LOG2_E = 1.4426950408889634
ADAM_LR, ADAM_B1, ADAM_B2, ADAM_EPS, ADAM_WD, ADAM_STEP = 0.001, 0.9, 0.999, 1e-08, 0.01, 10
VMEM_LIMIT = 52 * 1024 * 1024
MESH = pl.DeviceIdType.MESH
ANY = pl.BlockSpec(memory_space=pl.ANY)


def _sds(shape, dt):
    return jax.ShapeDtypeStruct(tuple(shape), dt)


def _pick(n, cands):
    for c in cands:
        if n % c == 0:
            return c
    return n


def _cparams(**kw):
    return pltpu.CompilerParams(vmem_limit_bytes=VMEM_LIMIT, **kw)


def _sig(x):
    return 1.0 / (1.0 + jnp.exp(-x))


def _gelu(x):
    t = jnp.tanh(0.7978845608028654 * (x + 0.044715 * x * x * x))
    return 0.5 * x * (1.0 + t), t


def _gelu_grad(x, t):
    return 0.5 * (1.0 + t) + 0.5 * x * (1.0 - t * t) * 0.7978845608028654 * (1.0 + 3.0 * 0.044715 * x * x)


class Comm:
    def __init__(self, reads, lands, n_sem, start, finish):
        self.reads, self.lands, self.n_sem, self.start, self.finish = list(reads), list(lands), n_sem, start, finish


def merge_comms(comms):
    reads = [r for c in comms for r in c.reads]
    lands = [l for c in comms for l in c.lands]

    def run(which):
        def fn(r, lo, send_sems, recv_sems, off=0):
            ro = lo_ = so = 0
            for c in comms:
                getattr(c, which)(r[ro:ro + len(c.reads)], lo[lo_:lo_ + len(c.lands)], send_sems, recv_sems, off + so)
                ro, lo_, so = ro + len(c.reads), lo_ + len(c.lands), so + c.n_sem
        return fn

    return Comm(reads, lands, sum(c.n_sem for c in comms), run('start'), run('finish'))


def hosted_call(body, *, name, grid, in_specs, out_specs, out_shape, args, scratch_shapes=(), aliases=None, comm=None):
    aliases = dict(aliases or {})
    if comm is None:
        return pl.pallas_call(
            body, name=name, grid=grid, in_specs=list(in_specs), out_specs=list(out_specs), out_shape=list(out_shape),
            scratch_shapes=list(scratch_shapes), input_output_aliases=aliases, compiler_params=_cparams())(*args)
    n_in, n_out, n_sc = len(args), len(out_shape), len(scratch_shapes)
    land_in = [(t, l) for t, l in enumerate(comm.lands) if not isinstance(l, jax.ShapeDtypeStruct)]
    nr, nli, nl = len(comm.reads), len(land_in), len(comm.lands)

    def wrapped(*refs):
        a = refs[:n_in]
        r = refs[n_in:n_in + nr]
        pos = n_in + nr + nli
        o = refs[pos:pos + n_out]
        lo = refs[pos + n_out:pos + n_out + nl]
        sc = refs[pos + n_out + nl:pos + n_out + nl + n_sc]
        send_sems, recv_sems = refs[pos + n_out + nl + n_sc:]
        ids = [pl.program_id(d) for d in range(len(grid))]
        first, last = ids[0] == 0, ids[0] == grid[0] - 1
        for d in range(1, len(grid)):
            first = first & (ids[d] == 0)
            last = last & (ids[d] == grid[d] - 1)

        @pl.when(first)
        def _():
            comm.start(r, lo, send_sems, recv_sems)

        body(*a, *o, *sc)

        @pl.when(last)
        def _():
            comm.finish(r, lo, send_sems, recv_sems)

    for q, (t, _) in enumerate(land_in):
        aliases[n_in + nr + q] = n_out + t
    res = pl.pallas_call(
        wrapped, name=name, grid=grid,
        in_specs=list(in_specs) + [ANY] * (nr + nli), out_specs=list(out_specs) + [ANY] * nl,
        out_shape=list(out_shape) + [l if isinstance(l, jax.ShapeDtypeStruct) else _sds(l.shape, l.dtype) for l in comm.lands],
        scratch_shapes=list(scratch_shapes) + [pltpu.SemaphoreType.DMA((comm.n_sem,)), pltpu.SemaphoreType.DMA((comm.n_sem,))],
        input_output_aliases=aliases, compiler_params=_cparams(),
    )(*args, *comm.reads, *[l for _, l in land_in])
    return list(res[:n_out]), list(res[n_out:])


def comm_call(name, comm):
    def body():
        pass

    return hosted_call(body, name=name, grid=(1,), in_specs=[], out_specs=[], out_shape=[], args=[], comm=comm)[1]


def ew_call(name, grid, fn, ins, outs, first=None, aliases=None, comm=None):
    n_in = len(ins)

    def body(*refs):
        ids = tuple(pl.program_id(a) for a in range(len(grid)))
        vals = fn(ids, *refs[:n_in])
        for (_, _, acc), o_ref, v in zip(outs, refs[n_in:], vals):
            if not acc:
                o_ref[...] = v.astype(o_ref.dtype)
            else:
                is_first = first(ids)

                @pl.when(is_first)
                def _(o_ref=o_ref, v=v):
                    o_ref[...] = v.astype(o_ref.dtype)

                @pl.when(jnp.logical_not(is_first))
                def _(o_ref=o_ref, v=v):
                    o_ref[...] += v.astype(o_ref.dtype)

    return hosted_call(body, name=name, grid=grid, in_specs=[s for _, s in ins], out_specs=[s for _, s, _ in outs],
                       out_shape=[o for o, _, _ in outs], args=[a for a, _ in ins], aliases=aliases, comm=comm)


def fused_mm(name, grid, ins, prods, acc_shapes, epi, outs, extras=(), pre=None, comm=None, row_split=1):
    n_in, n_ex, n_out = len(ins), len(extras), len(outs)
    nk = grid[-1]
    pre = pre or {}
    n_acc = len(acc_shapes)

    def body(*refs):
        in_refs = refs[:n_in]
        ex_refs = refs[n_in:n_in + n_ex]
        out_refs = refs[n_in + n_ex:n_in + n_ex + n_out]
        accs = refs[n_in + n_ex + n_out:]
        ids = tuple(pl.program_id(a) for a in range(len(grid)))
        k = ids[-1]
        loaded = {}

        def operand(i):
            if i not in loaded:
                v = in_refs[i][...]
                if i in pre:
                    v = pre[i](v)
                loaded[i] = v.astype(BF)
            return loaded[i]

        def product(ia, ib, dims):
            return lax.dot_general(operand(ia), operand(ib), (dims, ((), ())), preferred_element_type=F32)

        if nk == 1:
            rows = acc_shapes[0][0]
            step = rows // row_split
            parts = [slice(p * step, (p + 1) * step) for p in range(row_split)]
            all_sums = []
            for rs in parts:
                sums = [None] * n_acc
                for ia, ib, dims, ai in prods:
                    lhs = operand(ia) if row_split == 1 else operand(ia)[rs]
                    d = lax.dot_general(lhs, operand(ib), (dims, ((), ())), preferred_element_type=F32)
                    sums[ai] = d if sums[ai] is None else sums[ai] + d
                all_sums.append(sums)
            for rs, sums in zip(parts, all_sums):
                ex = ex_refs if row_split == 1 else [e.at[rs] if e.shape[0] == rows else e for e in ex_refs]
                for o_ref, v in zip(out_refs, epi(ids, sums, ex)):
                    if row_split == 1:
                        o_ref[...] = v.astype(o_ref.dtype)
                    else:
                        o_ref[rs] = v.astype(o_ref.dtype)
            return

        @pl.when(k == 0)
        def _():
            for a in accs:
                a[...] = jnp.zeros(a.shape, F32)

        for ia, ib, dims, ai in prods:
            accs[ai][...] += product(ia, ib, dims)

        @pl.when(k == nk - 1)
        def _():
            vals = epi(ids, [a[...] for a in accs], ex_refs)
            for o_ref, v in zip(out_refs, vals):
                o_ref[...] = v.astype(o_ref.dtype)

    return hosted_call(
        body, name=name, grid=grid, in_specs=[s for _, s in ins] + [s for _, s in extras],
        out_specs=[s for _, s in outs], out_shape=[o for o, _ in outs],
        scratch_shapes=[pltpu.VMEM(s, F32) for s in acc_shapes] if nk > 1 else [],
        args=[a for a, _ in ins] + [a for a, _ in extras], comm=comm)


NN = ((1,), (0,))
NT = ((1,), (1,))
TN = ((0,), (0,))


class Dims:
    def __init__(self, S, C, D, F4, W4, NS, LB):
        self.S, self.C, self.D, self.F4, self.W4, self.NS, self.LB = S, C, D, F4, W4, NS, LB
        self.T = S + C
        self.DFF = F4 * NS
        self.INW = W4 * NS
        self.NQ = D // HEAD_DIM
        self.KVW = (self.INW - 5 * D) // 2
        self.NKV = self.KVW // HEAD_DIM
        self.G = self.NQ // self.NKV
        self.OFF_K = D
        self.OFF_V = D + self.KVW
        self.OFF_LX = D + 2 * self.KVW
        self.OFF_LG = self.OFF_LX + D
        self.OFF_GA = self.OFF_LG + D
        self.OFF_GL = self.OFF_GA + D
        self.bm = _pick(C, [256, 128, 64, 32, 16, 8])
        self.nCb = C // self.bm
        self.nTb = self.T // self.bm
        self.nSb = S // self.bm
        self.mT = _pick(self.T, [544, 512, 384, 256, 128])
        self.mS = _pick(S, [512, 256, 128])
        self.kT = _pick(self.T, [1088, 1024, 768, 544, 512, 384, 256, 128])
        self.kS = _pick(S, [1024, 512, 256, 128])
        self.cw = _pick(D, [1024, 512, 256, 128]) if (self.OFF_LX % 1024 == 0 and D % 1024 == 0) else _pick(
            self.OFF_LX, [512, 256, 128])
        self.nsub = 2 if (W4 % 256 == 0 and W4 >= 512) else 1
        self.wb = W4 // self.nsub
        self.LBD = D // LB
        self.bq = _pick(C, [256, 128]) if S % _pick(C, [256, 128]) == 0 else 128


def rope_tables(dm):
    rows = dm.S // GRID_W
    row = jnp.repeat(jnp.arange(rows, dtype=F32), GRID_W)
    col = jnp.tile(jnp.arange(GRID_W, dtype=F32), rows)
    axis_dims = HEAD_DIM // 2
    freqs = ROPE_THETA ** (-jnp.arange(0, axis_dims, 2, dtype=F32) / axis_dims)
    ang = jnp.concatenate([row[:, None] * freqs, col[:, None] * freqs], axis=-1)
    cos = jnp.repeat(jnp.cos(ang), 2, axis=-1)
    sin = jnp.repeat(jnp.sin(ang), 2, axis=-1)
    sign = jnp.tile(jnp.array([-1.0, 1.0], F32), HEAD_DIM // 2)
    sin = sin * sign
    cos = jnp.concatenate([jnp.ones((dm.C, HEAD_DIM), F32), cos], axis=0)
    sin = jnp.concatenate([jnp.zeros((dm.C, HEAD_DIM), F32), sin], axis=0)
    return cos, sin


def _pair_swap(y):
    lane = lax.broadcasted_iota(jnp.int32, y.shape, 1)
    nxt = pltpu.roll(y, y.shape[1] - 1, 1)
    prv = pltpu.roll(y, 1, 1)
    return jnp.where((lane & 1) == 0, nxt, prv)


def normmod_fwd(name, dm, x, norm_g3, stage, modv, rows_T):
    D, bm = dm.D, dm.bm
    nb = dm.nTb if rows_T else dm.nSb
    typ = (lambda i: jnp.where(i < dm.nCb, 0, 1)) if rows_T else (lambda i: 1)

    def fn(ids, x_ref, g_ref, sh_ref, sc_ref):
        xv = x_ref[...]
        r = lax.rsqrt(jnp.mean(xv * xv, axis=-1, keepdims=True) + EPS)
        n = xv * r * g_ref[...]
        return [n * (1.0 + sc_ref[...]) + sh_ref[...]]

    return ew_call(
        name, (nb,), fn,
        [(x, pl.BlockSpec((bm, D), lambda i: (i, 0))),
         (norm_g3, pl.BlockSpec((None, 1, D), lambda i: (stage, 0, 0))),
         (modv, pl.BlockSpec((None, None, 1, D), lambda i: (typ(i), 3 * stage, 0, 0))),
         (modv, pl.BlockSpec((None, None, 1, D), lambda i: (typ(i), 3 * stage + 1, 0, 0)))],
        [(_sds(x.shape, BF), pl.BlockSpec((bm, D), lambda i: (i, 0)), False)])[0]


def normmod_concat_fwd(name, dm, ctx, x, norm_g3, modv):
    D, bm, nCb = dm.D, dm.bm, dm.nCb
    typ = lambda i: jnp.where(i < nCb, 0, 1)

    def fn(ids, c_ref, x_ref, g_ref, sh_ref, sc_ref):
        xv = jnp.where(ids[0] < nCb, c_ref[...], x_ref[...])
        r = lax.rsqrt(jnp.mean(xv * xv, axis=-1, keepdims=True) + EPS)
        n = xv * r * g_ref[...]
        return [n * (1.0 + sc_ref[...]) + sh_ref[...], xv]

    row = pl.BlockSpec((bm, D), lambda i: (i, 0))
    return ew_call(
        name, (dm.nTb,), fn,
        [(ctx, pl.BlockSpec((bm, D), lambda i: (jnp.minimum(i, nCb - 1), 0))),
         (x, pl.BlockSpec((bm, D), lambda i: (jnp.maximum(i - nCb, 0), 0))),
         (norm_g3, pl.BlockSpec((None, 1, D), lambda i: (0, 0, 0))),
         (modv, pl.BlockSpec((None, None, 1, D), lambda i: (typ(i), 0, 0, 0))),
         (modv, pl.BlockSpec((None, None, 1, D), lambda i: (typ(i), 1, 0, 0)))],
        [(_sds((dm.T, D), BF), row, False), (_sds((dm.T, D), F32), row, False)])


def normmod_bwd(name, dm, dh, x, dres, norm_g3, stage, modv, rows_T, dres_lat_only, out_lat_only=False, gate=None):
    D, bm = dm.D, dm.bm
    nb = dm.nTb if rows_T else dm.nSb
    nCb = dm.nCb
    typ = (lambda i: jnp.where(i < nCb, 0, 1)) if rows_T else (lambda i: 1)
    if dres_lat_only:
        dres_map = lambda i: (jnp.maximum(i - nCb, 0), 0)
    else:
        dres_map = lambda i: (i, 0)

    def fn(ids, dh_ref, x_ref, dres_ref, g_ref, sc_ref, *gate_refs):
        i = ids[0]
        xv = x_ref[...]
        dhv = dh_ref[...].astype(F32)
        r = lax.rsqrt(jnp.mean(xv * xv, axis=-1, keepdims=True) + EPS)
        xn = xv * r
        g = g_ref[...]
        n = xn * g
        dn = dhv * (1.0 + sc_ref[...])
        dxn = dn * g
        dx = r * (dxn - xn * jnp.mean(dxn * xn, axis=-1, keepdims=True))
        dresv = dres_ref[...]
        if dres_lat_only:
            dresv = jnp.where(i >= nCb, dresv, 0.0)
        dsh = jnp.sum(dhv, axis=0, keepdims=True)
        dsc = jnp.sum(dhv * n, axis=0, keepdims=True)
        dg = jnp.sum(dn * xn, axis=0, keepdims=True)
        dxt = dx + dresv
        res = [dxt, dsh, dsc, dg]
        if gate is not None:
            f_ref, gv_ref = gate_refs
            res += [gate[2] * gv_ref[...] * dxt, jnp.sum(gate[2] * f_ref[...].astype(F32) * dxt, axis=0, keepdims=True)]
        return res

    if rows_T:
        first = lambda ids: (ids[0] == 0) | (ids[0] == nCb)
    else:
        first = lambda ids: ids[0] == 0
    row = pl.BlockSpec((bm, D), lambda i: (i, 0))
    acc = (_sds((2, 1, D), F32), pl.BlockSpec((None, 1, D), lambda i: (typ(i), 0, 0)), True)
    ins = [(dh, row), (x, row), (dres, pl.BlockSpec((bm, D), dres_map)),
           (norm_g3, pl.BlockSpec((None, 1, D), lambda i: (stage, 0, 0))),
           (modv, pl.BlockSpec((None, None, 1, D), lambda i: (typ(i), 3 * stage + 1, 0, 0)))]
    outs = [(_sds((dm.S, D) if out_lat_only else x.shape, F32),
             pl.BlockSpec((bm, D), (lambda i: (jnp.maximum(i - nCb, 0), 0)) if out_lat_only else (lambda i: (i, 0))), False),
            acc, acc, acc]
    if gate is not None:
        ins += [(gate[0], row), (modv, pl.BlockSpec((None, None, 1, D), lambda i: (typ(i), gate[1], 0, 0)))]
        outs += [(_sds(x.shape, BF), row, False), acc]
    return ew_call(name, (nb,), fn, ins, outs, first=first)


def ffn_fwd(name, dm, h, xres, wg, wu, wd, modv, gidx, rows_T, comm_up=None, comm_down=None):
    D, F4, NS = dm.D, dm.F4, dm.NS
    M = h.shape[0]
    bm = dm.mT if rows_T else dm.mS
    C = dm.C

    def epi_up(ids, accs, ex):
        a, u = accs
        return [a, u, a * _sig(a) * u]

    hspec = pl.BlockSpec((bm, D), lambda j, i, k: (i, 0))
    wspec = pl.BlockSpec((None, D, F4), lambda j, i, k: (j, 0, 0))
    ospec = pl.BlockSpec((bm, F4), lambda j, i, k: (i, j))
    res = fused_mm(
        name + "_up", (NS, M // bm, 1), [(h, hspec), (wg, wspec), (wu, wspec)],
        [(0, 1, NN, 0), (0, 2, NN, 1)], [(bm, F4), (bm, F4)], epi_up,
        [(_sds((M, dm.DFF), BF), ospec)] * 3, comm=comm_up)
    (a, u, s), land_up = res if comm_up is not None else (res, None)
    if wd is None:
        wd = land_up[0]

    bn = _pick(D, [1024, 512, 256, 128])

    def epi_dn(ids, accs, ex):
        f = accs[0]
        if rows_T:
            row = ids[0] * bm + lax.broadcasted_iota(jnp.int32, (bm, 1), 0)
            gate = jnp.where(row < C, ex[1][...], ex[2][...])
        else:
            gate = ex[2][...]
        return [ex[0][...] + FFN_RES * gate * f, f]

    gspec = lambda t: pl.BlockSpec((None, None, 1, bn), lambda i, j, k: (t, gidx, 0, j))
    res = fused_mm(
        name + "_down", (M // bm, D // bn, NS // 2),
        [(s, pl.BlockSpec((bm, F4), lambda i, j, k: (i, 2 * k))),
         (wd, pl.BlockSpec((None, F4, bn), lambda i, j, k: (2 * k, 0, j))),
         (s, pl.BlockSpec((bm, F4), lambda i, j, k: (i, 2 * k + 1))),
         (wd, pl.BlockSpec((None, F4, bn), lambda i, j, k: (2 * k + 1, 0, j)))],
        [(0, 1, NN, 0), (2, 3, NN, 0)], [(bm, bn)], epi_dn,
        [(_sds((M, D), F32), pl.BlockSpec((bm, bn), lambda i, j, k: (i, j))),
         (_sds((M, D), BF), pl.BlockSpec((bm, bn), lambda i, j, k: (i, j)))],
        extras=[(xres, pl.BlockSpec((bm, bn), lambda i, j, k: (i, j))), (modv, gspec(0)), (modv, gspec(1))],
        comm=comm_down)
    (xo, f), land_down = res if comm_down is not None else (res, None)
    return xo, a, u, s, f, land_up, land_down


def ffn_bwd(name, dm, df, h, a, u, s, wg, wu, wd, rows_T, comms=None):
    comms = comms or {}
    landed, made = {}, {}

    def run(key, *args, **kw):
        comm = comms[key](landed, made) if key in comms else None
        res = fused_mm(*args, comm=comm, **kw)
        if comm is not None:
            res, landed[key] = res
        return res

    D, F4, NS = dm.D, dm.F4, dm.NS
    M = h.shape[0]
    bm = dm.mT if rows_T else dm.mS
    bkr = dm.kT if rows_T else dm.kS

    def epi_ds(ids, accs, ex):
        ds = accs[0]
        av = ex[0][...].astype(F32)
        uv = ex[1][...].astype(F32)
        sg = _sig(av)
        return [ds * uv * (sg * (1.0 + av * (1.0 - sg))), ds * av * sg]

    ospec = pl.BlockSpec((bm, F4), lambda j, i, k: (i, j))
    da, du = run(
        'ds', name + "_ds", (NS, M // bm, 1),
        [(df, pl.BlockSpec((bm, D), lambda j, i, k: (i, 0))),
         (wd, pl.BlockSpec((None, F4, D), lambda j, i, k: (j, 0, 0)))],
        [(0, 1, NT, 0)], [(bm, F4)], epi_ds, [(_sds((M, dm.DFF), BF), ospec)] * 2,
        extras=[(a, ospec), (u, ospec)], row_split=2)

    ident = lambda ids, accs, ex: list(accs)
    bn = _pick(D, [1024, 512, 256, 128])
    dwg, dwu = run(
        'dwgu', name + "_dwgu", (D // bn, NS, M // bkr),
        [(h, pl.BlockSpec((bkr, bn), lambda i, j, k: (k, i))),
         (da, pl.BlockSpec((bkr, F4), lambda i, j, k: (k, j))),
         (du, pl.BlockSpec((bkr, F4), lambda i, j, k: (k, j)))],
        [(0, 1, TN, 0), (0, 2, TN, 1)], [(bn, F4), (bn, F4)], ident,
        [(_sds((NS, D, F4), BF), pl.BlockSpec((None, bn, F4), lambda i, j, k: (j, i, 0)))] * 2)
    made['dwg'], made['dwu'] = dwg, dwu

    bk2 = 2 * bkr if M % (2 * bkr) == 0 else bkr
    dwd = run(
        'dwd', name + "_dwd", (NS, D // bn, M // bk2),
        [(s, pl.BlockSpec((bk2, F4), lambda i, j, k: (k, i))),
         (df, pl.BlockSpec((bk2, bn), lambda i, j, k: (k, j)))],
        [(0, 1, TN, 0)], [(F4, bn)], ident,
        [(_sds((NS, F4, D), BF), pl.BlockSpec((None, F4, bn), lambda i, j, k: (i, 0, j)))])[0]
    made['dwd'] = dwd

    a_spec = lambda o: pl.BlockSpec((bm, F4), lambda i, j, k: (i, 2 * k + o))
    w_spec = lambda o: pl.BlockSpec((None, bn, F4), lambda i, j, k: (2 * k + o, j, 0))
    dh = run(
        'dh', name + "_dh", (M // bm, D // bn, NS // 2),
        [(da, a_spec(0)), (wg, w_spec(0)), (du, a_spec(0)), (wu, w_spec(0)),
         (da, a_spec(1)), (wg, w_spec(1)), (du, a_spec(1)), (wu, w_spec(1))],
        [(0, 1, NT, 0), (2, 3, NT, 0), (4, 5, NT, 0), (6, 7, NT, 0)], [(bm, bn)], ident,
        [(_sds((M, D), F32), pl.BlockSpec((bm, bn), lambda i, j, k: (i, j)))])[0]
    return dh, dwg, dwu, dwd, landed


def qk_prep(dm, P, gq, gk, cosf, sinf):
    D, KVW, bm = dm.D, dm.KVW, dm.bm

    def head_norm_rope(xh, g, c, s):
        r = lax.rsqrt(jnp.mean(xh * xh, axis=-1, keepdims=True) + EPS)
        y = xh * r * g
        return y * c + _pair_swap(y) * s

    def fn(ids, q_ref, k_ref, v_ref, gq_ref, gk_ref, c_ref, s_ref):
        c, s = c_ref[...], s_ref[...]
        qs = [head_norm_rope(q_ref[:, h * HEAD_DIM:(h + 1) * HEAD_DIM], gq_ref[...], c, s) for h in range(dm.NQ)]
        ks = [head_norm_rope(k_ref[:, h * HEAD_DIM:(h + 1) * HEAD_DIM], gk_ref[...], c, s) for h in range(dm.NKV)]
        return [jnp.concatenate(qs, axis=1), jnp.concatenate(ks, axis=1), v_ref[...]]

    hspec = pl.BlockSpec((bm, HEAD_DIM), lambda i: (i, 0))
    vec = pl.BlockSpec((1, HEAD_DIM), lambda i: (0, 0))
    return ew_call(
        "qk_prep", (dm.nTb,), fn,
        [(P, pl.BlockSpec((bm, D), lambda i: (i, 0))),
         (P, pl.BlockSpec((bm, KVW), lambda i: (i, dm.OFF_K // KVW))),
         (P, pl.BlockSpec((bm, KVW), lambda i: (i, dm.OFF_V // KVW))),
         (gq, vec), (gk, vec), (cosf, hspec), (sinf, hspec)],
        [(_sds((dm.T, D), BF), pl.BlockSpec((bm, D), lambda i: (i, 0)), False),
         (_sds((dm.T, KVW), BF), pl.BlockSpec((bm, KVW), lambda i: (i, 0)), False),
         (_sds((dm.T, KVW), BF), pl.BlockSpec((bm, KVW), lambda i: (i, 0)), False)])


def qk_prep_bwd(dm, dq, dk, dv, P, gq, gk, cosf, sinf, dP):
    D, KVW, bm, nCb = dm.D, dm.KVW, dm.bm, dm.nCb
    W = D + 2 * KVW

    def head_bwd(d, xh, g, c, s):
        dy = d * c - _pair_swap(d) * s
        r = lax.rsqrt(jnp.mean(xh * xh, axis=-1, keepdims=True) + EPS)
        xn = xh * r
        dg = jnp.sum(dy * xn, axis=0, keepdims=True)
        dxn = dy * g
        return r * (dxn - xn * jnp.mean(dxn * xn, axis=-1, keepdims=True)), dg

    def fn(ids, dq_ref, dk_ref, dv_ref, q_ref, k_ref, gq_ref, gk_ref, c_ref, s_ref, dp_any):
        i = ids[0]
        c, s = c_ref[...], s_ref[...]
        lat = i >= nCb
        outs, dgq = [], jnp.zeros((1, HEAD_DIM), F32)
        for h in range(dm.NQ):
            sl = slice(h * HEAD_DIM, (h + 1) * HEAD_DIM)
            d = jnp.where(lat, dq_ref[:, sl], 0.0)
            dx, dg = head_bwd(d, q_ref[:, sl], gq_ref[...], c, s)
            outs.append(dx)
            dgq = dgq + dg
        dgk = jnp.zeros((1, HEAD_DIM), F32)
        for h in range(dm.NKV):
            sl = slice(h * HEAD_DIM, (h + 1) * HEAD_DIM)
            dx, dg = head_bwd(dk_ref[:, sl], k_ref[:, sl], gk_ref[...], c, s)
            outs.append(dx)
            dgk = dgk + dg
        outs.append(dv_ref[...])
        return [jnp.concatenate(outs, axis=1), dgq, dgk]

    hspec = pl.BlockSpec((bm, HEAD_DIM), lambda i: (i, 0))
    vec = pl.BlockSpec((1, HEAD_DIM), lambda i: (0, 0))
    return ew_call(
        "qk_prep_bwd", (dm.nTb,), fn,
        [(dq, pl.BlockSpec((bm, D), lambda i: (jnp.maximum(i - nCb, 0), 0))),
         (dk, pl.BlockSpec((bm, KVW), lambda i: (i, 0))),
         (dv, pl.BlockSpec((bm, KVW), lambda i: (i, 0))),
         (P, pl.BlockSpec((bm, D), lambda i: (i, 0))),
         (P, pl.BlockSpec((bm, KVW), lambda i: (i, dm.OFF_K // KVW))),
         (gq, vec), (gk, vec), (cosf, hspec), (sinf, hspec), (dP, ANY)],
        [(_sds(dP.shape, BF), pl.BlockSpec((bm, W), lambda i: (i, 0)), False),
         (_sds((1, HEAD_DIM), F32), vec, True), (_sds((1, HEAD_DIM), F32), vec, True)],
        first=lambda ids: ids[0] == 0, aliases={9: 0})


def _softmax_numerators(s_ref, eb_ref, mb_ref, scale):
    rows, T = s_ref.shape
    m = jnp.max(s_ref[...], axis=-1, keepdims=True)
    mb_ref[...] = jnp.broadcast_to(m, (rows, HEAD_DIM))
    lacc = jnp.zeros((rows, HEAD_DIM), F32)
    for c in range(T // HEAD_DIM):
        cs = slice(c * HEAD_DIM, (c + 1) * HEAD_DIM)
        e = jnp.exp2((s_ref[:, cs] - mb_ref[...]) * (scale * LOG2_E))
        lacc = lacc + e
        eb_ref[:, cs] = e.astype(BF)
    return jnp.sum(lacc, axis=-1, keepdims=True)


def attention_fwd(dm, qr, kr, vb, comm=None):
    S, T, D, G, nCb = dm.S, dm.T, dm.D, dm.G, dm.nCb
    bq = dm.bq
    off = dm.C // bq
    scale = HEAD_DIM ** -0.5
    GW = G * HEAD_DIM

    def body(q_ref, k_ref, v_ref, o_ref):
        k = k_ref[...]
        v = v_ref[...]
        head = lambda h: slice(h * HEAD_DIM, (h + 1) * HEAD_DIM)
        scores = lambda h: lax.dot_general(q_ref[:, head(h)], k, (NT, ((), ())), preferred_element_type=F32)
        s_next = scores(0)
        for h in range(G):
            s = s_next
            if h + 1 < G:
                s_next = scores(h + 1)
            m = jnp.max(s, axis=-1, keepdims=True)
            p = jnp.exp2((s - m) * (scale * LOG2_E))
            l = jnp.sum(p, axis=-1, keepdims=True)
            o = lax.dot_general(p.astype(BF), v, (NN, ((), ())), preferred_element_type=F32)
            o_ref[:, head(h)] = o / l

    return hosted_call(
        body, grid=(dm.NKV, S // bq), name="attn_fwd",
        in_specs=[pl.BlockSpec((bq, GW), lambda g, i: (i + off, g)),
                  pl.BlockSpec((T, HEAD_DIM), lambda g, i: (0, g)),
                  pl.BlockSpec((T, HEAD_DIM), lambda g, i: (0, g))],
        out_specs=[pl.BlockSpec((bq, GW), lambda g, i: (i, g))],
        out_shape=[_sds((S, D), F32)], args=[qr, kr, vb], comm=comm)


def attention_bwd(dm, qr, kr, vb, attn, dattn, comm=None):
    S, T, D, G = dm.S, dm.T, dm.D, dm.G
    bq = dm.bq
    off = dm.C // bq
    scale = HEAD_DIM ** -0.5
    GW = G * HEAD_DIM

    def body(q_ref, k_ref, v_ref, o_ref, do_ref, dq_ref, dk_ref, dv_ref, s2_ref, dp_ref, eb_ref, tb_ref, mb_ref):
        i = pl.program_id(1)

        @pl.when(i == 0)
        def _():
            dk_ref[...] = jnp.zeros(dk_ref.shape, F32)
            dv_ref[...] = jnp.zeros(dv_ref.shape, F32)

        k = k_ref[...]
        v = v_ref[...]
        head = lambda h: slice(h * HEAD_DIM, (h + 1) * HEAD_DIM)

        def finish(h, w):
            dq_ref[:, head(h)] = lax.dot_general(tb_ref[...], k, (NN, ((), ())), preferred_element_type=F32) * w
            dk_ref[...] += lax.dot_general(tb_ref[...], (q_ref[:, head(h)].astype(F32) * w).astype(BF), (TN, ((), ())),
                                           preferred_element_type=F32)

        s2_ref[0] = lax.dot_general(q_ref[:, head(0)], k, (NT, ((), ())), preferred_element_type=F32)
        w_prev = None
        for h in range(G):
            s_ref = s2_ref.at[h % 2]
            do = do_ref[:, head(h)]
            dof = do.astype(F32)
            if h + 1 < G:
                s2_ref[(h + 1) % 2] = lax.dot_general(q_ref[:, head(h + 1)], k, (NT, ((), ())),
                                                      preferred_element_type=F32)
            if h > 0:
                finish(h - 1, w_prev)
            l = _softmax_numerators(s_ref, eb_ref, mb_ref, scale)
            rl = 1.0 / l
            dp_ref[...] = lax.dot_general(do, v, (NT, ((), ())), preferred_element_type=F32)
            dv_ref[...] += lax.dot_general(eb_ref[...], (dof * rl).astype(BF), (TN, ((), ())), preferred_element_type=F32)
            delta = jnp.sum(dof * o_ref[:, head(h)], axis=-1, keepdims=True)
            mb_ref[...] = jnp.broadcast_to(delta, (bq, HEAD_DIM))
            for c in range(T // HEAD_DIM):
                cs = slice(c * HEAD_DIM, (c + 1) * HEAD_DIM)
                tb_ref[:, cs] = (eb_ref[:, cs].astype(F32) * (dp_ref[:, cs] - mb_ref[...])).astype(BF)
            w_prev = scale * rl
        finish(G - 1, w_prev)

    return hosted_call(
        body, grid=(dm.NKV, S // bq), name="attn_bwd",
        in_specs=[pl.BlockSpec((bq, GW), lambda g, i: (i + off, g)),
                  pl.BlockSpec((T, HEAD_DIM), lambda g, i: (0, g)),
                  pl.BlockSpec((T, HEAD_DIM), lambda g, i: (0, g)),
                  pl.BlockSpec((bq, GW), lambda g, i: (i, g)),
                  pl.BlockSpec((bq, GW), lambda g, i: (i + off, g))],
        out_specs=[pl.BlockSpec((bq, GW), lambda g, i: (i, g)),
                   pl.BlockSpec((T, HEAD_DIM), lambda g, i: (0, g)),
                   pl.BlockSpec((T, HEAD_DIM), lambda g, i: (0, g))],
        out_shape=[_sds((S, D), F32), _sds((T, dm.KVW), F32), _sds((T, dm.KVW), F32)],
        args=[qr, kr, vb, attn, dattn],
        scratch_shapes=[pltpu.VMEM((2, bq, T), F32), pltpu.VMEM((bq, T), F32), pltpu.VMEM((bq, T), BF),
                        pltpu.VMEM((bq, T), BF), pltpu.VMEM((bq, HEAD_DIM), F32)], comm=comm)


def _conv_taps(dm, lx, masks_only=False):
    T, C = dm.T, dm.C
    t = lax.broadcasted_iota(jnp.int32, (T, 1), 0)
    valid = [(t >= 2) & ((t < C) | (t >= C + 2)), (t >= 1) & ((t < C) | (t >= C + 1)), None,
             (t != C - 1) & (t != T - 1)]
    shifts = [2, 1, 0, T - 1]
    taps = []
    for k in range(4):
        if k == 2:
            taps.append(lx)
        else:
            taps.append(jnp.where(valid[k], pltpu.roll(lx, shifts[k], 0), 0.0))
    return taps


def _scan_tiles(dm, chains):
    T, C = dm.T, dm.C
    nT, nC = T // 8, C // 8
    row = lax.broadcasted_iota(jnp.int32, (8, HEAD_DIM), 0)

    def tile_of(i, asc, split):
        if not split:
            return i if asc else nT - 1 - i
        if asc:
            return jnp.where(i < nT - nC, nC + i, i - (nT - nC))
        return jnp.where(i < nC, nC - 1 - i, nT - 1 - (i - nC))

    def step(i, carry, asc, split, a_ref, u_ref, out_ref, mode):
        off = pl.multiple_of(tile_of(i, asc, split) * 8, 8)
        a = a_ref[pl.ds(off, 8), :]
        b = u_ref[pl.ds(off, 8), :]
        if mode == 'lam':
            if asc:
                coef = jnp.where(row == 0, 1.0, pltpu.roll(a, 1, 0))
            else:
                coef = jnp.where(row == 7, 1.0, pltpu.roll(a, 7, 0))
        else:
            coef = a
        A, B = coef, b
        for d in (1, 2, 4):
            if asc:
                ok = row >= d
                A_sh = jnp.where(ok, pltpu.roll(A, d, 0), 1.0)
                B_sh = jnp.where(ok, pltpu.roll(B, d, 0), 0.0)
            else:
                ok = row < 8 - d
                A_sh = jnp.where(ok, pltpu.roll(A, 8 - d, 0), 1.0)
                B_sh = jnp.where(ok, pltpu.roll(B, 8 - d, 0), 0.0)
            B = B + A * B_sh
            A = A * A_sh
        h = A * carry + B
        out_ref[pl.ds(off, 8), :] = h
        last = h[7:8, :] if asc else h[0:1, :]
        if mode == 'lam':
            last = last * (a[7:8, :] if asc else a[0:1, :])
        return jnp.broadcast_to(last, (8, HEAD_DIM))

    U = 4 if nT % 4 == 0 else 1

    def body(i, carries):
        for u in range(U):
            carries = tuple(step(i * U + u, c_, *ch) for c_, ch in zip(carries, chains))
        return carries

    lax.fori_loop(0, nT // U, body, tuple(jnp.zeros((8, HEAD_DIM), F32) for _ in chains))


def _lru_gates(xc, wa, ba, wx, bx, sp):
    xb = xc.astype(BF)
    r = _sig(jnp.dot(xb, wa, preferred_element_type=F32) + ba)
    i = _sig(jnp.dot(xb, wx, preferred_element_type=F32) + bx)
    a = jnp.exp(-LRU_C * r * sp)
    m = jnp.sqrt(1.0 - a * a)
    return r, i, a, m


def lru_fwd(dm, P, conv_w, conv_b, wa, ba, wx, bx, sp):
    T, D, LB = dm.T, dm.D, dm.LB
    W = dm.LBD
    R = _pick(T, [2176, 1088, 544, 272, 256, 128, 64, 8])
    lxb = dm.OFF_LX // W

    def body(lx_ref, cw_ref, cb_ref, wa_ref, ba_ref, wx_ref, bx_ref, sp_ref, hf_ref, hb_ref, xc_ref, a_ref, u_ref):
        taps = _conv_taps(dm, lx_ref[...])
        xc = cb_ref[...]
        for k in range(4):
            xc = xc + taps[k] * cw_ref[k:k + 1, :]
        xc_ref[...] = xc

        def chunk(ci, _):
            off = pl.multiple_of(ci * R, 8)
            x = xc_ref[pl.ds(off, R), :]
            for d in range(2):
                r, i, a, m = _lru_gates(x, wa_ref[d].astype(BF), ba_ref[d], wx_ref[d].astype(BF), bx_ref[d], sp_ref[d])
                a_ref[d, pl.ds(off, R), :] = a
                u_ref[d, pl.ds(off, R), :] = m * i * x
            return 0

        lax.fori_loop(0, T // R, chunk, 0)
        _scan_tiles(dm, [(True, False, a_ref.at[0], u_ref.at[0], hf_ref, 'h'),
                         (False, True, a_ref.at[1], u_ref.at[1], hb_ref, 'h')])

    strip = lambda j: (0, j)
    vec = pl.BlockSpec((2, 1, W), lambda j: (0, 0, j))
    mat = pl.BlockSpec((2, None, W, W), lambda j: (0, j, 0, 0))
    return pl.pallas_call(
        body, grid=(LB,), name="lru_fwd",
        in_specs=[pl.BlockSpec((T, W), lambda j: (0, lxb + j)),
                  pl.BlockSpec((4, W), strip), pl.BlockSpec((1, W), strip), mat, vec, mat, vec, vec],
        out_specs=[pl.BlockSpec((T, W), strip)] * 2, out_shape=[_sds((T, D), F32)] * 2,
        scratch_shapes=[pltpu.VMEM((T, W), F32), pltpu.VMEM((2, T, W), F32), pltpu.VMEM((2, T, W), F32)],
        compiler_params=_cparams(),
    )(P, conv_w, conv_b, wa, ba, wx, bx, sp)


def lru_bwd(dm, P, dh, hf, hb, conv_w, conv_b, wa, ba, wx, bx, sp, sg, dP):
    T, C, D, LB = dm.T, dm.C, dm.D, dm.LB
    W = dm.LBD
    R = _pick(T, [2176, 1088, 544, 272, 256, 128, 64, 8])
    lxb = dm.OFF_LX // W

    def body(lx_ref, dh_ref, hf_ref, hb_ref, cw_ref, cb_ref, wa_ref, ba_ref, wx_ref, bx_ref, sp_ref, sg_ref, _dp_any,
             dlx_ref, dwa_ref, dba_ref, dwx_ref, dbx_ref, dlam_ref, dcw_ref, dcb_ref,
             xc_ref, a_ref, lam_ref, hp_ref, dxc_ref):
        lx = lx_ref[...]
        taps = _conv_taps(dm, lx)
        xc = cb_ref[...]
        for k in range(4):
            xc = xc + taps[k] * cw_ref[k:k + 1, :]
        xc_ref[...] = xc

        def gates(d, x):
            return _lru_gates(x, wa_ref[d].astype(BF), ba_ref[d], wx_ref[d].astype(BF), bx_ref[d], sp_ref[d])

        def chunk_a(ci, _):
            off = pl.multiple_of(ci * R, 8)
            x = xc_ref[pl.ds(off, R), :]
            for d in range(2):
                a_ref[d, pl.ds(off, R), :] = gates(d, x)[2]
            return 0

        lax.fori_loop(0, T // R, chunk_a, 0)
        _scan_tiles(dm, [(False, False, a_ref.at[0], dh_ref, lam_ref.at[0], 'lam'),
                         (True, True, a_ref.at[1], dh_ref, lam_ref.at[1], 'lam')])
        t = lax.broadcasted_iota(jnp.int32, (T, 1), 0)
        hp_ref[0] = jnp.where(t == 0, 0.0, pltpu.roll(hf_ref[...], 1, 0))
        hv = hb_ref[...]
        hp_ref[1] = jnp.where(t == C - 1, 0.0, jnp.where(t == T - 1, jnp.broadcast_to(hv[0:1, :], hv.shape),
                                                         pltpu.roll(hv, T - 1, 0)))

        def chunk_b(d):
            wa_, wx_ = wa_ref[d].astype(BF), wx_ref[d].astype(BF)

            def run(ci, carry):
                dwa, dwx, dba, dbx, dlam = carry
                off = pl.multiple_of(ci * R, 8)
                x = xc_ref[pl.ds(off, R), :]
                r, i, a, m = gates(d, x)
                lam = lam_ref[d, pl.ds(off, R), :]
                da = lam * hp_ref[d, pl.ds(off, R), :] - lam * (i * x) * a / m
                dloga = da * a
                dza = dloga * (-LRU_C) * sp_ref[d] * r * (1.0 - r)
                dzx = lam * m * x * i * (1.0 - i)
                dzab, dzxb = dza.astype(BF), dzx.astype(BF)
                xb = x.astype(BF)
                dxc = lam * m * i
                dxc = dxc + lax.dot_general(dzab, wa_, (NT, ((), ())), preferred_element_type=F32)
                dxc = dxc + lax.dot_general(dzxb, wx_, (NT, ((), ())), preferred_element_type=F32)
                if d == 0:
                    dxc_ref[pl.ds(off, R), :] = dxc
                else:
                    dxc_ref[pl.ds(off, R), :] += dxc
                dwa = dwa + lax.dot_general(xb, dzab, (TN, ((), ())), preferred_element_type=F32)
                dwx = dwx + lax.dot_general(xb, dzxb, (TN, ((), ())), preferred_element_type=F32)
                dba = dba + jnp.sum(dza, axis=0, keepdims=True)
                dbx = dbx + jnp.sum(dzx, axis=0, keepdims=True)
                dlam = dlam + jnp.sum(dloga * LRU_C * r, axis=0, keepdims=True)
                return dwa, dwx, dba, dbx, dlam

            z = jnp.zeros((W, W), F32)
            zv = jnp.zeros((1, W), F32)
            dwa, dwx, dba, dbx, dlam = lax.fori_loop(0, T // R, run, (z, z, zv, zv, zv))
            dwa_ref[d] = dwa
            dwx_ref[d] = dwx
            dba_ref[d] = dba
            dbx_ref[d] = dbx
            dlam_ref[d] = dlam * sg_ref[d]

        chunk_b(0)
        chunk_b(1)
        dxc = dxc_ref[...]
        dcb_ref[...] = jnp.sum(dxc, axis=0, keepdims=True)
        dcw_ref[...] = jnp.concatenate([jnp.sum(dxc * taps[k], axis=0, keepdims=True) for k in range(4)], axis=0)
        valid = [(t < T - 2) & ((t >= C) | (t < C - 2)), (t < T - 1) & ((t >= C) | (t < C - 1)), None,
                 (t != 0) & (t != C)]
        shifts = [T - 2, T - 1, 0, 1]
        dlx = dxc * cw_ref[2:3, :]
        for k in (0, 1, 3):
            dlx = dlx + jnp.where(valid[k], pltpu.roll(dxc, shifts[k], 0), 0.0) * cw_ref[k:k + 1, :]
        dlx_ref[...] = dlx.astype(dlx_ref.dtype)

    strip = lambda j: (0, j)
    sspec = pl.BlockSpec((T, W), strip)
    vec = pl.BlockSpec((2, 1, W), lambda j: (0, 0, j))
    mat = pl.BlockSpec((2, None, W, W), lambda j: (0, j, 0, 0))
    ovec = pl.BlockSpec((2, 1, W), lambda j: (0, 0, j))
    return pl.pallas_call(
        body, grid=(LB,), name="lru_bwd",
        in_specs=[pl.BlockSpec((T, W), lambda j: (0, lxb + j)), sspec, sspec, sspec,
                  pl.BlockSpec((4, W), strip), pl.BlockSpec((1, W), strip), mat, vec, mat, vec, vec, vec, ANY],
        out_specs=[pl.BlockSpec((T, W), lambda j: (0, lxb + j)), mat, ovec, mat, ovec, ovec,
                   pl.BlockSpec((4, W), strip), pl.BlockSpec((1, W), strip)],
        out_shape=[_sds(dP.shape, BF), _sds((2, LB, W, W), F32), _sds((2, 1, D), F32), _sds((2, LB, W, W), F32),
                   _sds((2, 1, D), F32), _sds((2, 1, D), F32), _sds((4, D), F32), _sds((1, D), F32)],
        scratch_shapes=[pltpu.VMEM((T, W), F32), pltpu.VMEM((2, T, W), F32), pltpu.VMEM((2, T, W), F32),
                        pltpu.VMEM((2, T, W), F32), pltpu.VMEM((T, W), F32)],
        input_output_aliases={12: 0}, compiler_params=_cparams(),
    )(P, dh, hf, hb, conv_w, conv_b, wa, ba, wx, bx, sp, sg, dP)


def merge_fwd(dm, P, attn, hf, hb):
    S, D, bm, cw, nCb = dm.S, dm.D, dm.bm, dm.cw, dm.nCb

    def fn(ids, lg_ref, ga_ref, gl_ref, at_ref, hf_ref, hb_ref):
        ge, _ = _gelu(lg_ref[...])
        lru = (hf_ref[...] + hb_ref[...]) * ge
        return [_sig(ga_ref[...]) * at_ref[...] + _sig(gl_ref[...]) * lru]

    pspec = lambda off: pl.BlockSpec((bm, cw), lambda i, j: (i + nCb, off // cw + j))
    tspec = pl.BlockSpec((bm, cw), lambda i, j: (i + nCb, j))
    sspec = pl.BlockSpec((bm, cw), lambda i, j: (i, j))
    return ew_call(
        "merge_fwd", (dm.nSb, D // cw), fn,
        [(P, pspec(dm.OFF_LG)), (P, pspec(dm.OFF_GA)), (P, pspec(dm.OFF_GL)), (attn, sspec), (hf, tspec), (hb, tspec)],
        [(_sds((S, D), BF), sspec, False)])[0]


def merge_bwd(dm, dmg, P, attn, hf, hb):
    S, T, D, bm, cw, nCb = dm.S, dm.T, dm.D, dm.bm, dm.cw, dm.nCb
    nj = D // cw
    n_steps = dm.nTb * nj

    def body(dm_ref, lg_ref, ga_ref, gl_ref, at_ref, hf_ref, hb_ref, dp_ref, da_ref, dh_ref, buf, sems):
        i, j = pl.program_id(0), pl.program_id(1)
        lat = i >= nCb
        d = jnp.where(lat, dm_ref[...].astype(F32), 0.0)
        lg = lg_ref[...]
        ge, th = _gelu(lg)
        hs = hf_ref[...] + hb_ref[...]
        sa, sl = _sig(ga_ref[...]), _sig(gl_ref[...])
        at = jnp.where(lat, at_ref[...], 0.0)
        dlru = d * sl
        n = i * nj + j
        par = n % 2

        def copies(p):
            out = []
            for g, off in enumerate((dm.OFF_LG, dm.OFF_GA, dm.OFF_GL)):
                col = pl.multiple_of(off + j * cw, 128)
                out.append(pltpu.make_async_copy(
                    buf.at[p, g], dp_ref.at[pl.ds(pl.multiple_of(i * bm, 8), bm), pl.ds(col, cw)], sems.at[p, g]))
            return out

        @pl.when(n >= 2)
        def _():
            for cp in copies(par):
                cp.wait()

        buf[par, 0] = (dlru * hs * _gelu_grad(lg, th)).astype(BF)
        buf[par, 1] = (d * at * sa * (1.0 - sa)).astype(BF)
        buf[par, 2] = (d * hs * ge * sl * (1.0 - sl)).astype(BF)
        da_ref[...] = (d * sa).astype(BF)
        dh_ref[...] = dlru * ge
        for cp in copies(par):
            cp.start()

        @pl.when(n == n_steps - 1)
        def _():
            for cp in copies(par):
                cp.wait()
            if n_steps >= 2:
                for cp in copies(1 - par):
                    cp.wait()

    pspec = lambda off: pl.BlockSpec((bm, cw), lambda i, j: (i, off // cw + j))
    tspec = pl.BlockSpec((bm, cw), lambda i, j: (i, j))
    lspec = pl.BlockSpec((bm, cw), lambda i, j: (jnp.maximum(i - nCb, 0), j))
    return pl.pallas_call(
        body, grid=(dm.nTb, nj), name="merge_bwd",
        in_specs=[lspec, pspec(dm.OFF_LG), pspec(dm.OFF_GA), pspec(dm.OFF_GL), lspec, tspec, tspec],
        out_specs=[ANY, tspec, tspec],
        out_shape=[_sds((T, dm.INW), BF), _sds((T, D), BF), _sds((T, D), F32)],
        scratch_shapes=[pltpu.VMEM((2, 3, bm, cw), BF), pltpu.SemaphoreType.DMA((2, 3))],
        compiler_params=_cparams(),
    )(dmg, P, P, P, attn, hf, hb)


def final_loss(dm, x3, gfin, target, f3, modv):
    S, D, bm = dm.S, dm.D, dm.bm

    def fn(ids, x_ref, g_ref, t_ref, f_ref, gv_ref):
        xv = x_ref[...]
        g = g_ref[...]
        r = lax.rsqrt(jnp.mean(xv * xv, axis=-1, keepdims=True) + EPS)
        xn = xv * r
        err = xn * g - t_ref[...]
        loss = 0.5 * jnp.sum(jnp.mean(err * err, axis=-1, keepdims=True), axis=0, keepdims=True)
        dy = err / D
        dxn = dy * g
        dx = r * (dxn - xn * jnp.mean(dxn * xn, axis=-1, keepdims=True))
        return [jnp.broadcast_to(loss, (1, 128)), dx, jnp.sum(dy * xn, axis=0, keepdims=True),
                FFN_RES * gv_ref[...] * dx, jnp.sum(FFN_RES * f_ref[...].astype(F32) * dx, axis=0, keepdims=True)]

    row = pl.BlockSpec((bm, D), lambda i: (i, 0))
    vec = pl.BlockSpec((1, D), lambda i: (0, 0))
    return ew_call(
        "final_loss", (dm.nSb,), fn,
        [(x3, row), (gfin, vec), (target, row), (f3, row),
         (modv, pl.BlockSpec((None, None, 1, D), lambda i: (1, 8, 0, 0)))],
        [(_sds((1, 128), F32), pl.BlockSpec((1, 128), lambda i: (0, 0)), True), (_sds((S, D), F32), row, False),
         (_sds((1, D), F32), vec, True), (_sds((S, D), BF), row, False),
         (_sds((2, 1, D), F32), pl.BlockSpec((None, 1, D), lambda i: (1, 0, 0)), True)], first=lambda ids: ids[0] == 0)


def local_step(dm, x, ctx, target, modv, norm_g3, gfin, gq, gk, conv_w, conv_b, wa, ba, wx, bx, lam, wbuf, where):
    S, C, T, D, NS, F4, W4 = dm.S, dm.C, dm.T, dm.D, dm.NS, dm.F4, dm.W4
    Ds = D // NS
    wb, nsub = dm.wb, dm.nsub
    cosf, sinf = rope_tables(dm)
    sp = jax.nn.softplus(-lam)
    sg = jax.nn.sigmoid(-lam)
    ident = lambda ids, accs, ex: list(accs)
    mT, mS, kT, kS = dm.mT, dm.mS, dm.kT, dm.kS
    bn = _pick(D, [1024, 512, 256, 128])
    bk = _pick(D, [512, 256, 128])

    wg0, wu0 = comm_call("ag_ffn1", ag_comm([wbuf['wg0'], wbuf['wu0']]))
    h1, xt = normmod_concat_fwd("nm1", dm, ctx, x, norm_g3, modv)
    xt1, a1, u1, s1, f1, land, _ = ffn_fwd("ffn1", dm, h1, xt, wg0, wu0, None, modv, 2, True,
                                           comm_up=ag_comm([wbuf['wd0'], wbuf['w_in']]))
    wd0, w_in = land
    h2 = normmod_fwd("nm2", dm, xt1, norm_g3, 1, modv, True)
    P = fused_mm(
        "w_in", (NS * nsub, T // mT, 1),
        [(h2, pl.BlockSpec((mT, D), lambda j, i, k: (i, 0))),
         (w_in, pl.BlockSpec((None, D, wb), lambda j, i, k: (j // nsub, 0, j % nsub)))],
        [(0, 1, NN, 0)], [(mT, wb)], ident,
        [(_sds((T, dm.INW), F32), pl.BlockSpec((mT, wb), lambda j, i, k: (i, j)))])[0]
    qr, kr, vb = qk_prep(dm, P, gq, gk, cosf, sinf)
    (attn,), (wg1, wu1, wd1, w_out) = attention_fwd(
        dm, qr, kr, vb, comm=ag_comm([wbuf['wg1'], wbuf['wu1'], wbuf['wd1'], wbuf['w_out']]))
    w_out = w_out.reshape(D, D)
    hf, hb = lru_fwd(dm, P, conv_w, conv_b, wa, ba, wx, bx, sp)
    mg = merge_fwd(dm, P, attn, hf, hb)

    def epi_o(ids, accs, ex):
        o = accs[0]
        return [ex[0][...] + ex[1][...] * o, o]

    rb, nCb = dm.bm, dm.nCb
    x2, o2 = fused_mm(
        "w_out", (D // bn, S // rb, 1),
        [(mg, pl.BlockSpec((rb, D), lambda j, i, k: (i, 0))), (w_out, pl.BlockSpec((D, bn), lambda j, i, k: (0, j)))],
        [(0, 1, NN, 0)], [(rb, bn)], epi_o,
        [(_sds((S, D), F32), pl.BlockSpec((rb, bn), lambda j, i, k: (i, j))),
         (_sds((S, D), BF), pl.BlockSpec((rb, bn), lambda j, i, k: (i, j)))],
        extras=[(xt1, pl.BlockSpec((rb, bn), lambda j, i, k: (i + nCb, j))),
                (modv, pl.BlockSpec((None, None, 1, bn), lambda j, i, k: (1, 5, 0, j)))])
    h3 = normmod_fwd("nm3", dm, x2, norm_g3, 2, modv, False)
    x3, a3, u3, s3, f3, _, _ = ffn_fwd("ffn2", dm, h3, x2, wg1, wu1, wd1, modv, 8, False)
    loss, dx3, dgfin, df3, dg3 = final_loss(dm, x3, gfin, target, f3, modv)

    dh3, dwg1, dwu1, dwd1, _ = ffn_bwd("ffn2b", dm, df3, h3, a3, u3, s3, wg1, wu1, wd1, False)
    dx2, dsh3, dsc3, dgn3, do2, dg2 = normmod_bwd("nm3b", dm, dh3, x2, dx3, norm_g3, 2, modv, False, False,
                                                  gate=(o2, 5, 1.0))
    keep = {}

    def host_a(key, comm):
        if key == 'p1':
            (keep['dmg'],), landed = fused_mm(
                "w_out_dx", (S // mS, D // bn, 1),
                [(do2, pl.BlockSpec((mS, D), lambda i, j, k: (i, 0))),
                 (w_out, pl.BlockSpec((bn, D), lambda i, j, k: (j, 0)))],
                [(0, 1, NT, 0)], [(mS, bn)], ident,
                [(_sds((S, D), BF), pl.BlockSpec((mS, bn), lambda i, j, k: (i, j)))], comm=comm)
            return landed
        keep['dqkv'], landed = attention_bwd(dm, qr, kr, vb, attn, keep['dattn'], comm=comm)
        return landed

    gots_a = host_a('p1', rs_p1_comm([dwg1, dwu1, dwd1]))
    dmg = keep['dmg']
    dw_out = fused_mm(
        "w_out_dw", (D // bn, D // bn, S // kS),
        [(mg, pl.BlockSpec((kS, bn), lambda i, j, k: (k, i))), (do2, pl.BlockSpec((kS, bn), lambda i, j, k: (k, j)))],
        [(0, 1, TN, 0)], [(bn, bn)], ident,
        [(_sds((D, D), BF), pl.BlockSpec((bn, bn), lambda i, j, k: (i, j)))])[0]
    dP, dattn, dhs = merge_bwd(dm, dmg, P, attn, hf, hb)
    keep['dattn'] = dattn
    pairs_a = [add_pair("rs_add_" + n_, g_, got_, where)
               for n_, g_, got_ in zip(('wg1', 'wu1', 'wd1'), (dwg1, dwu1, dwd1), gots_a)]
    land_a = host_a('p2', rs_p2_comm([p_[0] for p_ in pairs_a], [p_[1] for p_ in pairs_a]))
    dq, dk, dv = keep['dqkv']
    dP, dwa, dba, dwx, dbx, dlam, dcw, dcb = lru_bwd(dm, P, dhs, hf, hb, conv_w, conv_b, wa, ba, wx, bx, sp, sg, dP)
    dP, dgq, dgk = qk_prep_bwd(dm, dq, dk, dv, P, gq, gk, cosf, sinf, dP)
    g_wg = sum_slots_into("rs_sum_wg1", land_a[0], where, None, (2, D, F4), 1)
    g_wu = sum_slots_into("rs_sum_wu1", land_a[1], where, None, (2, D, F4), 1)
    g_wd = sum_slots_into("rs_sum_wd1", land_a[2], where, None, (2, F4, D), 1)
    LBD = D // dm.LB
    NM = 2 * dm.LB * LBD
    mats = [dwa.reshape(1, NM, LBD), dwx.reshape(1, NM, LBD)]
    (dh2,), landed_x = fused_mm(
        "w_in_dx", (T // mT, D // bn, NS // 2),
        [(dP, pl.BlockSpec((mT, W4), lambda i, j, k: (i, 2 * k))),
         (w_in, pl.BlockSpec((None, bn, W4), lambda i, j, k: (2 * k, j, 0))),
         (dP, pl.BlockSpec((mT, W4), lambda i, j, k: (i, 2 * k + 1))),
         (w_in, pl.BlockSpec((None, bn, W4), lambda i, j, k: (2 * k + 1, j, 0)))],
        [(0, 1, NT, 0), (2, 3, NT, 0)], [(mT, bn)], ident,
        [(_sds((T, D), F32), pl.BlockSpec((mT, bn), lambda i, j, k: (i, j)))],
        comm=merge_comms([rs_p3_comm([g_wg, g_wu, g_wd], [(0, 1, D // 2), (1, 1, D // 2), (2, 1, F4 // 2)]),
                          rs_p1_comm(mats)]))
    g_wg, g_wu, g_wd = landed_x[:3]
    pairs_m = [add_pair("rs_add_" + n_, g_, got_, where) for n_, g_, got_ in zip(('lru_wa', 'lru_wx'), mats, landed_x[3:])]
    (dw_in,), land_m = fused_mm(
        "w_in_dw", (D // bn, NS, T // kT),
        [(h2, pl.BlockSpec((kT, bn), lambda i, j, k: (k, i))), (dP, pl.BlockSpec((kT, W4), lambda i, j, k: (k, j)))],
        [(0, 1, TN, 0)], [(bn, W4)], ident,
        [(_sds((NS, D, W4), BF), pl.BlockSpec((None, bn, W4), lambda i, j, k: (j, i, 0)))],
        comm=rs_p2_comm([p_[0] for p_ in pairs_m], [p_[1] for p_ in pairs_m]))
    g_wa = sum_slots_into("rs_sum_lru_wa", land_m[0], where, None, (NM, LBD), None)
    g_wx = sum_slots_into("rs_sum_lru_wx", land_m[1], where, None, (NM, LBD), None)
    dxt1, dsh2, dsc2, dgn2, df1, dg1 = normmod_bwd("nm2b", dm, dh2, xt1, dx2, norm_g3, 1, modv, True, True,
                                                   gate=(f1, 2, FFN_RES))

    tens_b = [dw_in, dw_out.reshape(NS, Ds, D)]

    def host_ds(landed, made):
        return merge_comms([rs_p1_comm(tens_b), rs_p3_comm([g_wa, g_wx], [(0, None, NM // 2), (1, None, NM // 2)])])

    def host_dwgu(landed, made):
        pairs = [add_pair("rs_add_" + n_, g_, got_, where) for n_, g_, got_ in zip(('w_in', 'w_out'), tens_b, landed['ds'][:2])]
        return rs_p2_comm([p_[0] for p_ in pairs], [p_[1] for p_ in pairs])

    def host_dwd(landed, made):
        g_win = sum_slots_into("rs_sum_w_in", landed['dwgu'][0], where, None, (D, W4), None)
        g_wout = sum_slots_into("rs_sum_w_out", landed['dwgu'][1], where, None, (Ds, D), None)
        return merge_comms([rs_p3_comm([g_win, g_wout], [(0, None, D // 2), (1, None, Ds // 2)]),
                            rs_p1_comm([made['dwg'], made['dwu']])])

    def host_dh(landed, made):
        pairs = [add_pair("rs_add_" + n_, g_, got_, where)
                 for n_, g_, got_ in zip(('wg0', 'wu0'), (made['dwg'], made['dwu']), landed['dwd'][2:])]
        return merge_comms([rs_p2_comm([p_[0] for p_ in pairs], [p_[1] for p_ in pairs]), rs_p1_comm([made['dwd']])])

    dh1, dwg0, dwu0, dwd0, landed = ffn_bwd(
        "ffn1b", dm, df1, h1, a1, u1, s1, wg0, wu0, wd0, True,
        comms={'ds': host_ds, 'dwgu': host_dwgu, 'dwd': host_dwd, 'dh': host_dh})
    g_wa, g_wx = landed['ds'][2:]
    g_win, g_wout = landed['dwd'][:2]
    g_wg = sum_slots_into("rs_sum_wg0", landed['dh'][0], where, g_wg, (2, D, F4), 0)
    g_wu = sum_slots_into("rs_sum_wu0", landed['dh'][1], where, g_wu, (2, D, F4), 0)
    late = add_pair("rs_add_wd0", dwd0, landed['dh'][2], where)
    grad_x, dsh1, dsc1, dgn1 = normmod_bwd("nm1b", dm, dh1, xt, dxt1, norm_g3, 0, modv, True, False, out_lat_only=True)

    dmod = jnp.concatenate([dsh1, dsc1, dg1, dsh2, dsc2, _lat(dg2), _lat(dsh3), _lat(dsc3), _lat(dg3)], axis=1)
    dnorm = jnp.stack([dgn1[0, 0] + dgn1[1, 0], dgn2[0, 0] + dgn2[1, 0], dgn3[1, 0]], axis=0)
    small = dict(norm_g=dnorm, q_norm_g=dgq, k_norm_g=dgk, conv_w=dcw, conv_b=dcb,
                 lru_ba=dba.reshape(2, D), lru_bx=dbx.reshape(2, D), lru_lambda=dlam.reshape(2, D), final_norm_g=dgfin)
    reduced = dict(ffn_wg=g_wg, ffn_wu=g_wu, ffn_wd=g_wd, w_in=g_win, w_out=g_wout, lru_wa=g_wa, lru_wx=g_wx)
    return loss, grad_x, dmod, small, reduced, late


def _lat(v):
    return jnp.concatenate([jnp.zeros_like(v[:1]), v[1:]], axis=0)


def _me():
    return lax.axis_index("x"), lax.axis_index("y"), lax.axis_index("c")


def allgather8(name, v):
    def body(v_ref, out_ref, send_sems, recv_sems, local_sem):
        x, y, c = _me()
        me = 4 * x + 2 * y + c
        mine = pltpu.make_async_copy(v_ref, out_ref.at[me], local_sem)
        mine.start()
        copies = []
        for k in range(1, 8):
            peer = (x ^ ((k >> 2) & 1), y ^ ((k >> 1) & 1), c ^ (k & 1))
            cp = pltpu.make_async_remote_copy(src_ref=v_ref, dst_ref=out_ref.at[me], send_sem=send_sems.at[k - 1],
                                              recv_sem=recv_sems.at[k - 1], device_id=peer, device_id_type=MESH)
            cp.start()
            copies.append(cp)
        for k in range(1, 8):
            peer = (x ^ ((k >> 2) & 1), y ^ ((k >> 1) & 1), c ^ (k & 1))
            pltpu.make_async_remote_copy(src_ref=v_ref, dst_ref=out_ref.at[me ^ k], send_sem=send_sems.at[k - 1],
                                         recv_sem=recv_sems.at[k - 1], device_id=peer, device_id_type=MESH).wait_recv()
        for cp in copies:
            cp.wait_send()
        mine.wait()

    return pl.pallas_call(
        body, name=name, out_shape=_sds((8,) + v.shape, v.dtype), in_specs=[ANY], out_specs=ANY,
        scratch_shapes=[pltpu.SemaphoreType.DMA((7,)), pltpu.SemaphoreType.DMA((7,)), pltpu.SemaphoreType.DMA],
    )(v)


def _chips(x, y):
    chips = [(1 - x, y), (x, 1 - y), (1 - x, 1 - y)]
    return chips, [2 * cx + cy for cx, cy in chips]


def ag_comm(bufs):
    n = len(bufs)

    def parts(outs):
        x, y, c = _me()
        chips, slots = _chips(x, y)
        return x, y, c, 2 * x + y, (x, y, 1 - c), chips, slots

    def ici(outs, t, j, send_sems, recv_sems, src_slot, off):
        x, y, c, s, sib, chips, slots = parts(outs)
        H = outs[t].shape[1] // 2
        blk = outs[t].at[src_slot, pl.ds(c * H, H)]
        return pltpu.make_async_remote_copy(
            src_ref=blk, dst_ref=blk, send_sem=send_sems.at[off + 6 * t + j], recv_sem=recv_sems.at[off + 6 * t + j],
            device_id=(chips[j][0], chips[j][1], c), device_id_type=MESH)

    def d2d(outs, t, j, send_sems, recv_sems, half, off):
        x, y, c, s, sib, chips, slots = parts(outs)
        H = outs[t].shape[1] // 2
        blk = outs[t].at[slots[j], pl.ds(half * H, H)]
        return pltpu.make_async_remote_copy(
            src_ref=blk, dst_ref=blk, send_sem=send_sems.at[off + 6 * t + 3 + j],
            recv_sem=recv_sems.at[off + 6 * t + 3 + j], device_id=sib, device_id_type=MESH)

    def start(reads, outs, send_sems, recv_sems, off=0):
        x, y, c, s, sib, chips, slots = parts(outs)
        for t in range(n):
            for j in range(3):
                ici(outs, t, j, send_sems, recv_sems, s, off).start()

    def finish(reads, outs, send_sems, recv_sems, off=0):
        x, y, c, s, sib, chips, slots = parts(outs)
        for t in range(n):
            for j in range(3):
                ici(outs, t, j, send_sems, recv_sems, slots[j], off).wait_recv()
                d2d(outs, t, j, send_sems, recv_sems, c, off).start()
        for t in range(n):
            for j in range(3):
                d2d(outs, t, j, send_sems, recv_sems, 1 - c, off).wait_recv()
        for t in range(n):
            for j in range(3):
                ici(outs, t, j, send_sems, recv_sems, s, off).wait_send()
                d2d(outs, t, j, send_sems, recv_sems, c, off).wait_send()

    return Comm([], bufs, 6 * n, start, finish)


def rs_p1_comm(tensors):
    n = len(tensors)

    def copy(ins, gots, t, send_sems, recv_sems, off):
        x, y, c = _me()
        H = ins[t].shape[1] // 2
        return pltpu.make_async_remote_copy(
            src_ref=ins[t].at[:, pl.ds((1 - c) * H, H)], dst_ref=gots[t], send_sem=send_sems.at[off + t],
            recv_sem=recv_sems.at[off + t], device_id=(x, y, 1 - c), device_id_type=MESH)

    def start(ins, gots, send_sems, recv_sems, off=0):
        for t in range(n):
            copy(ins, gots, t, send_sems, recv_sems, off).start()

    def finish(ins, gots, send_sems, recv_sems, off=0):
        for t in range(n):
            copy(ins, gots, t, send_sems, recv_sems, off).wait_recv()
        for t in range(n):
            copy(ins, gots, t, send_sems, recv_sems, off).wait_send()

    half = lambda t: _sds((t.shape[0], t.shape[1] // 2) + t.shape[2:], t.dtype)
    return Comm(tensors, [half(t) for t in tensors], n, start, finish)


def rs_p2_comm(partials, landeds):
    n = len(partials)

    def start(ins, outs, send_sems, recv_sems, off=0):
        x, y, c = _me()
        s = 2 * x + y
        chips, slots = _chips(x, y)
        for t in range(n):
            for j, chip in enumerate(chips):
                src = ins[t].at[slots[j]] if ins[t].shape[0] == 4 else ins[t].at[0]
                pltpu.make_async_remote_copy(
                    src_ref=src, dst_ref=outs[t].at[s], send_sem=send_sems.at[off + 3 * t + j],
                    recv_sem=recv_sems.at[off + 3 * t + j], device_id=(chip[0], chip[1], c), device_id_type=MESH).start()

    def finish(ins, outs, send_sems, recv_sems, off=0):
        x, y, c = _me()
        s = 2 * x + y
        chips, slots = _chips(x, y)
        for t in range(n):
            for j, chip in enumerate(chips):
                dst = outs[t].at[slots[j]]
                pltpu.make_async_remote_copy(
                    src_ref=dst, dst_ref=dst, send_sem=send_sems.at[off + 3 * t + j],
                    recv_sem=recv_sems.at[off + 3 * t + j], device_id=(chip[0], chip[1], c), device_id_type=MESH).wait_recv()
        for t in range(n):
            for j, chip in enumerate(chips):
                src = ins[t].at[slots[j]] if ins[t].shape[0] == 4 else ins[t].at[0]
                pltpu.make_async_remote_copy(
                    src_ref=src, dst_ref=outs[t].at[s], send_sem=send_sems.at[off + 3 * t + j],
                    recv_sem=recv_sems.at[off + 3 * t + j], device_id=(chip[0], chip[1], c), device_id_type=MESH).wait_send()

    return Comm(partials, landeds, 3 * n, start, finish)


def rs_p3_comm(greds, plan):
    n = len(plan)

    def copy(outs, t, send_sems, recv_sems, half, off):
        x, y, c = _me()
        oi, li, H = plan[t]
        dst = outs[oi] if li is None else outs[oi].at[li]
        blk = dst.at[pl.ds((c if half == 0 else 1 - c) * H, H)]
        return pltpu.make_async_remote_copy(
            src_ref=blk, dst_ref=blk, send_sem=send_sems.at[off + t], recv_sem=recv_sems.at[off + t],
            device_id=(x, y, 1 - c), device_id_type=MESH)

    def start(reads, outs, send_sems, recv_sems, off=0):
        for t in range(n):
            copy(outs, t, send_sems, recv_sems, 0, off).start()

    def finish(reads, outs, send_sems, recv_sems, off=0):
        for t in range(n):
            copy(outs, t, send_sems, recv_sems, 1, off).wait_recv()
        for t in range(n):
            copy(outs, t, send_sems, recv_sems, 0, off).wait_send()

    return Comm([], greds, n, start, finish)


def _rows_block(rows, cols, nbytes=1 << 20):
    unit = 16 if rows % 16 == 0 else 8
    best = unit
    for bm in range(unit, rows + 1, unit):
        if rows % bm == 0 and bm * cols * 4 <= nbytes:
            best = bm
    return best


def cast_into_slot(name, w, where, layer=None):
    rows, W = w.shape[-2:]
    bm = _rows_block(rows, W, 4 << 20)

    def body(p_ref, w_ref, o_ref):
        o_ref[...] = w_ref[...].astype(BF)

    if layer is None:
        ispec = pl.BlockSpec((bm, W), lambda i, p: (i, 0))
    else:
        ispec = pl.BlockSpec((None, bm, W), lambda i, p: (layer, i, 0))
    return pl.pallas_call(
        body, name=name, out_shape=_sds((4, rows, W), BF), compiler_params=_cparams(),
        grid_spec=pltpu.PrefetchScalarGridSpec(
            num_scalar_prefetch=1, grid=(rows // bm,), in_specs=[ispec],
            out_specs=pl.BlockSpec((None, bm, W), lambda i, p: (p[1], i, 0))),
    )(where, w)


def add_pair(name, g, got, where):
    K, R, W = g.shape
    H = R // 2
    bm = _rows_block(H, W, 4 << 20)
    nh = H // bm

    def body(p_ref, g_ref, got_ref, part_ref, land_ref):
        k = pl.program_id(1)
        v = (g_ref[...].astype(F32) + got_ref[...].astype(F32)).astype(part_ref.dtype)
        part_ref[...] = v
        own = (k == p_ref[1]) if K == 4 else (k == 0)

        @pl.when(own)
        def _():
            land_ref[...] = v

    return pl.pallas_call(
        body, name=name, out_shape=[_sds((K, H, W), g.dtype), _sds((4, H, W), g.dtype)], compiler_params=_cparams(),
        grid_spec=pltpu.PrefetchScalarGridSpec(
            num_scalar_prefetch=1, grid=(nh, K),
            in_specs=[pl.BlockSpec((None, bm, W), lambda i, k, p: (k, p[0] * nh + i, 0)),
                      pl.BlockSpec((None, bm, W), lambda i, k, p: (k, i, 0))],
            out_specs=[pl.BlockSpec((None, bm, W), lambda i, k, p: (k, i, 0)),
                       pl.BlockSpec((None, bm, W), lambda i, k, p: (p[1], i, 0))]),
    )(where, g, got)


def sum_slots_into(name, landed, where, dest, dest_shape, li):
    K, H, W = landed.shape
    bm = _rows_block(H, 2 * W, 4 << 20)
    nh = H // bm

    def body(*refs):
        r, o_ref = refs[1], refs[-1]
        acc = r[0].astype(F32)
        for k in range(1, K):
            acc = acc + r[k].astype(F32)
        o_ref[...] = acc

    if li is None:
        ospec = pl.BlockSpec((bm, W), lambda i, p: (p[0] * nh + i, 0))
    else:
        ospec = pl.BlockSpec((None, bm, W), lambda i, p: (li, p[0] * nh + i, 0))
    in_specs = [pl.BlockSpec((K, bm, W), lambda i, p: (0, i, 0))]
    args = [where, landed]
    aliases = {}
    if dest is not None:
        in_specs.append(ANY)
        args.append(dest)
        aliases = {2: 0}
    return pl.pallas_call(
        body, name=name, out_shape=_sds(dest_shape, F32), compiler_params=_cparams(), input_output_aliases=aliases,
        grid_spec=pltpu.PrefetchScalarGridSpec(num_scalar_prefetch=1, grid=(nh,), in_specs=in_specs, out_specs=ospec),
    )(*args)


def sum_slots(name, a):
    K, H, W = a.shape
    bm = _rows_block(H, W * K // 2)

    def fn(ids, r):
        acc = r[0]
        for k in range(1, K):
            acc = acc + r[k]
        return [acc]

    return ew_call(name, (H // bm,), fn, [(a, pl.BlockSpec((K, bm, W), lambda i: (0, i, 0)))],
                   [(_sds((H, W), F32), pl.BlockSpec((bm, W), lambda i: (i, 0)), False)])[0]


def _adamw_math(w, g, m, v):
    bc1 = 1.0 - ADAM_B1 ** ADAM_STEP
    bc2 = 1.0 - ADAM_B2 ** ADAM_STEP
    mn = ADAM_B1 * m + (1.0 - ADAM_B1) * g
    vn = ADAM_B2 * v + (1.0 - ADAM_B2) * (g * g)
    m_hat = mn / bc1
    v_hat = vn / bc2
    delta = -ADAM_LR * (m_hat / (jnp.sqrt(v_hat) + ADAM_EPS) + ADAM_WD * w)
    return delta, mn, vn


def adamw(name, w, g, m, v, copy_grad=False):
    shape = w.shape
    flat = lambda t: t.reshape(-1, shape[-1])
    w2, g2, m2, v2 = flat(w), flat(g), flat(m), flat(v)
    bm = _rows_block(w2.shape[0], w2.shape[1])

    def fn(ids, w_ref, g_ref, m_ref, v_ref):
        gv = g_ref[...]
        return list(_adamw_math(w_ref[...], gv, m_ref[...], v_ref[...])) + ([gv] if copy_grad else [])

    spec = pl.BlockSpec((bm, w2.shape[1]), lambda i: (i, 0))
    res = ew_call(name, (w2.shape[0] // bm,), fn, [(w2, spec), (g2, spec), (m2, spec), (v2, spec)],
                  [(_sds(w2.shape, F32), spec, False)] * (4 if copy_grad else 3))
    return [o.reshape(shape) for o in res]


def adamw_many(name, params):
    n = len(params)
    shapes = [p_[0].shape for p_ in params]
    two_d = lambda t: t.reshape(-1, t.shape[-1])
    args = [two_d(t) for p_ in params for t in p_]

    def body(*refs):
        ins, outs = refs[:4 * n], refs[4 * n:]
        for q in range(n):
            w_ref, g_ref, m_ref, v_ref = ins[4 * q:4 * q + 4]
            for o_ref, val in zip(outs[3 * q:3 * q + 3], _adamw_math(w_ref[...], g_ref[...], m_ref[...], v_ref[...])):
                o_ref[...] = val

    full = lambda a: pl.BlockSpec(a.shape, lambda i: (0, 0))
    out_shape = [_sds(args[4 * q].shape, F32) for q in range(n) for _ in range(3)]
    res = hosted_call(body, name=name, grid=(1,), in_specs=[full(a) for a in args],
                      out_specs=[full(o) for o in out_shape], out_shape=out_shape, args=args)
    return [tuple(res[3 * q + r].reshape(shapes[q]) for r in range(3)) for q in range(n)]


def dmod_pack(gd):
    N = gd.shape[-1]
    bn = _pick(N, [4608, 2304, 1152, 1024, 512, 256, 128])

    def fn(ids, r):
        lat = [r[d, 1:2, :] for d in range(8)]
        cs = r[0, 0:1, :]
        for d in range(1, 8):
            cs = cs + r[d, 0:1, :]
        tot = cs
        for d in range(8):
            tot = tot + lat[d]
        return [jnp.concatenate(lat + [cs, jnp.zeros((7, bn), F32)], axis=0), tot]

    return ew_call("dmod_pack", (N // bn,), fn, [(gd, pl.BlockSpec((8, 2, bn), lambda j: (0, 0, j)))],
                   [(_sds((16, N), F32), pl.BlockSpec((16, bn), lambda j: (0, j)), False),
                    (_sds((1, N), F32), pl.BlockSpec((1, bn), lambda j: (0, j)), False)])


def _silu(v):
    return v * _sig(v)


def kernel(x, c, ctx, c_ctx, w_mod, b_mod, norm_g, ffn_wg, ffn_wu, ffn_wd, w_in, w_out, q_norm_g, k_norm_g, conv_w, conv_b, lru_wa, lru_ba, lru_wx, lru_bx, lru_lambda, final_norm_g, loss_target, m_c_ctx, m_w_mod, m_b_mod, m_norm_g, m_ffn_wg, m_ffn_wu, m_ffn_wd, m_w_in, m_w_out, m_q_norm_g, m_k_norm_g, m_conv_w, m_conv_b, m_lru_wa, m_lru_ba, m_lru_wx, m_lru_bx, m_lru_lambda, m_final_norm_g, v_c_ctx, v_w_mod, v_b_mod, v_norm_g, v_ffn_wg, v_ffn_wu, v_ffn_wd, v_w_in, v_w_out, v_q_norm_g, v_k_norm_g, v_conv_w, v_conv_b, v_lru_wa, v_lru_ba, v_lru_wx, v_lru_bx, v_lru_lambda, v_final_norm_g):
    given = dict(locals())
    names = ['c_ctx', 'w_mod', 'b_mod', 'norm_g', 'ffn_wg', 'ffn_wu', 'ffn_wd', 'w_in', 'w_out', 'q_norm_g', 'k_norm_g',
             'conv_w', 'conv_b', 'lru_wa', 'lru_ba', 'lru_wx', 'lru_bx', 'lru_lambda', 'final_norm_g']
    S, D = x.shape[1], x.shape[2]
    C = ctx.shape[1]
    NS = 4
    F4, W4, LB = ffn_wg.shape[-1], w_in.shape[-1], lru_wa.shape[2]
    dm = Dims(S, C, D, F4, W4, NS, LB)
    Ds = D // NS
    Wm = w_mod.shape[-1]
    xi, yi, ci = lax.axis_index("x"), lax.axis_index("y"), lax.axis_index("c")
    slot = 2 * xi + yi
    me = 4 * xi + 2 * yi + ci
    ident = lambda ids, accs, ex: list(accs)

    pack1 = jnp.concatenate([c.reshape(-1), norm_g.reshape(-1), conv_w.reshape(-1), lru_ba.reshape(-1),
                             lru_bx.reshape(-1), lru_lambda.reshape(-1)]).reshape(1, -1)
    g1 = allgather8("ag_small_params", pack1)[:, 0]
    c_all = g1[:, :D]

    def unshard(off, k):
        part = g1[0::2, off:off + k * Ds].reshape(NS, k, Ds)
        return jnp.transpose(part, (1, 0, 2)).reshape(k, D)

    norm_g_f = unshard(D, 3)
    conv_w_f = unshard(D + 3 * Ds, 4)
    ba_f = unshard(D + 7 * Ds, 2)
    bx_f = unshard(D + 9 * Ds, 2)
    lam_f = unshard(D + 11 * Ds, 2)

    call16 = jnp.concatenate([c_all, c_ctx.reshape(1, D), jnp.zeros((7, D), F32)], axis=0)
    b_cols = lax.dynamic_slice(b_mod, (0, slot * Wm), (1, Wm))
    bnm = _pick(Wm, [1536, 1152, 768, 512, 384, 256, 128])
    bkm = _pick(D, [512, 256, 128])
    modp = fused_mm(
        "mod_fwd", (1, Wm // bnm, D // bkm),
        [(call16, pl.BlockSpec((16, bkm), lambda i, j, k: (0, k))),
         (w_mod[0], pl.BlockSpec((bkm, bnm), lambda i, j, k: (k, j)))],
        [(0, 1, NN, 0)], [(16, bnm)], lambda ids, accs, ex: [accs[0] + ex[0][...]],
        [(_sds((16, Wm), F32), pl.BlockSpec((16, bnm), lambda i, j, k: (0, j)))],
        extras=[(b_cols, pl.BlockSpec((1, bnm), lambda i, j, k: (0, j)))], pre={0: _silu})[0]
    gm = allgather8("ag_mod", modp)
    mod_full = jnp.concatenate([gm[0], gm[2], gm[4], gm[6]], axis=1)
    mod_x = lax.dynamic_index_in_dim(mod_full, me, axis=0, keepdims=False)
    modv = jnp.stack([mod_full[8], mod_x]).reshape(2, N_MOD, 1, D)

    where = jnp.stack([ci, slot]).astype(jnp.int32)
    wbuf = {}
    for key, short in (('ffn_wg', 'wg'), ('ffn_wu', 'wu'), ('ffn_wd', 'wd')):
        for l in range(2):
            wbuf[short + str(l)] = cast_into_slot("cast_%s%d" % (short, l), given[key][0], where, l)
    wbuf['w_in'] = cast_into_slot("cast_w_in", w_in[0], where)
    wbuf['w_out'] = cast_into_slot("cast_w_out", w_out[0], where)

    loss_l, grad_x, dmod, small, reduced, late = local_step(
        dm, x[0], ctx[0], loss_target[0], modv, norm_g_f.reshape(3, 1, D), final_norm_g.reshape(1, D),
        q_norm_g, k_norm_g, conv_w_f, conv_b, lru_wa[0], ba_f.reshape(2, 1, D), lru_wx[0], bx_f.reshape(2, 1, D),
        lam_f.reshape(2, 1, D), wbuf, where)
    loss = lax.psum(loss_l[0, 0], ("x", "y", "c"))

    grads = {}
    gd = allgather8("ag_dmod", dmod.reshape(2, N_MOD * D))
    dM, g_bmod = dmod_pack(gd)
    dMc = lax.dynamic_slice(dM, (0, slot * Wm), (16, Wm))
    bmm = _pick(D, [512, 256, 128])
    grads['w_mod'] = fused_mm(
        "w_mod_dw", (D // bmm, Wm // bnm, 1),
        [(call16, pl.BlockSpec((16, bmm), lambda i, j, k: (0, i))), (dMc, pl.BlockSpec((16, bnm), lambda i, j, k: (0, j)))],
        [(0, 1, TN, 0)], [(bmm, bnm)], ident,
        [(_sds((D, Wm), F32), pl.BlockSpec((bmm, bnm), lambda i, j, k: (i, j)))], pre={0: _silu})[0][None]
    grads['b_mod'] = g_bmod

    def epi_cc(ids, accs, ex):
        v = ex[0][...]
        sg = _sig(v)
        return [accs[0] * (sg * (1.0 + v * (1.0 - sg)))]

    pcc = fused_mm(
        "c_ctx_partial", (1, D // bmm, Wm // bnm),
        [(dMc, pl.BlockSpec((16, bnm), lambda i, j, k: (0, k))), (w_mod[0], pl.BlockSpec((bmm, bnm), lambda i, j, k: (j, k)))],
        [(0, 1, NT, 0)], [(16, bmm)], epi_cc,
        [(_sds((16, D), F32), pl.BlockSpec((16, bmm), lambda i, j, k: (0, j)))],
        extras=[(c_ctx.reshape(1, D), pl.BlockSpec((1, bmm), lambda i, j, k: (0, j)))])[0]
    pcc_row = jnp.where(ci == 0, pcc[8], 0.0)

    order = ['q_norm_g', 'k_norm_g', 'conv_b', 'final_norm_g', 'norm_g', 'conv_w', 'lru_ba', 'lru_bx', 'lru_lambda']
    flat = [small[k].reshape(-1) for k in order] + [pcc_row]
    sizes = [f.shape[0] for f in flat]
    tot = sum(sizes)
    LW = 1024
    padded = -(-tot // (8 * LW)) * (8 * LW)
    tiny = jnp.concatenate(flat + [jnp.zeros((padded - tot,), F32)]).reshape(-1, LW)
    summed = sum_slots("tiny_sum", allgather8("ag_tiny_grads", tiny)).reshape(-1)
    offs = {}
    o = 0
    for k, n_ in zip(order + ['c_ctx'], sizes):
        offs[k] = summed[o:o + n_]
        o += n_
    shard = lambda k, rows: lax.dynamic_slice_in_dim(offs[k].reshape(rows, D), slot * Ds, Ds, axis=1)
    grads['c_ctx'] = offs['c_ctx']
    grads['q_norm_g'] = offs['q_norm_g'].reshape(1, HEAD_DIM)
    grads['k_norm_g'] = offs['k_norm_g'].reshape(1, HEAD_DIM)
    grads['conv_b'] = offs['conv_b'].reshape(1, D)
    grads['final_norm_g'] = offs['final_norm_g']
    grads['norm_g'] = shard('norm_g', 3)[None]
    grads['conv_w'] = shard('conv_w', 4)[None]
    grads['lru_ba'] = shard('lru_ba', 2)[None]
    grads['lru_bx'] = shard('lru_bx', 2)[None]
    grads['lru_lambda'] = shard('lru_lambda', 2)[None]

    landed = comm_call("rs_tail_p2", rs_p2_comm([late[0]], [late[1]]))
    g_wd = sum_slots_into("rs_sum_wd0", landed[0], where, reduced['ffn_wd'], (2, F4, D), 0)
    g_wg, g_wu, g_wd = comm_call("rs_tail_p3", rs_p3_comm(
        [reduced['ffn_wg'], reduced['ffn_wu'], g_wd], [(0, 0, D // 2), (1, 0, D // 2), (2, 0, F4 // 2)]))
    grads.update(ffn_wg=g_wg[None], ffn_wu=g_wu[None], ffn_wd=g_wd[None], w_in=reduced['w_in'][None],
                 w_out=reduced['w_out'][None], lru_wa=reduced['lru_wa'].reshape(lru_wa.shape),
                 lru_wx=reduced['lru_wx'].reshape(lru_wx.shape))

    delta, new_m, new_v = {}, {}, {}
    big_names = ['w_mod', 'w_in', 'w_out', 'lru_wa', 'lru_wx', 'ffn_wg', 'ffn_wu', 'ffn_wd']
    for k in big_names:
        res = adamw("adamw_" + k, given[k], grads[k], given['m_' + k], given['v_' + k], copy_grad=(k != 'w_mod'))
        delta[k], new_m[k], new_v[k] = res[:3]
        if k != 'w_mod':
            grads[k] = res[3]
    tiny_names = [k for k in names if k not in big_names]
    res = adamw_many("adamw_tiny", [(given[k], grads[k], given['m_' + k], given['v_' + k]) for k in tiny_names])
    for k, (d_, m_, v_) in zip(tiny_names, res):
        delta[k], new_m[k], new_v[k] = d_, m_, v_

    return (loss, grad_x[None], *[grads[k] for k in names], *[delta[k] for k in names],
            *[new_m[k] for k in names], *[new_v[k] for k in names])
```
